```python
import jax, jax.numpy as jnp
from jax import lax
import numpy as np

D_MODEL = 1024
BATCH = 32
SEQ = 2048
DEPTH = 4

MEM_LEN = 256
MEM_HEADS = 4
MEM_HEAD_DIM = D_MODEL // 8
RET_HEADS = 4
RET_QK_DIM = D_MODEL // 8
RET_V_DIM = D_MODEL // 4
RET_CHUNK = 128
ROPE_BASE = 10000.0
POOL_WINDOWS = (2, 4, 8, 16)
POOL_GROUP = D_MODEL // 8
POOL_WIDTH = POOL_GROUP * len(POOL_WINDOWS)
FFN_HIDDEN = 4 * D_MODEL
N_BRANCHES = 3
EPS = 1e-6

RET_QK_W = RET_HEADS * RET_QK_DIM
RET_V_W = RET_HEADS * RET_V_DIM
MEM_Q_W = MEM_HEADS * MEM_HEAD_DIM
COL_SIZES = (RET_QK_W, RET_QK_W, RET_V_W, RET_V_W, POOL_WIDTH, MEM_Q_W, N_BRANCHES * D_MODEL)
COL_SPLITS = tuple(int(s) for s in np.cumsum(COL_SIZES)[:-1])
IN_PROJ_W = int(sum(COL_SIZES))

kernel_name = "hybrid_retention_pool_memory_encoder"


def rms_norm(x, g):
    xf = x.astype(jnp.float32)
    y = xf * lax.rsqrt(jnp.mean(jnp.square(xf), axis=-1, keepdims=True) + EPS)
    return (y * g.astype(jnp.float32)).astype(x.dtype)


def rotary(x, pos):
    d = x.shape[-1]
    inv = ROPE_BASE ** (-jnp.arange(0, d, 2, dtype=jnp.float32) / d)
    ang = pos.astype(jnp.float32)[:, None] * inv[None, :]
    cos = jnp.cos(ang)[None, :, None, :].astype(x.dtype)
    sin = jnp.sin(ang)[None, :, None, :].astype(x.dtype)
    x1, x2 = jnp.split(x, 2, axis=-1)
    return jnp.concatenate([x1 * cos - x2 * sin, x1 * sin + x2 * cos], axis=-1)


def retention_one_direction(q, k, v, log_g, strict):
    B, S, H, dk = q.shape
    dv = v.shape[-1]
    C = RET_CHUNK
    N = S // C
    q = q.reshape(B, N, C, H, dk)
    k = k.reshape(B, N, C, H, dk)
    v = v.reshape(B, N, C, H, dv)
    idx = jnp.arange(C, dtype=jnp.float32)
    diff = idx[:, None] - idx[None, :]
    mask = (diff > 0) if strict else (diff >= 0)
    dmat = jnp.where(mask[None], jnp.exp(log_g[:, None, None] * jnp.where(mask, diff, 0.0)[None]), 0.0)
    s = jnp.einsum('bnihd,bnjhd->bnhij', q, k) * dmat
    o_intra = jnp.einsum('bnhij,bnjhe->bnihe', s, v)
    zeta = jnp.exp(log_g[None, :] * (C - 1 - idx)[:, None])
    xi = jnp.exp(log_g[None, :] * (idx + 1)[:, None])
    chunk_decay = jnp.exp(log_g * C)
    qx = q * xi[:, :, None]
    kz = k * zeta[:, :, None]

    def step(state, inp):
        qc, kc, vc = inp
        out = jnp.einsum('bihd,bhde->bihe', qc, state)
        state = state * chunk_decay[None, :, None, None] + jnp.einsum('bjhd,bjhe->bhde', kc, vc)
        return state, out

    state0 = jnp.zeros((B, H, dk, dv), jnp.float32)
    _, o_cross = lax.scan(step, state0, (jnp.moveaxis(qx, 1, 0), jnp.moveaxis(kz, 1, 0), jnp.moveaxis(v, 1, 0)))
    o = o_intra.astype(jnp.float32) + jnp.moveaxis(o_cross, 0, 1)
    return o.reshape(B, S, H, dv)


def retention_branch(q, k, v, g, decay_logit, w_o, pos):
    B, S, _ = q.shape
    q = rotary(q.reshape(B, S, RET_HEADS, RET_QK_DIM), pos)
    k = rotary(k.reshape(B, S, RET_HEADS, RET_QK_DIM), pos) * (RET_QK_DIM ** -0.5)
    v = v.reshape(B, S, RET_HEADS, RET_V_DIM)
    log_g = jax.nn.log_sigmoid(decay_logit.astype(jnp.float32))
    o_fwd = retention_one_direction(q, k, v, log_g[0], strict=False)
    o_bwd = jnp.flip(retention_one_direction(jnp.flip(q, 1), jnp.flip(k, 1), jnp.flip(v, 1), log_g[1], strict=True), 1)
    o = o_fwd + o_bwd
    mu = jnp.mean(o, axis=-1, keepdims=True)
    var = jnp.mean(jnp.square(o - mu), axis=-1, keepdims=True)
    o = ((o - mu) * lax.rsqrt(var + EPS)).reshape(B, S, RET_V_W).astype(g.dtype)
    return (o * jax.nn.silu(g)) @ w_o


def pool_branch(p, w_grp, scale, w_o):
    B, S, _ = p.shape
    pf = p.astype(jnp.float32)
    cs = jnp.concatenate([jnp.zeros((B, 1, POOL_WIDTH), jnp.float32), jnp.cumsum(pf, axis=1)], axis=1)
    n = jnp.arange(S)
    groups = []
    for gi, w in enumerate(POOL_WINDOWS):
        lo = jnp.clip(n - w // 2, 0, S)
        hi = jnp.clip(n + w // 2, 0, S)
        sl = slice(gi * POOL_GROUP, (gi + 1) * POOL_GROUP)
        csg = cs[..., sl]
        mean = (csg[:, hi] - csg[:, lo]) / (hi - lo).astype(jnp.float32)[None, :, None]
        groups.append(mean - pf[..., sl])
    mixed = jnp.stack(groups, axis=2)
    y = jnp.einsum('bsgc,gcd->bsgd', mixed, w_grp.astype(jnp.float32)).reshape(B, S, POOL_WIDTH)
    y = (y * scale.astype(jnp.float32)).astype(p.dtype)
    return y @ w_o


def memory_branch(q, mem_n, w_kv, w_o):
    B, S, _ = q.shape
    M = mem_n.shape[1]
    q = q.reshape(B, S, MEM_HEADS, MEM_HEAD_DIM)
    kv = mem_n @ w_kv
    k, v = jnp.split(kv, 2, axis=-1)
    k = k.reshape(B, M, MEM_HEADS, MEM_HEAD_DIM)
    v = v.reshape(B, M, MEM_HEADS, MEM_HEAD_DIM)
    s = jnp.einsum('bshd,bmhd->bhsm', q, k).astype(jnp.float32) * (MEM_HEAD_DIM ** -0.5)
    a = jax.nn.softmax(s, axis=-1).astype(v.dtype)
    o = jnp.einsum('bhsm,bmhd->bshd', a, v).reshape(B, S, MEM_Q_W)
    return o @ w_o


def _fwd_setup_inputs(seed: int = 0) -> dict:
    key = jax.random.key(seed)
    ks = jax.random.split(key, 20)
    f32 = jnp.float32

    def dense(k, shape, fan_in):
        return jax.random.normal(k, shape, f32) * (fan_in ** -0.5)

    base = 1.0 - 2.0 ** (-5.0 - jnp.arange(RET_HEADS, dtype=f32))
    base_logit = jnp.log(base) - jnp.log1p(-base)
    decay_logit = base_logit[None, None, :] + 0.05 * jax.random.normal(ks[3], (DEPTH, 2, RET_HEADS), f32)
    return {
        "x": jax.random.normal(ks[0], (BATCH, SEQ, D_MODEL), f32),
        "mem": jax.random.normal(ks[1], (BATCH, MEM_LEN, D_MODEL), f32),
        "w_in": dense(ks[2], (DEPTH, D_MODEL, IN_PROJ_W), D_MODEL),
        "ret_decay_logit": decay_logit,
        "w_ret_o": dense(ks[4], (DEPTH, RET_V_W, D_MODEL), RET_V_W),
        "w_pool_grp": dense(ks[5], (DEPTH, len(POOL_WINDOWS), POOL_GROUP, POOL_GROUP), POOL_GROUP),
        "pool_scale": 1.0 + 0.1 * jax.random.normal(ks[6], (DEPTH, POOL_WIDTH), f32),
        "w_pool_o": dense(ks[7], (DEPTH, POOL_WIDTH, D_MODEL), POOL_WIDTH),
        "w_mem_kv": dense(ks[8], (DEPTH, D_MODEL, 2 * MEM_Q_W), D_MODEL),
        "w_mem_o": dense(ks[9], (DEPTH, MEM_Q_W, D_MODEL), MEM_Q_W),
        "w_out": dense(ks[10], (DEPTH, D_MODEL, D_MODEL), D_MODEL),
        "w_ff1": dense(ks[11], (DEPTH, D_MODEL, FFN_HIDDEN), D_MODEL),
        "w_ff2": dense(ks[12], (DEPTH, FFN_HIDDEN, D_MODEL), FFN_HIDDEN),
        "norm1_g": 1.0 + 0.05 * jax.random.normal(ks[13], (DEPTH, D_MODEL), f32),
        "norm2_g": 1.0 + 0.05 * jax.random.normal(ks[14], (DEPTH, D_MODEL), f32),
        "mem_norm_g": 1.0 + 0.05 * jax.random.normal(ks[15], (D_MODEL,), f32),
        "final_norm_g": 1.0 + 0.05 * jax.random.normal(ks[16], (D_MODEL,), f32),
    }


def _fwd_reference(x, mem, w_in, ret_decay_logit, w_ret_o, w_pool_grp, pool_scale, w_pool_o,
              w_mem_kv, w_mem_o, w_out, w_ff1, w_ff2, norm1_g, norm2_g, mem_norm_g, final_norm_g):
    S = x.shape[1]
    pos = jnp.arange(S)
    mem_n = rms_norm(mem, mem_norm_g)
    for l in range(DEPTH):
        h = rms_norm(x, norm1_g[l])
        proj = h @ w_in[l]
        q_r, k_r, v_r, g_r, p_in, q_m, gates = jnp.split(proj, COL_SPLITS, axis=-1)
        o_ret = retention_branch(q_r, k_r, v_r, g_r, ret_decay_logit[l], w_ret_o[l], pos)
        o_pool = pool_branch(p_in, w_pool_grp[l], pool_scale[l], w_pool_o[l])
        o_mem = memory_branch(q_m, mem_n, w_mem_kv[l], w_mem_o[l])
        gate_r, gate_p, gate_m = jnp.split(jax.nn.sigmoid(gates), N_BRANCHES, axis=-1)
        merged = (gate_r * o_ret + gate_p * o_pool + gate_m * o_mem).astype(x.dtype)
        x = x + merged @ w_out[l]
        h = rms_norm(x, norm2_g[l])
        x = x + jnp.square(jax.nn.relu(h @ w_ff1[l])) @ w_ff2[l]
    return rms_norm(x, final_norm_g)


import jax as _jax
import jax.numpy as _jnp

TWIN_FORMAT = 'train_step'
FWD_PARAMS = ['x', 'mem', 'w_in', 'ret_decay_logit', 'w_ret_o', 'w_pool_grp', 'pool_scale', 'w_pool_o', 'w_mem_kv', 'w_mem_o', 'w_out', 'w_ff1', 'w_ff2', 'norm1_g', 'norm2_g', 'mem_norm_g', 'final_norm_g']
TWIN_WEIGHTS = ['w_in', 'ret_decay_logit', 'w_ret_o', 'w_pool_grp', 'pool_scale', 'w_pool_o', 'w_mem_kv', 'w_mem_o', 'w_out', 'w_ff1', 'w_ff2', 'norm1_g', 'norm2_g', 'mem_norm_g', 'final_norm_g']
TWIN_DIFF_INPUT = 'x'
TWIN_INPUTS = ['x', 'mem', 'w_in', 'ret_decay_logit', 'w_ret_o', 'w_pool_grp', 'pool_scale', 'w_pool_o', 'w_mem_kv', 'w_mem_o', 'w_out', 'w_ff1', 'w_ff2', 'norm1_g', 'norm2_g', 'mem_norm_g', 'final_norm_g', 'loss_target', 'm_w_in', 'm_ret_decay_logit', 'm_w_ret_o', 'm_w_pool_grp', 'm_pool_scale', 'm_w_pool_o', 'm_w_mem_kv', 'm_w_mem_o', 'm_w_out', 'm_w_ff1', 'm_w_ff2', 'm_norm1_g', 'm_norm2_g', 'm_mem_norm_g', 'm_final_norm_g', 'v_w_in', 'v_ret_decay_logit', 'v_w_ret_o', 'v_w_pool_grp', 'v_pool_scale', 'v_w_pool_o', 'v_w_mem_kv', 'v_w_mem_o', 'v_w_out', 'v_w_ff1', 'v_w_ff2', 'v_norm1_g', 'v_norm2_g', 'v_mem_norm_g', 'v_final_norm_g']
TWIN_OUTPUTS = ['loss', 'grad_x', 'grad_w_in', 'grad_ret_decay_logit', 'grad_w_ret_o', 'grad_w_pool_grp', 'grad_pool_scale', 'grad_w_pool_o', 'grad_w_mem_kv', 'grad_w_mem_o', 'grad_w_out', 'grad_w_ff1', 'grad_w_ff2', 'grad_norm1_g', 'grad_norm2_g', 'grad_mem_norm_g', 'grad_final_norm_g', 'delta_w_in', 'delta_ret_decay_logit', 'delta_w_ret_o', 'delta_w_pool_grp', 'delta_pool_scale', 'delta_w_pool_o', 'delta_w_mem_kv', 'delta_w_mem_o', 'delta_w_out', 'delta_w_ff1', 'delta_w_ff2', 'delta_norm1_g', 'delta_norm2_g', 'delta_mem_norm_g', 'delta_final_norm_g', 'new_m_w_in', 'new_m_ret_decay_logit', 'new_m_w_ret_o', 'new_m_w_pool_grp', 'new_m_pool_scale', 'new_m_w_pool_o', 'new_m_w_mem_kv', 'new_m_w_mem_o', 'new_m_w_out', 'new_m_w_ff1', 'new_m_w_ff2', 'new_m_norm1_g', 'new_m_norm2_g', 'new_m_mem_norm_g', 'new_m_final_norm_g', 'new_v_w_in', 'new_v_ret_decay_logit', 'new_v_w_ret_o', 'new_v_w_pool_grp', 'new_v_pool_scale', 'new_v_w_pool_o', 'new_v_w_mem_kv', 'new_v_w_mem_o', 'new_v_w_out', 'new_v_w_ff1', 'new_v_w_ff2', 'new_v_norm1_g', 'new_v_norm2_g', 'new_v_mem_norm_g', 'new_v_final_norm_g']
TWIN_LEAF_KINDS = {'loss': 'loss', 'grad_x': 'grad_x', 'grad_w_in': 'grad_w', 'grad_ret_decay_logit': 'grad_w', 'grad_w_ret_o': 'grad_w', 'grad_w_pool_grp': 'grad_w', 'grad_pool_scale': 'grad_w', 'grad_w_pool_o': 'grad_w', 'grad_w_mem_kv': 'grad_w', 'grad_w_mem_o': 'grad_w', 'grad_w_out': 'grad_w', 'grad_w_ff1': 'grad_w', 'grad_w_ff2': 'grad_w', 'grad_norm1_g': 'grad_w', 'grad_norm2_g': 'grad_w', 'grad_mem_norm_g': 'grad_w', 'grad_final_norm_g': 'grad_w', 'delta_w_in': 'delta_w', 'delta_ret_decay_logit': 'delta_w', 'delta_w_ret_o': 'delta_w', 'delta_w_pool_grp': 'delta_w', 'delta_pool_scale': 'delta_w', 'delta_w_pool_o': 'delta_w', 'delta_w_mem_kv': 'delta_w', 'delta_w_mem_o': 'delta_w', 'delta_w_out': 'delta_w', 'delta_w_ff1': 'delta_w', 'delta_w_ff2': 'delta_w', 'delta_norm1_g': 'delta_w', 'delta_norm2_g': 'delta_w', 'delta_mem_norm_g': 'delta_w', 'delta_final_norm_g': 'delta_w', 'new_m_w_in': 'new_m', 'new_m_ret_decay_logit': 'new_m', 'new_m_w_ret_o': 'new_m', 'new_m_w_pool_grp': 'new_m', 'new_m_pool_scale': 'new_m', 'new_m_w_pool_o': 'new_m', 'new_m_w_mem_kv': 'new_m', 'new_m_w_mem_o': 'new_m', 'new_m_w_out': 'new_m', 'new_m_w_ff1': 'new_m', 'new_m_w_ff2': 'new_m', 'new_m_norm1_g': 'new_m', 'new_m_norm2_g': 'new_m', 'new_m_mem_norm_g': 'new_m', 'new_m_final_norm_g': 'new_m', 'new_v_w_in': 'new_v', 'new_v_ret_decay_logit': 'new_v', 'new_v_w_ret_o': 'new_v', 'new_v_w_pool_grp': 'new_v', 'new_v_pool_scale': 'new_v', 'new_v_w_pool_o': 'new_v', 'new_v_w_mem_kv': 'new_v', 'new_v_w_mem_o': 'new_v', 'new_v_w_out': 'new_v', 'new_v_w_ff1': 'new_v', 'new_v_w_ff2': 'new_v', 'new_v_norm1_g': 'new_v', 'new_v_norm2_g': 'new_v', 'new_v_mem_norm_g': 'new_v', 'new_v_final_norm_g': 'new_v'}


def _forward(args):
    return _fwd_reference(*[args[k] for k in FWD_PARAMS])


def _output_shape():
    out = _jax.eval_shape(lambda: _forward(_fwd_setup_inputs(0)))
    return out.shape, out.dtype

N_MICROBATCH = 1
ADAM_LR = 0.001
ADAM_B1 = 0.9
ADAM_B2 = 0.999
ADAM_EPS = 1e-08
ADAM_WD = 0.01
ADAM_STEP = 10
PER_EXAMPLE_BATCH_AXIS = {'x': 0, 'mem': 0, 'loss_target': 0}
SHARED_INPUTS = []
_WEIGHT_DTYPES = {'w_in': _jnp.float32, 'ret_decay_logit': _jnp.float32, 'w_ret_o': _jnp.float32, 'w_pool_grp': _jnp.float32, 'pool_scale': _jnp.float32, 'w_pool_o': _jnp.float32, 'w_mem_kv': _jnp.float32, 'w_mem_o': _jnp.float32, 'w_out': _jnp.float32, 'w_ff1': _jnp.float32, 'w_ff2': _jnp.float32, 'norm1_g': _jnp.float32, 'norm2_g': _jnp.float32, 'mem_norm_g': _jnp.float32, 'final_norm_g': _jnp.float32}
MOMENT_SCALE = {'w_in': 6.093079e-02, 'ret_decay_logit': 6.633769e-01, 'w_ret_o': 6.155241e-02, 'w_pool_grp': 1.285329e-01, 'pool_scale': 1.314376e-01, 'w_pool_o': 9.000998e-02, 'w_mem_kv': 1.722509e-02, 'w_mem_o': 1.318369e-02, 'w_out': 1.093927e-01, 'w_ff1': 1.060434e-01, 'w_ff2': 2.996316e-01, 'norm1_g': 1.710288e-01, 'norm2_g': 2.126499e-01, 'mem_norm_g': 4.250783e-02, 'final_norm_g': 6.565170e+01}


def _to_microbatches(a, axis):
    t = _jnp.moveaxis(a, axis, 0)
    t = t.reshape((N_MICROBATCH, t.shape[0] // N_MICROBATCH) + t.shape[1:])
    return _jnp.moveaxis(t, 1, axis + 1)


def setup_inputs(seed: int = 0) -> dict:
    inp = _fwd_setup_inputs(seed)
    key = _jax.random.fold_in(_jax.random.key(seed), 7919)
    shape, _ = _output_shape()
    out = dict(inp)
    out["loss_target"] = _jax.random.normal(_jax.random.fold_in(key, 0), shape, _jnp.float32)
    for i, name in enumerate(TWIN_WEIGHTS):
        w = inp[name].astype(_jnp.float32)
        if MOMENT_SCALE is None:
            s = _jnp.sqrt(_jnp.mean(_jnp.square(w)) + 1e-30)
        else:
            s = MOMENT_SCALE[name]
        km, kv = _jax.random.split(_jax.random.fold_in(key, i + 1))
        out[name] = w
        out["m_" + name] = s * _jax.random.normal(km, w.shape, _jnp.float32)
        out["v_" + name] = (s * s) * _jax.random.uniform(kv, w.shape, _jnp.float32, 0.5, 1.5)
    if N_MICROBATCH > 1:
        for name, axis in PER_EXAMPLE_BATCH_AXIS.items():
            out[name] = _to_microbatches(out[name], axis)
    return {'x': out['x'], 'mem': out['mem'], 'w_in': out['w_in'], 'ret_decay_logit': out['ret_decay_logit'], 'w_ret_o': out['w_ret_o'], 'w_pool_grp': out['w_pool_grp'], 'pool_scale': out['pool_scale'], 'w_pool_o': out['w_pool_o'], 'w_mem_kv': out['w_mem_kv'], 'w_mem_o': out['w_mem_o'], 'w_out': out['w_out'], 'w_ff1': out['w_ff1'], 'w_ff2': out['w_ff2'], 'norm1_g': out['norm1_g'], 'norm2_g': out['norm2_g'], 'mem_norm_g': out['mem_norm_g'], 'final_norm_g': out['final_norm_g'], 'loss_target': out['loss_target'], 'm_w_in': out['m_w_in'], 'm_ret_decay_logit': out['m_ret_decay_logit'], 'm_w_ret_o': out['m_w_ret_o'], 'm_w_pool_grp': out['m_w_pool_grp'], 'm_pool_scale': out['m_pool_scale'], 'm_w_pool_o': out['m_w_pool_o'], 'm_w_mem_kv': out['m_w_mem_kv'], 'm_w_mem_o': out['m_w_mem_o'], 'm_w_out': out['m_w_out'], 'm_w_ff1': out['m_w_ff1'], 'm_w_ff2': out['m_w_ff2'], 'm_norm1_g': out['m_norm1_g'], 'm_norm2_g': out['m_norm2_g'], 'm_mem_norm_g': out['m_mem_norm_g'], 'm_final_norm_g': out['m_final_norm_g'], 'v_w_in': out['v_w_in'], 'v_ret_decay_logit': out['v_ret_decay_logit'], 'v_w_ret_o': out['v_w_ret_o'], 'v_w_pool_grp': out['v_w_pool_grp'], 'v_pool_scale': out['v_pool_scale'], 'v_w_pool_o': out['v_w_pool_o'], 'v_w_mem_kv': out['v_w_mem_kv'], 'v_w_mem_o': out['v_w_mem_o'], 'v_w_out': out['v_w_out'], 'v_w_ff1': out['v_w_ff1'], 'v_w_ff2': out['v_w_ff2'], 'v_norm1_g': out['v_norm1_g'], 'v_norm2_g': out['v_norm2_g'], 'v_mem_norm_g': out['v_mem_norm_g'], 'v_final_norm_g': out['v_final_norm_g']}


def _loss(weights, diff, rest, loss_target):
    with _jax.named_scope("forward"):
        args = {**rest, TWIN_DIFF_INPUT: diff, **{k: w.astype(_WEIGHT_DTYPES[k]) for k, w in weights.items()}}
        y = _forward(args)
    with _jax.named_scope("loss_head"):
        err = _jnp.square(y.astype(_jnp.float32) - loss_target)
        return 0.5 * _jnp.sum(_jnp.mean(err, axis=-1)) if err.ndim else 0.5 * err


def _adamw(w, g, m, v):
    m = ADAM_B1 * m + (1.0 - ADAM_B1) * g
    v = ADAM_B2 * v + (1.0 - ADAM_B2) * _jnp.square(g)
    m_hat = m / (1.0 - ADAM_B1 ** ADAM_STEP)
    v_hat = v / (1.0 - ADAM_B2 ** ADAM_STEP)
    delta = -ADAM_LR * (m_hat / (_jnp.sqrt(v_hat) + ADAM_EPS) + ADAM_WD * w)
    return delta, m, v


def reference(x, mem, w_in, ret_decay_logit, w_ret_o, w_pool_grp, pool_scale, w_pool_o, w_mem_kv, w_mem_o, w_out, w_ff1, w_ff2, norm1_g, norm2_g, mem_norm_g, final_norm_g, loss_target, m_w_in, m_ret_decay_logit, m_w_ret_o, m_w_pool_grp, m_pool_scale, m_w_pool_o, m_w_mem_kv, m_w_mem_o, m_w_out, m_w_ff1, m_w_ff2, m_norm1_g, m_norm2_g, m_mem_norm_g, m_final_norm_g, v_w_in, v_ret_decay_logit, v_w_ret_o, v_w_pool_grp, v_pool_scale, v_w_pool_o, v_w_mem_kv, v_w_mem_o, v_w_out, v_w_ff1, v_w_ff2, v_norm1_g, v_norm2_g, v_mem_norm_g, v_final_norm_g):
    given = dict(x=x, mem=mem, w_in=w_in, ret_decay_logit=ret_decay_logit, w_ret_o=w_ret_o, w_pool_grp=w_pool_grp, pool_scale=pool_scale, w_pool_o=w_pool_o, w_mem_kv=w_mem_kv, w_mem_o=w_mem_o, w_out=w_out, w_ff1=w_ff1, w_ff2=w_ff2, norm1_g=norm1_g, norm2_g=norm2_g, mem_norm_g=mem_norm_g, final_norm_g=final_norm_g, loss_target=loss_target, m_w_in=m_w_in, m_ret_decay_logit=m_ret_decay_logit, m_w_ret_o=m_w_ret_o, m_w_pool_grp=m_w_pool_grp, m_pool_scale=m_pool_scale, m_w_pool_o=m_w_pool_o, m_w_mem_kv=m_w_mem_kv, m_w_mem_o=m_w_mem_o, m_w_out=m_w_out, m_w_ff1=m_w_ff1, m_w_ff2=m_w_ff2, m_norm1_g=m_norm1_g, m_norm2_g=m_norm2_g, m_mem_norm_g=m_mem_norm_g, m_final_norm_g=m_final_norm_g, v_w_in=v_w_in, v_ret_decay_logit=v_ret_decay_logit, v_w_ret_o=v_w_ret_o, v_w_pool_grp=v_w_pool_grp, v_pool_scale=v_pool_scale, v_w_pool_o=v_w_pool_o, v_w_mem_kv=v_w_mem_kv, v_w_mem_o=v_w_mem_o, v_w_out=v_w_out, v_w_ff1=v_w_ff1, v_w_ff2=v_w_ff2, v_norm1_g=v_norm1_g, v_norm2_g=v_norm2_g, v_mem_norm_g=v_mem_norm_g, v_final_norm_g=v_final_norm_g)
    weights = {n: given[n] for n in TWIN_WEIGHTS}
    shared = {n: given[n] for n in SHARED_INPUTS}
    per_example = {n: given[n] for n in ['x', 'mem']}
    grad_fn = _jax.value_and_grad(_loss, argnums=(0, 1))

    def one_microbatch(ex, loss_target):
        ex = dict(ex)
        diff = ex.pop(TWIN_DIFF_INPUT)
        return grad_fn(weights, diff, {**shared, **ex}, loss_target)

    if N_MICROBATCH == 1:
        loss, (grad_w, grad_x) = one_microbatch(per_example, given["loss_target"])
    else:
        def body(carry, xs):
            loss_sum, grad_sum = carry
            l_k, (gw_k, gx_k) = one_microbatch(xs[0], xs[1])
            with _jax.named_scope("update"):
                return (loss_sum + l_k, _jax.tree.map(_jnp.add, grad_sum, gw_k)), gx_k

        init = (_jnp.zeros((), _jnp.float32), _jax.tree.map(_jnp.zeros_like, weights))
        (loss, grad_w), grad_x = _jax.lax.scan(body, init, (per_example, given["loss_target"]))
    with _jax.named_scope("update"):
        delta_w, new_m, new_v = {}, {}, {}
        for n in TWIN_WEIGHTS:
            delta_w[n], new_m[n], new_v[n] = _adamw(weights[n], grad_w[n], given["m_" + n], given["v_" + n])
    return (loss, grad_x, *[grad_w[n] for n in TWIN_WEIGHTS], *[delta_w[n] for n in TWIN_WEIGHTS],
            *[new_m[n] for n in TWIN_WEIGHTS], *[new_v[n] for n in TWIN_WEIGHTS])
```

```python
import functools

import jax
import jax.numpy as jnp
import numpy as np
from jax import lax
from jax.experimental import pallas as pl
from jax.experimental.pallas import tpu as pltpu

F32 = jnp.float32
MM = jnp.bfloat16
N_DEV = 8
HEADS = 4
POOL_WINDOWS = (2, 4, 8, 16)
EPS = 1e-6
ROPE_BASE = 10000.0
ADAM_LR, ADAM_B1, ADAM_B2, ADAM_EPS, ADAM_WD, ADAM_STEP = 0.001, 0.9, 0.999, 1e-08, 0.01, 10
V7X_VMEM_LIMIT = 56 * 1024 * 1024
MESH = pl.DeviceIdType.MESH


def _params(n_axes):
    return pltpu.CompilerParams(dimension_semantics=("arbitrary",) * n_axes,
                                vmem_limit_bytes=V7X_VMEM_LIMIT)


def _tile(n, pref, align=128):
    cands = [c for c in range(align, min(pref, n) + 1, align) if n % c == 0]
    return max(cands) if cands else n


def _sigmoid(z):
    return 1.0 / (1.0 + jnp.exp(-z))


def _dot(a, b):
    return jnp.dot(a, b, preferred_element_type=F32)


def _dot_nt(a, b):
    return lax.dot_general(a, b, (((1,), (1,)), ((), ())), preferred_element_type=F32)


def _dot_tn(a, b):
    return lax.dot_general(a, b, (((0,), (0,)), ((), ())), preferred_element_type=F32)


def _pmm(name, prologue, row_ins, vec_ins, w, *, tm, tn, residual=None, save_a=False,
         epilogue=None, epi_ins=(), out_dtypes=(F32,)):
    m = row_ins[0][0].shape[0]
    k, n = w.shape
    tm, tn = _tile(m, tm, 8), _tile(n, tn)
    n_row, n_vec, n_epi, n_out = len(row_ins), len(vec_ins), len(epi_ins), len(out_dtypes)
    has_res = residual is not None
    use_scr = prologue is not None

    def body(*refs):
        row_refs = refs[:n_row]
        p = n_row
        vec_refs = refs[p:p + n_vec]
        p += n_vec
        w_ref = refs[p]
        p += 1
        res_ref = refs[p] if has_res else None
        p += int(has_res)
        epi_refs = refs[p:p + n_epi]
        p += n_epi
        out_refs = refs[p:p + n_out]
        p += n_out
        a_out = refs[p] if save_a else None
        p += int(save_a)
        if use_scr:
            a_scr = refs[p]

            @pl.when(pl.program_id(1) == 0)
            def _():
                a = prologue(*[r[...] for r in row_refs], *[v[...] for v in vec_refs]).astype(MM)
                a_scr[...] = a
                if save_a:
                    a_out[...] = a

            a = a_scr[...]
        else:
            a = row_refs[0][...]
        acc = _dot(a, w_ref[...])
        if has_res:
            acc = acc + res_ref[...]
        outs = epilogue(acc, *[e[...] for e in epi_refs]) if epilogue is not None else (acc,)
        for o_ref, o in zip(out_refs, outs):
            o_ref[...] = o.astype(o_ref.dtype)

    in_specs = [pl.BlockSpec((tm, wd), lambda i, j, cb=cb: (i, cb)) for (_, wd, cb) in row_ins]
    in_specs += [pl.BlockSpec(v.shape, lambda i, j: (0, 0)) for v in vec_ins]
    in_specs += [pl.BlockSpec((k, tn), lambda i, j: (0, j))]
    args = [r[0] for r in row_ins] + list(vec_ins) + [w]
    if has_res:
        in_specs.append(pl.BlockSpec((tm, tn), lambda i, j: (i, j)))
        args.append(residual)
    for (arr, off) in epi_ins:
        assert off % tn == 0
        in_specs.append(pl.BlockSpec((tm, tn), lambda i, j, ob=off // tn: (i, ob + j)))
        args.append(arr)
    out_specs = [pl.BlockSpec((tm, tn), lambda i, j: (i, j)) for _ in out_dtypes]
    out_shape = [jax.ShapeDtypeStruct((m, n), dt) for dt in out_dtypes]
    if save_a:
        out_specs.append(pl.BlockSpec((tm, k), lambda i, j: (i, 0)))
        out_shape.append(jax.ShapeDtypeStruct((m, k), MM))
    scratch = [pltpu.VMEM((tm, k), MM)] if use_scr else []
    return pl.pallas_call(body, name=name, grid=(m // tm, n // tn), in_specs=in_specs,
                          out_specs=out_specs, out_shape=out_shape, scratch_shapes=scratch,
                          compiler_params=_params(2))(*args)


def _tnmm(name, a, b, *, tm=1024, tn=512, tk=512):
    t, m = a.shape
    n = b.shape[1]
    tm, tn, tk = _tile(m, tm), _tile(n, tn), _tile(t, tk, 8)
    nk = t // tk

    def body(a_ref, b_ref, o_ref, acc):
        kk = pl.program_id(2)

        @pl.when(kk == 0)
        def _():
            acc[...] = jnp.zeros_like(acc)

        acc[...] += _dot_tn(a_ref[...].astype(MM), b_ref[...].astype(MM))

        @pl.when(kk == nk - 1)
        def _():
            o_ref[...] = acc[...]

    return pl.pallas_call(
        body, name=name, grid=(m // tm, n // tn, nk),
        in_specs=[pl.BlockSpec((tk, tm), lambda i, j, kk: (kk, i)),
                  pl.BlockSpec((tk, tn), lambda i, j, kk: (kk, j))],
        out_specs=pl.BlockSpec((tm, tn), lambda i, j, kk: (i, j)),
        out_shape=jax.ShapeDtypeStruct((m, n), F32),
        scratch_shapes=[pltpu.VMEM((tm, tn), F32)],
        compiler_params=_params(3))(a, b)


def _rms_prologue(x, g):
    r = lax.rsqrt(jnp.mean(x * x, axis=-1, keepdims=True) + EPS)
    return x * r * g


def _rms_bwd_rows(dh, x, g):
    d = x.shape[-1]
    r = lax.rsqrt(jnp.mean(x * x, axis=-1, keepdims=True) + EPS)
    xh = x * r
    dxh = dh * g
    dx = r * (dxh - xh * (jnp.sum(dxh * xh, axis=-1, keepdims=True) / d))
    dg = jnp.sum(dh * xh, axis=0, keepdims=True)
    return dx, dg


def _rms_bwd(name, dh, x, g, dres, *, tm=256):
    m, d = x.shape
    tm = min(tm, m)
    has_res = dres is not None

    def body(*refs):
        if has_res:
            dh_ref, x_ref, g_ref, r_ref, dx_ref, dg_ref = refs
        else:
            dh_ref, x_ref, g_ref, dx_ref, dg_ref = refs
        dx, dg = _rms_bwd_rows(dh_ref[...], x_ref[...], g_ref[...])
        if has_res:
            dx = dx + r_ref[...]
        dx_ref[...] = dx

        @pl.when(pl.program_id(0) == 0)
        def _():
            dg_ref[...] = jnp.zeros_like(dg_ref)

        dg_ref[...] += dg

    row = pl.BlockSpec((tm, d), lambda i: (i, 0))
    vec = pl.BlockSpec((1, d), lambda i: (0, 0))
    in_specs = [row, row, vec] + ([row] if has_res else [])
    args = [dh, x, g] + ([dres] if has_res else [])
    return pl.pallas_call(body, name=name, grid=(m // tm,), in_specs=in_specs, out_specs=[row, vec],
                          out_shape=[jax.ShapeDtypeStruct((m, d), F32), jax.ShapeDtypeStruct((1, d), F32)],
                          compiler_params=_params(1))(*args)


def _loss_head(x, target, g, *, tm=256):
    m, d = x.shape
    tm = min(tm, m)

    def body(x_ref, t_ref, g_ref, dx_ref, dg_ref, loss_ref):
        xv, gv = x_ref[...], g_ref[...]
        y = _rms_prologue(xv, gv)
        err = y - t_ref[...]
        part = 0.5 * jnp.sum(jnp.sum(err * err, axis=-1, keepdims=True) / d)
        dx, dg = _rms_bwd_rows(err / d, xv, gv)
        dx_ref[...] = dx

        @pl.when(pl.program_id(0) == 0)
        def _():
            dg_ref[...] = jnp.zeros_like(dg_ref)
            loss_ref[...] = jnp.zeros_like(loss_ref)

        dg_ref[...] += dg
        loss_ref[...] += jnp.full(loss_ref.shape, part, F32)

    row = pl.BlockSpec((tm, d), lambda i: (i, 0))
    vec = pl.BlockSpec((1, d), lambda i: (0, 0))
    lspec = pl.BlockSpec((1, 128), lambda i: (0, 0))
    return pl.pallas_call(body, name="loss_head", grid=(m // tm,), in_specs=[row, row, vec],
                          out_specs=[row, vec, lspec],
                          out_shape=[jax.ShapeDtypeStruct((m, d), F32), jax.ShapeDtypeStruct((1, d), F32),
                                     jax.ShapeDtypeStruct((1, 128), F32)],
                          compiler_params=_params(1))(x, target, g)


def _rot(xv, cos2, sin2, half):
    return xv * cos2 + pltpu.roll(xv, half, 1) * sin2


def _rot_t(dv, cos2, sin2, half):
    return dv * cos2 + pltpu.roll(dv * sin2, half, 1)


def _ret_pre(proj, cos2, sin2, d, seq, *, ts=512):
    t = proj.shape[0]
    ts = min(ts, seq)
    dk = d // 8
    ns = seq // ts
    scale = float(dk) ** -0.5

    def body(q_ref, k_ref, v_ref, c_ref, s_ref, qo, ko, vo):
        c, s = c_ref[...], s_ref[...]
        for h in range(HEADS):
            sl = slice(h * dk, (h + 1) * dk)
            qo[:, sl] = _rot(q_ref[:, sl], c, s, dk // 2).astype(MM)
            ko[:, sl] = (_rot(k_ref[:, sl], c, s, dk // 2) * scale).astype(MM)
        vo[...] = v_ref[...].astype(MM)

    half = pl.BlockSpec((ts, d // 2), lambda i: (i, 0))
    tab = pl.BlockSpec((ts, dk), lambda i: (i % ns, 0))
    return pl.pallas_call(
        body, name="ret_pre", grid=(t // ts,),
        in_specs=[half, pl.BlockSpec((ts, d // 2), lambda i: (i, 1)), pl.BlockSpec((ts, d), lambda i: (i, 1)),
                  tab, tab],
        out_specs=[half, half, pl.BlockSpec((ts, d), lambda i: (i, 0))],
        out_shape=[jax.ShapeDtypeStruct((t, d // 2), MM), jax.ShapeDtypeStruct((t, d // 2), MM),
                   jax.ShapeDtypeStruct((t, d), MM)],
        compiler_params=_params(1))(proj, proj, proj, cos2, sin2)


def _decay_tile(diff, lf, lb):
    return jnp.exp(jnp.where(diff >= 0, lf * diff, -lb * diff))


def _ret_core_fwd(qr, kr, vb, proj, lg, d, bl, seq, *, tq=256):
    t = qr.shape[0]
    dk, dv = d // 8, d // 4
    tq = min(tq, seq)
    nq = seq // tq

    def body(lg_ref, q_ref, k_ref, v_ref, g_ref, o_ref, a_ref):
        h, i = pl.program_id(1), pl.program_id(2)
        lf, lb = lg_ref[0, h], lg_ref[1, h]
        q = q_ref[...]
        ab = (lax.broadcasted_iota(jnp.int32, (tq, tq), 0)
              - lax.broadcasted_iota(jnp.int32, (tq, tq), 1)).astype(F32)

        def step(j, acc):
            kj = k_ref[pl.ds(pl.multiple_of(j * tq, tq), tq), :]
            vj = v_ref[pl.ds(pl.multiple_of(j * tq, tq), tq), :]
            diff = ab + ((i - j) * tq).astype(F32)
            p = _dot_nt(q, kj) * _decay_tile(diff, lf, lb)
            return acc + _dot(p.astype(MM), vj)

        o = lax.fori_loop(0, nq, step, jnp.zeros((tq, dv), F32))
        o_ref[...] = o
        mu = jnp.mean(o, axis=-1, keepdims=True)
        oc = o - mu
        on = oc * lax.rsqrt(jnp.mean(oc * oc, axis=-1, keepdims=True) + EPS)
        g = g_ref[...]
        a_ref[...] = (on * (g * _sigmoid(g))).astype(MM)

    return pl.pallas_call(
        body, name="ret_core_fwd", grid=(bl, HEADS, nq),
        in_specs=[pl.BlockSpec(memory_space=pltpu.SMEM),
                  pl.BlockSpec((tq, dk), lambda b, h, i: (b * nq + i, h)),
                  pl.BlockSpec((seq, dk), lambda b, h, i: (b, h)),
                  pl.BlockSpec((seq, dv), lambda b, h, i: (b, h)),
                  pl.BlockSpec((tq, dv), lambda b, h, i: (b * nq + i, 2 * HEADS + h))],
        out_specs=[pl.BlockSpec((tq, dv), lambda b, h, i: (b * nq + i, h)),
                   pl.BlockSpec((tq, dv), lambda b, h, i: (b * nq + i, h))],
        out_shape=[jax.ShapeDtypeStruct((t, d), F32), jax.ShapeDtypeStruct((t, d), MM)],
        compiler_params=_params(3))(lg, qr, kr, vb, proj)


def _ret_post_bwd(da, proj, o_raw, d, *, ts=512):
    t = da.shape[0]
    dv = d // 4
    ts = min(ts, t)

    def body(da_ref, g_ref, o_ref, dg_ref, do_ref):
        o, g, dav = o_ref[...], g_ref[...], da_ref[...]
        mu = jnp.mean(o, axis=-1, keepdims=True)
        oc = o - mu
        r = lax.rsqrt(jnp.mean(oc * oc, axis=-1, keepdims=True) + EPS)
        on = oc * r
        sg = _sigmoid(g)
        don = dav * (g * sg)
        dg_ref[...] = (dav * on * (sg * (1.0 + g * (1.0 - sg)))).astype(MM)
        do = r * (don - jnp.mean(don, axis=-1, keepdims=True) - on * jnp.mean(don * on, axis=-1, keepdims=True))
        do_ref[...] = do.astype(MM)

    blk = pl.BlockSpec((ts, dv), lambda i, h: (i, h))
    return pl.pallas_call(
        body, name="ret_post_bwd", grid=(t // ts, HEADS),
        in_specs=[blk, pl.BlockSpec((ts, dv), lambda i, h: (i, 2 * HEADS + h)), blk],
        out_specs=[blk, blk],
        out_shape=[jax.ShapeDtypeStruct((t, d), MM), jax.ShapeDtypeStruct((t, d), MM)],
        compiler_params=_params(2))(da, proj, o_raw)


def _ret_core_bwd(qr, kr, vb, do, cos2, sin2, lg, d, bl, seq, *, tq=256):
    t = qr.shape[0]
    dk, dv = d // 8, d // 4
    tq = min(tq, seq)
    nq = seq // tq
    scale = float(dk) ** -0.5

    def body(lg_ref, q_ref, k_ref, v_ref, do_ref, c_ref, s_ref, dq_ref, dk_ref, dv_ref, dlf_ref, dlb_ref,
             dq_acc, dk_acc, dv_acc, xf, xb):
        h = pl.program_id(1)
        lf, lb = lg_ref[0, h], lg_ref[1, h]
        dk_acc[...] = jnp.zeros_like(dk_acc)
        dv_acc[...] = jnp.zeros_like(dv_acc)
        xf[...] = jnp.zeros_like(xf)
        xb[...] = jnp.zeros_like(xb)
        ab = (lax.broadcasted_iota(jnp.int32, (tq, tq), 0)
              - lax.broadcasted_iota(jnp.int32, (tq, tq), 1)).astype(F32)

        def outer(i, carry):
            rows_i = pl.ds(pl.multiple_of(i * tq, tq), tq)
            qi = q_ref[rows_i, :]
            doi = do_ref[rows_i, :]

            def inner(j, dq):
                rows_j = pl.ds(pl.multiple_of(j * tq, tq), tq)
                kj = k_ref[rows_j, :]
                vj = v_ref[rows_j, :]
                diff = ab + ((i - j) * tq).astype(F32)
                fwd = diff >= 0
                dec = _decay_tile(diff, lf, lb)
                p = _dot_nt(qi, kj) * dec
                da = _dot_nt(doi, vj)
                x = p * da * diff
                xf[...] += jnp.where(fwd, x, 0.0)
                xb[...] += jnp.where(fwd, 0.0, -x)
                pb = p.astype(MM)
                dpb = (da * dec).astype(MM)
                dv_acc[rows_j, :] += _dot_tn(pb, doi)
                dk_acc[rows_j, :] += _dot_tn(dpb, qi)
                return dq + _dot(dpb, kj)

            dq_acc[rows_i, :] = lax.fori_loop(0, nq, inner, jnp.zeros((tq, dk), F32))
            return carry

        lax.fori_loop(0, nq, outer, 0)
        c, s = c_ref[...], s_ref[...]
        dq_ref[...] = _rot_t(dq_acc[...], c, s, dk // 2).astype(MM)
        dk_ref[...] = (_rot_t(dk_acc[...], c, s, dk // 2) * scale).astype(MM)
        dv_ref[...] = dv_acc[...].astype(MM)
        dlf_ref[...] = jnp.full(dlf_ref.shape, jnp.sum(xf[...]), F32)
        dlb_ref[...] = jnp.full(dlb_ref.shape, jnp.sum(xb[...]), F32)

    qk = pl.BlockSpec((seq, dk), lambda b, h: (b, h))
    vv = pl.BlockSpec((seq, dv), lambda b, h: (b, h))
    tab = pl.BlockSpec((seq, dk), lambda b, h: (0, 0))
    dl = pl.BlockSpec((None, 8, 128), lambda b, h: (b * HEADS + h, 0, 0))
    return pl.pallas_call(
        body, name="ret_core_bwd", grid=(bl, HEADS),
        in_specs=[pl.BlockSpec(memory_space=pltpu.SMEM), qk, qk, vv, vv, tab, tab],
        out_specs=[qk, qk, vv, dl, dl],
        out_shape=[jax.ShapeDtypeStruct((t, d // 2), MM), jax.ShapeDtypeStruct((t, d // 2), MM),
                   jax.ShapeDtypeStruct((t, d), MM),
                   jax.ShapeDtypeStruct((bl * HEADS, 8, 128), F32), jax.ShapeDtypeStruct((bl * HEADS, 8, 128), F32)],
        scratch_shapes=[pltpu.VMEM((seq, dk), F32), pltpu.VMEM((seq, dk), F32), pltpu.VMEM((seq, dv), F32),
                        pltpu.VMEM((tq, tq), F32), pltpu.VMEM((tq, tq), F32)],
        compiler_params=_params(2))(lg, qr, kr, vb, do, cos2, sin2)


def _window_count(row, w, seq):
    return (jnp.minimum(row + w // 2, seq) - jnp.maximum(row - w // 2, 0)).astype(F32)


def _window_sum(pv, row, w, seq, sign):
    acc = None
    for j in range(-(w // 2), w // 2):
        if j == 0:
            term = pv
        else:
            src = row + sign * j
            term = jnp.where((src >= 0) & (src < seq), pltpu.roll(pv, (-sign * j) % seq, 0), 0.0)
        acc = term if acc is None else acc + term
    return acc


def _pool_fwd(proj, w_grp, scale, d, bl, seq):
    t = proj.shape[0]
    dg = d // 8

    def body(p_ref, w_ref, s_ref, y_ref):
        row = lax.broadcasted_iota(jnp.int32, (seq, dg), 0)
        for gi, w in enumerate(POOL_WINDOWS):
            sl = slice(gi * dg, (gi + 1) * dg)
            pg = p_ref[:, sl]
            mixed = _window_sum(pg, row, w, seq, 1) / _window_count(row, w, seq) - pg
            yp = _dot(mixed.astype(MM), w_ref[gi].astype(MM))
            y_ref[:, sl] = (yp * s_ref[:, sl]).astype(MM)

    return pl.pallas_call(
        body, name="pool_fwd", grid=(bl,),
        in_specs=[pl.BlockSpec((seq, d // 2), lambda b: (b, 6)),
                  pl.BlockSpec(w_grp.shape, lambda b: (0, 0, 0)),
                  pl.BlockSpec((1, d // 2), lambda b: (0, 0))],
        out_specs=pl.BlockSpec((seq, d // 2), lambda b: (b, 0)),
        out_shape=jax.ShapeDtypeStruct((t, d // 2), MM),
        compiler_params=_params(1))(proj, w_grp, scale)


def _pool_bwd(proj, dy, w_grp, scale, d, bl, seq):
    t = proj.shape[0]
    dg = d // 8

    def body(p_ref, dy_ref, w_ref, s_ref, dp_ref, dw_ref, ds_ref):
        @pl.when(pl.program_id(0) == 0)
        def _():
            dw_ref[...] = jnp.zeros_like(dw_ref)
            ds_ref[...] = jnp.zeros_like(ds_ref)

        row = lax.broadcasted_iota(jnp.int32, (seq, dg), 0)
        for gi, w in enumerate(POOL_WINDOWS):
            sl = slice(gi * dg, (gi + 1) * dg)
            pg = p_ref[:, sl]
            cnt = _window_count(row, w, seq)
            mixb = (_window_sum(pg, row, w, seq, 1) / cnt - pg).astype(MM)
            wgb = w_ref[gi].astype(MM)
            yp = _dot(mixb, wgb)
            dyg = dy_ref[:, sl]
            ds_ref[:, sl] += jnp.sum(dyg * yp, axis=0, keepdims=True)
            dyp = (dyg * s_ref[:, sl]).astype(MM)
            dmixed = _dot_nt(dyp, wgb)
            dw_ref[gi] += _dot_tn(mixb, dyp)
            dp_ref[:, sl] = (_window_sum(dmixed / cnt, row, w, seq, -1) - dmixed).astype(MM)

    half = pl.BlockSpec((seq, d // 2), lambda b: (b, 0))
    wspec = pl.BlockSpec(w_grp.shape, lambda b: (0, 0, 0))
    sspec = pl.BlockSpec((1, d // 2), lambda b: (0, 0))
    return pl.pallas_call(
        body, name="pool_bwd", grid=(bl,),
        in_specs=[pl.BlockSpec((seq, d // 2), lambda b: (b, 6)), half, wspec, sspec],
        out_specs=[half, wspec, sspec],
        out_shape=[jax.ShapeDtypeStruct((t, d // 2), MM), jax.ShapeDtypeStruct(w_grp.shape, F32),
                   jax.ShapeDtypeStruct((1, d // 2), F32)],
        compiler_params=_params(1))(proj, dy, w_grp, scale)


def _attn_probs(q, kk, dh):
    s = _dot_nt(q, kk) * (float(dh) ** -0.5)
    e = jnp.exp(s - jnp.max(s, axis=-1, keepdims=True))
    return e / jnp.sum(e, axis=-1, keepdims=True)


def _attn_fwd(proj, kv, d, bl, seq, mlen, *, tq=512):
    t = proj.shape[0]
    dh = d // 8
    tq = min(tq, seq)
    nq = seq // tq

    def body(q_ref, k_ref, v_ref, o_ref):
        a = _attn_probs(q_ref[...].astype(MM), k_ref[...].astype(MM), dh)
        o_ref[...] = _dot(a.astype(MM), v_ref[...].astype(MM)).astype(MM)

    return pl.pallas_call(
        body, name="attn_fwd", grid=(bl, HEADS, nq),
        in_specs=[pl.BlockSpec((tq, dh), lambda b, h, i: (b * nq + i, 7 * HEADS + h)),
                  pl.BlockSpec((mlen, dh), lambda b, h, i: (b, h)),
                  pl.BlockSpec((mlen, dh), lambda b, h, i: (b, HEADS + h))],
        out_specs=pl.BlockSpec((tq, dh), lambda b, h, i: (b * nq + i, h)),
        out_shape=jax.ShapeDtypeStruct((t, d // 2), MM),
        compiler_params=_params(3))(proj, kv, kv)


def _attn_bwd(proj, kv, do, d, bl, seq, mlen, *, tq=512):
    t = proj.shape[0]
    dh = d // 8
    tq = min(tq, seq)
    nq = seq // tq

    def body(q_ref, k_ref, v_ref, do_ref, dq_ref, dk_ref, dv_ref):
        @pl.when(pl.program_id(2) == 0)
        def _():
            dk_ref[...] = jnp.zeros_like(dk_ref)
            dv_ref[...] = jnp.zeros_like(dv_ref)

        q, kk, vv = q_ref[...].astype(MM), k_ref[...].astype(MM), v_ref[...].astype(MM)
        dov = do_ref[...].astype(MM)
        a = _attn_probs(q, kk, dh)
        dp = _dot_nt(dov, vv)
        ds = (a * (dp - jnp.sum(dp * a, axis=-1, keepdims=True)) * (float(dh) ** -0.5)).astype(MM)
        dq_ref[...] = _dot(ds, kk).astype(MM)
        dk_ref[...] += _dot_tn(ds, q)
        dv_ref[...] += _dot_tn(a.astype(MM), dov)

    qs = pl.BlockSpec((tq, dh), lambda b, h, i: (b * nq + i, h))
    ms = pl.BlockSpec((mlen, dh), lambda b, h, i: (b, h))
    return pl.pallas_call(
        body, name="attn_bwd", grid=(bl, HEADS, nq),
        in_specs=[pl.BlockSpec((tq, dh), lambda b, h, i: (b * nq + i, 7 * HEADS + h)), ms,
                  pl.BlockSpec((mlen, dh), lambda b, h, i: (b, HEADS + h)), qs],
        out_specs=[qs, ms, ms],
        out_shape=[jax.ShapeDtypeStruct((t, d // 2), MM), jax.ShapeDtypeStruct((bl * mlen, d // 2), F32),
                   jax.ShapeDtypeStruct((bl * mlen, d // 2), F32)],
        compiler_params=_params(3))(proj, kv, kv, do)


def _all_gather(name, shard):
    def body(x_ref, out_ref, send_sems, recv_sems, local_sem):
        x, y, c = lax.axis_index("x"), lax.axis_index("y"), lax.axis_index("c")
        me, sibling = (x, y, c), (x, y, 1 - c)
        chips = [(1 - x, y), (x, 1 - y), (1 - x, 1 - y)]

        def slot(px, py, pc):
            return out_ref.at[4 * px + 2 * py + pc]

        def copy(k, block, to, src=None):
            return pltpu.make_async_remote_copy(
                src_ref=slot(*block) if src is None else src, dst_ref=slot(*block),
                send_sem=send_sems.at[k], recv_sem=recv_sems.at[k], device_id=to, device_id_type=MESH)

        mine = pltpu.make_async_copy(x_ref, slot(*me), local_sem)
        mine.start()
        first = [copy(0, me, sibling, src=x_ref)]
        first += [copy(1 + j, me, (*chip, c), src=x_ref) for j, chip in enumerate(chips)]
        for cp in first:
            cp.start()
        passed = [copy(4 + j, (*chip, c), sibling) for j, chip in enumerate(chips)]
        for j, chip in enumerate(chips):
            copy(1 + j, (*chip, c), me).wait_recv()
            passed[j].start()
        copy(0, sibling, me).wait_recv()
        for j, chip in enumerate(chips):
            copy(4 + j, (*chip, 1 - c), me).wait_recv()
        for cp in first + passed:
            cp.wait_send()
        mine.wait()

    return pl.pallas_call(
        body, name=name, out_shape=jax.ShapeDtypeStruct((N_DEV,) + shard.shape, shard.dtype),
        in_specs=[pl.BlockSpec(memory_space=pl.ANY)], out_specs=pl.BlockSpec(memory_space=pl.ANY),
        scratch_shapes=[pltpu.SemaphoreType.DMA((7,)), pltpu.SemaphoreType.DMA((7,)), pltpu.SemaphoreType.DMA],
    )(shard)


def _all_to_all(name, send):
    def body(s_ref, r_ref, send_sems, recv_sems, local_sem):
        x, y, c = lax.axis_index("x"), lax.axis_index("y"), lax.axis_index("c")
        me_idx = 4 * x + 2 * y + c
        mine = pltpu.make_async_copy(s_ref.at[me_idx], r_ref.at[me_idx], local_sem)
        mine.start()
        copies = []
        for k in range(1, N_DEV):
            peer = (1 - x if k & 4 else x, 1 - y if k & 2 else y, 1 - c if k & 1 else c)
            p_idx = 4 * peer[0] + 2 * peer[1] + peer[2]
            out = pltpu.make_async_remote_copy(
                src_ref=s_ref.at[p_idx], dst_ref=r_ref.at[me_idx], send_sem=send_sems.at[k - 1],
                recv_sem=recv_sems.at[k - 1], device_id=peer, device_id_type=MESH)
            arrival = pltpu.make_async_remote_copy(
                src_ref=s_ref.at[p_idx], dst_ref=r_ref.at[p_idx], send_sem=send_sems.at[k - 1],
                recv_sem=recv_sems.at[k - 1], device_id=peer, device_id_type=MESH)
            out.start()
            copies.append((out, arrival))
        for _, arrival in copies:
            arrival.wait_recv()
        for out, _ in copies:
            out.wait_send()
        mine.wait()

    return pl.pallas_call(
        body, name=name, out_shape=jax.ShapeDtypeStruct(send.shape, send.dtype),
        in_specs=[pl.BlockSpec(memory_space=pl.ANY)], out_specs=pl.BlockSpec(memory_space=pl.ANY),
        scratch_shapes=[pltpu.SemaphoreType.DMA((7,)), pltpu.SemaphoreType.DMA((7,)), pltpu.SemaphoreType.DMA],
    )(send)


def _adamw(name, parts, w, m, v, *, tr=256):
    r, c = w.shape
    tr = max(t8 for t8 in range(8, min(tr, r) + 1, 8) if r % t8 == 0)
    c1 = 1.0 - ADAM_B1 ** ADAM_STEP
    c2 = 1.0 - ADAM_B2 ** ADAM_STEP

    def body(p_ref, w_ref, m_ref, v_ref, g_out, d_out, m_out, v_out):
        g = p_ref[0].astype(F32)
        for s in range(1, N_DEV):
            g = g + p_ref[s].astype(F32)
        mn = ADAM_B1 * m_ref[...] + (1.0 - ADAM_B1) * g
        vn = ADAM_B2 * v_ref[...] + (1.0 - ADAM_B2) * (g * g)
        g_out[...] = g
        m_out[...] = mn
        v_out[...] = vn
        d_out[...] = -ADAM_LR * ((mn / c1) / (jnp.sqrt(vn / c2) + ADAM_EPS) + ADAM_WD * w_ref[...])

    row = pl.BlockSpec((tr, c), lambda i: (i, 0))
    return pl.pallas_call(
        body, name=name, grid=(r // tr,),
        in_specs=[pl.BlockSpec((N_DEV, tr, c), lambda i: (0, i, 0)), row, row, row],
        out_specs=[row] * 4, out_shape=[jax.ShapeDtypeStruct((r, c), F32)] * 4,
        compiler_params=_params(1))(parts, w, m, v)


_COL = ("w_in", "w_pool_o", "w_mem_o", "w_ff1")
_BIG = ("w_in", "w_ret_o", "w_pool_o", "w_mem_kv", "w_mem_o", "w_out", "w_ff1", "w_ff2")
_SMALL = ("ret_decay_logit", "w_pool_grp", "pool_scale", "norm1_g", "norm2_g", "mem_norm_g", "final_norm_g")
_WEIGHTS = ("w_in", "ret_decay_logit", "w_ret_o", "w_pool_grp", "pool_scale", "w_pool_o", "w_mem_kv", "w_mem_o",
            "w_out", "w_ff1", "w_ff2", "norm1_g", "norm2_g", "mem_norm_g", "final_norm_g")


def _pack_big(ws, d):
    depth = ws[_BIG[0]].shape[0]
    packed = jnp.concatenate([ws[n].reshape(depth, -1, d) for n in _BIG], axis=1)
    return packed.reshape(-1, d)


def _unpack_big(packed, like, d):
    depth = like[_BIG[0]].shape[0]
    p3 = packed.reshape(depth, -1, d)
    out, off = {}, 0
    for n in _BIG:
        rows = like[n].shape[1] * like[n].shape[2] // d
        out[n] = p3[:, off:off + rows].reshape(like[n].shape)
        off += rows
    return out


def _pack_small(ws, d):
    flat = jnp.concatenate([ws[n].reshape(-1) for n in _SMALL])
    rows = -(-flat.shape[0] // (8 * d)) * 8
    return jnp.pad(flat, (0, rows * d - flat.shape[0])).reshape(rows, d)


def _unpack_small(packed, like):
    flat, out, off = packed.reshape(-1), {}, 0
    for n in _SMALL:
        out[n] = flat[off:off + like[n].size].reshape(like[n].shape)
        off += like[n].size
    return out


def kernel(x, mem, w_in, ret_decay_logit, w_ret_o, w_pool_grp, pool_scale, w_pool_o, w_mem_kv, w_mem_o, w_out, w_ff1, w_ff2, norm1_g, norm2_g, mem_norm_g, final_norm_g, loss_target, m_w_in, m_ret_decay_logit, m_w_ret_o, m_w_pool_grp, m_pool_scale, m_w_pool_o, m_w_mem_kv, m_w_mem_o, m_w_out, m_w_ff1, m_w_ff2, m_norm1_g, m_norm2_g, m_mem_norm_g, m_final_norm_g, v_w_in, v_ret_decay_logit, v_w_ret_o, v_w_pool_grp, v_pool_scale, v_w_pool_o, v_w_mem_kv, v_w_mem_o, v_w_out, v_w_ff1, v_w_ff2, v_norm1_g, v_norm2_g, v_mem_norm_g, v_final_norm_g):
    w = dict(w_in=w_in, ret_decay_logit=ret_decay_logit, w_ret_o=w_ret_o, w_pool_grp=w_pool_grp,
             pool_scale=pool_scale, w_pool_o=w_pool_o, w_mem_kv=w_mem_kv, w_mem_o=w_mem_o, w_out=w_out,
             w_ff1=w_ff1, w_ff2=w_ff2, norm1_g=norm1_g, norm2_g=norm2_g, mem_norm_g=mem_norm_g,
             final_norm_g=final_norm_g)
    mom = dict(w_in=m_w_in, ret_decay_logit=m_ret_decay_logit, w_ret_o=m_w_ret_o, w_pool_grp=m_w_pool_grp,
               pool_scale=m_pool_scale, w_pool_o=m_w_pool_o, w_mem_kv=m_w_mem_kv, w_mem_o=m_w_mem_o,
               w_out=m_w_out, w_ff1=m_w_ff1, w_ff2=m_w_ff2, norm1_g=m_norm1_g, norm2_g=m_norm2_g,
               mem_norm_g=m_mem_norm_g, final_norm_g=m_final_norm_g)
    vel = dict(w_in=v_w_in, ret_decay_logit=v_ret_decay_logit, w_ret_o=v_w_ret_o, w_pool_grp=v_w_pool_grp,
               pool_scale=v_pool_scale, w_pool_o=v_w_pool_o, w_mem_kv=v_w_mem_kv, w_mem_o=v_w_mem_o,
               w_out=v_w_out, w_ff1=v_w_ff1, w_ff2=v_w_ff2, norm1_g=v_norm1_g, norm2_g=v_norm2_g,
               mem_norm_g=v_mem_norm_g, final_norm_g=v_final_norm_g)

    bl, seq, d = x.shape
    mlen = mem.shape[1]
    depth = w_in.shape[0]
    t = bl * seq
    dk = d // 8

    w_pack = _pack_big(w, d)
    gathered = _all_gather("gather_weights", w_pack.astype(MM)).reshape(N_DEV, depth, -1, d)
    full, full_t, off = {}, {}, 0
    for n in _BIG:
        a, b = w[n].shape[1:]
        rows = a * b // d
        g = gathered[:, :, off:off + rows].reshape(N_DEV, depth, a, b)
        off += rows
        if n in _COL:
            full[n] = jnp.transpose(g, (1, 2, 0, 3)).reshape(depth, a, N_DEV * b)
        else:
            full[n] = jnp.transpose(g, (1, 0, 2, 3)).reshape(depth, N_DEV * a, b)
        full_t[n] = jnp.swapaxes(full[n], 1, 2)

    inv = ROPE_BASE ** (-jnp.arange(0, dk, 2, dtype=F32) / dk)
    ang = jnp.arange(seq, dtype=F32)[:, None] * inv[None, :]
    cos2 = jnp.concatenate([jnp.cos(ang), jnp.cos(ang)], axis=-1)
    sin2 = jnp.concatenate([-jnp.sin(ang), jnp.sin(ang)], axis=-1)
    log_g = jax.nn.log_sigmoid(ret_decay_logit)
    x2 = x.reshape(t, d)
    mem2 = mem.reshape(bl * mlen, d)
    gmem = mem_norm_g.reshape(1, d)

    def merge(o_r, o_p, o_m, g_r, g_p, g_m):
        return _sigmoid(g_r) * o_r + _sigmoid(g_p) * o_p + _sigmoid(g_m) * o_m

    def relu2(u):
        r = jnp.maximum(u, 0.0)
        return r * r

    def ident(a):
        return a

    saved = []
    xc = x2
    for l in range(depth):
        s = dict(x_in=xc)
        g1 = norm1_g[l].reshape(1, d)
        g2 = norm2_g[l].reshape(1, d)
        s["proj"], s["h1"] = _pmm("proj", _rms_prologue, [(xc, d, 0)], [g1], full["w_in"][l],
                                  tm=512, tn=1024, save_a=True)
        proj = s["proj"]
        s["qr"], s["kr"], s["vb"] = _ret_pre(proj, cos2, sin2, d, seq)
        s["o_raw"], s["a_ret"] = _ret_core_fwd(s["qr"], s["kr"], s["vb"], proj, log_g[l], d, bl, seq)
        s["y"] = _pool_fwd(proj, w_pool_grp[l], pool_scale[l].reshape(1, -1), d, bl, seq)
        s["kv"], s["memn"] = _pmm("mem_kv", _rms_prologue, [(mem2, d, 0)], [gmem], full["w_mem_kv"][l],
                                  tm=512, tn=512, save_a=True)
        s["o_att"] = _attn_fwd(proj, s["kv"], d, bl, seq, mlen)
        (s["o_ret"],) = _pmm("ret_o", None, [(s["a_ret"], d, 0)], [], full["w_ret_o"][l], tm=512, tn=512)
        (s["o_pool"],) = _pmm("pool_o", None, [(s["y"], d // 2, 0)], [], full["w_pool_o"][l], tm=512, tn=512)
        (s["o_mem"],) = _pmm("mem_o", None, [(s["o_att"], d // 2, 0)], [], full["w_mem_o"][l], tm=512, tn=512)
        s["x_mid"], s["merged"] = _pmm(
            "merge_out", merge,
            [(s["o_ret"], d, 0), (s["o_pool"], d, 0), (s["o_mem"], d, 0), (proj, d, 4), (proj, d, 5), (proj, d, 6)],
            [], full["w_out"][l], tm=256, tn=512, residual=xc, save_a=True)
        s["u"], s["h2"] = _pmm("ff1", _rms_prologue, [(s["x_mid"], d, 0)], [g2], full["w_ff1"][l],
                               tm=512, tn=1024, save_a=True)
        xc, s["a"] = _pmm("ff2", relu2, [(s["u"], s["u"].shape[1], 0)], [], full["w_ff2"][l],
                          tm=256, tn=512, residual=s["x_mid"], save_a=True)
        saved.append(s)

    dxc, g_final, loss_part = _loss_head(xc, loss_target.reshape(t, d), final_norm_g.reshape(1, d))
    loss = lax.psum(loss_part[0, 0], ("x", "y", "c"))

    grads = {n: [None] * depth for n in _BIG + ("w_pool_grp", "pool_scale", "norm1_g", "norm2_g", "ret_decay_logit")}
    dmemn = jnp.zeros((bl * mlen, d), F32)

    def relu2_bwd(acc, u):
        return (acc * (2.0 * jnp.maximum(u, 0.0)),)

    def gates_bwd(acc, g_r, g_p, g_m, o_r, o_p, o_m):
        outs_o, outs_g = [], []
        for gz, oz in ((g_r, o_r), (g_p, o_p), (g_m, o_m)):
            sg = _sigmoid(gz)
            outs_o.append(acc * sg)
            outs_g.append(acc * oz * (sg * (1.0 - sg)))
        return tuple(outs_o + outs_g)

    for l in reversed(range(depth)):
        s = saved[l]
        proj = s["proj"]
        g1 = norm1_g[l].reshape(1, d)
        g2 = norm2_g[l].reshape(1, d)
        (du,) = _pmm("ff2_bwd", ident, [(dxc, d, 0)], [], full_t["w_ff2"][l], tm=512, tn=1024,
                     epilogue=relu2_bwd, epi_ins=[(s["u"], 0)], out_dtypes=(MM,))
        grads["w_ff2"][l] = _tnmm("dw_ff2", s["a"], dxc)
        (dh2,) = _pmm("ff1_bwd", None, [(du, du.shape[1], 0)], [], full_t["w_ff1"][l], tm=512, tn=512)
        grads["w_ff1"][l] = _tnmm("dw_ff1", s["h2"], du)
        dmid, grads["norm2_g"][l] = _rms_bwd("norm2_bwd", dh2, s["x_mid"], g2, dxc)
        d_oret, d_opool, d_omem, dgr, dgp, dgm = _pmm(
            "out_bwd", ident, [(dmid, d, 0)], [], full_t["w_out"][l], tm=512, tn=512, epilogue=gates_bwd,
            epi_ins=[(proj, 4 * d), (proj, 5 * d), (proj, 6 * d), (s["o_ret"], 0), (s["o_pool"], 0), (s["o_mem"], 0)],
            out_dtypes=(MM,) * 6)
        grads["w_out"][l] = _tnmm("dw_out", s["merged"], dmid)
        (da_ret,) = _pmm("ret_o_bwd", None, [(d_oret, d, 0)], [], full_t["w_ret_o"][l], tm=512, tn=512)
        grads["w_ret_o"][l] = _tnmm("dw_ret_o", s["a_ret"], d_oret)
        (dy,) = _pmm("pool_o_bwd", None, [(d_opool, d, 0)], [], full_t["w_pool_o"][l], tm=512, tn=512)
        grads["w_pool_o"][l] = _tnmm("dw_pool_o", s["y"], d_opool)
        (do_att,) = _pmm("mem_o_bwd", None, [(d_omem, d, 0)], [], full_t["w_mem_o"][l], tm=512, tn=512)
        grads["w_mem_o"][l] = _tnmm("dw_mem_o", s["o_att"], d_omem)
        dg_ret, do_ret = _ret_post_bwd(da_ret, proj, s["o_raw"], d)
        dq, dkk, dvv, dlf, dlb = _ret_core_bwd(s["qr"], s["kr"], s["vb"], do_ret, cos2, sin2, log_g[l], d, bl, seq)
        dl = jnp.stack([dlf[:, 0, 0].reshape(bl, HEADS).sum(0), dlb[:, 0, 0].reshape(bl, HEADS).sum(0)])
        grads["ret_decay_logit"][l] = dl * jax.nn.sigmoid(-ret_decay_logit[l])
        dp, grads["w_pool_grp"][l], dscale = _pool_bwd(proj, dy, w_pool_grp[l], pool_scale[l].reshape(1, -1),
                                                       d, bl, seq)
        grads["pool_scale"][l] = dscale.reshape(-1)
        dqm, dmk, dmv = _attn_bwd(proj, s["kv"], do_att, d, bl, seq, mlen)
        dkv = jnp.concatenate([dmk, dmv], axis=-1).astype(MM)
        grads["w_mem_kv"][l] = _tnmm("dw_mem_kv", s["memn"], dkv)
        (dmemn,) = _pmm("mem_kv_bwd", None, [(dkv, d, 0)], [], full_t["w_mem_kv"][l], tm=512, tn=512,
                        residual=dmemn)
        dproj = jnp.concatenate([dq, dkk, dvv, dg_ret, dp, dqm, dgr, dgp, dgm], axis=-1)
        (dh1,) = _pmm("proj_bwd", None, [(dproj, dproj.shape[1], 0)], [], full_t["w_in"][l], tm=512, tn=512)
        grads["w_in"][l] = _tnmm("dw_in", s["h1"], dproj)
        dxc, grads["norm1_g"][l] = _rms_bwd("norm1_bwd", dh1, s["x_in"], g1, dmid)

    _, g_memn = _rms_bwd("mem_norm_bwd", dmemn, mem2, gmem, None)
    grad_x = dxc.reshape(bl, seq, d)

    send_parts = []
    for n in _BIG:
        g = jnp.stack(grads[n])
        a, b = w[n].shape[1:]
        if n in _COL:
            g = jnp.transpose(g.reshape(depth, a, N_DEV, b), (2, 0, 1, 3))
        else:
            g = jnp.transpose(g.reshape(depth, N_DEV, a, b), (1, 0, 2, 3))
        send_parts.append(g.reshape(N_DEV, depth, -1, d))
    send = jnp.concatenate(send_parts, axis=2).reshape(N_DEV, -1, d).astype(MM)
    recv = _all_to_all("scatter_grads", send)
    big = _adamw("adamw_big", recv, w_pack, _pack_big(mom, d), _pack_big(vel, d))
    big = [_unpack_big(o, w, d) for o in big]

    small_g = dict(ret_decay_logit=jnp.stack(grads["ret_decay_logit"]), w_pool_grp=jnp.stack(grads["w_pool_grp"]),
                   pool_scale=jnp.stack(grads["pool_scale"]),
                   norm1_g=jnp.concatenate(grads["norm1_g"], axis=0), norm2_g=jnp.concatenate(grads["norm2_g"], axis=0),
                   mem_norm_g=g_memn.reshape(-1), final_norm_g=g_final.reshape(-1))
    small_parts = _all_gather("gather_small_grads", _pack_small(small_g, d))
    small = _adamw("adamw_small", small_parts, _pack_small(w, d), _pack_small(mom, d), _pack_small(vel, d))
    small = [_unpack_small(o, w) for o in small]

    outs = [loss, grad_x]
    for k in range(4):
        outs += [big[k][n] if n in _BIG else small[k][n] for n in _WEIGHTS]
    return tuple(outs)
```

```python
import jax
import jax.numpy as jnp
from jax import lax
from jax.experimental import pallas as pl
from jax.experimental.pallas import tpu as pltpu

F32 = jnp.float32
MM = jnp.bfloat16
N_DEV = 8
HEADS = 4
POOL_WINDOWS = (2, 4, 8, 16)
EPS = 1e-6
ROPE_BASE = 10000.0
ADAM_LR, ADAM_B1, ADAM_B2, ADAM_EPS, ADAM_WD, ADAM_STEP = 0.001, 0.9, 0.999, 1e-08, 0.01, 10
V7X_VMEM_LIMIT = 56 * 1024 * 1024
MESH = pl.DeviceIdType.MESH


def _params(n_axes):
    return pltpu.CompilerParams(dimension_semantics=("arbitrary",) * n_axes,
                                vmem_limit_bytes=V7X_VMEM_LIMIT)


def _tile(n, pref, align=128):
    cands = [c for c in range(align, min(pref, n) + 1, align) if n % c == 0]
    return max(cands) if cands else n


def _sigmoid(z):
    return 1.0 / (1.0 + jnp.exp(-z))


def _dot(a, b):
    return jnp.dot(a, b, preferred_element_type=F32)


def _dot_nt(a, b):
    return lax.dot_general(a, b, (((1,), (1,)), ((), ())), preferred_element_type=F32)


def _dot_tn(a, b):
    return lax.dot_general(a, b, (((0,), (0,)), ((), ())), preferred_element_type=F32)


def _pmm(name, prologue, row_ins, vec_ins, w, *, tm, tn, w_mode="nn", residual=None, save_a=False,
         epilogue=None, epi_ins=(), out_dtypes=(F32,)):
    m = row_ins[0][0].shape[0]
    wb = None
    if w_mode == "nn":
        k, n = w.shape
        tn = _tile(n, tn)
        w_spec = pl.BlockSpec((k, tn), lambda i, j: (0, j))
    elif w_mode == "nt":
        n, k = w.shape
        tn = _tile(n, tn)
        w_spec = pl.BlockSpec((tn, k), lambda i, j: (j, 0))
    elif w_mode == "col":
        _, k, wb = w.shape
        n = N_DEV * wb
        tn = _tile(wb, tn)
        w_spec = pl.BlockSpec((None, k, tn), lambda i, j, q=wb // tn: (j // q, 0, j % q))
    else:
        _, n, wb = w.shape
        k = N_DEV * wb
        tn = _tile(n, tn)
        w_spec = pl.BlockSpec((N_DEV, tn, wb), lambda i, j: (0, j, 0))
    tm = _tile(m, tm, 8)
    n_row, n_vec, n_epi, n_out = len(row_ins), len(vec_ins), len(epi_ins), len(out_dtypes)
    has_res = residual is not None
    use_scr = prologue is not None

    def body(*refs):
        row_refs = refs[:n_row]
        p = n_row
        vec_refs = refs[p:p + n_vec]
        p += n_vec
        w_ref = refs[p]
        p += 1
        res_ref = refs[p] if has_res else None
        p += int(has_res)
        epi_refs = refs[p:p + n_epi]
        p += n_epi
        out_refs = refs[p:p + n_out]
        p += n_out
        a_out = refs[p] if save_a else None
        p += int(save_a)
        if use_scr:
            a_src = refs[p]

            @pl.when(pl.program_id(1) == 0)
            def _():
                a = prologue(*[r[...] for r in row_refs], *[v[...] for v in vec_refs]).astype(MM)
                a_src[...] = a
                if save_a:
                    a_out[...] = a
        else:
            a_src = row_refs[0]
        if w_mode == "nt":
            acc = _dot_nt(a_src[...], w_ref[...])
        elif w_mode == "col_t":
            acc = _dot_nt(a_src[:, 0:wb], w_ref[0])
            for dev in range(1, N_DEV):
                acc = acc + _dot_nt(a_src[:, dev * wb:(dev + 1) * wb], w_ref[dev])
        else:
            acc = _dot(a_src[...], w_ref[...])
        if has_res:
            acc = acc + res_ref[...]
        outs = epilogue(acc, *[e[...] for e in epi_refs]) if epilogue is not None else (acc,)
        for o_ref, o in zip(out_refs, outs):
            o_ref[...] = o.astype(o_ref.dtype)

    in_specs = [pl.BlockSpec((tm, wd), lambda i, j, cb=cb: (i, cb)) for (_, wd, cb) in row_ins]
    in_specs += [pl.BlockSpec(v.shape, lambda i, j: (0, 0)) for v in vec_ins]
    in_specs += [w_spec]
    args = [r[0] for r in row_ins] + list(vec_ins) + [w]
    if has_res:
        in_specs.append(pl.BlockSpec((tm, tn), lambda i, j: (i, j)))
        args.append(residual)
    for (arr, off) in epi_ins:
        assert off % tn == 0
        in_specs.append(pl.BlockSpec((tm, tn), lambda i, j, ob=off // tn: (i, ob + j)))
        args.append(arr)
    out_specs = [pl.BlockSpec((tm, tn), lambda i, j: (i, j)) for _ in out_dtypes]
    out_shape = [jax.ShapeDtypeStruct((m, n), dt) for dt in out_dtypes]
    if save_a:
        out_specs.append(pl.BlockSpec((tm, k), lambda i, j: (i, 0)))
        out_shape.append(jax.ShapeDtypeStruct((m, k), MM))
    scratch = [pltpu.VMEM((tm, k), MM)] if use_scr else []
    return pl.pallas_call(body, name=name, grid=(m // tm, n // tn), in_specs=in_specs,
                          out_specs=out_specs, out_shape=out_shape, scratch_shapes=scratch,
                          compiler_params=_params(2))(*args)


def _tnmm(name, a, b, *, tm=1024, tn=1024, tk=512, col_shards=False):
    t, m = a.shape
    n = b.shape[1]
    tm, tk = _tile(m, tm), _tile(t, tk, 8)
    if col_shards:
        wb = n // N_DEV
        tn = _tile(wb, tn)
        out_spec = pl.BlockSpec((None, tm, tn), lambda i, j, kk, q=wb // tn: (j // q, i, j % q))
        out_shape = jax.ShapeDtypeStruct((N_DEV, m, wb), MM)
    else:
        tn = _tile(n, tn)
        out_spec = pl.BlockSpec((tm, tn), lambda i, j, kk: (i, j))
        out_shape = jax.ShapeDtypeStruct((m, n), MM)
    nk = t // tk

    def body(a_ref, b_ref, o_ref, acc):
        kk = pl.program_id(2)

        @pl.when(kk == 0)
        def _():
            acc[...] = jnp.zeros_like(acc)

        acc[...] += _dot_tn(a_ref[...].astype(MM), b_ref[...].astype(MM))

        @pl.when(kk == nk - 1)
        def _():
            o_ref[...] = acc[...].astype(o_ref.dtype)

    return pl.pallas_call(
        body, name=name, grid=(m // tm, n // tn, nk),
        in_specs=[pl.BlockSpec((tk, tm), lambda i, j, kk: (kk, i)),
                  pl.BlockSpec((tk, tn), lambda i, j, kk: (kk, j))],
        out_specs=out_spec, out_shape=out_shape,
        scratch_shapes=[pltpu.VMEM((tm, tn), F32)],
        compiler_params=_params(3))(a, b)


def _rms_prologue(x, g):
    r = lax.rsqrt(jnp.mean(x * x, axis=-1, keepdims=True) + EPS)
    return x * r * g


def _rms_bwd_rows(dh, x, g):
    d = x.shape[-1]
    r = lax.rsqrt(jnp.mean(x * x, axis=-1, keepdims=True) + EPS)
    xh = x * r
    dxh = dh * g
    dx = r * (dxh - xh * (jnp.sum(dxh * xh, axis=-1, keepdims=True) / d))
    dg = jnp.sum(dh * xh, axis=0, keepdims=True)
    return dx, dg


def _rms_bwd(name, dh, x, g, dres, *, tm=256):
    m, d = x.shape
    tm = min(tm, m)
    has_res = dres is not None

    def body(*refs):
        if has_res:
            dh_ref, x_ref, g_ref, r_ref, dx_ref, dg_ref = refs
        else:
            dh_ref, x_ref, g_ref, dx_ref, dg_ref = refs
        dx, dg = _rms_bwd_rows(dh_ref[...], x_ref[...], g_ref[...])
        if has_res:
            dx = dx + r_ref[...]
        dx_ref[...] = dx

        @pl.when(pl.program_id(0) == 0)
        def _():
            dg_ref[...] = jnp.zeros_like(dg_ref)

        dg_ref[...] += dg

    row = pl.BlockSpec((tm, d), lambda i: (i, 0))
    vec = pl.BlockSpec((1, d), lambda i: (0, 0))
    in_specs = [row, row, vec] + ([row] if has_res else [])
    args = [dh, x, g] + ([dres] if has_res else [])
    return pl.pallas_call(body, name=name, grid=(m // tm,), in_specs=in_specs, out_specs=[row, vec],
                          out_shape=[jax.ShapeDtypeStruct((m, d), F32), jax.ShapeDtypeStruct((1, d), F32)],
                          compiler_params=_params(1))(*args)


def _loss_head(x, target, g, *, tm=256):
    m, d = x.shape
    tm = min(tm, m)

    def body(x_ref, t_ref, g_ref, dx_ref, dg_ref, loss_ref):
        xv, gv = x_ref[...], g_ref[...]
        y = _rms_prologue(xv, gv)
        err = y - t_ref[...]
        part = 0.5 * jnp.sum(jnp.sum(err * err, axis=-1, keepdims=True) / d)
        dx, dg = _rms_bwd_rows(err / d, xv, gv)
        dx_ref[...] = dx

        @pl.when(pl.program_id(0) == 0)
        def _():
            dg_ref[...] = jnp.zeros_like(dg_ref)
            loss_ref[...] = jnp.zeros_like(loss_ref)

        dg_ref[...] += dg
        loss_ref[...] += jnp.full(loss_ref.shape, part, F32)

    row = pl.BlockSpec((tm, d), lambda i: (i, 0))
    vec = pl.BlockSpec((1, d), lambda i: (0, 0))
    lspec = pl.BlockSpec((1, 128), lambda i: (0, 0))
    return pl.pallas_call(body, name="loss_head", grid=(m // tm,), in_specs=[row, row, vec],
                          out_specs=[row, vec, lspec],
                          out_shape=[jax.ShapeDtypeStruct((m, d), F32), jax.ShapeDtypeStruct((1, d), F32),
                                     jax.ShapeDtypeStruct((1, 128), F32)],
                          compiler_params=_params(1))(x, target, g)


def _rot(xv, cos2, sin2, half):
    return xv * cos2 + pltpu.roll(xv, half, 1) * sin2


def _rot_t(dv, cos2, sin2, half):
    return dv * cos2 + pltpu.roll(dv * sin2, half, 1)


def _ret_pre(proj, cos2, sin2, d, seq, *, ts=512):
    t = proj.shape[0]
    ts = min(ts, seq)
    dk = d // 8
    ns = seq // ts
    scale = float(dk) ** -0.5

    def body(q_ref, k_ref, v_ref, c_ref, s_ref, qo, ko, vo):
        c, s = c_ref[...], s_ref[...]
        for h in range(HEADS):
            sl = slice(h * dk, (h + 1) * dk)
            qo[:, sl] = _rot(q_ref[:, sl].astype(F32), c, s, dk // 2).astype(MM)
            ko[:, sl] = (_rot(k_ref[:, sl].astype(F32), c, s, dk // 2) * scale).astype(MM)
        vo[...] = v_ref[...].astype(MM)

    half = pl.BlockSpec((ts, d // 2), lambda i: (i, 0))
    tab = pl.BlockSpec((ts, dk), lambda i: (i % ns, 0))
    return pl.pallas_call(
        body, name="ret_pre", grid=(t // ts,),
        in_specs=[half, pl.BlockSpec((ts, d // 2), lambda i: (i, 1)), pl.BlockSpec((ts, d), lambda i: (i, 1)),
                  tab, tab],
        out_specs=[half, half, pl.BlockSpec((ts, d), lambda i: (i, 0))],
        out_shape=[jax.ShapeDtypeStruct((t, d // 2), MM), jax.ShapeDtypeStruct((t, d // 2), MM),
                   jax.ShapeDtypeStruct((t, d), MM)],
        compiler_params=_params(1))(proj, proj, proj, cos2, sin2)


def _ret_consts(lg_ref, h, t, dk):
    lf, lb = lg_ref[0, h], lg_ref[1, h]
    ab = (lax.broadcasted_iota(jnp.int32, (t, t), 0) - lax.broadcasted_iota(jnp.int32, (t, t), 1)).astype(F32)
    dmat = jnp.exp(jnp.where(ab >= 0, lf * ab, -lb * ab))
    up = lax.broadcasted_iota(jnp.int32, (t, dk), 0).astype(F32) + 1.0
    down = float(t) - up
    one = jnp.ones((1, 1), F32)
    return dict(ab=ab, dmat=dmat, xi_f=jnp.exp(lf * up), zeta_f=jnp.exp(lf * down), xi_b=jnp.exp(lb * up),
                zeta_b=jnp.exp(lb * down), up=up[:, 0:1], down=down[:, 0:1],
                cf=jnp.exp(one * (lf * t)), cb=jnp.exp(one * (lb * t)))


def _scaled(xv, rows):
    return (xv.astype(F32) * rows).astype(MM)


def _ret_core_fwd(qr, kr, vb, proj, lg, d, bl, seq, *, tc=256):
    t = qr.shape[0]
    dk, dv = d // 8, d // 4
    tc = min(tc, seq)
    nc = seq // tc

    def body(lg_ref, q_ref, k_ref, v_ref, g_ref, o_ref, a_ref):
        c = _ret_consts(lg_ref, pl.program_id(1), tc, dk)

        def rows_of(i):
            return pl.ds(pl.multiple_of(i * tc, tc), tc)

        def fwd_step(i, sf):
            rows = rows_of(i)
            q, kk, v = q_ref[rows, :], k_ref[rows, :], v_ref[rows, :]
            p = (_dot_nt(q, kk) * c["dmat"]).astype(MM)
            o_ref[rows, :] = _dot(p, v) + _dot(_scaled(q, c["xi_f"]), sf.astype(MM))
            return sf * c["cf"] + _dot_tn(_scaled(kk, c["zeta_f"]), v)

        lax.fori_loop(0, nc, fwd_step, jnp.zeros((dk, dv), F32))

        def bwd_step(ii, sb):
            rows = rows_of(nc - 1 - ii)
            q, kk, v = q_ref[rows, :], k_ref[rows, :], v_ref[rows, :]
            o_ref[rows, :] += _dot(_scaled(q, c["zeta_b"]), sb.astype(MM))
            return sb * c["cb"] + _dot_tn(_scaled(kk, c["xi_b"]), v)

        lax.fori_loop(0, nc, bwd_step, jnp.zeros((dk, dv), F32))

        def post(i, carry):
            rows = rows_of(i)
            o = o_ref[rows, :]
            oc = o - jnp.mean(o, axis=-1, keepdims=True)
            on = oc * lax.rsqrt(jnp.mean(oc * oc, axis=-1, keepdims=True) + EPS)
            g = g_ref[rows, :].astype(F32)
            a_ref[rows, :] = (on * (g * _sigmoid(g))).astype(MM)
            return carry

        lax.fori_loop(0, nc, post, 0)

    qk = pl.BlockSpec((seq, dk), lambda b, h: (b, h))
    vv = pl.BlockSpec((seq, dv), lambda b, h: (b, h))
    return pl.pallas_call(
        body, name="ret_core_fwd", grid=(bl, HEADS),
        in_specs=[pl.BlockSpec(memory_space=pltpu.SMEM), qk, qk, vv,
                  pl.BlockSpec((seq, dv), lambda b, h: (b, 2 * HEADS + h))],
        out_specs=[vv, vv],
        out_shape=[jax.ShapeDtypeStruct((t, d), F32), jax.ShapeDtypeStruct((t, d), MM)],
        compiler_params=_params(2))(lg, qr, kr, vb, proj)


def _ret_post_bwd(da, proj, o_raw, d, *, ts=512):
    t = da.shape[0]
    dv = d // 4
    ts = min(ts, t)

    def body(da_ref, g_ref, o_ref, dg_ref, do_ref):
        o, g, dav = o_ref[...], g_ref[...].astype(F32), da_ref[...]
        mu = jnp.mean(o, axis=-1, keepdims=True)
        oc = o - mu
        r = lax.rsqrt(jnp.mean(oc * oc, axis=-1, keepdims=True) + EPS)
        on = oc * r
        sg = _sigmoid(g)
        don = dav * (g * sg)
        dg_ref[...] = (dav * on * (sg * (1.0 + g * (1.0 - sg)))).astype(MM)
        do = r * (don - jnp.mean(don, axis=-1, keepdims=True) - on * jnp.mean(don * on, axis=-1, keepdims=True))
        do_ref[...] = do.astype(MM)

    blk = pl.BlockSpec((ts, dv), lambda i, h: (i, h))
    return pl.pallas_call(
        body, name="ret_post_bwd", grid=(t // ts, HEADS),
        in_specs=[blk, pl.BlockSpec((ts, dv), lambda i, h: (i, 2 * HEADS + h)), blk],
        out_specs=[blk, blk],
        out_shape=[jax.ShapeDtypeStruct((t, d), MM), jax.ShapeDtypeStruct((t, d), MM)],
        compiler_params=_params(2))(da, proj, o_raw)


def _ret_core_bwd(qr, kr, vb, do, cos2, sin2, lg, d, bl, seq, *, tc=256):
    t = qr.shape[0]
    dk, dv = d // 8, d // 4
    tc = min(tc, seq)
    nc = seq // tc
    scale = float(dk) ** -0.5

    def body(lg_ref, q_ref, k_ref, v_ref, do_ref, c_ref, s_ref, dq_ref, dk_ref, dv_ref, dlf_ref, dlb_ref,
             dq_acc, dk_acc, dv_acc, sf_all, sb_all):
        c = _ret_consts(lg_ref, pl.program_id(1), tc, dk)
        fwd = c["ab"] >= 0
        zero_state = jnp.zeros((dk, dv), F32)
        zero = jnp.zeros((1, 1), F32)

        def rows_of(i):
            return pl.ds(pl.multiple_of(i * tc, tc), tc)

        def total(xv):
            return jnp.sum(xv, keepdims=True)

        def sf_pass(i, sf):
            rows = rows_of(i)
            sf_all[i] = sf
            return sf * c["cf"] + _dot_tn(_scaled(k_ref[rows, :], c["zeta_f"]), v_ref[rows, :])

        lax.fori_loop(0, nc, sf_pass, zero_state)

        def sb_pass(ii, sb):
            i = nc - 1 - ii
            rows = rows_of(i)
            sb_all[i] = sb
            return sb * c["cb"] + _dot_tn(_scaled(k_ref[rows, :], c["xi_b"]), v_ref[rows, :])

        lax.fori_loop(0, nc, sb_pass, zero_state)

        def fwd_sweep(i, carry):
            hh, dlf, dlb = carry
            rows = rows_of(i)
            q, kk, v, dov = q_ref[rows, :], k_ref[rows, :], v_ref[rows, :], do_ref[rows, :]
            dof, vf = dov.astype(F32), v.astype(F32)
            p = _dot_nt(q, kk) * c["dmat"]
            da = _dot_nt(dov, v)
            x = p * da * c["ab"]
            dlf = dlf + total(jnp.where(fwd, x, 0.0))
            dlb = dlb - total(jnp.where(fwd, 0.0, x))
            pb, dpb = p.astype(MM), (da * c["dmat"]).astype(MM)
            dq = _dot(dpb, kk)
            dkc = _dot_tn(dpb, q)
            dvc = _dot_tn(pb, dov)
            sf, sb = sf_all[i], sb_all[i]
            sfb, sbb = sf.astype(MM), sb.astype(MM)
            q_xf, q_zb = _scaled(q, c["xi_f"]), _scaled(q, c["zeta_b"])
            dq = dq + _dot_nt(dov, sfb) * c["xi_f"] + _dot_nt(dov, sbb) * c["zeta_b"]
            dlf = dlf + total(jnp.sum(_dot(q_xf, sfb) * dof, axis=-1, keepdims=True) * c["up"])
            dlb = dlb + total(jnp.sum(_dot(q_zb, sbb) * dof, axis=-1, keepdims=True) * c["down"])
            hb = hh.astype(MM)
            dkc = dkc + _dot_nt(v, hb) * c["xi_b"]
            dv_bx = _dot(_scaled(kk, c["xi_b"]), hb)
            dlb = dlb + total(jnp.sum(vf * dv_bx, axis=-1, keepdims=True) * c["up"])
            dlb = dlb + float(tc) * total(hh * (sb * c["cb"]))
            dq_acc[rows, :] = dq
            dk_acc[rows, :] = dkc
            dv_acc[rows, :] = dvc + dv_bx
            return hh * c["cb"] + _dot_tn(q_zb, dov), dlf, dlb

        _, dlf, dlb = lax.fori_loop(0, nc, fwd_sweep, (zero_state, zero, zero))

        def rev_sweep(ii, carry):
            gg, dlf = carry
            i = nc - 1 - ii
            rows = rows_of(i)
            q, kk, v, dov = q_ref[rows, :], k_ref[rows, :], v_ref[rows, :], do_ref[rows, :]
            gb = gg.astype(MM)
            dk_acc[rows, :] += _dot_nt(v, gb) * c["zeta_f"]
            dv_fx = _dot(_scaled(kk, c["zeta_f"]), gb)
            dv_acc[rows, :] += dv_fx
            dlf = dlf + total(jnp.sum(v.astype(F32) * dv_fx, axis=-1, keepdims=True) * c["down"])
            dlf = dlf + float(tc) * total(gg * (sf_all[i] * c["cf"]))
            return gg * c["cf"] + _dot_tn(_scaled(q, c["xi_f"]), dov), dlf

        _, dlf = lax.fori_loop(0, nc, rev_sweep, (zero_state, dlf))

        cs, sn = c_ref[...], s_ref[...]
        dq_ref[...] = _rot_t(dq_acc[...], cs, sn, dk // 2).astype(MM)
        dk_ref[...] = (_rot_t(dk_acc[...], cs, sn, dk // 2) * scale).astype(MM)
        dv_ref[...] = dv_acc[...].astype(MM)
        dlf_ref[...] = jnp.broadcast_to(dlf, dlf_ref.shape)
        dlb_ref[...] = jnp.broadcast_to(dlb, dlb_ref.shape)

    qk = pl.BlockSpec((seq, dk), lambda b, h: (b, h))
    vv = pl.BlockSpec((seq, dv), lambda b, h: (b, h))
    tab = pl.BlockSpec((seq, dk), lambda b, h: (0, 0))
    dl = pl.BlockSpec((None, 8, 128), lambda b, h: (b * HEADS + h, 0, 0))
    return pl.pallas_call(
        body, name="ret_core_bwd", grid=(bl, HEADS),
        in_specs=[pl.BlockSpec(memory_space=pltpu.SMEM), qk, qk, vv, vv, tab, tab],
        out_specs=[qk, qk, vv, dl, dl],
        out_shape=[jax.ShapeDtypeStruct((t, d // 2), MM), jax.ShapeDtypeStruct((t, d // 2), MM),
                   jax.ShapeDtypeStruct((t, d), MM),
                   jax.ShapeDtypeStruct((bl * HEADS, 8, 128), F32), jax.ShapeDtypeStruct((bl * HEADS, 8, 128), F32)],
        scratch_shapes=[pltpu.VMEM((seq, dk), F32), pltpu.VMEM((seq, dk), F32), pltpu.VMEM((seq, dv), F32),
                        pltpu.VMEM((nc, dk, dv), F32), pltpu.VMEM((nc, dk, dv), F32)],
        compiler_params=_params(2))(lg, qr, kr, vb, do, cos2, sin2)


def _window_count(row, w, seq):
    return (jnp.minimum(row + w // 2, seq) - jnp.maximum(row - w // 2, 0)).astype(F32)


def _window_sum(pv, row, w, seq, sign):
    acc = None
    for j in range(-(w // 2), w // 2):
        if j == 0:
            term = pv
        else:
            src = row + sign * j
            term = jnp.where((src >= 0) & (src < seq), pltpu.roll(pv, (-sign * j) % seq, 0), 0.0)
        acc = term if acc is None else acc + term
    return acc


def _pool_fwd(proj, w_grp, scale, d, bl, seq):
    t = proj.shape[0]
    dg = d // 8

    def body(p_ref, w_ref, s_ref, y_ref):
        row = lax.broadcasted_iota(jnp.int32, (seq, dg), 0)
        for gi, w in enumerate(POOL_WINDOWS):
            sl = slice(gi * dg, (gi + 1) * dg)
            pg = p_ref[:, sl].astype(F32)
            mixed = _window_sum(pg, row, w, seq, 1) / _window_count(row, w, seq) - pg
            yp = _dot(mixed.astype(MM), w_ref[gi].astype(MM))
            y_ref[:, sl] = (yp * s_ref[:, sl]).astype(MM)

    return pl.pallas_call(
        body, name="pool_fwd", grid=(bl,),
        in_specs=[pl.BlockSpec((seq, d // 2), lambda b: (b, 6)),
                  pl.BlockSpec(w_grp.shape, lambda b: (0, 0, 0)),
                  pl.BlockSpec((1, d // 2), lambda b: (0, 0))],
        out_specs=pl.BlockSpec((seq, d // 2), lambda b: (b, 0)),
        out_shape=jax.ShapeDtypeStruct((t, d // 2), MM),
        compiler_params=_params(1))(proj, w_grp, scale)


def _pool_bwd(proj, dy, w_grp, scale, d, bl, seq):
    t = proj.shape[0]
    dg = d // 8

    def body(p_ref, dy_ref, w_ref, s_ref, dp_ref, dw_ref, ds_ref):
        @pl.when(pl.program_id(0) == 0)
        def _():
            dw_ref[...] = jnp.zeros_like(dw_ref)
            ds_ref[...] = jnp.zeros_like(ds_ref)

        row = lax.broadcasted_iota(jnp.int32, (seq, dg), 0)
        for gi, w in enumerate(POOL_WINDOWS):
            sl = slice(gi * dg, (gi + 1) * dg)
            pg = p_ref[:, sl].astype(F32)
            cnt = _window_count(row, w, seq)
            mixb = (_window_sum(pg, row, w, seq, 1) / cnt - pg).astype(MM)
            wgb = w_ref[gi].astype(MM)
            yp = _dot(mixb, wgb)
            dyg = dy_ref[:, sl]
            ds_ref[:, sl] += jnp.sum(dyg * yp, axis=0, keepdims=True)
            dyp = (dyg * s_ref[:, sl]).astype(MM)
            dmixed = _dot_nt(dyp, wgb)
            dw_ref[gi] += _dot_tn(mixb, dyp)
            dp_ref[:, sl] = (_window_sum(dmixed / cnt, row, w, seq, -1) - dmixed).astype(MM)

    half = pl.BlockSpec((seq, d // 2), lambda b: (b, 0))
    wspec = pl.BlockSpec(w_grp.shape, lambda b: (0, 0, 0))
    sspec = pl.BlockSpec((1, d // 2), lambda b: (0, 0))
    return pl.pallas_call(
        body, name="pool_bwd", grid=(bl,),
        in_specs=[pl.BlockSpec((seq, d // 2), lambda b: (b, 6)), half, wspec, sspec],
        out_specs=[half, wspec, sspec],
        out_shape=[jax.ShapeDtypeStruct((t, d // 2), MM), jax.ShapeDtypeStruct(w_grp.shape, F32),
                   jax.ShapeDtypeStruct((1, d // 2), F32)],
        compiler_params=_params(1))(proj, dy, w_grp, scale)


def _attn_probs(q, kk, dh):
    s = _dot_nt(q, kk) * (float(dh) ** -0.5)
    e = jnp.exp(s - jnp.max(s, axis=-1, keepdims=True))
    return e / jnp.sum(e, axis=-1, keepdims=True)


def _attn_fwd(proj, kv, d, bl, seq, mlen, *, tq=512):
    t = proj.shape[0]
    dh = d // 8
    tq = min(tq, seq)
    nq = seq // tq

    def body(q_ref, k_ref, v_ref, o_ref):
        a = _attn_probs(q_ref[...].astype(MM), k_ref[...].astype(MM), dh)
        o_ref[...] = _dot(a.astype(MM), v_ref[...].astype(MM)).astype(MM)

    return pl.pallas_call(
        body, name="attn_fwd", grid=(bl, HEADS, nq),
        in_specs=[pl.BlockSpec((tq, dh), lambda b, h, i: (b * nq + i, 7 * HEADS + h)),
                  pl.BlockSpec((mlen, dh), lambda b, h, i: (b, h)),
                  pl.BlockSpec((mlen, dh), lambda b, h, i: (b, HEADS + h))],
        out_specs=pl.BlockSpec((tq, dh), lambda b, h, i: (b * nq + i, h)),
        out_shape=jax.ShapeDtypeStruct((t, d // 2), MM),
        compiler_params=_params(3))(proj, kv, kv)


def _attn_bwd(proj, kv, do, d, bl, seq, mlen, *, tq=512):
    t = proj.shape[0]
    dh = d // 8
    tq = min(tq, seq)
    nq = seq // tq

    def body(q_ref, k_ref, v_ref, do_ref, dq_ref, dk_ref, dv_ref):
        @pl.when(pl.program_id(2) == 0)
        def _():
            dk_ref[...] = jnp.zeros_like(dk_ref)
            dv_ref[...] = jnp.zeros_like(dv_ref)

        q, kk, vv = q_ref[...].astype(MM), k_ref[...].astype(MM), v_ref[...].astype(MM)
        dov = do_ref[...].astype(MM)
        a = _attn_probs(q, kk, dh)
        dp = _dot_nt(dov, vv)
        ds = (a * (dp - jnp.sum(dp * a, axis=-1, keepdims=True)) * (float(dh) ** -0.5)).astype(MM)
        dq_ref[...] = _dot(ds, kk).astype(MM)
        dk_ref[...] += _dot_tn(ds, q)
        dv_ref[...] += _dot_tn(a.astype(MM), dov)

    qs = pl.BlockSpec((tq, dh), lambda b, h, i: (b * nq + i, h))
    ms = pl.BlockSpec((mlen, dh), lambda b, h, i: (b, h))
    return pl.pallas_call(
        body, name="attn_bwd", grid=(bl, HEADS, nq),
        in_specs=[pl.BlockSpec((tq, dh), lambda b, h, i: (b * nq + i, 7 * HEADS + h)), ms,
                  pl.BlockSpec((mlen, dh), lambda b, h, i: (b, HEADS + h)), qs],
        out_specs=[qs, ms, ms],
        out_shape=[jax.ShapeDtypeStruct((t, d // 2), MM), jax.ShapeDtypeStruct((bl * mlen, d // 2), F32),
                   jax.ShapeDtypeStruct((bl * mlen, d // 2), F32)],
        compiler_params=_params(3))(proj, kv, kv, do)


def _comm_call(name, body, arrays, out_shapes):
    n = len(arrays)
    hbm = pl.BlockSpec(memory_space=pl.ANY)
    return pl.pallas_call(
        body, name=name, out_shape=out_shapes, in_specs=[hbm] * n, out_specs=[hbm] * n,
        scratch_shapes=[pltpu.SemaphoreType.DMA((7 * n,)), pltpu.SemaphoreType.DMA((7 * n,)),
                        pltpu.SemaphoreType.DMA((n,))],
    )(*arrays)


def _all_gather(name, shards):
    n = len(shards)

    def body(*refs):
        x_refs, out_refs = refs[:n], refs[n:2 * n]
        send_sems, recv_sems, local_sems = refs[2 * n:]
        x, y, c = lax.axis_index("x"), lax.axis_index("y"), lax.axis_index("c")
        me, sibling = (x, y, c), (x, y, 1 - c)
        chips = [(1 - x, y), (x, 1 - y), (1 - x, 1 - y)]

        def copy(o, k, block, to, src=None):
            slot = out_refs[o].at[4 * block[0] + 2 * block[1] + block[2]]
            return pltpu.make_async_remote_copy(
                src_ref=slot if src is None else src, dst_ref=slot, send_sem=send_sems.at[7 * o + k],
                recv_sem=recv_sems.at[7 * o + k], device_id=to, device_id_type=MESH)

        locals_, remotes = [], []
        for o in range(n):
            mine = pltpu.make_async_copy(x_refs[o], out_refs[o].at[4 * x + 2 * y + c], local_sems.at[o])
            mine.start()
            locals_.append(mine)
            first = [copy(o, 0, me, sibling, src=x_refs[o])]
            first += [copy(o, 1 + j, me, (*chip, c), src=x_refs[o]) for j, chip in enumerate(chips)]
            for cp in first:
                cp.start()
            remotes += first
        for o in range(n):
            for j, chip in enumerate(chips):
                copy(o, 1 + j, (*chip, c), me).wait_recv()
                passed = copy(o, 4 + j, (*chip, c), sibling)
                passed.start()
                remotes.append(passed)
        for o in range(n):
            copy(o, 0, sibling, me).wait_recv()
            for j, chip in enumerate(chips):
                copy(o, 4 + j, (*chip, 1 - c), me).wait_recv()
        for cp in remotes:
            cp.wait_send()
        for mine in locals_:
            mine.wait()

    outs = [jax.ShapeDtypeStruct((N_DEV,) + s.shape, s.dtype) for s in shards]
    return _comm_call(name, body, shards, outs)


def _all_to_all(name, sends):
    n = len(sends)

    def body(*refs):
        s_refs, r_refs = refs[:n], refs[n:2 * n]
        send_sems, recv_sems, local_sems = refs[2 * n:]
        x, y, c = lax.axis_index("x"), lax.axis_index("y"), lax.axis_index("c")
        me_idx = 4 * x + 2 * y + c
        locals_, outs, arrivals = [], [], []
        for o in range(n):
            mine = pltpu.make_async_copy(s_refs[o].at[me_idx], r_refs[o].at[me_idx], local_sems.at[o])
            mine.start()
            locals_.append(mine)
            for k in range(1, N_DEV):
                peer = (1 - x if k & 4 else x, 1 - y if k & 2 else y, 1 - c if k & 1 else c)
                p_idx = 4 * peer[0] + 2 * peer[1] + peer[2]
                sems = dict(send_sem=send_sems.at[7 * o + k - 1], recv_sem=recv_sems.at[7 * o + k - 1],
                            device_id=peer, device_id_type=MESH)
                out = pltpu.make_async_remote_copy(src_ref=s_refs[o].at[p_idx], dst_ref=r_refs[o].at[me_idx], **sems)
                out.start()
                outs.append(out)
                arrivals.append(pltpu.make_async_remote_copy(src_ref=s_refs[o].at[p_idx],
                                                             dst_ref=r_refs[o].at[p_idx], **sems))
        for arrival in arrivals:
            arrival.wait_recv()
        for out in outs:
            out.wait_send()
        for mine in locals_:
            mine.wait()

    outs = [jax.ShapeDtypeStruct(s.shape, s.dtype) for s in sends]
    return _comm_call(name, body, sends, outs)


def _adamw(name, parts, w, m, v, *, tr=256):
    depth, a, b = w.shape
    tr = _tile(a, tr, 8)
    c1 = 1.0 - ADAM_B1 ** ADAM_STEP
    c2 = 1.0 - ADAM_B2 ** ADAM_STEP

    def body(p_ref, w_ref, m_ref, v_ref, g_out, d_out, m_out, v_out):
        g = p_ref[0].astype(F32)
        for s in range(1, N_DEV):
            g = g + p_ref[s].astype(F32)
        mn = ADAM_B1 * m_ref[...] + (1.0 - ADAM_B1) * g
        vn = ADAM_B2 * v_ref[...] + (1.0 - ADAM_B2) * (g * g)
        g_out[...] = g
        m_out[...] = mn
        v_out[...] = vn
        d_out[...] = -ADAM_LR * ((mn / c1) / (jnp.sqrt(vn / c2) + ADAM_EPS) + ADAM_WD * w_ref[...])

    row = pl.BlockSpec((None, tr, b), lambda l, i: (l, i, 0))
    return pl.pallas_call(
        body, name=name, grid=(depth, a // tr),
        in_specs=[pl.BlockSpec((None, N_DEV, tr, b), lambda l, i: (l, 0, i, 0)), row, row, row],
        out_specs=[row] * 4, out_shape=[jax.ShapeDtypeStruct((depth, a, b), F32)] * 4,
        compiler_params=_params(2))(parts, w, m, v)


_COL = ("w_in", "w_pool_o", "w_mem_o", "w_ff1")
_BIG = ("w_in", "w_ret_o", "w_pool_o", "w_mem_kv", "w_mem_o", "w_out", "w_ff1", "w_ff2")
_SMALL = ("ret_decay_logit", "w_pool_grp", "pool_scale", "norm1_g", "norm2_g", "mem_norm_g", "final_norm_g")
_WEIGHTS = ("w_in", "ret_decay_logit", "w_ret_o", "w_pool_grp", "pool_scale", "w_pool_o", "w_mem_kv", "w_mem_o",
            "w_out", "w_ff1", "w_ff2", "norm1_g", "norm2_g", "mem_norm_g", "final_norm_g")


def _pack_small(ws, d):
    flat = jnp.concatenate([ws[n].reshape(-1) for n in _SMALL])
    rows = -(-flat.shape[0] // (8 * d)) * 8
    return jnp.pad(flat, (0, rows * d - flat.shape[0])).reshape(1, rows, d)


def _unpack_small(packed, like):
    flat, out, off = packed.reshape(-1), {}, 0
    for n in _SMALL:
        out[n] = flat[off:off + like[n].size].reshape(like[n].shape)
        off += like[n].size
    return out


def kernel(x, mem, w_in, ret_decay_logit, w_ret_o, w_pool_grp, pool_scale, w_pool_o, w_mem_kv, w_mem_o, w_out, w_ff1, w_ff2, norm1_g, norm2_g, mem_norm_g, final_norm_g, loss_target, m_w_in, m_ret_decay_logit, m_w_ret_o, m_w_pool_grp, m_pool_scale, m_w_pool_o, m_w_mem_kv, m_w_mem_o, m_w_out, m_w_ff1, m_w_ff2, m_norm1_g, m_norm2_g, m_mem_norm_g, m_final_norm_g, v_w_in, v_ret_decay_logit, v_w_ret_o, v_w_pool_grp, v_pool_scale, v_w_pool_o, v_w_mem_kv, v_w_mem_o, v_w_out, v_w_ff1, v_w_ff2, v_norm1_g, v_norm2_g, v_mem_norm_g, v_final_norm_g):
    w = dict(w_in=w_in, ret_decay_logit=ret_decay_logit, w_ret_o=w_ret_o, w_pool_grp=w_pool_grp,
             pool_scale=pool_scale, w_pool_o=w_pool_o, w_mem_kv=w_mem_kv, w_mem_o=w_mem_o, w_out=w_out,
             w_ff1=w_ff1, w_ff2=w_ff2, norm1_g=norm1_g, norm2_g=norm2_g, mem_norm_g=mem_norm_g,
             final_norm_g=final_norm_g)
    mom = dict(w_in=m_w_in, ret_decay_logit=m_ret_decay_logit, w_ret_o=m_w_ret_o, w_pool_grp=m_w_pool_grp,
               pool_scale=m_pool_scale, w_pool_o=m_w_pool_o, w_mem_kv=m_w_mem_kv, w_mem_o=m_w_mem_o,
               w_out=m_w_out, w_ff1=m_w_ff1, w_ff2=m_w_ff2, norm1_g=m_norm1_g, norm2_g=m_norm2_g,
               mem_norm_g=m_mem_norm_g, final_norm_g=m_final_norm_g)
    vel = dict(w_in=v_w_in, ret_decay_logit=v_ret_decay_logit, w_ret_o=v_w_ret_o, w_pool_grp=v_w_pool_grp,
               pool_scale=v_pool_scale, w_pool_o=v_w_pool_o, w_mem_kv=v_w_mem_kv, w_mem_o=v_w_mem_o,
               w_out=v_w_out, w_ff1=v_w_ff1, w_ff2=v_w_ff2, norm1_g=v_norm1_g, norm2_g=v_norm2_g,
               mem_norm_g=v_mem_norm_g, final_norm_g=v_final_norm_g)

    bl, seq, d = x.shape
    mlen = mem.shape[1]
    depth = w_in.shape[0]
    t = bl * seq
    dk = d // 8

    full = []
    for l in range(depth):
        got = _all_gather("gather_weights", [w[n][l].astype(MM) for n in _BIG])
        full.append({n: g if n in _COL else g.reshape(-1, g.shape[-1]) for n, g in zip(_BIG, got)})

    inv = ROPE_BASE ** (-jnp.arange(0, dk, 2, dtype=F32) / dk)
    ang = jnp.arange(seq, dtype=F32)[:, None] * inv[None, :]
    cos2 = jnp.concatenate([jnp.cos(ang), jnp.cos(ang)], axis=-1)
    sin2 = jnp.concatenate([-jnp.sin(ang), jnp.sin(ang)], axis=-1)
    log_g = jax.nn.log_sigmoid(ret_decay_logit)
    x2 = x.reshape(t, d)
    mem2 = mem.reshape(bl * mlen, d)
    gmem = mem_norm_g.reshape(1, d)

    def merge(o_r, o_p, o_m, g_r, g_p, g_m):
        f = lambda z: z.astype(F32)
        return _sigmoid(f(g_r)) * f(o_r) + _sigmoid(f(g_p)) * f(o_p) + _sigmoid(f(g_m)) * f(o_m)

    def relu2(u):
        r = jnp.maximum(u, 0.0)
        return r * r

    def ident(a):
        return a

    saved = []
    xc = x2
    for l in range(depth):
        s = dict(x_in=xc)
        fw = full[l]
        g1 = norm1_g[l].reshape(1, d)
        g2 = norm2_g[l].reshape(1, d)
        s["proj"], s["h1"] = _pmm("proj", _rms_prologue, [(xc, d, 0)], [g1], fw["w_in"], w_mode="col",
                                  tm=1024, tn=1024, save_a=True, out_dtypes=(MM,))
        proj = s["proj"]
        s["qr"], s["kr"], s["vb"] = _ret_pre(proj, cos2, sin2, d, seq)
        s["o_raw"], s["a_ret"] = _ret_core_fwd(s["qr"], s["kr"], s["vb"], proj, log_g[l], d, bl, seq)
        s["y"] = _pool_fwd(proj, w_pool_grp[l], pool_scale[l].reshape(1, -1), d, bl, seq)
        s["kv"], s["memn"] = _pmm("mem_kv", _rms_prologue, [(mem2, d, 0)], [gmem], fw["w_mem_kv"],
                                  tm=512, tn=512, save_a=True)
        s["o_att"] = _attn_fwd(proj, s["kv"], d, bl, seq, mlen)
        (s["o_ret"],) = _pmm("ret_o", None, [(s["a_ret"], d, 0)], [], fw["w_ret_o"], tm=1024, tn=512,
                             out_dtypes=(MM,))
        (s["o_pool"],) = _pmm("pool_o", None, [(s["y"], d // 2, 0)], [], fw["w_pool_o"], w_mode="col",
                              tm=1024, tn=512, out_dtypes=(MM,))
        (s["o_mem"],) = _pmm("mem_o", None, [(s["o_att"], d // 2, 0)], [], fw["w_mem_o"], w_mode="col",
                             tm=1024, tn=512, out_dtypes=(MM,))
        s["x_mid"], s["merged"] = _pmm(
            "merge_out", merge,
            [(s["o_ret"], d, 0), (s["o_pool"], d, 0), (s["o_mem"], d, 0), (proj, d, 4), (proj, d, 5), (proj, d, 6)],
            [], fw["w_out"], tm=512, tn=512, residual=xc, save_a=True)
        s["u"], s["h2"] = _pmm("ff1", _rms_prologue, [(s["x_mid"], d, 0)], [g2], fw["w_ff1"], w_mode="col",
                               tm=1024, tn=512, save_a=True)
        xc, s["a"] = _pmm("ff2", relu2, [(s["u"], s["u"].shape[1], 0)], [], fw["w_ff2"],
                          tm=256, tn=512, residual=s["x_mid"], save_a=True)
        saved.append(s)

    dxc, g_final, loss_part = _loss_head(xc, loss_target.reshape(t, d), final_norm_g.reshape(1, d))
    loss = lax.psum(loss_part[0, 0], ("x", "y", "c"))

    small_names = ("w_pool_grp", "pool_scale", "norm1_g", "norm2_g", "ret_decay_logit")
    grads = {n: [None] * depth for n in small_names}
    recv = [None] * depth
    dmemn = jnp.zeros((bl * mlen, d), F32)

    def relu2_bwd(acc, u):
        return (acc * (2.0 * jnp.maximum(u, 0.0)),)

    def gates_bwd(acc, g_r, g_p, g_m, o_r, o_p, o_m):
        outs_o, outs_g = [], []
        for gz, oz in ((g_r, o_r), (g_p, o_p), (g_m, o_m)):
            sg = _sigmoid(gz.astype(F32))
            outs_o.append(acc * sg)
            outs_g.append(acc * oz.astype(F32) * (sg * (1.0 - sg)))
        return tuple(outs_o + outs_g)

    for l in reversed(range(depth)):
        s = saved[l]
        fw = full[l]
        proj = s["proj"]
        g1 = norm1_g[l].reshape(1, d)
        g2 = norm2_g[l].reshape(1, d)
        dw = {}
        (du,) = _pmm("ff2_bwd", ident, [(dxc, d, 0)], [], fw["w_ff2"], w_mode="nt", tm=512, tn=1024,
                     epilogue=relu2_bwd, epi_ins=[(s["u"], 0)], out_dtypes=(MM,))
        dw["w_ff2"] = _tnmm("dw_ff2", s["a"], dxc)
        (dh2,) = _pmm("ff1_bwd", None, [(du, du.shape[1], 0)], [], fw["w_ff1"], w_mode="col_t", tm=512, tn=512)
        dw["w_ff1"] = _tnmm("dw_ff1", s["h2"], du, col_shards=True)
        dmid, grads["norm2_g"][l] = _rms_bwd("norm2_bwd", dh2, s["x_mid"], g2, dxc)
        d_oret, d_opool, d_omem, dgr, dgp, dgm = _pmm(
            "out_bwd", ident, [(dmid, d, 0)], [], fw["w_out"], w_mode="nt", tm=512, tn=512, epilogue=gates_bwd,
            epi_ins=[(proj, 4 * d), (proj, 5 * d), (proj, 6 * d), (s["o_ret"], 0), (s["o_pool"], 0), (s["o_mem"], 0)],
            out_dtypes=(MM,) * 6)
        dw["w_out"] = _tnmm("dw_out", s["merged"], dmid)
        (da_ret,) = _pmm("ret_o_bwd", None, [(d_oret, d, 0)], [], fw["w_ret_o"], w_mode="nt", tm=1024, tn=512)
        dw["w_ret_o"] = _tnmm("dw_ret_o", s["a_ret"], d_oret)
        (dy,) = _pmm("pool_o_bwd", None, [(d_opool, d, 0)], [], fw["w_pool_o"], w_mode="col_t", tm=1024, tn=512)
        dw["w_pool_o"] = _tnmm("dw_pool_o", s["y"], d_opool, col_shards=True)
        (do_att,) = _pmm("mem_o_bwd", None, [(d_omem, d, 0)], [], fw["w_mem_o"], w_mode="col_t", tm=1024, tn=512)
        dw["w_mem_o"] = _tnmm("dw_mem_o", s["o_att"], d_omem, col_shards=True)
        dg_ret, do_ret = _ret_post_bwd(da_ret, proj, s["o_raw"], d)
        dq, dkk, dvv, dlf, dlb = _ret_core_bwd(s["qr"], s["kr"], s["vb"], do_ret, cos2, sin2, log_g[l], d, bl, seq)
        dl = jnp.stack([dlf[:, 0, 0].reshape(bl, HEADS).sum(0), dlb[:, 0, 0].reshape(bl, HEADS).sum(0)])
        grads["ret_decay_logit"][l] = dl * jax.nn.sigmoid(-ret_decay_logit[l])
        dp, grads["w_pool_grp"][l], dscale = _pool_bwd(proj, dy, w_pool_grp[l], pool_scale[l].reshape(1, -1),
                                                       d, bl, seq)
        grads["pool_scale"][l] = dscale.reshape(-1)
        dqm, dmk, dmv = _attn_bwd(proj, s["kv"], do_att, d, bl, seq, mlen)
        dkv = jnp.concatenate([dmk, dmv], axis=-1).astype(MM)
        dw["w_mem_kv"] = _tnmm("dw_mem_kv", s["memn"], dkv)
        (dmemn,) = _pmm("mem_kv_bwd", None, [(dkv, d, 0)], [], fw["w_mem_kv"], w_mode="nt", tm=512, tn=512,
                        residual=dmemn)
        dproj = jnp.concatenate([dq, dkk, dvv, dg_ret, dp, dqm, dgr, dgp, dgm], axis=-1)
        (dh1,) = _pmm("proj_bwd", None, [(dproj, dproj.shape[1], 0)], [], fw["w_in"], w_mode="col_t",
                      tm=512, tn=512)
        dw["w_in"] = _tnmm("dw_in", s["h1"], dproj, col_shards=True)
        dxc, grads["norm1_g"][l] = _rms_bwd("norm1_bwd", dh1, s["x_in"], g1, dmid)
        sends = [dw[n] if n in _COL else dw[n].reshape((N_DEV,) + w[n].shape[1:]) for n in _BIG]
        recv[l] = _all_to_all("scatter_grads", sends)

    _, g_memn = _rms_bwd("mem_norm_bwd", dmemn, mem2, gmem, None)
    grad_x = dxc.reshape(bl, seq, d)

    big = {}
    for i, n in enumerate(_BIG):
        parts = jnp.stack([recv[l][i] for l in range(depth)])
        big[n] = _adamw("adamw_" + n, parts, w[n], mom[n], vel[n])

    small_g = dict(ret_decay_logit=jnp.stack(grads["ret_decay_logit"]), w_pool_grp=jnp.stack(grads["w_pool_grp"]),
                   pool_scale=jnp.stack(grads["pool_scale"]),
                   norm1_g=jnp.concatenate(grads["norm1_g"], axis=0), norm2_g=jnp.concatenate(grads["norm2_g"], axis=0),
                   mem_norm_g=g_memn.reshape(-1), final_norm_g=g_final.reshape(-1))
    (small_parts,) = _all_gather("gather_small_grads", [_pack_small(small_g, d)[0]])
    small = _adamw("adamw_small", small_parts[None], _pack_small(w, d), _pack_small(mom, d), _pack_small(vel, d))
    small = [_unpack_small(o, w) for o in small]

    outs = [loss, grad_x]
    for k in range(4):
        outs += [big[n][k] if n in _BIG else small[k][n] for n in _WEIGHTS]
    return tuple(outs)
```

```python
import jax
import jax.numpy as jnp
from jax import lax
from jax.experimental import pallas as pl
from jax.experimental.pallas import tpu as pltpu

F32 = jnp.float32
MM = jnp.bfloat16
N_DEV = 8
HEADS = 4
POOL_WINDOWS = (2, 4, 8, 16)
EPS = 1e-6
ROPE_BASE = 10000.0
ADAM_LR, ADAM_B1, ADAM_B2, ADAM_EPS, ADAM_WD, ADAM_STEP = 0.001, 0.9, 0.999, 1e-08, 0.01, 10
V7X_VMEM_LIMIT = 56 * 1024 * 1024
MESH = pl.DeviceIdType.MESH


def _params(n_axes):
    return pltpu.CompilerParams(dimension_semantics=("arbitrary",) * n_axes,
                                vmem_limit_bytes=V7X_VMEM_LIMIT)


def _tile(n, pref, align=128):
    cands = [c for c in range(align, min(pref, n) + 1, align) if n % c == 0]
    return max(cands) if cands else n


def _sigmoid(z):
    return 1.0 / (1.0 + jnp.exp(-z))


def _dot(a, b):
    return jnp.dot(a, b, preferred_element_type=F32)


def _dot_nt(a, b):
    return lax.dot_general(a, b, (((1,), (1,)), ((), ())), preferred_element_type=F32)


def _dot_tn(a, b):
    return lax.dot_general(a, b, (((0,), (0,)), ((), ())), preferred_element_type=F32)


def _pmm(name, prologue, row_ins, vec_ins, w, *, tm, tn, w_mode="nn", residual=None, save_a=False,
         epilogue=None, epi_ins=(), out_dtypes=(F32,), after=()):
    m = row_ins[0][0].shape[0]
    wb = None
    if w_mode == "nn":
        k, n = w.shape
        tn = _tile(n, tn)
        w_spec = pl.BlockSpec((k, tn), lambda i, j: (0, j))
    elif w_mode == "nt":
        n, k = w.shape
        tn = _tile(n, tn)
        w_spec = pl.BlockSpec((tn, k), lambda i, j: (j, 0))
    elif w_mode == "col":
        _, k, wb = w.shape
        n = N_DEV * wb
        tn = _tile(wb, tn)
        w_spec = pl.BlockSpec((None, k, tn), lambda i, j, q=wb // tn: (j // q, 0, j % q))
    else:
        _, n, wb = w.shape
        k = N_DEV * wb
        tn = _tile(n, tn)
        w_spec = pl.BlockSpec((N_DEV, tn, wb), lambda i, j: (0, j, 0))
    tm = _tile(m, tm, 8)
    n_row, n_vec, n_epi, n_out = len(row_ins), len(vec_ins), len(epi_ins), len(out_dtypes)
    has_res = residual is not None
    use_scr = prologue is not None

    def body(*refs):
        row_refs = refs[:n_row]
        p = n_row
        vec_refs = refs[p:p + n_vec]
        p += n_vec
        w_ref = refs[p]
        p += 1
        res_ref = refs[p] if has_res else None
        p += int(has_res)
        epi_refs = refs[p:p + n_epi]
        p += n_epi + len(after)
        out_refs = refs[p:p + n_out]
        p += n_out
        a_out = refs[p] if save_a else None
        p += int(save_a)
        if use_scr:
            a_src = refs[p]

            @pl.when(pl.program_id(1) == 0)
            def _():
                a = prologue(*[r[...] for r in row_refs], *[v[...] for v in vec_refs]).astype(MM)
                a_src[...] = a
                if save_a:
                    a_out[...] = a
        else:
            a_src = row_refs[0]
        if w_mode == "nt":
            acc = _dot_nt(a_src[...], w_ref[...])
        elif w_mode == "col_t":
            acc = _dot_nt(a_src[:, 0:wb], w_ref[0])
            for dev in range(1, N_DEV):
                acc = acc + _dot_nt(a_src[:, dev * wb:(dev + 1) * wb], w_ref[dev])
        else:
            acc = _dot(a_src[...], w_ref[...])
        if has_res:
            acc = acc + res_ref[...]
        outs = epilogue(acc, *[e[...] for e in epi_refs]) if epilogue is not None else (acc,)
        for o_ref, o in zip(out_refs, outs):
            o_ref[...] = o.astype(o_ref.dtype)

    in_specs = [pl.BlockSpec((tm, wd), lambda i, j, cb=cb: (i, cb)) for (_, wd, cb) in row_ins]
    in_specs += [pl.BlockSpec(v.shape, lambda i, j: (0, 0)) for v in vec_ins]
    in_specs += [w_spec]
    args = [r[0] for r in row_ins] + list(vec_ins) + [w]
    if has_res:
        in_specs.append(pl.BlockSpec((tm, tn), lambda i, j: (i, j)))
        args.append(residual)
    for (arr, off) in epi_ins:
        assert off % tn == 0
        in_specs.append(pl.BlockSpec((tm, tn), lambda i, j, ob=off // tn: (i, ob + j)))
        args.append(arr)
    n_after = len(after)
    in_specs += [pl.BlockSpec(memory_space=pl.ANY)] * n_after
    args += list(after)
    out_specs = [pl.BlockSpec((tm, tn), lambda i, j: (i, j)) for _ in out_dtypes]
    out_shape = [jax.ShapeDtypeStruct((m, n), dt) for dt in out_dtypes]
    if save_a:
        out_specs.append(pl.BlockSpec((tm, k), lambda i, j: (i, 0)))
        out_shape.append(jax.ShapeDtypeStruct((m, k), MM))
    scratch = [pltpu.VMEM((tm, k), MM)] if use_scr else []
    return pl.pallas_call(body, name=name, grid=(m // tm, n // tn), in_specs=in_specs,
                          out_specs=out_specs, out_shape=out_shape, scratch_shapes=scratch,
                          compiler_params=_params(2))(*args)


def _tnmm(name, a, b, *, tm=1024, tn=1024, tk=512, col_shards=False):
    t, m = a.shape
    n = b.shape[1]
    tm, tk = _tile(m, tm), _tile(t, tk, 8)
    per_tile = 1
    if col_shards:
        wb = n // N_DEV
        while 2 * per_tile * wb <= tn and 2 * per_tile <= N_DEV:
            per_tile *= 2
        tn = per_tile * wb
        out_spec = pl.BlockSpec((per_tile, tm, wb), lambda i, j, kk: (j, i, 0))
        out_shape = jax.ShapeDtypeStruct((N_DEV, m, wb), MM)
    else:
        tn = _tile(n, tn)
        out_spec = pl.BlockSpec((tm, tn), lambda i, j, kk: (i, j))
        out_shape = jax.ShapeDtypeStruct((m, n), MM)
    nk = t // tk

    def body(a_ref, b_ref, o_ref, acc):
        kk = pl.program_id(2)

        @pl.when(kk == 0)
        def _():
            acc[...] = jnp.zeros_like(acc)

        acc[...] += _dot_tn(a_ref[...].astype(MM), b_ref[...].astype(MM))

        @pl.when(kk == nk - 1)
        def _():
            if col_shards:
                for sh in range(per_tile):
                    o_ref[sh] = acc[:, sh * wb:(sh + 1) * wb].astype(o_ref.dtype)
            else:
                o_ref[...] = acc[...].astype(o_ref.dtype)

    return pl.pallas_call(
        body, name=name, grid=(m // tm, n // tn, nk),
        in_specs=[pl.BlockSpec((tk, tm), lambda i, j, kk: (kk, i)),
                  pl.BlockSpec((tk, tn), lambda i, j, kk: (kk, j))],
        out_specs=out_spec, out_shape=out_shape,
        scratch_shapes=[pltpu.VMEM((tm, tn), F32)],
        compiler_params=_params(3))(a, b)


def _rms_prologue(x, g):
    r = lax.rsqrt(jnp.mean(x * x, axis=-1, keepdims=True) + EPS)
    return x * r * g


def _rms_bwd_rows(dh, x, g):
    d = x.shape[-1]
    r = lax.rsqrt(jnp.mean(x * x, axis=-1, keepdims=True) + EPS)
    xh = x * r
    dxh = dh * g
    dx = r * (dxh - xh * (jnp.sum(dxh * xh, axis=-1, keepdims=True) / d))
    dg = jnp.sum(dh * xh, axis=0, keepdims=True)
    return dx, dg


def _rms_bwd(name, dh, x, g, dres, *, tm=256):
    m, d = x.shape
    tm = min(tm, m)
    has_res = dres is not None

    def body(*refs):
        if has_res:
            dh_ref, x_ref, g_ref, r_ref, dx_ref, dg_ref = refs
        else:
            dh_ref, x_ref, g_ref, dx_ref, dg_ref = refs
        dx, dg = _rms_bwd_rows(dh_ref[...], x_ref[...], g_ref[...])
        if has_res:
            dx = dx + r_ref[...]
        dx_ref[...] = dx

        @pl.when(pl.program_id(0) == 0)
        def _():
            dg_ref[...] = jnp.zeros_like(dg_ref)

        dg_ref[...] += dg

    row = pl.BlockSpec((tm, d), lambda i: (i, 0))
    vec = pl.BlockSpec((1, d), lambda i: (0, 0))
    in_specs = [row, row, vec] + ([row] if has_res else [])
    args = [dh, x, g] + ([dres] if has_res else [])
    return pl.pallas_call(body, name=name, grid=(m // tm,), in_specs=in_specs, out_specs=[row, vec],
                          out_shape=[jax.ShapeDtypeStruct((m, d), F32), jax.ShapeDtypeStruct((1, d), F32)],
                          compiler_params=_params(1))(*args)


def _loss_head(x, target, g, *, tm=256):
    m, d = x.shape
    tm = min(tm, m)

    def body(x_ref, t_ref, g_ref, dx_ref, dg_ref, loss_ref):
        xv, gv = x_ref[...], g_ref[...]
        y = _rms_prologue(xv, gv)
        err = y - t_ref[...]
        part = 0.5 * jnp.sum(jnp.sum(err * err, axis=-1, keepdims=True) / d)
        dx, dg = _rms_bwd_rows(err / d, xv, gv)
        dx_ref[...] = dx

        @pl.when(pl.program_id(0) == 0)
        def _():
            dg_ref[...] = jnp.zeros_like(dg_ref)
            loss_ref[...] = jnp.zeros_like(loss_ref)

        dg_ref[...] += dg
        loss_ref[...] += jnp.full(loss_ref.shape, part, F32)

    row = pl.BlockSpec((tm, d), lambda i: (i, 0))
    vec = pl.BlockSpec((1, d), lambda i: (0, 0))
    lspec = pl.BlockSpec((1, 128), lambda i: (0, 0))
    return pl.pallas_call(body, name="loss_head", grid=(m // tm,), in_specs=[row, row, vec],
                          out_specs=[row, vec, lspec],
                          out_shape=[jax.ShapeDtypeStruct((m, d), F32), jax.ShapeDtypeStruct((1, d), F32),
                                     jax.ShapeDtypeStruct((1, 128), F32)],
                          compiler_params=_params(1))(x, target, g)


def _rot(xv, cos2, sin2, half):
    return xv * cos2 + pltpu.roll(xv, half, 1) * sin2


def _rot_t(dv, cos2, sin2, half):
    return dv * cos2 + pltpu.roll(dv * sin2, half, 1)


def _ret_pre(proj, cos2, sin2, d, seq, *, ts=512):
    t = proj.shape[0]
    ts = min(ts, seq)
    dk = d // 8
    ns = seq // ts
    scale = float(dk) ** -0.5

    def body(q_ref, k_ref, v_ref, c_ref, s_ref, qo, ko, vo):
        c, s = c_ref[...], s_ref[...]
        for h in range(HEADS):
            sl = slice(h * dk, (h + 1) * dk)
            qo[:, sl] = _rot(q_ref[:, sl].astype(F32), c, s, dk // 2).astype(MM)
            ko[:, sl] = (_rot(k_ref[:, sl].astype(F32), c, s, dk // 2) * scale).astype(MM)
        vo[...] = v_ref[...].astype(MM)

    half = pl.BlockSpec((ts, d // 2), lambda i: (i, 0))
    tab = pl.BlockSpec((ts, dk), lambda i: (i % ns, 0))
    return pl.pallas_call(
        body, name="ret_pre", grid=(t // ts,),
        in_specs=[half, pl.BlockSpec((ts, d // 2), lambda i: (i, 1)), pl.BlockSpec((ts, d), lambda i: (i, 1)),
                  tab, tab],
        out_specs=[half, half, pl.BlockSpec((ts, d), lambda i: (i, 0))],
        out_shape=[jax.ShapeDtypeStruct((t, d // 2), MM), jax.ShapeDtypeStruct((t, d // 2), MM),
                   jax.ShapeDtypeStruct((t, d), MM)],
        compiler_params=_params(1))(proj, proj, proj, cos2, sin2)


def _ret_consts(lg_ref, h, t, dk):
    lf, lb = lg_ref[0, h], lg_ref[1, h]
    ab = (lax.broadcasted_iota(jnp.int32, (t, t), 0) - lax.broadcasted_iota(jnp.int32, (t, t), 1)).astype(F32)
    dmat = jnp.exp(jnp.where(ab >= 0, lf * ab, -lb * ab))
    up = lax.broadcasted_iota(jnp.int32, (t, dk), 0).astype(F32) + 1.0
    down = float(t) - up
    one = jnp.ones((1, 1), F32)
    return dict(ab=ab, dmat=dmat, xi_f=jnp.exp(lf * up), zeta_f=jnp.exp(lf * down), xi_b=jnp.exp(lb * up),
                zeta_b=jnp.exp(lb * down), up=up[:, 0:1], down=down[:, 0:1],
                cf=jnp.exp(one * (lf * t)), cb=jnp.exp(one * (lb * t)))


def _scaled(xv, rows):
    return (xv.astype(F32) * rows).astype(MM)


def _ret_core_fwd(qr, kr, vb, proj, lg, d, bl, seq, *, tc=256):
    t = qr.shape[0]
    dk, dv = d // 8, d // 4
    tc = min(tc, seq)
    nc = seq // tc

    def body(lg_ref, q_ref, k_ref, v_ref, g_ref, o_ref, a_ref):
        c = _ret_consts(lg_ref, pl.program_id(1), tc, dk)

        def rows_of(i):
            return pl.ds(pl.multiple_of(i * tc, tc), tc)

        def fwd_step(i, sf):
            rows = rows_of(i)
            q, kk, v = q_ref[rows, :], k_ref[rows, :], v_ref[rows, :]
            p = (_dot_nt(q, kk) * c["dmat"]).astype(MM)
            o_ref[rows, :] = _dot(p, v) + _dot(_scaled(q, c["xi_f"]), sf.astype(MM))
            return sf * c["cf"] + _dot_tn(_scaled(kk, c["zeta_f"]), v)

        lax.fori_loop(0, nc, fwd_step, jnp.zeros((dk, dv), F32))

        def bwd_step(ii, sb):
            rows = rows_of(nc - 1 - ii)
            q, kk, v = q_ref[rows, :], k_ref[rows, :], v_ref[rows, :]
            o_ref[rows, :] += _dot(_scaled(q, c["zeta_b"]), sb.astype(MM))
            return sb * c["cb"] + _dot_tn(_scaled(kk, c["xi_b"]), v)

        lax.fori_loop(0, nc, bwd_step, jnp.zeros((dk, dv), F32))

        def post(i, carry):
            rows = rows_of(i)
            o = o_ref[rows, :]
            oc = o - jnp.mean(o, axis=-1, keepdims=True)
            on = oc * lax.rsqrt(jnp.mean(oc * oc, axis=-1, keepdims=True) + EPS)
            g = g_ref[rows, :].astype(F32)
            a_ref[rows, :] = (on * (g * _sigmoid(g))).astype(MM)
            return carry

        lax.fori_loop(0, nc, post, 0)

    qk = pl.BlockSpec((seq, dk), lambda b, h: (b, h))
    vv = pl.BlockSpec((seq, dv), lambda b, h: (b, h))
    return pl.pallas_call(
        body, name="ret_core_fwd", grid=(bl, HEADS),
        in_specs=[pl.BlockSpec(memory_space=pltpu.SMEM), qk, qk, vv,
                  pl.BlockSpec((seq, dv), lambda b, h: (b, 2 * HEADS + h))],
        out_specs=[vv, vv],
        out_shape=[jax.ShapeDtypeStruct((t, d), F32), jax.ShapeDtypeStruct((t, d), MM)],
        compiler_params=_params(2))(lg, qr, kr, vb, proj)


def _ret_post_bwd(da, proj, o_raw, d, *, ts=512):
    t = da.shape[0]
    dv = d // 4
    ts = min(ts, t)

    def body(da_ref, g_ref, o_ref, dg_ref, do_ref):
        o, g, dav = o_ref[...], g_ref[...].astype(F32), da_ref[...]
        mu = jnp.mean(o, axis=-1, keepdims=True)
        oc = o - mu
        r = lax.rsqrt(jnp.mean(oc * oc, axis=-1, keepdims=True) + EPS)
        on = oc * r
        sg = _sigmoid(g)
        don = dav * (g * sg)
        dg_ref[...] = (dav * on * (sg * (1.0 + g * (1.0 - sg)))).astype(MM)
        do = r * (don - jnp.mean(don, axis=-1, keepdims=True) - on * jnp.mean(don * on, axis=-1, keepdims=True))
        do_ref[...] = do.astype(MM)

    blk = pl.BlockSpec((ts, dv), lambda i, h: (i, h))
    return pl.pallas_call(
        body, name="ret_post_bwd", grid=(t // ts, HEADS),
        in_specs=[blk, pl.BlockSpec((ts, dv), lambda i, h: (i, 2 * HEADS + h)), blk],
        out_specs=[blk, blk],
        out_shape=[jax.ShapeDtypeStruct((t, d), MM), jax.ShapeDtypeStruct((t, d), MM)],
        compiler_params=_params(2))(da, proj, o_raw)


def _ret_core_bwd(qr, kr, vb, do, cos2, sin2, lg, d, bl, seq, *, tc=256):
    t = qr.shape[0]
    dk, dv = d // 8, d // 4
    tc = min(tc, seq)
    nc = seq // tc
    scale = float(dk) ** -0.5

    def body(lg_ref, q_ref, k_ref, v_ref, do_ref, c_ref, s_ref, dq_ref, dk_ref, dv_ref, dlf_ref, dlb_ref,
             dq_acc, dk_acc, dv_acc, sf_all, sb_all):
        c = _ret_consts(lg_ref, pl.program_id(1), tc, dk)
        fwd = c["ab"] >= 0
        zero_state = jnp.zeros((dk, dv), F32)
        zero = jnp.zeros((1, 1), F32)

        def rows_of(i):
            return pl.ds(pl.multiple_of(i * tc, tc), tc)

        def total(xv):
            return jnp.sum(xv, keepdims=True)

        def sf_pass(i, sf):
            rows = rows_of(i)
            sf_all[i] = sf
            return sf * c["cf"] + _dot_tn(_scaled(k_ref[rows, :], c["zeta_f"]), v_ref[rows, :])

        lax.fori_loop(0, nc, sf_pass, zero_state)

        def sb_pass(ii, sb):
            i = nc - 1 - ii
            rows = rows_of(i)
            sb_all[i] = sb
            return sb * c["cb"] + _dot_tn(_scaled(k_ref[rows, :], c["xi_b"]), v_ref[rows, :])

        lax.fori_loop(0, nc, sb_pass, zero_state)

        def fwd_sweep(i, carry):
            hh, dlf, dlb = carry
            rows = rows_of(i)
            q, kk, v, dov = q_ref[rows, :], k_ref[rows, :], v_ref[rows, :], do_ref[rows, :]
            dof, vf = dov.astype(F32), v.astype(F32)
            p = _dot_nt(q, kk) * c["dmat"]
            da = _dot_nt(dov, v)
            x = p * da * c["ab"]
            dlf = dlf + total(jnp.where(fwd, x, 0.0))
            dlb = dlb - total(jnp.where(fwd, 0.0, x))
            pb, dpb = p.astype(MM), (da * c["dmat"]).astype(MM)
            dq = _dot(dpb, kk)
            dkc = _dot_tn(dpb, q)
            dvc = _dot_tn(pb, dov)
            sf, sb = sf_all[i], sb_all[i]
            sfb, sbb = sf.astype(MM), sb.astype(MM)
            q_xf, q_zb = _scaled(q, c["xi_f"]), _scaled(q, c["zeta_b"])
            dq = dq + _dot_nt(dov, sfb) * c["xi_f"] + _dot_nt(dov, sbb) * c["zeta_b"]
            dlf = dlf + total(jnp.sum(_dot(q_xf, sfb) * dof, axis=-1, keepdims=True) * c["up"])
            dlb = dlb + total(jnp.sum(_dot(q_zb, sbb) * dof, axis=-1, keepdims=True) * c["down"])
            hb = hh.astype(MM)
            dkc = dkc + _dot_nt(v, hb) * c["xi_b"]
            dv_bx = _dot(_scaled(kk, c["xi_b"]), hb)
            dlb = dlb + total(jnp.sum(vf * dv_bx, axis=-1, keepdims=True) * c["up"])
            dlb = dlb + float(tc) * total(hh * (sb * c["cb"]))
            dq_acc[rows, :] = dq
            dk_acc[rows, :] = dkc
            dv_acc[rows, :] = dvc + dv_bx
            return hh * c["cb"] + _dot_tn(q_zb, dov), dlf, dlb

        _, dlf, dlb = lax.fori_loop(0, nc, fwd_sweep, (zero_state, zero, zero))

        def rev_sweep(ii, carry):
            gg, dlf = carry
            i = nc - 1 - ii
            rows = rows_of(i)
            q, kk, v, dov = q_ref[rows, :], k_ref[rows, :], v_ref[rows, :], do_ref[rows, :]
            gb = gg.astype(MM)
            dk_acc[rows, :] += _dot_nt(v, gb) * c["zeta_f"]
            dv_fx = _dot(_scaled(kk, c["zeta_f"]), gb)
            dv_acc[rows, :] += dv_fx
            dlf = dlf + total(jnp.sum(v.astype(F32) * dv_fx, axis=-1, keepdims=True) * c["down"])
            dlf = dlf + float(tc) * total(gg * (sf_all[i] * c["cf"]))
            return gg * c["cf"] + _dot_tn(_scaled(q, c["xi_f"]), dov), dlf

        _, dlf = lax.fori_loop(0, nc, rev_sweep, (zero_state, dlf))

        cs, sn = c_ref[...], s_ref[...]
        dq_ref[...] = _rot_t(dq_acc[...], cs, sn, dk // 2).astype(MM)
        dk_ref[...] = (_rot_t(dk_acc[...], cs, sn, dk // 2) * scale).astype(MM)
        dv_ref[...] = dv_acc[...].astype(MM)
        dlf_ref[...] = jnp.broadcast_to(dlf, dlf_ref.shape)
        dlb_ref[...] = jnp.broadcast_to(dlb, dlb_ref.shape)

    qk = pl.BlockSpec((seq, dk), lambda b, h: (b, h))
    vv = pl.BlockSpec((seq, dv), lambda b, h: (b, h))
    tab = pl.BlockSpec((seq, dk), lambda b, h: (0, 0))
    dl = pl.BlockSpec((None, 8, 128), lambda b, h: (b * HEADS + h, 0, 0))
    return pl.pallas_call(
        body, name="ret_core_bwd", grid=(bl, HEADS),
        in_specs=[pl.BlockSpec(memory_space=pltpu.SMEM), qk, qk, vv, vv, tab, tab],
        out_specs=[qk, qk, vv, dl, dl],
        out_shape=[jax.ShapeDtypeStruct((t, d // 2), MM), jax.ShapeDtypeStruct((t, d // 2), MM),
                   jax.ShapeDtypeStruct((t, d), MM),
                   jax.ShapeDtypeStruct((bl * HEADS, 8, 128), F32), jax.ShapeDtypeStruct((bl * HEADS, 8, 128), F32)],
        scratch_shapes=[pltpu.VMEM((seq, dk), F32), pltpu.VMEM((seq, dk), F32), pltpu.VMEM((seq, dv), F32),
                        pltpu.VMEM((nc, dk, dv), F32), pltpu.VMEM((nc, dk, dv), F32)],
        compiler_params=_params(2))(lg, qr, kr, vb, do, cos2, sin2)


def _window_count(row, w, seq):
    return (jnp.minimum(row + w // 2, seq) - jnp.maximum(row - w // 2, 0)).astype(F32)


def _window_sum(pv, row, w, seq, sign):
    acc = None
    for j in range(-(w // 2), w // 2):
        if j == 0:
            term = pv
        else:
            src = row + sign * j
            term = jnp.where((src >= 0) & (src < seq), pltpu.roll(pv, (-sign * j) % seq, 0), 0.0)
        acc = term if acc is None else acc + term
    return acc


def _pool_fwd(proj, w_grp, scale, d, bl, seq):
    t = proj.shape[0]
    dg = d // 8

    def body(p_ref, w_ref, s_ref, y_ref):
        row = lax.broadcasted_iota(jnp.int32, (seq, dg), 0)
        for gi, w in enumerate(POOL_WINDOWS):
            sl = slice(gi * dg, (gi + 1) * dg)
            pg = p_ref[:, sl].astype(F32)
            mixed = _window_sum(pg, row, w, seq, 1) / _window_count(row, w, seq) - pg
            yp = _dot(mixed.astype(MM), w_ref[gi].astype(MM))
            y_ref[:, sl] = (yp * s_ref[:, sl]).astype(MM)

    return pl.pallas_call(
        body, name="pool_fwd", grid=(bl,),
        in_specs=[pl.BlockSpec((seq, d // 2), lambda b: (b, 6)),
                  pl.BlockSpec(w_grp.shape, lambda b: (0, 0, 0)),
                  pl.BlockSpec((1, d // 2), lambda b: (0, 0))],
        out_specs=pl.BlockSpec((seq, d // 2), lambda b: (b, 0)),
        out_shape=jax.ShapeDtypeStruct((t, d // 2), MM),
        compiler_params=_params(1))(proj, w_grp, scale)


def _pool_bwd(proj, dy, w_grp, scale, d, bl, seq):
    t = proj.shape[0]
    dg = d // 8

    def body(p_ref, dy_ref, w_ref, s_ref, dp_ref, dw_ref, ds_ref):
        @pl.when(pl.program_id(0) == 0)
        def _():
            dw_ref[...] = jnp.zeros_like(dw_ref)
            ds_ref[...] = jnp.zeros_like(ds_ref)

        row = lax.broadcasted_iota(jnp.int32, (seq, dg), 0)
        for gi, w in enumerate(POOL_WINDOWS):
            sl = slice(gi * dg, (gi + 1) * dg)
            pg = p_ref[:, sl].astype(F32)
            cnt = _window_count(row, w, seq)
            mixb = (_window_sum(pg, row, w, seq, 1) / cnt - pg).astype(MM)
            wgb = w_ref[gi].astype(MM)
            yp = _dot(mixb, wgb)
            dyg = dy_ref[:, sl]
            ds_ref[:, sl] += jnp.sum(dyg * yp, axis=0, keepdims=True)
            dyp = (dyg * s_ref[:, sl]).astype(MM)
            dmixed = _dot_nt(dyp, wgb)
            dw_ref[gi] += _dot_tn(mixb, dyp)
            dp_ref[:, sl] = (_window_sum(dmixed / cnt, row, w, seq, -1) - dmixed).astype(MM)

    half = pl.BlockSpec((seq, d // 2), lambda b: (b, 0))
    wspec = pl.BlockSpec(w_grp.shape, lambda b: (0, 0, 0))
    sspec = pl.BlockSpec((1, d // 2), lambda b: (0, 0))
    return pl.pallas_call(
        body, name="pool_bwd", grid=(bl,),
        in_specs=[pl.BlockSpec((seq, d // 2), lambda b: (b, 6)), half, wspec, sspec],
        out_specs=[half, wspec, sspec],
        out_shape=[jax.ShapeDtypeStruct((t, d // 2), MM), jax.ShapeDtypeStruct(w_grp.shape, F32),
                   jax.ShapeDtypeStruct((1, d // 2), F32)],
        compiler_params=_params(1))(proj, dy, w_grp, scale)


def _attn_probs(q, kk, dh):
    s = _dot_nt(q, kk) * (float(dh) ** -0.5)
    e = jnp.exp(s - jnp.max(s, axis=-1, keepdims=True))
    return e / jnp.sum(e, axis=-1, keepdims=True)


def _attn_fwd(proj, kv, d, bl, seq, mlen, *, tq=512):
    t = proj.shape[0]
    dh = d // 8
    tq = min(tq, seq)
    nq = seq // tq

    def body(q_ref, k_ref, v_ref, o_ref):
        a = _attn_probs(q_ref[...].astype(MM), k_ref[...].astype(MM), dh)
        o_ref[...] = _dot(a.astype(MM), v_ref[...].astype(MM)).astype(MM)

    return pl.pallas_call(
        body, name="attn_fwd", grid=(bl, HEADS, nq),
        in_specs=[pl.BlockSpec((tq, dh), lambda b, h, i: (b * nq + i, 7 * HEADS + h)),
                  pl.BlockSpec((mlen, dh), lambda b, h, i: (b, h)),
                  pl.BlockSpec((mlen, dh), lambda b, h, i: (b, HEADS + h))],
        out_specs=pl.BlockSpec((tq, dh), lambda b, h, i: (b * nq + i, h)),
        out_shape=jax.ShapeDtypeStruct((t, d // 2), MM),
        compiler_params=_params(3))(proj, kv, kv)


def _attn_bwd(proj, kv, do, d, bl, seq, mlen, *, tq=512):
    t = proj.shape[0]
    dh = d // 8
    tq = min(tq, seq)
    nq = seq // tq

    def body(q_ref, k_ref, v_ref, do_ref, dq_ref, dk_ref, dv_ref):
        @pl.when(pl.program_id(2) == 0)
        def _():
            dk_ref[...] = jnp.zeros_like(dk_ref)
            dv_ref[...] = jnp.zeros_like(dv_ref)

        q, kk, vv = q_ref[...].astype(MM), k_ref[...].astype(MM), v_ref[...].astype(MM)
        dov = do_ref[...].astype(MM)
        a = _attn_probs(q, kk, dh)
        dp = _dot_nt(dov, vv)
        ds = (a * (dp - jnp.sum(dp * a, axis=-1, keepdims=True)) * (float(dh) ** -0.5)).astype(MM)
        dq_ref[...] = _dot(ds, kk).astype(MM)
        dk_ref[...] += _dot_tn(ds, q)
        dv_ref[...] += _dot_tn(a.astype(MM), dov)

    qs = pl.BlockSpec((tq, dh), lambda b, h, i: (b * nq + i, h))
    ms = pl.BlockSpec((mlen, dh), lambda b, h, i: (b, h))
    return pl.pallas_call(
        body, name="attn_bwd", grid=(bl, HEADS, nq),
        in_specs=[pl.BlockSpec((tq, dh), lambda b, h, i: (b * nq + i, 7 * HEADS + h)), ms,
                  pl.BlockSpec((mlen, dh), lambda b, h, i: (b, HEADS + h)), qs],
        out_specs=[qs, ms, ms],
        out_shape=[jax.ShapeDtypeStruct((t, d // 2), MM), jax.ShapeDtypeStruct((bl * mlen, d // 2), F32),
                   jax.ShapeDtypeStruct((bl * mlen, d // 2), F32)],
        compiler_params=_params(3))(proj, kv, kv, do)


def _comm_call(name, body, arrays, out_shapes):
    n = len(arrays)
    hbm = pl.BlockSpec(memory_space=pl.ANY)
    return pl.pallas_call(
        body, name=name, out_shape=out_shapes, in_specs=[hbm] * n, out_specs=[hbm] * n,
        scratch_shapes=[pltpu.SemaphoreType.DMA((7 * n,)), pltpu.SemaphoreType.DMA((7 * n,)),
                        pltpu.SemaphoreType.DMA((n,))],
    )(*arrays)


def _all_gather(name, shards):
    n = len(shards)

    def body(*refs):
        x_refs, out_refs = refs[:n], refs[n:2 * n]
        send_sems, recv_sems, local_sems = refs[2 * n:]
        x, y, c = lax.axis_index("x"), lax.axis_index("y"), lax.axis_index("c")
        me, sibling = (x, y, c), (x, y, 1 - c)
        chips = [(1 - x, y), (x, 1 - y), (1 - x, 1 - y)]

        def copy(o, k, block, to, src=None):
            slot = out_refs[o].at[4 * block[0] + 2 * block[1] + block[2]]
            return pltpu.make_async_remote_copy(
                src_ref=slot if src is None else src, dst_ref=slot, send_sem=send_sems.at[7 * o + k],
                recv_sem=recv_sems.at[7 * o + k], device_id=to, device_id_type=MESH)

        locals_, remotes = [], []
        for o in range(n):
            mine = pltpu.make_async_copy(x_refs[o], out_refs[o].at[4 * x + 2 * y + c], local_sems.at[o])
            mine.start()
            locals_.append(mine)
            first = [copy(o, 0, me, sibling, src=x_refs[o])]
            first += [copy(o, 1 + j, me, (*chip, c), src=x_refs[o]) for j, chip in enumerate(chips)]
            for cp in first:
                cp.start()
            remotes += first
        for o in range(n):
            for j, chip in enumerate(chips):
                copy(o, 1 + j, (*chip, c), me).wait_recv()
                passed = copy(o, 4 + j, (*chip, c), sibling)
                passed.start()
                remotes.append(passed)
        for o in range(n):
            copy(o, 0, sibling, me).wait_recv()
            for j, chip in enumerate(chips):
                copy(o, 4 + j, (*chip, 1 - c), me).wait_recv()
        for cp in remotes:
            cp.wait_send()
        for mine in locals_:
            mine.wait()

    outs = [jax.ShapeDtypeStruct((N_DEV,) + s.shape, s.dtype) for s in shards]
    return _comm_call(name, body, shards, outs)


def _peer_of(k, x, y, c):
    peer = (1 - x if k & 4 else x, 1 - y if k & 2 else y, 1 - c if k & 1 else c)
    return peer, 4 * peer[0] + 2 * peer[1] + peer[2]


def _split_copies(scatter, srcs, lands, send_sems, recv_sems, arriving):
    x, y, c = lax.axis_index("x"), lax.axis_index("y"), lax.axis_index("c")
    me_idx = 4 * x + 2 * y + c
    copies = []
    for o, (src, land) in enumerate(zip(srcs, lands)):
        for k in range(1, N_DEV):
            peer, p_idx = _peer_of(k, x, y, c)
            mine = src.at[p_idx] if scatter else src
            sems = dict(send_sem=send_sems.at[7 * o + k - 1], recv_sem=recv_sems.at[7 * o + k - 1],
                        device_id=peer, device_id_type=MESH)
            slot = land.at[p_idx] if arriving else land.at[me_idx]
            copies.append(pltpu.make_async_remote_copy(src_ref=mine, dst_ref=slot, **sems))
    return copies


_HBM = pl.BlockSpec(memory_space=pltpu.HBM)
_SEM = pl.BlockSpec(memory_space=pltpu.SEMAPHORE)
_EFFECT = pltpu.SideEffectType.DATAFLOW_SIDE_EFFECTING


def _exchange_start(name, scatter, arrays):
    n = len(arrays)
    lands = [lax.empty(a.shape if scatter else (N_DEV,) + a.shape, a.dtype) for a in arrays]

    def body(*refs):
        srcs, lnds = refs[:n], refs[n:2 * n]
        send_sems, recv_sems = refs[2 * n], refs[2 * n + 1]
        token = refs[-1]
        for cp in _split_copies(scatter, srcs, lnds, send_sems, recv_sems, False):
            cp.start()
        token[...] = jnp.zeros_like(token)

    hbm_in = [pltpu.with_memory_space_constraint(a, pltpu.HBM) for a in list(arrays) + lands]
    res = pl.pallas_call(
        body, name=name,
        out_shape=(pltpu.SemaphoreType.DMA((7 * n,)), pltpu.SemaphoreType.DMA((7 * n,)),
                   *[pltpu.HBM(a.shape, a.dtype) for a in hbm_in], jax.ShapeDtypeStruct((8, 128), F32)),
        in_specs=[_HBM] * (2 * n),
        out_specs=(_SEM, _SEM, *[_HBM] * (2 * n), pl.BlockSpec(memory_space=pltpu.VMEM)),
        input_output_aliases={i: 2 + i for i in range(2 * n)},
        compiler_params=pltpu.CompilerParams(has_side_effects=_EFFECT),
    )(*hbm_in)
    return res[0], res[1], list(res[2:2 + n]), list(res[2 + n:2 + 2 * n]), res[-1]


def _exchange_wait(name, scatter, started, after):
    send_sems, recv_sems, srcs, lands, _ = started
    n = len(srcs)

    def body(*refs):
        src_refs, lnd_refs = refs[:n], refs[n:2 * n]
        for cp in _split_copies(scatter, src_refs, lnd_refs, refs[2 * n], refs[2 * n + 1], False):
            cp.wait_send()
        for cp in _split_copies(scatter, src_refs, lnd_refs, refs[2 * n], refs[2 * n + 1], True):
            cp.wait_recv()

    res = pl.pallas_call(
        body, name=name, out_shape=tuple(pltpu.HBM(a.shape, a.dtype) for a in srcs + lands),
        in_specs=[_HBM] * (2 * n) + [_SEM, _SEM, pl.BlockSpec(memory_space=pl.ANY)],
        out_specs=tuple([_HBM] * (2 * n)), input_output_aliases={i: i for i in range(2 * n)},
        compiler_params=pltpu.CompilerParams(has_side_effects=_EFFECT),
    )(*srcs, *lands, send_sems, recv_sems, after)
    return list(res[n:]), list(res[:n])


def _adamw(name, parts, w, m, v, prev, layer, *, tr=256):
    _, a, b = w.shape
    tr = _tile(a, tr, 8)
    c1 = 1.0 - ADAM_B1 ** ADAM_STEP
    c2 = 1.0 - ADAM_B2 ** ADAM_STEP

    def body(p_ref, w_ref, m_ref, v_ref, _g, _d, _m, _v, g_out, d_out, m_out, v_out):
        g = p_ref[0].astype(F32)
        for s in range(1, N_DEV):
            g = g + p_ref[s].astype(F32)
        mn = ADAM_B1 * m_ref[...] + (1.0 - ADAM_B1) * g
        vn = ADAM_B2 * v_ref[...] + (1.0 - ADAM_B2) * (g * g)
        g_out[...] = g
        m_out[...] = mn
        v_out[...] = vn
        d_out[...] = -ADAM_LR * ((mn / c1) / (jnp.sqrt(vn / c2) + ADAM_EPS) + ADAM_WD * w_ref[...])

    slab = pl.BlockSpec((None, tr, b), lambda i: (layer, i, 0))
    whole = pl.BlockSpec(memory_space=pl.ANY)
    return pl.pallas_call(
        body, name=name, grid=(a // tr,),
        in_specs=[pl.BlockSpec((N_DEV, tr, b), lambda i: (0, i, 0)), slab, slab, slab] + [whole] * 4,
        out_specs=[slab] * 4, out_shape=[jax.ShapeDtypeStruct(w.shape, F32)] * 4,
        input_output_aliases={4: 0, 5: 1, 6: 2, 7: 3},
        compiler_params=_params(1))(parts, w, m, v, *prev)


_COL = ("w_in", "w_pool_o", "w_mem_o", "w_ff1")
_COL_IN_PLACE = ("w_in", "w_ff1")
_BIG =("w_in", "w_ret_o", "w_pool_o", "w_mem_kv", "w_mem_o", "w_out", "w_ff1", "w_ff2")
_SMALL = ("ret_decay_logit", "w_pool_grp", "pool_scale", "norm1_g", "norm2_g", "mem_norm_g", "final_norm_g")
_WEIGHTS = ("w_in", "ret_decay_logit", "w_ret_o", "w_pool_grp", "pool_scale", "w_pool_o", "w_mem_kv", "w_mem_o",
            "w_out", "w_ff1", "w_ff2", "norm1_g", "norm2_g", "mem_norm_g", "final_norm_g")


def _small_rows(size, d):
    return -(-size // (8 * d)) * 8


def _pack_small(ws, d):
    parts = []
    for n in _SMALL:
        flat = ws[n].reshape(-1)
        rows = _small_rows(flat.shape[0], d)
        parts.append(jnp.pad(flat, (0, rows * d - flat.shape[0])).reshape(rows, d))
    return jnp.concatenate(parts, axis=0)[None]


def _unpack_small(packed, like, d):
    out, off = {}, 0
    for n in _SMALL:
        rows = _small_rows(like[n].size, d)
        out[n] = packed[0, off:off + rows].reshape(-1)[:like[n].size].reshape(like[n].shape)
        off += rows
    return out


def kernel(x, mem, w_in, ret_decay_logit, w_ret_o, w_pool_grp, pool_scale, w_pool_o, w_mem_kv, w_mem_o, w_out, w_ff1, w_ff2, norm1_g, norm2_g, mem_norm_g, final_norm_g, loss_target, m_w_in, m_ret_decay_logit, m_w_ret_o, m_w_pool_grp, m_pool_scale, m_w_pool_o, m_w_mem_kv, m_w_mem_o, m_w_out, m_w_ff1, m_w_ff2, m_norm1_g, m_norm2_g, m_mem_norm_g, m_final_norm_g, v_w_in, v_ret_decay_logit, v_w_ret_o, v_w_pool_grp, v_pool_scale, v_w_pool_o, v_w_mem_kv, v_w_mem_o, v_w_out, v_w_ff1, v_w_ff2, v_norm1_g, v_norm2_g, v_mem_norm_g, v_final_norm_g):
    w = dict(w_in=w_in, ret_decay_logit=ret_decay_logit, w_ret_o=w_ret_o, w_pool_grp=w_pool_grp,
             pool_scale=pool_scale, w_pool_o=w_pool_o, w_mem_kv=w_mem_kv, w_mem_o=w_mem_o, w_out=w_out,
             w_ff1=w_ff1, w_ff2=w_ff2, norm1_g=norm1_g, norm2_g=norm2_g, mem_norm_g=mem_norm_g,
             final_norm_g=final_norm_g)
    mom = dict(w_in=m_w_in, ret_decay_logit=m_ret_decay_logit, w_ret_o=m_w_ret_o, w_pool_grp=m_w_pool_grp,
               pool_scale=m_pool_scale, w_pool_o=m_w_pool_o, w_mem_kv=m_w_mem_kv, w_mem_o=m_w_mem_o,
               w_out=m_w_out, w_ff1=m_w_ff1, w_ff2=m_w_ff2, norm1_g=m_norm1_g, norm2_g=m_norm2_g,
               mem_norm_g=m_mem_norm_g, final_norm_g=m_final_norm_g)
    vel = dict(w_in=v_w_in, ret_decay_logit=v_ret_decay_logit, w_ret_o=v_w_ret_o, w_pool_grp=v_w_pool_grp,
               pool_scale=v_pool_scale, w_pool_o=v_w_pool_o, w_mem_kv=v_w_mem_kv, w_mem_o=v_w_mem_o,
               w_out=v_w_out, w_ff1=v_w_ff1, w_ff2=v_w_ff2, norm1_g=v_norm1_g, norm2_g=v_norm2_g,
               mem_norm_g=v_mem_norm_g, final_norm_g=v_final_norm_g)

    bl, seq, d = x.shape
    mlen = mem.shape[1]
    depth = w_in.shape[0]
    t = bl * seq
    dk = d // 8

    me_idx = 4 * lax.axis_index("x") + 2 * lax.axis_index("y") + lax.axis_index("c")

    def natural(got):
        out = {}
        for n, g in zip(_BIG, got):
            if n in _COL_IN_PLACE:
                out[n] = g
            elif n in _COL:
                out[n] = jnp.transpose(g, (1, 0, 2)).reshape(g.shape[1], -1)
            else:
                out[n] = g.reshape(-1, g.shape[-1])
        return out

    shards = [[w[n][l].astype(MM) for n in _BIG] for l in range(depth)]
    full = [natural(_all_gather("gather_weights", shards[0]))]
    gathers = [None] + [_exchange_start(f"gather_start_{l}", False, shards[l]) for l in range(1, depth)]
    gather_tokens = [g[4] for g in gathers[1:]]

    inv = ROPE_BASE ** (-jnp.arange(0, dk, 2, dtype=F32) / dk)
    ang = jnp.arange(seq, dtype=F32)[:, None] * inv[None, :]
    cos2 = jnp.concatenate([jnp.cos(ang), jnp.cos(ang)], axis=-1)
    sin2 = jnp.concatenate([-jnp.sin(ang), jnp.sin(ang)], axis=-1)
    log_g = jax.nn.log_sigmoid(ret_decay_logit)
    x2 = x.reshape(t, d)
    mem2 = mem.reshape(bl * mlen, d)
    gmem = mem_norm_g.reshape(1, d)

    def merge(o_r, o_p, o_m, g_r, g_p, g_m):
        f = lambda z: z.astype(F32)
        return _sigmoid(f(g_r)) * f(o_r) + _sigmoid(f(g_p)) * f(o_p) + _sigmoid(f(g_m)) * f(o_m)

    def relu2(u):
        r = jnp.maximum(u, 0.0)
        return r * r

    def ident(a):
        return a

    saved = []
    xc = x2
    for l in range(depth):
        s = dict(x_in=xc)
        if l > 0:
            got, mine = _exchange_wait(f"gather_wait_{l}", False, gathers[l], xc)
            got = [lax.dynamic_update_slice(g, sh[None], (me_idx, 0, 0)) for g, sh in zip(got, mine)]
            full.append(natural(got))
        fw = full[l]
        g1 = norm1_g[l].reshape(1, d)
        g2 = norm2_g[l].reshape(1, d)
        s["proj"], s["h1"] = _pmm("proj", _rms_prologue, [(xc, d, 0)], [g1], fw["w_in"], w_mode="col",
                                  tm=1024, tn=1024, save_a=True, out_dtypes=(MM,),
                                  after=gather_tokens if l == 0 else ())
        proj = s["proj"]
        s["qr"], s["kr"], s["vb"] = _ret_pre(proj, cos2, sin2, d, seq)
        s["o_raw"], s["a_ret"] = _ret_core_fwd(s["qr"], s["kr"], s["vb"], proj, log_g[l], d, bl, seq)
        s["y"] = _pool_fwd(proj, w_pool_grp[l], pool_scale[l].reshape(1, -1), d, bl, seq)
        s["kv"], s["memn"] = _pmm("mem_kv", _rms_prologue, [(mem2, d, 0)], [gmem], fw["w_mem_kv"],
                                  tm=512, tn=512, save_a=True)
        s["o_att"] = _attn_fwd(proj, s["kv"], d, bl, seq, mlen)
        (s["o_ret"],) = _pmm("ret_o", None, [(s["a_ret"], d, 0)], [], fw["w_ret_o"], tm=1024, tn=512,
                             out_dtypes=(MM,))
        (s["o_pool"],) = _pmm("pool_o", None, [(s["y"], d // 2, 0)], [], fw["w_pool_o"],
                              tm=1024, tn=512, out_dtypes=(MM,))
        (s["o_mem"],) = _pmm("mem_o", None, [(s["o_att"], d // 2, 0)], [], fw["w_mem_o"],
                             tm=1024, tn=512, out_dtypes=(MM,))
        s["x_mid"], s["merged"] = _pmm(
            "merge_out", merge,
            [(s["o_ret"], d, 0), (s["o_pool"], d, 0), (s["o_mem"], d, 0), (proj, d, 4), (proj, d, 5), (proj, d, 6)],
            [], fw["w_out"], tm=512, tn=512, residual=xc, save_a=True)
        s["u"], s["h2"] = _pmm("ff1", _rms_prologue, [(s["x_mid"], d, 0)], [g2], fw["w_ff1"], w_mode="col",
                               tm=1024, tn=512, save_a=True)
        xc, s["a"] = _pmm("ff2", relu2, [(s["u"], s["u"].shape[1], 0)], [], fw["w_ff2"],
                          tm=512, tn=512, residual=s["x_mid"], save_a=True)
        saved.append(s)

    dxc, g_final, loss_part = _loss_head(xc, loss_target.reshape(t, d), final_norm_g.reshape(1, d))
    loss = lax.psum(loss_part[0, 0], ("x", "y", "c"))

    small_names = ("w_pool_grp", "pool_scale", "norm1_g", "norm2_g", "ret_decay_logit")
    grads = {n: [None] * depth for n in small_names}
    scatters = [None] * depth
    prev_token = ()
    dmemn = jnp.zeros((bl * mlen, d), F32)

    def relu2_bwd(acc, u):
        return (acc * (2.0 * jnp.maximum(u, 0.0)),)

    def gates_bwd(acc, g_r, g_p, g_m, o_r, o_p, o_m):
        outs_o, outs_g = [], []
        for gz, oz in ((g_r, o_r), (g_p, o_p), (g_m, o_m)):
            sg = _sigmoid(gz.astype(F32))
            outs_o.append(acc * sg)
            outs_g.append(acc * oz.astype(F32) * (sg * (1.0 - sg)))
        return tuple(outs_o + outs_g)

    for l in reversed(range(depth)):
        s = saved[l]
        fw = full[l]
        proj = s["proj"]
        g1 = norm1_g[l].reshape(1, d)
        g2 = norm2_g[l].reshape(1, d)
        dw = {}
        (du,) = _pmm("ff2_bwd", ident, [(dxc, d, 0)], [], fw["w_ff2"], w_mode="nt", tm=512, tn=1024,
                     epilogue=relu2_bwd, epi_ins=[(s["u"], 0)], out_dtypes=(MM,), after=prev_token)
        dw["w_ff2"] = _tnmm("dw_ff2", s["a"], dxc)
        (dh2,) = _pmm("ff1_bwd", None, [(du, du.shape[1], 0)], [], fw["w_ff1"], w_mode="col_t", tm=512, tn=512)
        dw["w_ff1"] = _tnmm("dw_ff1", s["h2"], du, col_shards=True)
        dmid, grads["norm2_g"][l] = _rms_bwd("norm2_bwd", dh2, s["x_mid"], g2, dxc)
        d_oret, d_opool, d_omem, dgr, dgp, dgm = _pmm(
            "out_bwd", ident, [(dmid, d, 0)], [], fw["w_out"], w_mode="nt", tm=512, tn=512, epilogue=gates_bwd,
            epi_ins=[(proj, 4 * d), (proj, 5 * d), (proj, 6 * d), (s["o_ret"], 0), (s["o_pool"], 0), (s["o_mem"], 0)],
            out_dtypes=(MM,) * 6)
        dw["w_out"] = _tnmm("dw_out", s["merged"], dmid)
        (da_ret,) = _pmm("ret_o_bwd", None, [(d_oret, d, 0)], [], fw["w_ret_o"], w_mode="nt", tm=1024, tn=512)
        dw["w_ret_o"] = _tnmm("dw_ret_o", s["a_ret"], d_oret)
        (dy,) = _pmm("pool_o_bwd", None, [(d_opool, d, 0)], [], fw["w_pool_o"], w_mode="nt", tm=1024, tn=512)
        dw["w_pool_o"] = _tnmm("dw_pool_o", s["y"], d_opool)
        (do_att,) = _pmm("mem_o_bwd", None, [(d_omem, d, 0)], [], fw["w_mem_o"], w_mode="nt", tm=1024, tn=512)
        dw["w_mem_o"] = _tnmm("dw_mem_o", s["o_att"], d_omem)
        dg_ret, do_ret = _ret_post_bwd(da_ret, proj, s["o_raw"], d)
        dq, dkk, dvv, dlf, dlb = _ret_core_bwd(s["qr"], s["kr"], s["vb"], do_ret, cos2, sin2, log_g[l], d, bl, seq)
        dl = jnp.stack([dlf[:, 0, 0].reshape(bl, HEADS).sum(0), dlb[:, 0, 0].reshape(bl, HEADS).sum(0)])
        grads["ret_decay_logit"][l] = dl * jax.nn.sigmoid(-ret_decay_logit[l])
        dp, grads["w_pool_grp"][l], dscale = _pool_bwd(proj, dy, w_pool_grp[l], pool_scale[l].reshape(1, -1),
                                                       d, bl, seq)
        grads["pool_scale"][l] = dscale.reshape(-1)
        dqm, dmk, dmv = _attn_bwd(proj, s["kv"], do_att, d, bl, seq, mlen)
        dkv = jnp.concatenate([dmk, dmv], axis=-1).astype(MM)
        dw["w_mem_kv"] = _tnmm("dw_mem_kv", s["memn"], dkv)
        (dmemn,) = _pmm("mem_kv_bwd", None, [(dkv, d, 0)], [], fw["w_mem_kv"], w_mode="nt", tm=512, tn=512,
                        residual=dmemn)
        dproj = jnp.concatenate([dq, dkk, dvv, dg_ret, dp, dqm, dgr, dgp, dgm], axis=-1)
        (dh1,) = _pmm("proj_bwd", None, [(dproj, dproj.shape[1], 0)], [], fw["w_in"], w_mode="col_t",
                      tm=512, tn=512)
        dw["w_in"] = _tnmm("dw_in", s["h1"], dproj, col_shards=True)
        dxc, grads["norm1_g"][l] = _rms_bwd("norm1_bwd", dh1, s["x_in"], g1, dmid)
        sends = []
        for n in _BIG:
            a, b = w[n].shape[1:]
            if n in _COL_IN_PLACE:
                sends.append(dw[n])
            elif n in _COL:
                sends.append(jnp.transpose(dw[n].reshape(a, N_DEV, b), (1, 0, 2)))
            else:
                sends.append(dw[n].reshape(N_DEV, a, b))
        scatters[l] = _exchange_start(f"scatter_start_{l}", True, sends)
        prev_token = (scatters[l][4],)

    _, g_memn = _rms_bwd("mem_norm_bwd", dmemn, mem2, gmem, None)
    grad_x = dxc.reshape(bl, seq, d)

    big = {n: [lax.empty(w[n].shape, F32) for _ in range(4)] for n in _BIG}
    after = dxc
    for l in reversed(range(depth)):
        recv, sent = _exchange_wait(f"scatter_wait_{l}", True, scatters[l], after)
        for i, n in enumerate(_BIG):
            own = lax.dynamic_slice_in_dim(sent[i], me_idx, 1, axis=0)
            parts = lax.dynamic_update_slice(recv[i], own, (me_idx, 0, 0))
            big[n] = _adamw("adamw_" + n, parts, w[n], mom[n], vel[n], big[n], l)
        after = big[_BIG[-1]][0]

    small_g = dict(ret_decay_logit=jnp.stack(grads["ret_decay_logit"]), w_pool_grp=jnp.stack(grads["w_pool_grp"]),
                   pool_scale=jnp.stack(grads["pool_scale"]),
                   norm1_g=jnp.concatenate(grads["norm1_g"], axis=0), norm2_g=jnp.concatenate(grads["norm2_g"], axis=0),
                   mem_norm_g=g_memn.reshape(-1), final_norm_g=g_final.reshape(-1))
    (small_parts,) = _all_gather("gather_small_grads", [_pack_small(small_g, d)[0]])
    w_small = _pack_small(w, d)
    small = _adamw("adamw_small", small_parts, w_small, _pack_small(mom, d), _pack_small(vel, d),
                   [lax.empty(w_small.shape, F32) for _ in range(4)], 0)
    small = [_unpack_small(o, w, d) for o in small]

    outs = [loss, grad_x]
    for k in range(4):
        outs += [big[n][k] if n in _BIG else small[k][n] for n in _WEIGHTS]
    return tuple(outs)
```

```python
import jax
import jax.numpy as jnp
from jax import lax
from jax.experimental import pallas as pl
from jax.experimental.pallas import tpu as pltpu

F32 = jnp.float32
MM = jnp.bfloat16
N_DEV = 8
HEADS = 4
POOL_WINDOWS = (2, 4, 8, 16)
EPS = 1e-6
ROPE_BASE = 10000.0
ADAM_LR, ADAM_B1, ADAM_B2, ADAM_EPS, ADAM_WD, ADAM_STEP = 0.001, 0.9, 0.999, 1e-08, 0.01, 10
V7X_VMEM_LIMIT = 56 * 1024 * 1024
MESH = pl.DeviceIdType.MESH


def _params(n_axes):
    return pltpu.CompilerParams(dimension_semantics=("arbitrary",) * n_axes,
                                vmem_limit_bytes=V7X_VMEM_LIMIT)


def _tile(n, pref, align=128):
    cands = [c for c in range(align, min(pref, n) + 1, align) if n % c == 0]
    return max(cands) if cands else n


def _sigmoid(z):
    return 0.5 * jnp.tanh(0.5 * z) + 0.5


def _dot(a, b):
    return jnp.dot(a, b, preferred_element_type=F32)


def _dot_nt(a, b):
    return lax.dot_general(a, b, (((1,), (1,)), ((), ())), preferred_element_type=F32)


def _dot_tn(a, b):
    return lax.dot_general(a, b, (((0,), (0,)), ((), ())), preferred_element_type=F32)


def _pmm(name, prologue, row_ins, vec_ins, w, *, tm, tn, w_mode="nn", residual=None, save_a=False,
         epilogue=None, epi_ins=(), out_dtypes=(F32,), after=()):
    m = row_ins[0][0].shape[0]
    wb = None
    if w_mode == "nn":
        k, n = w.shape
        tn = _tile(n, tn)
        w_spec = pl.BlockSpec((k, tn), lambda i, j: (0, j))
    elif w_mode == "nt":
        n, k = w.shape
        tn = _tile(n, tn)
        w_spec = pl.BlockSpec((tn, k), lambda i, j: (j, 0))
    elif w_mode == "col":
        _, k, wb = w.shape
        n = N_DEV * wb
        tn = _tile(wb, tn)
        w_spec = pl.BlockSpec((None, k, tn), lambda i, j, q=wb // tn: (j // q, 0, j % q))
    else:
        _, n, wb = w.shape
        k = N_DEV * wb
        tn = _tile(n, tn)
        w_spec = pl.BlockSpec((N_DEV, tn, wb), lambda i, j: (0, j, 0))
    tm = _tile(m, tm, 8)
    n_row, n_vec, n_epi, n_out = len(row_ins), len(vec_ins), len(epi_ins), len(out_dtypes)
    has_res = residual is not None
    use_scr = prologue is not None

    def body(*refs):
        row_refs = refs[:n_row]
        p = n_row
        vec_refs = refs[p:p + n_vec]
        p += n_vec
        w_ref = refs[p]
        p += 1
        res_ref = refs[p] if has_res else None
        p += int(has_res)
        epi_refs = refs[p:p + n_epi]
        p += n_epi + len(after)
        out_refs = refs[p:p + n_out]
        p += n_out
        a_out = refs[p] if save_a else None
        p += int(save_a)
        if use_scr:
            a_src = refs[p]

            @pl.when(pl.program_id(1) == 0)
            def _():
                a = prologue(*[r[...] for r in row_refs], *[v[...] for v in vec_refs]).astype(MM)
                a_src[...] = a
                if save_a:
                    a_out[...] = a
        else:
            a_src = row_refs[0]
        if w_mode == "nt":
            acc = _dot_nt(a_src[...], w_ref[...])
        elif w_mode == "col_t":
            acc = _dot_nt(a_src[:, 0:wb], w_ref[0])
            for dev in range(1, N_DEV):
                acc = acc + _dot_nt(a_src[:, dev * wb:(dev + 1) * wb], w_ref[dev])
        else:
            acc = _dot(a_src[...], w_ref[...])
        if has_res:
            acc = acc + res_ref[...]
        outs = epilogue(acc, *[e[...] for e in epi_refs]) if epilogue is not None else (acc,)
        for o_ref, o in zip(out_refs, outs):
            o_ref[...] = o.astype(o_ref.dtype)

    in_specs = [pl.BlockSpec((tm, wd), lambda i, j, cb=cb: (i, cb)) for (_, wd, cb) in row_ins]
    in_specs += [pl.BlockSpec(v.shape, lambda i, j: (0, 0)) for v in vec_ins]
    in_specs += [w_spec]
    args = [r[0] for r in row_ins] + list(vec_ins) + [w]
    if has_res:
        in_specs.append(pl.BlockSpec((tm, tn), lambda i, j: (i, j)))
        args.append(residual)
    for (arr, off) in epi_ins:
        assert off % tn == 0
        in_specs.append(pl.BlockSpec((tm, tn), lambda i, j, ob=off // tn: (i, ob + j)))
        args.append(arr)
    n_after = len(after)
    in_specs += [pl.BlockSpec(memory_space=pl.ANY)] * n_after
    args += list(after)
    out_specs = [pl.BlockSpec((tm, tn), lambda i, j: (i, j)) for _ in out_dtypes]
    out_shape = [jax.ShapeDtypeStruct((m, n), dt) for dt in out_dtypes]
    if save_a:
        out_specs.append(pl.BlockSpec((tm, k), lambda i, j: (i, 0)))
        out_shape.append(jax.ShapeDtypeStruct((m, k), MM))
    scratch = [pltpu.VMEM((tm, k), MM)] if use_scr else []
    return pl.pallas_call(body, name=name, grid=(m // tm, n // tn), in_specs=in_specs,
                          out_specs=out_specs, out_shape=out_shape, scratch_shapes=scratch,
                          compiler_params=_params(2))(*args)


def _tnmm(name, a, b, *, tm=1024, tn=1024, tk=1024, col_shards=False):
    t, m = a.shape
    n = b.shape[1]
    tm, tk = _tile(m, tm), _tile(t, tk, 8)
    per_tile = 1
    if col_shards:
        wb = n // N_DEV
        while 2 * per_tile * wb <= tn and 2 * per_tile <= N_DEV:
            per_tile *= 2
        tn = per_tile * wb
        out_spec = pl.BlockSpec((per_tile, tm, wb), lambda i, j, kk: (j, i, 0))
        out_shape = jax.ShapeDtypeStruct((N_DEV, m, wb), MM)
    else:
        tn = _tile(n, tn)
        out_spec = pl.BlockSpec((tm, tn), lambda i, j, kk: (i, j))
        out_shape = jax.ShapeDtypeStruct((m, n), MM)
    nk = t // tk

    def body(a_ref, b_ref, o_ref, acc):
        kk = pl.program_id(2)

        @pl.when(kk == 0)
        def _():
            acc[...] = jnp.zeros_like(acc)

        acc[...] += _dot_tn(a_ref[...].astype(MM), b_ref[...].astype(MM))

        @pl.when(kk == nk - 1)
        def _():
            if col_shards:
                for sh in range(per_tile):
                    o_ref[sh] = acc[:, sh * wb:(sh + 1) * wb].astype(o_ref.dtype)
            else:
                o_ref[...] = acc[...].astype(o_ref.dtype)

    return pl.pallas_call(
        body, name=name, grid=(m // tm, n // tn, nk),
        in_specs=[pl.BlockSpec((tk, tm), lambda i, j, kk: (kk, i)),
                  pl.BlockSpec((tk, tn), lambda i, j, kk: (kk, j))],
        out_specs=out_spec, out_shape=out_shape,
        scratch_shapes=[pltpu.VMEM((tm, tn), F32)],
        compiler_params=_params(3))(a, b)


def _rms_prologue(x, g):
    r = lax.rsqrt(jnp.mean(x * x, axis=-1, keepdims=True) + EPS)
    return x * r * g


def _rms_bwd_rows(dh, x, g):
    d = x.shape[-1]
    r = lax.rsqrt(jnp.mean(x * x, axis=-1, keepdims=True) + EPS)
    xh = x * r
    dxh = dh * g
    dx = r * (dxh - xh * (jnp.sum(dxh * xh, axis=-1, keepdims=True) / d))
    dg = jnp.sum(dh * xh, axis=0, keepdims=True)
    return dx, dg


def _rms_bwd(name, dh, x, g, dres, *, tm=256):
    m, d = x.shape
    tm = min(tm, m)
    has_res = dres is not None

    def body(*refs):
        if has_res:
            dh_ref, x_ref, g_ref, r_ref, dx_ref, dg_ref = refs
        else:
            dh_ref, x_ref, g_ref, dx_ref, dg_ref = refs
        dx, dg = _rms_bwd_rows(dh_ref[...], x_ref[...], g_ref[...])
        if has_res:
            dx = dx + r_ref[...]
        dx_ref[...] = dx

        @pl.when(pl.program_id(0) == 0)
        def _():
            dg_ref[...] = jnp.zeros_like(dg_ref)

        dg_ref[...] += dg

    row = pl.BlockSpec((tm, d), lambda i: (i, 0))
    vec = pl.BlockSpec((1, d), lambda i: (0, 0))
    in_specs = [row, row, vec] + ([row] if has_res else [])
    args = [dh, x, g] + ([dres] if has_res else [])
    return pl.pallas_call(body, name=name, grid=(m // tm,), in_specs=in_specs, out_specs=[row, vec],
                          out_shape=[jax.ShapeDtypeStruct((m, d), F32), jax.ShapeDtypeStruct((1, d), F32)],
                          compiler_params=_params(1))(*args)


def _loss_head(x, target, g, *, tm=256):
    m, d = x.shape
    tm = min(tm, m)

    def body(x_ref, t_ref, g_ref, dx_ref, dg_ref, loss_ref):
        xv, gv = x_ref[...], g_ref[...]
        y = _rms_prologue(xv, gv)
        err = y - t_ref[...]
        part = 0.5 * jnp.sum(jnp.sum(err * err, axis=-1, keepdims=True) / d)
        dx, dg = _rms_bwd_rows(err / d, xv, gv)
        dx_ref[...] = dx

        @pl.when(pl.program_id(0) == 0)
        def _():
            dg_ref[...] = jnp.zeros_like(dg_ref)
            loss_ref[...] = jnp.zeros_like(loss_ref)

        dg_ref[...] += dg
        loss_ref[...] += jnp.full(loss_ref.shape, part, F32)

    row = pl.BlockSpec((tm, d), lambda i: (i, 0))
    vec = pl.BlockSpec((1, d), lambda i: (0, 0))
    lspec = pl.BlockSpec((1, 128), lambda i: (0, 0))
    return pl.pallas_call(body, name="loss_head", grid=(m // tm,), in_specs=[row, row, vec],
                          out_specs=[row, vec, lspec],
                          out_shape=[jax.ShapeDtypeStruct((m, d), F32), jax.ShapeDtypeStruct((1, d), F32),
                                     jax.ShapeDtypeStruct((1, 128), F32)],
                          compiler_params=_params(1))(x, target, g)


def _rot(xv, cos2, sin2, half):
    return xv * cos2 + pltpu.roll(xv, half, 1) * sin2


def _rot_t(dv, cos2, sin2, half):
    return dv * cos2 + pltpu.roll(dv * sin2, half, 1)


def _ret_pre(proj, cos2, sin2, d, seq, *, ts=512):
    t = proj.shape[0]
    ts = min(ts, seq)
    dk = d // 8
    ns = seq // ts
    scale = float(dk) ** -0.5

    def body(q_ref, k_ref, v_ref, c_ref, s_ref, qo, ko, vo):
        c, s = c_ref[...], s_ref[...]
        for h in range(HEADS):
            sl = slice(h * dk, (h + 1) * dk)
            qo[:, sl] = _rot(q_ref[:, sl].astype(F32), c, s, dk // 2).astype(MM)
            ko[:, sl] = (_rot(k_ref[:, sl].astype(F32), c, s, dk // 2) * scale).astype(MM)
        vo[...] = v_ref[...].astype(MM)

    half = pl.BlockSpec((ts, d // 2), lambda i: (i, 0))
    tab = pl.BlockSpec((ts, dk), lambda i: (i % ns, 0))
    return pl.pallas_call(
        body, name="ret_pre", grid=(t // ts,),
        in_specs=[half, pl.BlockSpec((ts, d // 2), lambda i: (i, 1)), pl.BlockSpec((ts, d), lambda i: (i, 1)),
                  tab, tab],
        out_specs=[half, half, pl.BlockSpec((ts, d), lambda i: (i, 0))],
        out_shape=[jax.ShapeDtypeStruct((t, d // 2), MM), jax.ShapeDtypeStruct((t, d // 2), MM),
                   jax.ShapeDtypeStruct((t, d), MM)],
        compiler_params=_params(1))(proj, proj, proj, cos2, sin2)


def _ret_consts(lg_ref, h, t, dk):
    lf, lb = lg_ref[0, h], lg_ref[1, h]
    ab = (lax.broadcasted_iota(jnp.int32, (t, t), 0) - lax.broadcasted_iota(jnp.int32, (t, t), 1)).astype(F32)
    dmat = jnp.exp(jnp.where(ab >= 0, lf * ab, -lb * ab))
    up = lax.broadcasted_iota(jnp.int32, (t, dk), 0).astype(F32) + 1.0
    down = float(t) - up
    one = jnp.ones((1, 1), F32)
    return dict(ab=ab, dmat=dmat, xi_f=jnp.exp(lf * up), zeta_f=jnp.exp(lf * down), xi_b=jnp.exp(lb * up),
                zeta_b=jnp.exp(lb * down), up=up[:, 0:1], down=down[:, 0:1],
                cf=jnp.exp(one * (lf * t)), cb=jnp.exp(one * (lb * t)))


def _scaled(xv, rows):
    return (xv.astype(F32) * rows).astype(MM)


def _ret_core_fwd(qr, kr, vb, proj, lg, d, bl, seq, *, tc=256):
    t = qr.shape[0]
    dk, dv = d // 8, d // 4
    tc = min(tc, seq)
    nc = seq // tc

    def body(lg_ref, q_ref, k_ref, v_ref, g_ref, o_ref, a_ref):
        c = _ret_consts(lg_ref, pl.program_id(1), tc, dk)

        def rows_of(i):
            return pl.ds(pl.multiple_of(i * tc, tc), tc)

        def fwd_step(i, sf):
            rows = rows_of(i)
            q, kk, v = q_ref[rows, :], k_ref[rows, :], v_ref[rows, :]
            p = (_dot_nt(q, kk) * c["dmat"]).astype(MM)
            o_ref[rows, :] = _dot(p, v) + _dot(_scaled(q, c["xi_f"]), sf.astype(MM))
            return sf * c["cf"] + _dot_tn(_scaled(kk, c["zeta_f"]), v)

        lax.fori_loop(0, nc, fwd_step, jnp.zeros((dk, dv), F32))

        def bwd_step(ii, sb):
            rows = rows_of(nc - 1 - ii)
            q, kk, v = q_ref[rows, :], k_ref[rows, :], v_ref[rows, :]
            o_ref[rows, :] += _dot(_scaled(q, c["zeta_b"]), sb.astype(MM))
            return sb * c["cb"] + _dot_tn(_scaled(kk, c["xi_b"]), v)

        lax.fori_loop(0, nc, bwd_step, jnp.zeros((dk, dv), F32))

        def post(i, carry):
            rows = rows_of(i)
            o = o_ref[rows, :]
            oc = o - jnp.mean(o, axis=-1, keepdims=True)
            on = oc * lax.rsqrt(jnp.mean(oc * oc, axis=-1, keepdims=True) + EPS)
            g = g_ref[rows, :].astype(F32)
            a_ref[rows, :] = (on * (g * _sigmoid(g))).astype(MM)
            return carry

        lax.fori_loop(0, nc, post, 0)

    qk = pl.BlockSpec((seq, dk), lambda b, h: (b, h))
    vv = pl.BlockSpec((seq, dv), lambda b, h: (b, h))
    return pl.pallas_call(
        body, name="ret_core_fwd", grid=(bl, HEADS),
        in_specs=[pl.BlockSpec(memory_space=pltpu.SMEM), qk, qk, vv,
                  pl.BlockSpec((seq, dv), lambda b, h: (b, 2 * HEADS + h))],
        out_specs=[vv, vv],
        out_shape=[jax.ShapeDtypeStruct((t, d), F32), jax.ShapeDtypeStruct((t, d), MM)],
        compiler_params=_params(2))(lg, qr, kr, vb, proj)


def _ret_post_bwd(da, proj, o_raw, d, *, ts=512):
    t = da.shape[0]
    dv = d // 4
    ts = min(ts, t)

    def body(da_ref, g_ref, o_ref, dg_ref, do_ref):
        o, g, dav = o_ref[...], g_ref[...].astype(F32), da_ref[...]
        mu = jnp.mean(o, axis=-1, keepdims=True)
        oc = o - mu
        r = lax.rsqrt(jnp.mean(oc * oc, axis=-1, keepdims=True) + EPS)
        on = oc * r
        sg = _sigmoid(g)
        don = dav * (g * sg)
        dg_ref[...] = (dav * on * (sg * (1.0 + g * (1.0 - sg)))).astype(MM)
        do = r * (don - jnp.mean(don, axis=-1, keepdims=True) - on * jnp.mean(don * on, axis=-1, keepdims=True))
        do_ref[...] = do.astype(MM)

    blk = pl.BlockSpec((ts, dv), lambda i, h: (i, h))
    return pl.pallas_call(
        body, name="ret_post_bwd", grid=(t // ts, HEADS),
        in_specs=[blk, pl.BlockSpec((ts, dv), lambda i, h: (i, 2 * HEADS + h)), blk],
        out_specs=[blk, blk],
        out_shape=[jax.ShapeDtypeStruct((t, d), MM), jax.ShapeDtypeStruct((t, d), MM)],
        compiler_params=_params(2))(da, proj, o_raw)


def _ret_core_bwd(qr, kr, vb, do, cos2, sin2, lg, d, bl, seq, *, tc=256):
    t = qr.shape[0]
    dk, dv = d // 8, d // 4
    tc = min(tc, seq)
    nc = seq // tc
    scale = float(dk) ** -0.5

    def body(lg_ref, q_ref, k_ref, v_ref, do_ref, c_ref, s_ref, dq_ref, dk_ref, dv_ref, dlf_ref, dlb_ref,
             dq_acc, dk_acc, dv_acc, sf_all, sb_all):
        c = _ret_consts(lg_ref, pl.program_id(1), tc, dk)
        fwd = c["ab"] >= 0
        zero_state = jnp.zeros((dk, dv), F32)
        zero = jnp.zeros((1, 1), F32)

        def rows_of(i):
            return pl.ds(pl.multiple_of(i * tc, tc), tc)

        def total(xv):
            return jnp.sum(xv, keepdims=True)

        def sf_pass(i, sf):
            rows = rows_of(i)
            sf_all[i] = sf
            return sf * c["cf"] + _dot_tn(_scaled(k_ref[rows, :], c["zeta_f"]), v_ref[rows, :])

        lax.fori_loop(0, nc, sf_pass, zero_state)

        def sb_pass(ii, sb):
            i = nc - 1 - ii
            rows = rows_of(i)
            sb_all[i] = sb
            return sb * c["cb"] + _dot_tn(_scaled(k_ref[rows, :], c["xi_b"]), v_ref[rows, :])

        lax.fori_loop(0, nc, sb_pass, zero_state)

        def fwd_sweep(i, carry):
            hh, dlf, dlb = carry
            rows = rows_of(i)
            q, kk, v, dov = q_ref[rows, :], k_ref[rows, :], v_ref[rows, :], do_ref[rows, :]
            dof, vf = dov.astype(F32), v.astype(F32)
            p = _dot_nt(q, kk) * c["dmat"]
            da = _dot_nt(dov, v)
            x = p * da * c["ab"]
            dlf = dlf + total(jnp.where(fwd, x, 0.0))
            dlb = dlb - total(jnp.where(fwd, 0.0, x))
            pb, dpb = p.astype(MM), (da * c["dmat"]).astype(MM)
            dq = _dot(dpb, kk)
            dkc = _dot_tn(dpb, q)
            dvc = _dot_tn(pb, dov)
            sf, sb = sf_all[i], sb_all[i]
            sfb, sbb = sf.astype(MM), sb.astype(MM)
            q_xf, q_zb = _scaled(q, c["xi_f"]), _scaled(q, c["zeta_b"])
            dq = dq + _dot_nt(dov, sfb) * c["xi_f"] + _dot_nt(dov, sbb) * c["zeta_b"]
            dlf = dlf + total(jnp.sum(_dot(q_xf, sfb) * dof, axis=-1, keepdims=True) * c["up"])
            dlb = dlb + total(jnp.sum(_dot(q_zb, sbb) * dof, axis=-1, keepdims=True) * c["down"])
            hb = hh.astype(MM)
            dkc = dkc + _dot_nt(v, hb) * c["xi_b"]
            dv_bx = _dot(_scaled(kk, c["xi_b"]), hb)
            dlb = dlb + total(jnp.sum(vf * dv_bx, axis=-1, keepdims=True) * c["up"])
            dlb = dlb + float(tc) * total(hh * (sb * c["cb"]))
            dq_acc[rows, :] = dq
            dk_acc[rows, :] = dkc
            dv_acc[rows, :] = dvc + dv_bx
            return hh * c["cb"] + _dot_tn(q_zb, dov), dlf, dlb

        _, dlf, dlb = lax.fori_loop(0, nc, fwd_sweep, (zero_state, zero, zero))

        def rev_sweep(ii, carry):
            gg, dlf = carry
            i = nc - 1 - ii
            rows = rows_of(i)
            q, kk, v, dov = q_ref[rows, :], k_ref[rows, :], v_ref[rows, :], do_ref[rows, :]
            gb = gg.astype(MM)
            dk_acc[rows, :] += _dot_nt(v, gb) * c["zeta_f"]
            dv_fx = _dot(_scaled(kk, c["zeta_f"]), gb)
            dv_acc[rows, :] += dv_fx
            dlf = dlf + total(jnp.sum(v.astype(F32) * dv_fx, axis=-1, keepdims=True) * c["down"])
            dlf = dlf + float(tc) * total(gg * (sf_all[i] * c["cf"]))
            return gg * c["cf"] + _dot_tn(_scaled(q, c["xi_f"]), dov), dlf

        _, dlf = lax.fori_loop(0, nc, rev_sweep, (zero_state, dlf))

        cs, sn = c_ref[...], s_ref[...]
        dq_ref[...] = _rot_t(dq_acc[...], cs, sn, dk // 2).astype(MM)
        dk_ref[...] = (_rot_t(dk_acc[...], cs, sn, dk // 2) * scale).astype(MM)
        dv_ref[...] = dv_acc[...].astype(MM)
        dlf_ref[...] = jnp.broadcast_to(dlf, dlf_ref.shape)
        dlb_ref[...] = jnp.broadcast_to(dlb, dlb_ref.shape)

    qk = pl.BlockSpec((seq, dk), lambda b, h: (b, h))
    vv = pl.BlockSpec((seq, dv), lambda b, h: (b, h))
    tab = pl.BlockSpec((seq, dk), lambda b, h: (0, 0))
    dl = pl.BlockSpec((None, 8, 128), lambda b, h: (b * HEADS + h, 0, 0))
    return pl.pallas_call(
        body, name="ret_core_bwd", grid=(bl, HEADS),
        in_specs=[pl.BlockSpec(memory_space=pltpu.SMEM), qk, qk, vv, vv, tab, tab],
        out_specs=[qk, qk, vv, dl, dl],
        out_shape=[jax.ShapeDtypeStruct((t, d // 2), MM), jax.ShapeDtypeStruct((t, d // 2), MM),
                   jax.ShapeDtypeStruct((t, d), MM),
                   jax.ShapeDtypeStruct((bl * HEADS, 8, 128), F32), jax.ShapeDtypeStruct((bl * HEADS, 8, 128), F32)],
        scratch_shapes=[pltpu.VMEM((seq, dk), F32), pltpu.VMEM((seq, dk), F32), pltpu.VMEM((seq, dv), F32),
                        pltpu.VMEM((nc, dk, dv), F32), pltpu.VMEM((nc, dk, dv), F32)],
        compiler_params=_params(2))(lg, qr, kr, vb, do, cos2, sin2)


def _window_count(row, w, seq):
    return (jnp.minimum(row + w // 2, seq) - jnp.maximum(row - w // 2, 0)).astype(F32)


def _window_sum(pv, row, w, seq, sign):
    acc = None
    for j in range(-(w // 2), w // 2):
        if j == 0:
            term = pv
        else:
            src = row + sign * j
            term = jnp.where((src >= 0) & (src < seq), pltpu.roll(pv, (-sign * j) % seq, 0), 0.0)
        acc = term if acc is None else acc + term
    return acc


def _pool_fwd(proj, w_grp, scale, d, bl, seq):
    t = proj.shape[0]
    dg = d // 8

    def body(p_ref, w_ref, s_ref, y_ref):
        row = lax.broadcasted_iota(jnp.int32, (seq, dg), 0)
        for gi, w in enumerate(POOL_WINDOWS):
            sl = slice(gi * dg, (gi + 1) * dg)
            pg = p_ref[:, sl].astype(F32)
            mixed = _window_sum(pg, row, w, seq, 1) / _window_count(row, w, seq) - pg
            yp = _dot(mixed.astype(MM), w_ref[gi].astype(MM))
            y_ref[:, sl] = (yp * s_ref[:, sl]).astype(MM)

    return pl.pallas_call(
        body, name="pool_fwd", grid=(bl,),
        in_specs=[pl.BlockSpec((seq, d // 2), lambda b: (b, 6)),
                  pl.BlockSpec(w_grp.shape, lambda b: (0, 0, 0)),
                  pl.BlockSpec((1, d // 2), lambda b: (0, 0))],
        out_specs=pl.BlockSpec((seq, d // 2), lambda b: (b, 0)),
        out_shape=jax.ShapeDtypeStruct((t, d // 2), MM),
        compiler_params=_params(1))(proj, w_grp, scale)


def _pool_bwd(proj, dy, w_grp, scale, d, bl, seq):
    t = proj.shape[0]
    dg = d // 8

    def body(p_ref, dy_ref, w_ref, s_ref, dp_ref, dw_ref, ds_ref):
        @pl.when(pl.program_id(0) == 0)
        def _():
            dw_ref[...] = jnp.zeros_like(dw_ref)
            ds_ref[...] = jnp.zeros_like(ds_ref)

        row = lax.broadcasted_iota(jnp.int32, (seq, dg), 0)
        for gi, w in enumerate(POOL_WINDOWS):
            sl = slice(gi * dg, (gi + 1) * dg)
            pg = p_ref[:, sl].astype(F32)
            cnt = _window_count(row, w, seq)
            mixb = (_window_sum(pg, row, w, seq, 1) / cnt - pg).astype(MM)
            wgb = w_ref[gi].astype(MM)
            yp = _dot(mixb, wgb)
            dyg = dy_ref[:, sl]
            ds_ref[:, sl] += jnp.sum(dyg * yp, axis=0, keepdims=True)
            dyp = (dyg * s_ref[:, sl]).astype(MM)
            dmixed = _dot_nt(dyp, wgb)
            dw_ref[gi] += _dot_tn(mixb, dyp)
            dp_ref[:, sl] = (_window_sum(dmixed / cnt, row, w, seq, -1) - dmixed).astype(MM)

    half = pl.BlockSpec((seq, d // 2), lambda b: (b, 0))
    wspec = pl.BlockSpec(w_grp.shape, lambda b: (0, 0, 0))
    sspec = pl.BlockSpec((1, d // 2), lambda b: (0, 0))
    return pl.pallas_call(
        body, name="pool_bwd", grid=(bl,),
        in_specs=[pl.BlockSpec((seq, d // 2), lambda b: (b, 6)), half, wspec, sspec],
        out_specs=[half, wspec, sspec],
        out_shape=[jax.ShapeDtypeStruct((t, d // 2), MM), jax.ShapeDtypeStruct(w_grp.shape, F32),
                   jax.ShapeDtypeStruct((1, d // 2), F32)],
        compiler_params=_params(1))(proj, dy, w_grp, scale)


def _attn_probs(q, kk, dh):
    s = _dot_nt(q, kk) * (float(dh) ** -0.5)
    e = jnp.exp(s - jnp.max(s, axis=-1, keepdims=True))
    return e / jnp.sum(e, axis=-1, keepdims=True)


def _attn_fwd(proj, kv, d, bl, seq, mlen, *, tq=512):
    t = proj.shape[0]
    dh = d // 8
    tq = min(tq, seq)
    nq = seq // tq

    def body(q_ref, k_ref, v_ref, o_ref):
        a = _attn_probs(q_ref[...].astype(MM), k_ref[...].astype(MM), dh)
        o_ref[...] = _dot(a.astype(MM), v_ref[...].astype(MM)).astype(MM)

    return pl.pallas_call(
        body, name="attn_fwd", grid=(bl, HEADS, nq),
        in_specs=[pl.BlockSpec((tq, dh), lambda b, h, i: (b * nq + i, 7 * HEADS + h)),
                  pl.BlockSpec((mlen, dh), lambda b, h, i: (b, h)),
                  pl.BlockSpec((mlen, dh), lambda b, h, i: (b, HEADS + h))],
        out_specs=pl.BlockSpec((tq, dh), lambda b, h, i: (b * nq + i, h)),
        out_shape=jax.ShapeDtypeStruct((t, d // 2), MM),
        compiler_params=_params(3))(proj, kv, kv)


def _attn_bwd(proj, kv, do, d, bl, seq, mlen, *, tq=512):
    t = proj.shape[0]
    dh = d // 8
    tq = min(tq, seq)
    nq = seq // tq

    def body(q_ref, k_ref, v_ref, do_ref, dq_ref, dk_ref, dv_ref):
        @pl.when(pl.program_id(2) == 0)
        def _():
            dk_ref[...] = jnp.zeros_like(dk_ref)
            dv_ref[...] = jnp.zeros_like(dv_ref)

        q, kk, vv = q_ref[...].astype(MM), k_ref[...].astype(MM), v_ref[...].astype(MM)
        dov = do_ref[...].astype(MM)
        a = _attn_probs(q, kk, dh)
        dp = _dot_nt(dov, vv)
        ds = (a * (dp - jnp.sum(dp * a, axis=-1, keepdims=True)) * (float(dh) ** -0.5)).astype(MM)
        dq_ref[...] = _dot(ds, kk).astype(MM)
        dk_ref[...] += _dot_tn(ds, q)
        dv_ref[...] += _dot_tn(a.astype(MM), dov)

    qs = pl.BlockSpec((tq, dh), lambda b, h, i: (b * nq + i, h))
    ms = pl.BlockSpec((mlen, dh), lambda b, h, i: (b, h))
    return pl.pallas_call(
        body, name="attn_bwd", grid=(bl, HEADS, nq),
        in_specs=[pl.BlockSpec((tq, dh), lambda b, h, i: (b * nq + i, 7 * HEADS + h)), ms,
                  pl.BlockSpec((mlen, dh), lambda b, h, i: (b, HEADS + h)), qs],
        out_specs=[qs, ms, ms],
        out_shape=[jax.ShapeDtypeStruct((t, d // 2), MM), jax.ShapeDtypeStruct((bl * mlen, d // 2), F32),
                   jax.ShapeDtypeStruct((bl * mlen, d // 2), F32)],
        compiler_params=_params(3))(proj, kv, kv, do)


def _comm_call(name, body, arrays, out_shapes):
    n = len(arrays)
    hbm = pl.BlockSpec(memory_space=pl.ANY)
    return pl.pallas_call(
        body, name=name, out_shape=out_shapes, in_specs=[hbm] * n, out_specs=[hbm] * n,
        scratch_shapes=[pltpu.SemaphoreType.DMA((7 * n,)), pltpu.SemaphoreType.DMA((7 * n,)),
                        pltpu.SemaphoreType.DMA((n,))],
    )(*arrays)


def _all_gather(name, shards):
    n = len(shards)

    def body(*refs):
        x_refs, out_refs = refs[:n], refs[n:2 * n]
        send_sems, recv_sems, local_sems = refs[2 * n:]
        x, y, c = lax.axis_index("x"), lax.axis_index("y"), lax.axis_index("c")
        me, sibling = (x, y, c), (x, y, 1 - c)
        chips = [(1 - x, y), (x, 1 - y), (1 - x, 1 - y)]

        def copy(o, k, block, to, src=None):
            slot = out_refs[o].at[4 * block[0] + 2 * block[1] + block[2]]
            return pltpu.make_async_remote_copy(
                src_ref=slot if src is None else src, dst_ref=slot, send_sem=send_sems.at[7 * o + k],
                recv_sem=recv_sems.at[7 * o + k], device_id=to, device_id_type=MESH)

        locals_, remotes = [], []
        for o in range(n):
            mine = pltpu.make_async_copy(x_refs[o], out_refs[o].at[4 * x + 2 * y + c], local_sems.at[o])
            mine.start()
            locals_.append(mine)
            first = [copy(o, 0, me, sibling, src=x_refs[o])]
            first += [copy(o, 1 + j, me, (*chip, c), src=x_refs[o]) for j, chip in enumerate(chips)]
            for cp in first:
                cp.start()
            remotes += first
        for o in range(n):
            for j, chip in enumerate(chips):
                copy(o, 1 + j, (*chip, c), me).wait_recv()
                passed = copy(o, 4 + j, (*chip, c), sibling)
                passed.start()
                remotes.append(passed)
        for o in range(n):
            copy(o, 0, sibling, me).wait_recv()
            for j, chip in enumerate(chips):
                copy(o, 4 + j, (*chip, 1 - c), me).wait_recv()
        for cp in remotes:
            cp.wait_send()
        for mine in locals_:
            mine.wait()

    outs = [jax.ShapeDtypeStruct((N_DEV,) + s.shape, s.dtype) for s in shards]
    return _comm_call(name, body, shards, outs)


def _peer_of(k, x, y, c):
    peer = (1 - x if k & 4 else x, 1 - y if k & 2 else y, 1 - c if k & 1 else c)
    return peer, 4 * peer[0] + 2 * peer[1] + peer[2]


def _split_copies(scatter, srcs, lands, send_sems, recv_sems, arriving):
    x, y, c = lax.axis_index("x"), lax.axis_index("y"), lax.axis_index("c")
    me_idx = 4 * x + 2 * y + c
    copies = []
    for o, (src, land) in enumerate(zip(srcs, lands)):
        for k in range(1, N_DEV):
            peer, p_idx = _peer_of(k, x, y, c)
            mine = src.at[p_idx] if scatter else src
            sems = dict(send_sem=send_sems.at[7 * o + k - 1], recv_sem=recv_sems.at[7 * o + k - 1],
                        device_id=peer, device_id_type=MESH)
            slot = land.at[p_idx] if arriving else land.at[me_idx]
            copies.append(pltpu.make_async_remote_copy(src_ref=mine, dst_ref=slot, **sems))
    return copies


_HBM = pl.BlockSpec(memory_space=pltpu.HBM)
_SEM = pl.BlockSpec(memory_space=pltpu.SEMAPHORE)
_EFFECT = pltpu.SideEffectType.DATAFLOW_SIDE_EFFECTING


def _exchange_start(name, scatter, arrays, after=()):
    n = len(arrays)
    lands = [lax.empty(a.shape if scatter else (N_DEV,) + a.shape, a.dtype) for a in arrays]

    def body(*refs):
        srcs, lnds = refs[:n], refs[n:2 * n]
        send_sems, recv_sems = refs[2 * n + len(after)], refs[2 * n + len(after) + 1]
        token = refs[-1]
        for cp in _split_copies(scatter, srcs, lnds, send_sems, recv_sems, False):
            cp.start()
        token[...] = jnp.zeros_like(token)

    hbm_in = [pltpu.with_memory_space_constraint(a, pltpu.HBM) for a in list(arrays) + lands]
    res = pl.pallas_call(
        body, name=name,
        out_shape=(pltpu.SemaphoreType.DMA((7 * n,)), pltpu.SemaphoreType.DMA((7 * n,)),
                   *[pltpu.HBM(a.shape, a.dtype) for a in hbm_in], jax.ShapeDtypeStruct((8, 128), F32)),
        in_specs=[_HBM] * (2 * n) + [pl.BlockSpec(memory_space=pl.ANY)] * len(after),
        out_specs=(_SEM, _SEM, *[_HBM] * (2 * n), pl.BlockSpec(memory_space=pltpu.VMEM)),
        input_output_aliases={i: 2 + i for i in range(2 * n)},
        compiler_params=pltpu.CompilerParams(has_side_effects=_EFFECT),
    )(*hbm_in, *after)
    return res[0], res[1], list(res[2:2 + n]), list(res[2 + n:2 + 2 * n]), res[-1]


def _exchange_wait(name, scatter, started, after):
    send_sems, recv_sems, srcs, lands, _ = started
    n = len(srcs)

    def body(*refs):
        src_refs, lnd_refs = refs[:n], refs[n:2 * n]
        for cp in _split_copies(scatter, src_refs, lnd_refs, refs[2 * n], refs[2 * n + 1], False):
            cp.wait_send()
        for cp in _split_copies(scatter, src_refs, lnd_refs, refs[2 * n], refs[2 * n + 1], True):
            cp.wait_recv()

    res = pl.pallas_call(
        body, name=name, out_shape=tuple(pltpu.HBM(a.shape, a.dtype) for a in srcs + lands),
        in_specs=[_HBM] * (2 * n) + [_SEM, _SEM, pl.BlockSpec(memory_space=pl.ANY)],
        out_specs=tuple([_HBM] * (2 * n)), input_output_aliases={i: i for i in range(2 * n)},
        compiler_params=pltpu.CompilerParams(has_side_effects=_EFFECT),
    )(*srcs, *lands, send_sems, recv_sems, after)
    return list(res[n:]), list(res[:n])


def _adamw(name, parts, w, m, v, prev, layer, *, tr=256):
    _, a, b = w.shape
    tr = _tile(a, tr, 8)
    c1 = 1.0 - ADAM_B1 ** ADAM_STEP
    c2 = 1.0 - ADAM_B2 ** ADAM_STEP

    def body(p_ref, w_ref, m_ref, v_ref, _g, _d, _m, _v, g_out, d_out, m_out, v_out):
        g = p_ref[0].astype(F32)
        for s in range(1, N_DEV):
            g = g + p_ref[s].astype(F32)
        mn = ADAM_B1 * m_ref[...] + (1.0 - ADAM_B1) * g
        vn = ADAM_B2 * v_ref[...] + (1.0 - ADAM_B2) * (g * g)
        g_out[...] = g
        m_out[...] = mn
        v_out[...] = vn
        d_out[...] = -ADAM_LR * ((mn / c1) / (jnp.sqrt(vn / c2) + ADAM_EPS) + ADAM_WD * w_ref[...])

    slab = pl.BlockSpec((None, tr, b), lambda i: (layer, i, 0))
    whole = pl.BlockSpec(memory_space=pl.ANY)
    return pl.pallas_call(
        body, name=name, grid=(a // tr,),
        in_specs=[pl.BlockSpec((N_DEV, tr, b), lambda i: (0, i, 0)), slab, slab, slab] + [whole] * 4,
        out_specs=[slab] * 4, out_shape=[jax.ShapeDtypeStruct(w.shape, F32)] * 4,
        input_output_aliases={4: 0, 5: 1, 6: 2, 7: 3},
        compiler_params=_params(1))(parts, w, m, v, *prev)


_COL = ("w_in", "w_pool_o", "w_mem_o", "w_ff1")
_COL_IN_PLACE = ("w_in", "w_ff1")
_BIG =("w_in", "w_ret_o", "w_pool_o", "w_mem_kv", "w_mem_o", "w_out", "w_ff1", "w_ff2")
_SMALL = ("ret_decay_logit", "w_pool_grp", "pool_scale", "norm1_g", "norm2_g", "mem_norm_g", "final_norm_g")
_WEIGHTS = ("w_in", "ret_decay_logit", "w_ret_o", "w_pool_grp", "pool_scale", "w_pool_o", "w_mem_kv", "w_mem_o",
            "w_out", "w_ff1", "w_ff2", "norm1_g", "norm2_g", "mem_norm_g", "final_norm_g")


def _small_rows(size, d):
    return -(-size // (8 * d)) * 8


def _pack_small(ws, d):
    parts = []
    for n in _SMALL:
        flat = ws[n].reshape(-1)
        rows = _small_rows(flat.shape[0], d)
        parts.append(jnp.pad(flat, (0, rows * d - flat.shape[0])).reshape(rows, d))
    return jnp.concatenate(parts, axis=0)[None]


def _unpack_small(packed, like, d):
    out, off = {}, 0
    for n in _SMALL:
        rows = _small_rows(like[n].size, d)
        out[n] = packed[0, off:off + rows].reshape(-1)[:like[n].size].reshape(like[n].shape)
        off += rows
    return out


def kernel(x, mem, w_in, ret_decay_logit, w_ret_o, w_pool_grp, pool_scale, w_pool_o, w_mem_kv, w_mem_o, w_out, w_ff1, w_ff2, norm1_g, norm2_g, mem_norm_g, final_norm_g, loss_target, m_w_in, m_ret_decay_logit, m_w_ret_o, m_w_pool_grp, m_pool_scale, m_w_pool_o, m_w_mem_kv, m_w_mem_o, m_w_out, m_w_ff1, m_w_ff2, m_norm1_g, m_norm2_g, m_mem_norm_g, m_final_norm_g, v_w_in, v_ret_decay_logit, v_w_ret_o, v_w_pool_grp, v_pool_scale, v_w_pool_o, v_w_mem_kv, v_w_mem_o, v_w_out, v_w_ff1, v_w_ff2, v_norm1_g, v_norm2_g, v_mem_norm_g, v_final_norm_g):
    w = dict(w_in=w_in, ret_decay_logit=ret_decay_logit, w_ret_o=w_ret_o, w_pool_grp=w_pool_grp,
             pool_scale=pool_scale, w_pool_o=w_pool_o, w_mem_kv=w_mem_kv, w_mem_o=w_mem_o, w_out=w_out,
             w_ff1=w_ff1, w_ff2=w_ff2, norm1_g=norm1_g, norm2_g=norm2_g, mem_norm_g=mem_norm_g,
             final_norm_g=final_norm_g)
    mom = dict(w_in=m_w_in, ret_decay_logit=m_ret_decay_logit, w_ret_o=m_w_ret_o, w_pool_grp=m_w_pool_grp,
               pool_scale=m_pool_scale, w_pool_o=m_w_pool_o, w_mem_kv=m_w_mem_kv, w_mem_o=m_w_mem_o,
               w_out=m_w_out, w_ff1=m_w_ff1, w_ff2=m_w_ff2, norm1_g=m_norm1_g, norm2_g=m_norm2_g,
               mem_norm_g=m_mem_norm_g, final_norm_g=m_final_norm_g)
    vel = dict(w_in=v_w_in, ret_decay_logit=v_ret_decay_logit, w_ret_o=v_w_ret_o, w_pool_grp=v_w_pool_grp,
               pool_scale=v_pool_scale, w_pool_o=v_w_pool_o, w_mem_kv=v_w_mem_kv, w_mem_o=v_w_mem_o,
               w_out=v_w_out, w_ff1=v_w_ff1, w_ff2=v_w_ff2, norm1_g=v_norm1_g, norm2_g=v_norm2_g,
               mem_norm_g=v_mem_norm_g, final_norm_g=v_final_norm_g)

    bl, seq, d = x.shape
    mlen = mem.shape[1]
    depth = w_in.shape[0]
    t = bl * seq
    dk = d // 8

    me_idx = 4 * lax.axis_index("x") + 2 * lax.axis_index("y") + lax.axis_index("c")

    def natural(n, g):
        if n in _COL_IN_PLACE:
            return g
        if n in _COL:
            return jnp.transpose(g, (1, 0, 2)).reshape(g.shape[1], -1)
        return g.reshape(-1, g.shape[-1])

    def finish_gather(name, names, started, after):
        got, mine = _exchange_wait(name, False, started, after)
        return {n: natural(n, lax.dynamic_update_slice(g, sh[None], (me_idx, 0, 0)))
                for n, g, sh in zip(names, got, mine)}

    shards = [{n: w[n][l].astype(MM) for n in _BIG} for l in range(depth)]
    rest = _BIG[1:]
    (w_in0,) = _all_gather("gather_w_in", [shards[0][_BIG[0]]])
    full = [{_BIG[0]: w_in0}]
    pending = _exchange_start("gather_start_0", False, [shards[0][n] for n in rest], after=[w_in0])

    inv = ROPE_BASE ** (-jnp.arange(0, dk, 2, dtype=F32) / dk)
    ang = jnp.arange(seq, dtype=F32)[:, None] * inv[None, :]
    cos2 = jnp.concatenate([jnp.cos(ang), jnp.cos(ang)], axis=-1)
    sin2 = jnp.concatenate([-jnp.sin(ang), jnp.sin(ang)], axis=-1)
    log_g = jax.nn.log_sigmoid(ret_decay_logit)
    x2 = x.reshape(t, d)
    mem2 = mem.reshape(bl * mlen, d)
    gmem = mem_norm_g.reshape(1, d)

    def merge(o_r, o_p, o_m, g_r, g_p, g_m):
        f = lambda z: z.astype(F32)
        return _sigmoid(f(g_r)) * f(o_r) + _sigmoid(f(g_p)) * f(o_p) + _sigmoid(f(g_m)) * f(o_m)

    def relu2(u):
        r = jnp.maximum(u, 0.0)
        return r * r

    def ident(a):
        return a

    saved = []
    xc = x2
    for l in range(depth):
        s = dict(x_in=xc)
        started_now = ()
        if l > 0:
            full.append(finish_gather(f"gather_wait_{l}", _BIG, pending, xc))
            if l + 1 < depth:
                pending = _exchange_start(f"gather_start_{l + 1}", False, [shards[l + 1][n] for n in _BIG],
                                          after=[full[l]["w_in"]])
                started_now = (pending[4],)
        fw = full[l]
        g1 = norm1_g[l].reshape(1, d)
        g2 = norm2_g[l].reshape(1, d)
        s["proj"], s["h1"] = _pmm("proj", _rms_prologue, [(xc, d, 0)], [g1], fw["w_in"], w_mode="col",
                                  tm=1024, tn=1024, save_a=True, out_dtypes=(MM,), after=started_now)
        proj = s["proj"]
        s["qr"], s["kr"], s["vb"] = _ret_pre(proj, cos2, sin2, d, seq)
        s["o_raw"], s["a_ret"] = _ret_core_fwd(s["qr"], s["kr"], s["vb"], proj, log_g[l], d, bl, seq)
        s["y"] = _pool_fwd(proj, w_pool_grp[l], pool_scale[l].reshape(1, -1), d, bl, seq)
        started_now = ()
        if l == 0:
            fw.update(finish_gather("gather_wait_0", rest, pending, s["a_ret"]))
            if depth > 1:
                pending = _exchange_start("gather_start_1", False, [shards[1][n] for n in _BIG],
                                          after=[fw["w_mem_kv"]])
                started_now = (pending[4],)
        s["kv"], s["memn"] = _pmm("mem_kv", _rms_prologue, [(mem2, d, 0)], [gmem], fw["w_mem_kv"],
                                  tm=512, tn=512, save_a=True, after=started_now)
        s["o_att"] = _attn_fwd(proj, s["kv"], d, bl, seq, mlen)
        (s["o_ret"],) = _pmm("ret_o", None, [(s["a_ret"], d, 0)], [], fw["w_ret_o"], tm=1024, tn=512,
                             out_dtypes=(MM,))
        (s["o_pool"],) = _pmm("pool_o", None, [(s["y"], d // 2, 0)], [], fw["w_pool_o"],
                              tm=1024, tn=512, out_dtypes=(MM,))
        (s["o_mem"],) = _pmm("mem_o", None, [(s["o_att"], d // 2, 0)], [], fw["w_mem_o"],
                             tm=1024, tn=512, out_dtypes=(MM,))
        s["x_mid"], s["merged"] = _pmm(
            "merge_out", merge,
            [(s["o_ret"], d, 0), (s["o_pool"], d, 0), (s["o_mem"], d, 0), (proj, d, 4), (proj, d, 5), (proj, d, 6)],
            [], fw["w_out"], tm=512, tn=512, residual=xc, save_a=True)
        s["u"], s["h2"] = _pmm("ff1", _rms_prologue, [(s["x_mid"], d, 0)], [g2], fw["w_ff1"], w_mode="col",
                               tm=1024, tn=512, save_a=True)
        xc, s["a"] = _pmm("ff2", relu2, [(s["u"], s["u"].shape[1], 0)], [], fw["w_ff2"],
                          tm=512, tn=512, residual=s["x_mid"], save_a=True)
        saved.append(s)

    dxc, g_final, loss_part = _loss_head(xc, loss_target.reshape(t, d), final_norm_g.reshape(1, d))
    loss = lax.psum(loss_part[0, 0], ("x", "y", "c"))

    small_names = ("w_pool_grp", "pool_scale", "norm1_g", "norm2_g", "ret_decay_logit")
    grads = {n: [None] * depth for n in small_names}
    group_a = ("w_ff1", "w_ff2")
    group_b = tuple(n for n in _BIG if n not in group_a)
    scatters = {}
    dmemn = jnp.zeros((bl * mlen, d), F32)

    def relu2_bwd(acc, u):
        return (acc * (2.0 * jnp.maximum(u, 0.0)),)

    def gates_bwd(acc, g_r, g_p, g_m, o_r, o_p, o_m):
        outs_o, outs_g = [], []
        for gz, oz in ((g_r, o_r), (g_p, o_p), (g_m, o_m)):
            sg = _sigmoid(gz.astype(F32))
            outs_o.append(acc * sg)
            outs_g.append(acc * oz.astype(F32) * (sg * (1.0 - sg)))
        return tuple(outs_o + outs_g)

    def to_send(n, g):
        a, b = w[n].shape[1:]
        if n in _COL_IN_PLACE:
            return g
        if n in _COL:
            return jnp.transpose(g.reshape(a, N_DEV, b), (1, 0, 2))
        return g.reshape(N_DEV, a, b)

    for l in reversed(range(depth)):
        s = saved[l]
        fw = full[l]
        proj = s["proj"]
        g1 = norm1_g[l].reshape(1, d)
        g2 = norm2_g[l].reshape(1, d)
        dw = {}
        (du,) = _pmm("ff2_bwd", ident, [(dxc, d, 0)], [], fw["w_ff2"], w_mode="nt", tm=1024, tn=1024,
                     epilogue=relu2_bwd, epi_ins=[(s["u"], 0)], out_dtypes=(MM,))
        dw["w_ff2"] = _tnmm("dw_ff2", s["a"], dxc)
        (dh2,) = _pmm("ff1_bwd", None, [(du, du.shape[1], 0)], [], fw["w_ff1"], w_mode="col_t", tm=1024, tn=512)
        dw["w_ff1"] = _tnmm("dw_ff1", s["h2"], du, col_shards=True)
        scatters[l, "a"] = _exchange_start(f"scatter_start_a{l}", True, [to_send(n, dw[n]) for n in group_a])
        dmid, grads["norm2_g"][l] = _rms_bwd("norm2_bwd", dh2, s["x_mid"], g2, dxc)
        d_oret, d_opool, d_omem, dgr, dgp, dgm = _pmm(
            "out_bwd", ident, [(dmid, d, 0)], [], fw["w_out"], w_mode="nt", tm=512, tn=512, epilogue=gates_bwd,
            epi_ins=[(proj, 4 * d), (proj, 5 * d), (proj, 6 * d), (s["o_ret"], 0), (s["o_pool"], 0), (s["o_mem"], 0)],
            out_dtypes=(MM,) * 6, after=(scatters[l, "a"][4],))
        dw["w_out"] = _tnmm("dw_out", s["merged"], dmid)
        (da_ret,) = _pmm("ret_o_bwd", None, [(d_oret, d, 0)], [], fw["w_ret_o"], w_mode="nt", tm=1024, tn=512)
        dw["w_ret_o"] = _tnmm("dw_ret_o", s["a_ret"], d_oret)
        (dy,) = _pmm("pool_o_bwd", None, [(d_opool, d, 0)], [], fw["w_pool_o"], w_mode="nt", tm=1024, tn=512)
        dw["w_pool_o"] = _tnmm("dw_pool_o", s["y"], d_opool)
        (do_att,) = _pmm("mem_o_bwd", None, [(d_omem, d, 0)], [], fw["w_mem_o"], w_mode="nt", tm=1024, tn=512)
        dw["w_mem_o"] = _tnmm("dw_mem_o", s["o_att"], d_omem)
        dg_ret, do_ret = _ret_post_bwd(da_ret, proj, s["o_raw"], d)
        dq, dkk, dvv, dlf, dlb = _ret_core_bwd(s["qr"], s["kr"], s["vb"], do_ret, cos2, sin2, log_g[l], d, bl, seq)
        dl = jnp.stack([dlf[:, 0, 0].reshape(bl, HEADS).sum(0), dlb[:, 0, 0].reshape(bl, HEADS).sum(0)])
        grads["ret_decay_logit"][l] = dl * jax.nn.sigmoid(-ret_decay_logit[l])
        dp, grads["w_pool_grp"][l], dscale = _pool_bwd(proj, dy, w_pool_grp[l], pool_scale[l].reshape(1, -1),
                                                       d, bl, seq)
        grads["pool_scale"][l] = dscale.reshape(-1)
        dqm, dmk, dmv = _attn_bwd(proj, s["kv"], do_att, d, bl, seq, mlen)
        dkv = jnp.concatenate([dmk, dmv], axis=-1).astype(MM)
        dw["w_mem_kv"] = _tnmm("dw_mem_kv", s["memn"], dkv)
        (dmemn,) = _pmm("mem_kv_bwd", None, [(dkv, d, 0)], [], fw["w_mem_kv"], w_mode="nt", tm=512, tn=512,
                        residual=dmemn)
        dproj = jnp.concatenate([dq, dkk, dvv, dg_ret, dp, dqm, dgr, dgp, dgm], axis=-1)
        dw["w_in"] = _tnmm("dw_in", s["h1"], dproj, col_shards=True)
        scatters[l, "b"] = _exchange_start(f"scatter_start_b{l}", True, [to_send(n, dw[n]) for n in group_b])
        (dh1,) = _pmm("proj_bwd", None, [(dproj, dproj.shape[1], 0)], [], fw["w_in"], w_mode="col_t",
                      tm=512, tn=512, after=(scatters[l, "b"][4],))
        dxc, grads["norm1_g"][l] = _rms_bwd("norm1_bwd", dh1, s["x_in"], g1, dmid)

    _, g_memn = _rms_bwd("mem_norm_bwd", dmemn, mem2, gmem, None)
    grad_x = dxc.reshape(bl, seq, d)

    small_g = dict(ret_decay_logit=jnp.stack(grads["ret_decay_logit"]), w_pool_grp=jnp.stack(grads["w_pool_grp"]),
                   pool_scale=jnp.stack(grads["pool_scale"]),
                   norm1_g=jnp.concatenate(grads["norm1_g"], axis=0), norm2_g=jnp.concatenate(grads["norm2_g"], axis=0),
                   mem_norm_g=g_memn.reshape(-1), final_norm_g=g_final.reshape(-1))
    (small_parts,) = _all_gather("gather_small_grads", [_pack_small(small_g, d)[0]])
    w_small = _pack_small(w, d)
    small = _adamw("adamw_small", small_parts, w_small, _pack_small(mom, d), _pack_small(vel, d),
                   [lax.empty(w_small.shape, F32) for _ in range(4)], 0)
    small = [_unpack_small(o, w, d) for o in small]

    big = {n: [lax.empty(w[n].shape, F32) for _ in range(4)] for n in _BIG}
    after = small[0]["w_pool_grp"]
    for l in reversed(range(depth)):
        for grp, names in (("a", group_a), ("b", group_b)):
            recv, sent = _exchange_wait(f"scatter_wait_{grp}{l}", True, scatters[l, grp], after)
            for n, r, snt in zip(names, recv, sent):
                own = lax.dynamic_slice_in_dim(snt, me_idx, 1, axis=0)
                parts = lax.dynamic_update_slice(r, own, (me_idx, 0, 0))
                big[n] = _adamw("adamw_" + n, parts, w[n], mom[n], vel[n], big[n], l)
            after = big[names[-1]][0]

    outs = [loss, grad_x]
    for k in range(4):
        outs += [big[n][k] if n in _BIG else small[k][n] for n in _WEIGHTS]
    return tuple(outs)
```

```python
import jax
import jax.numpy as jnp
from jax import lax
from jax.experimental import pallas as pl
from jax.experimental.pallas import tpu as pltpu

F32 = jnp.float32
MM = jnp.bfloat16
N_DEV = 8
HEADS = 4
POOL_WINDOWS = (2, 4, 8, 16)
EPS = 1e-6
ROPE_BASE = 10000.0
ADAM_LR, ADAM_B1, ADAM_B2, ADAM_EPS, ADAM_WD, ADAM_STEP = 0.001, 0.9, 0.999, 1e-08, 0.01, 10
V7X_VMEM_LIMIT = 56 * 1024 * 1024
MESH = pl.DeviceIdType.MESH


def _params(n_axes):
    return pltpu.CompilerParams(dimension_semantics=("arbitrary",) * n_axes,
                                vmem_limit_bytes=V7X_VMEM_LIMIT)


def _tile(n, pref, align=128):
    cands = [c for c in range(align, min(pref, n) + 1, align) if n % c == 0]
    return max(cands) if cands else n


def _sigmoid(z):
    return 0.5 * jnp.tanh(0.5 * z) + 0.5


def _dot(a, b):
    return jnp.dot(a, b, preferred_element_type=F32)


def _dot_nt(a, b):
    return lax.dot_general(a, b, (((1,), (1,)), ((), ())), preferred_element_type=F32)


def _dot_tn(a, b):
    return lax.dot_general(a, b, (((0,), (0,)), ((), ())), preferred_element_type=F32)


def _pmm(name, prologue, row_ins, vec_ins, w, *, tm, tn, w_mode="nn", residual=None, save_a=False,
         epilogue=None, epi_ins=(), out_dtypes=(F32,), after=()):
    m = row_ins[0][0].shape[0]
    wb = None
    if w_mode == "nn":
        k, n = w.shape
        tn = _tile(n, tn)
        w_spec = pl.BlockSpec((k, tn), lambda i, j: (0, j))
    elif w_mode == "nt":
        n, k = w.shape
        tn = _tile(n, tn)
        w_spec = pl.BlockSpec((tn, k), lambda i, j: (j, 0))
    elif w_mode == "col":
        _, k, wb = w.shape
        n = N_DEV * wb
        tn = _tile(wb, tn)
        w_spec = pl.BlockSpec((None, k, tn), lambda i, j, q=wb // tn: (j // q, 0, j % q))
    else:
        _, n, wb = w.shape
        k = N_DEV * wb
        tn = _tile(n, tn)
        w_spec = pl.BlockSpec((N_DEV, tn, wb), lambda i, j: (0, j, 0))
    tm = _tile(m, tm, 8)
    n_row, n_vec, n_epi, n_out = len(row_ins), len(vec_ins), len(epi_ins), len(out_dtypes)
    has_res = residual is not None
    use_scr = prologue is not None

    def body(*refs):
        row_refs = refs[:n_row]
        p = n_row
        vec_refs = refs[p:p + n_vec]
        p += n_vec
        w_ref = refs[p]
        p += 1
        res_ref = refs[p] if has_res else None
        p += int(has_res)
        epi_refs = refs[p:p + n_epi]
        p += n_epi + len(after)
        out_refs = refs[p:p + n_out]
        p += n_out
        a_out = refs[p] if save_a else None
        p += int(save_a)
        if use_scr:
            a_src = refs[p]

            @pl.when(pl.program_id(1) == 0)
            def _():
                a = prologue(*[r[...] for r in row_refs], *[v[...] for v in vec_refs]).astype(MM)
                a_src[...] = a
                if save_a:
                    a_out[...] = a
        else:
            a_src = row_refs[0]
        if w_mode == "nt":
            acc = _dot_nt(a_src[...], w_ref[...])
        elif w_mode == "col_t":
            acc = _dot_nt(a_src[:, 0:wb], w_ref[0])
            for dev in range(1, N_DEV):
                acc = acc + _dot_nt(a_src[:, dev * wb:(dev + 1) * wb], w_ref[dev])
        else:
            acc = _dot(a_src[...], w_ref[...])
        if has_res:
            acc = acc + res_ref[...]
        outs = epilogue(acc, *[e[...] for e in epi_refs]) if epilogue is not None else (acc,)
        for o_ref, o in zip(out_refs, outs):
            o_ref[...] = o.astype(o_ref.dtype)

    in_specs = [pl.BlockSpec((tm, wd), lambda i, j, cb=cb: (i, cb)) for (_, wd, cb) in row_ins]
    in_specs += [pl.BlockSpec(v.shape, lambda i, j: (0, 0)) for v in vec_ins]
    in_specs += [w_spec]
    args = [r[0] for r in row_ins] + list(vec_ins) + [w]
    if has_res:
        in_specs.append(pl.BlockSpec((tm, tn), lambda i, j: (i, j)))
        args.append(residual)
    for (arr, off) in epi_ins:
        assert off % tn == 0
        in_specs.append(pl.BlockSpec((tm, tn), lambda i, j, ob=off // tn: (i, ob + j)))
        args.append(arr)
    n_after = len(after)
    in_specs += [pl.BlockSpec(memory_space=pl.ANY)] * n_after
    args += list(after)
    out_specs = [pl.BlockSpec((tm, tn), lambda i, j: (i, j)) for _ in out_dtypes]
    out_shape = [jax.ShapeDtypeStruct((m, n), dt) for dt in out_dtypes]
    if save_a:
        out_specs.append(pl.BlockSpec((tm, k), lambda i, j: (i, 0)))
        out_shape.append(jax.ShapeDtypeStruct((m, k), MM))
    scratch = [pltpu.VMEM((tm, k), MM)] if use_scr else []
    return pl.pallas_call(body, name=name, grid=(m // tm, n // tn), in_specs=in_specs,
                          out_specs=out_specs, out_shape=out_shape, scratch_shapes=scratch,
                          compiler_params=_params(2))(*args)


def _tnmm(name, a, b, *, tm=1024, tn=1024, tk=1024, col_shards=False):
    t, m = a.shape
    n = b.shape[1]
    tm, tk = _tile(m, tm), _tile(t, tk, 8)
    per_tile = 1
    if col_shards:
        wb = n // N_DEV
        while 2 * per_tile * wb <= tn and 2 * per_tile <= N_DEV:
            per_tile *= 2
        tn = per_tile * wb
        out_spec = pl.BlockSpec((per_tile, tm, wb), lambda i, j, kk: (j, i, 0))
        out_shape = jax.ShapeDtypeStruct((N_DEV, m, wb), MM)
    else:
        tn = _tile(n, tn)
        out_spec = pl.BlockSpec((tm, tn), lambda i, j, kk: (i, j))
        out_shape = jax.ShapeDtypeStruct((m, n), MM)
    nk = t // tk

    def body(a_ref, b_ref, o_ref, acc):
        kk = pl.program_id(2)

        @pl.when(kk == 0)
        def _():
            acc[...] = jnp.zeros_like(acc)

        acc[...] += _dot_tn(a_ref[...].astype(MM), b_ref[...].astype(MM))

        @pl.when(kk == nk - 1)
        def _():
            if col_shards:
                for sh in range(per_tile):
                    o_ref[sh] = acc[:, sh * wb:(sh + 1) * wb].astype(o_ref.dtype)
            else:
                o_ref[...] = acc[...].astype(o_ref.dtype)

    return pl.pallas_call(
        body, name=name, grid=(m // tm, n // tn, nk),
        in_specs=[pl.BlockSpec((tk, tm), lambda i, j, kk: (kk, i)),
                  pl.BlockSpec((tk, tn), lambda i, j, kk: (kk, j))],
        out_specs=out_spec, out_shape=out_shape,
        scratch_shapes=[pltpu.VMEM((tm, tn), F32)],
        compiler_params=_params(3))(a, b)


def _rms_prologue(x, g):
    r = lax.rsqrt(jnp.mean(x * x, axis=-1, keepdims=True) + EPS)
    return x * r * g


def _rms_bwd_rows(dh, x, g):
    d = x.shape[-1]
    r = lax.rsqrt(jnp.mean(x * x, axis=-1, keepdims=True) + EPS)
    xh = x * r
    dxh = dh * g
    dx = r * (dxh - xh * (jnp.sum(dxh * xh, axis=-1, keepdims=True) / d))
    dg = jnp.sum(dh * xh, axis=0, keepdims=True)
    return dx, dg


def _rms_bwd(name, dh, x, g, dres, *, tm=512):
    m, d = x.shape
    tm = min(tm, m)
    has_res = dres is not None

    def body(*refs):
        if has_res:
            dh_ref, x_ref, g_ref, r_ref, dx_ref, dg_ref = refs
        else:
            dh_ref, x_ref, g_ref, dx_ref, dg_ref = refs
        dx, dg = _rms_bwd_rows(dh_ref[...], x_ref[...], g_ref[...])
        if has_res:
            dx = dx + r_ref[...]
        dx_ref[...] = dx

        @pl.when(pl.program_id(0) == 0)
        def _():
            dg_ref[...] = jnp.zeros_like(dg_ref)

        dg_ref[...] += dg

    row = pl.BlockSpec((tm, d), lambda i: (i, 0))
    vec = pl.BlockSpec((1, d), lambda i: (0, 0))
    in_specs = [row, row, vec] + ([row] if has_res else [])
    args = [dh, x, g] + ([dres] if has_res else [])
    return pl.pallas_call(body, name=name, grid=(m // tm,), in_specs=in_specs, out_specs=[row, vec],
                          out_shape=[jax.ShapeDtypeStruct((m, d), F32), jax.ShapeDtypeStruct((1, d), F32)],
                          compiler_params=_params(1))(*args)


def _loss_head(x, target, g, *, tm=256):
    m, d = x.shape
    tm = min(tm, m)

    def body(x_ref, t_ref, g_ref, dx_ref, dg_ref, loss_ref):
        xv, gv = x_ref[...], g_ref[...]
        y = _rms_prologue(xv, gv)
        err = y - t_ref[...]
        part = 0.5 * jnp.sum(jnp.sum(err * err, axis=-1, keepdims=True) / d)
        dx, dg = _rms_bwd_rows(err / d, xv, gv)
        dx_ref[...] = dx

        @pl.when(pl.program_id(0) == 0)
        def _():
            dg_ref[...] = jnp.zeros_like(dg_ref)
            loss_ref[...] = jnp.zeros_like(loss_ref)

        dg_ref[...] += dg
        loss_ref[...] += jnp.full(loss_ref.shape, part, F32)

    row = pl.BlockSpec((tm, d), lambda i: (i, 0))
    vec = pl.BlockSpec((1, d), lambda i: (0, 0))
    lspec = pl.BlockSpec((1, 128), lambda i: (0, 0))
    return pl.pallas_call(body, name="loss_head", grid=(m // tm,), in_specs=[row, row, vec],
                          out_specs=[row, vec, lspec],
                          out_shape=[jax.ShapeDtypeStruct((m, d), F32), jax.ShapeDtypeStruct((1, d), F32),
                                     jax.ShapeDtypeStruct((1, 128), F32)],
                          compiler_params=_params(1))(x, target, g)


def _rot(xv, cos2, sin2, half):
    return xv * cos2 + pltpu.roll(xv, half, 1) * sin2


def _rot_t(dv, cos2, sin2, half):
    return dv * cos2 + pltpu.roll(dv * sin2, half, 1)


def _ret_pre(proj, cos2, sin2, d, seq, *, ts=512):
    t = proj.shape[0]
    ts = min(ts, seq)
    dk = d // 8
    ns = seq // ts
    scale = float(dk) ** -0.5

    def body(q_ref, k_ref, v_ref, c_ref, s_ref, qo, ko, vo):
        c, s = c_ref[...], s_ref[...]
        for h in range(HEADS):
            sl = slice(h * dk, (h + 1) * dk)
            qo[:, sl] = _rot(q_ref[:, sl].astype(F32), c, s, dk // 2).astype(MM)
            ko[:, sl] = (_rot(k_ref[:, sl].astype(F32), c, s, dk // 2) * scale).astype(MM)
        vo[...] = v_ref[...].astype(MM)

    half = pl.BlockSpec((ts, d // 2), lambda i: (i, 0))
    tab = pl.BlockSpec((ts, dk), lambda i: (i % ns, 0))
    return pl.pallas_call(
        body, name="ret_pre", grid=(t // ts,),
        in_specs=[half, pl.BlockSpec((ts, d // 2), lambda i: (i, 1)), pl.BlockSpec((ts, d), lambda i: (i, 1)),
                  tab, tab],
        out_specs=[half, half, pl.BlockSpec((ts, d), lambda i: (i, 0))],
        out_shape=[jax.ShapeDtypeStruct((t, d // 2), MM), jax.ShapeDtypeStruct((t, d // 2), MM),
                   jax.ShapeDtypeStruct((t, d), MM)],
        compiler_params=_params(1))(proj, proj, proj, cos2, sin2)


def _ret_consts(lg_ref, h, t, dk):
    lf, lb = lg_ref[0, h], lg_ref[1, h]
    ab = (lax.broadcasted_iota(jnp.int32, (t, t), 0) - lax.broadcasted_iota(jnp.int32, (t, t), 1)).astype(F32)
    dmat = jnp.exp(jnp.where(ab >= 0, lf * ab, -lb * ab))
    up = lax.broadcasted_iota(jnp.int32, (t, dk), 0).astype(F32) + 1.0
    down = float(t) - up
    one = jnp.ones((1, 1), F32)
    return dict(ab=ab, dmat=dmat, xi_f=jnp.exp(lf * up), zeta_f=jnp.exp(lf * down), xi_b=jnp.exp(lb * up),
                zeta_b=jnp.exp(lb * down), up=up[:, 0:1], down=down[:, 0:1],
                cf=jnp.exp(one * (lf * t)), cb=jnp.exp(one * (lb * t)))


def _scaled(xv, rows):
    return (xv.astype(F32) * rows).astype(MM)


def _ret_core_fwd(qr, kr, vb, proj, lg, d, bl, seq, *, tc=256):
    t = qr.shape[0]
    dk, dv = d // 8, d // 4
    tc = min(tc, seq)
    nc = seq // tc

    def body(lg_ref, q_ref, k_ref, v_ref, g_ref, o_ref, a_ref, sf_ref, sb_ref):
        c = _ret_consts(lg_ref, pl.program_id(1), tc, dk)

        def rows_of(i):
            return pl.ds(pl.multiple_of(i * tc, tc), tc)

        def fwd_step(i, sf):
            rows = rows_of(i)
            sf_ref[i] = sf
            q, kk, v = q_ref[rows, :], k_ref[rows, :], v_ref[rows, :]
            p = (_dot_nt(q, kk) * c["dmat"]).astype(MM)
            o_ref[rows, :] = _dot(p, v) + _dot(_scaled(q, c["xi_f"]), sf.astype(MM))
            return sf * c["cf"] + _dot_tn(_scaled(kk, c["zeta_f"]), v)

        lax.fori_loop(0, nc, fwd_step, jnp.zeros((dk, dv), F32))

        def bwd_step(ii, sb):
            rows = rows_of(nc - 1 - ii)
            sb_ref[nc - 1 - ii] = sb
            q, kk, v = q_ref[rows, :], k_ref[rows, :], v_ref[rows, :]
            o_ref[rows, :] += _dot(_scaled(q, c["zeta_b"]), sb.astype(MM))
            return sb * c["cb"] + _dot_tn(_scaled(kk, c["xi_b"]), v)

        lax.fori_loop(0, nc, bwd_step, jnp.zeros((dk, dv), F32))

        def post(i, carry):
            rows = rows_of(i)
            o = o_ref[rows, :]
            oc = o - jnp.mean(o, axis=-1, keepdims=True)
            on = oc * lax.rsqrt(jnp.mean(oc * oc, axis=-1, keepdims=True) + EPS)
            g = g_ref[rows, :].astype(F32)
            a_ref[rows, :] = (on * (g * _sigmoid(g))).astype(MM)
            return carry

        lax.fori_loop(0, nc, post, 0)

    qk = pl.BlockSpec((seq, dk), lambda b, h: (b, h))
    vv = pl.BlockSpec((seq, dv), lambda b, h: (b, h))
    states = pl.BlockSpec((None, nc, dk, dv), lambda b, h: (b * HEADS + h, 0, 0, 0))
    return pl.pallas_call(
        body, name="ret_core_fwd", grid=(bl, HEADS),
        in_specs=[pl.BlockSpec(memory_space=pltpu.SMEM), qk, qk, vv,
                  pl.BlockSpec((seq, dv), lambda b, h: (b, 2 * HEADS + h))],
        out_specs=[vv, vv, states, states],
        out_shape=[jax.ShapeDtypeStruct((t, d), F32), jax.ShapeDtypeStruct((t, d), MM),
                   jax.ShapeDtypeStruct((bl * HEADS, nc, dk, dv), F32),
                   jax.ShapeDtypeStruct((bl * HEADS, nc, dk, dv), F32)],
        compiler_params=_params(2))(lg, qr, kr, vb, proj)


def _ret_post_bwd(da, proj, o_raw, d, *, ts=2048):
    t = da.shape[0]
    dv = d // 4
    ts = min(ts, t)

    def body(da_ref, g_ref, o_ref, dg_ref, do_ref):
        o, g, dav = o_ref[...], g_ref[...].astype(F32), da_ref[...]
        mu = jnp.mean(o, axis=-1, keepdims=True)
        oc = o - mu
        r = lax.rsqrt(jnp.mean(oc * oc, axis=-1, keepdims=True) + EPS)
        on = oc * r
        sg = _sigmoid(g)
        don = dav * (g * sg)
        dg_ref[...] = (dav * on * (sg * (1.0 + g * (1.0 - sg)))).astype(MM)
        do = r * (don - jnp.mean(don, axis=-1, keepdims=True) - on * jnp.mean(don * on, axis=-1, keepdims=True))
        do_ref[...] = do.astype(MM)

    blk = pl.BlockSpec((ts, dv), lambda i, h: (i, h))
    return pl.pallas_call(
        body, name="ret_post_bwd", grid=(t // ts, HEADS),
        in_specs=[blk, pl.BlockSpec((ts, dv), lambda i, h: (i, 2 * HEADS + h)), blk],
        out_specs=[blk, blk],
        out_shape=[jax.ShapeDtypeStruct((t, d), MM), jax.ShapeDtypeStruct((t, d), MM)],
        compiler_params=_params(2))(da, proj, o_raw)


def _ret_core_bwd(qr, kr, vb, do, sf_in, sb_in, cos2, sin2, lg, d, bl, seq, *, tc=256):
    t = qr.shape[0]
    dk, dv = d // 8, d // 4
    tc = min(tc, seq)
    nc = seq // tc
    scale = float(dk) ** -0.5

    def body(lg_ref, q_ref, k_ref, v_ref, do_ref, sf_all, sb_all, c_ref, s_ref, dq_ref, dk_ref, dv_ref,
             dlf_ref, dlb_ref, dq_acc, dk_acc, dv_acc):
        c = _ret_consts(lg_ref, pl.program_id(1), tc, dk)
        fwd = c["ab"] >= 0
        zero_state = jnp.zeros((dk, dv), F32)
        zero = jnp.zeros((1, 1), F32)

        def rows_of(i):
            return pl.ds(pl.multiple_of(i * tc, tc), tc)

        def total(xv):
            return jnp.sum(xv, keepdims=True)

        def fwd_sweep(i, carry):
            hh, dlf, dlb = carry
            rows = rows_of(i)
            q, kk, v, dov = q_ref[rows, :], k_ref[rows, :], v_ref[rows, :], do_ref[rows, :]
            dof, vf = dov.astype(F32), v.astype(F32)
            p = _dot_nt(q, kk) * c["dmat"]
            da = _dot_nt(dov, v)
            x = p * da * c["ab"]
            dlf = dlf + total(jnp.where(fwd, x, 0.0))
            dlb = dlb - total(jnp.where(fwd, 0.0, x))
            pb, dpb = p.astype(MM), (da * c["dmat"]).astype(MM)
            dq = _dot(dpb, kk)
            dkc = _dot_tn(dpb, q)
            dvc = _dot_tn(pb, dov)
            sf, sb = sf_all[i], sb_all[i]
            sfb, sbb = sf.astype(MM), sb.astype(MM)
            q_xf, q_zb = _scaled(q, c["xi_f"]), _scaled(q, c["zeta_b"])
            dq = dq + _dot_nt(dov, sfb) * c["xi_f"] + _dot_nt(dov, sbb) * c["zeta_b"]
            dlf = dlf + total(jnp.sum(_dot(q_xf, sfb) * dof, axis=-1, keepdims=True) * c["up"])
            dlb = dlb + total(jnp.sum(_dot(q_zb, sbb) * dof, axis=-1, keepdims=True) * c["down"])
            hb = hh.astype(MM)
            dkc = dkc + _dot_nt(v, hb) * c["xi_b"]
            dv_bx = _dot(_scaled(kk, c["xi_b"]), hb)
            dlb = dlb + total(jnp.sum(vf * dv_bx, axis=-1, keepdims=True) * c["up"])
            dlb = dlb + float(tc) * total(hh * (sb * c["cb"]))
            dq_acc[rows, :] = dq
            dk_acc[rows, :] = dkc
            dv_acc[rows, :] = dvc + dv_bx
            return hh * c["cb"] + _dot_tn(q_zb, dov), dlf, dlb

        _, dlf, dlb = lax.fori_loop(0, nc, fwd_sweep, (zero_state, zero, zero))

        def rev_sweep(ii, carry):
            gg, dlf = carry
            i = nc - 1 - ii
            rows = rows_of(i)
            q, kk, v, dov = q_ref[rows, :], k_ref[rows, :], v_ref[rows, :], do_ref[rows, :]
            gb = gg.astype(MM)
            dk_acc[rows, :] += _dot_nt(v, gb) * c["zeta_f"]
            dv_fx = _dot(_scaled(kk, c["zeta_f"]), gb)
            dv_acc[rows, :] += dv_fx
            dlf = dlf + total(jnp.sum(v.astype(F32) * dv_fx, axis=-1, keepdims=True) * c["down"])
            dlf = dlf + float(tc) * total(gg * (sf_all[i] * c["cf"]))
            return gg * c["cf"] + _dot_tn(_scaled(q, c["xi_f"]), dov), dlf

        _, dlf = lax.fori_loop(0, nc, rev_sweep, (zero_state, dlf))

        cs, sn = c_ref[...], s_ref[...]
        dq_ref[...] = _rot_t(dq_acc[...], cs, sn, dk // 2).astype(MM)
        dk_ref[...] = (_rot_t(dk_acc[...], cs, sn, dk // 2) * scale).astype(MM)
        dv_ref[...] = dv_acc[...].astype(MM)
        dlf_ref[...] = jnp.broadcast_to(dlf, dlf_ref.shape)
        dlb_ref[...] = jnp.broadcast_to(dlb, dlb_ref.shape)

    qk = pl.BlockSpec((seq, dk), lambda b, h: (b, h))
    vv = pl.BlockSpec((seq, dv), lambda b, h: (b, h))
    tab = pl.BlockSpec((seq, dk), lambda b, h: (0, 0))
    dl = pl.BlockSpec((None, 8, 128), lambda b, h: (b * HEADS + h, 0, 0))
    states = pl.BlockSpec((None, nc, dk, dv), lambda b, h: (b * HEADS + h, 0, 0, 0))
    return pl.pallas_call(
        body, name="ret_core_bwd", grid=(bl, HEADS),
        in_specs=[pl.BlockSpec(memory_space=pltpu.SMEM), qk, qk, vv, vv, states, states, tab, tab],
        out_specs=[qk, qk, vv, dl, dl],
        out_shape=[jax.ShapeDtypeStruct((t, d // 2), MM), jax.ShapeDtypeStruct((t, d // 2), MM),
                   jax.ShapeDtypeStruct((t, d), MM),
                   jax.ShapeDtypeStruct((bl * HEADS, 8, 128), F32), jax.ShapeDtypeStruct((bl * HEADS, 8, 128), F32)],
        scratch_shapes=[pltpu.VMEM((seq, dk), F32), pltpu.VMEM((seq, dk), F32), pltpu.VMEM((seq, dv), F32)],
        compiler_params=_params(2))(lg, qr, kr, vb, do, sf_in, sb_in, cos2, sin2)


def _window_count(row, w, seq):
    return (jnp.minimum(row + w // 2, seq) - jnp.maximum(row - w // 2, 0)).astype(F32)


def _window_sum(pv, row, w, seq, sign):
    acc = None
    for j in range(-(w // 2), w // 2):
        if j == 0:
            term = pv
        else:
            src = row + sign * j
            term = jnp.where((src >= 0) & (src < seq), pltpu.roll(pv, (-sign * j) % seq, 0), 0.0)
        acc = term if acc is None else acc + term
    return acc


def _pool_fwd(proj, w_grp, scale, d, bl, seq):
    t = proj.shape[0]
    dg = d // 8

    def body(p_ref, w_ref, s_ref, y_ref):
        row = lax.broadcasted_iota(jnp.int32, (seq, dg), 0)
        for gi, w in enumerate(POOL_WINDOWS):
            sl = slice(gi * dg, (gi + 1) * dg)
            pg = p_ref[:, sl].astype(F32)
            mixed = _window_sum(pg, row, w, seq, 1) / _window_count(row, w, seq) - pg
            yp = _dot(mixed.astype(MM), w_ref[gi].astype(MM))
            y_ref[:, sl] = (yp * s_ref[:, sl]).astype(MM)

    return pl.pallas_call(
        body, name="pool_fwd", grid=(bl,),
        in_specs=[pl.BlockSpec((seq, d // 2), lambda b: (b, 6)),
                  pl.BlockSpec(w_grp.shape, lambda b: (0, 0, 0)),
                  pl.BlockSpec((1, d // 2), lambda b: (0, 0))],
        out_specs=pl.BlockSpec((seq, d // 2), lambda b: (b, 0)),
        out_shape=jax.ShapeDtypeStruct((t, d // 2), MM),
        compiler_params=_params(1))(proj, w_grp, scale)


def _pool_bwd(proj, dy, w_grp, scale, d, bl, seq):
    t = proj.shape[0]
    dg = d // 8

    def body(p_ref, dy_ref, w_ref, s_ref, dp_ref, dw_ref, ds_ref):
        @pl.when(pl.program_id(0) == 0)
        def _():
            dw_ref[...] = jnp.zeros_like(dw_ref)
            ds_ref[...] = jnp.zeros_like(ds_ref)

        row = lax.broadcasted_iota(jnp.int32, (seq, dg), 0)
        for gi, w in enumerate(POOL_WINDOWS):
            sl = slice(gi * dg, (gi + 1) * dg)
            pg = p_ref[:, sl].astype(F32)
            cnt = _window_count(row, w, seq)
            mixb = (_window_sum(pg, row, w, seq, 1) / cnt - pg).astype(MM)
            wgb = w_ref[gi].astype(MM)
            yp = _dot(mixb, wgb)
            dyg = dy_ref[:, sl]
            ds_ref[:, sl] += jnp.sum(dyg * yp, axis=0, keepdims=True)
            dyp = (dyg * s_ref[:, sl]).astype(MM)
            dmixed = _dot_nt(dyp, wgb)
            dw_ref[gi] += _dot_tn(mixb, dyp)
            dp_ref[:, sl] = (_window_sum(dmixed / cnt, row, w, seq, -1) - dmixed).astype(MM)

    half = pl.BlockSpec((seq, d // 2), lambda b: (b, 0))
    wspec = pl.BlockSpec(w_grp.shape, lambda b: (0, 0, 0))
    sspec = pl.BlockSpec((1, d // 2), lambda b: (0, 0))
    return pl.pallas_call(
        body, name="pool_bwd", grid=(bl,),
        in_specs=[pl.BlockSpec((seq, d // 2), lambda b: (b, 6)), half, wspec, sspec],
        out_specs=[half, wspec, sspec],
        out_shape=[jax.ShapeDtypeStruct((t, d // 2), MM), jax.ShapeDtypeStruct(w_grp.shape, F32),
                   jax.ShapeDtypeStruct((1, d // 2), F32)],
        compiler_params=_params(1))(proj, dy, w_grp, scale)


def _attn_probs(q, kk, dh):
    s = _dot_nt(q, kk) * (float(dh) ** -0.5)
    e = jnp.exp(s - jnp.max(s, axis=-1, keepdims=True))
    return e / jnp.sum(e, axis=-1, keepdims=True)


def _attn_fwd(proj, kv, d, bl, seq, mlen, *, tq=2048):
    t = proj.shape[0]
    dh = d // 8
    tq = min(tq, seq)
    nq = seq // tq

    def body(q_ref, k_ref, v_ref, o_ref):
        a = _attn_probs(q_ref[...].astype(MM), k_ref[...].astype(MM), dh)
        o_ref[...] = _dot(a.astype(MM), v_ref[...].astype(MM)).astype(MM)

    return pl.pallas_call(
        body, name="attn_fwd", grid=(bl, HEADS, nq),
        in_specs=[pl.BlockSpec((tq, dh), lambda b, h, i: (b * nq + i, 7 * HEADS + h)),
                  pl.BlockSpec((mlen, dh), lambda b, h, i: (b, h)),
                  pl.BlockSpec((mlen, dh), lambda b, h, i: (b, HEADS + h))],
        out_specs=pl.BlockSpec((tq, dh), lambda b, h, i: (b * nq + i, h)),
        out_shape=jax.ShapeDtypeStruct((t, d // 2), MM),
        compiler_params=_params(3))(proj, kv, kv)


def _attn_bwd(proj, kv, do, d, bl, seq, mlen, *, tq=2048):
    t = proj.shape[0]
    dh = d // 8
    tq = min(tq, seq)
    nq = seq // tq

    def body(q_ref, k_ref, v_ref, do_ref, dq_ref, dk_ref, dv_ref):
        @pl.when(pl.program_id(2) == 0)
        def _():
            dk_ref[...] = jnp.zeros_like(dk_ref)
            dv_ref[...] = jnp.zeros_like(dv_ref)

        q, kk, vv = q_ref[...].astype(MM), k_ref[...].astype(MM), v_ref[...].astype(MM)
        dov = do_ref[...].astype(MM)
        a = _attn_probs(q, kk, dh)
        dp = _dot_nt(dov, vv)
        ds = (a * (dp - jnp.sum(dp * a, axis=-1, keepdims=True)) * (float(dh) ** -0.5)).astype(MM)
        dq_ref[...] = _dot(ds, kk).astype(MM)
        dk_ref[...] += _dot_tn(ds, q)
        dv_ref[...] += _dot_tn(a.astype(MM), dov)

    qs = pl.BlockSpec((tq, dh), lambda b, h, i: (b * nq + i, h))
    ms = pl.BlockSpec((mlen, dh), lambda b, h, i: (b, h))
    return pl.pallas_call(
        body, name="attn_bwd", grid=(bl, HEADS, nq),
        in_specs=[pl.BlockSpec((tq, dh), lambda b, h, i: (b * nq + i, 7 * HEADS + h)), ms,
                  pl.BlockSpec((mlen, dh), lambda b, h, i: (b, HEADS + h)), qs],
        out_specs=[qs, ms, ms],
        out_shape=[jax.ShapeDtypeStruct((t, d // 2), MM), jax.ShapeDtypeStruct((bl * mlen, d // 2), F32),
                   jax.ShapeDtypeStruct((bl * mlen, d // 2), F32)],
        compiler_params=_params(3))(proj, kv, kv, do)


def _comm_call(name, body, arrays, out_shapes):
    n = len(arrays)
    hbm = pl.BlockSpec(memory_space=pl.ANY)
    return pl.pallas_call(
        body, name=name, out_shape=out_shapes, in_specs=[hbm] * n, out_specs=[hbm] * n,
        scratch_shapes=[pltpu.SemaphoreType.DMA((7 * n,)), pltpu.SemaphoreType.DMA((7 * n,)),
                        pltpu.SemaphoreType.DMA((n,))],
    )(*arrays)


def _all_gather(name, shards):
    n = len(shards)

    def body(*refs):
        x_refs, out_refs = refs[:n], refs[n:2 * n]
        send_sems, recv_sems, local_sems = refs[2 * n:]
        x, y, c = lax.axis_index("x"), lax.axis_index("y"), lax.axis_index("c")
        me, sibling = (x, y, c), (x, y, 1 - c)
        chips = [(1 - x, y), (x, 1 - y), (1 - x, 1 - y)]

        def copy(o, k, block, to, src=None):
            slot = out_refs[o].at[4 * block[0] + 2 * block[1] + block[2]]
            return pltpu.make_async_remote_copy(
                src_ref=slot if src is None else src, dst_ref=slot, send_sem=send_sems.at[7 * o + k],
                recv_sem=recv_sems.at[7 * o + k], device_id=to, device_id_type=MESH)

        locals_, remotes = [], []
        for o in range(n):
            mine = pltpu.make_async_copy(x_refs[o], out_refs[o].at[4 * x + 2 * y + c], local_sems.at[o])
            mine.start()
            locals_.append(mine)
            first = [copy(o, 0, me, sibling, src=x_refs[o])]
            first += [copy(o, 1 + j, me, (*chip, c), src=x_refs[o]) for j, chip in enumerate(chips)]
            for cp in first:
                cp.start()
            remotes += first
        for o in range(n):
            for j, chip in enumerate(chips):
                copy(o, 1 + j, (*chip, c), me).wait_recv()
                passed = copy(o, 4 + j, (*chip, c), sibling)
                passed.start()
                remotes.append(passed)
        for o in range(n):
            copy(o, 0, sibling, me).wait_recv()
            for j, chip in enumerate(chips):
                copy(o, 4 + j, (*chip, 1 - c), me).wait_recv()
        for cp in remotes:
            cp.wait_send()
        for mine in locals_:
            mine.wait()

    outs = [jax.ShapeDtypeStruct((N_DEV,) + s.shape, s.dtype) for s in shards]
    return _comm_call(name, body, shards, outs)


def _peer_of(k, x, y, c):
    peer = (1 - x if k & 4 else x, 1 - y if k & 2 else y, 1 - c if k & 1 else c)
    return peer, 4 * peer[0] + 2 * peer[1] + peer[2]


def _split_copies(scatter, srcs, lands, send_sems, recv_sems, arriving):
    x, y, c = lax.axis_index("x"), lax.axis_index("y"), lax.axis_index("c")
    me_idx = 4 * x + 2 * y + c
    copies = []
    for o, (src, land) in enumerate(zip(srcs, lands)):
        for k in range(1, N_DEV):
            peer, p_idx = _peer_of(k, x, y, c)
            mine = src.at[p_idx] if scatter else src
            sems = dict(send_sem=send_sems.at[7 * o + k - 1], recv_sem=recv_sems.at[7 * o + k - 1],
                        device_id=peer, device_id_type=MESH)
            slot = land.at[p_idx] if arriving else land.at[me_idx]
            copies.append(pltpu.make_async_remote_copy(src_ref=mine, dst_ref=slot, **sems))
    return copies


_HBM = pl.BlockSpec(memory_space=pltpu.HBM)
_SEM = pl.BlockSpec(memory_space=pltpu.SEMAPHORE)
_EFFECT = pltpu.SideEffectType.DATAFLOW_SIDE_EFFECTING


def _exchange_start(name, scatter, arrays, after=()):
    n = len(arrays)
    lands = [lax.empty(a.shape if scatter else (N_DEV,) + a.shape, a.dtype) for a in arrays]

    def body(*refs):
        srcs, lnds = refs[:n], refs[n:2 * n]
        send_sems, recv_sems = refs[2 * n + len(after)], refs[2 * n + len(after) + 1]
        token = refs[-1]
        for cp in _split_copies(scatter, srcs, lnds, send_sems, recv_sems, False):
            cp.start()
        token[...] = jnp.zeros_like(token)

    hbm_in = [pltpu.with_memory_space_constraint(a, pltpu.HBM) for a in list(arrays) + lands]
    res = pl.pallas_call(
        body, name=name,
        out_shape=(pltpu.SemaphoreType.DMA((7 * n,)), pltpu.SemaphoreType.DMA((7 * n,)),
                   *[pltpu.HBM(a.shape, a.dtype) for a in hbm_in], jax.ShapeDtypeStruct((8, 128), F32)),
        in_specs=[_HBM] * (2 * n) + [pl.BlockSpec(memory_space=pl.ANY)] * len(after),
        out_specs=(_SEM, _SEM, *[_HBM] * (2 * n), pl.BlockSpec(memory_space=pltpu.VMEM)),
        input_output_aliases={i: 2 + i for i in range(2 * n)},
        compiler_params=pltpu.CompilerParams(has_side_effects=_EFFECT),
    )(*hbm_in, *after)
    return res[0], res[1], list(res[2:2 + n]), list(res[2 + n:2 + 2 * n]), res[-1]


def _exchange_wait(name, scatter, started, after):
    send_sems, recv_sems, srcs, lands, _ = started
    n = len(srcs)

    def body(*refs):
        src_refs, lnd_refs = refs[:n], refs[n:2 * n]
        for cp in _split_copies(scatter, src_refs, lnd_refs, refs[2 * n], refs[2 * n + 1], False):
            cp.wait_send()
        for cp in _split_copies(scatter, src_refs, lnd_refs, refs[2 * n], refs[2 * n + 1], True):
            cp.wait_recv()

    res = pl.pallas_call(
        body, name=name, out_shape=tuple(pltpu.HBM(a.shape, a.dtype) for a in srcs + lands),
        in_specs=[_HBM] * (2 * n) + [_SEM, _SEM, pl.BlockSpec(memory_space=pl.ANY)],
        out_specs=tuple([_HBM] * (2 * n)), input_output_aliases={i: i for i in range(2 * n)},
        compiler_params=pltpu.CompilerParams(has_side_effects=_EFFECT),
    )(*srcs, *lands, send_sems, recv_sems, after)
    return list(res[n:]), list(res[:n])


def _adamw(name, parts, w, m, v, prev, layer, *, tr=256):
    _, a, b = w.shape
    tr = _tile(a, tr, 8)
    c1 = 1.0 - ADAM_B1 ** ADAM_STEP
    c2 = 1.0 - ADAM_B2 ** ADAM_STEP

    def body(p_ref, w_ref, m_ref, v_ref, _g, _d, _m, _v, g_out, d_out, m_out, v_out):
        g = p_ref[0].astype(F32)
        for s in range(1, N_DEV):
            g = g + p_ref[s].astype(F32)
        mn = ADAM_B1 * m_ref[...] + (1.0 - ADAM_B1) * g
        vn = ADAM_B2 * v_ref[...] + (1.0 - ADAM_B2) * (g * g)
        g_out[...] = g
        m_out[...] = mn
        v_out[...] = vn
        d_out[...] = -ADAM_LR * ((mn / c1) / (jnp.sqrt(vn / c2) + ADAM_EPS) + ADAM_WD * w_ref[...])

    slab = pl.BlockSpec((None, tr, b), lambda i: (layer, i, 0))
    whole = pl.BlockSpec(memory_space=pl.ANY)
    return pl.pallas_call(
        body, name=name, grid=(a // tr,),
        in_specs=[pl.BlockSpec((N_DEV, tr, b), lambda i: (0, i, 0)), slab, slab, slab] + [whole] * 4,
        out_specs=[slab] * 4, out_shape=[jax.ShapeDtypeStruct(w.shape, F32)] * 4,
        input_output_aliases={4: 0, 5: 1, 6: 2, 7: 3},
        compiler_params=_params(1))(parts, w, m, v, *prev)


_COL = ("w_in", "w_pool_o", "w_mem_o", "w_ff1")
_COL_IN_PLACE = ("w_in", "w_ff1")
_BIG =("w_in", "w_ret_o", "w_pool_o", "w_mem_kv", "w_mem_o", "w_out", "w_ff1", "w_ff2")
_SMALL = ("ret_decay_logit", "w_pool_grp", "pool_scale", "norm1_g", "norm2_g", "mem_norm_g", "final_norm_g")
_WEIGHTS = ("w_in", "ret_decay_logit", "w_ret_o", "w_pool_grp", "pool_scale", "w_pool_o", "w_mem_kv", "w_mem_o",
            "w_out", "w_ff1", "w_ff2", "norm1_g", "norm2_g", "mem_norm_g", "final_norm_g")


def _small_rows(size, d):
    return -(-size // (8 * d)) * 8


def _pack_small(ws, d):
    parts = []
    for n in _SMALL:
        flat = ws[n].reshape(-1)
        rows = _small_rows(flat.shape[0], d)
        parts.append(jnp.pad(flat, (0, rows * d - flat.shape[0])).reshape(rows, d))
    return jnp.concatenate(parts, axis=0)[None]


def _unpack_small(packed, like, d):
    out, off = {}, 0
    for n in _SMALL:
        rows = _small_rows(like[n].size, d)
        out[n] = packed[0, off:off + rows].reshape(-1)[:like[n].size].reshape(like[n].shape)
        off += rows
    return out


def kernel(x, mem, w_in, ret_decay_logit, w_ret_o, w_pool_grp, pool_scale, w_pool_o, w_mem_kv, w_mem_o, w_out, w_ff1, w_ff2, norm1_g, norm2_g, mem_norm_g, final_norm_g, loss_target, m_w_in, m_ret_decay_logit, m_w_ret_o, m_w_pool_grp, m_pool_scale, m_w_pool_o, m_w_mem_kv, m_w_mem_o, m_w_out, m_w_ff1, m_w_ff2, m_norm1_g, m_norm2_g, m_mem_norm_g, m_final_norm_g, v_w_in, v_ret_decay_logit, v_w_ret_o, v_w_pool_grp, v_pool_scale, v_w_pool_o, v_w_mem_kv, v_w_mem_o, v_w_out, v_w_ff1, v_w_ff2, v_norm1_g, v_norm2_g, v_mem_norm_g, v_final_norm_g):
    w = dict(w_in=w_in, ret_decay_logit=ret_decay_logit, w_ret_o=w_ret_o, w_pool_grp=w_pool_grp,
             pool_scale=pool_scale, w_pool_o=w_pool_o, w_mem_kv=w_mem_kv, w_mem_o=w_mem_o, w_out=w_out,
             w_ff1=w_ff1, w_ff2=w_ff2, norm1_g=norm1_g, norm2_g=norm2_g, mem_norm_g=mem_norm_g,
             final_norm_g=final_norm_g)
    mom = dict(w_in=m_w_in, ret_decay_logit=m_ret_decay_logit, w_ret_o=m_w_ret_o, w_pool_grp=m_w_pool_grp,
               pool_scale=m_pool_scale, w_pool_o=m_w_pool_o, w_mem_kv=m_w_mem_kv, w_mem_o=m_w_mem_o,
               w_out=m_w_out, w_ff1=m_w_ff1, w_ff2=m_w_ff2, norm1_g=m_norm1_g, norm2_g=m_norm2_g,
               mem_norm_g=m_mem_norm_g, final_norm_g=m_final_norm_g)
    vel = dict(w_in=v_w_in, ret_decay_logit=v_ret_decay_logit, w_ret_o=v_w_ret_o, w_pool_grp=v_w_pool_grp,
               pool_scale=v_pool_scale, w_pool_o=v_w_pool_o, w_mem_kv=v_w_mem_kv, w_mem_o=v_w_mem_o,
               w_out=v_w_out, w_ff1=v_w_ff1, w_ff2=v_w_ff2, norm1_g=v_norm1_g, norm2_g=v_norm2_g,
               mem_norm_g=v_mem_norm_g, final_norm_g=v_final_norm_g)

    bl, seq, d = x.shape
    mlen = mem.shape[1]
    depth = w_in.shape[0]
    t = bl * seq
    dk = d // 8

    me_idx = 4 * lax.axis_index("x") + 2 * lax.axis_index("y") + lax.axis_index("c")

    def natural(n, g):
        if n in _COL_IN_PLACE:
            return g
        if n in _COL:
            return jnp.transpose(g, (1, 0, 2)).reshape(g.shape[1], -1)
        return g.reshape(-1, g.shape[-1])

    def finish_gather(name, names, started, after):
        got, mine = _exchange_wait(name, False, started, after)
        return {n: natural(n, lax.dynamic_update_slice(g, sh[None], (me_idx, 0, 0)))
                for n, g, sh in zip(names, got, mine)}

    shards = [{n: w[n][l].astype(MM) for n in _BIG} for l in range(depth)]
    rest = _BIG[1:]
    (w_in0,) = _all_gather("gather_w_in", [shards[0][_BIG[0]]])
    full = [{_BIG[0]: w_in0}]
    pending = _exchange_start("gather_start_0", False, [shards[0][n] for n in rest], after=[w_in0])

    inv = ROPE_BASE ** (-jnp.arange(0, dk, 2, dtype=F32) / dk)
    ang = jnp.arange(seq, dtype=F32)[:, None] * inv[None, :]
    cos2 = jnp.concatenate([jnp.cos(ang), jnp.cos(ang)], axis=-1)
    sin2 = jnp.concatenate([-jnp.sin(ang), jnp.sin(ang)], axis=-1)
    log_g = jax.nn.log_sigmoid(ret_decay_logit)
    x2 = x.reshape(t, d)
    mem2 = mem.reshape(bl * mlen, d)
    gmem = mem_norm_g.reshape(1, d)

    def merge(o_r, o_p, o_m, g_r, g_p, g_m):
        f = lambda z: z.astype(F32)
        return _sigmoid(f(g_r)) * f(o_r) + _sigmoid(f(g_p)) * f(o_p) + _sigmoid(f(g_m)) * f(o_m)

    def relu2(u):
        r = jnp.maximum(u.astype(F32), 0.0)
        return r * r

    def ident(a):
        return a

    saved = []
    xc = x2
    for l in range(depth):
        s = dict(x_in=xc)
        started_now = ()
        if l > 0:
            full.append(finish_gather(f"gather_wait_{l}", _BIG, pending, xc))
            if l + 1 < depth:
                pending = _exchange_start(f"gather_start_{l + 1}", False, [shards[l + 1][n] for n in _BIG],
                                          after=[full[l]["w_in"]])
                started_now = (pending[4],)
        fw = full[l]
        g1 = norm1_g[l].reshape(1, d)
        g2 = norm2_g[l].reshape(1, d)
        s["proj"], s["h1"] = _pmm("proj", _rms_prologue, [(xc, d, 0)], [g1], fw["w_in"], w_mode="col",
                                  tm=1024, tn=1024, save_a=True, out_dtypes=(MM,),
                                  after=started_now if l > 0 else (pending[4],))
        proj = s["proj"]
        s["qr"], s["kr"], s["vb"] = _ret_pre(proj, cos2, sin2, d, seq)
        s["o_raw"], s["a_ret"], s["sf"], s["sb"] = _ret_core_fwd(s["qr"], s["kr"], s["vb"], proj, log_g[l],
                                                                 d, bl, seq)
        s["y"] = _pool_fwd(proj, w_pool_grp[l], pool_scale[l].reshape(1, -1), d, bl, seq)
        started_now = ()
        if l == 0:
            fw.update(finish_gather("gather_wait_0", rest, pending, s["a_ret"]))
            if depth > 1:
                pending = _exchange_start("gather_start_1", False, [shards[1][n] for n in _BIG],
                                          after=[fw["w_mem_kv"]])
                started_now = (pending[4],)
        s["kv"], s["memn"] = _pmm("mem_kv", _rms_prologue, [(mem2, d, 0)], [gmem], fw["w_mem_kv"],
                                  tm=512, tn=512, save_a=True, after=started_now)
        s["o_att"] = _attn_fwd(proj, s["kv"], d, bl, seq, mlen)
        (s["o_ret"],) = _pmm("ret_o", None, [(s["a_ret"], d, 0)], [], fw["w_ret_o"], tm=1024, tn=512,
                             out_dtypes=(MM,))
        (s["o_pool"],) = _pmm("pool_o", None, [(s["y"], d // 2, 0)], [], fw["w_pool_o"],
                              tm=1024, tn=512, out_dtypes=(MM,))
        (s["o_mem"],) = _pmm("mem_o", None, [(s["o_att"], d // 2, 0)], [], fw["w_mem_o"],
                             tm=1024, tn=512, out_dtypes=(MM,))
        s["x_mid"], s["merged"] = _pmm(
            "merge_out", merge,
            [(s["o_ret"], d, 0), (s["o_pool"], d, 0), (s["o_mem"], d, 0), (proj, d, 4), (proj, d, 5), (proj, d, 6)],
            [], fw["w_out"], tm=512, tn=512, residual=xc, save_a=True)
        s["u"], s["h2"] = _pmm("ff1", _rms_prologue, [(s["x_mid"], d, 0)], [g2], fw["w_ff1"], w_mode="col",
                               tm=1024, tn=512, save_a=True, out_dtypes=(MM,))
        xc, s["a"] = _pmm("ff2", relu2, [(s["u"], s["u"].shape[1], 0)], [], fw["w_ff2"],
                          tm=512, tn=512, residual=s["x_mid"], save_a=True)
        saved.append(s)

    dxc, g_final, loss_part = _loss_head(xc, loss_target.reshape(t, d), final_norm_g.reshape(1, d))
    loss = lax.psum(loss_part[0, 0], ("x", "y", "c"))

    small_names = ("w_pool_grp", "pool_scale", "norm1_g", "norm2_g", "ret_decay_logit")
    grads = {n: [None] * depth for n in small_names}
    group_a = ("w_ff1", "w_ff2")
    group_b = tuple(n for n in _BIG if n not in group_a)
    scatters = {}
    dmemn = jnp.zeros((bl * mlen, d), F32)

    def relu2_bwd(acc, u):
        return (acc * (2.0 * jnp.maximum(u.astype(F32), 0.0)),)

    def gates_bwd(acc, g_r, g_p, g_m, o_r, o_p, o_m):
        outs_o, outs_g = [], []
        for gz, oz in ((g_r, o_r), (g_p, o_p), (g_m, o_m)):
            sg = _sigmoid(gz.astype(F32))
            outs_o.append(acc * sg)
            outs_g.append(acc * oz.astype(F32) * (sg * (1.0 - sg)))
        return tuple(outs_o + outs_g)

    def to_send(n, g):
        a, b = w[n].shape[1:]
        if n in _COL_IN_PLACE:
            return g
        if n in _COL:
            return jnp.transpose(g.reshape(a, N_DEV, b), (1, 0, 2))
        return g.reshape(N_DEV, a, b)

    for l in reversed(range(depth)):
        s = saved[l]
        fw = full[l]
        proj = s["proj"]
        g1 = norm1_g[l].reshape(1, d)
        g2 = norm2_g[l].reshape(1, d)
        dw = {}
        (du,) = _pmm("ff2_bwd", ident, [(dxc, d, 0)], [], fw["w_ff2"], w_mode="nt", tm=1024, tn=1024,
                     epilogue=relu2_bwd, epi_ins=[(s["u"], 0)], out_dtypes=(MM,))
        dw["w_ff2"] = _tnmm("dw_ff2", s["a"], dxc)
        (dh2,) = _pmm("ff1_bwd", None, [(du, du.shape[1], 0)], [], fw["w_ff1"], w_mode="col_t", tm=1024, tn=512)
        dw["w_ff1"] = _tnmm("dw_ff1", s["h2"], du, col_shards=True)
        scatters[l, "a"] = _exchange_start(f"scatter_start_a{l}", True, [to_send(n, dw[n]) for n in group_a])
        dmid, grads["norm2_g"][l] = _rms_bwd("norm2_bwd", dh2, s["x_mid"], g2, dxc)
        d_oret, d_opool, d_omem, dgr, dgp, dgm = _pmm(
            "out_bwd", ident, [(dmid, d, 0)], [], fw["w_out"], w_mode="nt", tm=512, tn=512, epilogue=gates_bwd,
            epi_ins=[(proj, 4 * d), (proj, 5 * d), (proj, 6 * d), (s["o_ret"], 0), (s["o_pool"], 0), (s["o_mem"], 0)],
            out_dtypes=(MM,) * 6, after=(scatters[l, "a"][4],))
        dw["w_out"] = _tnmm("dw_out", s["merged"], dmid)
        (da_ret,) = _pmm("ret_o_bwd", None, [(d_oret, d, 0)], [], fw["w_ret_o"], w_mode="nt", tm=1024, tn=512)
        dw["w_ret_o"] = _tnmm("dw_ret_o", s["a_ret"], d_oret)
        (dy,) = _pmm("pool_o_bwd", None, [(d_opool, d, 0)], [], fw["w_pool_o"], w_mode="nt", tm=1024, tn=512)
        dw["w_pool_o"] = _tnmm("dw_pool_o", s["y"], d_opool)
        (do_att,) = _pmm("mem_o_bwd", None, [(d_omem, d, 0)], [], fw["w_mem_o"], w_mode="nt", tm=1024, tn=512)
        dw["w_mem_o"] = _tnmm("dw_mem_o", s["o_att"], d_omem)
        dg_ret, do_ret = _ret_post_bwd(da_ret, proj, s["o_raw"], d)
        dq, dkk, dvv, dlf, dlb = _ret_core_bwd(s["qr"], s["kr"], s["vb"], do_ret, s["sf"], s["sb"], cos2, sin2,
                                               log_g[l], d, bl, seq)
        dl = jnp.stack([dlf[:, 0, 0].reshape(bl, HEADS).sum(0), dlb[:, 0, 0].reshape(bl, HEADS).sum(0)])
        grads["ret_decay_logit"][l] = dl * jax.nn.sigmoid(-ret_decay_logit[l])
        dp, grads["w_pool_grp"][l], dscale = _pool_bwd(proj, dy, w_pool_grp[l], pool_scale[l].reshape(1, -1),
                                                       d, bl, seq)
        grads["pool_scale"][l] = dscale.reshape(-1)
        dqm, dmk, dmv = _attn_bwd(proj, s["kv"], do_att, d, bl, seq, mlen)
        dkv = jnp.concatenate([dmk, dmv], axis=-1).astype(MM)
        dw["w_mem_kv"] = _tnmm("dw_mem_kv", s["memn"], dkv)
        (dmemn,) = _pmm("mem_kv_bwd", None, [(dkv, d, 0)], [], fw["w_mem_kv"], w_mode="nt", tm=512, tn=512,
                        residual=dmemn)
        dproj = jnp.concatenate([dq, dkk, dvv, dg_ret, dp, dqm, dgr, dgp, dgm], axis=-1)
        dw["w_in"] = _tnmm("dw_in", s["h1"], dproj, col_shards=True)
        scatters[l, "b"] = _exchange_start(f"scatter_start_b{l}", True, [to_send(n, dw[n]) for n in group_b])
        (dh1,) = _pmm("proj_bwd", None, [(dproj, dproj.shape[1], 0)], [], fw["w_in"], w_mode="col_t",
                      tm=512, tn=512, after=(scatters[l, "b"][4],))
        dxc, grads["norm1_g"][l] = _rms_bwd("norm1_bwd", dh1, s["x_in"], g1, dmid)

    _, g_memn = _rms_bwd("mem_norm_bwd", dmemn, mem2, gmem, None)
    grad_x = dxc.reshape(bl, seq, d)

    small_g = dict(ret_decay_logit=jnp.stack(grads["ret_decay_logit"]), w_pool_grp=jnp.stack(grads["w_pool_grp"]),
                   pool_scale=jnp.stack(grads["pool_scale"]),
                   norm1_g=jnp.concatenate(grads["norm1_g"], axis=0), norm2_g=jnp.concatenate(grads["norm2_g"], axis=0),
                   mem_norm_g=g_memn.reshape(-1), final_norm_g=g_final.reshape(-1))
    (small_parts,) = _all_gather("gather_small_grads", [_pack_small(small_g, d)[0]])
    w_small = _pack_small(w, d)
    small = _adamw("adamw_small", small_parts, w_small, _pack_small(mom, d), _pack_small(vel, d),
                   [lax.empty(w_small.shape, F32) for _ in range(4)], 0)
    small = [_unpack_small(o, w, d) for o in small]

    big = {n: [lax.empty(w[n].shape, F32) for _ in range(4)] for n in _BIG}
    after = small[0]["w_pool_grp"]
    for l in reversed(range(depth)):
        for grp, names in (("a", group_a), ("b", group_b)):
            recv, sent = _exchange_wait(f"scatter_wait_{grp}{l}", True, scatters[l, grp], after)
            for n, r, snt in zip(names, recv, sent):
                own = lax.dynamic_slice_in_dim(snt, me_idx, 1, axis=0)
                parts = lax.dynamic_update_slice(r, own, (me_idx, 0, 0))
                big[n] = _adamw("adamw_" + n, parts, w[n], mom[n], vel[n], big[n], l)
            after = big[names[-1]][0]

    outs = [loss, grad_x]
    for k in range(4):
        outs += [big[n][k] if n in _BIG else small[k][n] for n in _WEIGHTS]
    return tuple(outs)
```

```python
import jax
import jax.numpy as jnp
from jax import lax
from jax.experimental import pallas as pl
from jax.experimental.pallas import tpu as pltpu

F32 = jnp.float32
MM = jnp.bfloat16
N_DEV = 8
HEADS = 4
POOL_WINDOWS = (2, 4, 8, 16)
EPS = 1e-6
ROPE_BASE = 10000.0
ADAM_LR, ADAM_B1, ADAM_B2, ADAM_EPS, ADAM_WD, ADAM_STEP = 0.001, 0.9, 0.999, 1e-08, 0.01, 10
V7X_VMEM_LIMIT = 56 * 1024 * 1024
MESH = pl.DeviceIdType.MESH


def _params(n_axes):
    return pltpu.CompilerParams(dimension_semantics=("arbitrary",) * n_axes,
                                vmem_limit_bytes=V7X_VMEM_LIMIT)


def _tile(n, pref, align=128):
    cands = [c for c in range(align, min(pref, n) + 1, align) if n % c == 0]
    return max(cands) if cands else n


def _sigmoid(z):
    return 0.5 * jnp.tanh(0.5 * z) + 0.5


def _dot(a, b):
    return jnp.dot(a, b, preferred_element_type=F32)


def _dot_nt(a, b):
    return lax.dot_general(a, b, (((1,), (1,)), ((), ())), preferred_element_type=F32)


def _dot_tn(a, b):
    return lax.dot_general(a, b, (((0,), (0,)), ((), ())), preferred_element_type=F32)


def _pmm(name, prologue, row_ins, vec_ins, w, *, tm, tn, w_mode="nn", residual=None, save_a=False,
         epilogue=None, epi_ins=(), out_dtypes=(F32,), after=()):
    m = row_ins[0][0].shape[0]
    wb = None
    if w_mode == "nn":
        k, n = w.shape
        tn = _tile(n, tn)
        w_spec = pl.BlockSpec((k, tn), lambda i, j: (0, j))
    elif w_mode == "nt":
        n, k = w.shape
        tn = _tile(n, tn)
        w_spec = pl.BlockSpec((tn, k), lambda i, j: (j, 0))
    elif w_mode == "col":
        _, k, wb = w.shape
        n = N_DEV * wb
        tn = _tile(wb, tn)
        w_spec = pl.BlockSpec((None, k, tn), lambda i, j, q=wb // tn: (j // q, 0, j % q))
    else:
        _, n, wb = w.shape
        k = N_DEV * wb
        tn = _tile(n, tn)
        w_spec = pl.BlockSpec((N_DEV, tn, wb), lambda i, j: (0, j, 0))
    tm = _tile(m, tm, 8)
    n_row, n_vec, n_epi, n_out = len(row_ins), len(vec_ins), len(epi_ins), len(out_dtypes)
    has_res = residual is not None
    use_scr = prologue is not None

    def body(*refs):
        row_refs = refs[:n_row]
        p = n_row
        vec_refs = refs[p:p + n_vec]
        p += n_vec
        w_ref = refs[p]
        p += 1
        res_ref = refs[p] if has_res else None
        p += int(has_res)
        epi_refs = refs[p:p + n_epi]
        p += n_epi + len(after)
        out_refs = refs[p:p + n_out]
        p += n_out
        a_out = refs[p] if save_a else None
        p += int(save_a)
        if use_scr:
            a_src = refs[p]

            @pl.when(pl.program_id(1) == 0)
            def _():
                a = prologue(*[r[...] for r in row_refs], *[v[...] for v in vec_refs]).astype(MM)
                a_src[...] = a
                if save_a:
                    a_out[...] = a
        else:
            a_src = row_refs[0]
        if w_mode == "nt":
            acc = _dot_nt(a_src[...], w_ref[...])
        elif w_mode == "col_t":
            acc = _dot_nt(a_src[:, 0:wb], w_ref[0])
            for dev in range(1, N_DEV):
                acc = acc + _dot_nt(a_src[:, dev * wb:(dev + 1) * wb], w_ref[dev])
        else:
            acc = _dot(a_src[...], w_ref[...])
        if has_res:
            acc = acc + res_ref[...]
        outs = epilogue(acc, *[e[...] for e in epi_refs]) if epilogue is not None else (acc,)
        for o_ref, o in zip(out_refs, outs):
            o_ref[...] = o.astype(o_ref.dtype)

    in_specs = [pl.BlockSpec((tm, wd), lambda i, j, cb=cb: (i, cb)) for (_, wd, cb) in row_ins]
    in_specs += [pl.BlockSpec(v.shape, lambda i, j: (0, 0)) for v in vec_ins]
    in_specs += [w_spec]
    args = [r[0] for r in row_ins] + list(vec_ins) + [w]
    if has_res:
        in_specs.append(pl.BlockSpec((tm, tn), lambda i, j: (i, j)))
        args.append(residual)
    for (arr, off) in epi_ins:
        assert off % tn == 0
        in_specs.append(pl.BlockSpec((tm, tn), lambda i, j, ob=off // tn: (i, ob + j)))
        args.append(arr)
    n_after = len(after)
    in_specs += [pl.BlockSpec(memory_space=pl.ANY)] * n_after
    args += list(after)
    out_specs = [pl.BlockSpec((tm, tn), lambda i, j: (i, j)) for _ in out_dtypes]
    out_shape = [jax.ShapeDtypeStruct((m, n), dt) for dt in out_dtypes]
    if save_a:
        out_specs.append(pl.BlockSpec((tm, k), lambda i, j: (i, 0)))
        out_shape.append(jax.ShapeDtypeStruct((m, k), MM))
    scratch = [pltpu.VMEM((tm, k), MM)] if use_scr else []
    return pl.pallas_call(body, name=name, grid=(m // tm, n // tn), in_specs=in_specs,
                          out_specs=out_specs, out_shape=out_shape, scratch_shapes=scratch,
                          compiler_params=_params(2))(*args)


def _tnmm(name, a, b, *, tm=1024, tn=1024, tk=1024, col_shards=False, a_fn=None):
    t, m = a.shape
    pieces = list(b) if isinstance(b, (list, tuple)) else [b]
    widths = [p.shape[1] for p in pieces]
    offs = [sum(widths[:p]) for p in range(len(pieces))]
    n = sum(widths)
    tm, tk = _tile(m, tm), _tile(t, tk, 8)
    per_tile = 1
    if col_shards:
        wb = n // N_DEV
        if len(pieces) > 1:
            tn = n
        while 2 * per_tile * wb <= tn and 2 * per_tile <= N_DEV:
            per_tile *= 2
        tn = per_tile * wb
        out_spec = pl.BlockSpec((per_tile, tm, wb), lambda i, j, kk: (j, i, 0))
        out_shape = jax.ShapeDtypeStruct((N_DEV, m, wb), MM)
    else:
        tn = _tile(n, tn)
        out_spec = pl.BlockSpec((tm, tn), lambda i, j, kk: (i, j))
        out_shape = jax.ShapeDtypeStruct((m, n), MM)
    nk = t // tk

    assert len(pieces) == 1 or tn == n

    def body(a_ref, *rest):
        b_refs, (o_ref, acc) = rest[:len(pieces)], rest[len(pieces):]
        kk = pl.program_id(2)

        @pl.when(kk == 0)
        def _():
            acc[...] = jnp.zeros_like(acc)

        av = (a_ref[...] if a_fn is None else a_fn(a_ref[...])).astype(MM)
        if len(pieces) == 1:
            acc[...] += _dot_tn(av, b_refs[0][...].astype(MM))
        else:
            for b_ref, off, wd in zip(b_refs, offs, widths):
                acc[:, off:off + wd] += _dot_tn(av, b_ref[...].astype(MM))

        @pl.when(kk == nk - 1)
        def _():
            if col_shards:
                for sh in range(per_tile):
                    o_ref[sh] = acc[:, sh * wb:(sh + 1) * wb].astype(o_ref.dtype)
            else:
                o_ref[...] = acc[...].astype(o_ref.dtype)

    return pl.pallas_call(
        body, name=name, grid=(m // tm, n // tn, nk),
        in_specs=[pl.BlockSpec((tk, tm), lambda i, j, kk: (kk, i))]
        + [pl.BlockSpec((tk, tn if len(pieces) == 1 else wd), lambda i, j, kk: (kk, j)) for wd in widths],
        out_specs=out_spec, out_shape=out_shape,
        scratch_shapes=[pltpu.VMEM((tm, tn), F32)],
        compiler_params=_params(3))(a, *pieces)


def _rms_prologue(x, g):
    r = lax.rsqrt(jnp.mean(x * x, axis=-1, keepdims=True) + EPS)
    return x * r * g


def _rms_bwd_rows(dh, x, g):
    d = x.shape[-1]
    r = lax.rsqrt(jnp.mean(x * x, axis=-1, keepdims=True) + EPS)
    xh = x * r
    dxh = dh * g
    dx = r * (dxh - xh * (jnp.sum(dxh * xh, axis=-1, keepdims=True) / d))
    dg = jnp.sum(dh * xh, axis=0, keepdims=True)
    return dx, dg


def _rms_bwd(name, dh, x, g, dres, *, tm=512):
    m, d = x.shape
    tm = min(tm, m)
    has_res = dres is not None

    def body(*refs):
        if has_res:
            dh_ref, x_ref, g_ref, r_ref, dx_ref, dg_ref = refs
        else:
            dh_ref, x_ref, g_ref, dx_ref, dg_ref = refs
        dx, dg = _rms_bwd_rows(dh_ref[...], x_ref[...], g_ref[...])
        if has_res:
            dx = dx + r_ref[...]
        dx_ref[...] = dx

        @pl.when(pl.program_id(0) == 0)
        def _():
            dg_ref[...] = jnp.zeros_like(dg_ref)

        dg_ref[...] += dg

    row = pl.BlockSpec((tm, d), lambda i: (i, 0))
    vec = pl.BlockSpec((1, d), lambda i: (0, 0))
    in_specs = [row, row, vec] + ([row] if has_res else [])
    args = [dh, x, g] + ([dres] if has_res else [])
    return pl.pallas_call(body, name=name, grid=(m // tm,), in_specs=in_specs, out_specs=[row, vec],
                          out_shape=[jax.ShapeDtypeStruct((m, d), F32), jax.ShapeDtypeStruct((1, d), F32)],
                          compiler_params=_params(1))(*args)


def _mm_rms_bwd(name, a, w, x, g, dres, *, tm, after=()):
    pieces = list(a) if isinstance(a, (list, tuple)) else [a]
    widths = [p.shape[1] for p in pieces]
    offs = [sum(widths[:p]) for p in range(len(pieces))]
    m = pieces[0].shape[0]
    _, d, wb = w.shape
    tm = _tile(m, tm, 8)
    n_a = len(pieces)

    def body(*refs):
        a_refs = refs[:n_a]
        w_ref, x_ref, g_ref, r_ref = refs[n_a:n_a + 4]
        dx_ref, dg_ref = refs[n_a + 4 + len(after):]

        def window(lo, hi):
            parts = [a_ref[:, max(lo, off) - off:min(hi, off + wd) - off]
                     for a_ref, off, wd in zip(a_refs, offs, widths) if min(hi, off + wd) > max(lo, off)]
            return parts[0] if len(parts) == 1 else jnp.concatenate(parts, axis=1)

        dh = _dot_nt(window(0, wb), w_ref[0])
        for dev in range(1, N_DEV):
            dh = dh + _dot_nt(window(dev * wb, (dev + 1) * wb), w_ref[dev])
        dx, dg = _rms_bwd_rows(dh, x_ref[...], g_ref[...])
        dx_ref[...] = dx + r_ref[...]

        @pl.when(pl.program_id(0) == 0)
        def _():
            dg_ref[...] = jnp.zeros_like(dg_ref)

        dg_ref[...] += dg

    row = pl.BlockSpec((tm, d), lambda i: (i, 0))
    vec = pl.BlockSpec((1, d), lambda i: (0, 0))
    return pl.pallas_call(
        body, name=name, grid=(m // tm,),
        in_specs=[pl.BlockSpec((tm, wd), lambda i: (i, 0)) for wd in widths]
        + [pl.BlockSpec(w.shape, lambda i: (0, 0, 0)), row, vec, row]
        + [pl.BlockSpec(memory_space=pl.ANY)] * len(after),
        out_specs=[row, vec],
        out_shape=[jax.ShapeDtypeStruct((m, d), F32), jax.ShapeDtypeStruct((1, d), F32)],
        compiler_params=_params(1))(*pieces, w, x, g, dres, *after)


def _loss_head(x, target, g, *, tm=256):
    m, d = x.shape
    tm = min(tm, m)

    def body(x_ref, t_ref, g_ref, dx_ref, dg_ref, loss_ref):
        xv, gv = x_ref[...], g_ref[...]
        y = _rms_prologue(xv, gv)
        err = y - t_ref[...]
        part = 0.5 * jnp.sum(jnp.sum(err * err, axis=-1, keepdims=True) / d)
        dx, dg = _rms_bwd_rows(err / d, xv, gv)
        dx_ref[...] = dx

        @pl.when(pl.program_id(0) == 0)
        def _():
            dg_ref[...] = jnp.zeros_like(dg_ref)
            loss_ref[...] = jnp.zeros_like(loss_ref)

        dg_ref[...] += dg
        loss_ref[...] += jnp.full(loss_ref.shape, part, F32)

    row = pl.BlockSpec((tm, d), lambda i: (i, 0))
    vec = pl.BlockSpec((1, d), lambda i: (0, 0))
    lspec = pl.BlockSpec((1, 128), lambda i: (0, 0))
    return pl.pallas_call(body, name="loss_head", grid=(m // tm,), in_specs=[row, row, vec],
                          out_specs=[row, vec, lspec],
                          out_shape=[jax.ShapeDtypeStruct((m, d), F32), jax.ShapeDtypeStruct((1, d), F32),
                                     jax.ShapeDtypeStruct((1, 128), F32)],
                          compiler_params=_params(1))(x, target, g)


def _rot(xv, cos2, sin2, half):
    return xv * cos2 + pltpu.roll(xv, half, 1) * sin2


def _rot_t(dv, cos2, sin2, half):
    return dv * cos2 + pltpu.roll(dv * sin2, half, 1)


def _ret_pre(proj, cos2, sin2, d, seq, *, ts=512):
    t = proj.shape[0]
    ts = min(ts, seq)
    dk = d // 8
    ns = seq // ts
    scale = float(dk) ** -0.5

    def body(q_ref, k_ref, v_ref, c_ref, s_ref, qo, ko, vo):
        c, s = c_ref[...], s_ref[...]
        for h in range(HEADS):
            sl = slice(h * dk, (h + 1) * dk)
            qo[:, sl] = _rot(q_ref[:, sl].astype(F32), c, s, dk // 2).astype(MM)
            ko[:, sl] = (_rot(k_ref[:, sl].astype(F32), c, s, dk // 2) * scale).astype(MM)
        vo[...] = v_ref[...].astype(MM)

    half = pl.BlockSpec((ts, d // 2), lambda i: (i, 0))
    tab = pl.BlockSpec((ts, dk), lambda i: (i % ns, 0))
    return pl.pallas_call(
        body, name="ret_pre", grid=(t // ts,),
        in_specs=[half, pl.BlockSpec((ts, d // 2), lambda i: (i, 1)), pl.BlockSpec((ts, d), lambda i: (i, 1)),
                  tab, tab],
        out_specs=[half, half, pl.BlockSpec((ts, d), lambda i: (i, 0))],
        out_shape=[jax.ShapeDtypeStruct((t, d // 2), MM), jax.ShapeDtypeStruct((t, d // 2), MM),
                   jax.ShapeDtypeStruct((t, d), MM)],
        compiler_params=_params(1))(proj, proj, proj, cos2, sin2)


def _ret_consts(lg_ref, h, t, dk):
    lf, lb = lg_ref[0, h], lg_ref[1, h]
    ab = (lax.broadcasted_iota(jnp.int32, (t, t), 0) - lax.broadcasted_iota(jnp.int32, (t, t), 1)).astype(F32)
    dmat = jnp.exp(jnp.where(ab >= 0, lf * ab, -lb * ab))
    up = lax.broadcasted_iota(jnp.int32, (t, dk), 0).astype(F32) + 1.0
    down = float(t) - up
    one = jnp.ones((1, 1), F32)
    return dict(ab=ab, dmat=dmat, xi_f=jnp.exp(lf * up), zeta_f=jnp.exp(lf * down), xi_b=jnp.exp(lb * up),
                zeta_b=jnp.exp(lb * down), up=up[:, 0:1], down=down[:, 0:1],
                cf=jnp.exp(one * (lf * t)), cb=jnp.exp(one * (lb * t)))


def _scaled(xv, rows):
    return (xv.astype(F32) * rows).astype(MM)


def _ret_core_fwd(qr, kr, vb, proj, lg, d, bl, seq, *, tc=256):
    t = qr.shape[0]
    dk, dv = d // 8, d // 4
    tc = min(tc, seq)
    nc = seq // tc

    def body(lg_ref, q_ref, k_ref, v_ref, g_ref, o_ref, a_ref, sf_ref, sb_ref):
        c = _ret_consts(lg_ref, pl.program_id(1), tc, dk)

        def rows_of(i):
            return pl.ds(pl.multiple_of(i * tc, tc), tc)

        def fwd_step(i, sf):
            rows = rows_of(i)
            sf_ref[i] = sf
            q, kk, v = q_ref[rows, :], k_ref[rows, :], v_ref[rows, :]
            p = (_dot_nt(q, kk) * c["dmat"]).astype(MM)
            o_ref[rows, :] = _dot(p, v) + _dot(_scaled(q, c["xi_f"]), sf.astype(MM))
            return sf * c["cf"] + _dot_tn(_scaled(kk, c["zeta_f"]), v)

        lax.fori_loop(0, nc, fwd_step, jnp.zeros((dk, dv), F32))

        def bwd_step(ii, sb):
            rows = rows_of(nc - 1 - ii)
            sb_ref[nc - 1 - ii] = sb
            q, kk, v = q_ref[rows, :], k_ref[rows, :], v_ref[rows, :]
            o_ref[rows, :] += _dot(_scaled(q, c["zeta_b"]), sb.astype(MM))
            return sb * c["cb"] + _dot_tn(_scaled(kk, c["xi_b"]), v)

        lax.fori_loop(0, nc, bwd_step, jnp.zeros((dk, dv), F32))

        def post(i, carry):
            rows = rows_of(i)
            o = o_ref[rows, :]
            oc = o - jnp.mean(o, axis=-1, keepdims=True)
            on = oc * lax.rsqrt(jnp.mean(oc * oc, axis=-1, keepdims=True) + EPS)
            g = g_ref[rows, :].astype(F32)
            a_ref[rows, :] = (on * (g * _sigmoid(g))).astype(MM)
            return carry

        lax.fori_loop(0, nc, post, 0)

    qk = pl.BlockSpec((seq, dk), lambda b, h: (b, h))
    vv = pl.BlockSpec((seq, dv), lambda b, h: (b, h))
    states = pl.BlockSpec((None, nc, dk, dv), lambda b, h: (b * HEADS + h, 0, 0, 0))
    return pl.pallas_call(
        body, name="ret_core_fwd", grid=(bl, HEADS),
        in_specs=[pl.BlockSpec(memory_space=pltpu.SMEM), qk, qk, vv,
                  pl.BlockSpec((seq, dv), lambda b, h: (b, 2 * HEADS + h))],
        out_specs=[vv, vv, states, states],
        out_shape=[jax.ShapeDtypeStruct((t, d), F32), jax.ShapeDtypeStruct((t, d), MM),
                   jax.ShapeDtypeStruct((bl * HEADS, nc, dk, dv), F32),
                   jax.ShapeDtypeStruct((bl * HEADS, nc, dk, dv), F32)],
        compiler_params=_params(2))(lg, qr, kr, vb, proj)


def _ret_post_bwd(da, proj, o_raw, d, *, ts=2048):
    t = da.shape[0]
    dv = d // 4
    ts = min(ts, t)

    def body(da_ref, g_ref, o_ref, dg_ref, do_ref):
        o, g, dav = o_ref[...], g_ref[...].astype(F32), da_ref[...]
        mu = jnp.mean(o, axis=-1, keepdims=True)
        oc = o - mu
        r = lax.rsqrt(jnp.mean(oc * oc, axis=-1, keepdims=True) + EPS)
        on = oc * r
        sg = _sigmoid(g)
        don = dav * (g * sg)
        dg_ref[...] = (dav * on * (sg * (1.0 + g * (1.0 - sg)))).astype(MM)
        do = r * (don - jnp.mean(don, axis=-1, keepdims=True) - on * jnp.mean(don * on, axis=-1, keepdims=True))
        do_ref[...] = do.astype(MM)

    blk = pl.BlockSpec((ts, dv), lambda i, h: (i, h))
    return pl.pallas_call(
        body, name="ret_post_bwd", grid=(t // ts, HEADS),
        in_specs=[blk, pl.BlockSpec((ts, dv), lambda i, h: (i, 2 * HEADS + h)), blk],
        out_specs=[blk, blk],
        out_shape=[jax.ShapeDtypeStruct((t, d), MM), jax.ShapeDtypeStruct((t, d), MM)],
        compiler_params=_params(2))(da, proj, o_raw)


def _ret_core_bwd(qr, kr, vb, do, sf_in, sb_in, cos2, sin2, lg, d, bl, seq, *, tc=256):
    t = qr.shape[0]
    dk, dv = d // 8, d // 4
    tc = min(tc, seq)
    nc = seq // tc
    scale = float(dk) ** -0.5

    def body(lg_ref, q_ref, k_ref, v_ref, do_ref, sf_all, sb_all, c_ref, s_ref, dq_ref, dk_ref, dv_ref,
             dlf_ref, dlb_ref, dq_acc, dk_acc, dv_acc):
        c = _ret_consts(lg_ref, pl.program_id(1), tc, dk)
        fwd = c["ab"] >= 0
        zero_state = jnp.zeros((dk, dv), F32)
        zero = jnp.zeros((1, 1), F32)

        def rows_of(i):
            return pl.ds(pl.multiple_of(i * tc, tc), tc)

        def total(xv):
            return jnp.sum(xv, keepdims=True)

        def fwd_sweep(i, carry):
            hh, dlf, dlb = carry
            rows = rows_of(i)
            q, kk, v, dov = q_ref[rows, :], k_ref[rows, :], v_ref[rows, :], do_ref[rows, :]
            dof, vf = dov.astype(F32), v.astype(F32)
            p = _dot_nt(q, kk) * c["dmat"]
            da = _dot_nt(dov, v)
            x = p * da * c["ab"]
            dlf = dlf + total(jnp.where(fwd, x, 0.0))
            dlb = dlb - total(jnp.where(fwd, 0.0, x))
            pb, dpb = p.astype(MM), (da * c["dmat"]).astype(MM)
            dq = _dot(dpb, kk)
            dkc = _dot_tn(dpb, q)
            dvc = _dot_tn(pb, dov)
            sf, sb = sf_all[i], sb_all[i]
            sfb, sbb = sf.astype(MM), sb.astype(MM)
            q_xf, q_zb = _scaled(q, c["xi_f"]), _scaled(q, c["zeta_b"])
            dq = dq + _dot_nt(dov, sfb) * c["xi_f"] + _dot_nt(dov, sbb) * c["zeta_b"]
            dlf = dlf + total(jnp.sum(_dot(q_xf, sfb) * dof, axis=-1, keepdims=True) * c["up"])
            dlb = dlb + total(jnp.sum(_dot(q_zb, sbb) * dof, axis=-1, keepdims=True) * c["down"])
            hb = hh.astype(MM)
            dkc = dkc + _dot_nt(v, hb) * c["xi_b"]
            dv_bx = _dot(_scaled(kk, c["xi_b"]), hb)
            dlb = dlb + total(jnp.sum(vf * dv_bx, axis=-1, keepdims=True) * c["up"])
            dlb = dlb + float(tc) * total(hh * (sb * c["cb"]))
            dq_acc[rows, :] = dq
            dk_acc[rows, :] = dkc
            dv_acc[rows, :] = dvc + dv_bx
            return hh * c["cb"] + _dot_tn(q_zb, dov), dlf, dlb

        _, dlf, dlb = lax.fori_loop(0, nc, fwd_sweep, (zero_state, zero, zero))

        def rev_sweep(ii, carry):
            gg, dlf = carry
            i = nc - 1 - ii
            rows = rows_of(i)
            q, kk, v, dov = q_ref[rows, :], k_ref[rows, :], v_ref[rows, :], do_ref[rows, :]
            gb = gg.astype(MM)
            dk_acc[rows, :] += _dot_nt(v, gb) * c["zeta_f"]
            dv_fx = _dot(_scaled(kk, c["zeta_f"]), gb)
            dv_acc[rows, :] += dv_fx
            dlf = dlf + total(jnp.sum(v.astype(F32) * dv_fx, axis=-1, keepdims=True) * c["down"])
            dlf = dlf + float(tc) * total(gg * (sf_all[i] * c["cf"]))
            return gg * c["cf"] + _dot_tn(_scaled(q, c["xi_f"]), dov), dlf

        _, dlf = lax.fori_loop(0, nc, rev_sweep, (zero_state, dlf))

        cs, sn = c_ref[...], s_ref[...]
        dq_ref[...] = _rot_t(dq_acc[...], cs, sn, dk // 2).astype(MM)
        dk_ref[...] = (_rot_t(dk_acc[...], cs, sn, dk // 2) * scale).astype(MM)
        dv_ref[...] = dv_acc[...].astype(MM)
        dlf_ref[...] = jnp.broadcast_to(dlf, dlf_ref.shape)
        dlb_ref[...] = jnp.broadcast_to(dlb, dlb_ref.shape)

    qk = pl.BlockSpec((seq, dk), lambda b, h: (b, h))
    vv = pl.BlockSpec((seq, dv), lambda b, h: (b, h))
    tab = pl.BlockSpec((seq, dk), lambda b, h: (0, 0))
    dl = pl.BlockSpec((None, 8, 128), lambda b, h: (b * HEADS + h, 0, 0))
    states = pl.BlockSpec((None, nc, dk, dv), lambda b, h: (b * HEADS + h, 0, 0, 0))
    return pl.pallas_call(
        body, name="ret_core_bwd", grid=(bl, HEADS),
        in_specs=[pl.BlockSpec(memory_space=pltpu.SMEM), qk, qk, vv, vv, states, states, tab, tab],
        out_specs=[qk, qk, vv, dl, dl],
        out_shape=[jax.ShapeDtypeStruct((t, d // 2), MM), jax.ShapeDtypeStruct((t, d // 2), MM),
                   jax.ShapeDtypeStruct((t, d), MM),
                   jax.ShapeDtypeStruct((bl * HEADS, 8, 128), F32), jax.ShapeDtypeStruct((bl * HEADS, 8, 128), F32)],
        scratch_shapes=[pltpu.VMEM((seq, dk), F32), pltpu.VMEM((seq, dk), F32), pltpu.VMEM((seq, dv), F32)],
        compiler_params=_params(2))(lg, qr, kr, vb, do, sf_in, sb_in, cos2, sin2)


def _window_count(row, w, seq):
    return (jnp.minimum(row + w // 2, seq) - jnp.maximum(row - w // 2, 0)).astype(F32)


def _window_sum(pv, row, w, seq, sign):
    acc = None
    for j in range(-(w // 2), w // 2):
        if j == 0:
            term = pv
        else:
            src = row + sign * j
            term = jnp.where((src >= 0) & (src < seq), pltpu.roll(pv, (-sign * j) % seq, 0), 0.0)
        acc = term if acc is None else acc + term
    return acc


def _pool_fwd(proj, w_grp, scale, d, bl, seq):
    t = proj.shape[0]
    dg = d // 8

    def body(p_ref, w_ref, s_ref, y_ref):
        row = lax.broadcasted_iota(jnp.int32, (seq, dg), 0)
        for gi, w in enumerate(POOL_WINDOWS):
            sl = slice(gi * dg, (gi + 1) * dg)
            pg = p_ref[:, sl].astype(F32)
            mixed = _window_sum(pg, row, w, seq, 1) / _window_count(row, w, seq) - pg
            yp = _dot(mixed.astype(MM), w_ref[gi].astype(MM))
            y_ref[:, sl] = (yp * s_ref[:, sl]).astype(MM)

    return pl.pallas_call(
        body, name="pool_fwd", grid=(bl,),
        in_specs=[pl.BlockSpec((seq, d // 2), lambda b: (b, 6)),
                  pl.BlockSpec(w_grp.shape, lambda b: (0, 0, 0)),
                  pl.BlockSpec((1, d // 2), lambda b: (0, 0))],
        out_specs=pl.BlockSpec((seq, d // 2), lambda b: (b, 0)),
        out_shape=jax.ShapeDtypeStruct((t, d // 2), MM),
        compiler_params=_params(1))(proj, w_grp, scale)


def _pool_bwd(proj, dy, w_grp, scale, d, bl, seq):
    t = proj.shape[0]
    dg = d // 8

    def body(p_ref, dy_ref, w_ref, s_ref, dp_ref, dw_ref, ds_ref):
        @pl.when(pl.program_id(0) == 0)
        def _():
            dw_ref[...] = jnp.zeros_like(dw_ref)
            ds_ref[...] = jnp.zeros_like(ds_ref)

        row = lax.broadcasted_iota(jnp.int32, (seq, dg), 0)
        for gi, w in enumerate(POOL_WINDOWS):
            sl = slice(gi * dg, (gi + 1) * dg)
            pg = p_ref[:, sl].astype(F32)
            cnt = _window_count(row, w, seq)
            mixb = (_window_sum(pg, row, w, seq, 1) / cnt - pg).astype(MM)
            wgb = w_ref[gi].astype(MM)
            yp = _dot(mixb, wgb)
            dyg = dy_ref[:, sl]
            ds_ref[:, sl] += jnp.sum(dyg * yp, axis=0, keepdims=True)
            dyp = (dyg * s_ref[:, sl]).astype(MM)
            dmixed = _dot_nt(dyp, wgb)
            dw_ref[gi] += _dot_tn(mixb, dyp)
            dp_ref[:, sl] = (_window_sum(dmixed / cnt, row, w, seq, -1) - dmixed).astype(MM)

    half = pl.BlockSpec((seq, d // 2), lambda b: (b, 0))
    wspec = pl.BlockSpec(w_grp.shape, lambda b: (0, 0, 0))
    sspec = pl.BlockSpec((1, d // 2), lambda b: (0, 0))
    return pl.pallas_call(
        body, name="pool_bwd", grid=(bl,),
        in_specs=[pl.BlockSpec((seq, d // 2), lambda b: (b, 6)), half, wspec, sspec],
        out_specs=[half, wspec, sspec],
        out_shape=[jax.ShapeDtypeStruct((t, d // 2), MM), jax.ShapeDtypeStruct(w_grp.shape, F32),
                   jax.ShapeDtypeStruct((1, d // 2), F32)],
        compiler_params=_params(1))(proj, dy, w_grp, scale)


def _attn_probs(q, kk, dh):
    s = _dot_nt(q, kk) * (float(dh) ** -0.5)
    e = jnp.exp(s - jnp.max(s, axis=-1, keepdims=True))
    return e / jnp.sum(e, axis=-1, keepdims=True)


def _attn_fwd(proj, kv, d, bl, seq, mlen, *, tq=2048):
    t = proj.shape[0]
    dh = d // 8
    tq = min(tq, seq)
    nq = seq // tq

    def body(q_ref, k_ref, v_ref, o_ref):
        a = _attn_probs(q_ref[...].astype(MM), k_ref[...].astype(MM), dh)
        o_ref[...] = _dot(a.astype(MM), v_ref[...].astype(MM)).astype(MM)

    return pl.pallas_call(
        body, name="attn_fwd", grid=(bl, HEADS, nq),
        in_specs=[pl.BlockSpec((tq, dh), lambda b, h, i: (b * nq + i, 7 * HEADS + h)),
                  pl.BlockSpec((mlen, dh), lambda b, h, i: (b, h)),
                  pl.BlockSpec((mlen, dh), lambda b, h, i: (b, HEADS + h))],
        out_specs=pl.BlockSpec((tq, dh), lambda b, h, i: (b * nq + i, h)),
        out_shape=jax.ShapeDtypeStruct((t, d // 2), MM),
        compiler_params=_params(3))(proj, kv, kv)


def _attn_bwd(proj, kv, do, d, bl, seq, mlen, *, tq=2048):
    t = proj.shape[0]
    dh = d // 8
    tq = min(tq, seq)
    nq = seq // tq

    def body(q_ref, k_ref, v_ref, do_ref, dq_ref, dk_ref, dv_ref):
        @pl.when(pl.program_id(2) == 0)
        def _():
            dk_ref[...] = jnp.zeros_like(dk_ref)
            dv_ref[...] = jnp.zeros_like(dv_ref)

        q, kk, vv = q_ref[...].astype(MM), k_ref[...].astype(MM), v_ref[...].astype(MM)
        dov = do_ref[...].astype(MM)
        a = _attn_probs(q, kk, dh)
        dp = _dot_nt(dov, vv)
        ds = (a * (dp - jnp.sum(dp * a, axis=-1, keepdims=True)) * (float(dh) ** -0.5)).astype(MM)
        dq_ref[...] = _dot(ds, kk).astype(MM)
        dk_ref[...] += _dot_tn(ds, q)
        dv_ref[...] += _dot_tn(a.astype(MM), dov)

    qs = pl.BlockSpec((tq, dh), lambda b, h, i: (b * nq + i, h))
    ms = pl.BlockSpec((mlen, dh), lambda b, h, i: (b, h))
    return pl.pallas_call(
        body, name="attn_bwd", grid=(bl, HEADS, nq),
        in_specs=[pl.BlockSpec((tq, dh), lambda b, h, i: (b * nq + i, 7 * HEADS + h)), ms,
                  pl.BlockSpec((mlen, dh), lambda b, h, i: (b, HEADS + h)), qs],
        out_specs=[qs, ms, ms],
        out_shape=[jax.ShapeDtypeStruct((t, d // 2), MM), jax.ShapeDtypeStruct((bl * mlen, d // 2), F32),
                   jax.ShapeDtypeStruct((bl * mlen, d // 2), F32)],
        compiler_params=_params(3))(proj, kv, kv, do)


def _comm_call(name, body, arrays, out_shapes):
    n = len(arrays)
    hbm = pl.BlockSpec(memory_space=pl.ANY)
    return pl.pallas_call(
        body, name=name, out_shape=out_shapes, in_specs=[hbm] * n, out_specs=[hbm] * n,
        scratch_shapes=[pltpu.SemaphoreType.DMA((7 * n,)), pltpu.SemaphoreType.DMA((7 * n,)),
                        pltpu.SemaphoreType.DMA((n,))],
    )(*arrays)


def _all_gather(name, shards):
    n = len(shards)

    def body(*refs):
        x_refs, out_refs = refs[:n], refs[n:2 * n]
        send_sems, recv_sems, local_sems = refs[2 * n:]
        x, y, c = lax.axis_index("x"), lax.axis_index("y"), lax.axis_index("c")
        me, sibling = (x, y, c), (x, y, 1 - c)
        chips = [(1 - x, y), (x, 1 - y), (1 - x, 1 - y)]

        def copy(o, k, block, to, src=None):
            slot = out_refs[o].at[4 * block[0] + 2 * block[1] + block[2]]
            return pltpu.make_async_remote_copy(
                src_ref=slot if src is None else src, dst_ref=slot, send_sem=send_sems.at[7 * o + k],
                recv_sem=recv_sems.at[7 * o + k], device_id=to, device_id_type=MESH)

        locals_, remotes = [], []
        for o in range(n):
            mine = pltpu.make_async_copy(x_refs[o], out_refs[o].at[4 * x + 2 * y + c], local_sems.at[o])
            mine.start()
            locals_.append(mine)
            first = [copy(o, 0, me, sibling, src=x_refs[o])]
            first += [copy(o, 1 + j, me, (*chip, c), src=x_refs[o]) for j, chip in enumerate(chips)]
            for cp in first:
                cp.start()
            remotes += first
        for o in range(n):
            for j, chip in enumerate(chips):
                copy(o, 1 + j, (*chip, c), me).wait_recv()
                passed = copy(o, 4 + j, (*chip, c), sibling)
                passed.start()
                remotes.append(passed)
        for o in range(n):
            copy(o, 0, sibling, me).wait_recv()
            for j, chip in enumerate(chips):
                copy(o, 4 + j, (*chip, 1 - c), me).wait_recv()
        for cp in remotes:
            cp.wait_send()
        for mine in locals_:
            mine.wait()

    outs = [jax.ShapeDtypeStruct((N_DEV,) + s.shape, s.dtype) for s in shards]
    return _comm_call(name, body, shards, outs)


def _peer_of(k, x, y, c):
    peer = (1 - x if k & 4 else x, 1 - y if k & 2 else y, 1 - c if k & 1 else c)
    return peer, 4 * peer[0] + 2 * peer[1] + peer[2]


def _split_copies(scatter, srcs, lands, send_sems, recv_sems, arriving):
    x, y, c = lax.axis_index("x"), lax.axis_index("y"), lax.axis_index("c")
    me_idx = 4 * x + 2 * y + c
    copies = []
    for o, (src, land) in enumerate(zip(srcs, lands)):
        for k in range(1, N_DEV):
            peer, p_idx = _peer_of(k, x, y, c)
            mine = src.at[p_idx] if scatter else src
            sems = dict(send_sem=send_sems.at[7 * o + k - 1], recv_sem=recv_sems.at[7 * o + k - 1],
                        device_id=peer, device_id_type=MESH)
            slot = land.at[p_idx] if arriving else land.at[me_idx]
            copies.append(pltpu.make_async_remote_copy(src_ref=mine, dst_ref=slot, **sems))
    return copies


_HBM = pl.BlockSpec(memory_space=pltpu.HBM)
_SEM = pl.BlockSpec(memory_space=pltpu.SEMAPHORE)
_EFFECT = pltpu.SideEffectType.DATAFLOW_SIDE_EFFECTING


def _exchange_start(name, scatter, arrays, after=()):
    n = len(arrays)
    lands = [lax.empty(a.shape if scatter else (N_DEV,) + a.shape, a.dtype) for a in arrays]

    def body(*refs):
        srcs, lnds = refs[:n], refs[n:2 * n]
        send_sems, recv_sems = refs[2 * n + len(after)], refs[2 * n + len(after) + 1]
        token = refs[-1]
        for cp in _split_copies(scatter, srcs, lnds, send_sems, recv_sems, False):
            cp.start()
        token[...] = jnp.zeros_like(token)

    hbm_in = [pltpu.with_memory_space_constraint(a, pltpu.HBM) for a in list(arrays) + lands]
    res = pl.pallas_call(
        body, name=name,
        out_shape=(pltpu.SemaphoreType.DMA((7 * n,)), pltpu.SemaphoreType.DMA((7 * n,)),
                   *[pltpu.HBM(a.shape, a.dtype) for a in hbm_in], jax.ShapeDtypeStruct((8, 128), F32)),
        in_specs=[_HBM] * (2 * n) + [pl.BlockSpec(memory_space=pl.ANY)] * len(after),
        out_specs=(_SEM, _SEM, *[_HBM] * (2 * n), pl.BlockSpec(memory_space=pltpu.VMEM)),
        input_output_aliases={i: 2 + i for i in range(2 * n)},
        compiler_params=pltpu.CompilerParams(has_side_effects=_EFFECT),
    )(*hbm_in, *after)
    return res[0], res[1], list(res[2:2 + n]), list(res[2 + n:2 + 2 * n]), res[-1]


def _exchange_wait(name, scatter, started, after):
    send_sems, recv_sems, srcs, lands, _ = started
    n = len(srcs)

    def body(*refs):
        src_refs, lnd_refs = refs[:n], refs[n:2 * n]
        for cp in _split_copies(scatter, src_refs, lnd_refs, refs[2 * n], refs[2 * n + 1], False):
            cp.wait_send()
        for cp in _split_copies(scatter, src_refs, lnd_refs, refs[2 * n], refs[2 * n + 1], True):
            cp.wait_recv()

    res = pl.pallas_call(
        body, name=name, out_shape=tuple(pltpu.HBM(a.shape, a.dtype) for a in srcs + lands),
        in_specs=[_HBM] * (2 * n) + [_SEM, _SEM, pl.BlockSpec(memory_space=pl.ANY)],
        out_specs=tuple([_HBM] * (2 * n)), input_output_aliases={i: i for i in range(2 * n)},
        compiler_params=pltpu.CompilerParams(has_side_effects=_EFFECT),
    )(*srcs, *lands, send_sems, recv_sems, after)
    return list(res[n:]), list(res[:n])


def _adamw(name, parts, w, m, v, prev, layer, *, tr=256):
    _, a, b = w.shape
    tr = _tile(a, tr, 8)
    c1 = 1.0 - ADAM_B1 ** ADAM_STEP
    c2 = 1.0 - ADAM_B2 ** ADAM_STEP

    def body(p_ref, w_ref, m_ref, v_ref, _g, _d, _m, _v, g_out, d_out, m_out, v_out):
        g = p_ref[0].astype(F32)
        for s in range(1, N_DEV):
            g = g + p_ref[s].astype(F32)
        mn = ADAM_B1 * m_ref[...] + (1.0 - ADAM_B1) * g
        vn = ADAM_B2 * v_ref[...] + (1.0 - ADAM_B2) * (g * g)
        g_out[...] = g
        m_out[...] = mn
        v_out[...] = vn
        d_out[...] = -ADAM_LR * ((mn / c1) / (jnp.sqrt(vn / c2) + ADAM_EPS) + ADAM_WD * w_ref[...])

    slab = pl.BlockSpec((None, tr, b), lambda i: (layer, i, 0))
    whole = pl.BlockSpec(memory_space=pl.ANY)
    return pl.pallas_call(
        body, name=name, grid=(a // tr,),
        in_specs=[pl.BlockSpec((N_DEV, tr, b), lambda i: (0, i, 0)), slab, slab, slab] + [whole] * 4,
        out_specs=[slab] * 4, out_shape=[jax.ShapeDtypeStruct(w.shape, F32)] * 4,
        input_output_aliases={4: 0, 5: 1, 6: 2, 7: 3},
        compiler_params=_params(1))(parts, w, m, v, *prev)


_COL = ("w_in", "w_pool_o", "w_mem_o", "w_ff1")
_COL_IN_PLACE = ("w_in", "w_ff1")
_BIG =("w_in", "w_ret_o", "w_pool_o", "w_mem_kv", "w_mem_o", "w_out", "w_ff1", "w_ff2")
_SMALL = ("ret_decay_logit", "w_pool_grp", "pool_scale", "norm1_g", "norm2_g", "mem_norm_g", "final_norm_g")
_WEIGHTS = ("w_in", "ret_decay_logit", "w_ret_o", "w_pool_grp", "pool_scale", "w_pool_o", "w_mem_kv", "w_mem_o",
            "w_out", "w_ff1", "w_ff2", "norm1_g", "norm2_g", "mem_norm_g", "final_norm_g")


def _small_rows(size, d):
    return -(-size // (8 * d)) * 8


def _pack_small(ws, d):
    parts = []
    for n in _SMALL:
        flat = ws[n].reshape(-1)
        rows = _small_rows(flat.shape[0], d)
        parts.append(jnp.pad(flat, (0, rows * d - flat.shape[0])).reshape(rows, d))
    return jnp.concatenate(parts, axis=0)[None]


def _unpack_small(packed, like, d):
    out, off = {}, 0
    for n in _SMALL:
        rows = _small_rows(like[n].size, d)
        out[n] = packed[0, off:off + rows].reshape(-1)[:like[n].size].reshape(like[n].shape)
        off += rows
    return out


def kernel(x, mem, w_in, ret_decay_logit, w_ret_o, w_pool_grp, pool_scale, w_pool_o, w_mem_kv, w_mem_o, w_out, w_ff1, w_ff2, norm1_g, norm2_g, mem_norm_g, final_norm_g, loss_target, m_w_in, m_ret_decay_logit, m_w_ret_o, m_w_pool_grp, m_pool_scale, m_w_pool_o, m_w_mem_kv, m_w_mem_o, m_w_out, m_w_ff1, m_w_ff2, m_norm1_g, m_norm2_g, m_mem_norm_g, m_final_norm_g, v_w_in, v_ret_decay_logit, v_w_ret_o, v_w_pool_grp, v_pool_scale, v_w_pool_o, v_w_mem_kv, v_w_mem_o, v_w_out, v_w_ff1, v_w_ff2, v_norm1_g, v_norm2_g, v_mem_norm_g, v_final_norm_g):
    w = dict(w_in=w_in, ret_decay_logit=ret_decay_logit, w_ret_o=w_ret_o, w_pool_grp=w_pool_grp,
             pool_scale=pool_scale, w_pool_o=w_pool_o, w_mem_kv=w_mem_kv, w_mem_o=w_mem_o, w_out=w_out,
             w_ff1=w_ff1, w_ff2=w_ff2, norm1_g=norm1_g, norm2_g=norm2_g, mem_norm_g=mem_norm_g,
             final_norm_g=final_norm_g)
    mom = dict(w_in=m_w_in, ret_decay_logit=m_ret_decay_logit, w_ret_o=m_w_ret_o, w_pool_grp=m_w_pool_grp,
               pool_scale=m_pool_scale, w_pool_o=m_w_pool_o, w_mem_kv=m_w_mem_kv, w_mem_o=m_w_mem_o,
               w_out=m_w_out, w_ff1=m_w_ff1, w_ff2=m_w_ff2, norm1_g=m_norm1_g, norm2_g=m_norm2_g,
               mem_norm_g=m_mem_norm_g, final_norm_g=m_final_norm_g)
    vel = dict(w_in=v_w_in, ret_decay_logit=v_ret_decay_logit, w_ret_o=v_w_ret_o, w_pool_grp=v_w_pool_grp,
               pool_scale=v_pool_scale, w_pool_o=v_w_pool_o, w_mem_kv=v_w_mem_kv, w_mem_o=v_w_mem_o,
               w_out=v_w_out, w_ff1=v_w_ff1, w_ff2=v_w_ff2, norm1_g=v_norm1_g, norm2_g=v_norm2_g,
               mem_norm_g=v_mem_norm_g, final_norm_g=v_final_norm_g)

    bl, seq, d = x.shape
    mlen = mem.shape[1]
    depth = w_in.shape[0]
    t = bl * seq
    dk = d // 8

    me_idx = 4 * lax.axis_index("x") + 2 * lax.axis_index("y") + lax.axis_index("c")

    def natural(n, g):
        if n in _COL_IN_PLACE:
            return g
        if n in _COL:
            return jnp.transpose(g, (1, 0, 2)).reshape(g.shape[1], -1)
        return g.reshape(-1, g.shape[-1])

    def finish_gather(name, names, started, after):
        got, mine = _exchange_wait(name, False, started, after)
        return {n: natural(n, lax.dynamic_update_slice(g, sh[None], (me_idx, 0, 0)))
                for n, g, sh in zip(names, got, mine)}

    shards = [{n: w[n][l].astype(MM) for n in _BIG} for l in range(depth)]
    rest = _BIG[1:]
    (w_in0,) = _all_gather("gather_w_in", [shards[0][_BIG[0]]])
    full = [{_BIG[0]: w_in0}]
    pending = _exchange_start("gather_start_0", False, [shards[0][n] for n in rest], after=[w_in0])

    inv = ROPE_BASE ** (-jnp.arange(0, dk, 2, dtype=F32) / dk)
    ang = jnp.arange(seq, dtype=F32)[:, None] * inv[None, :]
    cos2 = jnp.concatenate([jnp.cos(ang), jnp.cos(ang)], axis=-1)
    sin2 = jnp.concatenate([-jnp.sin(ang), jnp.sin(ang)], axis=-1)
    log_g = jax.nn.log_sigmoid(ret_decay_logit)
    x2 = x.reshape(t, d)
    mem2 = mem.reshape(bl * mlen, d)
    gmem = mem_norm_g.reshape(1, d)

    def merge(o_r, o_p, o_m, g_r, g_p, g_m):
        f = lambda z: z.astype(F32)
        return _sigmoid(f(g_r)) * f(o_r) + _sigmoid(f(g_p)) * f(o_p) + _sigmoid(f(g_m)) * f(o_m)

    def relu2(u):
        r = jnp.maximum(u.astype(MM), 0.0)
        return r * r

    def ident(a):
        return a

    saved = []
    xc = x2
    for l in range(depth):
        s = dict(x_in=xc)
        started_now = ()
        if l > 0:
            full.append(finish_gather(f"gather_wait_{l}", _BIG, pending, xc))
            if l + 1 < depth:
                pending = _exchange_start(f"gather_start_{l + 1}", False, [shards[l + 1][n] for n in _BIG],
                                          after=[full[l]["w_in"]])
                started_now = (pending[4],)
        fw = full[l]
        g1 = norm1_g[l].reshape(1, d)
        g2 = norm2_g[l].reshape(1, d)
        s["proj"], s["h1"] = _pmm("proj", _rms_prologue, [(xc, d, 0)], [g1], fw["w_in"], w_mode="col",
                                  tm=1024, tn=1024, save_a=True, out_dtypes=(MM,),
                                  after=started_now if l > 0 else (pending[4],))
        proj = s["proj"]
        s["qr"], s["kr"], s["vb"] = _ret_pre(proj, cos2, sin2, d, seq)
        s["o_raw"], s["a_ret"], s["sf"], s["sb"] = _ret_core_fwd(s["qr"], s["kr"], s["vb"], proj, log_g[l],
                                                                 d, bl, seq)
        s["y"] = _pool_fwd(proj, w_pool_grp[l], pool_scale[l].reshape(1, -1), d, bl, seq)
        started_now = ()
        if l == 0:
            fw.update(finish_gather("gather_wait_0", rest, pending, s["a_ret"]))
            if depth > 1:
                pending = _exchange_start("gather_start_1", False, [shards[1][n] for n in _BIG],
                                          after=[fw["w_mem_kv"]])
                started_now = (pending[4],)
        s["kv"], s["memn"] = _pmm("mem_kv", _rms_prologue, [(mem2, d, 0)], [gmem], fw["w_mem_kv"],
                                  tm=512, tn=512, save_a=True, after=started_now)
        s["o_att"] = _attn_fwd(proj, s["kv"], d, bl, seq, mlen)
        (s["o_ret"],) = _pmm("ret_o", None, [(s["a_ret"], d, 0)], [], fw["w_ret_o"], tm=1024, tn=512,
                             out_dtypes=(MM,))
        (s["o_pool"],) = _pmm("pool_o", None, [(s["y"], d // 2, 0)], [], fw["w_pool_o"],
                              tm=1024, tn=512, out_dtypes=(MM,))
        (s["o_mem"],) = _pmm("mem_o", None, [(s["o_att"], d // 2, 0)], [], fw["w_mem_o"],
                             tm=1024, tn=512, out_dtypes=(MM,))
        s["x_mid"], s["merged"] = _pmm(
            "merge_out", merge,
            [(s["o_ret"], d, 0), (s["o_pool"], d, 0), (s["o_mem"], d, 0), (proj, d, 4), (proj, d, 5), (proj, d, 6)],
            [], fw["w_out"], tm=512, tn=512, residual=xc, save_a=True)
        s["u"], s["h2"] = _pmm("ff1", _rms_prologue, [(s["x_mid"], d, 0)], [g2], fw["w_ff1"], w_mode="col",
                               tm=1024, tn=512, save_a=True, out_dtypes=(MM,))
        (xc,) = _pmm("ff2", relu2, [(s["u"], s["u"].shape[1], 0)], [], fw["w_ff2"],
                     tm=512, tn=512, residual=s["x_mid"])
        saved.append(s)

    dxc, g_final, loss_part = _loss_head(xc, loss_target.reshape(t, d), final_norm_g.reshape(1, d))
    loss = lax.psum(loss_part[0, 0], ("x", "y", "c"))

    small_names = ("w_pool_grp", "pool_scale", "norm1_g", "norm2_g", "ret_decay_logit")
    grads = {n: [None] * depth for n in small_names}
    group_a = ("w_ff1", "w_ff2")
    group_b = tuple(n for n in _BIG if n not in group_a)
    scatters = {}
    dmemn = jnp.zeros((bl * mlen, d), F32)

    def relu2_bwd(acc, u):
        return (acc * (2.0 * jnp.maximum(u.astype(F32), 0.0)),)

    def gates_bwd(acc, g_r, g_p, g_m, o_r, o_p, o_m):
        outs_o, outs_g = [], []
        for gz, oz in ((g_r, o_r), (g_p, o_p), (g_m, o_m)):
            sg = _sigmoid(gz.astype(F32))
            outs_o.append(acc * sg)
            outs_g.append(acc * oz.astype(F32) * (sg * (1.0 - sg)))
        return tuple(outs_o + outs_g)

    def to_send(n, g):
        a, b = w[n].shape[1:]
        if n in _COL_IN_PLACE:
            return g
        if n in _COL:
            return jnp.transpose(g.reshape(a, N_DEV, b), (1, 0, 2))
        return g.reshape(N_DEV, a, b)

    for l in reversed(range(depth)):
        s = saved[l]
        fw = full[l]
        proj = s["proj"]
        g1 = norm1_g[l].reshape(1, d)
        g2 = norm2_g[l].reshape(1, d)
        dw = {}
        (du,) = _pmm("ff2_bwd", ident, [(dxc, d, 0)], [], fw["w_ff2"], w_mode="nt", tm=1024, tn=1024,
                     epilogue=relu2_bwd, epi_ins=[(s["u"], 0)], out_dtypes=(MM,))
        dw["w_ff2"] = _tnmm("dw_ff2", s["u"], dxc, a_fn=relu2)
        dw["w_ff1"] = _tnmm("dw_ff1", s["h2"], du, col_shards=True)
        scatters[l, "a"] = _exchange_start(f"scatter_start_a{l}", True, [to_send(n, dw[n]) for n in group_a])
        dmid, grads["norm2_g"][l] = _mm_rms_bwd("ff1_norm2_bwd", du, fw["w_ff1"], s["x_mid"], g2, dxc, tm=512)
        d_oret, d_opool, d_omem, dgr, dgp, dgm = _pmm(
            "out_bwd", ident, [(dmid, d, 0)], [], fw["w_out"], w_mode="nt", tm=512, tn=512, epilogue=gates_bwd,
            epi_ins=[(proj, 4 * d), (proj, 5 * d), (proj, 6 * d), (s["o_ret"], 0), (s["o_pool"], 0), (s["o_mem"], 0)],
            out_dtypes=(MM,) * 6, after=(scatters[l, "a"][4],))
        dw["w_out"] = _tnmm("dw_out", s["merged"], dmid)
        (da_ret,) = _pmm("ret_o_bwd", None, [(d_oret, d, 0)], [], fw["w_ret_o"], w_mode="nt", tm=1024, tn=512)
        dw["w_ret_o"] = _tnmm("dw_ret_o", s["a_ret"], d_oret)
        (dy,) = _pmm("pool_o_bwd", None, [(d_opool, d, 0)], [], fw["w_pool_o"], w_mode="nt", tm=1024, tn=512)
        dw["w_pool_o"] = _tnmm("dw_pool_o", s["y"], d_opool)
        (do_att,) = _pmm("mem_o_bwd", None, [(d_omem, d, 0)], [], fw["w_mem_o"], w_mode="nt", tm=1024, tn=512)
        dw["w_mem_o"] = _tnmm("dw_mem_o", s["o_att"], d_omem)
        dg_ret, do_ret = _ret_post_bwd(da_ret, proj, s["o_raw"], d)
        dq, dkk, dvv, dlf, dlb = _ret_core_bwd(s["qr"], s["kr"], s["vb"], do_ret, s["sf"], s["sb"], cos2, sin2,
                                               log_g[l], d, bl, seq)
        dl = jnp.stack([dlf[:, 0, 0].reshape(bl, HEADS).sum(0), dlb[:, 0, 0].reshape(bl, HEADS).sum(0)])
        grads["ret_decay_logit"][l] = dl * jax.nn.sigmoid(-ret_decay_logit[l])
        dp, grads["w_pool_grp"][l], dscale = _pool_bwd(proj, dy, w_pool_grp[l], pool_scale[l].reshape(1, -1),
                                                       d, bl, seq)
        grads["pool_scale"][l] = dscale.reshape(-1)
        dqm, dmk, dmv = _attn_bwd(proj, s["kv"], do_att, d, bl, seq, mlen)
        dkv = jnp.concatenate([dmk, dmv], axis=-1).astype(MM)
        dw["w_mem_kv"] = _tnmm("dw_mem_kv", s["memn"], dkv)
        (dmemn,) = _pmm("mem_kv_bwd", None, [(dkv, d, 0)], [], fw["w_mem_kv"], w_mode="nt", tm=512, tn=512,
                        residual=dmemn)
        dproj = [dq, dkk, dvv, dg_ret, dp, dqm, dgr, dgp, dgm]
        dw["w_in"] = _tnmm("dw_in", s["h1"], dproj, col_shards=True, tm=512, tk=512)
        scatters[l, "b"] = _exchange_start(f"scatter_start_b{l}", True, [to_send(n, dw[n]) for n in group_b])
        dxc, grads["norm1_g"][l] = _mm_rms_bwd("proj_norm1_bwd", dproj, fw["w_in"], s["x_in"], g1, dmid, tm=256,
                                               after=(scatters[l, "b"][4],))

    _, g_memn = _rms_bwd("mem_norm_bwd", dmemn, mem2, gmem, None)
    grad_x = dxc.reshape(bl, seq, d)

    small_g = dict(ret_decay_logit=jnp.stack(grads["ret_decay_logit"]), w_pool_grp=jnp.stack(grads["w_pool_grp"]),
                   pool_scale=jnp.stack(grads["pool_scale"]),
                   norm1_g=jnp.concatenate(grads["norm1_g"], axis=0), norm2_g=jnp.concatenate(grads["norm2_g"], axis=0),
                   mem_norm_g=g_memn.reshape(-1), final_norm_g=g_final.reshape(-1))
    (small_parts,) = _all_gather("gather_small_grads", [_pack_small(small_g, d)[0]])
    w_small = _pack_small(w, d)
    small = _adamw("adamw_small", small_parts, w_small, _pack_small(mom, d), _pack_small(vel, d),
                   [lax.empty(w_small.shape, F32) for _ in range(4)], 0)
    small = [_unpack_small(o, w, d) for o in small]

    big = {n: [lax.empty(w[n].shape, F32) for _ in range(4)] for n in _BIG}
    after = small[0]["w_pool_grp"]
    for l in reversed(range(depth)):
        for grp, names in (("a", group_a), ("b", group_b)):
            recv, sent = _exchange_wait(f"scatter_wait_{grp}{l}", True, scatters[l, grp], after)
            for n, r, snt in zip(names, recv, sent):
                own = lax.dynamic_slice_in_dim(snt, me_idx, 1, axis=0)
                parts = lax.dynamic_update_slice(r, own, (me_idx, 0, 0))
                big[n] = _adamw("adamw_" + n, parts, w[n], mom[n], vel[n], big[n], l)
            after = big[names[-1]][0]

    outs = [loss, grad_x]
    for k in range(4):
        outs += [big[n][k] if n in _BIG else small[k][n] for n in _WEIGHTS]
    return tuple(outs)
```

```python
import jax
import jax.numpy as jnp
from jax import lax
from jax.experimental import pallas as pl
from jax.experimental.pallas import tpu as pltpu

F32 = jnp.float32
MM = jnp.bfloat16
N_DEV = 8
HEADS = 4
POOL_WINDOWS = (2, 4, 8, 16)
EPS = 1e-6
ROPE_BASE = 10000.0
ADAM_LR, ADAM_B1, ADAM_B2, ADAM_EPS, ADAM_WD, ADAM_STEP = 0.001, 0.9, 0.999, 1e-08, 0.01, 10
V7X_VMEM_LIMIT = 56 * 1024 * 1024
MESH = pl.DeviceIdType.MESH


def _params(n_axes):
    return pltpu.CompilerParams(dimension_semantics=("arbitrary",) * n_axes,
                                vmem_limit_bytes=V7X_VMEM_LIMIT)


def _tile(n, pref, align=128):
    cands = [c for c in range(align, min(pref, n) + 1, align) if n % c == 0]
    return max(cands) if cands else n


def _sigmoid(z):
    return 0.5 * jnp.tanh(0.5 * z) + 0.5


def _dot(a, b):
    return jnp.dot(a, b, preferred_element_type=F32)


def _dot_nt(a, b):
    return lax.dot_general(a, b, (((1,), (1,)), ((), ())), preferred_element_type=F32)


def _dot_tn(a, b):
    return lax.dot_general(a, b, (((0,), (0,)), ((), ())), preferred_element_type=F32)


def _pmm(name, prologue, row_ins, vec_ins, w, *, tm, tn, w_mode="nn", residual=None, save_a=False,
         epilogue=None, epi_ins=(), out_dtypes=(F32,), after=(), extra_outs=(), epi_full=(), out_widths=None):
    m = row_ins[0][0].shape[0]
    wb = None
    if w_mode == "nn":
        k, n = w.shape
        tn = _tile(n, tn)
        w_spec = pl.BlockSpec((k, tn), lambda i, j: (0, j))
    elif w_mode == "nt":
        n, k = w.shape
        tn = _tile(n, tn)
        w_spec = pl.BlockSpec((tn, k), lambda i, j: (j, 0))
    elif w_mode == "col":
        _, k, wb = w.shape
        n = N_DEV * wb
        tn = _tile(wb, tn)
        w_spec = pl.BlockSpec((None, k, tn), lambda i, j, q=wb // tn: (j // q, 0, j % q))
    else:
        _, n, wb = w.shape
        k = N_DEV * wb
        tn = _tile(n, tn)
        w_spec = pl.BlockSpec((N_DEV, tn, wb), lambda i, j: (0, j, 0))
    tm = _tile(m, tm, 8)
    n_row, n_vec, n_epi, n_out = len(row_ins), len(vec_ins), len(epi_ins), len(out_dtypes)
    has_res = residual is not None
    use_scr = prologue is not None
    out_widths = [n] * n_out if out_widths is None else list(out_widths)
    assert all(wd == n for wd in out_widths) or tn == n

    def body(*refs):
        row_refs = refs[:n_row]
        p = n_row
        vec_refs = refs[p:p + n_vec]
        p += n_vec
        w_ref = refs[p]
        p += 1
        res_ref = refs[p] if has_res else None
        p += int(has_res)
        epi_refs = refs[p:p + n_epi + len(epi_full)]
        p += n_epi + len(epi_full) + len(after)
        out_refs = refs[p:p + n_out]
        p += n_out
        a_out = refs[p] if save_a else None
        p += int(save_a)
        extra_refs = refs[p:p + len(extra_outs)]
        p += len(extra_outs)
        if use_scr:
            a_src = refs[p]

            @pl.when(pl.program_id(1) == 0)
            def _():
                made = prologue(*[r[...] for r in row_refs], *[v[...] for v in vec_refs])
                made = made if isinstance(made, tuple) else (made,)
                a = made[0].astype(MM)
                a_src[...] = a
                if save_a:
                    a_out[...] = a
                for e_ref, e in zip(extra_refs, made[1:]):
                    e_ref[...] = e.astype(e_ref.dtype)
        else:
            a_src = row_refs[0]
        if w_mode == "nt":
            acc = _dot_nt(a_src[...], w_ref[...])
        elif w_mode == "col_t":
            acc = _dot_nt(a_src[:, 0:wb], w_ref[0])
            for dev in range(1, N_DEV):
                acc = acc + _dot_nt(a_src[:, dev * wb:(dev + 1) * wb], w_ref[dev])
        else:
            acc = _dot(a_src[...], w_ref[...])
        if has_res:
            acc = acc + res_ref[...]
        outs = epilogue(acc, *[e[...] for e in epi_refs]) if epilogue is not None else (acc,)
        for o_ref, o in zip(out_refs, outs):
            o_ref[...] = o.astype(o_ref.dtype)

    in_specs = [pl.BlockSpec((tm, wd), lambda i, j, cb=cb: (i, cb)) for (_, wd, cb) in row_ins]
    in_specs += [pl.BlockSpec(v.shape, lambda i, j: (0, 0)) for v in vec_ins]
    in_specs += [w_spec]
    args = [r[0] for r in row_ins] + list(vec_ins) + [w]
    if has_res:
        in_specs.append(pl.BlockSpec((tm, tn), lambda i, j: (i, j)))
        args.append(residual)
    for (arr, off) in epi_ins:
        assert off % tn == 0
        in_specs.append(pl.BlockSpec((tm, tn), lambda i, j, ob=off // tn: (i, ob + j)))
        args.append(arr)
    in_specs += [pl.BlockSpec(v.shape, lambda i, j: (0, 0)) for v in epi_full]
    args += list(epi_full)
    n_after = len(after)
    in_specs += [pl.BlockSpec(memory_space=pl.ANY)] * n_after
    args += list(after)
    out_specs = [pl.BlockSpec((tm, tn if wd == n else wd), lambda i, j: (i, j)) for wd in out_widths]
    out_shape = [jax.ShapeDtypeStruct((m, wd), dt) for wd, dt in zip(out_widths, out_dtypes)]
    if save_a:
        out_specs.append(pl.BlockSpec((tm, k), lambda i, j: (i, 0)))
        out_shape.append(jax.ShapeDtypeStruct((m, k), MM))
    for wd, dt in extra_outs:
        out_specs.append(pl.BlockSpec((tm, wd), lambda i, j: (i, 0)))
        out_shape.append(jax.ShapeDtypeStruct((m, wd), dt))
    scratch = [pltpu.VMEM((tm, k), MM)] if use_scr else []
    return pl.pallas_call(body, name=name, grid=(m // tm, n // tn), in_specs=in_specs,
                          out_specs=out_specs, out_shape=out_shape, scratch_shapes=scratch,
                          compiler_params=_params(2))(*args)


def _tnmm(name, a, b, *, tm=1024, tn=1024, tk=1024, col_shards=False, a_fn=None):
    t, m = a.shape
    pieces = list(b) if isinstance(b, (list, tuple)) else [b]
    widths = [p.shape[1] for p in pieces]
    offs = [sum(widths[:p]) for p in range(len(pieces))]
    n = sum(widths)
    tm, tk = _tile(m, tm), _tile(t, tk, 8)
    per_tile = 1
    if col_shards:
        wb = n // N_DEV
        if len(pieces) > 1:
            tn = n
        while 2 * per_tile * wb <= tn and 2 * per_tile <= N_DEV:
            per_tile *= 2
        tn = per_tile * wb
        out_spec = pl.BlockSpec((per_tile, tm, wb), lambda i, j, kk: (j, i, 0))
        out_shape = jax.ShapeDtypeStruct((N_DEV, m, wb), MM)
    else:
        tn = _tile(n, tn)
        out_spec = pl.BlockSpec((tm, tn), lambda i, j, kk: (i, j))
        out_shape = jax.ShapeDtypeStruct((m, n), MM)
    nk = t // tk

    assert len(pieces) == 1 or tn == n

    def body(a_ref, *rest):
        b_refs, (o_ref, acc) = rest[:len(pieces)], rest[len(pieces):]
        kk = pl.program_id(2)

        @pl.when(kk == 0)
        def _():
            acc[...] = jnp.zeros_like(acc)

        av = (a_ref[...] if a_fn is None else a_fn(a_ref[...])).astype(MM)
        if len(pieces) == 1:
            acc[...] += _dot_tn(av, b_refs[0][...].astype(MM))
        else:
            for b_ref, off, wd in zip(b_refs, offs, widths):
                acc[:, off:off + wd] += _dot_tn(av, b_ref[...].astype(MM))

        @pl.when(kk == nk - 1)
        def _():
            if col_shards:
                for sh in range(per_tile):
                    o_ref[sh] = acc[:, sh * wb:(sh + 1) * wb].astype(o_ref.dtype)
            else:
                o_ref[...] = acc[...].astype(o_ref.dtype)

    return pl.pallas_call(
        body, name=name, grid=(m // tm, n // tn, nk),
        in_specs=[pl.BlockSpec((tk, tm), lambda i, j, kk: (kk, i))]
        + [pl.BlockSpec((tk, tn if len(pieces) == 1 else wd), lambda i, j, kk: (kk, j)) for wd in widths],
        out_specs=out_spec, out_shape=out_shape,
        scratch_shapes=[pltpu.VMEM((tm, tn), F32)],
        compiler_params=_params(3))(a, *pieces)


def _rms_prologue(x, g):
    r = lax.rsqrt(jnp.mean(x * x, axis=-1, keepdims=True) + EPS)
    return x * r * g


def _rms_bwd_rows(dh, x, g):
    d = x.shape[-1]
    r = lax.rsqrt(jnp.mean(x * x, axis=-1, keepdims=True) + EPS)
    xh = x * r
    dxh = dh * g
    dx = r * (dxh - xh * (jnp.sum(dxh * xh, axis=-1, keepdims=True) / d))
    dg = jnp.sum(dh * xh, axis=0, keepdims=True)
    return dx, dg


def _rms_bwd(name, dh, x, g, dres, *, tm=512):
    m, d = x.shape
    tm = min(tm, m)
    has_res = dres is not None

    def body(*refs):
        if has_res:
            dh_ref, x_ref, g_ref, r_ref, dx_ref, dg_ref = refs
        else:
            dh_ref, x_ref, g_ref, dx_ref, dg_ref = refs
        dx, dg = _rms_bwd_rows(dh_ref[...], x_ref[...], g_ref[...])
        if has_res:
            dx = dx + r_ref[...]
        dx_ref[...] = dx

        @pl.when(pl.program_id(0) == 0)
        def _():
            dg_ref[...] = jnp.zeros_like(dg_ref)

        dg_ref[...] += dg

    row = pl.BlockSpec((tm, d), lambda i: (i, 0))
    vec = pl.BlockSpec((1, d), lambda i: (0, 0))
    in_specs = [row, row, vec] + ([row] if has_res else [])
    args = [dh, x, g] + ([dres] if has_res else [])
    return pl.pallas_call(body, name=name, grid=(m // tm,), in_specs=in_specs, out_specs=[row, vec],
                          out_shape=[jax.ShapeDtypeStruct((m, d), F32), jax.ShapeDtypeStruct((1, d), F32)],
                          compiler_params=_params(1))(*args)


def _mm_rms_bwd(name, a, w, x, g, dres, *, tm, after=()):
    pieces = list(a) if isinstance(a, (list, tuple)) else [a]
    widths = [p.shape[1] for p in pieces]
    offs = [sum(widths[:p]) for p in range(len(pieces))]
    m = pieces[0].shape[0]
    _, d, wb = w.shape
    tm = _tile(m, tm, 8)
    n_a = len(pieces)

    def body(*refs):
        a_refs = refs[:n_a]
        w_ref, x_ref, g_ref, r_ref = refs[n_a:n_a + 4]
        dx_ref, dg_ref = refs[n_a + 4 + len(after):]

        def window(lo, hi):
            parts = [a_ref[:, max(lo, off) - off:min(hi, off + wd) - off]
                     for a_ref, off, wd in zip(a_refs, offs, widths) if min(hi, off + wd) > max(lo, off)]
            return parts[0] if len(parts) == 1 else jnp.concatenate(parts, axis=1)

        dh = _dot_nt(window(0, wb), w_ref[0])
        for dev in range(1, N_DEV):
            dh = dh + _dot_nt(window(dev * wb, (dev + 1) * wb), w_ref[dev])
        dx, dg = _rms_bwd_rows(dh, x_ref[...], g_ref[...])
        dx_ref[...] = dx + r_ref[...]

        @pl.when(pl.program_id(0) == 0)
        def _():
            dg_ref[...] = jnp.zeros_like(dg_ref)

        dg_ref[...] += dg

    row = pl.BlockSpec((tm, d), lambda i: (i, 0))
    vec = pl.BlockSpec((1, d), lambda i: (0, 0))
    return pl.pallas_call(
        body, name=name, grid=(m // tm,),
        in_specs=[pl.BlockSpec((tm, wd), lambda i: (i, 0)) for wd in widths]
        + [pl.BlockSpec(w.shape, lambda i: (0, 0, 0)), row, vec, row]
        + [pl.BlockSpec(memory_space=pl.ANY)] * len(after),
        out_specs=[row, vec],
        out_shape=[jax.ShapeDtypeStruct((m, d), F32), jax.ShapeDtypeStruct((1, d), F32)],
        compiler_params=_params(1))(*pieces, w, x, g, dres, *after)


def _loss_head(x, target, g, *, tm=256):
    m, d = x.shape
    tm = min(tm, m)

    def body(x_ref, t_ref, g_ref, dx_ref, dg_ref, loss_ref):
        xv, gv = x_ref[...], g_ref[...]
        y = _rms_prologue(xv, gv)
        err = y - t_ref[...]
        part = 0.5 * jnp.sum(jnp.sum(err * err, axis=-1, keepdims=True) / d)
        dx, dg = _rms_bwd_rows(err / d, xv, gv)
        dx_ref[...] = dx

        @pl.when(pl.program_id(0) == 0)
        def _():
            dg_ref[...] = jnp.zeros_like(dg_ref)
            loss_ref[...] = jnp.zeros_like(loss_ref)

        dg_ref[...] += dg
        loss_ref[...] += jnp.full(loss_ref.shape, part, F32)

    row = pl.BlockSpec((tm, d), lambda i: (i, 0))
    vec = pl.BlockSpec((1, d), lambda i: (0, 0))
    lspec = pl.BlockSpec((1, 128), lambda i: (0, 0))
    return pl.pallas_call(body, name="loss_head", grid=(m // tm,), in_specs=[row, row, vec],
                          out_specs=[row, vec, lspec],
                          out_shape=[jax.ShapeDtypeStruct((m, d), F32), jax.ShapeDtypeStruct((1, d), F32),
                                     jax.ShapeDtypeStruct((1, 128), F32)],
                          compiler_params=_params(1))(x, target, g)


def _rot(xv, cos2, sin2, half):
    return xv * cos2 + pltpu.roll(xv, half, 1) * sin2


def _rot_t(dv, cos2, sin2, half):
    return dv * cos2 + pltpu.roll(dv * sin2, half, 1)


def _ret_pre(proj, cos2, sin2, d, seq, *, ts=512):
    t = proj.shape[0]
    ts = min(ts, seq)
    dk = d // 8
    ns = seq // ts
    scale = float(dk) ** -0.5

    def body(q_ref, k_ref, v_ref, c_ref, s_ref, qo, ko, vo):
        c, s = c_ref[...], s_ref[...]
        for h in range(HEADS):
            sl = slice(h * dk, (h + 1) * dk)
            qo[:, sl] = _rot(q_ref[:, sl].astype(F32), c, s, dk // 2).astype(MM)
            ko[:, sl] = (_rot(k_ref[:, sl].astype(F32), c, s, dk // 2) * scale).astype(MM)
        vo[...] = v_ref[...].astype(MM)

    half = pl.BlockSpec((ts, d // 2), lambda i: (i, 0))
    tab = pl.BlockSpec((ts, dk), lambda i: (i % ns, 0))
    return pl.pallas_call(
        body, name="ret_pre", grid=(t // ts,),
        in_specs=[half, pl.BlockSpec((ts, d // 2), lambda i: (i, 1)), pl.BlockSpec((ts, d), lambda i: (i, 1)),
                  tab, tab],
        out_specs=[half, half, pl.BlockSpec((ts, d), lambda i: (i, 0))],
        out_shape=[jax.ShapeDtypeStruct((t, d // 2), MM), jax.ShapeDtypeStruct((t, d // 2), MM),
                   jax.ShapeDtypeStruct((t, d), MM)],
        compiler_params=_params(1))(proj, proj, proj, cos2, sin2)


def _ret_consts(lg_ref, h, t, dk):
    lf, lb = lg_ref[0, h], lg_ref[1, h]
    ab = (lax.broadcasted_iota(jnp.int32, (t, t), 0) - lax.broadcasted_iota(jnp.int32, (t, t), 1)).astype(F32)
    dmat = jnp.exp(jnp.where(ab >= 0, lf * ab, -lb * ab))
    up = lax.broadcasted_iota(jnp.int32, (t, dk), 0).astype(F32) + 1.0
    down = float(t) - up
    one = jnp.ones((1, 1), F32)
    return dict(ab=ab, dmat=dmat, xi_f=jnp.exp(lf * up), zeta_f=jnp.exp(lf * down), xi_b=jnp.exp(lb * up),
                zeta_b=jnp.exp(lb * down), up=up[:, 0:1], down=down[:, 0:1],
                cf=jnp.exp(one * (lf * t)), cb=jnp.exp(one * (lb * t)))


def _scaled(xv, rows):
    return (xv.astype(F32) * rows).astype(MM)


def _ret_core_fwd(qr, kr, vb, proj, lg, d, bl, seq, *, tc=256):
    t = qr.shape[0]
    dk, dv = d // 8, d // 4
    tc = min(tc, seq)
    nc = seq // tc

    def body(lg_ref, q_ref, k_ref, v_ref, g_ref, o_ref, a_ref, sf_ref, sb_ref):
        c = _ret_consts(lg_ref, pl.program_id(1), tc, dk)

        def rows_of(i):
            return pl.ds(pl.multiple_of(i * tc, tc), tc)

        def fwd_step(i, sf):
            rows = rows_of(i)
            sf_ref[i] = sf
            q, kk, v = q_ref[rows, :], k_ref[rows, :], v_ref[rows, :]
            p = (_dot_nt(q, kk) * c["dmat"]).astype(MM)
            o_ref[rows, :] = _dot(p, v) + _dot(_scaled(q, c["xi_f"]), sf.astype(MM))
            return sf * c["cf"] + _dot_tn(_scaled(kk, c["zeta_f"]), v)

        lax.fori_loop(0, nc, fwd_step, jnp.zeros((dk, dv), F32))

        def bwd_step(ii, sb):
            rows = rows_of(nc - 1 - ii)
            sb_ref[nc - 1 - ii] = sb
            q, kk, v = q_ref[rows, :], k_ref[rows, :], v_ref[rows, :]
            o_ref[rows, :] += _dot(_scaled(q, c["zeta_b"]), sb.astype(MM))
            return sb * c["cb"] + _dot_tn(_scaled(kk, c["xi_b"]), v)

        lax.fori_loop(0, nc, bwd_step, jnp.zeros((dk, dv), F32))

        def post(i, carry):
            rows = rows_of(i)
            o = o_ref[rows, :]
            oc = o - jnp.mean(o, axis=-1, keepdims=True)
            on = oc * lax.rsqrt(jnp.mean(oc * oc, axis=-1, keepdims=True) + EPS)
            g = g_ref[rows, :].astype(F32)
            a_ref[rows, :] = (on * (g * _sigmoid(g))).astype(MM)
            return carry

        lax.fori_loop(0, nc, post, 0)

    qk = pl.BlockSpec((seq, dk), lambda b, h: (b, h))
    vv = pl.BlockSpec((seq, dv), lambda b, h: (b, h))
    states = pl.BlockSpec((None, nc, dk, dv), lambda b, h: (b * HEADS + h, 0, 0, 0))
    return pl.pallas_call(
        body, name="ret_core_fwd", grid=(bl, HEADS),
        in_specs=[pl.BlockSpec(memory_space=pltpu.SMEM), qk, qk, vv,
                  pl.BlockSpec((seq, dv), lambda b, h: (b, 2 * HEADS + h))],
        out_specs=[vv, vv, states, states],
        out_shape=[jax.ShapeDtypeStruct((t, d), F32), jax.ShapeDtypeStruct((t, d), MM),
                   jax.ShapeDtypeStruct((bl * HEADS, nc, dk, dv), F32),
                   jax.ShapeDtypeStruct((bl * HEADS, nc, dk, dv), F32)],
        compiler_params=_params(2))(lg, qr, kr, vb, proj)


def _ret_post_bwd(da, proj, o_raw, d, *, ts=2048):
    t = da.shape[0]
    dv = d // 4
    ts = min(ts, t)

    def body(da_ref, g_ref, o_ref, dg_ref, do_ref):
        o, g, dav = o_ref[...], g_ref[...].astype(F32), da_ref[...]
        mu = jnp.mean(o, axis=-1, keepdims=True)
        oc = o - mu
        r = lax.rsqrt(jnp.mean(oc * oc, axis=-1, keepdims=True) + EPS)
        on = oc * r
        sg = _sigmoid(g)
        don = dav * (g * sg)
        dg_ref[...] = (dav * on * (sg * (1.0 + g * (1.0 - sg)))).astype(MM)
        do = r * (don - jnp.mean(don, axis=-1, keepdims=True) - on * jnp.mean(don * on, axis=-1, keepdims=True))
        do_ref[...] = do.astype(MM)

    blk = pl.BlockSpec((ts, dv), lambda i, h: (i, h))
    return pl.pallas_call(
        body, name="ret_post_bwd", grid=(t // ts, HEADS),
        in_specs=[blk, pl.BlockSpec((ts, dv), lambda i, h: (i, 2 * HEADS + h)), blk],
        out_specs=[blk, blk],
        out_shape=[jax.ShapeDtypeStruct((t, d), MM), jax.ShapeDtypeStruct((t, d), MM)],
        compiler_params=_params(2))(da, proj, o_raw)


def _ret_core_bwd(qr, kr, vb, do, sf_in, sb_in, cos2, sin2, lg, d, bl, seq, *, tc=256):
    t = qr.shape[0]
    dk, dv = d // 8, d // 4
    tc = min(tc, seq)
    nc = seq // tc
    scale = float(dk) ** -0.5

    def body(lg_ref, q_ref, k_ref, v_ref, do_ref, sf_all, sb_all, c_ref, s_ref, dq_ref, dk_ref, dv_ref,
             dlf_ref, dlb_ref, dq_acc, dk_acc, dv_acc):
        c = _ret_consts(lg_ref, pl.program_id(1), tc, dk)
        fwd = c["ab"] >= 0
        zero_state = jnp.zeros((dk, dv), F32)
        zero = jnp.zeros((1, 1), F32)

        def rows_of(i):
            return pl.ds(pl.multiple_of(i * tc, tc), tc)

        def total(xv):
            return jnp.sum(xv, keepdims=True)

        def fwd_sweep(i, carry):
            hh, dlf, dlb = carry
            rows = rows_of(i)
            q, kk, v, dov = q_ref[rows, :], k_ref[rows, :], v_ref[rows, :], do_ref[rows, :]
            dof, vf = dov.astype(F32), v.astype(F32)
            p = _dot_nt(q, kk) * c["dmat"]
            da = _dot_nt(dov, v)
            x = p * da * c["ab"]
            dlf = dlf + total(jnp.where(fwd, x, 0.0))
            dlb = dlb - total(jnp.where(fwd, 0.0, x))
            pb, dpb = p.astype(MM), (da * c["dmat"]).astype(MM)
            dq = _dot(dpb, kk)
            dkc = _dot_tn(dpb, q)
            dvc = _dot_tn(pb, dov)
            sf, sb = sf_all[i], sb_all[i]
            sfb, sbb = sf.astype(MM), sb.astype(MM)
            q_xf, q_zb = _scaled(q, c["xi_f"]), _scaled(q, c["zeta_b"])
            dq = dq + _dot_nt(dov, sfb) * c["xi_f"] + _dot_nt(dov, sbb) * c["zeta_b"]
            dlf = dlf + total(jnp.sum(_dot(q_xf, sfb) * dof, axis=-1, keepdims=True) * c["up"])
            dlb = dlb + total(jnp.sum(_dot(q_zb, sbb) * dof, axis=-1, keepdims=True) * c["down"])
            hb = hh.astype(MM)
            dkc = dkc + _dot_nt(v, hb) * c["xi_b"]
            dv_bx = _dot(_scaled(kk, c["xi_b"]), hb)
            dlb = dlb + total(jnp.sum(vf * dv_bx, axis=-1, keepdims=True) * c["up"])
            dlb = dlb + float(tc) * total(hh * (sb * c["cb"]))
            dq_acc[rows, :] = dq
            dk_acc[rows, :] = dkc
            dv_acc[rows, :] = dvc + dv_bx
            return hh * c["cb"] + _dot_tn(q_zb, dov), dlf, dlb

        _, dlf, dlb = lax.fori_loop(0, nc, fwd_sweep, (zero_state, zero, zero))

        def rev_sweep(ii, carry):
            gg, dlf = carry
            i = nc - 1 - ii
            rows = rows_of(i)
            q, kk, v, dov = q_ref[rows, :], k_ref[rows, :], v_ref[rows, :], do_ref[rows, :]
            gb = gg.astype(MM)
            dk_acc[rows, :] += _dot_nt(v, gb) * c["zeta_f"]
            dv_fx = _dot(_scaled(kk, c["zeta_f"]), gb)
            dv_acc[rows, :] += dv_fx
            dlf = dlf + total(jnp.sum(v.astype(F32) * dv_fx, axis=-1, keepdims=True) * c["down"])
            dlf = dlf + float(tc) * total(gg * (sf_all[i] * c["cf"]))
            return gg * c["cf"] + _dot_tn(_scaled(q, c["xi_f"]), dov), dlf

        _, dlf = lax.fori_loop(0, nc, rev_sweep, (zero_state, dlf))

        cs, sn = c_ref[...], s_ref[...]
        dq_ref[...] = _rot_t(dq_acc[...], cs, sn, dk // 2).astype(MM)
        dk_ref[...] = (_rot_t(dk_acc[...], cs, sn, dk // 2) * scale).astype(MM)
        dv_ref[...] = dv_acc[...].astype(MM)
        dlf_ref[...] = jnp.broadcast_to(dlf, dlf_ref.shape)
        dlb_ref[...] = jnp.broadcast_to(dlb, dlb_ref.shape)

    qk = pl.BlockSpec((seq, dk), lambda b, h: (b, h))
    vv = pl.BlockSpec((seq, dv), lambda b, h: (b, h))
    tab = pl.BlockSpec((seq, dk), lambda b, h: (0, 0))
    dl = pl.BlockSpec((None, 8, 128), lambda b, h: (b * HEADS + h, 0, 0))
    states = pl.BlockSpec((None, nc, dk, dv), lambda b, h: (b * HEADS + h, 0, 0, 0))
    return pl.pallas_call(
        body, name="ret_core_bwd", grid=(bl, HEADS),
        in_specs=[pl.BlockSpec(memory_space=pltpu.SMEM), qk, qk, vv, vv, states, states, tab, tab],
        out_specs=[qk, qk, vv, dl, dl],
        out_shape=[jax.ShapeDtypeStruct((t, d // 2), MM), jax.ShapeDtypeStruct((t, d // 2), MM),
                   jax.ShapeDtypeStruct((t, d), MM),
                   jax.ShapeDtypeStruct((bl * HEADS, 8, 128), F32), jax.ShapeDtypeStruct((bl * HEADS, 8, 128), F32)],
        scratch_shapes=[pltpu.VMEM((seq, dk), F32), pltpu.VMEM((seq, dk), F32), pltpu.VMEM((seq, dv), F32)],
        compiler_params=_params(2))(lg, qr, kr, vb, do, sf_in, sb_in, cos2, sin2)


def _window_count(row, w, seq):
    return (jnp.minimum(row + w // 2, seq) - jnp.maximum(row - w // 2, 0)).astype(F32)


def _window_sum(pv, row, w, seq, sign):
    acc = None
    for j in range(-(w // 2), w // 2):
        if j == 0:
            term = pv
        else:
            src = row + sign * j
            term = jnp.where((src >= 0) & (src < seq), pltpu.roll(pv, (-sign * j) % seq, 0), 0.0)
        acc = term if acc is None else acc + term
    return acc


def _pool_fwd(proj, w_grp, scale, d, bl, seq):
    t = proj.shape[0]
    dg = d // 8

    def body(p_ref, w_ref, s_ref, y_ref):
        row = lax.broadcasted_iota(jnp.int32, (seq, dg), 0)
        for gi, w in enumerate(POOL_WINDOWS):
            sl = slice(gi * dg, (gi + 1) * dg)
            pg = p_ref[:, sl].astype(F32)
            mixed = _window_sum(pg, row, w, seq, 1) / _window_count(row, w, seq) - pg
            yp = _dot(mixed.astype(MM), w_ref[gi].astype(MM))
            y_ref[:, sl] = (yp * s_ref[:, sl]).astype(MM)

    return pl.pallas_call(
        body, name="pool_fwd", grid=(bl,),
        in_specs=[pl.BlockSpec((seq, d // 2), lambda b: (b, 6)),
                  pl.BlockSpec(w_grp.shape, lambda b: (0, 0, 0)),
                  pl.BlockSpec((1, d // 2), lambda b: (0, 0))],
        out_specs=pl.BlockSpec((seq, d // 2), lambda b: (b, 0)),
        out_shape=jax.ShapeDtypeStruct((t, d // 2), MM),
        compiler_params=_params(1))(proj, w_grp, scale)


def _pool_bwd(proj, dy, w_grp, scale, d, bl, seq):
    t = proj.shape[0]
    dg = d // 8

    def body(p_ref, dy_ref, w_ref, s_ref, dp_ref, dw_ref, ds_ref):
        @pl.when(pl.program_id(0) == 0)
        def _():
            dw_ref[...] = jnp.zeros_like(dw_ref)
            ds_ref[...] = jnp.zeros_like(ds_ref)

        row = lax.broadcasted_iota(jnp.int32, (seq, dg), 0)
        for gi, w in enumerate(POOL_WINDOWS):
            sl = slice(gi * dg, (gi + 1) * dg)
            pg = p_ref[:, sl].astype(F32)
            cnt = _window_count(row, w, seq)
            mixb = (_window_sum(pg, row, w, seq, 1) / cnt - pg).astype(MM)
            wgb = w_ref[gi].astype(MM)
            yp = _dot(mixb, wgb)
            dyg = dy_ref[:, sl]
            ds_ref[:, sl] += jnp.sum(dyg * yp, axis=0, keepdims=True)
            dyp = (dyg * s_ref[:, sl]).astype(MM)
            dmixed = _dot_nt(dyp, wgb)
            dw_ref[gi] += _dot_tn(mixb, dyp)
            dp_ref[:, sl] = (_window_sum(dmixed / cnt, row, w, seq, -1) - dmixed).astype(MM)

    half = pl.BlockSpec((seq, d // 2), lambda b: (b, 0))
    wspec = pl.BlockSpec(w_grp.shape, lambda b: (0, 0, 0))
    sspec = pl.BlockSpec((1, d // 2), lambda b: (0, 0))
    return pl.pallas_call(
        body, name="pool_bwd", grid=(bl,),
        in_specs=[pl.BlockSpec((seq, d // 2), lambda b: (b, 6)), half, wspec, sspec],
        out_specs=[half, wspec, sspec],
        out_shape=[jax.ShapeDtypeStruct((t, d // 2), MM), jax.ShapeDtypeStruct(w_grp.shape, F32),
                   jax.ShapeDtypeStruct((1, d // 2), F32)],
        compiler_params=_params(1))(proj, dy, w_grp, scale)


def _attn_probs(q, kk, dh):
    s = _dot_nt(q, kk) * (float(dh) ** -0.5)
    e = jnp.exp(s - jnp.max(s, axis=-1, keepdims=True))
    return e / jnp.sum(e, axis=-1, keepdims=True)


def _attn_fwd(proj, kv, d, bl, seq, mlen, *, tq=2048):
    t = proj.shape[0]
    dh = d // 8
    tq = min(tq, seq)
    nq = seq // tq

    def body(q_ref, k_ref, v_ref, o_ref):
        a = _attn_probs(q_ref[...].astype(MM), k_ref[...].astype(MM), dh)
        o_ref[...] = _dot(a.astype(MM), v_ref[...].astype(MM)).astype(MM)

    return pl.pallas_call(
        body, name="attn_fwd", grid=(bl, HEADS, nq),
        in_specs=[pl.BlockSpec((tq, dh), lambda b, h, i: (b * nq + i, 7 * HEADS + h)),
                  pl.BlockSpec((mlen, dh), lambda b, h, i: (b, h)),
                  pl.BlockSpec((mlen, dh), lambda b, h, i: (b, HEADS + h))],
        out_specs=pl.BlockSpec((tq, dh), lambda b, h, i: (b * nq + i, h)),
        out_shape=jax.ShapeDtypeStruct((t, d // 2), MM),
        compiler_params=_params(3))(proj, kv, kv)


def _attn_bwd(proj, kv, do, d, bl, seq, mlen, *, tq=2048):
    t = proj.shape[0]
    dh = d // 8
    tq = min(tq, seq)
    nq = seq // tq

    def body(q_ref, k_ref, v_ref, do_ref, dq_ref, dk_ref, dv_ref):
        @pl.when(pl.program_id(2) == 0)
        def _():
            dk_ref[...] = jnp.zeros_like(dk_ref)
            dv_ref[...] = jnp.zeros_like(dv_ref)

        q, kk, vv = q_ref[...].astype(MM), k_ref[...].astype(MM), v_ref[...].astype(MM)
        dov = do_ref[...].astype(MM)
        a = _attn_probs(q, kk, dh)
        dp = _dot_nt(dov, vv)
        ds = (a * (dp - jnp.sum(dp * a, axis=-1, keepdims=True)) * (float(dh) ** -0.5)).astype(MM)
        dq_ref[...] = _dot(ds, kk).astype(MM)
        dk_ref[...] += _dot_tn(ds, q)
        dv_ref[...] += _dot_tn(a.astype(MM), dov)

    qs = pl.BlockSpec((tq, dh), lambda b, h, i: (b * nq + i, h))
    ms = pl.BlockSpec((mlen, dh), lambda b, h, i: (b, h))
    return pl.pallas_call(
        body, name="attn_bwd", grid=(bl, HEADS, nq),
        in_specs=[pl.BlockSpec((tq, dh), lambda b, h, i: (b * nq + i, 7 * HEADS + h)), ms,
                  pl.BlockSpec((mlen, dh), lambda b, h, i: (b, HEADS + h)), qs],
        out_specs=[qs, ms, ms],
        out_shape=[jax.ShapeDtypeStruct((t, d // 2), MM), jax.ShapeDtypeStruct((bl * mlen, d // 2), F32),
                   jax.ShapeDtypeStruct((bl * mlen, d // 2), F32)],
        compiler_params=_params(3))(proj, kv, kv, do)


def _comm_call(name, body, arrays, out_shapes):
    n = len(arrays)
    hbm = pl.BlockSpec(memory_space=pl.ANY)
    return pl.pallas_call(
        body, name=name, out_shape=out_shapes, in_specs=[hbm] * n, out_specs=[hbm] * n,
        scratch_shapes=[pltpu.SemaphoreType.DMA((7 * n,)), pltpu.SemaphoreType.DMA((7 * n,)),
                        pltpu.SemaphoreType.DMA((n,))],
    )(*arrays)


def _all_gather(name, shards):
    n = len(shards)

    def body(*refs):
        x_refs, out_refs = refs[:n], refs[n:2 * n]
        send_sems, recv_sems, local_sems = refs[2 * n:]
        x, y, c = lax.axis_index("x"), lax.axis_index("y"), lax.axis_index("c")
        me, sibling = (x, y, c), (x, y, 1 - c)
        chips = [(1 - x, y), (x, 1 - y), (1 - x, 1 - y)]

        def copy(o, k, block, to, src=None):
            slot = out_refs[o].at[4 * block[0] + 2 * block[1] + block[2]]
            return pltpu.make_async_remote_copy(
                src_ref=slot if src is None else src, dst_ref=slot, send_sem=send_sems.at[7 * o + k],
                recv_sem=recv_sems.at[7 * o + k], device_id=to, device_id_type=MESH)

        locals_, remotes = [], []
        for o in range(n):
            mine = pltpu.make_async_copy(x_refs[o], out_refs[o].at[4 * x + 2 * y + c], local_sems.at[o])
            mine.start()
            locals_.append(mine)
            first = [copy(o, 0, me, sibling, src=x_refs[o])]
            first += [copy(o, 1 + j, me, (*chip, c), src=x_refs[o]) for j, chip in enumerate(chips)]
            for cp in first:
                cp.start()
            remotes += first
        for o in range(n):
            for j, chip in enumerate(chips):
                copy(o, 1 + j, (*chip, c), me).wait_recv()
                passed = copy(o, 4 + j, (*chip, c), sibling)
                passed.start()
                remotes.append(passed)
        for o in range(n):
            copy(o, 0, sibling, me).wait_recv()
            for j, chip in enumerate(chips):
                copy(o, 4 + j, (*chip, 1 - c), me).wait_recv()
        for cp in remotes:
            cp.wait_send()
        for mine in locals_:
            mine.wait()

    outs = [jax.ShapeDtypeStruct((N_DEV,) + s.shape, s.dtype) for s in shards]
    return _comm_call(name, body, shards, outs)


def _peer_of(k, x, y, c):
    peer = (1 - x if k & 4 else x, 1 - y if k & 2 else y, 1 - c if k & 1 else c)
    return peer, 4 * peer[0] + 2 * peer[1] + peer[2]


def _split_copies(scatter, srcs, lands, send_sems, recv_sems, arriving):
    x, y, c = lax.axis_index("x"), lax.axis_index("y"), lax.axis_index("c")
    me_idx = 4 * x + 2 * y + c
    copies = []
    for o, (src, land) in enumerate(zip(srcs, lands)):
        for k in range(1, N_DEV):
            peer, p_idx = _peer_of(k, x, y, c)
            mine = src.at[p_idx] if scatter else src
            sems = dict(send_sem=send_sems.at[7 * o + k - 1], recv_sem=recv_sems.at[7 * o + k - 1],
                        device_id=peer, device_id_type=MESH)
            slot = land.at[p_idx] if arriving else land.at[me_idx]
            copies.append(pltpu.make_async_remote_copy(src_ref=mine, dst_ref=slot, **sems))
    return copies


_HBM = pl.BlockSpec(memory_space=pltpu.HBM)
_SEM = pl.BlockSpec(memory_space=pltpu.SEMAPHORE)
_EFFECT = pltpu.SideEffectType.DATAFLOW_SIDE_EFFECTING


def _exchange_start(name, scatter, arrays, after=()):
    n = len(arrays)
    lands = [lax.empty(a.shape if scatter else (N_DEV,) + a.shape, a.dtype) for a in arrays]

    def body(*refs):
        srcs, lnds = refs[:n], refs[n:2 * n]
        send_sems, recv_sems = refs[2 * n + len(after)], refs[2 * n + len(after) + 1]
        token = refs[-1]
        for cp in _split_copies(scatter, srcs, lnds, send_sems, recv_sems, False):
            cp.start()
        token[...] = jnp.zeros_like(token)

    hbm_in = [pltpu.with_memory_space_constraint(a, pltpu.HBM) for a in list(arrays) + lands]
    res = pl.pallas_call(
        body, name=name,
        out_shape=(pltpu.SemaphoreType.DMA((7 * n,)), pltpu.SemaphoreType.DMA((7 * n,)),
                   *[pltpu.HBM(a.shape, a.dtype) for a in hbm_in], jax.ShapeDtypeStruct((8, 128), F32)),
        in_specs=[_HBM] * (2 * n) + [pl.BlockSpec(memory_space=pl.ANY)] * len(after),
        out_specs=(_SEM, _SEM, *[_HBM] * (2 * n), pl.BlockSpec(memory_space=pltpu.VMEM)),
        input_output_aliases={i: 2 + i for i in range(2 * n)},
        compiler_params=pltpu.CompilerParams(has_side_effects=_EFFECT),
    )(*hbm_in, *after)
    return res[0], res[1], list(res[2:2 + n]), list(res[2 + n:2 + 2 * n]), res[-1]


def _exchange_wait(name, scatter, started, after):
    send_sems, recv_sems, srcs, lands, _ = started
    n = len(srcs)

    def body(*refs):
        src_refs, lnd_refs = refs[:n], refs[n:2 * n]
        for cp in _split_copies(scatter, src_refs, lnd_refs, refs[2 * n], refs[2 * n + 1], False):
            cp.wait_send()
        for cp in _split_copies(scatter, src_refs, lnd_refs, refs[2 * n], refs[2 * n + 1], True):
            cp.wait_recv()

    res = pl.pallas_call(
        body, name=name, out_shape=tuple(pltpu.HBM(a.shape, a.dtype) for a in srcs + lands),
        in_specs=[_HBM] * (2 * n) + [_SEM, _SEM, pl.BlockSpec(memory_space=pl.ANY)],
        out_specs=tuple([_HBM] * (2 * n)), input_output_aliases={i: i for i in range(2 * n)},
        compiler_params=pltpu.CompilerParams(has_side_effects=_EFFECT),
    )(*srcs, *lands, send_sems, recv_sems, after)
    return list(res[n:]), list(res[:n])


def _adamw(name, parts, w, m, v, prev, layer, *, tr=256):
    _, a, b = w.shape
    tr = _tile(a, tr, 8)
    c1 = 1.0 - ADAM_B1 ** ADAM_STEP
    c2 = 1.0 - ADAM_B2 ** ADAM_STEP

    def body(p_ref, w_ref, m_ref, v_ref, _g, _d, _m, _v, g_out, d_out, m_out, v_out):
        g = p_ref[0].astype(F32)
        for s in range(1, N_DEV):
            g = g + p_ref[s].astype(F32)
        mn = ADAM_B1 * m_ref[...] + (1.0 - ADAM_B1) * g
        vn = ADAM_B2 * v_ref[...] + (1.0 - ADAM_B2) * (g * g)
        g_out[...] = g
        m_out[...] = mn
        v_out[...] = vn
        d_out[...] = -ADAM_LR * ((mn / c1) / (jnp.sqrt(vn / c2) + ADAM_EPS) + ADAM_WD * w_ref[...])

    slab = pl.BlockSpec((None, tr, b), lambda i: (layer, i, 0))
    whole = pl.BlockSpec(memory_space=pl.ANY)
    return pl.pallas_call(
        body, name=name, grid=(a // tr,),
        in_specs=[pl.BlockSpec((N_DEV, tr, b), lambda i: (0, i, 0)), slab, slab, slab] + [whole] * 4,
        out_specs=[slab] * 4, out_shape=[jax.ShapeDtypeStruct(w.shape, F32)] * 4,
        input_output_aliases={4: 0, 5: 1, 6: 2, 7: 3},
        compiler_params=_params(1))(parts, w, m, v, *prev)


_COL = ("w_in", "w_pool_o", "w_mem_o", "w_ff1")
_COL_IN_PLACE = ("w_in", "w_ff1")
_BIG =("w_in", "w_ret_o", "w_pool_o", "w_mem_kv", "w_mem_o", "w_out", "w_ff1", "w_ff2")
_SMALL = ("ret_decay_logit", "w_pool_grp", "pool_scale", "norm1_g", "norm2_g", "mem_norm_g", "final_norm_g")
_WEIGHTS = ("w_in", "ret_decay_logit", "w_ret_o", "w_pool_grp", "pool_scale", "w_pool_o", "w_mem_kv", "w_mem_o",
            "w_out", "w_ff1", "w_ff2", "norm1_g", "norm2_g", "mem_norm_g", "final_norm_g")


def _small_rows(size, d):
    return -(-size // (8 * d)) * 8


def _pack_small(ws, d):
    parts = []
    for n in _SMALL:
        flat = ws[n].reshape(-1)
        rows = _small_rows(flat.shape[0], d)
        parts.append(jnp.pad(flat, (0, rows * d - flat.shape[0])).reshape(rows, d))
    return jnp.concatenate(parts, axis=0)[None]


def _unpack_small(packed, like, d):
    out, off = {}, 0
    for n in _SMALL:
        rows = _small_rows(like[n].size, d)
        out[n] = packed[0, off:off + rows].reshape(-1)[:like[n].size].reshape(like[n].shape)
        off += rows
    return out


def kernel(x, mem, w_in, ret_decay_logit, w_ret_o, w_pool_grp, pool_scale, w_pool_o, w_mem_kv, w_mem_o, w_out, w_ff1, w_ff2, norm1_g, norm2_g, mem_norm_g, final_norm_g, loss_target, m_w_in, m_ret_decay_logit, m_w_ret_o, m_w_pool_grp, m_pool_scale, m_w_pool_o, m_w_mem_kv, m_w_mem_o, m_w_out, m_w_ff1, m_w_ff2, m_norm1_g, m_norm2_g, m_mem_norm_g, m_final_norm_g, v_w_in, v_ret_decay_logit, v_w_ret_o, v_w_pool_grp, v_pool_scale, v_w_pool_o, v_w_mem_kv, v_w_mem_o, v_w_out, v_w_ff1, v_w_ff2, v_norm1_g, v_norm2_g, v_mem_norm_g, v_final_norm_g):
    w = dict(w_in=w_in, ret_decay_logit=ret_decay_logit, w_ret_o=w_ret_o, w_pool_grp=w_pool_grp,
             pool_scale=pool_scale, w_pool_o=w_pool_o, w_mem_kv=w_mem_kv, w_mem_o=w_mem_o, w_out=w_out,
             w_ff1=w_ff1, w_ff2=w_ff2, norm1_g=norm1_g, norm2_g=norm2_g, mem_norm_g=mem_norm_g,
             final_norm_g=final_norm_g)
    mom = dict(w_in=m_w_in, ret_decay_logit=m_ret_decay_logit, w_ret_o=m_w_ret_o, w_pool_grp=m_w_pool_grp,
               pool_scale=m_pool_scale, w_pool_o=m_w_pool_o, w_mem_kv=m_w_mem_kv, w_mem_o=m_w_mem_o,
               w_out=m_w_out, w_ff1=m_w_ff1, w_ff2=m_w_ff2, norm1_g=m_norm1_g, norm2_g=m_norm2_g,
               mem_norm_g=m_mem_norm_g, final_norm_g=m_final_norm_g)
    vel = dict(w_in=v_w_in, ret_decay_logit=v_ret_decay_logit, w_ret_o=v_w_ret_o, w_pool_grp=v_w_pool_grp,
               pool_scale=v_pool_scale, w_pool_o=v_w_pool_o, w_mem_kv=v_w_mem_kv, w_mem_o=v_w_mem_o,
               w_out=v_w_out, w_ff1=v_w_ff1, w_ff2=v_w_ff2, norm1_g=v_norm1_g, norm2_g=v_norm2_g,
               mem_norm_g=v_mem_norm_g, final_norm_g=v_final_norm_g)

    bl, seq, d = x.shape
    mlen = mem.shape[1]
    depth = w_in.shape[0]
    t = bl * seq
    dk = d // 8

    me_idx = 4 * lax.axis_index("x") + 2 * lax.axis_index("y") + lax.axis_index("c")

    def natural(n, g):
        if n in _COL_IN_PLACE:
            return g
        if n in _COL:
            return jnp.transpose(g, (1, 0, 2)).reshape(g.shape[1], -1)
        return g.reshape(-1, g.shape[-1])

    def finish_gather(name, names, started, after):
        got, mine = _exchange_wait(name, False, started, after)
        return {n: natural(n, lax.dynamic_update_slice(g, sh[None], (me_idx, 0, 0)))
                for n, g, sh in zip(names, got, mine)}

    shards = [{n: w[n][l].astype(MM) for n in _BIG} for l in range(depth)]
    rest = _BIG[1:]
    (w_in0,) = _all_gather("gather_w_in", [shards[0][_BIG[0]]])
    full = [{_BIG[0]: w_in0}]
    pending = _exchange_start("gather_start_0", False, [shards[0][n] for n in rest], after=[w_in0])

    inv = ROPE_BASE ** (-jnp.arange(0, dk, 2, dtype=F32) / dk)
    ang = jnp.arange(seq, dtype=F32)[:, None] * inv[None, :]
    cos2 = jnp.concatenate([jnp.cos(ang), jnp.cos(ang)], axis=-1)
    sin2 = jnp.concatenate([-jnp.sin(ang), jnp.sin(ang)], axis=-1)
    log_g = jax.nn.log_sigmoid(ret_decay_logit)
    x2 = x.reshape(t, d)
    mem2 = mem.reshape(bl * mlen, d)
    gmem = mem_norm_g.reshape(1, d)

    def merge(a_r, y_p, o_a, g_r, g_p, g_m, w_r, w_p, w_m):
        f = lambda z: z.astype(F32)
        o_r, o_p, o_m = _dot(a_r, w_r), _dot(y_p, w_p), _dot(o_a, w_m)
        return _sigmoid(f(g_r)) * o_r + _sigmoid(f(g_p)) * o_p + _sigmoid(f(g_m)) * o_m, o_r, o_p, o_m

    def relu2(u):
        r = jnp.maximum(u.astype(MM), 0.0)
        return r * r

    def ident(a):
        return a

    saved = []
    xc = x2
    for l in range(depth):
        s = dict(x_in=xc)
        started_now = ()
        if l > 0:
            full.append(finish_gather(f"gather_wait_{l}", _BIG, pending, xc))
            if l + 1 < depth:
                pending = _exchange_start(f"gather_start_{l + 1}", False, [shards[l + 1][n] for n in _BIG],
                                          after=[full[l]["w_in"]])
                started_now = (pending[4],)
        fw = full[l]
        g1 = norm1_g[l].reshape(1, d)
        g2 = norm2_g[l].reshape(1, d)
        s["proj"], s["h1"] = _pmm("proj", _rms_prologue, [(xc, d, 0)], [g1], fw["w_in"], w_mode="col",
                                  tm=1024, tn=1024, save_a=True, out_dtypes=(MM,),
                                  after=started_now if l > 0 else (pending[4],))
        proj = s["proj"]
        s["qr"], s["kr"], s["vb"] = _ret_pre(proj, cos2, sin2, d, seq)
        s["o_raw"], s["a_ret"], s["sf"], s["sb"] = _ret_core_fwd(s["qr"], s["kr"], s["vb"], proj, log_g[l],
                                                                 d, bl, seq)
        s["y"] = _pool_fwd(proj, w_pool_grp[l], pool_scale[l].reshape(1, -1), d, bl, seq)
        started_now = ()
        if l == 0:
            fw.update(finish_gather("gather_wait_0", rest, pending, s["a_ret"]))
            if depth > 1:
                pending = _exchange_start("gather_start_1", False, [shards[1][n] for n in _BIG],
                                          after=[fw["w_mem_kv"]])
                started_now = (pending[4],)
        s["kv"], s["memn"] = _pmm("mem_kv", _rms_prologue, [(mem2, d, 0)], [gmem], fw["w_mem_kv"],
                                  tm=512, tn=512, save_a=True, after=started_now)
        s["o_att"] = _attn_fwd(proj, s["kv"], d, bl, seq, mlen)
        s["x_mid"], s["merged"], s["o_ret"], s["o_pool"], s["o_mem"] = _pmm(
            "merge_out", merge,
            [(s["a_ret"], d, 0), (s["y"], d // 2, 0), (s["o_att"], d // 2, 0), (proj, d, 4), (proj, d, 5), (proj, d, 6)],
            [fw["w_ret_o"], fw["w_pool_o"], fw["w_mem_o"]], fw["w_out"], tm=512, tn=512, residual=xc, save_a=True,
            extra_outs=[(d, MM)] * 3)
        s["u"], s["h2"] = _pmm("ff1", _rms_prologue, [(s["x_mid"], d, 0)], [g2], fw["w_ff1"], w_mode="col",
                               tm=1024, tn=512, save_a=True, out_dtypes=(MM,))
        (xc,) = _pmm("ff2", relu2, [(s["u"], s["u"].shape[1], 0)], [], fw["w_ff2"],
                     tm=512, tn=512, residual=s["x_mid"])
        saved.append(s)

    dxc, g_final, loss_part = _loss_head(xc, loss_target.reshape(t, d), final_norm_g.reshape(1, d))
    loss = lax.psum(loss_part[0, 0], ("x", "y", "c"))

    small_names = ("w_pool_grp", "pool_scale", "norm1_g", "norm2_g", "ret_decay_logit")
    grads = {n: [None] * depth for n in small_names}
    group_a = ("w_ff1", "w_ff2")
    group_b = tuple(n for n in _BIG if n not in group_a)
    scatters = {}
    dmemn = jnp.zeros((bl * mlen, d), F32)

    def relu2_bwd(acc, u):
        return (acc * (2.0 * jnp.maximum(u.astype(F32), 0.0)),)

    def gates_bwd(acc, g_r, g_p, g_m, o_r, o_p, o_m, w_r, w_p, w_m):
        d_os, d_gs, backs = [], [], []
        for gz, oz, wz in ((g_r, o_r, w_r), (g_p, o_p, w_p), (g_m, o_m, w_m)):
            sg = _sigmoid(gz.astype(F32))
            d_o = (acc * sg).astype(MM)
            d_os.append(d_o)
            d_gs.append(acc * oz.astype(F32) * (sg * (1.0 - sg)))
            backs.append(_dot_nt(d_o, wz))
        return tuple(d_os + d_gs + backs)

    def to_send(n, g):
        a, b = w[n].shape[1:]
        if n in _COL_IN_PLACE:
            return g
        if n in _COL:
            return jnp.transpose(g.reshape(a, N_DEV, b), (1, 0, 2))
        return g.reshape(N_DEV, a, b)

    for l in reversed(range(depth)):
        s = saved[l]
        fw = full[l]
        proj = s["proj"]
        g1 = norm1_g[l].reshape(1, d)
        g2 = norm2_g[l].reshape(1, d)
        dw = {}
        (du,) = _pmm("ff2_bwd", ident, [(dxc, d, 0)], [], fw["w_ff2"], w_mode="nt", tm=1024, tn=1024,
                     epilogue=relu2_bwd, epi_ins=[(s["u"], 0)], out_dtypes=(MM,))
        dw["w_ff2"] = _tnmm("dw_ff2", s["u"], dxc, a_fn=relu2)
        dw["w_ff1"] = _tnmm("dw_ff1", s["h2"], du, col_shards=True)
        scatters[l, "a"] = _exchange_start(f"scatter_start_a{l}", True, [to_send(n, dw[n]) for n in group_a])
        dmid, grads["norm2_g"][l] = _mm_rms_bwd("ff1_norm2_bwd", du, fw["w_ff1"], s["x_mid"], g2, dxc, tm=512)
        d_oret, d_opool, d_omem, dgr, dgp, dgm, da_ret, dy, do_att = _pmm(
            "out_bwd", ident, [(dmid, d, 0)], [], fw["w_out"], w_mode="nt", tm=256, tn=d, epilogue=gates_bwd,
            epi_ins=[(proj, 4 * d), (proj, 5 * d), (proj, 6 * d), (s["o_ret"], 0), (s["o_pool"], 0), (s["o_mem"], 0)],
            epi_full=[fw["w_ret_o"], fw["w_pool_o"], fw["w_mem_o"]], out_dtypes=(MM,) * 6 + (F32,) * 3,
            out_widths=[d] * 7 + [d // 2] * 2, after=(scatters[l, "a"][4],))
        dw["w_out"] = _tnmm("dw_out", s["merged"], dmid)
        dw["w_ret_o"] = _tnmm("dw_ret_o", s["a_ret"], d_oret)
        dw["w_pool_o"] = _tnmm("dw_pool_o", s["y"], d_opool)
        dw["w_mem_o"] = _tnmm("dw_mem_o", s["o_att"], d_omem)
        dg_ret, do_ret = _ret_post_bwd(da_ret, proj, s["o_raw"], d)
        dq, dkk, dvv, dlf, dlb = _ret_core_bwd(s["qr"], s["kr"], s["vb"], do_ret, s["sf"], s["sb"], cos2, sin2,
                                               log_g[l], d, bl, seq)
        dl = jnp.stack([dlf[:, 0, 0].reshape(bl, HEADS).sum(0), dlb[:, 0, 0].reshape(bl, HEADS).sum(0)])
        grads["ret_decay_logit"][l] = dl * jax.nn.sigmoid(-ret_decay_logit[l])
        dp, grads["w_pool_grp"][l], dscale = _pool_bwd(proj, dy, w_pool_grp[l], pool_scale[l].reshape(1, -1),
                                                       d, bl, seq)
        grads["pool_scale"][l] = dscale.reshape(-1)
        dqm, dmk, dmv = _attn_bwd(proj, s["kv"], do_att, d, bl, seq, mlen)
        dkv = jnp.concatenate([dmk, dmv], axis=-1).astype(MM)
        dw["w_mem_kv"] = _tnmm("dw_mem_kv", s["memn"], dkv)
        (dmemn,) = _pmm("mem_kv_bwd", None, [(dkv, d, 0)], [], fw["w_mem_kv"], w_mode="nt", tm=512, tn=512,
                        residual=dmemn)
        dproj = [dq, dkk, dvv, dg_ret, dp, dqm, dgr, dgp, dgm]
        dw["w_in"] = _tnmm("dw_in", s["h1"], dproj, col_shards=True, tm=512, tk=512)
        scatters[l, "b"] = _exchange_start(f"scatter_start_b{l}", True, [to_send(n, dw[n]) for n in group_b])
        dxc, grads["norm1_g"][l] = _mm_rms_bwd("proj_norm1_bwd", dproj, fw["w_in"], s["x_in"], g1, dmid, tm=256,
                                               after=(scatters[l, "b"][4],))

    _, g_memn = _rms_bwd("mem_norm_bwd", dmemn, mem2, gmem, None)
    grad_x = dxc.reshape(bl, seq, d)

    small_g = dict(ret_decay_logit=jnp.stack(grads["ret_decay_logit"]), w_pool_grp=jnp.stack(grads["w_pool_grp"]),
                   pool_scale=jnp.stack(grads["pool_scale"]),
                   norm1_g=jnp.concatenate(grads["norm1_g"], axis=0), norm2_g=jnp.concatenate(grads["norm2_g"], axis=0),
                   mem_norm_g=g_memn.reshape(-1), final_norm_g=g_final.reshape(-1))
    small_started = _exchange_start("gather_small_start", False, [_pack_small(small_g, d)[0]])

    big = {n: [lax.empty(w[n].shape, F32) for _ in range(4)] for n in _BIG}

    def update(l, grp, names, after):
        recv, sent = _exchange_wait(f"scatter_wait_{grp}{l}", True, scatters[l, grp], after)
        for n, r, snt in zip(names, recv, sent):
            own = lax.dynamic_slice_in_dim(snt, me_idx, 1, axis=0)
            parts = lax.dynamic_update_slice(r, own, (me_idx, 0, 0))
            big[n] = _adamw("adamw_" + n, parts, w[n], mom[n], vel[n], big[n], l)
        return big[names[-1]][0]

    after = dxc
    for l in reversed(range(1, depth)):
        for grp, names in (("a", group_a), ("b", group_b)):
            after = update(l, grp, names, after)
    (small_land,), (small_mine,) = _exchange_wait("gather_small_wait", False, small_started, after)
    small_parts = lax.dynamic_update_slice(small_land, small_mine[None], (me_idx, 0, 0))
    w_small = _pack_small(w, d)
    small = _adamw("adamw_small", small_parts, w_small, _pack_small(mom, d), _pack_small(vel, d),
                   [lax.empty(w_small.shape, F32) for _ in range(4)], 0)
    after = small[0]
    small = [_unpack_small(o, w, d) for o in small]
    for grp, names in (("a", group_a), ("b", group_b)):
        after = update(0, grp, names, after)

    outs = [loss, grad_x]
    for k in range(4):
        outs += [big[n][k] if n in _BIG else small[k][n] for n in _WEIGHTS]
    return tuple(outs)
```

```python
import jax
import jax.numpy as jnp
from jax import lax
from jax.experimental import pallas as pl
from jax.experimental.pallas import tpu as pltpu

F32 = jnp.float32
MM = jnp.bfloat16
N_DEV = 8
HEADS = 4
POOL_WINDOWS = (2, 4, 8, 16)
EPS = 1e-6
ROPE_BASE = 10000.0
ADAM_LR, ADAM_B1, ADAM_B2, ADAM_EPS, ADAM_WD, ADAM_STEP = 0.001, 0.9, 0.999, 1e-08, 0.01, 10
V7X_VMEM_LIMIT = 56 * 1024 * 1024
MESH = pl.DeviceIdType.MESH


def _params(n_axes):
    return pltpu.CompilerParams(dimension_semantics=("arbitrary",) * n_axes,
                                vmem_limit_bytes=V7X_VMEM_LIMIT)


def _tile(n, pref, align=128):
    cands = [c for c in range(align, min(pref, n) + 1, align) if n % c == 0]
    return max(cands) if cands else n


def _sigmoid(z):
    return 0.5 * jnp.tanh(0.5 * z) + 0.5


def _dot(a, b):
    return jnp.dot(a, b, preferred_element_type=F32)


def _dot_nt(a, b):
    return lax.dot_general(a, b, (((1,), (1,)), ((), ())), preferred_element_type=F32)


def _dot_tn(a, b):
    return lax.dot_general(a, b, (((0,), (0,)), ((), ())), preferred_element_type=F32)


def _pmm(name, prologue, row_ins, vec_ins, w, *, tm, tn, w_mode="nn", residual=None, save_a=False,
         epilogue=None, epi_ins=(), out_dtypes=(F32,), after=(), extra_outs=(), epi_full=(), out_widths=None):
    m = row_ins[0][0].shape[0]
    wb = None
    if w_mode == "nn":
        k, n = w.shape
        tn = _tile(n, tn)
        w_spec = pl.BlockSpec((k, tn), lambda i, j: (0, j))
    elif w_mode == "nt":
        n, k = w.shape
        tn = _tile(n, tn)
        w_spec = pl.BlockSpec((tn, k), lambda i, j: (j, 0))
    elif w_mode == "col":
        _, k, wb = w.shape
        n = N_DEV * wb
        tn = _tile(wb, tn)
        w_spec = pl.BlockSpec((None, k, tn), lambda i, j, q=wb // tn: (j // q, 0, j % q))
    else:
        _, n, wb = w.shape
        k = N_DEV * wb
        tn = _tile(n, tn)
        w_spec = pl.BlockSpec((N_DEV, tn, wb), lambda i, j: (0, j, 0))
    tm = _tile(m, tm, 8)
    n_row, n_vec, n_epi, n_out = len(row_ins), len(vec_ins), len(epi_ins), len(out_dtypes)
    has_res = residual is not None
    use_scr = prologue is not None
    out_widths = [n] * n_out if out_widths is None else list(out_widths)
    assert all(wd == n for wd in out_widths) or tn == n

    def body(*refs):
        row_refs = refs[:n_row]
        p = n_row
        vec_refs = refs[p:p + n_vec]
        p += n_vec
        w_ref = refs[p]
        p += 1
        res_ref = refs[p] if has_res else None
        p += int(has_res)
        epi_refs = refs[p:p + n_epi + len(epi_full)]
        p += n_epi + len(epi_full) + len(after)
        out_refs = refs[p:p + n_out]
        p += n_out
        a_out = refs[p] if save_a else None
        p += int(save_a)
        extra_refs = refs[p:p + len(extra_outs)]
        p += len(extra_outs)
        if use_scr:
            a_src = refs[p]

            @pl.when(pl.program_id(1) == 0)
            def _():
                made = prologue(*[r[...] for r in row_refs], *[v[...] for v in vec_refs])
                made = made if isinstance(made, tuple) else (made,)
                a = made[0].astype(MM)
                a_src[...] = a
                if save_a:
                    a_out[...] = a
                for e_ref, e in zip(extra_refs, made[1:]):
                    e_ref[...] = e.astype(e_ref.dtype)
        else:
            a_src = row_refs[0]
        if w_mode == "nt":
            acc = _dot_nt(a_src[...], w_ref[...])
        elif w_mode == "col_t":
            acc = _dot_nt(a_src[:, 0:wb], w_ref[0])
            for dev in range(1, N_DEV):
                acc = acc + _dot_nt(a_src[:, dev * wb:(dev + 1) * wb], w_ref[dev])
        else:
            acc = _dot(a_src[...], w_ref[...])
        if has_res:
            acc = acc + res_ref[...]
        outs = epilogue(acc, *[e[...] for e in epi_refs]) if epilogue is not None else (acc,)
        for o_ref, o in zip(out_refs, outs):
            o_ref[...] = o.astype(o_ref.dtype)

    in_specs = [pl.BlockSpec((tm, wd), lambda i, j, cb=cb: (i, cb)) for (_, wd, cb) in row_ins]
    in_specs += [pl.BlockSpec(v.shape, lambda i, j: (0, 0)) for v in vec_ins]
    in_specs += [w_spec]
    args = [r[0] for r in row_ins] + list(vec_ins) + [w]
    if has_res:
        in_specs.append(pl.BlockSpec((tm, tn), lambda i, j: (i, j)))
        args.append(residual)
    for (arr, off) in epi_ins:
        assert off % tn == 0
        in_specs.append(pl.BlockSpec((tm, tn), lambda i, j, ob=off // tn: (i, ob + j)))
        args.append(arr)
    in_specs += [pl.BlockSpec(v.shape, lambda i, j: (0, 0)) for v in epi_full]
    args += list(epi_full)
    n_after = len(after)
    in_specs += [pl.BlockSpec(memory_space=pl.ANY)] * n_after
    args += list(after)
    out_specs = [pl.BlockSpec((tm, tn if wd == n else wd), lambda i, j: (i, j)) for wd in out_widths]
    out_shape = [jax.ShapeDtypeStruct((m, wd), dt) for wd, dt in zip(out_widths, out_dtypes)]
    if save_a:
        out_specs.append(pl.BlockSpec((tm, k), lambda i, j: (i, 0)))
        out_shape.append(jax.ShapeDtypeStruct((m, k), MM))
    for wd, dt in extra_outs:
        out_specs.append(pl.BlockSpec((tm, wd), lambda i, j: (i, 0)))
        out_shape.append(jax.ShapeDtypeStruct((m, wd), dt))
    scratch = [pltpu.VMEM((tm, k), MM)] if use_scr else []
    return pl.pallas_call(body, name=name, grid=(m // tm, n // tn), in_specs=in_specs,
                          out_specs=out_specs, out_shape=out_shape, scratch_shapes=scratch,
                          compiler_params=_params(2))(*args)


def _tnmm(name, a, b, *, tm=1024, tn=1024, tk=1024, col_shards=False, a_fn=None):
    t, m = a.shape
    pieces = list(b) if isinstance(b, (list, tuple)) else [b]
    widths = [p.shape[1] for p in pieces]
    offs = [sum(widths[:p]) for p in range(len(pieces))]
    n = sum(widths)
    tm, tk = _tile(m, tm), _tile(t, tk, 8)
    per_tile = 1
    if col_shards:
        wb = n // N_DEV
        if len(pieces) > 1:
            tn = n
        while 2 * per_tile * wb <= tn and 2 * per_tile <= N_DEV:
            per_tile *= 2
        tn = per_tile * wb
        out_spec = pl.BlockSpec((per_tile, tm, wb), lambda i, j, kk: (j, i, 0))
        out_shape = jax.ShapeDtypeStruct((N_DEV, m, wb), MM)
    else:
        tn = _tile(n, tn)
        out_spec = pl.BlockSpec((tm, tn), lambda i, j, kk: (i, j))
        out_shape = jax.ShapeDtypeStruct((m, n), MM)
    nk = t // tk

    assert len(pieces) == 1 or tn == n

    def body(a_ref, *rest):
        b_refs, (o_ref, acc) = rest[:len(pieces)], rest[len(pieces):]
        kk = pl.program_id(2)

        @pl.when(kk == 0)
        def _():
            acc[...] = jnp.zeros_like(acc)

        av = (a_ref[...] if a_fn is None else a_fn(a_ref[...])).astype(MM)
        if len(pieces) == 1:
            acc[...] += _dot_tn(av, b_refs[0][...].astype(MM))
        else:
            for b_ref, off, wd in zip(b_refs, offs, widths):
                acc[:, off:off + wd] += _dot_tn(av, b_ref[...].astype(MM))

        @pl.when(kk == nk - 1)
        def _():
            if col_shards:
                for sh in range(per_tile):
                    o_ref[sh] = acc[:, sh * wb:(sh + 1) * wb].astype(o_ref.dtype)
            else:
                o_ref[...] = acc[...].astype(o_ref.dtype)

    return pl.pallas_call(
        body, name=name, grid=(m // tm, n // tn, nk),
        in_specs=[pl.BlockSpec((tk, tm), lambda i, j, kk: (kk, i))]
        + [pl.BlockSpec((tk, tn if len(pieces) == 1 else wd), lambda i, j, kk: (kk, j)) for wd in widths],
        out_specs=out_spec, out_shape=out_shape,
        scratch_shapes=[pltpu.VMEM((tm, tn), F32)],
        compiler_params=_params(3))(a, *pieces)


def _rms_prologue(x, g):
    r = lax.rsqrt(jnp.mean(x * x, axis=-1, keepdims=True) + EPS)
    return x * r * g


def _rms_bwd_rows(dh, x, g):
    d = x.shape[-1]
    r = lax.rsqrt(jnp.mean(x * x, axis=-1, keepdims=True) + EPS)
    xh = x * r
    dxh = dh * g
    dx = r * (dxh - xh * (jnp.sum(dxh * xh, axis=-1, keepdims=True) / d))
    dg = jnp.sum(dh * xh, axis=0, keepdims=True)
    return dx, dg


def _rms_bwd(name, dh, x, g, dres, *, tm=512):
    m, d = x.shape
    tm = min(tm, m)
    has_res = dres is not None

    def body(*refs):
        if has_res:
            dh_ref, x_ref, g_ref, r_ref, dx_ref, dg_ref = refs
        else:
            dh_ref, x_ref, g_ref, dx_ref, dg_ref = refs
        dx, dg = _rms_bwd_rows(dh_ref[...], x_ref[...], g_ref[...])
        if has_res:
            dx = dx + r_ref[...]
        dx_ref[...] = dx

        @pl.when(pl.program_id(0) == 0)
        def _():
            dg_ref[...] = jnp.zeros_like(dg_ref)

        dg_ref[...] += dg

    row = pl.BlockSpec((tm, d), lambda i: (i, 0))
    vec = pl.BlockSpec((1, d), lambda i: (0, 0))
    in_specs = [row, row, vec] + ([row] if has_res else [])
    args = [dh, x, g] + ([dres] if has_res else [])
    return pl.pallas_call(body, name=name, grid=(m // tm,), in_specs=in_specs, out_specs=[row, vec],
                          out_shape=[jax.ShapeDtypeStruct((m, d), F32), jax.ShapeDtypeStruct((1, d), F32)],
                          compiler_params=_params(1))(*args)


def _mm_rms_bwd(name, a, w, x, g, dres, *, tm, after=()):
    pieces = list(a) if isinstance(a, (list, tuple)) else [a]
    widths = [p.shape[1] for p in pieces]
    offs = [sum(widths[:p]) for p in range(len(pieces))]
    m = pieces[0].shape[0]
    _, d, wb = w.shape
    tm = _tile(m, tm, 8)
    n_a = len(pieces)

    def body(*refs):
        a_refs = refs[:n_a]
        w_ref, x_ref, g_ref, r_ref = refs[n_a:n_a + 4]
        dx_ref, dg_ref = refs[n_a + 4 + len(after):]

        def window(lo, hi):
            parts = [a_ref[:, max(lo, off) - off:min(hi, off + wd) - off]
                     for a_ref, off, wd in zip(a_refs, offs, widths) if min(hi, off + wd) > max(lo, off)]
            return parts[0] if len(parts) == 1 else jnp.concatenate(parts, axis=1)

        dh = _dot_nt(window(0, wb), w_ref[0])
        for dev in range(1, N_DEV):
            dh = dh + _dot_nt(window(dev * wb, (dev + 1) * wb), w_ref[dev])
        dx, dg = _rms_bwd_rows(dh, x_ref[...], g_ref[...])
        dx_ref[...] = dx + r_ref[...]

        @pl.when(pl.program_id(0) == 0)
        def _():
            dg_ref[...] = jnp.zeros_like(dg_ref)

        dg_ref[...] += dg

    row = pl.BlockSpec((tm, d), lambda i: (i, 0))
    vec = pl.BlockSpec((1, d), lambda i: (0, 0))
    return pl.pallas_call(
        body, name=name, grid=(m // tm,),
        in_specs=[pl.BlockSpec((tm, wd), lambda i: (i, 0)) for wd in widths]
        + [pl.BlockSpec(w.shape, lambda i: (0, 0, 0)), row, vec, row]
        + [pl.BlockSpec(memory_space=pl.ANY)] * len(after),
        out_specs=[row, vec],
        out_shape=[jax.ShapeDtypeStruct((m, d), F32), jax.ShapeDtypeStruct((1, d), F32)],
        compiler_params=_params(1))(*pieces, w, x, g, dres, *after)


def _loss_head(x, target, g, *, tm=256):
    m, d = x.shape
    tm = min(tm, m)

    def body(x_ref, t_ref, g_ref, dx_ref, dg_ref, loss_ref):
        xv, gv = x_ref[...], g_ref[...]
        y = _rms_prologue(xv, gv)
        err = y - t_ref[...]
        part = 0.5 * jnp.sum(jnp.sum(err * err, axis=-1, keepdims=True) / d)
        dx, dg = _rms_bwd_rows(err / d, xv, gv)
        dx_ref[...] = dx

        @pl.when(pl.program_id(0) == 0)
        def _():
            dg_ref[...] = jnp.zeros_like(dg_ref)
            loss_ref[...] = jnp.zeros_like(loss_ref)

        dg_ref[...] += dg
        loss_ref[...] += jnp.full(loss_ref.shape, part, F32)

    row = pl.BlockSpec((tm, d), lambda i: (i, 0))
    vec = pl.BlockSpec((1, d), lambda i: (0, 0))
    lspec = pl.BlockSpec((1, 128), lambda i: (0, 0))
    return pl.pallas_call(body, name="loss_head", grid=(m // tm,), in_specs=[row, row, vec],
                          out_specs=[row, vec, lspec],
                          out_shape=[jax.ShapeDtypeStruct((m, d), F32), jax.ShapeDtypeStruct((1, d), F32),
                                     jax.ShapeDtypeStruct((1, 128), F32)],
                          compiler_params=_params(1))(x, target, g)


def _rot(xv, cos2, sin2, half):
    return xv * cos2 + pltpu.roll(xv, half, 1) * sin2


def _rot_t(dv, cos2, sin2, half):
    return dv * cos2 + pltpu.roll(dv * sin2, half, 1)


def _ret_pre(proj, cos2, sin2, d, seq, *, ts=512):
    t = proj.shape[0]
    ts = min(ts, seq)
    dk = d // 8
    ns = seq // ts
    scale = float(dk) ** -0.5

    def body(q_ref, k_ref, v_ref, c_ref, s_ref, qo, ko, vo):
        c, s = c_ref[...], s_ref[...]
        for h in range(HEADS):
            sl = slice(h * dk, (h + 1) * dk)
            qo[:, sl] = _rot(q_ref[:, sl].astype(F32), c, s, dk // 2).astype(MM)
            ko[:, sl] = (_rot(k_ref[:, sl].astype(F32), c, s, dk // 2) * scale).astype(MM)
        vo[...] = v_ref[...].astype(MM)

    half = pl.BlockSpec((ts, d // 2), lambda i: (i, 0))
    tab = pl.BlockSpec((ts, dk), lambda i: (i % ns, 0))
    return pl.pallas_call(
        body, name="ret_pre", grid=(t // ts,),
        in_specs=[half, pl.BlockSpec((ts, d // 2), lambda i: (i, 1)), pl.BlockSpec((ts, d), lambda i: (i, 1)),
                  tab, tab],
        out_specs=[half, half, pl.BlockSpec((ts, d), lambda i: (i, 0))],
        out_shape=[jax.ShapeDtypeStruct((t, d // 2), MM), jax.ShapeDtypeStruct((t, d // 2), MM),
                   jax.ShapeDtypeStruct((t, d), MM)],
        compiler_params=_params(1))(proj, proj, proj, cos2, sin2)


def _ret_consts(lg_ref, h, t, dk):
    lf, lb = lg_ref[0, h], lg_ref[1, h]
    ab = (lax.broadcasted_iota(jnp.int32, (t, t), 0) - lax.broadcasted_iota(jnp.int32, (t, t), 1)).astype(F32)
    dmat = jnp.exp(jnp.where(ab >= 0, lf * ab, -lb * ab))
    up = lax.broadcasted_iota(jnp.int32, (t, dk), 0).astype(F32) + 1.0
    down = float(t) - up
    one = jnp.ones((1, 1), F32)
    return dict(ab=ab, dmat=dmat, xi_f=jnp.exp(lf * up), zeta_f=jnp.exp(lf * down), xi_b=jnp.exp(lb * up),
                zeta_b=jnp.exp(lb * down), up=up[:, 0:1], down=down[:, 0:1],
                cf=jnp.exp(one * (lf * t)), cb=jnp.exp(one * (lb * t)))


def _scaled(xv, rows):
    return (xv.astype(F32) * rows).astype(MM)


def _ret_core_fwd(qr, kr, vb, proj, lg, d, bl, seq, *, tc=256):
    t = qr.shape[0]
    dk, dv = d // 8, d // 4
    tc = min(tc, seq)
    nc = seq // tc

    def body(lg_ref, q_ref, k_ref, v_ref, g_ref, o_ref, a_ref, sf_ref, sb_ref):
        c = _ret_consts(lg_ref, pl.program_id(1), tc, dk)

        def rows_of(i):
            return pl.ds(pl.multiple_of(i * tc, tc), tc)

        def fwd_step(i, sf):
            rows = rows_of(i)
            sf_ref[i] = sf
            q, kk, v = q_ref[rows, :], k_ref[rows, :], v_ref[rows, :]
            p = (_dot_nt(q, kk) * c["dmat"]).astype(MM)
            o_ref[rows, :] = _dot(p, v) + _dot(_scaled(q, c["xi_f"]), sf.astype(MM))
            return sf * c["cf"] + _dot_tn(_scaled(kk, c["zeta_f"]), v)

        lax.fori_loop(0, nc, fwd_step, jnp.zeros((dk, dv), F32))

        def bwd_step(ii, sb):
            rows = rows_of(nc - 1 - ii)
            sb_ref[nc - 1 - ii] = sb
            q, kk, v = q_ref[rows, :], k_ref[rows, :], v_ref[rows, :]
            o_ref[rows, :] += _dot(_scaled(q, c["zeta_b"]), sb.astype(MM))
            return sb * c["cb"] + _dot_tn(_scaled(kk, c["xi_b"]), v)

        lax.fori_loop(0, nc, bwd_step, jnp.zeros((dk, dv), F32))

        def post(i, carry):
            rows = rows_of(i)
            o = o_ref[rows, :]
            oc = o - jnp.mean(o, axis=-1, keepdims=True)
            on = oc * lax.rsqrt(jnp.mean(oc * oc, axis=-1, keepdims=True) + EPS)
            g = g_ref[rows, :].astype(F32)
            a_ref[rows, :] = (on * (g * _sigmoid(g))).astype(MM)
            return carry

        lax.fori_loop(0, nc, post, 0)

    qk = pl.BlockSpec((seq, dk), lambda b, h: (b, h))
    vv = pl.BlockSpec((seq, dv), lambda b, h: (b, h))
    states = pl.BlockSpec((None, nc, dk, dv), lambda b, h: (b * HEADS + h, 0, 0, 0))
    return pl.pallas_call(
        body, name="ret_core_fwd", grid=(bl, HEADS),
        in_specs=[pl.BlockSpec(memory_space=pltpu.SMEM), qk, qk, vv,
                  pl.BlockSpec((seq, dv), lambda b, h: (b, 2 * HEADS + h))],
        out_specs=[vv, vv, states, states],
        out_shape=[jax.ShapeDtypeStruct((t, d), F32), jax.ShapeDtypeStruct((t, d), MM),
                   jax.ShapeDtypeStruct((bl * HEADS, nc, dk, dv), F32),
                   jax.ShapeDtypeStruct((bl * HEADS, nc, dk, dv), F32)],
        compiler_params=_params(2))(lg, qr, kr, vb, proj)


def _gate_norm_bwd(o, g, dav):
    mu = jnp.mean(o, axis=-1, keepdims=True)
    oc = o - mu
    r = lax.rsqrt(jnp.mean(oc * oc, axis=-1, keepdims=True) + EPS)
    on = oc * r
    sg = _sigmoid(g)
    don = dav * (g * sg)
    dg = dav * on * (sg * (1.0 + g * (1.0 - sg)))
    do = r * (don - jnp.mean(don, axis=-1, keepdims=True) - on * jnp.mean(don * on, axis=-1, keepdims=True))
    return dg, do


def _ret_core_bwd(qr, kr, vb, da, proj, o_raw, sf_in, sb_in, cos2, sin2, lg, d, bl, seq, *, tc=256):
    t = qr.shape[0]
    dk, dv = d // 8, d // 4
    tc = min(tc, seq)
    nc = seq // tc
    scale = float(dk) ** -0.5

    def body(lg_ref, q_ref, k_ref, v_ref, da_ref, g_ref, o_ref, sf_all, sb_all, c_ref, s_ref, dq_ref, dk_ref,
             dv_ref, dg_ref, dlf_ref, dlb_ref, dq_acc, dk_acc, dv_acc, do_ref):
        c = _ret_consts(lg_ref, pl.program_id(1), tc, dk)
        fwd = c["ab"] >= 0
        zero_state = jnp.zeros((dk, dv), F32)
        zero = jnp.zeros((1, 1), F32)

        def rows_of(i):
            return pl.ds(pl.multiple_of(i * tc, tc), tc)

        def total(xv):
            return jnp.sum(xv, keepdims=True)

        def within(i, carry):
            dlf, dlb = carry
            rows = rows_of(i)
            dg, do = _gate_norm_bwd(o_ref[rows, :], g_ref[rows, :].astype(F32), da_ref[rows, :])
            dg_ref[rows, :] = dg.astype(MM)
            dov = do.astype(MM)
            do_ref[rows, :] = dov
            q, kk, v = q_ref[rows, :], k_ref[rows, :], v_ref[rows, :]
            p = _dot_nt(q, kk) * c["dmat"]
            dav = _dot_nt(dov, v)
            x = p * dav * c["ab"]
            dlf = dlf + total(jnp.where(fwd, x, 0.0))
            dlb = dlb - total(jnp.where(fwd, 0.0, x))
            pb, dpb = p.astype(MM), (dav * c["dmat"]).astype(MM)
            dq_acc[rows, :] = _dot(dpb, kk)
            dk_acc[rows, :] = _dot_tn(dpb, q)
            dv_acc[rows, :] = _dot_tn(pb, dov)
            return dlf, dlb

        dlf, dlb = lax.fori_loop(0, nc, within, (zero, zero))

        def fwd_sweep(i, carry):
            hh, dlf, dlb = carry
            rows = rows_of(i)
            q, kk, v, dov = q_ref[rows, :], k_ref[rows, :], v_ref[rows, :], do_ref[rows, :]
            dof, vf = dov.astype(F32), v.astype(F32)
            sf, sb = sf_all[i], sb_all[i]
            sfb, sbb = sf.astype(MM), sb.astype(MM)
            q_xf, q_zb = _scaled(q, c["xi_f"]), _scaled(q, c["zeta_b"])
            dq_acc[rows, :] += _dot_nt(dov, sfb) * c["xi_f"] + _dot_nt(dov, sbb) * c["zeta_b"]
            dlf = dlf + total(jnp.sum(_dot(q_xf, sfb) * dof, axis=-1, keepdims=True) * c["up"])
            dlb = dlb + total(jnp.sum(_dot(q_zb, sbb) * dof, axis=-1, keepdims=True) * c["down"])
            hb = hh.astype(MM)
            dk_acc[rows, :] += _dot_nt(v, hb) * c["xi_b"]
            dv_bx = _dot(_scaled(kk, c["xi_b"]), hb)
            dv_acc[rows, :] += dv_bx
            dlb = dlb + total(jnp.sum(vf * dv_bx, axis=-1, keepdims=True) * c["up"])
            dlb = dlb + float(tc) * total(hh * (sb * c["cb"]))
            return hh * c["cb"] + _dot_tn(q_zb, dov), dlf, dlb

        _, dlf, dlb = lax.fori_loop(0, nc, fwd_sweep, (zero_state, dlf, dlb))

        def rev_sweep(ii, carry):
            gg, dlf = carry
            i = nc - 1 - ii
            rows = rows_of(i)
            q, kk, v, dov = q_ref[rows, :], k_ref[rows, :], v_ref[rows, :], do_ref[rows, :]
            gb = gg.astype(MM)
            dk_acc[rows, :] += _dot_nt(v, gb) * c["zeta_f"]
            dv_fx = _dot(_scaled(kk, c["zeta_f"]), gb)
            dv_acc[rows, :] += dv_fx
            dlf = dlf + total(jnp.sum(v.astype(F32) * dv_fx, axis=-1, keepdims=True) * c["down"])
            dlf = dlf + float(tc) * total(gg * (sf_all[i] * c["cf"]))
            return gg * c["cf"] + _dot_tn(_scaled(q, c["xi_f"]), dov), dlf

        _, dlf = lax.fori_loop(0, nc, rev_sweep, (zero_state, dlf))

        cs, sn = c_ref[...], s_ref[...]
        dq_ref[...] = _rot_t(dq_acc[...], cs, sn, dk // 2).astype(MM)
        dk_ref[...] = (_rot_t(dk_acc[...], cs, sn, dk // 2) * scale).astype(MM)
        dv_ref[...] = dv_acc[...].astype(MM)
        dlf_ref[...] = jnp.broadcast_to(dlf, dlf_ref.shape)
        dlb_ref[...] = jnp.broadcast_to(dlb, dlb_ref.shape)

    qk = pl.BlockSpec((seq, dk), lambda b, h: (b, h))
    vv = pl.BlockSpec((seq, dv), lambda b, h: (b, h))
    tab = pl.BlockSpec((seq, dk), lambda b, h: (0, 0))
    dl = pl.BlockSpec((None, 8, 128), lambda b, h: (b * HEADS + h, 0, 0))
    states = pl.BlockSpec((None, nc, dk, dv), lambda b, h: (b * HEADS + h, 0, 0, 0))
    return pl.pallas_call(
        body, name="ret_core_bwd", grid=(bl, HEADS),
        in_specs=[pl.BlockSpec(memory_space=pltpu.SMEM), qk, qk, vv, vv,
                  pl.BlockSpec((seq, dv), lambda b, h: (b, 2 * HEADS + h)), vv, states, states, tab, tab],
        out_specs=[qk, qk, vv, vv, dl, dl],
        out_shape=[jax.ShapeDtypeStruct((t, d // 2), MM), jax.ShapeDtypeStruct((t, d // 2), MM),
                   jax.ShapeDtypeStruct((t, d), MM), jax.ShapeDtypeStruct((t, d), MM),
                   jax.ShapeDtypeStruct((bl * HEADS, 8, 128), F32), jax.ShapeDtypeStruct((bl * HEADS, 8, 128), F32)],
        scratch_shapes=[pltpu.VMEM((seq, dk), F32), pltpu.VMEM((seq, dk), F32), pltpu.VMEM((seq, dv), F32),
                        pltpu.VMEM((seq, dv), MM)],
        compiler_params=_params(2))(lg, qr, kr, vb, da, proj, o_raw, sf_in, sb_in, cos2, sin2)


def _window_count(row, w, seq):
    return (jnp.minimum(row + w // 2, seq) - jnp.maximum(row - w // 2, 0)).astype(F32)


def _window_sum(pv, row, w, seq, sign):
    acc = None
    for j in range(-(w // 2), w // 2):
        if j == 0:
            term = pv
        else:
            src = row + sign * j
            term = jnp.where((src >= 0) & (src < seq), pltpu.roll(pv, (-sign * j) % seq, 0), 0.0)
        acc = term if acc is None else acc + term
    return acc


def _pool_fwd(proj, w_grp, scale, d, bl, seq):
    t = proj.shape[0]
    dg = d // 8

    def body(p_ref, w_ref, s_ref, y_ref):
        row = lax.broadcasted_iota(jnp.int32, (seq, dg), 0)
        for gi, w in enumerate(POOL_WINDOWS):
            sl = slice(gi * dg, (gi + 1) * dg)
            pg = p_ref[:, sl].astype(F32)
            mixed = _window_sum(pg, row, w, seq, 1) / _window_count(row, w, seq) - pg
            yp = _dot(mixed.astype(MM), w_ref[gi].astype(MM))
            y_ref[:, sl] = (yp * s_ref[:, sl]).astype(MM)

    return pl.pallas_call(
        body, name="pool_fwd", grid=(bl,),
        in_specs=[pl.BlockSpec((seq, d // 2), lambda b: (b, 6)),
                  pl.BlockSpec(w_grp.shape, lambda b: (0, 0, 0)),
                  pl.BlockSpec((1, d // 2), lambda b: (0, 0))],
        out_specs=pl.BlockSpec((seq, d // 2), lambda b: (b, 0)),
        out_shape=jax.ShapeDtypeStruct((t, d // 2), MM),
        compiler_params=_params(1))(proj, w_grp, scale)


def _pool_bwd(proj, dy, w_grp, scale, d, bl, seq):
    t = proj.shape[0]
    dg = d // 8

    def body(p_ref, dy_ref, w_ref, s_ref, dp_ref, dw_ref, ds_ref):
        @pl.when(pl.program_id(0) == 0)
        def _():
            dw_ref[...] = jnp.zeros_like(dw_ref)
            ds_ref[...] = jnp.zeros_like(ds_ref)

        row = lax.broadcasted_iota(jnp.int32, (seq, dg), 0)
        for gi, w in enumerate(POOL_WINDOWS):
            sl = slice(gi * dg, (gi + 1) * dg)
            pg = p_ref[:, sl].astype(F32)
            cnt = _window_count(row, w, seq)
            mixb = (_window_sum(pg, row, w, seq, 1) / cnt - pg).astype(MM)
            wgb = w_ref[gi].astype(MM)
            yp = _dot(mixb, wgb)
            dyg = dy_ref[:, sl]
            ds_ref[:, sl] += jnp.sum(dyg * yp, axis=0, keepdims=True)
            dyp = (dyg * s_ref[:, sl]).astype(MM)
            dmixed = _dot_nt(dyp, wgb)
            dw_ref[gi] += _dot_tn(mixb, dyp)
            dp_ref[:, sl] = (_window_sum(dmixed / cnt, row, w, seq, -1) - dmixed).astype(MM)

    half = pl.BlockSpec((seq, d // 2), lambda b: (b, 0))
    wspec = pl.BlockSpec(w_grp.shape, lambda b: (0, 0, 0))
    sspec = pl.BlockSpec((1, d // 2), lambda b: (0, 0))
    return pl.pallas_call(
        body, name="pool_bwd", grid=(bl,),
        in_specs=[pl.BlockSpec((seq, d // 2), lambda b: (b, 6)), half, wspec, sspec],
        out_specs=[half, wspec, sspec],
        out_shape=[jax.ShapeDtypeStruct((t, d // 2), MM), jax.ShapeDtypeStruct(w_grp.shape, F32),
                   jax.ShapeDtypeStruct((1, d // 2), F32)],
        compiler_params=_params(1))(proj, dy, w_grp, scale)


def _attn_probs(q, kk, dh):
    s = _dot_nt(q, kk) * (float(dh) ** -0.5)
    e = jnp.exp(s - jnp.max(s, axis=-1, keepdims=True))
    return e / jnp.sum(e, axis=-1, keepdims=True)


def _attn_fwd(proj, kv, d, bl, seq, mlen, *, tq=2048):
    t = proj.shape[0]
    dh = d // 8
    tq = min(tq, seq)
    nq = seq // tq

    def body(q_ref, k_ref, v_ref, o_ref):
        a = _attn_probs(q_ref[...].astype(MM), k_ref[...].astype(MM), dh)
        o_ref[...] = _dot(a.astype(MM), v_ref[...].astype(MM)).astype(MM)

    return pl.pallas_call(
        body, name="attn_fwd", grid=(bl, HEADS, nq),
        in_specs=[pl.BlockSpec((tq, dh), lambda b, h, i: (b * nq + i, 7 * HEADS + h)),
                  pl.BlockSpec((mlen, dh), lambda b, h, i: (b, h)),
                  pl.BlockSpec((mlen, dh), lambda b, h, i: (b, HEADS + h))],
        out_specs=pl.BlockSpec((tq, dh), lambda b, h, i: (b * nq + i, h)),
        out_shape=jax.ShapeDtypeStruct((t, d // 2), MM),
        compiler_params=_params(3))(proj, kv, kv)


def _attn_bwd(proj, kv, do, d, bl, seq, mlen, *, tq=2048):
    t = proj.shape[0]
    dh = d // 8
    tq = min(tq, seq)
    nq = seq // tq

    def body(q_ref, k_ref, v_ref, do_ref, dq_ref, dk_ref, dv_ref):
        @pl.when(pl.program_id(2) == 0)
        def _():
            dk_ref[...] = jnp.zeros_like(dk_ref)
            dv_ref[...] = jnp.zeros_like(dv_ref)

        q, kk, vv = q_ref[...].astype(MM), k_ref[...].astype(MM), v_ref[...].astype(MM)
        dov = do_ref[...].astype(MM)
        a = _attn_probs(q, kk, dh)
        dp = _dot_nt(dov, vv)
        ds = (a * (dp - jnp.sum(dp * a, axis=-1, keepdims=True)) * (float(dh) ** -0.5)).astype(MM)
        dq_ref[...] = _dot(ds, kk).astype(MM)
        dk_ref[...] += _dot_tn(ds, q)
        dv_ref[...] += _dot_tn(a.astype(MM), dov)

    qs = pl.BlockSpec((tq, dh), lambda b, h, i: (b * nq + i, h))
    ms = pl.BlockSpec((mlen, dh), lambda b, h, i: (b, h))
    return pl.pallas_call(
        body, name="attn_bwd", grid=(bl, HEADS, nq),
        in_specs=[pl.BlockSpec((tq, dh), lambda b, h, i: (b * nq + i, 7 * HEADS + h)), ms,
                  pl.BlockSpec((mlen, dh), lambda b, h, i: (b, HEADS + h)), qs],
        out_specs=[qs, ms, ms],
        out_shape=[jax.ShapeDtypeStruct((t, d // 2), MM), jax.ShapeDtypeStruct((bl * mlen, d // 2), F32),
                   jax.ShapeDtypeStruct((bl * mlen, d // 2), F32)],
        compiler_params=_params(3))(proj, kv, kv, do)


def _comm_call(name, body, arrays, out_shapes):
    n = len(arrays)
    hbm = pl.BlockSpec(memory_space=pl.ANY)
    return pl.pallas_call(
        body, name=name, out_shape=out_shapes, in_specs=[hbm] * n, out_specs=[hbm] * n,
        scratch_shapes=[pltpu.SemaphoreType.DMA((7 * n,)), pltpu.SemaphoreType.DMA((7 * n,)),
                        pltpu.SemaphoreType.DMA((n,))],
    )(*arrays)


def _all_gather(name, shards):
    n = len(shards)

    def body(*refs):
        x_refs, out_refs = refs[:n], refs[n:2 * n]
        send_sems, recv_sems, local_sems = refs[2 * n:]
        x, y, c = lax.axis_index("x"), lax.axis_index("y"), lax.axis_index("c")
        me, sibling = (x, y, c), (x, y, 1 - c)
        chips = [(1 - x, y), (x, 1 - y), (1 - x, 1 - y)]

        def copy(o, k, block, to, src=None):
            slot = out_refs[o].at[4 * block[0] + 2 * block[1] + block[2]]
            return pltpu.make_async_remote_copy(
                src_ref=slot if src is None else src, dst_ref=slot, send_sem=send_sems.at[7 * o + k],
                recv_sem=recv_sems.at[7 * o + k], device_id=to, device_id_type=MESH)

        locals_, remotes = [], []
        for o in range(n):
            mine = pltpu.make_async_copy(x_refs[o], out_refs[o].at[4 * x + 2 * y + c], local_sems.at[o])
            mine.start()
            locals_.append(mine)
            first = [copy(o, 0, me, sibling, src=x_refs[o])]
            first += [copy(o, 1 + j, me, (*chip, c), src=x_refs[o]) for j, chip in enumerate(chips)]
            for cp in first:
                cp.start()
            remotes += first
        for o in range(n):
            for j, chip in enumerate(chips):
                copy(o, 1 + j, (*chip, c), me).wait_recv()
                passed = copy(o, 4 + j, (*chip, c), sibling)
                passed.start()
                remotes.append(passed)
        for o in range(n):
            copy(o, 0, sibling, me).wait_recv()
            for j, chip in enumerate(chips):
                copy(o, 4 + j, (*chip, 1 - c), me).wait_recv()
        for cp in remotes:
            cp.wait_send()
        for mine in locals_:
            mine.wait()

    outs = [jax.ShapeDtypeStruct((N_DEV,) + s.shape, s.dtype) for s in shards]
    return _comm_call(name, body, shards, outs)


def _peer_of(k, x, y, c):
    peer = (1 - x if k & 4 else x, 1 - y if k & 2 else y, 1 - c if k & 1 else c)
    return peer, 4 * peer[0] + 2 * peer[1] + peer[2]


def _split_copies(scatter, srcs, lands, send_sems, recv_sems, arriving):
    x, y, c = lax.axis_index("x"), lax.axis_index("y"), lax.axis_index("c")
    me_idx = 4 * x + 2 * y + c
    copies = []
    for o, (src, land) in enumerate(zip(srcs, lands)):
        for k in range(1, N_DEV):
            peer, p_idx = _peer_of(k, x, y, c)
            mine = src.at[p_idx] if scatter else src
            sems = dict(send_sem=send_sems.at[7 * o + k - 1], recv_sem=recv_sems.at[7 * o + k - 1],
                        device_id=peer, device_id_type=MESH)
            slot = land.at[p_idx] if arriving else land.at[me_idx]
            copies.append(pltpu.make_async_remote_copy(src_ref=mine, dst_ref=slot, **sems))
    return copies


_HBM = pl.BlockSpec(memory_space=pltpu.HBM)
_SEM = pl.BlockSpec(memory_space=pltpu.SEMAPHORE)
_EFFECT = pltpu.SideEffectType.DATAFLOW_SIDE_EFFECTING


def _exchange_start(name, scatter, arrays, after=()):
    n = len(arrays)
    lands = [lax.empty(a.shape if scatter else (N_DEV,) + a.shape, a.dtype) for a in arrays]

    def body(*refs):
        srcs, lnds = refs[:n], refs[n:2 * n]
        send_sems, recv_sems = refs[2 * n + len(after)], refs[2 * n + len(after) + 1]
        token = refs[-1]
        for cp in _split_copies(scatter, srcs, lnds, send_sems, recv_sems, False):
            cp.start()
        token[...] = jnp.zeros_like(token)

    hbm_in = [pltpu.with_memory_space_constraint(a, pltpu.HBM) for a in list(arrays) + lands]
    res = pl.pallas_call(
        body, name=name,
        out_shape=(pltpu.SemaphoreType.DMA((7 * n,)), pltpu.SemaphoreType.DMA((7 * n,)),
                   *[pltpu.HBM(a.shape, a.dtype) for a in hbm_in], jax.ShapeDtypeStruct((8, 128), F32)),
        in_specs=[_HBM] * (2 * n) + [pl.BlockSpec(memory_space=pl.ANY)] * len(after),
        out_specs=(_SEM, _SEM, *[_HBM] * (2 * n), pl.BlockSpec(memory_space=pltpu.VMEM)),
        input_output_aliases={i: 2 + i for i in range(2 * n)},
        compiler_params=pltpu.CompilerParams(has_side_effects=_EFFECT),
    )(*hbm_in, *after)
    return res[0], res[1], list(res[2:2 + n]), list(res[2 + n:2 + 2 * n]), res[-1]


def _exchange_wait(name, scatter, started, after):
    send_sems, recv_sems, srcs, lands, _ = started
    n = len(srcs)

    def body(*refs):
        src_refs, lnd_refs = refs[:n], refs[n:2 * n]
        for cp in _split_copies(scatter, src_refs, lnd_refs, refs[2 * n], refs[2 * n + 1], False):
            cp.wait_send()
        for cp in _split_copies(scatter, src_refs, lnd_refs, refs[2 * n], refs[2 * n + 1], True):
            cp.wait_recv()

    res = pl.pallas_call(
        body, name=name, out_shape=tuple(pltpu.HBM(a.shape, a.dtype) for a in srcs + lands),
        in_specs=[_HBM] * (2 * n) + [_SEM, _SEM, pl.BlockSpec(memory_space=pl.ANY)],
        out_specs=tuple([_HBM] * (2 * n)), input_output_aliases={i: i for i in range(2 * n)},
        compiler_params=pltpu.CompilerParams(has_side_effects=_EFFECT),
    )(*srcs, *lands, send_sems, recv_sems, after)
    return list(res[n:]), list(res[:n])


def _adamw(name, parts, w, m, v, prev, layer, *, tr=256):
    _, a, b = w.shape
    tr = _tile(a, tr, 8)
    c1 = 1.0 - ADAM_B1 ** ADAM_STEP
    c2 = 1.0 - ADAM_B2 ** ADAM_STEP

    def body(p_ref, w_ref, m_ref, v_ref, _g, _d, _m, _v, g_out, d_out, m_out, v_out):
        g = p_ref[0].astype(F32)
        for s in range(1, N_DEV):
            g = g + p_ref[s].astype(F32)
        mn = ADAM_B1 * m_ref[...] + (1.0 - ADAM_B1) * g
        vn = ADAM_B2 * v_ref[...] + (1.0 - ADAM_B2) * (g * g)
        g_out[...] = g
        m_out[...] = mn
        v_out[...] = vn
        d_out[...] = -ADAM_LR * ((mn / c1) / (jnp.sqrt(vn / c2) + ADAM_EPS) + ADAM_WD * w_ref[...])

    slab = pl.BlockSpec((None, tr, b), lambda i: (layer, i, 0))
    whole = pl.BlockSpec(memory_space=pl.ANY)
    return pl.pallas_call(
        body, name=name, grid=(a // tr,),
        in_specs=[pl.BlockSpec((N_DEV, tr, b), lambda i: (0, i, 0)), slab, slab, slab] + [whole] * 4,
        out_specs=[slab] * 4, out_shape=[jax.ShapeDtypeStruct(w.shape, F32)] * 4,
        input_output_aliases={4: 0, 5: 1, 6: 2, 7: 3},
        compiler_params=_params(1))(parts, w, m, v, *prev)


_COL = ("w_in", "w_pool_o", "w_mem_o", "w_ff1")
_COL_IN_PLACE = ("w_in", "w_ff1")
_BIG =("w_in", "w_ret_o", "w_pool_o", "w_mem_kv", "w_mem_o", "w_out", "w_ff1", "w_ff2")
_SMALL = ("ret_decay_logit", "w_pool_grp", "pool_scale", "norm1_g", "norm2_g", "mem_norm_g", "final_norm_g")
_WEIGHTS = ("w_in", "ret_decay_logit", "w_ret_o", "w_pool_grp", "pool_scale", "w_pool_o", "w_mem_kv", "w_mem_o",
            "w_out", "w_ff1", "w_ff2", "norm1_g", "norm2_g", "mem_norm_g", "final_norm_g")


def _small_rows(size, d):
    return -(-size // (8 * d)) * 8


def _pack_small(ws, d):
    parts = []
    for n in _SMALL:
        flat = ws[n].reshape(-1)
        rows = _small_rows(flat.shape[0], d)
        parts.append(jnp.pad(flat, (0, rows * d - flat.shape[0])).reshape(rows, d))
    return jnp.concatenate(parts, axis=0)[None]


def _unpack_small(packed, like, d):
    out, off = {}, 0
    for n in _SMALL:
        rows = _small_rows(like[n].size, d)
        out[n] = packed[0, off:off + rows].reshape(-1)[:like[n].size].reshape(like[n].shape)
        off += rows
    return out


def kernel(x, mem, w_in, ret_decay_logit, w_ret_o, w_pool_grp, pool_scale, w_pool_o, w_mem_kv, w_mem_o, w_out, w_ff1, w_ff2, norm1_g, norm2_g, mem_norm_g, final_norm_g, loss_target, m_w_in, m_ret_decay_logit, m_w_ret_o, m_w_pool_grp, m_pool_scale, m_w_pool_o, m_w_mem_kv, m_w_mem_o, m_w_out, m_w_ff1, m_w_ff2, m_norm1_g, m_norm2_g, m_mem_norm_g, m_final_norm_g, v_w_in, v_ret_decay_logit, v_w_ret_o, v_w_pool_grp, v_pool_scale, v_w_pool_o, v_w_mem_kv, v_w_mem_o, v_w_out, v_w_ff1, v_w_ff2, v_norm1_g, v_norm2_g, v_mem_norm_g, v_final_norm_g):
    w = dict(w_in=w_in, ret_decay_logit=ret_decay_logit, w_ret_o=w_ret_o, w_pool_grp=w_pool_grp,
             pool_scale=pool_scale, w_pool_o=w_pool_o, w_mem_kv=w_mem_kv, w_mem_o=w_mem_o, w_out=w_out,
             w_ff1=w_ff1, w_ff2=w_ff2, norm1_g=norm1_g, norm2_g=norm2_g, mem_norm_g=mem_norm_g,
             final_norm_g=final_norm_g)
    mom = dict(w_in=m_w_in, ret_decay_logit=m_ret_decay_logit, w_ret_o=m_w_ret_o, w_pool_grp=m_w_pool_grp,
               pool_scale=m_pool_scale, w_pool_o=m_w_pool_o, w_mem_kv=m_w_mem_kv, w_mem_o=m_w_mem_o,
               w_out=m_w_out, w_ff1=m_w_ff1, w_ff2=m_w_ff2, norm1_g=m_norm1_g, norm2_g=m_norm2_g,
               mem_norm_g=m_mem_norm_g, final_norm_g=m_final_norm_g)
    vel = dict(w_in=v_w_in, ret_decay_logit=v_ret_decay_logit, w_ret_o=v_w_ret_o, w_pool_grp=v_w_pool_grp,
               pool_scale=v_pool_scale, w_pool_o=v_w_pool_o, w_mem_kv=v_w_mem_kv, w_mem_o=v_w_mem_o,
               w_out=v_w_out, w_ff1=v_w_ff1, w_ff2=v_w_ff2, norm1_g=v_norm1_g, norm2_g=v_norm2_g,
               mem_norm_g=v_mem_norm_g, final_norm_g=v_final_norm_g)

    bl, seq, d = x.shape
    mlen = mem.shape[1]
    depth = w_in.shape[0]
    t = bl * seq
    dk = d // 8

    me_idx = 4 * lax.axis_index("x") + 2 * lax.axis_index("y") + lax.axis_index("c")

    def natural(n, g):
        if n in _COL_IN_PLACE:
            return g
        if n in _COL:
            return jnp.transpose(g, (1, 0, 2)).reshape(g.shape[1], -1)
        return g.reshape(-1, g.shape[-1])

    def finish_gather(name, names, started, after):
        got, mine = _exchange_wait(name, False, started, after)
        return {n: natural(n, lax.dynamic_update_slice(g, sh[None], (me_idx, 0, 0)))
                for n, g, sh in zip(names, got, mine)}

    shards = [{n: w[n][l].astype(MM) for n in _BIG} for l in range(depth)]
    rest = _BIG[1:]
    (w_in0,) = _all_gather("gather_w_in", [shards[0][_BIG[0]]])
    full = [{_BIG[0]: w_in0}]
    pending = _exchange_start("gather_start_0", False, [shards[0][n] for n in rest], after=[w_in0])

    inv = ROPE_BASE ** (-jnp.arange(0, dk, 2, dtype=F32) / dk)
    ang = jnp.arange(seq, dtype=F32)[:, None] * inv[None, :]
    cos2 = jnp.concatenate([jnp.cos(ang), jnp.cos(ang)], axis=-1)
    sin2 = jnp.concatenate([-jnp.sin(ang), jnp.sin(ang)], axis=-1)
    log_g = jax.nn.log_sigmoid(ret_decay_logit)
    x2 = x.reshape(t, d)
    mem2 = mem.reshape(bl * mlen, d)
    gmem = mem_norm_g.reshape(1, d)

    def merge(a_r, y_p, o_a, g_r, g_p, g_m, w_r, w_p, w_m):
        f = lambda z: z.astype(F32)
        o_r, o_p, o_m = _dot(a_r, w_r), _dot(y_p, w_p), _dot(o_a, w_m)
        return _sigmoid(f(g_r)) * o_r + _sigmoid(f(g_p)) * o_p + _sigmoid(f(g_m)) * o_m, o_r, o_p, o_m

    def relu2(u):
        r = jnp.maximum(u.astype(MM), 0.0)
        return r * r

    def ident(a):
        return a

    saved = []
    xc = x2
    for l in range(depth):
        s = dict(x_in=xc)
        started_now = ()
        if l > 0:
            full.append(finish_gather(f"gather_wait_{l}", _BIG, pending, xc))
            if l + 1 < depth:
                pending = _exchange_start(f"gather_start_{l + 1}", False, [shards[l + 1][n] for n in _BIG],
                                          after=[full[l]["w_in"]])
                started_now = (pending[4],)
        fw = full[l]
        g1 = norm1_g[l].reshape(1, d)
        g2 = norm2_g[l].reshape(1, d)
        s["proj"], s["h1"] = _pmm("proj", _rms_prologue, [(xc, d, 0)], [g1], fw["w_in"], w_mode="col",
                                  tm=2048, tn=1024, save_a=True, out_dtypes=(MM,),
                                  after=started_now if l > 0 else (pending[4],))
        proj = s["proj"]
        s["qr"], s["kr"], s["vb"] = _ret_pre(proj, cos2, sin2, d, seq)
        s["o_raw"], s["a_ret"], s["sf"], s["sb"] = _ret_core_fwd(s["qr"], s["kr"], s["vb"], proj, log_g[l],
                                                                 d, bl, seq)
        s["y"] = _pool_fwd(proj, w_pool_grp[l], pool_scale[l].reshape(1, -1), d, bl, seq)
        started_now = ()
        if l == 0:
            fw.update(finish_gather("gather_wait_0", rest, pending, s["a_ret"]))
            if depth > 1:
                pending = _exchange_start("gather_start_1", False, [shards[1][n] for n in _BIG],
                                          after=[fw["w_mem_kv"]])
                started_now = (pending[4],)
        s["kv"], s["memn"] = _pmm("mem_kv", _rms_prologue, [(mem2, d, 0)], [gmem], fw["w_mem_kv"],
                                  tm=512, tn=512, save_a=True, after=started_now)
        s["o_att"] = _attn_fwd(proj, s["kv"], d, bl, seq, mlen)
        s["x_mid"], s["merged"], s["o_ret"], s["o_pool"], s["o_mem"] = _pmm(
            "merge_out", merge,
            [(s["a_ret"], d, 0), (s["y"], d // 2, 0), (s["o_att"], d // 2, 0), (proj, d, 4), (proj, d, 5), (proj, d, 6)],
            [fw["w_ret_o"], fw["w_pool_o"], fw["w_mem_o"]], fw["w_out"], tm=512, tn=512, residual=xc, save_a=True,
            extra_outs=[(d, MM)] * 3)
        s["u"], s["h2"] = _pmm("ff1", _rms_prologue, [(s["x_mid"], d, 0)], [g2], fw["w_ff1"], w_mode="col",
                               tm=1024, tn=512, save_a=True, out_dtypes=(MM,))
        (xc,) = _pmm("ff2", relu2, [(s["u"], s["u"].shape[1], 0)], [], fw["w_ff2"],
                     tm=512, tn=512, residual=s["x_mid"])
        saved.append(s)

    dxc, g_final, loss_part = _loss_head(xc, loss_target.reshape(t, d), final_norm_g.reshape(1, d))
    loss = lax.psum(loss_part[0, 0], ("x", "y", "c"))

    small_names = ("w_pool_grp", "pool_scale", "norm1_g", "norm2_g", "ret_decay_logit")
    grads = {n: [None] * depth for n in small_names}
    group_a = ("w_ff1", "w_ff2")
    group_b = tuple(n for n in _BIG if n not in group_a)
    scatters = {}
    dmemn = jnp.zeros((bl * mlen, d), F32)

    def relu2_bwd(acc, u):
        return (acc * (2.0 * jnp.maximum(u.astype(F32), 0.0)),)

    def gates_bwd(acc, g_r, g_p, g_m, o_r, o_p, o_m, w_r, w_p, w_m):
        d_os, d_gs, backs = [], [], []
        for gz, oz, wz in ((g_r, o_r, w_r), (g_p, o_p, w_p), (g_m, o_m, w_m)):
            sg = _sigmoid(gz.astype(F32))
            d_o = (acc * sg).astype(MM)
            d_os.append(d_o)
            d_gs.append(acc * oz.astype(F32) * (sg * (1.0 - sg)))
            backs.append(_dot_nt(d_o, wz))
        return tuple(d_os + d_gs + backs)

    def to_send(n, g):
        a, b = w[n].shape[1:]
        if n in _COL_IN_PLACE:
            return g
        if n in _COL:
            return jnp.transpose(g.reshape(a, N_DEV, b), (1, 0, 2))
        return g.reshape(N_DEV, a, b)

    for l in reversed(range(depth)):
        s = saved[l]
        fw = full[l]
        proj = s["proj"]
        g1 = norm1_g[l].reshape(1, d)
        g2 = norm2_g[l].reshape(1, d)
        dw = {}
        (du,) = _pmm("ff2_bwd", ident, [(dxc, d, 0)], [], fw["w_ff2"], w_mode="nt", tm=1024, tn=1024,
                     epilogue=relu2_bwd, epi_ins=[(s["u"], 0)], out_dtypes=(MM,))
        dw["w_ff2"] = _tnmm("dw_ff2", s["u"], dxc, a_fn=relu2)
        dw["w_ff1"] = _tnmm("dw_ff1", s["h2"], du, col_shards=True)
        scatters[l, "a"] = _exchange_start(f"scatter_start_a{l}", True, [to_send(n, dw[n]) for n in group_a])
        dmid, grads["norm2_g"][l] = _mm_rms_bwd("ff1_norm2_bwd", du, fw["w_ff1"], s["x_mid"], g2, dxc, tm=512)
        d_oret, d_opool, d_omem, dgr, dgp, dgm, da_ret, dy, do_att = _pmm(
            "out_bwd", ident, [(dmid, d, 0)], [], fw["w_out"], w_mode="nt", tm=256, tn=d, epilogue=gates_bwd,
            epi_ins=[(proj, 4 * d), (proj, 5 * d), (proj, 6 * d), (s["o_ret"], 0), (s["o_pool"], 0), (s["o_mem"], 0)],
            epi_full=[fw["w_ret_o"], fw["w_pool_o"], fw["w_mem_o"]], out_dtypes=(MM,) * 6 + (F32,) * 3,
            out_widths=[d] * 7 + [d // 2] * 2, after=(scatters[l, "a"][4],))
        dw["w_out"] = _tnmm("dw_out", s["merged"], dmid)
        dw["w_ret_o"] = _tnmm("dw_ret_o", s["a_ret"], d_oret)
        dw["w_pool_o"] = _tnmm("dw_pool_o", s["y"], d_opool)
        dw["w_mem_o"] = _tnmm("dw_mem_o", s["o_att"], d_omem)
        dq, dkk, dvv, dg_ret, dlf, dlb = _ret_core_bwd(s["qr"], s["kr"], s["vb"], da_ret, proj, s["o_raw"],
                                                       s["sf"], s["sb"], cos2, sin2, log_g[l], d, bl, seq)
        dl = jnp.stack([dlf[:, 0, 0].reshape(bl, HEADS).sum(0), dlb[:, 0, 0].reshape(bl, HEADS).sum(0)])
        grads["ret_decay_logit"][l] = dl * jax.nn.sigmoid(-ret_decay_logit[l])
        dp, grads["w_pool_grp"][l], dscale = _pool_bwd(proj, dy, w_pool_grp[l], pool_scale[l].reshape(1, -1),
                                                       d, bl, seq)
        grads["pool_scale"][l] = dscale.reshape(-1)
        dqm, dmk, dmv = _attn_bwd(proj, s["kv"], do_att, d, bl, seq, mlen)
        dkv = jnp.concatenate([dmk, dmv], axis=-1).astype(MM)
        dw["w_mem_kv"] = _tnmm("dw_mem_kv", s["memn"], dkv)
        (dmemn,) = _pmm("mem_kv_bwd", None, [(dkv, d, 0)], [], fw["w_mem_kv"], w_mode="nt", tm=512, tn=512,
                        residual=dmemn)
        dproj = [dq, dkk, dvv, dg_ret, dp, dqm, dgr, dgp, dgm]
        dw["w_in"] = _tnmm("dw_in", s["h1"], dproj, col_shards=True, tm=512, tk=512)
        scatters[l, "b"] = _exchange_start(f"scatter_start_b{l}", True, [to_send(n, dw[n]) for n in group_b])
        dxc, grads["norm1_g"][l] = _mm_rms_bwd("proj_norm1_bwd", dproj, fw["w_in"], s["x_in"], g1, dmid, tm=256,
                                               after=(scatters[l, "b"][4],))

    _, g_memn = _rms_bwd("mem_norm_bwd", dmemn, mem2, gmem, None)
    grad_x = dxc.reshape(bl, seq, d)

    small_g = dict(ret_decay_logit=jnp.stack(grads["ret_decay_logit"]), w_pool_grp=jnp.stack(grads["w_pool_grp"]),
                   pool_scale=jnp.stack(grads["pool_scale"]),
                   norm1_g=jnp.concatenate(grads["norm1_g"], axis=0), norm2_g=jnp.concatenate(grads["norm2_g"], axis=0),
                   mem_norm_g=g_memn.reshape(-1), final_norm_g=g_final.reshape(-1))
    small_started = _exchange_start("gather_small_start", False, [_pack_small(small_g, d)[0]])

    big = {n: [lax.empty(w[n].shape, F32) for _ in range(4)] for n in _BIG}

    def update(l, grp, names, after):
        recv, sent = _exchange_wait(f"scatter_wait_{grp}{l}", True, scatters[l, grp], after)
        for n, r, snt in zip(names, recv, sent):
            own = lax.dynamic_slice_in_dim(snt, me_idx, 1, axis=0)
            parts = lax.dynamic_update_slice(r, own, (me_idx, 0, 0))
            big[n] = _adamw("adamw_" + n, parts, w[n], mom[n], vel[n], big[n], l)
        return big[names[-1]][0]

    after = dxc
    for l in reversed(range(1, depth)):
        for grp, names in (("a", group_a), ("b", group_b)):
            after = update(l, grp, names, after)
    after = update(0, "a", group_a, after)
    (small_land,), (small_mine,) = _exchange_wait("gather_small_wait", False, small_started, after)
    small_parts = lax.dynamic_update_slice(small_land, small_mine[None], (me_idx, 0, 0))
    w_small = _pack_small(w, d)
    small = _adamw("adamw_small", small_parts, w_small, _pack_small(mom, d), _pack_small(vel, d),
                   [lax.empty(w_small.shape, F32) for _ in range(4)], 0)
    after = small[0]
    small = [_unpack_small(o, w, d) for o in small]
    update(0, "b", group_b, after)

    outs = [loss, grad_x]
    for k in range(4):
        outs += [big[n][k] if n in _BIG else small[k][n] for n in _WEIGHTS]
    return tuple(outs)
```

```python
import jax
import jax.numpy as jnp
from jax import lax
from jax.experimental import pallas as pl
from jax.experimental.pallas import tpu as pltpu

F32 = jnp.float32
MM = jnp.bfloat16
N_DEV = 8
HEADS = 4
POOL_WINDOWS = (2, 4, 8, 16)
EPS = 1e-6
ROPE_BASE = 10000.0
ADAM_LR, ADAM_B1, ADAM_B2, ADAM_EPS, ADAM_WD, ADAM_STEP = 0.001, 0.9, 0.999, 1e-08, 0.01, 10
V7X_VMEM_LIMIT = 56 * 1024 * 1024
MESH = pl.DeviceIdType.MESH


def _params(n_axes):
    return pltpu.CompilerParams(dimension_semantics=("arbitrary",) * n_axes,
                                vmem_limit_bytes=V7X_VMEM_LIMIT)


def _tile(n, pref, align=128):
    cands = [c for c in range(align, min(pref, n) + 1, align) if n % c == 0]
    return max(cands) if cands else n


def _sigmoid(z):
    return 0.5 * jnp.tanh(0.5 * z) + 0.5


def _dot(a, b):
    return jnp.dot(a, b, preferred_element_type=F32)


def _dot_nt(a, b):
    return lax.dot_general(a, b, (((1,), (1,)), ((), ())), preferred_element_type=F32)


def _dot_tn(a, b):
    return lax.dot_general(a, b, (((0,), (0,)), ((), ())), preferred_element_type=F32)


def _pmm(name, prologue, row_ins, vec_ins, w, *, tm, tn, w_mode="nn", residual=None, save_a=False,
         epilogue=None, epi_ins=(), out_dtypes=(F32,), after=(), extra_outs=(), epi_full=(), out_widths=None):
    m = row_ins[0][0].shape[0]
    wb = None
    shards_per_tile = 1
    if w_mode == "nn":
        k, n = w.shape
        tn = _tile(n, tn)
        w_spec = pl.BlockSpec((k, tn), lambda i, j: (0, j))
    elif w_mode == "nt":
        n, k = w.shape
        tn = _tile(n, tn)
        w_spec = pl.BlockSpec((tn, k), lambda i, j: (j, 0))
    elif w_mode == "col":
        _, k, wb = w.shape
        n = N_DEV * wb
        while 2 * shards_per_tile * wb <= tn and 2 * shards_per_tile <= N_DEV:
            shards_per_tile *= 2
        if shards_per_tile > 1:
            tn = shards_per_tile * wb
            w_spec = pl.BlockSpec((shards_per_tile, k, wb), lambda i, j: (j, 0, 0))
        else:
            tn = _tile(wb, tn)
            w_spec = pl.BlockSpec((None, k, tn), lambda i, j, q=wb // tn: (j // q, 0, j % q))
    else:
        _, n, wb = w.shape
        k = N_DEV * wb
        tn = _tile(n, tn)
        w_spec = pl.BlockSpec((N_DEV, tn, wb), lambda i, j: (0, j, 0))
    tm = _tile(m, tm, 8)
    n_row, n_vec, n_epi, n_out = len(row_ins), len(vec_ins), len(epi_ins), len(out_dtypes)
    has_res = residual is not None
    use_scr = prologue is not None
    out_widths = [n] * n_out if out_widths is None else list(out_widths)
    assert all(wd == n for wd in out_widths) or tn == n

    def body(*refs):
        row_refs = refs[:n_row]
        p = n_row
        vec_refs = refs[p:p + n_vec]
        p += n_vec
        w_ref = refs[p]
        p += 1
        res_ref = refs[p] if has_res else None
        p += int(has_res)
        epi_refs = refs[p:p + n_epi + len(epi_full)]
        p += n_epi + len(epi_full) + len(after)
        out_refs = refs[p:p + n_out]
        p += n_out
        a_out = refs[p] if save_a else None
        p += int(save_a)
        extra_refs = refs[p:p + len(extra_outs)]
        p += len(extra_outs)
        if use_scr:
            a_src = refs[p]

            @pl.when(pl.program_id(1) == 0)
            def _():
                made = prologue(*[r[...] for r in row_refs], *[v[...] for v in vec_refs])
                made = made if isinstance(made, tuple) else (made,)
                a = made[0].astype(MM)
                a_src[...] = a
                if save_a:
                    a_out[...] = a
                for e_ref, e in zip(extra_refs, made[1:]):
                    e_ref[...] = e.astype(e_ref.dtype)
        else:
            a_src = row_refs[0]
        if w_mode == "nt":
            acc = _dot_nt(a_src[...], w_ref[...])
        elif w_mode == "col_t":
            acc = _dot_nt(a_src[:, 0:wb], w_ref[0])
            for dev in range(1, N_DEV):
                acc = acc + _dot_nt(a_src[:, dev * wb:(dev + 1) * wb], w_ref[dev])
        elif shards_per_tile > 1:
            acc = jnp.concatenate([_dot(a_src[...], w_ref[sh]) for sh in range(shards_per_tile)], axis=1)
        else:
            acc = _dot(a_src[...], w_ref[...])
        if has_res:
            acc = acc + res_ref[...]
        outs = epilogue(acc, *[e[...] for e in epi_refs]) if epilogue is not None else (acc,)
        for o_ref, o in zip(out_refs, outs):
            o_ref[...] = o.astype(o_ref.dtype)

    in_specs = [pl.BlockSpec((tm, wd), lambda i, j, cb=cb: (i, cb)) for (_, wd, cb) in row_ins]
    in_specs += [pl.BlockSpec(v.shape, lambda i, j: (0, 0)) for v in vec_ins]
    in_specs += [w_spec]
    args = [r[0] for r in row_ins] + list(vec_ins) + [w]
    if has_res:
        in_specs.append(pl.BlockSpec((tm, tn), lambda i, j: (i, j)))
        args.append(residual)
    for (arr, off) in epi_ins:
        assert off % tn == 0
        in_specs.append(pl.BlockSpec((tm, tn), lambda i, j, ob=off // tn: (i, ob + j)))
        args.append(arr)
    in_specs += [pl.BlockSpec(v.shape, lambda i, j: (0, 0)) for v in epi_full]
    args += list(epi_full)
    n_after = len(after)
    in_specs += [pl.BlockSpec(memory_space=pl.ANY)] * n_after
    args += list(after)
    out_specs = [pl.BlockSpec((tm, tn if wd == n else wd), lambda i, j: (i, j)) for wd in out_widths]
    out_shape = [jax.ShapeDtypeStruct((m, wd), dt) for wd, dt in zip(out_widths, out_dtypes)]
    if save_a:
        out_specs.append(pl.BlockSpec((tm, k), lambda i, j: (i, 0)))
        out_shape.append(jax.ShapeDtypeStruct((m, k), MM))
    for wd, dt in extra_outs:
        out_specs.append(pl.BlockSpec((tm, wd), lambda i, j: (i, 0)))
        out_shape.append(jax.ShapeDtypeStruct((m, wd), dt))
    scratch = [pltpu.VMEM((tm, k), MM)] if use_scr else []
    return pl.pallas_call(body, name=name, grid=(m // tm, n // tn), in_specs=in_specs,
                          out_specs=out_specs, out_shape=out_shape, scratch_shapes=scratch,
                          compiler_params=_params(2))(*args)


def _tnmm(name, a, b, *, tm=1024, tn=1024, tk=1024, col_shards=False, a_fn=None):
    t, m = a.shape
    pieces = list(b) if isinstance(b, (list, tuple)) else [b]
    widths = [p.shape[1] for p in pieces]
    offs = [sum(widths[:p]) for p in range(len(pieces))]
    n = sum(widths)
    tm, tk = _tile(m, tm), _tile(t, tk, 8)
    per_tile = 1
    if col_shards:
        wb = n // N_DEV
        if len(pieces) > 1:
            tn = n
        while 2 * per_tile * wb <= tn and 2 * per_tile <= N_DEV:
            per_tile *= 2
        tn = per_tile * wb
        out_spec = pl.BlockSpec((per_tile, tm, wb), lambda i, j, kk: (j, i, 0))
        out_shape = jax.ShapeDtypeStruct((N_DEV, m, wb), MM)
    else:
        tn = _tile(n, tn)
        out_spec = pl.BlockSpec((tm, tn), lambda i, j, kk: (i, j))
        out_shape = jax.ShapeDtypeStruct((m, n), MM)
    nk = t // tk

    assert len(pieces) == 1 or tn == n

    def body(a_ref, *rest):
        b_refs, (o_ref, acc) = rest[:len(pieces)], rest[len(pieces):]
        kk = pl.program_id(2)

        @pl.when(kk == 0)
        def _():
            acc[...] = jnp.zeros_like(acc)

        av = (a_ref[...] if a_fn is None else a_fn(a_ref[...])).astype(MM)
        if len(pieces) == 1:
            acc[...] += _dot_tn(av, b_refs[0][...].astype(MM))
        else:
            for b_ref, off, wd in zip(b_refs, offs, widths):
                acc[:, off:off + wd] += _dot_tn(av, b_ref[...].astype(MM))

        @pl.when(kk == nk - 1)
        def _():
            if col_shards:
                for sh in range(per_tile):
                    o_ref[sh] = acc[:, sh * wb:(sh + 1) * wb].astype(o_ref.dtype)
            else:
                o_ref[...] = acc[...].astype(o_ref.dtype)

    return pl.pallas_call(
        body, name=name, grid=(m // tm, n // tn, nk),
        in_specs=[pl.BlockSpec((tk, tm), lambda i, j, kk: (kk, i))]
        + [pl.BlockSpec((tk, tn if len(pieces) == 1 else wd), lambda i, j, kk: (kk, j)) for wd in widths],
        out_specs=out_spec, out_shape=out_shape,
        scratch_shapes=[pltpu.VMEM((tm, tn), F32)],
        compiler_params=_params(3))(a, *pieces)


def _rms_prologue(x, g):
    r = lax.rsqrt(jnp.mean(x * x, axis=-1, keepdims=True) + EPS)
    return x * r * g


def _rms_bwd_rows(dh, x, g):
    d = x.shape[-1]
    r = lax.rsqrt(jnp.mean(x * x, axis=-1, keepdims=True) + EPS)
    xh = x * r
    dxh = dh * g
    dx = r * (dxh - xh * (jnp.sum(dxh * xh, axis=-1, keepdims=True) / d))
    dg = jnp.sum(dh * xh, axis=0, keepdims=True)
    return dx, dg


def _rms_bwd(name, dh, x, g, dres, *, tm=512):
    m, d = x.shape
    tm = min(tm, m)
    has_res = dres is not None

    def body(*refs):
        if has_res:
            dh_ref, x_ref, g_ref, r_ref, dx_ref, dg_ref = refs
        else:
            dh_ref, x_ref, g_ref, dx_ref, dg_ref = refs
        dx, dg = _rms_bwd_rows(dh_ref[...], x_ref[...], g_ref[...])
        if has_res:
            dx = dx + r_ref[...]
        dx_ref[...] = dx

        @pl.when(pl.program_id(0) == 0)
        def _():
            dg_ref[...] = jnp.zeros_like(dg_ref)

        dg_ref[...] += dg

    row = pl.BlockSpec((tm, d), lambda i: (i, 0))
    vec = pl.BlockSpec((1, d), lambda i: (0, 0))
    in_specs = [row, row, vec] + ([row] if has_res else [])
    args = [dh, x, g] + ([dres] if has_res else [])
    return pl.pallas_call(body, name=name, grid=(m // tm,), in_specs=in_specs, out_specs=[row, vec],
                          out_shape=[jax.ShapeDtypeStruct((m, d), F32), jax.ShapeDtypeStruct((1, d), F32)],
                          compiler_params=_params(1))(*args)


def _mm_rms_bwd(name, a, w, x, g, dres, *, tm, after=()):
    pieces = list(a) if isinstance(a, (list, tuple)) else [a]
    widths = [p.shape[1] for p in pieces]
    offs = [sum(widths[:p]) for p in range(len(pieces))]
    m = pieces[0].shape[0]
    _, d, wb = w.shape
    tm = _tile(m, tm, 8)
    n_a = len(pieces)

    def body(*refs):
        a_refs = refs[:n_a]
        w_ref, x_ref, g_ref, r_ref = refs[n_a:n_a + 4]
        dx_ref, dg_ref = refs[n_a + 4 + len(after):]

        def window(lo, hi):
            parts = [a_ref[:, max(lo, off) - off:min(hi, off + wd) - off]
                     for a_ref, off, wd in zip(a_refs, offs, widths) if min(hi, off + wd) > max(lo, off)]
            return parts[0] if len(parts) == 1 else jnp.concatenate(parts, axis=1)

        dh = _dot_nt(window(0, wb), w_ref[0])
        for dev in range(1, N_DEV):
            dh = dh + _dot_nt(window(dev * wb, (dev + 1) * wb), w_ref[dev])
        dx, dg = _rms_bwd_rows(dh, x_ref[...], g_ref[...])
        dx_ref[...] = dx + r_ref[...]

        @pl.when(pl.program_id(0) == 0)
        def _():
            dg_ref[...] = jnp.zeros_like(dg_ref)

        dg_ref[...] += dg

    row = pl.BlockSpec((tm, d), lambda i: (i, 0))
    vec = pl.BlockSpec((1, d), lambda i: (0, 0))
    return pl.pallas_call(
        body, name=name, grid=(m // tm,),
        in_specs=[pl.BlockSpec((tm, wd), lambda i: (i, 0)) for wd in widths]
        + [pl.BlockSpec(w.shape, lambda i: (0, 0, 0)), row, vec, row]
        + [pl.BlockSpec(memory_space=pl.ANY)] * len(after),
        out_specs=[row, vec],
        out_shape=[jax.ShapeDtypeStruct((m, d), F32), jax.ShapeDtypeStruct((1, d), F32)],
        compiler_params=_params(1))(*pieces, w, x, g, dres, *after)


def _loss_head(x, target, g, *, tm=256):
    m, d = x.shape
    tm = min(tm, m)

    def body(x_ref, t_ref, g_ref, dx_ref, dg_ref, loss_ref):
        xv, gv = x_ref[...], g_ref[...]
        y = _rms_prologue(xv, gv)
        err = y - t_ref[...]
        part = 0.5 * jnp.sum(jnp.sum(err * err, axis=-1, keepdims=True) / d)
        dx, dg = _rms_bwd_rows(err / d, xv, gv)
        dx_ref[...] = dx

        @pl.when(pl.program_id(0) == 0)
        def _():
            dg_ref[...] = jnp.zeros_like(dg_ref)
            loss_ref[...] = jnp.zeros_like(loss_ref)

        dg_ref[...] += dg
        loss_ref[...] += jnp.full(loss_ref.shape, part, F32)

    row = pl.BlockSpec((tm, d), lambda i: (i, 0))
    vec = pl.BlockSpec((1, d), lambda i: (0, 0))
    lspec = pl.BlockSpec((1, 128), lambda i: (0, 0))
    return pl.pallas_call(body, name="loss_head", grid=(m // tm,), in_specs=[row, row, vec],
                          out_specs=[row, vec, lspec],
                          out_shape=[jax.ShapeDtypeStruct((m, d), F32), jax.ShapeDtypeStruct((1, d), F32),
                                     jax.ShapeDtypeStruct((1, 128), F32)],
                          compiler_params=_params(1))(x, target, g)


def _rot(xv, cos2, sin2, half):
    return xv * cos2 + pltpu.roll(xv, half, 1) * sin2


def _rot_t(dv, cos2, sin2, half):
    return dv * cos2 + pltpu.roll(dv * sin2, half, 1)


def _ret_pre(proj, cos2, sin2, d, seq, *, ts=512):
    t = proj.shape[0]
    ts = min(ts, seq)
    dk = d // 8
    ns = seq // ts
    scale = float(dk) ** -0.5

    def body(q_ref, k_ref, v_ref, c_ref, s_ref, qo, ko, vo):
        c, s = c_ref[...], s_ref[...]
        for h in range(HEADS):
            sl = slice(h * dk, (h + 1) * dk)
            qo[:, sl] = _rot(q_ref[:, sl].astype(F32), c, s, dk // 2).astype(MM)
            ko[:, sl] = (_rot(k_ref[:, sl].astype(F32), c, s, dk // 2) * scale).astype(MM)
        vo[...] = v_ref[...].astype(MM)

    half = pl.BlockSpec((ts, d // 2), lambda i: (i, 0))
    tab = pl.BlockSpec((ts, dk), lambda i: (i % ns, 0))
    return pl.pallas_call(
        body, name="ret_pre", grid=(t // ts,),
        in_specs=[half, pl.BlockSpec((ts, d // 2), lambda i: (i, 1)), pl.BlockSpec((ts, d), lambda i: (i, 1)),
                  tab, tab],
        out_specs=[half, half, pl.BlockSpec((ts, d), lambda i: (i, 0))],
        out_shape=[jax.ShapeDtypeStruct((t, d // 2), MM), jax.ShapeDtypeStruct((t, d // 2), MM),
                   jax.ShapeDtypeStruct((t, d), MM)],
        compiler_params=_params(1))(proj, proj, proj, cos2, sin2)


def _ret_consts(lg_ref, h, t, dk):
    lf, lb = lg_ref[0, h], lg_ref[1, h]
    ab = (lax.broadcasted_iota(jnp.int32, (t, t), 0) - lax.broadcasted_iota(jnp.int32, (t, t), 1)).astype(F32)
    dmat = jnp.exp(jnp.where(ab >= 0, lf * ab, -lb * ab))
    up = lax.broadcasted_iota(jnp.int32, (t, dk), 0).astype(F32) + 1.0
    down = float(t) - up
    one = jnp.ones((1, 1), F32)
    return dict(ab=ab, dmat=dmat, xi_f=jnp.exp(lf * up), zeta_f=jnp.exp(lf * down), xi_b=jnp.exp(lb * up),
                zeta_b=jnp.exp(lb * down), up=up[:, 0:1], down=down[:, 0:1],
                cf=jnp.exp(one * (lf * t)), cb=jnp.exp(one * (lb * t)))


def _scaled(xv, rows):
    return (xv.astype(F32) * rows).astype(MM)


def _ret_core_fwd(qr, kr, vb, proj, lg, d, bl, seq, *, tc=256):
    t = qr.shape[0]
    dk, dv = d // 8, d // 4
    tc = min(tc, seq)
    nc = seq // tc

    def body(lg_ref, q_ref, k_ref, v_ref, g_ref, o_ref, a_ref, sf_ref, sb_ref):
        c = _ret_consts(lg_ref, pl.program_id(1), tc, dk)

        def rows_of(i):
            return pl.ds(pl.multiple_of(i * tc, tc), tc)

        def fwd_step(i, sf):
            rows = rows_of(i)
            sf_ref[i] = sf
            q, kk, v = q_ref[rows, :], k_ref[rows, :], v_ref[rows, :]
            p = (_dot_nt(q, kk) * c["dmat"]).astype(MM)
            o_ref[rows, :] = _dot(p, v) + _dot(_scaled(q, c["xi_f"]), sf.astype(MM))
            return sf * c["cf"] + _dot_tn(_scaled(kk, c["zeta_f"]), v)

        lax.fori_loop(0, nc, fwd_step, jnp.zeros((dk, dv), F32))

        def bwd_step(ii, sb):
            rows = rows_of(nc - 1 - ii)
            sb_ref[nc - 1 - ii] = sb
            q, kk, v = q_ref[rows, :], k_ref[rows, :], v_ref[rows, :]
            o_ref[rows, :] += _dot(_scaled(q, c["zeta_b"]), sb.astype(MM))
            return sb * c["cb"] + _dot_tn(_scaled(kk, c["xi_b"]), v)

        lax.fori_loop(0, nc, bwd_step, jnp.zeros((dk, dv), F32))

        def post(i, carry):
            rows = rows_of(i)
            o = o_ref[rows, :]
            oc = o - jnp.mean(o, axis=-1, keepdims=True)
            on = oc * lax.rsqrt(jnp.mean(oc * oc, axis=-1, keepdims=True) + EPS)
            g = g_ref[rows, :].astype(F32)
            a_ref[rows, :] = (on * (g * _sigmoid(g))).astype(MM)
            return carry

        lax.fori_loop(0, nc, post, 0)

    qk = pl.BlockSpec((seq, dk), lambda b, h: (b, h))
    vv = pl.BlockSpec((seq, dv), lambda b, h: (b, h))
    states = pl.BlockSpec((None, nc, dk, dv), lambda b, h: (b * HEADS + h, 0, 0, 0))
    return pl.pallas_call(
        body, name="ret_core_fwd", grid=(bl, HEADS),
        in_specs=[pl.BlockSpec(memory_space=pltpu.SMEM), qk, qk, vv,
                  pl.BlockSpec((seq, dv), lambda b, h: (b, 2 * HEADS + h))],
        out_specs=[vv, vv, states, states],
        out_shape=[jax.ShapeDtypeStruct((t, d), F32), jax.ShapeDtypeStruct((t, d), MM),
                   jax.ShapeDtypeStruct((bl * HEADS, nc, dk, dv), F32),
                   jax.ShapeDtypeStruct((bl * HEADS, nc, dk, dv), F32)],
        compiler_params=_params(2))(lg, qr, kr, vb, proj)


def _ret_post_bwd(da, proj, o_raw, d, *, ts=2048):
    t = da.shape[0]
    dv = d // 4
    ts = min(ts, t)

    def body(da_ref, g_ref, o_ref, dg_ref, do_ref):
        o, g, dav = o_ref[...], g_ref[...].astype(F32), da_ref[...]
        mu = jnp.mean(o, axis=-1, keepdims=True)
        oc = o - mu
        r = lax.rsqrt(jnp.mean(oc * oc, axis=-1, keepdims=True) + EPS)
        on = oc * r
        sg = _sigmoid(g)
        don = dav * (g * sg)
        dg_ref[...] = (dav * on * (sg * (1.0 + g * (1.0 - sg)))).astype(MM)
        do = r * (don - jnp.mean(don, axis=-1, keepdims=True) - on * jnp.mean(don * on, axis=-1, keepdims=True))
        do_ref[...] = do.astype(MM)

    blk = pl.BlockSpec((ts, dv), lambda i, h: (i, h))
    return pl.pallas_call(
        body, name="ret_post_bwd", grid=(t // ts, HEADS),
        in_specs=[blk, pl.BlockSpec((ts, dv), lambda i, h: (i, 2 * HEADS + h)), blk],
        out_specs=[blk, blk],
        out_shape=[jax.ShapeDtypeStruct((t, d), MM), jax.ShapeDtypeStruct((t, d), MM)],
        compiler_params=_params(2))(da, proj, o_raw)


def _ret_core_bwd(qr, kr, vb, do, sf_in, sb_in, cos2, sin2, lg, d, bl, seq, *, tc=256):
    t = qr.shape[0]
    dk, dv = d // 8, d // 4
    tc = min(tc, seq)
    nc = seq // tc
    scale = float(dk) ** -0.5

    def body(lg_ref, q_ref, k_ref, v_ref, do_ref, sf_all, sb_all, c_ref, s_ref, dq_ref, dk_ref, dv_ref,
             dlf_ref, dlb_ref, dq_acc, dk_acc, dv_acc):
        c = _ret_consts(lg_ref, pl.program_id(1), tc, dk)
        fwd = c["ab"] >= 0
        zero_state = jnp.zeros((dk, dv), F32)
        zero = jnp.zeros((1, 1), F32)

        def rows_of(i):
            return pl.ds(pl.multiple_of(i * tc, tc), tc)

        def total(xv):
            return jnp.sum(xv, keepdims=True)

        def fwd_sweep(i, carry):
            hh, dlf, dlb = carry
            rows = rows_of(i)
            q, kk, v, dov = q_ref[rows, :], k_ref[rows, :], v_ref[rows, :], do_ref[rows, :]
            dof, vf = dov.astype(F32), v.astype(F32)
            p = _dot_nt(q, kk) * c["dmat"]
            da = _dot_nt(dov, v)
            x = p * da * c["ab"]
            dlf = dlf + total(jnp.where(fwd, x, 0.0))
            dlb = dlb - total(jnp.where(fwd, 0.0, x))
            pb, dpb = p.astype(MM), (da * c["dmat"]).astype(MM)
            dq = _dot(dpb, kk)
            dkc = _dot_tn(dpb, q)
            dvc = _dot_tn(pb, dov)
            sf, sb = sf_all[i], sb_all[i]
            sfb, sbb = sf.astype(MM), sb.astype(MM)
            q_xf, q_zb = _scaled(q, c["xi_f"]), _scaled(q, c["zeta_b"])
            dq = dq + _dot_nt(dov, sfb) * c["xi_f"] + _dot_nt(dov, sbb) * c["zeta_b"]
            dlf = dlf + total(jnp.sum(_dot(q_xf, sfb) * dof, axis=-1, keepdims=True) * c["up"])
            dlb = dlb + total(jnp.sum(_dot(q_zb, sbb) * dof, axis=-1, keepdims=True) * c["down"])
            hb = hh.astype(MM)
            dkc = dkc + _dot_nt(v, hb) * c["xi_b"]
            dv_bx = _dot(_scaled(kk, c["xi_b"]), hb)
            dlb = dlb + total(jnp.sum(vf * dv_bx, axis=-1, keepdims=True) * c["up"])
            dlb = dlb + float(tc) * total(hh * (sb * c["cb"]))
            dq_acc[rows, :] = dq
            dk_acc[rows, :] = dkc
            dv_acc[rows, :] = dvc + dv_bx
            return hh * c["cb"] + _dot_tn(q_zb, dov), dlf, dlb

        _, dlf, dlb = lax.fori_loop(0, nc, fwd_sweep, (zero_state, zero, zero))

        def rev_sweep(ii, carry):
            gg, dlf = carry
            i = nc - 1 - ii
            rows = rows_of(i)
            q, kk, v, dov = q_ref[rows, :], k_ref[rows, :], v_ref[rows, :], do_ref[rows, :]
            gb = gg.astype(MM)
            dk_acc[rows, :] += _dot_nt(v, gb) * c["zeta_f"]
            dv_fx = _dot(_scaled(kk, c["zeta_f"]), gb)
            dv_acc[rows, :] += dv_fx
            dlf = dlf + total(jnp.sum(v.astype(F32) * dv_fx, axis=-1, keepdims=True) * c["down"])
            dlf = dlf + float(tc) * total(gg * (sf_all[i] * c["cf"]))
            return gg * c["cf"] + _dot_tn(_scaled(q, c["xi_f"]), dov), dlf

        _, dlf = lax.fori_loop(0, nc, rev_sweep, (zero_state, dlf))

        cs, sn = c_ref[...], s_ref[...]
        dq_ref[...] = _rot_t(dq_acc[...], cs, sn, dk // 2).astype(MM)
        dk_ref[...] = (_rot_t(dk_acc[...], cs, sn, dk // 2) * scale).astype(MM)
        dv_ref[...] = dv_acc[...].astype(MM)
        dlf_ref[...] = jnp.broadcast_to(dlf, dlf_ref.shape)
        dlb_ref[...] = jnp.broadcast_to(dlb, dlb_ref.shape)

    qk = pl.BlockSpec((seq, dk), lambda b, h: (b, h))
    vv = pl.BlockSpec((seq, dv), lambda b, h: (b, h))
    tab = pl.BlockSpec((seq, dk), lambda b, h: (0, 0))
    dl = pl.BlockSpec((None, 8, 128), lambda b, h: (b * HEADS + h, 0, 0))
    states = pl.BlockSpec((None, nc, dk, dv), lambda b, h: (b * HEADS + h, 0, 0, 0))
    return pl.pallas_call(
        body, name="ret_core_bwd", grid=(bl, HEADS),
        in_specs=[pl.BlockSpec(memory_space=pltpu.SMEM), qk, qk, vv, vv, states, states, tab, tab],
        out_specs=[qk, qk, vv, dl, dl],
        out_shape=[jax.ShapeDtypeStruct((t, d // 2), MM), jax.ShapeDtypeStruct((t, d // 2), MM),
                   jax.ShapeDtypeStruct((t, d), MM),
                   jax.ShapeDtypeStruct((bl * HEADS, 8, 128), F32), jax.ShapeDtypeStruct((bl * HEADS, 8, 128), F32)],
        scratch_shapes=[pltpu.VMEM((seq, dk), F32), pltpu.VMEM((seq, dk), F32), pltpu.VMEM((seq, dv), F32)],
        compiler_params=_params(2))(lg, qr, kr, vb, do, sf_in, sb_in, cos2, sin2)


def _window_count(row, w, seq):
    return (jnp.minimum(row + w // 2, seq) - jnp.maximum(row - w // 2, 0)).astype(F32)


def _window_sum(pv, row, w, seq, sign):
    acc = None
    for j in range(-(w // 2), w // 2):
        if j == 0:
            term = pv
        else:
            src = row + sign * j
            term = jnp.where((src >= 0) & (src < seq), pltpu.roll(pv, (-sign * j) % seq, 0), 0.0)
        acc = term if acc is None else acc + term
    return acc


def _pool_fwd(proj, w_grp, scale, d, bl, seq):
    t = proj.shape[0]
    dg = d // 8

    def body(p_ref, w_ref, s_ref, y_ref):
        row = lax.broadcasted_iota(jnp.int32, (seq, dg), 0)
        for gi, w in enumerate(POOL_WINDOWS):
            sl = slice(gi * dg, (gi + 1) * dg)
            pg = p_ref[:, sl].astype(F32)
            mixed = _window_sum(pg, row, w, seq, 1) / _window_count(row, w, seq) - pg
            yp = _dot(mixed.astype(MM), w_ref[gi].astype(MM))
            y_ref[:, sl] = (yp * s_ref[:, sl]).astype(MM)

    return pl.pallas_call(
        body, name="pool_fwd", grid=(bl,),
        in_specs=[pl.BlockSpec((seq, d // 2), lambda b: (b, 6)),
                  pl.BlockSpec(w_grp.shape, lambda b: (0, 0, 0)),
                  pl.BlockSpec((1, d // 2), lambda b: (0, 0))],
        out_specs=pl.BlockSpec((seq, d // 2), lambda b: (b, 0)),
        out_shape=jax.ShapeDtypeStruct((t, d // 2), MM),
        compiler_params=_params(1))(proj, w_grp, scale)


def _pool_bwd(proj, dy, w_grp, scale, d, bl, seq):
    t = proj.shape[0]
    dg = d // 8

    def body(p_ref, dy_ref, w_ref, s_ref, dp_ref, dw_ref, ds_ref):
        @pl.when(pl.program_id(0) == 0)
        def _():
            dw_ref[...] = jnp.zeros_like(dw_ref)
            ds_ref[...] = jnp.zeros_like(ds_ref)

        row = lax.broadcasted_iota(jnp.int32, (seq, dg), 0)
        for gi, w in enumerate(POOL_WINDOWS):
            sl = slice(gi * dg, (gi + 1) * dg)
            pg = p_ref[:, sl].astype(F32)
            cnt = _window_count(row, w, seq)
            mixb = (_window_sum(pg, row, w, seq, 1) / cnt - pg).astype(MM)
            wgb = w_ref[gi].astype(MM)
            yp = _dot(mixb, wgb)
            dyg = dy_ref[:, sl]
            ds_ref[:, sl] += jnp.sum(dyg * yp, axis=0, keepdims=True)
            dyp = (dyg * s_ref[:, sl]).astype(MM)
            dmixed = _dot_nt(dyp, wgb)
            dw_ref[gi] += _dot_tn(mixb, dyp)
            dp_ref[:, sl] = (_window_sum(dmixed / cnt, row, w, seq, -1) - dmixed).astype(MM)

    half = pl.BlockSpec((seq, d // 2), lambda b: (b, 0))
    wspec = pl.BlockSpec(w_grp.shape, lambda b: (0, 0, 0))
    sspec = pl.BlockSpec((1, d // 2), lambda b: (0, 0))
    return pl.pallas_call(
        body, name="pool_bwd", grid=(bl,),
        in_specs=[pl.BlockSpec((seq, d // 2), lambda b: (b, 6)), half, wspec, sspec],
        out_specs=[half, wspec, sspec],
        out_shape=[jax.ShapeDtypeStruct((t, d // 2), MM), jax.ShapeDtypeStruct(w_grp.shape, F32),
                   jax.ShapeDtypeStruct((1, d // 2), F32)],
        compiler_params=_params(1))(proj, dy, w_grp, scale)


def _attn_probs(q, kk, dh):
    s = _dot_nt(q, kk) * (float(dh) ** -0.5)
    e = jnp.exp(s - jnp.max(s, axis=-1, keepdims=True))
    return e / jnp.sum(e, axis=-1, keepdims=True)


def _attn_fwd(proj, kv, d, bl, seq, mlen, *, tq=2048):
    t = proj.shape[0]
    dh = d // 8
    tq = min(tq, seq)
    nq = seq // tq

    def body(q_ref, k_ref, v_ref, o_ref):
        a = _attn_probs(q_ref[...].astype(MM), k_ref[...].astype(MM), dh)
        o_ref[...] = _dot(a.astype(MM), v_ref[...].astype(MM)).astype(MM)

    return pl.pallas_call(
        body, name="attn_fwd", grid=(bl, HEADS, nq),
        in_specs=[pl.BlockSpec((tq, dh), lambda b, h, i: (b * nq + i, 7 * HEADS + h)),
                  pl.BlockSpec((mlen, dh), lambda b, h, i: (b, h)),
                  pl.BlockSpec((mlen, dh), lambda b, h, i: (b, HEADS + h))],
        out_specs=pl.BlockSpec((tq, dh), lambda b, h, i: (b * nq + i, h)),
        out_shape=jax.ShapeDtypeStruct((t, d // 2), MM),
        compiler_params=_params(3))(proj, kv, kv)


def _attn_bwd(proj, kv, do, d, bl, seq, mlen, *, tq=2048):
    t = proj.shape[0]
    dh = d // 8
    tq = min(tq, seq)
    nq = seq // tq

    def body(q_ref, k_ref, v_ref, do_ref, dq_ref, dk_ref, dv_ref):
        @pl.when(pl.program_id(2) == 0)
        def _():
            dk_ref[...] = jnp.zeros_like(dk_ref)
            dv_ref[...] = jnp.zeros_like(dv_ref)

        q, kk, vv = q_ref[...].astype(MM), k_ref[...].astype(MM), v_ref[...].astype(MM)
        dov = do_ref[...].astype(MM)
        a = _attn_probs(q, kk, dh)
        dp = _dot_nt(dov, vv)
        ds = (a * (dp - jnp.sum(dp * a, axis=-1, keepdims=True)) * (float(dh) ** -0.5)).astype(MM)
        dq_ref[...] = _dot(ds, kk).astype(MM)
        dk_ref[...] += _dot_tn(ds, q)
        dv_ref[...] += _dot_tn(a.astype(MM), dov)

    qs = pl.BlockSpec((tq, dh), lambda b, h, i: (b * nq + i, h))
    ms = pl.BlockSpec((mlen, dh), lambda b, h, i: (b, h))
    return pl.pallas_call(
        body, name="attn_bwd", grid=(bl, HEADS, nq),
        in_specs=[pl.BlockSpec((tq, dh), lambda b, h, i: (b * nq + i, 7 * HEADS + h)), ms,
                  pl.BlockSpec((mlen, dh), lambda b, h, i: (b, HEADS + h)), qs],
        out_specs=[qs, ms, ms],
        out_shape=[jax.ShapeDtypeStruct((t, d // 2), MM), jax.ShapeDtypeStruct((bl * mlen, d // 2), F32),
                   jax.ShapeDtypeStruct((bl * mlen, d // 2), F32)],
        compiler_params=_params(3))(proj, kv, kv, do)


def _comm_call(name, body, arrays, out_shapes):
    n = len(arrays)
    hbm = pl.BlockSpec(memory_space=pl.ANY)
    return pl.pallas_call(
        body, name=name, out_shape=out_shapes, in_specs=[hbm] * n, out_specs=[hbm] * n,
        scratch_shapes=[pltpu.SemaphoreType.DMA((7 * n,)), pltpu.SemaphoreType.DMA((7 * n,)),
                        pltpu.SemaphoreType.DMA((n,))],
    )(*arrays)


def _all_gather(name, shards):
    n = len(shards)

    def body(*refs):
        x_refs, out_refs = refs[:n], refs[n:2 * n]
        send_sems, recv_sems, local_sems = refs[2 * n:]
        x, y, c = lax.axis_index("x"), lax.axis_index("y"), lax.axis_index("c")
        me, sibling = (x, y, c), (x, y, 1 - c)
        chips = [(1 - x, y), (x, 1 - y), (1 - x, 1 - y)]

        def copy(o, k, block, to, src=None):
            slot = out_refs[o].at[4 * block[0] + 2 * block[1] + block[2]]
            return pltpu.make_async_remote_copy(
                src_ref=slot if src is None else src, dst_ref=slot, send_sem=send_sems.at[7 * o + k],
                recv_sem=recv_sems.at[7 * o + k], device_id=to, device_id_type=MESH)

        locals_, remotes = [], []
        for o in range(n):
            mine = pltpu.make_async_copy(x_refs[o], out_refs[o].at[4 * x + 2 * y + c], local_sems.at[o])
            mine.start()
            locals_.append(mine)
            first = [copy(o, 0, me, sibling, src=x_refs[o])]
            first += [copy(o, 1 + j, me, (*chip, c), src=x_refs[o]) for j, chip in enumerate(chips)]
            for cp in first:
                cp.start()
            remotes += first
        for o in range(n):
            for j, chip in enumerate(chips):
                copy(o, 1 + j, (*chip, c), me).wait_recv()
                passed = copy(o, 4 + j, (*chip, c), sibling)
                passed.start()
                remotes.append(passed)
        for o in range(n):
            copy(o, 0, sibling, me).wait_recv()
            for j, chip in enumerate(chips):
                copy(o, 4 + j, (*chip, 1 - c), me).wait_recv()
        for cp in remotes:
            cp.wait_send()
        for mine in locals_:
            mine.wait()

    outs = [jax.ShapeDtypeStruct((N_DEV,) + s.shape, s.dtype) for s in shards]
    return _comm_call(name, body, shards, outs)


def _peer_of(k, x, y, c):
    peer = (1 - x if k & 4 else x, 1 - y if k & 2 else y, 1 - c if k & 1 else c)
    return peer, 4 * peer[0] + 2 * peer[1] + peer[2]


def _split_copies(scatter, srcs, lands, send_sems, recv_sems, arriving):
    x, y, c = lax.axis_index("x"), lax.axis_index("y"), lax.axis_index("c")
    me_idx = 4 * x + 2 * y + c
    copies = []
    for o, (src, land) in enumerate(zip(srcs, lands)):
        for k in range(1, N_DEV):
            peer, p_idx = _peer_of(k, x, y, c)
            mine = src.at[p_idx] if scatter else src
            sems = dict(send_sem=send_sems.at[7 * o + k - 1], recv_sem=recv_sems.at[7 * o + k - 1],
                        device_id=peer, device_id_type=MESH)
            slot = land.at[p_idx] if arriving else land.at[me_idx]
            copies.append(pltpu.make_async_remote_copy(src_ref=mine, dst_ref=slot, **sems))
    return copies


_HBM = pl.BlockSpec(memory_space=pltpu.HBM)
_SEM = pl.BlockSpec(memory_space=pltpu.SEMAPHORE)
_EFFECT = pltpu.SideEffectType.DATAFLOW_SIDE_EFFECTING


def _exchange_start(name, scatter, arrays, after=()):
    n = len(arrays)
    lands = [lax.empty(a.shape if scatter else (N_DEV,) + a.shape, a.dtype) for a in arrays]

    def body(*refs):
        srcs, lnds = refs[:n], refs[n:2 * n]
        send_sems, recv_sems = refs[2 * n + len(after)], refs[2 * n + len(after) + 1]
        token = refs[-1]
        for cp in _split_copies(scatter, srcs, lnds, send_sems, recv_sems, False):
            cp.start()
        token[...] = jnp.zeros_like(token)

    hbm_in = [pltpu.with_memory_space_constraint(a, pltpu.HBM) for a in list(arrays) + lands]
    res = pl.pallas_call(
        body, name=name,
        out_shape=(pltpu.SemaphoreType.DMA((7 * n,)), pltpu.SemaphoreType.DMA((7 * n,)),
                   *[pltpu.HBM(a.shape, a.dtype) for a in hbm_in], jax.ShapeDtypeStruct((8, 128), F32)),
        in_specs=[_HBM] * (2 * n) + [pl.BlockSpec(memory_space=pl.ANY)] * len(after),
        out_specs=(_SEM, _SEM, *[_HBM] * (2 * n), pl.BlockSpec(memory_space=pltpu.VMEM)),
        input_output_aliases={i: 2 + i for i in range(2 * n)},
        compiler_params=pltpu.CompilerParams(has_side_effects=_EFFECT),
    )(*hbm_in, *after)
    return res[0], res[1], list(res[2:2 + n]), list(res[2 + n:2 + 2 * n]), res[-1]


def _exchange_wait(name, scatter, started, after):
    send_sems, recv_sems, srcs, lands, _ = started
    n = len(srcs)

    def body(*refs):
        src_refs, lnd_refs = refs[:n], refs[n:2 * n]
        for cp in _split_copies(scatter, src_refs, lnd_refs, refs[2 * n], refs[2 * n + 1], False):
            cp.wait_send()
        for cp in _split_copies(scatter, src_refs, lnd_refs, refs[2 * n], refs[2 * n + 1], True):
            cp.wait_recv()

    res = pl.pallas_call(
        body, name=name, out_shape=tuple(pltpu.HBM(a.shape, a.dtype) for a in srcs + lands),
        in_specs=[_HBM] * (2 * n) + [_SEM, _SEM, pl.BlockSpec(memory_space=pl.ANY)],
        out_specs=tuple([_HBM] * (2 * n)), input_output_aliases={i: i for i in range(2 * n)},
        compiler_params=pltpu.CompilerParams(has_side_effects=_EFFECT),
    )(*srcs, *lands, send_sems, recv_sems, after)
    return list(res[n:]), list(res[:n])


def _adamw(name, parts, w, m, v, prev, layer, *, tr=256):
    _, a, b = w.shape
    tr = _tile(a, tr, 8)
    c1 = 1.0 - ADAM_B1 ** ADAM_STEP
    c2 = 1.0 - ADAM_B2 ** ADAM_STEP

    def body(p_ref, w_ref, m_ref, v_ref, _g, _d, _m, _v, g_out, d_out, m_out, v_out):
        g = p_ref[0].astype(F32)
        for s in range(1, N_DEV):
            g = g + p_ref[s].astype(F32)
        mn = ADAM_B1 * m_ref[...] + (1.0 - ADAM_B1) * g
        vn = ADAM_B2 * v_ref[...] + (1.0 - ADAM_B2) * (g * g)
        g_out[...] = g
        m_out[...] = mn
        v_out[...] = vn
        d_out[...] = -ADAM_LR * ((mn / c1) / (jnp.sqrt(vn / c2) + ADAM_EPS) + ADAM_WD * w_ref[...])

    slab = pl.BlockSpec((None, tr, b), lambda i: (layer, i, 0))
    whole = pl.BlockSpec(memory_space=pl.ANY)
    return pl.pallas_call(
        body, name=name, grid=(a // tr,),
        in_specs=[pl.BlockSpec((N_DEV, tr, b), lambda i: (0, i, 0)), slab, slab, slab] + [whole] * 4,
        out_specs=[slab] * 4, out_shape=[jax.ShapeDtypeStruct(w.shape, F32)] * 4,
        input_output_aliases={4: 0, 5: 1, 6: 2, 7: 3},
        compiler_params=_params(1))(parts, w, m, v, *prev)


_COL = ("w_in", "w_pool_o", "w_mem_o", "w_ff1")
_COL_IN_PLACE = ("w_in", "w_ff1")
_BIG =("w_in", "w_ret_o", "w_pool_o", "w_mem_kv", "w_mem_o", "w_out", "w_ff1", "w_ff2")
_SMALL = ("ret_decay_logit", "w_pool_grp", "pool_scale", "norm1_g", "norm2_g", "mem_norm_g", "final_norm_g")
_WEIGHTS = ("w_in", "ret_decay_logit", "w_ret_o", "w_pool_grp", "pool_scale", "w_pool_o", "w_mem_kv", "w_mem_o",
            "w_out", "w_ff1", "w_ff2", "norm1_g", "norm2_g", "mem_norm_g", "final_norm_g")


def _small_rows(size, d):
    return -(-size // (8 * d)) * 8


def _pack_small(ws, d):
    parts = []
    for n in _SMALL:
        flat = ws[n].reshape(-1)
        rows = _small_rows(flat.shape[0], d)
        parts.append(jnp.pad(flat, (0, rows * d - flat.shape[0])).reshape(rows, d))
    return jnp.concatenate(parts, axis=0)[None]


def _unpack_small(packed, like, d):
    out, off = {}, 0
    for n in _SMALL:
        rows = _small_rows(like[n].size, d)
        out[n] = packed[0, off:off + rows].reshape(-1)[:like[n].size].reshape(like[n].shape)
        off += rows
    return out


def kernel(x, mem, w_in, ret_decay_logit, w_ret_o, w_pool_grp, pool_scale, w_pool_o, w_mem_kv, w_mem_o, w_out, w_ff1, w_ff2, norm1_g, norm2_g, mem_norm_g, final_norm_g, loss_target, m_w_in, m_ret_decay_logit, m_w_ret_o, m_w_pool_grp, m_pool_scale, m_w_pool_o, m_w_mem_kv, m_w_mem_o, m_w_out, m_w_ff1, m_w_ff2, m_norm1_g, m_norm2_g, m_mem_norm_g, m_final_norm_g, v_w_in, v_ret_decay_logit, v_w_ret_o, v_w_pool_grp, v_pool_scale, v_w_pool_o, v_w_mem_kv, v_w_mem_o, v_w_out, v_w_ff1, v_w_ff2, v_norm1_g, v_norm2_g, v_mem_norm_g, v_final_norm_g):
    w = dict(w_in=w_in, ret_decay_logit=ret_decay_logit, w_ret_o=w_ret_o, w_pool_grp=w_pool_grp,
             pool_scale=pool_scale, w_pool_o=w_pool_o, w_mem_kv=w_mem_kv, w_mem_o=w_mem_o, w_out=w_out,
             w_ff1=w_ff1, w_ff2=w_ff2, norm1_g=norm1_g, norm2_g=norm2_g, mem_norm_g=mem_norm_g,
             final_norm_g=final_norm_g)
    mom = dict(w_in=m_w_in, ret_decay_logit=m_ret_decay_logit, w_ret_o=m_w_ret_o, w_pool_grp=m_w_pool_grp,
               pool_scale=m_pool_scale, w_pool_o=m_w_pool_o, w_mem_kv=m_w_mem_kv, w_mem_o=m_w_mem_o,
               w_out=m_w_out, w_ff1=m_w_ff1, w_ff2=m_w_ff2, norm1_g=m_norm1_g, norm2_g=m_norm2_g,
               mem_norm_g=m_mem_norm_g, final_norm_g=m_final_norm_g)
    vel = dict(w_in=v_w_in, ret_decay_logit=v_ret_decay_logit, w_ret_o=v_w_ret_o, w_pool_grp=v_w_pool_grp,
               pool_scale=v_pool_scale, w_pool_o=v_w_pool_o, w_mem_kv=v_w_mem_kv, w_mem_o=v_w_mem_o,
               w_out=v_w_out, w_ff1=v_w_ff1, w_ff2=v_w_ff2, norm1_g=v_norm1_g, norm2_g=v_norm2_g,
               mem_norm_g=v_mem_norm_g, final_norm_g=v_final_norm_g)

    bl, seq, d = x.shape
    mlen = mem.shape[1]
    depth = w_in.shape[0]
    t = bl * seq
    dk = d // 8

    me_idx = 4 * lax.axis_index("x") + 2 * lax.axis_index("y") + lax.axis_index("c")

    def natural(n, g):
        if n in _COL_IN_PLACE:
            return g
        if n in _COL:
            return jnp.transpose(g, (1, 0, 2)).reshape(g.shape[1], -1)
        return g.reshape(-1, g.shape[-1])

    def finish_gather(name, names, started, after):
        got, mine = _exchange_wait(name, False, started, after)
        return {n: natural(n, lax.dynamic_update_slice(g, sh[None], (me_idx, 0, 0)))
                for n, g, sh in zip(names, got, mine)}

    shards = [{n: w[n][l].astype(MM) for n in _BIG} for l in range(depth)]
    rest = _BIG[1:]
    (w_in0,) = _all_gather("gather_w_in", [shards[0][_BIG[0]]])
    full = [{_BIG[0]: w_in0}]
    pending = _exchange_start("gather_start_0", False, [shards[0][n] for n in rest], after=[w_in0])

    inv = ROPE_BASE ** (-jnp.arange(0, dk, 2, dtype=F32) / dk)
    ang = jnp.arange(seq, dtype=F32)[:, None] * inv[None, :]
    cos2 = jnp.concatenate([jnp.cos(ang), jnp.cos(ang)], axis=-1)
    sin2 = jnp.concatenate([-jnp.sin(ang), jnp.sin(ang)], axis=-1)
    log_g = jax.nn.log_sigmoid(ret_decay_logit)
    x2 = x.reshape(t, d)
    mem2 = mem.reshape(bl * mlen, d)
    gmem = mem_norm_g.reshape(1, d)

    def merge(a_r, y_p, o_a, g_r, g_p, g_m, w_r, w_p, w_m):
        f = lambda z: z.astype(F32)
        o_r, o_p, o_m = _dot(a_r, w_r), _dot(y_p, w_p), _dot(o_a, w_m)
        return _sigmoid(f(g_r)) * o_r + _sigmoid(f(g_p)) * o_p + _sigmoid(f(g_m)) * o_m, o_r, o_p, o_m

    def relu2(u):
        r = jnp.maximum(u.astype(MM), 0.0)
        return r * r

    def ident(a):
        return a

    saved = []
    xc = x2
    for l in range(depth):
        s = dict(x_in=xc)
        started_now = ()
        if l > 0:
            full.append(finish_gather(f"gather_wait_{l}", _BIG, pending, xc))
            if l + 1 < depth:
                pending = _exchange_start(f"gather_start_{l + 1}", False, [shards[l + 1][n] for n in _BIG],
                                          after=[full[l]["w_in"]])
                started_now = (pending[4],)
        fw = full[l]
        g1 = norm1_g[l].reshape(1, d)
        g2 = norm2_g[l].reshape(1, d)
        s["proj"], s["h1"] = _pmm("proj", _rms_prologue, [(xc, d, 0)], [g1], fw["w_in"], w_mode="col",
                                  tm=2048, tn=1024, save_a=True, out_dtypes=(MM,),
                                  after=started_now if l > 0 else (pending[4],))
        proj = s["proj"]
        s["qr"], s["kr"], s["vb"] = _ret_pre(proj, cos2, sin2, d, seq)
        s["o_raw"], s["a_ret"], s["sf"], s["sb"] = _ret_core_fwd(s["qr"], s["kr"], s["vb"], proj, log_g[l],
                                                                 d, bl, seq)
        s["y"] = _pool_fwd(proj, w_pool_grp[l], pool_scale[l].reshape(1, -1), d, bl, seq)
        started_now = ()
        if l == 0:
            fw.update(finish_gather("gather_wait_0", rest, pending, s["a_ret"]))
            if depth > 1:
                pending = _exchange_start("gather_start_1", False, [shards[1][n] for n in _BIG],
                                          after=[fw["w_mem_kv"]])
                started_now = (pending[4],)
        s["kv"], s["memn"] = _pmm("mem_kv", _rms_prologue, [(mem2, d, 0)], [gmem], fw["w_mem_kv"],
                                  tm=512, tn=512, save_a=True, after=started_now)
        s["o_att"] = _attn_fwd(proj, s["kv"], d, bl, seq, mlen)
        s["x_mid"], s["merged"], s["o_ret"], s["o_pool"], s["o_mem"] = _pmm(
            "merge_out", merge,
            [(s["a_ret"], d, 0), (s["y"], d // 2, 0), (s["o_att"], d // 2, 0), (proj, d, 4), (proj, d, 5), (proj, d, 6)],
            [fw["w_ret_o"], fw["w_pool_o"], fw["w_mem_o"]], fw["w_out"], tm=512, tn=1024, residual=xc, save_a=True,
            extra_outs=[(d, MM)] * 3)
        s["u"], s["h2"] = _pmm("ff1", _rms_prologue, [(s["x_mid"], d, 0)], [g2], fw["w_ff1"], w_mode="col",
                               tm=1024, tn=1024, save_a=True, out_dtypes=(MM,))
        (xc,) = _pmm("ff2", relu2, [(s["u"], s["u"].shape[1], 0)], [], fw["w_ff2"],
                     tm=512, tn=1024, residual=s["x_mid"])
        saved.append(s)

    dxc, g_final, loss_part = _loss_head(xc, loss_target.reshape(t, d), final_norm_g.reshape(1, d))
    loss = lax.psum(loss_part[0, 0], ("x", "y", "c"))

    small_names = ("w_pool_grp", "pool_scale", "norm1_g", "norm2_g", "ret_decay_logit")
    grads = {n: [None] * depth for n in small_names}
    group_a = ("w_ff1", "w_ff2")
    group_b = tuple(n for n in _BIG if n not in group_a)
    scatters = {}
    dmemn = jnp.zeros((bl * mlen, d), F32)

    def relu2_bwd(acc, u):
        return (acc * (2.0 * jnp.maximum(u.astype(F32), 0.0)),)

    def gates_bwd(acc, g_r, g_p, g_m, o_r, o_p, o_m, w_r, w_p, w_m):
        d_os, d_gs, backs = [], [], []
        for gz, oz, wz in ((g_r, o_r, w_r), (g_p, o_p, w_p), (g_m, o_m, w_m)):
            sg = _sigmoid(gz.astype(F32))
            d_o = (acc * sg).astype(MM)
            d_os.append(d_o)
            d_gs.append(acc * oz.astype(F32) * (sg * (1.0 - sg)))
            backs.append(_dot_nt(d_o, wz))
        return tuple(d_os + d_gs + backs)

    def to_send(n, g):
        a, b = w[n].shape[1:]
        if n in _COL_IN_PLACE:
            return g
        if n in _COL:
            return jnp.transpose(g.reshape(a, N_DEV, b), (1, 0, 2))
        return g.reshape(N_DEV, a, b)

    for l in reversed(range(depth)):
        s = saved[l]
        fw = full[l]
        proj = s["proj"]
        g1 = norm1_g[l].reshape(1, d)
        g2 = norm2_g[l].reshape(1, d)
        dw = {}
        (du,) = _pmm("ff2_bwd", ident, [(dxc, d, 0)], [], fw["w_ff2"], w_mode="nt", tm=1024, tn=1024,
                     epilogue=relu2_bwd, epi_ins=[(s["u"], 0)], out_dtypes=(MM,))
        dw["w_ff2"] = _tnmm("dw_ff2", s["u"], dxc, a_fn=relu2)
        dw["w_ff1"] = _tnmm("dw_ff1", s["h2"], du, col_shards=True)
        scatters[l, "a"] = _exchange_start(f"scatter_start_a{l}", True, [to_send(n, dw[n]) for n in group_a])
        dmid, grads["norm2_g"][l] = _mm_rms_bwd("ff1_norm2_bwd", du, fw["w_ff1"], s["x_mid"], g2, dxc, tm=512)
        d_oret, d_opool, d_omem, dgr, dgp, dgm, da_ret, dy, do_att = _pmm(
            "out_bwd", ident, [(dmid, d, 0)], [], fw["w_out"], w_mode="nt", tm=256, tn=d, epilogue=gates_bwd,
            epi_ins=[(proj, 4 * d), (proj, 5 * d), (proj, 6 * d), (s["o_ret"], 0), (s["o_pool"], 0), (s["o_mem"], 0)],
            epi_full=[fw["w_ret_o"], fw["w_pool_o"], fw["w_mem_o"]], out_dtypes=(MM,) * 6 + (F32,) * 3,
            out_widths=[d] * 7 + [d // 2] * 2, after=(scatters[l, "a"][4],))
        dw["w_out"] = _tnmm("dw_out", s["merged"], dmid)
        dw["w_ret_o"] = _tnmm("dw_ret_o", s["a_ret"], d_oret)
        dw["w_pool_o"] = _tnmm("dw_pool_o", s["y"], d_opool)
        dw["w_mem_o"] = _tnmm("dw_mem_o", s["o_att"], d_omem)
        dg_ret, do_ret = _ret_post_bwd(da_ret, proj, s["o_raw"], d)
        dq, dkk, dvv, dlf, dlb = _ret_core_bwd(s["qr"], s["kr"], s["vb"], do_ret, s["sf"], s["sb"], cos2, sin2,
                                               log_g[l], d, bl, seq)
        dl = jnp.stack([dlf[:, 0, 0].reshape(bl, HEADS).sum(0), dlb[:, 0, 0].reshape(bl, HEADS).sum(0)])
        grads["ret_decay_logit"][l] = dl * jax.nn.sigmoid(-ret_decay_logit[l])
        dp, grads["w_pool_grp"][l], dscale = _pool_bwd(proj, dy, w_pool_grp[l], pool_scale[l].reshape(1, -1),
                                                       d, bl, seq)
        grads["pool_scale"][l] = dscale.reshape(-1)
        dqm, dmk, dmv = _attn_bwd(proj, s["kv"], do_att, d, bl, seq, mlen)
        dkv = jnp.concatenate([dmk, dmv], axis=-1).astype(MM)
        dw["w_mem_kv"] = _tnmm("dw_mem_kv", s["memn"], dkv)
        (dmemn,) = _pmm("mem_kv_bwd", None, [(dkv, d, 0)], [], fw["w_mem_kv"], w_mode="nt", tm=512, tn=512,
                        residual=dmemn)
        dproj = [dq, dkk, dvv, dg_ret, dp, dqm, dgr, dgp, dgm]
        dw["w_in"] = _tnmm("dw_in", s["h1"], dproj, col_shards=True, tm=512, tk=512)
        scatters[l, "b"] = _exchange_start(f"scatter_start_b{l}", True, [to_send(n, dw[n]) for n in group_b])
        dxc, grads["norm1_g"][l] = _mm_rms_bwd("proj_norm1_bwd", dproj, fw["w_in"], s["x_in"], g1, dmid, tm=256,
                                               after=(scatters[l, "b"][4],))

    _, g_memn = _rms_bwd("mem_norm_bwd", dmemn, mem2, gmem, None)
    grad_x = dxc.reshape(bl, seq, d)

    small_g = dict(ret_decay_logit=jnp.stack(grads["ret_decay_logit"]), w_pool_grp=jnp.stack(grads["w_pool_grp"]),
                   pool_scale=jnp.stack(grads["pool_scale"]),
                   norm1_g=jnp.concatenate(grads["norm1_g"], axis=0), norm2_g=jnp.concatenate(grads["norm2_g"], axis=0),
                   mem_norm_g=g_memn.reshape(-1), final_norm_g=g_final.reshape(-1))
    small_started = _exchange_start("gather_small_start", False, [_pack_small(small_g, d)[0]])

    big = {n: [lax.empty(w[n].shape, F32) for _ in range(4)] for n in _BIG}

    def update(l, grp, names, after):
        recv, sent = _exchange_wait(f"scatter_wait_{grp}{l}", True, scatters[l, grp], after)
        for n, r, snt in zip(names, recv, sent):
            own = lax.dynamic_slice_in_dim(snt, me_idx, 1, axis=0)
            parts = lax.dynamic_update_slice(r, own, (me_idx, 0, 0))
            big[n] = _adamw("adamw_" + n, parts, w[n], mom[n], vel[n], big[n], l)
        return big[names[-1]][0]

    after = dxc
    for l in reversed(range(1, depth)):
        for grp, names in (("a", group_a), ("b", group_b)):
            after = update(l, grp, names, after)
    after = update(0, "a", group_a, after)
    (small_land,), (small_mine,) = _exchange_wait("gather_small_wait", False, small_started, after)
    small_parts = lax.dynamic_update_slice(small_land, small_mine[None], (me_idx, 0, 0))
    w_small = _pack_small(w, d)
    small = _adamw("adamw_small", small_parts, w_small, _pack_small(mom, d), _pack_small(vel, d),
                   [lax.empty(w_small.shape, F32) for _ in range(4)], 0)
    after = small[0]
    small = [_unpack_small(o, w, d) for o in small]
    update(0, "b", group_b, after)

    outs = [loss, grad_x]
    for k in range(4):
        outs += [big[n][k] if n in _BIG else small[k][n] for n in _WEIGHTS]
    return tuple(outs)
```

```python
import jax
import jax.numpy as jnp
from jax import lax
from jax.experimental import pallas as pl
from jax.experimental.pallas import tpu as pltpu

F32 = jnp.float32
MM = jnp.bfloat16
N_DEV = 8
HEADS = 4
POOL_WINDOWS = (2, 4, 8, 16)
EPS = 1e-6
ROPE_BASE = 10000.0
ADAM_LR, ADAM_B1, ADAM_B2, ADAM_EPS, ADAM_WD, ADAM_STEP = 0.001, 0.9, 0.999, 1e-08, 0.01, 10
V7X_VMEM_LIMIT = 56 * 1024 * 1024
MESH = pl.DeviceIdType.MESH


def _params(n_axes):
    return pltpu.CompilerParams(dimension_semantics=("arbitrary",) * n_axes,
                                vmem_limit_bytes=V7X_VMEM_LIMIT)


def _tile(n, pref, align=128):
    cands = [c for c in range(align, min(pref, n) + 1, align) if n % c == 0]
    return max(cands) if cands else n


def _sigmoid(z):
    return 0.5 * jnp.tanh(0.5 * z) + 0.5


def _dot(a, b):
    return jnp.dot(a, b, preferred_element_type=F32)


def _dot_nt(a, b):
    return lax.dot_general(a, b, (((1,), (1,)), ((), ())), preferred_element_type=F32)


def _dot_tn(a, b):
    return lax.dot_general(a, b, (((0,), (0,)), ((), ())), preferred_element_type=F32)


def _pmm(name, prologue, row_ins, vec_ins, w, *, tm, tn, w_mode="nn", residual=None, save_a=False,
         epilogue=None, epi_ins=(), out_dtypes=(F32,), after=(), extra_outs=(), epi_full=(), out_widths=None):
    m = row_ins[0][0].shape[0]
    wb = None
    shards_per_tile = 1
    if w_mode == "nn":
        k, n = w.shape
        tn = _tile(n, tn)
        w_spec = pl.BlockSpec((k, tn), lambda i, j: (0, j))
    elif w_mode == "nt":
        n, k = w.shape
        tn = _tile(n, tn)
        w_spec = pl.BlockSpec((tn, k), lambda i, j: (j, 0))
    elif w_mode == "col":
        _, k, wb = w.shape
        n = N_DEV * wb
        while 2 * shards_per_tile * wb <= tn and 2 * shards_per_tile <= N_DEV:
            shards_per_tile *= 2
        if shards_per_tile > 1:
            tn = shards_per_tile * wb
            w_spec = pl.BlockSpec((shards_per_tile, k, wb), lambda i, j: (j, 0, 0))
        else:
            tn = _tile(wb, tn)
            w_spec = pl.BlockSpec((None, k, tn), lambda i, j, q=wb // tn: (j // q, 0, j % q))
    else:
        _, n, wb = w.shape
        k = N_DEV * wb
        tn = _tile(n, tn)
        w_spec = pl.BlockSpec((N_DEV, tn, wb), lambda i, j: (0, j, 0))
    tm = _tile(m, tm, 8)
    n_row, n_vec, n_epi, n_out = len(row_ins), len(vec_ins), len(epi_ins), len(out_dtypes)
    has_res = residual is not None
    use_scr = prologue is not None
    out_widths = [n] * n_out if out_widths is None else list(out_widths)
    assert all(wd == n for wd in out_widths) or tn == n

    def body(*refs):
        row_refs = refs[:n_row]
        p = n_row
        vec_refs = refs[p:p + n_vec]
        p += n_vec
        w_ref = refs[p]
        p += 1
        res_ref = refs[p] if has_res else None
        p += int(has_res)
        epi_refs = refs[p:p + n_epi + len(epi_full)]
        p += n_epi + len(epi_full) + len(after)
        out_refs = refs[p:p + n_out]
        p += n_out
        a_out = refs[p] if save_a else None
        p += int(save_a)
        extra_refs = refs[p:p + len(extra_outs)]
        p += len(extra_outs)
        if use_scr:
            a_src = refs[p]

            @pl.when(pl.program_id(1) == 0)
            def _():
                made = prologue(*[r[...] for r in row_refs], *[v[...] for v in vec_refs])
                made = made if isinstance(made, tuple) else (made,)
                a = made[0].astype(MM)
                a_src[...] = a
                if save_a:
                    a_out[...] = a
                for e_ref, e in zip(extra_refs, made[1:]):
                    e_ref[...] = e.astype(e_ref.dtype)
        else:
            a_src = row_refs[0]
        if w_mode == "nt":
            acc = _dot_nt(a_src[...], w_ref[...])
        elif w_mode == "col_t":
            acc = _dot_nt(a_src[:, 0:wb], w_ref[0])
            for dev in range(1, N_DEV):
                acc = acc + _dot_nt(a_src[:, dev * wb:(dev + 1) * wb], w_ref[dev])
        elif shards_per_tile > 1:
            acc = jnp.concatenate([_dot(a_src[...], w_ref[sh]) for sh in range(shards_per_tile)], axis=1)
        else:
            acc = _dot(a_src[...], w_ref[...])
        if has_res:
            acc = acc + res_ref[...]
        outs = epilogue(acc, *[e[...] for e in epi_refs]) if epilogue is not None else (acc,)
        for o_ref, o in zip(out_refs, outs):
            o_ref[...] = o.astype(o_ref.dtype)

    in_specs = [pl.BlockSpec((tm, wd), lambda i, j, cb=cb: (i, cb)) for (_, wd, cb) in row_ins]
    in_specs += [pl.BlockSpec(v.shape, lambda i, j: (0, 0)) for v in vec_ins]
    in_specs += [w_spec]
    args = [r[0] for r in row_ins] + list(vec_ins) + [w]
    if has_res:
        in_specs.append(pl.BlockSpec((tm, tn), lambda i, j: (i, j)))
        args.append(residual)
    for (arr, off) in epi_ins:
        assert off % tn == 0
        in_specs.append(pl.BlockSpec((tm, tn), lambda i, j, ob=off // tn: (i, ob + j)))
        args.append(arr)
    in_specs += [pl.BlockSpec(v.shape, lambda i, j: (0, 0)) for v in epi_full]
    args += list(epi_full)
    n_after = len(after)
    in_specs += [pl.BlockSpec(memory_space=pl.ANY)] * n_after
    args += list(after)
    out_specs = [pl.BlockSpec((tm, tn if wd == n else wd), lambda i, j: (i, j)) for wd in out_widths]
    out_shape = [jax.ShapeDtypeStruct((m, wd), dt) for wd, dt in zip(out_widths, out_dtypes)]
    if save_a:
        out_specs.append(pl.BlockSpec((tm, k), lambda i, j: (i, 0)))
        out_shape.append(jax.ShapeDtypeStruct((m, k), MM))
    for wd, dt in extra_outs:
        out_specs.append(pl.BlockSpec((tm, wd), lambda i, j: (i, 0)))
        out_shape.append(jax.ShapeDtypeStruct((m, wd), dt))
    scratch = [pltpu.VMEM((tm, k), MM)] if use_scr else []
    return pl.pallas_call(body, name=name, grid=(m // tm, n // tn), in_specs=in_specs,
                          out_specs=out_specs, out_shape=out_shape, scratch_shapes=scratch,
                          compiler_params=_params(2))(*args)


def _tnmm(name, a, b, *, tm=1024, tn=1024, tk=1024, col_shards=False, a_fn=None):
    t, m = a.shape
    pieces = list(b) if isinstance(b, (list, tuple)) else [b]
    widths = [p.shape[1] for p in pieces]
    offs = [sum(widths[:p]) for p in range(len(pieces))]
    n = sum(widths)
    tm, tk = _tile(m, tm), _tile(t, tk, 8)
    per_tile = 1
    if col_shards:
        wb = n // N_DEV
        if len(pieces) > 1:
            tn = n
        while 2 * per_tile * wb <= tn and 2 * per_tile <= N_DEV:
            per_tile *= 2
        tn = per_tile * wb
        out_spec = pl.BlockSpec((per_tile, tm, wb), lambda i, j, kk: (j, i, 0))
        out_shape = jax.ShapeDtypeStruct((N_DEV, m, wb), MM)
    else:
        tn = _tile(n, tn)
        out_spec = pl.BlockSpec((tm, tn), lambda i, j, kk: (i, j))
        out_shape = jax.ShapeDtypeStruct((m, n), MM)
    nk = t // tk

    assert len(pieces) == 1 or tn == n

    def body(a_ref, *rest):
        b_refs, (o_ref, acc) = rest[:len(pieces)], rest[len(pieces):]
        kk = pl.program_id(2)

        @pl.when(kk == 0)
        def _():
            acc[...] = jnp.zeros_like(acc)

        av = (a_ref[...] if a_fn is None else a_fn(a_ref[...])).astype(MM)
        if len(pieces) == 1:
            acc[...] += _dot_tn(av, b_refs[0][...].astype(MM))
        else:
            for b_ref, off, wd in zip(b_refs, offs, widths):
                acc[:, off:off + wd] += _dot_tn(av, b_ref[...].astype(MM))

        @pl.when(kk == nk - 1)
        def _():
            if col_shards:
                for sh in range(per_tile):
                    o_ref[sh] = acc[:, sh * wb:(sh + 1) * wb].astype(o_ref.dtype)
            else:
                o_ref[...] = acc[...].astype(o_ref.dtype)

    return pl.pallas_call(
        body, name=name, grid=(m // tm, n // tn, nk),
        in_specs=[pl.BlockSpec((tk, tm), lambda i, j, kk: (kk, i))]
        + [pl.BlockSpec((tk, tn if len(pieces) == 1 else wd), lambda i, j, kk: (kk, j)) for wd in widths],
        out_specs=out_spec, out_shape=out_shape,
        scratch_shapes=[pltpu.VMEM((tm, tn), F32)],
        compiler_params=_params(3))(a, *pieces)


def _rms_prologue(x, g):
    r = lax.rsqrt(jnp.mean(x * x, axis=-1, keepdims=True) + EPS)
    return x * r * g


def _rms_bwd_rows(dh, x, g):
    d = x.shape[-1]
    r = lax.rsqrt(jnp.mean(x * x, axis=-1, keepdims=True) + EPS)
    xh = x * r
    dxh = dh * g
    dx = r * (dxh - xh * (jnp.sum(dxh * xh, axis=-1, keepdims=True) / d))
    dg = jnp.sum(dh * xh, axis=0, keepdims=True)
    return dx, dg


def _rms_bwd(name, dh, x, g, dres, *, tm=512):
    m, d = x.shape
    tm = min(tm, m)
    has_res = dres is not None

    def body(*refs):
        if has_res:
            dh_ref, x_ref, g_ref, r_ref, dx_ref, dg_ref = refs
        else:
            dh_ref, x_ref, g_ref, dx_ref, dg_ref = refs
        dx, dg = _rms_bwd_rows(dh_ref[...], x_ref[...], g_ref[...])
        if has_res:
            dx = dx + r_ref[...]
        dx_ref[...] = dx

        @pl.when(pl.program_id(0) == 0)
        def _():
            dg_ref[...] = jnp.zeros_like(dg_ref)

        dg_ref[...] += dg

    row = pl.BlockSpec((tm, d), lambda i: (i, 0))
    vec = pl.BlockSpec((1, d), lambda i: (0, 0))
    in_specs = [row, row, vec] + ([row] if has_res else [])
    args = [dh, x, g] + ([dres] if has_res else [])
    return pl.pallas_call(body, name=name, grid=(m // tm,), in_specs=in_specs, out_specs=[row, vec],
                          out_shape=[jax.ShapeDtypeStruct((m, d), F32), jax.ShapeDtypeStruct((1, d), F32)],
                          compiler_params=_params(1))(*args)


def _mm_rms_bwd(name, a, w, x, g, dres, *, tm, after=()):
    pieces = list(a) if isinstance(a, (list, tuple)) else [a]
    widths = [p.shape[1] for p in pieces]
    offs = [sum(widths[:p]) for p in range(len(pieces))]
    m = pieces[0].shape[0]
    _, d, wb = w.shape
    tm = _tile(m, tm, 8)
    n_a = len(pieces)

    def body(*refs):
        a_refs = refs[:n_a]
        w_ref, x_ref, g_ref, r_ref = refs[n_a:n_a + 4]
        dx_ref, dg_ref = refs[n_a + 4 + len(after):]

        def window(lo, hi):
            parts = [a_ref[:, max(lo, off) - off:min(hi, off + wd) - off]
                     for a_ref, off, wd in zip(a_refs, offs, widths) if min(hi, off + wd) > max(lo, off)]
            return parts[0] if len(parts) == 1 else jnp.concatenate(parts, axis=1)

        dh = _dot_nt(window(0, wb), w_ref[0])
        for dev in range(1, N_DEV):
            dh = dh + _dot_nt(window(dev * wb, (dev + 1) * wb), w_ref[dev])
        dx, dg = _rms_bwd_rows(dh, x_ref[...], g_ref[...])
        dx_ref[...] = dx + r_ref[...]

        @pl.when(pl.program_id(0) == 0)
        def _():
            dg_ref[...] = jnp.zeros_like(dg_ref)

        dg_ref[...] += dg

    row = pl.BlockSpec((tm, d), lambda i: (i, 0))
    vec = pl.BlockSpec((1, d), lambda i: (0, 0))
    return pl.pallas_call(
        body, name=name, grid=(m // tm,),
        in_specs=[pl.BlockSpec((tm, wd), lambda i: (i, 0)) for wd in widths]
        + [pl.BlockSpec(w.shape, lambda i: (0, 0, 0)), row, vec, row]
        + [pl.BlockSpec(memory_space=pl.ANY)] * len(after),
        out_specs=[row, vec],
        out_shape=[jax.ShapeDtypeStruct((m, d), F32), jax.ShapeDtypeStruct((1, d), F32)],
        compiler_params=_params(1))(*pieces, w, x, g, dres, *after)


def _loss_head(x, target, g, *, tm=256):
    m, d = x.shape
    tm = min(tm, m)

    def body(x_ref, t_ref, g_ref, dx_ref, dg_ref, loss_ref):
        xv, gv = x_ref[...], g_ref[...]
        y = _rms_prologue(xv, gv)
        err = y - t_ref[...]
        part = 0.5 * jnp.sum(jnp.sum(err * err, axis=-1, keepdims=True) / d)
        dx, dg = _rms_bwd_rows(err / d, xv, gv)
        dx_ref[...] = dx

        @pl.when(pl.program_id(0) == 0)
        def _():
            dg_ref[...] = jnp.zeros_like(dg_ref)
            loss_ref[...] = jnp.zeros_like(loss_ref)

        dg_ref[...] += dg
        loss_ref[...] += jnp.full(loss_ref.shape, part, F32)

    row = pl.BlockSpec((tm, d), lambda i: (i, 0))
    vec = pl.BlockSpec((1, d), lambda i: (0, 0))
    lspec = pl.BlockSpec((1, 128), lambda i: (0, 0))
    return pl.pallas_call(body, name="loss_head", grid=(m // tm,), in_specs=[row, row, vec],
                          out_specs=[row, vec, lspec],
                          out_shape=[jax.ShapeDtypeStruct((m, d), F32), jax.ShapeDtypeStruct((1, d), F32),
                                     jax.ShapeDtypeStruct((1, 128), F32)],
                          compiler_params=_params(1))(x, target, g)


def _rot(xv, cos2, sin2, half):
    return xv * cos2 + pltpu.roll(xv, half, 1) * sin2


def _rot_t(dv, cos2, sin2, half):
    return dv * cos2 + pltpu.roll(dv * sin2, half, 1)


def _ret_pre(proj, cos2, sin2, d, seq, *, ts=512):
    t = proj.shape[0]
    ts = min(ts, seq)
    dk = d // 8
    ns = seq // ts
    scale = float(dk) ** -0.5

    def body(q_ref, k_ref, v_ref, c_ref, s_ref, qo, ko, vo):
        c, s = c_ref[...], s_ref[...]
        for h in range(HEADS):
            sl = slice(h * dk, (h + 1) * dk)
            qo[:, sl] = _rot(q_ref[:, sl].astype(F32), c, s, dk // 2).astype(MM)
            ko[:, sl] = (_rot(k_ref[:, sl].astype(F32), c, s, dk // 2) * scale).astype(MM)
        vo[...] = v_ref[...].astype(MM)

    half = pl.BlockSpec((ts, d // 2), lambda i: (i, 0))
    tab = pl.BlockSpec((ts, dk), lambda i: (i % ns, 0))
    return pl.pallas_call(
        body, name="ret_pre", grid=(t // ts,),
        in_specs=[half, pl.BlockSpec((ts, d // 2), lambda i: (i, 1)), pl.BlockSpec((ts, d), lambda i: (i, 1)),
                  tab, tab],
        out_specs=[half, half, pl.BlockSpec((ts, d), lambda i: (i, 0))],
        out_shape=[jax.ShapeDtypeStruct((t, d // 2), MM), jax.ShapeDtypeStruct((t, d // 2), MM),
                   jax.ShapeDtypeStruct((t, d), MM)],
        compiler_params=_params(1))(proj, proj, proj, cos2, sin2)


def _ret_consts(lg_ref, h, t, dk):
    lf, lb = lg_ref[0, h], lg_ref[1, h]
    ab = (lax.broadcasted_iota(jnp.int32, (t, t), 0) - lax.broadcasted_iota(jnp.int32, (t, t), 1)).astype(F32)
    dmat = jnp.exp(jnp.where(ab >= 0, lf * ab, -lb * ab))
    up = lax.broadcasted_iota(jnp.int32, (t, dk), 0).astype(F32) + 1.0
    down = float(t) - up
    one = jnp.ones((1, 1), F32)
    return dict(ab=ab, dmat=dmat, xi_f=jnp.exp(lf * up), zeta_f=jnp.exp(lf * down), xi_b=jnp.exp(lb * up),
                zeta_b=jnp.exp(lb * down), up=up[:, 0:1], down=down[:, 0:1],
                cf=jnp.exp(one * (lf * t)), cb=jnp.exp(one * (lb * t)))


def _scaled(xv, rows):
    return (xv.astype(F32) * rows).astype(MM)


def _ret_core_fwd(qr, kr, vb, proj, lg, d, bl, seq, *, tc=256):
    t = qr.shape[0]
    dk, dv = d // 8, d // 4
    tc = min(tc, seq)
    nc = seq // tc

    def body(lg_ref, q_ref, k_ref, v_ref, g_ref, o_ref, a_ref, sf_ref, sb_ref):
        c = _ret_consts(lg_ref, pl.program_id(1), tc, dk)

        def rows_of(i):
            return pl.ds(pl.multiple_of(i * tc, tc), tc)

        def fwd_step(i, sf):
            rows = rows_of(i)
            sf_ref[i] = sf
            q, kk, v = q_ref[rows, :], k_ref[rows, :], v_ref[rows, :]
            p = (_dot_nt(q, kk) * c["dmat"]).astype(MM)
            o_ref[rows, :] = _dot(p, v) + _dot(_scaled(q, c["xi_f"]), sf.astype(MM))
            return sf * c["cf"] + _dot_tn(_scaled(kk, c["zeta_f"]), v)

        lax.fori_loop(0, nc, fwd_step, jnp.zeros((dk, dv), F32))

        def bwd_step(ii, sb):
            rows = rows_of(nc - 1 - ii)
            sb_ref[nc - 1 - ii] = sb
            q, kk, v = q_ref[rows, :], k_ref[rows, :], v_ref[rows, :]
            o_ref[rows, :] += _dot(_scaled(q, c["zeta_b"]), sb.astype(MM))
            return sb * c["cb"] + _dot_tn(_scaled(kk, c["xi_b"]), v)

        lax.fori_loop(0, nc, bwd_step, jnp.zeros((dk, dv), F32))

        def post(i, carry):
            rows = rows_of(i)
            o = o_ref[rows, :]
            oc = o - jnp.mean(o, axis=-1, keepdims=True)
            on = oc * lax.rsqrt(jnp.mean(oc * oc, axis=-1, keepdims=True) + EPS)
            g = g_ref[rows, :].astype(F32)
            a_ref[rows, :] = (on * (g * _sigmoid(g))).astype(MM)
            return carry

        lax.fori_loop(0, nc, post, 0)

    qk = pl.BlockSpec((seq, dk), lambda b, h: (b, h))
    vv = pl.BlockSpec((seq, dv), lambda b, h: (b, h))
    states = pl.BlockSpec((None, nc, dk, dv), lambda b, h: (b * HEADS + h, 0, 0, 0))
    return pl.pallas_call(
        body, name="ret_core_fwd", grid=(bl, HEADS),
        in_specs=[pl.BlockSpec(memory_space=pltpu.SMEM), qk, qk, vv,
                  pl.BlockSpec((seq, dv), lambda b, h: (b, 2 * HEADS + h))],
        out_specs=[vv, vv, states, states],
        out_shape=[jax.ShapeDtypeStruct((t, d), F32), jax.ShapeDtypeStruct((t, d), MM),
                   jax.ShapeDtypeStruct((bl * HEADS, nc, dk, dv), F32),
                   jax.ShapeDtypeStruct((bl * HEADS, nc, dk, dv), F32)],
        compiler_params=_params(2))(lg, qr, kr, vb, proj)


def _ret_post_bwd(da, proj, o_raw, d, *, ts=2048):
    t = da.shape[0]
    dv = d // 4
    ts = min(ts, t)

    def body(da_ref, g_ref, o_ref, dg_ref, do_ref):
        o, g, dav = o_ref[...], g_ref[...].astype(F32), da_ref[...]
        mu = jnp.mean(o, axis=-1, keepdims=True)
        oc = o - mu
        r = lax.rsqrt(jnp.mean(oc * oc, axis=-1, keepdims=True) + EPS)
        on = oc * r
        sg = _sigmoid(g)
        don = dav * (g * sg)
        dg_ref[...] = (dav * on * (sg * (1.0 + g * (1.0 - sg)))).astype(MM)
        do = r * (don - jnp.mean(don, axis=-1, keepdims=True) - on * jnp.mean(don * on, axis=-1, keepdims=True))
        do_ref[...] = do.astype(MM)

    blk = pl.BlockSpec((ts, dv), lambda i, h: (i, h))
    return pl.pallas_call(
        body, name="ret_post_bwd", grid=(t // ts, HEADS),
        in_specs=[blk, pl.BlockSpec((ts, dv), lambda i, h: (i, 2 * HEADS + h)), blk],
        out_specs=[blk, blk],
        out_shape=[jax.ShapeDtypeStruct((t, d), MM), jax.ShapeDtypeStruct((t, d), MM)],
        compiler_params=_params(2))(da, proj, o_raw)


def _ret_core_bwd(qr, kr, vb, do, sf_in, sb_in, cos2, sin2, lg, d, bl, seq, *, tc=256):
    t = qr.shape[0]
    dk, dv = d // 8, d // 4
    tc = min(tc, seq)
    nc = seq // tc
    scale = float(dk) ** -0.5

    def body(lg_ref, q_ref, k_ref, v_ref, do_ref, sf_all, sb_all, c_ref, s_ref, dq_ref, dk_ref, dv_ref,
             dlf_ref, dlb_ref, dq_acc, dk_acc, dv_acc):
        c = _ret_consts(lg_ref, pl.program_id(1), tc, dk)
        fwd = c["ab"] >= 0
        zero_state = jnp.zeros((dk, dv), F32)
        zero = jnp.zeros((1, 1), F32)

        def rows_of(i):
            return pl.ds(pl.multiple_of(i * tc, tc), tc)

        def total(xv):
            return jnp.sum(xv, keepdims=True)

        def fwd_sweep(i, carry):
            hh, dlf, dlb = carry
            rows = rows_of(i)
            q, kk, v, dov = q_ref[rows, :], k_ref[rows, :], v_ref[rows, :], do_ref[rows, :]
            dof, vf = dov.astype(F32), v.astype(F32)
            p = _dot_nt(q, kk) * c["dmat"]
            da = _dot_nt(dov, v)
            x = p * da * c["ab"]
            dlf = dlf + total(jnp.where(fwd, x, 0.0))
            dlb = dlb - total(jnp.where(fwd, 0.0, x))
            pb, dpb = p.astype(MM), (da * c["dmat"]).astype(MM)
            dq = _dot(dpb, kk)
            dkc = _dot_tn(dpb, q)
            dvc = _dot_tn(pb, dov)
            sf, sb = sf_all[i], sb_all[i]
            sfb, sbb = sf.astype(MM), sb.astype(MM)
            q_xf, q_zb = _scaled(q, c["xi_f"]), _scaled(q, c["zeta_b"])
            dq = dq + _dot_nt(dov, sfb) * c["xi_f"] + _dot_nt(dov, sbb) * c["zeta_b"]
            dlf = dlf + total(jnp.sum(_dot(q_xf, sfb) * dof, axis=-1, keepdims=True) * c["up"])
            dlb = dlb + total(jnp.sum(_dot(q_zb, sbb) * dof, axis=-1, keepdims=True) * c["down"])
            hb = hh.astype(MM)
            dkc = dkc + _dot_nt(v, hb) * c["xi_b"]
            dv_bx = _dot(_scaled(kk, c["xi_b"]), hb)
            dlb = dlb + total(jnp.sum(vf * dv_bx, axis=-1, keepdims=True) * c["up"])
            dlb = dlb + float(tc) * total(hh * (sb * c["cb"]))
            dq_acc[rows, :] = dq
            dk_acc[rows, :] = dkc
            dv_acc[rows, :] = dvc + dv_bx
            return hh * c["cb"] + _dot_tn(q_zb, dov), dlf, dlb

        _, dlf, dlb = lax.fori_loop(0, nc, fwd_sweep, (zero_state, zero, zero))

        def rev_sweep(ii, carry):
            gg, dlf = carry
            i = nc - 1 - ii
            rows = rows_of(i)
            q, kk, v, dov = q_ref[rows, :], k_ref[rows, :], v_ref[rows, :], do_ref[rows, :]
            gb = gg.astype(MM)
            dk_acc[rows, :] += _dot_nt(v, gb) * c["zeta_f"]
            dv_fx = _dot(_scaled(kk, c["zeta_f"]), gb)
            dv_acc[rows, :] += dv_fx
            dlf = dlf + total(jnp.sum(v.astype(F32) * dv_fx, axis=-1, keepdims=True) * c["down"])
            dlf = dlf + float(tc) * total(gg * (sf_all[i] * c["cf"]))
            return gg * c["cf"] + _dot_tn(_scaled(q, c["xi_f"]), dov), dlf

        _, dlf = lax.fori_loop(0, nc, rev_sweep, (zero_state, dlf))

        cs, sn = c_ref[...], s_ref[...]
        dq_ref[...] = _rot_t(dq_acc[...], cs, sn, dk // 2).astype(MM)
        dk_ref[...] = (_rot_t(dk_acc[...], cs, sn, dk // 2) * scale).astype(MM)
        dv_ref[...] = dv_acc[...].astype(MM)
        dlf_ref[...] = jnp.broadcast_to(dlf, dlf_ref.shape)
        dlb_ref[...] = jnp.broadcast_to(dlb, dlb_ref.shape)

    qk = pl.BlockSpec((seq, dk), lambda b, h: (b, h))
    vv = pl.BlockSpec((seq, dv), lambda b, h: (b, h))
    tab = pl.BlockSpec((seq, dk), lambda b, h: (0, 0))
    dl = pl.BlockSpec((None, 8, 128), lambda b, h: (b * HEADS + h, 0, 0))
    states = pl.BlockSpec((None, nc, dk, dv), lambda b, h: (b * HEADS + h, 0, 0, 0))
    return pl.pallas_call(
        body, name="ret_core_bwd", grid=(bl, HEADS),
        in_specs=[pl.BlockSpec(memory_space=pltpu.SMEM), qk, qk, vv, vv, states, states, tab, tab],
        out_specs=[qk, qk, vv, dl, dl],
        out_shape=[jax.ShapeDtypeStruct((t, d // 2), MM), jax.ShapeDtypeStruct((t, d // 2), MM),
                   jax.ShapeDtypeStruct((t, d), MM),
                   jax.ShapeDtypeStruct((bl * HEADS, 8, 128), F32), jax.ShapeDtypeStruct((bl * HEADS, 8, 128), F32)],
        scratch_shapes=[pltpu.VMEM((seq, dk), F32), pltpu.VMEM((seq, dk), F32), pltpu.VMEM((seq, dv), F32)],
        compiler_params=_params(2))(lg, qr, kr, vb, do, sf_in, sb_in, cos2, sin2)


def _window_count(row, w, seq):
    return (jnp.minimum(row + w // 2, seq) - jnp.maximum(row - w // 2, 0)).astype(F32)


def _window_sum(pv, row, w, seq, sign):
    acc = None
    for j in range(-(w // 2), w // 2):
        if j == 0:
            term = pv
        else:
            src = row + sign * j
            term = jnp.where((src >= 0) & (src < seq), pltpu.roll(pv, (-sign * j) % seq, 0), 0.0)
        acc = term if acc is None else acc + term
    return acc


def _pool_fwd(proj, w_grp, scale, d, bl, seq):
    t = proj.shape[0]
    dg = d // 8

    def body(p_ref, w_ref, s_ref, y_ref):
        row = lax.broadcasted_iota(jnp.int32, (seq, dg), 0)
        for gi, w in enumerate(POOL_WINDOWS):
            sl = slice(gi * dg, (gi + 1) * dg)
            pg = p_ref[:, sl].astype(F32)
            mixed = _window_sum(pg, row, w, seq, 1) / _window_count(row, w, seq) - pg
            yp = _dot(mixed.astype(MM), w_ref[gi].astype(MM))
            y_ref[:, sl] = (yp * s_ref[:, sl]).astype(MM)

    return pl.pallas_call(
        body, name="pool_fwd", grid=(bl,),
        in_specs=[pl.BlockSpec((seq, d // 2), lambda b: (b, 6)),
                  pl.BlockSpec(w_grp.shape, lambda b: (0, 0, 0)),
                  pl.BlockSpec((1, d // 2), lambda b: (0, 0))],
        out_specs=pl.BlockSpec((seq, d // 2), lambda b: (b, 0)),
        out_shape=jax.ShapeDtypeStruct((t, d // 2), MM),
        compiler_params=_params(1))(proj, w_grp, scale)


def _pool_bwd(proj, dy, w_grp, scale, d, bl, seq):
    t = proj.shape[0]
    dg = d // 8

    def body(p_ref, dy_ref, w_ref, s_ref, dp_ref, dw_ref, ds_ref):
        @pl.when(pl.program_id(0) == 0)
        def _():
            dw_ref[...] = jnp.zeros_like(dw_ref)
            ds_ref[...] = jnp.zeros_like(ds_ref)

        row = lax.broadcasted_iota(jnp.int32, (seq, dg), 0)
        for gi, w in enumerate(POOL_WINDOWS):
            sl = slice(gi * dg, (gi + 1) * dg)
            pg = p_ref[:, sl].astype(F32)
            cnt = _window_count(row, w, seq)
            mixb = (_window_sum(pg, row, w, seq, 1) / cnt - pg).astype(MM)
            wgb = w_ref[gi].astype(MM)
            yp = _dot(mixb, wgb)
            dyg = dy_ref[:, sl]
            ds_ref[:, sl] += jnp.sum(dyg * yp, axis=0, keepdims=True)
            dyp = (dyg * s_ref[:, sl]).astype(MM)
            dmixed = _dot_nt(dyp, wgb)
            dw_ref[gi] += _dot_tn(mixb, dyp)
            dp_ref[:, sl] = (_window_sum(dmixed / cnt, row, w, seq, -1) - dmixed).astype(MM)

    half = pl.BlockSpec((seq, d // 2), lambda b: (b, 0))
    wspec = pl.BlockSpec(w_grp.shape, lambda b: (0, 0, 0))
    sspec = pl.BlockSpec((1, d // 2), lambda b: (0, 0))
    return pl.pallas_call(
        body, name="pool_bwd", grid=(bl,),
        in_specs=[pl.BlockSpec((seq, d // 2), lambda b: (b, 6)), half, wspec, sspec],
        out_specs=[half, wspec, sspec],
        out_shape=[jax.ShapeDtypeStruct((t, d // 2), MM), jax.ShapeDtypeStruct(w_grp.shape, F32),
                   jax.ShapeDtypeStruct((1, d // 2), F32)],
        compiler_params=_params(1))(proj, dy, w_grp, scale)


def _attn_probs(q, kk, dh):
    s = _dot_nt(q, kk) * (float(dh) ** -0.5)
    e = jnp.exp(s - jnp.max(s, axis=-1, keepdims=True))
    return e / jnp.sum(e, axis=-1, keepdims=True)


def _attn_fwd(proj, kv, d, bl, seq, mlen, *, tq=2048):
    t = proj.shape[0]
    dh = d // 8
    tq = min(tq, seq)
    nq = seq // tq

    def body(q_ref, k_ref, v_ref, o_ref):
        a = _attn_probs(q_ref[...].astype(MM), k_ref[...].astype(MM), dh)
        o_ref[...] = _dot(a.astype(MM), v_ref[...].astype(MM)).astype(MM)

    return pl.pallas_call(
        body, name="attn_fwd", grid=(bl, HEADS, nq),
        in_specs=[pl.BlockSpec((tq, dh), lambda b, h, i: (b * nq + i, 7 * HEADS + h)),
                  pl.BlockSpec((mlen, dh), lambda b, h, i: (b, h)),
                  pl.BlockSpec((mlen, dh), lambda b, h, i: (b, HEADS + h))],
        out_specs=pl.BlockSpec((tq, dh), lambda b, h, i: (b * nq + i, h)),
        out_shape=jax.ShapeDtypeStruct((t, d // 2), MM),
        compiler_params=_params(3))(proj, kv, kv)


def _attn_bwd(proj, kv, do, d, bl, seq, mlen, *, tq=2048):
    t = proj.shape[0]
    dh = d // 8
    tq = min(tq, seq)
    nq = seq // tq

    def body(q_ref, k_ref, v_ref, do_ref, dq_ref, dk_ref, dv_ref):
        @pl.when(pl.program_id(2) == 0)
        def _():
            dk_ref[...] = jnp.zeros_like(dk_ref)
            dv_ref[...] = jnp.zeros_like(dv_ref)

        q, kk, vv = q_ref[...].astype(MM), k_ref[...].astype(MM), v_ref[...].astype(MM)
        dov = do_ref[...].astype(MM)
        a = _attn_probs(q, kk, dh)
        dp = _dot_nt(dov, vv)
        ds = (a * (dp - jnp.sum(dp * a, axis=-1, keepdims=True)) * (float(dh) ** -0.5)).astype(MM)
        dq_ref[...] = _dot(ds, kk).astype(MM)
        dk_ref[...] += _dot_tn(ds, q)
        dv_ref[...] += _dot_tn(a.astype(MM), dov)

    qs = pl.BlockSpec((tq, dh), lambda b, h, i: (b * nq + i, h))
    ms = pl.BlockSpec((mlen, dh), lambda b, h, i: (b, h))
    return pl.pallas_call(
        body, name="attn_bwd", grid=(bl, HEADS, nq),
        in_specs=[pl.BlockSpec((tq, dh), lambda b, h, i: (b * nq + i, 7 * HEADS + h)), ms,
                  pl.BlockSpec((mlen, dh), lambda b, h, i: (b, HEADS + h)), qs],
        out_specs=[qs, ms, ms],
        out_shape=[jax.ShapeDtypeStruct((t, d // 2), MM), jax.ShapeDtypeStruct((bl * mlen, d // 2), F32),
                   jax.ShapeDtypeStruct((bl * mlen, d // 2), F32)],
        compiler_params=_params(3))(proj, kv, kv, do)


def _comm_call(name, body, arrays, out_shapes):
    n = len(arrays)
    hbm = pl.BlockSpec(memory_space=pl.ANY)
    return pl.pallas_call(
        body, name=name, out_shape=out_shapes, in_specs=[hbm] * n, out_specs=[hbm] * n,
        scratch_shapes=[pltpu.SemaphoreType.DMA((7 * n,)), pltpu.SemaphoreType.DMA((7 * n,)),
                        pltpu.SemaphoreType.DMA((n,))],
    )(*arrays)


def _all_gather(name, shards):
    n = len(shards)

    def body(*refs):
        x_refs, out_refs = refs[:n], refs[n:2 * n]
        send_sems, recv_sems, local_sems = refs[2 * n:]
        x, y, c = lax.axis_index("x"), lax.axis_index("y"), lax.axis_index("c")
        me, sibling = (x, y, c), (x, y, 1 - c)
        chips = [(1 - x, y), (x, 1 - y), (1 - x, 1 - y)]

        def copy(o, k, block, to, src=None):
            slot = out_refs[o].at[4 * block[0] + 2 * block[1] + block[2]]
            return pltpu.make_async_remote_copy(
                src_ref=slot if src is None else src, dst_ref=slot, send_sem=send_sems.at[7 * o + k],
                recv_sem=recv_sems.at[7 * o + k], device_id=to, device_id_type=MESH)

        locals_, remotes = [], []
        for o in range(n):
            mine = pltpu.make_async_copy(x_refs[o], out_refs[o].at[4 * x + 2 * y + c], local_sems.at[o])
            mine.start()
            locals_.append(mine)
            first = [copy(o, 0, me, sibling, src=x_refs[o])]
            first += [copy(o, 1 + j, me, (*chip, c), src=x_refs[o]) for j, chip in enumerate(chips)]
            for cp in first:
                cp.start()
            remotes += first
        for o in range(n):
            for j, chip in enumerate(chips):
                copy(o, 1 + j, (*chip, c), me).wait_recv()
                passed = copy(o, 4 + j, (*chip, c), sibling)
                passed.start()
                remotes.append(passed)
        for o in range(n):
            copy(o, 0, sibling, me).wait_recv()
            for j, chip in enumerate(chips):
                copy(o, 4 + j, (*chip, 1 - c), me).wait_recv()
        for cp in remotes:
            cp.wait_send()
        for mine in locals_:
            mine.wait()

    outs = [jax.ShapeDtypeStruct((N_DEV,) + s.shape, s.dtype) for s in shards]
    return _comm_call(name, body, shards, outs)


def _peer_of(k, x, y, c):
    peer = (1 - x if k & 4 else x, 1 - y if k & 2 else y, 1 - c if k & 1 else c)
    return peer, 4 * peer[0] + 2 * peer[1] + peer[2]


def _split_copies(scatter, srcs, lands, send_sems, recv_sems, arriving):
    x, y, c = lax.axis_index("x"), lax.axis_index("y"), lax.axis_index("c")
    me_idx = 4 * x + 2 * y + c
    copies = []
    for o, (src, land) in enumerate(zip(srcs, lands)):
        for k in range(1, N_DEV):
            peer, p_idx = _peer_of(k, x, y, c)
            mine = src.at[p_idx] if scatter else src
            sems = dict(send_sem=send_sems.at[7 * o + k - 1], recv_sem=recv_sems.at[7 * o + k - 1],
                        device_id=peer, device_id_type=MESH)
            slot = land.at[p_idx] if arriving else land.at[me_idx]
            copies.append(pltpu.make_async_remote_copy(src_ref=mine, dst_ref=slot, **sems))
    return copies


_HBM = pl.BlockSpec(memory_space=pltpu.HBM)
_SEM = pl.BlockSpec(memory_space=pltpu.SEMAPHORE)
_EFFECT = pltpu.SideEffectType.DATAFLOW_SIDE_EFFECTING


def _exchange_start(name, scatter, arrays, after=()):
    n = len(arrays)
    lands = [lax.empty(a.shape if scatter else (N_DEV,) + a.shape, a.dtype) for a in arrays]

    def body(*refs):
        srcs, lnds = refs[:n], refs[n:2 * n]
        send_sems, recv_sems = refs[2 * n + len(after)], refs[2 * n + len(after) + 1]
        token = refs[-1]
        for cp in _split_copies(scatter, srcs, lnds, send_sems, recv_sems, False):
            cp.start()
        token[...] = jnp.zeros_like(token)

    hbm_in = [pltpu.with_memory_space_constraint(a, pltpu.HBM) for a in list(arrays) + lands]
    res = pl.pallas_call(
        body, name=name,
        out_shape=(pltpu.SemaphoreType.DMA((7 * n,)), pltpu.SemaphoreType.DMA((7 * n,)),
                   *[pltpu.HBM(a.shape, a.dtype) for a in hbm_in], jax.ShapeDtypeStruct((8, 128), F32)),
        in_specs=[_HBM] * (2 * n) + [pl.BlockSpec(memory_space=pl.ANY)] * len(after),
        out_specs=(_SEM, _SEM, *[_HBM] * (2 * n), pl.BlockSpec(memory_space=pltpu.VMEM)),
        input_output_aliases={i: 2 + i for i in range(2 * n)},
        compiler_params=pltpu.CompilerParams(has_side_effects=_EFFECT),
    )(*hbm_in, *after)
    return res[0], res[1], list(res[2:2 + n]), list(res[2 + n:2 + 2 * n]), res[-1]


def _exchange_wait(name, scatter, started, after):
    send_sems, recv_sems, srcs, lands, _ = started
    n = len(srcs)

    def body(*refs):
        src_refs, lnd_refs = refs[:n], refs[n:2 * n]
        for cp in _split_copies(scatter, src_refs, lnd_refs, refs[2 * n], refs[2 * n + 1], False):
            cp.wait_send()
        for cp in _split_copies(scatter, src_refs, lnd_refs, refs[2 * n], refs[2 * n + 1], True):
            cp.wait_recv()

    res = pl.pallas_call(
        body, name=name, out_shape=tuple(pltpu.HBM(a.shape, a.dtype) for a in srcs + lands),
        in_specs=[_HBM] * (2 * n) + [_SEM, _SEM, pl.BlockSpec(memory_space=pl.ANY)],
        out_specs=tuple([_HBM] * (2 * n)), input_output_aliases={i: i for i in range(2 * n)},
        compiler_params=pltpu.CompilerParams(has_side_effects=_EFFECT),
    )(*srcs, *lands, send_sems, recv_sems, after)
    return list(res[n:]), list(res[:n])


def _adamw(name, parts, w, m, v, prev, layer, *, tr=256):
    _, a, b = w.shape
    tr = _tile(a, tr, 8)
    c1 = 1.0 - ADAM_B1 ** ADAM_STEP
    c2 = 1.0 - ADAM_B2 ** ADAM_STEP

    def body(p_ref, w_ref, m_ref, v_ref, _g, _d, _m, _v, g_out, d_out, m_out, v_out):
        g = p_ref[0].astype(F32)
        for s in range(1, N_DEV):
            g = g + p_ref[s].astype(F32)
        mn = ADAM_B1 * m_ref[...] + (1.0 - ADAM_B1) * g
        vn = ADAM_B2 * v_ref[...] + (1.0 - ADAM_B2) * (g * g)
        g_out[...] = g
        m_out[...] = mn
        v_out[...] = vn
        d_out[...] = -ADAM_LR * ((mn / c1) / (jnp.sqrt(vn / c2) + ADAM_EPS) + ADAM_WD * w_ref[...])

    slab = pl.BlockSpec((None, tr, b), lambda i: (layer, i, 0))
    whole = pl.BlockSpec(memory_space=pl.ANY)
    return pl.pallas_call(
        body, name=name, grid=(a // tr,),
        in_specs=[pl.BlockSpec((N_DEV, tr, b), lambda i: (0, i, 0)), slab, slab, slab] + [whole] * 4,
        out_specs=[slab] * 4, out_shape=[jax.ShapeDtypeStruct(w.shape, F32)] * 4,
        input_output_aliases={4: 0, 5: 1, 6: 2, 7: 3},
        compiler_params=_params(1))(parts, w, m, v, *prev)


_COL = ("w_in", "w_pool_o", "w_mem_o", "w_ff1")
_COL_IN_PLACE = ("w_in", "w_ff1")
_BIG =("w_in", "w_ret_o", "w_pool_o", "w_mem_kv", "w_mem_o", "w_out", "w_ff1", "w_ff2")
_SMALL = ("ret_decay_logit", "w_pool_grp", "pool_scale", "norm1_g", "norm2_g", "mem_norm_g", "final_norm_g")
_SMALL_MM = ("w_pool_grp",)
_SMALL_F32 = tuple(n for n in _SMALL if n not in _SMALL_MM)
_WEIGHTS = ("w_in", "ret_decay_logit", "w_ret_o", "w_pool_grp", "pool_scale", "w_pool_o", "w_mem_kv", "w_mem_o",
            "w_out", "w_ff1", "w_ff2", "norm1_g", "norm2_g", "mem_norm_g", "final_norm_g")


def _small_rows(size, d):
    return -(-size // (8 * d)) * 8


def _pack_small(ws, d, names):
    parts = []
    for n in names:
        flat = ws[n].reshape(-1)
        rows = _small_rows(flat.shape[0], d)
        parts.append(jnp.pad(flat, (0, rows * d - flat.shape[0])).reshape(rows, d))
    return jnp.concatenate(parts, axis=0)[None]


def _unpack_small(packed, like, d, names):
    out, off = {}, 0
    for n in names:
        rows = _small_rows(like[n].size, d)
        out[n] = packed[0, off:off + rows].reshape(-1)[:like[n].size].reshape(like[n].shape)
        off += rows
    return out


def kernel(x, mem, w_in, ret_decay_logit, w_ret_o, w_pool_grp, pool_scale, w_pool_o, w_mem_kv, w_mem_o, w_out, w_ff1, w_ff2, norm1_g, norm2_g, mem_norm_g, final_norm_g, loss_target, m_w_in, m_ret_decay_logit, m_w_ret_o, m_w_pool_grp, m_pool_scale, m_w_pool_o, m_w_mem_kv, m_w_mem_o, m_w_out, m_w_ff1, m_w_ff2, m_norm1_g, m_norm2_g, m_mem_norm_g, m_final_norm_g, v_w_in, v_ret_decay_logit, v_w_ret_o, v_w_pool_grp, v_pool_scale, v_w_pool_o, v_w_mem_kv, v_w_mem_o, v_w_out, v_w_ff1, v_w_ff2, v_norm1_g, v_norm2_g, v_mem_norm_g, v_final_norm_g):
    w = dict(w_in=w_in, ret_decay_logit=ret_decay_logit, w_ret_o=w_ret_o, w_pool_grp=w_pool_grp,
             pool_scale=pool_scale, w_pool_o=w_pool_o, w_mem_kv=w_mem_kv, w_mem_o=w_mem_o, w_out=w_out,
             w_ff1=w_ff1, w_ff2=w_ff2, norm1_g=norm1_g, norm2_g=norm2_g, mem_norm_g=mem_norm_g,
             final_norm_g=final_norm_g)
    mom = dict(w_in=m_w_in, ret_decay_logit=m_ret_decay_logit, w_ret_o=m_w_ret_o, w_pool_grp=m_w_pool_grp,
               pool_scale=m_pool_scale, w_pool_o=m_w_pool_o, w_mem_kv=m_w_mem_kv, w_mem_o=m_w_mem_o,
               w_out=m_w_out, w_ff1=m_w_ff1, w_ff2=m_w_ff2, norm1_g=m_norm1_g, norm2_g=m_norm2_g,
               mem_norm_g=m_mem_norm_g, final_norm_g=m_final_norm_g)
    vel = dict(w_in=v_w_in, ret_decay_logit=v_ret_decay_logit, w_ret_o=v_w_ret_o, w_pool_grp=v_w_pool_grp,
               pool_scale=v_pool_scale, w_pool_o=v_w_pool_o, w_mem_kv=v_w_mem_kv, w_mem_o=v_w_mem_o,
               w_out=v_w_out, w_ff1=v_w_ff1, w_ff2=v_w_ff2, norm1_g=v_norm1_g, norm2_g=v_norm2_g,
               mem_norm_g=v_mem_norm_g, final_norm_g=v_final_norm_g)

    bl, seq, d = x.shape
    mlen = mem.shape[1]
    depth = w_in.shape[0]
    t = bl * seq
    dk = d // 8

    me_idx = 4 * lax.axis_index("x") + 2 * lax.axis_index("y") + lax.axis_index("c")

    def natural(n, g):
        if n in _COL_IN_PLACE:
            return g
        if n in _COL:
            return jnp.transpose(g, (1, 0, 2)).reshape(g.shape[1], -1)
        return g.reshape(-1, g.shape[-1])

    def finish_gather(name, names, started, after):
        got, mine = _exchange_wait(name, False, started, after)
        return {n: natural(n, lax.dynamic_update_slice(g, sh[None], (me_idx, 0, 0)))
                for n, g, sh in zip(names, got, mine)}

    shards = [{n: w[n][l].astype(MM) for n in _BIG} for l in range(depth)]
    rest = _BIG[1:]
    (w_in0,) = _all_gather("gather_w_in", [shards[0][_BIG[0]]])
    full = [{_BIG[0]: w_in0}]
    pending = _exchange_start("gather_start_0", False, [shards[0][n] for n in rest], after=[w_in0])
    first_tokens = (pending[4],)
    pending_next = None
    if depth > 1:
        pending_next = _exchange_start("gather_start_1", False, [shards[1][n] for n in _BIG], after=[pending[4]])
        first_tokens += (pending_next[4],)

    inv = ROPE_BASE ** (-jnp.arange(0, dk, 2, dtype=F32) / dk)
    ang = jnp.arange(seq, dtype=F32)[:, None] * inv[None, :]
    cos2 = jnp.concatenate([jnp.cos(ang), jnp.cos(ang)], axis=-1)
    sin2 = jnp.concatenate([-jnp.sin(ang), jnp.sin(ang)], axis=-1)
    log_g = jax.nn.log_sigmoid(ret_decay_logit)
    x2 = x.reshape(t, d)
    mem2 = mem.reshape(bl * mlen, d)
    gmem = mem_norm_g.reshape(1, d)

    def merge(a_r, y_p, o_a, g_r, g_p, g_m, w_r, w_p, w_m):
        f = lambda z: z.astype(F32)
        o_r, o_p, o_m = _dot(a_r, w_r), _dot(y_p, w_p), _dot(o_a, w_m)
        return _sigmoid(f(g_r)) * o_r + _sigmoid(f(g_p)) * o_p + _sigmoid(f(g_m)) * o_m, o_r, o_p, o_m

    def relu2(u):
        r = jnp.maximum(u.astype(MM), 0.0)
        return r * r

    def ident(a):
        return a

    saved = []
    xc = x2
    for l in range(depth):
        s = dict(x_in=xc)
        started_now = ()
        if l > 0:
            full.append(finish_gather(f"gather_wait_{l}", _BIG, pending, xc))
            if l + 1 < depth:
                pending = _exchange_start(f"gather_start_{l + 1}", False, [shards[l + 1][n] for n in _BIG],
                                          after=[full[l]["w_in"]])
                started_now = (pending[4],)
        fw = full[l]
        g1 = norm1_g[l].reshape(1, d)
        g2 = norm2_g[l].reshape(1, d)
        s["proj"], s["h1"] = _pmm("proj", _rms_prologue, [(xc, d, 0)], [g1], fw["w_in"], w_mode="col",
                                  tm=2048, tn=1024, save_a=True, out_dtypes=(MM,),
                                  after=started_now if l > 0 else first_tokens)
        proj = s["proj"]
        s["qr"], s["kr"], s["vb"] = _ret_pre(proj, cos2, sin2, d, seq)
        s["o_raw"], s["a_ret"], s["sf"], s["sb"] = _ret_core_fwd(s["qr"], s["kr"], s["vb"], proj, log_g[l],
                                                                 d, bl, seq)
        s["y"] = _pool_fwd(proj, w_pool_grp[l], pool_scale[l].reshape(1, -1), d, bl, seq)
        started_now = ()
        if l == 0:
            fw.update(finish_gather("gather_wait_0", rest, pending, s["a_ret"]))
            pending = pending_next
        s["kv"], s["memn"] = _pmm("mem_kv", _rms_prologue, [(mem2, d, 0)], [gmem], fw["w_mem_kv"],
                                  tm=512, tn=512, save_a=True, after=started_now)
        s["o_att"] = _attn_fwd(proj, s["kv"], d, bl, seq, mlen)
        s["x_mid"], s["merged"], s["o_ret"], s["o_pool"], s["o_mem"] = _pmm(
            "merge_out", merge,
            [(s["a_ret"], d, 0), (s["y"], d // 2, 0), (s["o_att"], d // 2, 0), (proj, d, 4), (proj, d, 5), (proj, d, 6)],
            [fw["w_ret_o"], fw["w_pool_o"], fw["w_mem_o"]], fw["w_out"], tm=512, tn=1024, residual=xc, save_a=True,
            extra_outs=[(d, MM)] * 3)
        s["u"], s["h2"] = _pmm("ff1", _rms_prologue, [(s["x_mid"], d, 0)], [g2], fw["w_ff1"], w_mode="col",
                               tm=512, tn=4096, save_a=True, out_dtypes=(MM,))
        (xc,) = _pmm("ff2", relu2, [(s["u"], s["u"].shape[1], 0)], [], fw["w_ff2"],
                     tm=512, tn=1024, residual=s["x_mid"])
        saved.append(s)

    dxc, g_final, loss_part = _loss_head(xc, loss_target.reshape(t, d), final_norm_g.reshape(1, d))
    loss = lax.psum(loss_part[0, 0], ("x", "y", "c"))

    small_names = ("w_pool_grp", "pool_scale", "norm1_g", "norm2_g", "ret_decay_logit")
    grads = {n: [None] * depth for n in small_names}
    group_a = ("w_ff1", "w_ff2")
    group_b = tuple(n for n in _BIG if n not in group_a)
    scatters = {}
    dmemn = jnp.zeros((bl * mlen, d), F32)

    def relu2_bwd(acc, u):
        return (acc * (2.0 * jnp.maximum(u.astype(F32), 0.0)),)

    def gates_bwd(acc, g_r, g_p, g_m, o_r, o_p, o_m, w_r, w_p, w_m):
        d_os, d_gs, backs = [], [], []
        for gz, oz, wz in ((g_r, o_r, w_r), (g_p, o_p, w_p), (g_m, o_m, w_m)):
            sg = _sigmoid(gz.astype(F32))
            d_o = (acc * sg).astype(MM)
            d_os.append(d_o)
            d_gs.append(acc * oz.astype(F32) * (sg * (1.0 - sg)))
            backs.append(_dot_nt(d_o, wz))
        return tuple(d_os + d_gs + backs)

    def to_send(n, g):
        a, b = w[n].shape[1:]
        if n in _COL_IN_PLACE:
            return g
        if n in _COL:
            return jnp.transpose(g.reshape(a, N_DEV, b), (1, 0, 2))
        return g.reshape(N_DEV, a, b)

    for l in reversed(range(depth)):
        s = saved[l]
        fw = full[l]
        proj = s["proj"]
        g1 = norm1_g[l].reshape(1, d)
        g2 = norm2_g[l].reshape(1, d)
        dw = {}
        (du,) = _pmm("ff2_bwd", ident, [(dxc, d, 0)], [], fw["w_ff2"], w_mode="nt", tm=512, tn=4096,
                     epilogue=relu2_bwd, epi_ins=[(s["u"], 0)], out_dtypes=(MM,))
        dw["w_ff2"] = _tnmm("dw_ff2", s["u"], dxc, a_fn=relu2)
        dw["w_ff1"] = _tnmm("dw_ff1", s["h2"], du, col_shards=True)
        scatters[l, "a"] = _exchange_start(f"scatter_start_a{l}", True, [to_send(n, dw[n]) for n in group_a])
        dmid, grads["norm2_g"][l] = _mm_rms_bwd("ff1_norm2_bwd", du, fw["w_ff1"], s["x_mid"], g2, dxc, tm=512)
        d_oret, d_opool, d_omem, dgr, dgp, dgm, da_ret, dy, do_att = _pmm(
            "out_bwd", ident, [(dmid, d, 0)], [], fw["w_out"], w_mode="nt", tm=256, tn=d, epilogue=gates_bwd,
            epi_ins=[(proj, 4 * d), (proj, 5 * d), (proj, 6 * d), (s["o_ret"], 0), (s["o_pool"], 0), (s["o_mem"], 0)],
            epi_full=[fw["w_ret_o"], fw["w_pool_o"], fw["w_mem_o"]], out_dtypes=(MM,) * 6 + (F32,) * 3,
            out_widths=[d] * 7 + [d // 2] * 2, after=(scatters[l, "a"][4],))
        dw["w_out"] = _tnmm("dw_out", s["merged"], dmid)
        dw["w_ret_o"] = _tnmm("dw_ret_o", s["a_ret"], d_oret)
        dw["w_pool_o"] = _tnmm("dw_pool_o", s["y"], d_opool)
        dw["w_mem_o"] = _tnmm("dw_mem_o", s["o_att"], d_omem)
        dg_ret, do_ret = _ret_post_bwd(da_ret, proj, s["o_raw"], d)
        dq, dkk, dvv, dlf, dlb = _ret_core_bwd(s["qr"], s["kr"], s["vb"], do_ret, s["sf"], s["sb"], cos2, sin2,
                                               log_g[l], d, bl, seq)
        dl = jnp.stack([dlf[:, 0, 0].reshape(bl, HEADS).sum(0), dlb[:, 0, 0].reshape(bl, HEADS).sum(0)])
        grads["ret_decay_logit"][l] = dl * jax.nn.sigmoid(-ret_decay_logit[l])
        dp, grads["w_pool_grp"][l], dscale = _pool_bwd(proj, dy, w_pool_grp[l], pool_scale[l].reshape(1, -1),
                                                       d, bl, seq)
        grads["pool_scale"][l] = dscale.reshape(-1)
        dqm, dmk, dmv = _attn_bwd(proj, s["kv"], do_att, d, bl, seq, mlen)
        dkv = jnp.concatenate([dmk, dmv], axis=-1).astype(MM)
        dw["w_mem_kv"] = _tnmm("dw_mem_kv", s["memn"], dkv)
        (dmemn,) = _pmm("mem_kv_bwd", None, [(dkv, d, 0)], [], fw["w_mem_kv"], w_mode="nt", tm=512, tn=512,
                        residual=dmemn)
        dproj = [dq, dkk, dvv, dg_ret, dp, dqm, dgr, dgp, dgm]
        dw["w_in"] = _tnmm("dw_in", s["h1"], dproj, col_shards=True, tm=512, tk=512)
        scatters[l, "b"] = _exchange_start(f"scatter_start_b{l}", True, [to_send(n, dw[n]) for n in group_b])
        dxc, grads["norm1_g"][l] = _mm_rms_bwd("proj_norm1_bwd", dproj, fw["w_in"], s["x_in"], g1, dmid, tm=256,
                                               after=(scatters[l, "b"][4],))

    _, g_memn = _rms_bwd("mem_norm_bwd", dmemn, mem2, gmem, None)
    grad_x = dxc.reshape(bl, seq, d)

    small_g = dict(ret_decay_logit=jnp.stack(grads["ret_decay_logit"]), w_pool_grp=jnp.stack(grads["w_pool_grp"]),
                   pool_scale=jnp.stack(grads["pool_scale"]),
                   norm1_g=jnp.concatenate(grads["norm1_g"], axis=0), norm2_g=jnp.concatenate(grads["norm2_g"], axis=0),
                   mem_norm_g=g_memn.reshape(-1), final_norm_g=g_final.reshape(-1))
    small_started = _exchange_start("gather_small_start", False,
                                    [_pack_small(small_g, d, _SMALL_MM)[0].astype(MM),
                                     _pack_small(small_g, d, _SMALL_F32)[0]])

    big = {n: [lax.empty(w[n].shape, F32) for _ in range(4)] for n in _BIG}

    def update(l, grp, names, after):
        recv, sent = _exchange_wait(f"scatter_wait_{grp}{l}", True, scatters[l, grp], after)
        for n, r, snt in zip(names, recv, sent):
            own = lax.dynamic_slice_in_dim(snt, me_idx, 1, axis=0)
            parts = lax.dynamic_update_slice(r, own, (me_idx, 0, 0))
            big[n] = _adamw("adamw_" + n, parts, w[n], mom[n], vel[n], big[n], l)
        return big[names[-1]][0]

    after = dxc
    for l in reversed(range(1, depth)):
        for grp, names in (("a", group_a), ("b", group_b)):
            after = update(l, grp, names, after)
    after = update(0, "a", group_a, after)
    small_lands, small_mine = _exchange_wait("gather_small_wait", False, small_started, after)
    small = [{} for _ in range(4)]
    for names, land, mine in zip((_SMALL_MM, _SMALL_F32), small_lands, small_mine):
        parts = lax.dynamic_update_slice(land, mine[None], (me_idx, 0, 0))
        w_small = _pack_small(w, d, names)
        res = _adamw("adamw_small", parts, w_small, _pack_small(mom, d, names), _pack_small(vel, d, names),
                     [lax.empty(w_small.shape, F32) for _ in range(4)], 0)
        after = res[0]
        for k in range(4):
            small[k].update(_unpack_small(res[k], w, d, names))
    update(0, "b", group_b, after)

    outs = [loss, grad_x]
    for k in range(4):
        outs += [big[n][k] if n in _BIG else small[k][n] for n in _WEIGHTS]
    return tuple(outs)
```

```python
import jax
import jax.numpy as jnp
from jax import lax
from jax.experimental import pallas as pl
from jax.experimental.pallas import tpu as pltpu

F32 = jnp.float32
MM = jnp.bfloat16
N_DEV = 8
HEADS = 4
POOL_WINDOWS = (2, 4, 8, 16)
EPS = 1e-6
ROPE_BASE = 10000.0
ADAM_LR, ADAM_B1, ADAM_B2, ADAM_EPS, ADAM_WD, ADAM_STEP = 0.001, 0.9, 0.999, 1e-08, 0.01, 10
V7X_VMEM_LIMIT = 56 * 1024 * 1024
MESH = pl.DeviceIdType.MESH


def _params(n_axes):
    return pltpu.CompilerParams(dimension_semantics=("arbitrary",) * n_axes,
                                vmem_limit_bytes=V7X_VMEM_LIMIT)


def _tile(n, pref, align=128):
    cands = [c for c in range(align, min(pref, n) + 1, align) if n % c == 0]
    return max(cands) if cands else n


def _sigmoid(z):
    return 0.5 * jnp.tanh(0.5 * z) + 0.5


def _dot(a, b):
    return jnp.dot(a, b, preferred_element_type=F32)


def _dot_nt(a, b):
    return lax.dot_general(a, b, (((1,), (1,)), ((), ())), preferred_element_type=F32)


def _dot_tn(a, b):
    return lax.dot_general(a, b, (((0,), (0,)), ((), ())), preferred_element_type=F32)


def _pmm(name, prologue, row_ins, vec_ins, w, *, tm, tn, w_mode="nn", residual=None, save_a=False,
         epilogue=None, epi_ins=(), out_dtypes=(F32,), after=(), extra_outs=(), epi_full=(), out_widths=None):
    m = row_ins[0][0].shape[0]
    if w_mode == "nn":
        k, n = w.shape
        tn = _tile(n, tn)
        w_spec = pl.BlockSpec((k, tn), lambda i, j: (0, j))
    else:
        n, k = w.shape
        tn = _tile(n, tn)
        w_spec = pl.BlockSpec((tn, k), lambda i, j: (j, 0))
    tm = _tile(m, tm, 8)
    n_row, n_vec, n_epi, n_out = len(row_ins), len(vec_ins), len(epi_ins), len(out_dtypes)
    has_res = residual is not None
    use_scr = prologue is not None
    out_widths = [n] * n_out if out_widths is None else list(out_widths)
    assert all(wd == n for wd in out_widths) or tn == n

    def body(*refs):
        row_refs = refs[:n_row]
        p = n_row
        vec_refs = refs[p:p + n_vec]
        p += n_vec
        w_ref = refs[p]
        p += 1
        res_ref = refs[p] if has_res else None
        p += int(has_res)
        epi_refs = refs[p:p + n_epi + len(epi_full)]
        p += n_epi + len(epi_full) + len(after)
        out_refs = refs[p:p + n_out]
        p += n_out
        a_out = refs[p] if save_a else None
        p += int(save_a)
        extra_refs = refs[p:p + len(extra_outs)]
        p += len(extra_outs)
        if use_scr:
            a_src = refs[p]

            @pl.when(pl.program_id(1) == 0)
            def _():
                made = prologue(*[r[...] for r in row_refs], *[v[...] for v in vec_refs])
                made = made if isinstance(made, tuple) else (made,)
                a = made[0].astype(MM)
                a_src[...] = a
                if save_a:
                    a_out[...] = a
                for e_ref, e in zip(extra_refs, made[1:]):
                    e_ref[...] = e.astype(e_ref.dtype)
        else:
            a_src = row_refs[0]
        if w_mode == "nt":
            acc = _dot_nt(a_src[...], w_ref[...])
        else:
            acc = _dot(a_src[...], w_ref[...])
        if has_res:
            acc = acc + res_ref[...]
        outs = epilogue(acc, *[e[...] for e in epi_refs]) if epilogue is not None else (acc,)
        for o_ref, o in zip(out_refs, outs):
            o_ref[...] = o.astype(o_ref.dtype)

    in_specs = [pl.BlockSpec((tm, wd), lambda i, j, cb=cb: (i, cb)) for (_, wd, cb) in row_ins]
    in_specs += [pl.BlockSpec(v.shape, lambda i, j: (0, 0)) for v in vec_ins]
    in_specs += [w_spec]
    args = [r[0] for r in row_ins] + list(vec_ins) + [w]
    if has_res:
        in_specs.append(pl.BlockSpec((tm, tn), lambda i, j: (i, j)))
        args.append(residual)
    for (arr, off) in epi_ins:
        assert off % tn == 0
        in_specs.append(pl.BlockSpec((tm, tn), lambda i, j, ob=off // tn: (i, ob + j)))
        args.append(arr)
    in_specs += [pl.BlockSpec(v.shape, lambda i, j: (0, 0)) for v in epi_full]
    args += list(epi_full)
    n_after = len(after)
    in_specs += [pl.BlockSpec(memory_space=pl.ANY)] * n_after
    args += list(after)
    out_specs = [pl.BlockSpec((tm, tn if wd == n else wd), lambda i, j: (i, j)) for wd in out_widths]
    out_shape = [jax.ShapeDtypeStruct((m, wd), dt) for wd, dt in zip(out_widths, out_dtypes)]
    if save_a:
        out_specs.append(pl.BlockSpec((tm, k), lambda i, j: (i, 0)))
        out_shape.append(jax.ShapeDtypeStruct((m, k), MM))
    for wd, dt in extra_outs:
        out_specs.append(pl.BlockSpec((tm, wd), lambda i, j: (i, 0)))
        out_shape.append(jax.ShapeDtypeStruct((m, wd), dt))
    scratch = [pltpu.VMEM((tm, k), MM)] if use_scr else []
    return pl.pallas_call(body, name=name, grid=(m // tm, n // tn), in_specs=in_specs,
                          out_specs=out_specs, out_shape=out_shape, scratch_shapes=scratch,
                          compiler_params=_params(2))(*args)


def _tnmm(name, a, b, *, tm=1024, tn=1024, tk=1024, col_shards=False, a_fn=None):
    t, m = a.shape
    pieces = list(b) if isinstance(b, (list, tuple)) else [b]
    widths = [p.shape[1] for p in pieces]
    offs = [sum(widths[:p]) for p in range(len(pieces))]
    n = sum(widths)
    tm, tk = _tile(m, tm), _tile(t, tk, 8)
    per_tile = 1
    if col_shards:
        wb = n // N_DEV
        if len(pieces) > 1:
            tn = n
        while 2 * per_tile * wb <= tn and 2 * per_tile <= N_DEV:
            per_tile *= 2
        tn = per_tile * wb
        out_spec = pl.BlockSpec((per_tile, tm, wb), lambda i, j, kk: (j, i, 0))
        out_shape = jax.ShapeDtypeStruct((N_DEV, m, wb), MM)
    else:
        tn = _tile(n, tn)
        out_spec = pl.BlockSpec((tm, tn), lambda i, j, kk: (i, j))
        out_shape = jax.ShapeDtypeStruct((m, n), MM)
    nk = t // tk

    assert len(pieces) == 1 or tn == n

    def body(a_ref, *rest):
        b_refs, (o_ref, acc) = rest[:len(pieces)], rest[len(pieces):]
        kk = pl.program_id(2)

        @pl.when(kk == 0)
        def _():
            acc[...] = jnp.zeros_like(acc)

        av = (a_ref[...] if a_fn is None else a_fn(a_ref[...])).astype(MM)
        if len(pieces) == 1:
            acc[...] += _dot_tn(av, b_refs[0][...].astype(MM))
        else:
            for b_ref, off, wd in zip(b_refs, offs, widths):
                acc[:, off:off + wd] += _dot_tn(av, b_ref[...].astype(MM))

        @pl.when(kk == nk - 1)
        def _():
            if col_shards:
                for sh in range(per_tile):
                    o_ref[sh] = acc[:, sh * wb:(sh + 1) * wb].astype(o_ref.dtype)
            else:
                o_ref[...] = acc[...].astype(o_ref.dtype)

    return pl.pallas_call(
        body, name=name, grid=(m // tm, n // tn, nk),
        in_specs=[pl.BlockSpec((tk, tm), lambda i, j, kk: (kk, i))]
        + [pl.BlockSpec((tk, tn if len(pieces) == 1 else wd), lambda i, j, kk: (kk, j)) for wd in widths],
        out_specs=out_spec, out_shape=out_shape,
        scratch_shapes=[pltpu.VMEM((tm, tn), F32)],
        compiler_params=_params(3))(a, *pieces)


def _rms_prologue(x, g):
    r = lax.rsqrt(jnp.mean(x * x, axis=-1, keepdims=True) + EPS)
    return x * r * g


def _rms_bwd_rows(dh, x, g):
    d = x.shape[-1]
    r = lax.rsqrt(jnp.mean(x * x, axis=-1, keepdims=True) + EPS)
    xh = x * r
    dxh = dh * g
    dx = r * (dxh - xh * (jnp.sum(dxh * xh, axis=-1, keepdims=True) / d))
    dg = jnp.sum(dh * xh, axis=0, keepdims=True)
    return dx, dg


def _rms_bwd(name, dh, x, g, dres, *, tm=512):
    m, d = x.shape
    tm = min(tm, m)
    has_res = dres is not None

    def body(*refs):
        if has_res:
            dh_ref, x_ref, g_ref, r_ref, dx_ref, dg_ref = refs
        else:
            dh_ref, x_ref, g_ref, dx_ref, dg_ref = refs
        dx, dg = _rms_bwd_rows(dh_ref[...], x_ref[...], g_ref[...])
        if has_res:
            dx = dx + r_ref[...]
        dx_ref[...] = dx

        @pl.when(pl.program_id(0) == 0)
        def _():
            dg_ref[...] = jnp.zeros_like(dg_ref)

        dg_ref[...] += dg

    row = pl.BlockSpec((tm, d), lambda i: (i, 0))
    vec = pl.BlockSpec((1, d), lambda i: (0, 0))
    in_specs = [row, row, vec] + ([row] if has_res else [])
    args = [dh, x, g] + ([dres] if has_res else [])
    return pl.pallas_call(body, name=name, grid=(m // tm,), in_specs=in_specs, out_specs=[row, vec],
                          out_shape=[jax.ShapeDtypeStruct((m, d), F32), jax.ShapeDtypeStruct((1, d), F32)],
                          compiler_params=_params(1))(*args)


def _mm_rms_bwd(name, a, w, x, g, dres, *, tm, after=()):
    pieces = list(a) if isinstance(a, (list, tuple)) else [a]
    widths = [p.shape[1] for p in pieces]
    m = pieces[0].shape[0]
    d = w.shape[0]
    tm = _tile(m, tm, 8)
    n_a = len(pieces)

    def body(*refs):
        a_refs = refs[:n_a]
        w_ref, x_ref, g_ref, r_ref = refs[n_a:n_a + 4]
        dx_ref, dg_ref = refs[n_a + 4 + len(after):]

        av = a_refs[0][...] if n_a == 1 else jnp.concatenate([a_ref[...] for a_ref in a_refs], axis=1)
        dh = _dot_nt(av, w_ref[...])
        dx, dg = _rms_bwd_rows(dh, x_ref[...], g_ref[...])
        dx_ref[...] = dx + r_ref[...]

        @pl.when(pl.program_id(0) == 0)
        def _():
            dg_ref[...] = jnp.zeros_like(dg_ref)

        dg_ref[...] += dg

    row = pl.BlockSpec((tm, d), lambda i: (i, 0))
    vec = pl.BlockSpec((1, d), lambda i: (0, 0))
    return pl.pallas_call(
        body, name=name, grid=(m // tm,),
        in_specs=[pl.BlockSpec((tm, wd), lambda i: (i, 0)) for wd in widths]
        + [pl.BlockSpec(w.shape, lambda i: (0, 0)), row, vec, row]
        + [pl.BlockSpec(memory_space=pl.ANY)] * len(after),
        out_specs=[row, vec],
        out_shape=[jax.ShapeDtypeStruct((m, d), F32), jax.ShapeDtypeStruct((1, d), F32)],
        compiler_params=_params(1))(*pieces, w, x, g, dres, *after)


def _loss_head(x, target, g, *, tm=256):
    m, d = x.shape
    tm = min(tm, m)

    def body(x_ref, t_ref, g_ref, dx_ref, dg_ref, loss_ref):
        xv, gv = x_ref[...], g_ref[...]
        y = _rms_prologue(xv, gv)
        err = y - t_ref[...]
        part = 0.5 * jnp.sum(jnp.sum(err * err, axis=-1, keepdims=True) / d)
        dx, dg = _rms_bwd_rows(err / d, xv, gv)
        dx_ref[...] = dx

        @pl.when(pl.program_id(0) == 0)
        def _():
            dg_ref[...] = jnp.zeros_like(dg_ref)
            loss_ref[...] = jnp.zeros_like(loss_ref)

        dg_ref[...] += dg
        loss_ref[...] += jnp.full(loss_ref.shape, part, F32)

    row = pl.BlockSpec((tm, d), lambda i: (i, 0))
    vec = pl.BlockSpec((1, d), lambda i: (0, 0))
    lspec = pl.BlockSpec((1, 128), lambda i: (0, 0))
    return pl.pallas_call(body, name="loss_head", grid=(m // tm,), in_specs=[row, row, vec],
                          out_specs=[row, vec, lspec],
                          out_shape=[jax.ShapeDtypeStruct((m, d), F32), jax.ShapeDtypeStruct((1, d), F32),
                                     jax.ShapeDtypeStruct((1, 128), F32)],
                          compiler_params=_params(1))(x, target, g)


def _rot(xv, cos2, sin2, half):
    return xv * cos2 + pltpu.roll(xv, half, 1) * sin2


def _rot_t(dv, cos2, sin2, half):
    return dv * cos2 + pltpu.roll(dv * sin2, half, 1)


def _ret_pre(proj, cos2, sin2, d, seq, *, ts=512):
    t = proj.shape[0]
    ts = min(ts, seq)
    dk = d // 8
    ns = seq // ts
    scale = float(dk) ** -0.5

    def body(q_ref, k_ref, v_ref, c_ref, s_ref, qo, ko, vo):
        c, s = c_ref[...], s_ref[...]
        for h in range(HEADS):
            sl = slice(h * dk, (h + 1) * dk)
            qo[:, sl] = _rot(q_ref[:, sl].astype(F32), c, s, dk // 2).astype(MM)
            ko[:, sl] = (_rot(k_ref[:, sl].astype(F32), c, s, dk // 2) * scale).astype(MM)
        vo[...] = v_ref[...].astype(MM)

    half = pl.BlockSpec((ts, d // 2), lambda i: (i, 0))
    tab = pl.BlockSpec((ts, dk), lambda i: (i % ns, 0))
    return pl.pallas_call(
        body, name="ret_pre", grid=(t // ts,),
        in_specs=[half, pl.BlockSpec((ts, d // 2), lambda i: (i, 1)), pl.BlockSpec((ts, d), lambda i: (i, 1)),
                  tab, tab],
        out_specs=[half, half, pl.BlockSpec((ts, d), lambda i: (i, 0))],
        out_shape=[jax.ShapeDtypeStruct((t, d // 2), MM), jax.ShapeDtypeStruct((t, d // 2), MM),
                   jax.ShapeDtypeStruct((t, d), MM)],
        compiler_params=_params(1))(proj, proj, proj, cos2, sin2)


def _ret_consts(lg_ref, h, t, dk):
    lf, lb = lg_ref[0, h], lg_ref[1, h]
    ab = (lax.broadcasted_iota(jnp.int32, (t, t), 0) - lax.broadcasted_iota(jnp.int32, (t, t), 1)).astype(F32)
    dmat = jnp.exp(jnp.where(ab >= 0, lf * ab, -lb * ab))
    up = lax.broadcasted_iota(jnp.int32, (t, dk), 0).astype(F32) + 1.0
    down = float(t) - up
    one = jnp.ones((1, 1), F32)
    return dict(ab=ab, dmat=dmat, xi_f=jnp.exp(lf * up), zeta_f=jnp.exp(lf * down), xi_b=jnp.exp(lb * up),
                zeta_b=jnp.exp(lb * down), up=up[:, 0:1], down=down[:, 0:1],
                cf=jnp.exp(one * (lf * t)), cb=jnp.exp(one * (lb * t)))


def _scaled(xv, rows):
    return (xv.astype(F32) * rows).astype(MM)


def _ret_core_fwd(qr, kr, vb, proj, lg, d, bl, seq, *, tc=256):
    t = qr.shape[0]
    dk, dv = d // 8, d // 4
    tc = min(tc, seq)
    nc = seq // tc

    def body(lg_ref, q_ref, k_ref, v_ref, g_ref, o_ref, a_ref, sf_ref, sb_ref):
        c = _ret_consts(lg_ref, pl.program_id(1), tc, dk)

        def rows_of(i):
            return pl.ds(pl.multiple_of(i * tc, tc), tc)

        def fwd_step(i, sf):
            rows = rows_of(i)
            sf_ref[i] = sf
            q, kk, v = q_ref[rows, :], k_ref[rows, :], v_ref[rows, :]
            p = (_dot_nt(q, kk) * c["dmat"]).astype(MM)
            o_ref[rows, :] = _dot(p, v) + _dot(_scaled(q, c["xi_f"]), sf.astype(MM))
            return sf * c["cf"] + _dot_tn(_scaled(kk, c["zeta_f"]), v)

        lax.fori_loop(0, nc, fwd_step, jnp.zeros((dk, dv), F32))

        def bwd_step(ii, sb):
            rows = rows_of(nc - 1 - ii)
            sb_ref[nc - 1 - ii] = sb
            q, kk, v = q_ref[rows, :], k_ref[rows, :], v_ref[rows, :]
            o_ref[rows, :] += _dot(_scaled(q, c["zeta_b"]), sb.astype(MM))
            return sb * c["cb"] + _dot_tn(_scaled(kk, c["xi_b"]), v)

        lax.fori_loop(0, nc, bwd_step, jnp.zeros((dk, dv), F32))

        def post(i, carry):
            rows = rows_of(i)
            o = o_ref[rows, :]
            oc = o - jnp.mean(o, axis=-1, keepdims=True)
            on = oc * lax.rsqrt(jnp.mean(oc * oc, axis=-1, keepdims=True) + EPS)
            g = g_ref[rows, :].astype(F32)
            a_ref[rows, :] = (on * (g * _sigmoid(g))).astype(MM)
            return carry

        lax.fori_loop(0, nc, post, 0)

    qk = pl.BlockSpec((seq, dk), lambda b, h: (b, h))
    vv = pl.BlockSpec((seq, dv), lambda b, h: (b, h))
    states = pl.BlockSpec((None, nc, dk, dv), lambda b, h: (b * HEADS + h, 0, 0, 0))
    return pl.pallas_call(
        body, name="ret_core_fwd", grid=(bl, HEADS),
        in_specs=[pl.BlockSpec(memory_space=pltpu.SMEM), qk, qk, vv,
                  pl.BlockSpec((seq, dv), lambda b, h: (b, 2 * HEADS + h))],
        out_specs=[vv, vv, states, states],
        out_shape=[jax.ShapeDtypeStruct((t, d), F32), jax.ShapeDtypeStruct((t, d), MM),
                   jax.ShapeDtypeStruct((bl * HEADS, nc, dk, dv), F32),
                   jax.ShapeDtypeStruct((bl * HEADS, nc, dk, dv), F32)],
        compiler_params=_params(2))(lg, qr, kr, vb, proj)


def _ret_post_bwd(da, proj, o_raw, d, *, ts=2048):
    t = da.shape[0]
    dv = d // 4
    ts = min(ts, t)

    def body(da_ref, g_ref, o_ref, dg_ref, do_ref):
        o, g, dav = o_ref[...], g_ref[...].astype(F32), da_ref[...]
        mu = jnp.mean(o, axis=-1, keepdims=True)
        oc = o - mu
        r = lax.rsqrt(jnp.mean(oc * oc, axis=-1, keepdims=True) + EPS)
        on = oc * r
        sg = _sigmoid(g)
        don = dav * (g * sg)
        dg_ref[...] = (dav * on * (sg * (1.0 + g * (1.0 - sg)))).astype(MM)
        do = r * (don - jnp.mean(don, axis=-1, keepdims=True) - on * jnp.mean(don * on, axis=-1, keepdims=True))
        do_ref[...] = do.astype(MM)

    blk = pl.BlockSpec((ts, dv), lambda i, h: (i, h))
    return pl.pallas_call(
        body, name="ret_post_bwd", grid=(t // ts, HEADS),
        in_specs=[blk, pl.BlockSpec((ts, dv), lambda i, h: (i, 2 * HEADS + h)), blk],
        out_specs=[blk, blk],
        out_shape=[jax.ShapeDtypeStruct((t, d), MM), jax.ShapeDtypeStruct((t, d), MM)],
        compiler_params=_params(2))(da, proj, o_raw)


def _ret_core_bwd(qr, kr, vb, do, sf_in, sb_in, cos2, sin2, lg, d, bl, seq, *, tc=256):
    t = qr.shape[0]
    dk, dv = d // 8, d // 4
    tc = min(tc, seq)
    nc = seq // tc
    scale = float(dk) ** -0.5

    def body(lg_ref, q_ref, k_ref, v_ref, do_ref, sf_all, sb_all, c_ref, s_ref, dq_ref, dk_ref, dv_ref,
             dlf_ref, dlb_ref, dq_acc, dk_acc, dv_acc):
        c = _ret_consts(lg_ref, pl.program_id(1), tc, dk)
        fwd = c["ab"] >= 0
        zero_state = jnp.zeros((dk, dv), F32)
        zero = jnp.zeros((1, 1), F32)

        def rows_of(i):
            return pl.ds(pl.multiple_of(i * tc, tc), tc)

        def total(xv):
            return jnp.sum(xv, keepdims=True)

        def fwd_sweep(i, carry):
            hh, dlf, dlb = carry
            rows = rows_of(i)
            q, kk, v, dov = q_ref[rows, :], k_ref[rows, :], v_ref[rows, :], do_ref[rows, :]
            dof, vf = dov.astype(F32), v.astype(F32)
            p = _dot_nt(q, kk) * c["dmat"]
            da = _dot_nt(dov, v)
            x = p * da * c["ab"]
            dlf = dlf + total(jnp.where(fwd, x, 0.0))
            dlb = dlb - total(jnp.where(fwd, 0.0, x))
            pb, dpb = p.astype(MM), (da * c["dmat"]).astype(MM)
            dq = _dot(dpb, kk)
            dkc = _dot_tn(dpb, q)
            dvc = _dot_tn(pb, dov)
            sf, sb = sf_all[i], sb_all[i]
            sfb, sbb = sf.astype(MM), sb.astype(MM)
            q_xf, q_zb = _scaled(q, c["xi_f"]), _scaled(q, c["zeta_b"])
            dq = dq + _dot_nt(dov, sfb) * c["xi_f"] + _dot_nt(dov, sbb) * c["zeta_b"]
            dlf = dlf + total(jnp.sum(_dot(q_xf, sfb) * dof, axis=-1, keepdims=True) * c["up"])
            dlb = dlb + total(jnp.sum(_dot(q_zb, sbb) * dof, axis=-1, keepdims=True) * c["down"])
            hb = hh.astype(MM)
            dkc = dkc + _dot_nt(v, hb) * c["xi_b"]
            dv_bx = _dot(_scaled(kk, c["xi_b"]), hb)
            dlb = dlb + total(jnp.sum(vf * dv_bx, axis=-1, keepdims=True) * c["up"])
            dlb = dlb + float(tc) * total(hh * (sb * c["cb"]))
            dq_acc[rows, :] = dq
            dk_acc[rows, :] = dkc
            dv_acc[rows, :] = dvc + dv_bx
            return hh * c["cb"] + _dot_tn(q_zb, dov), dlf, dlb

        _, dlf, dlb = lax.fori_loop(0, nc, fwd_sweep, (zero_state, zero, zero))

        def rev_sweep(ii, carry):
            gg, dlf = carry
            i = nc - 1 - ii
            rows = rows_of(i)
            q, kk, v, dov = q_ref[rows, :], k_ref[rows, :], v_ref[rows, :], do_ref[rows, :]
            gb = gg.astype(MM)
            dk_acc[rows, :] += _dot_nt(v, gb) * c["zeta_f"]
            dv_fx = _dot(_scaled(kk, c["zeta_f"]), gb)
            dv_acc[rows, :] += dv_fx
            dlf = dlf + total(jnp.sum(v.astype(F32) * dv_fx, axis=-1, keepdims=True) * c["down"])
            dlf = dlf + float(tc) * total(gg * (sf_all[i] * c["cf"]))
            return gg * c["cf"] + _dot_tn(_scaled(q, c["xi_f"]), dov), dlf

        _, dlf = lax.fori_loop(0, nc, rev_sweep, (zero_state, dlf))

        cs, sn = c_ref[...], s_ref[...]
        dq_ref[...] = _rot_t(dq_acc[...], cs, sn, dk // 2).astype(MM)
        dk_ref[...] = (_rot_t(dk_acc[...], cs, sn, dk // 2) * scale).astype(MM)
        dv_ref[...] = dv_acc[...].astype(MM)
        dlf_ref[...] = jnp.broadcast_to(dlf, dlf_ref.shape)
        dlb_ref[...] = jnp.broadcast_to(dlb, dlb_ref.shape)

    qk = pl.BlockSpec((seq, dk), lambda b, h: (b, h))
    vv = pl.BlockSpec((seq, dv), lambda b, h: (b, h))
    tab = pl.BlockSpec((seq, dk), lambda b, h: (0, 0))
    dl = pl.BlockSpec((None, 8, 128), lambda b, h: (b * HEADS + h, 0, 0))
    states = pl.BlockSpec((None, nc, dk, dv), lambda b, h: (b * HEADS + h, 0, 0, 0))
    return pl.pallas_call(
        body, name="ret_core_bwd", grid=(bl, HEADS),
        in_specs=[pl.BlockSpec(memory_space=pltpu.SMEM), qk, qk, vv, vv, states, states, tab, tab],
        out_specs=[qk, qk, vv, dl, dl],
        out_shape=[jax.ShapeDtypeStruct((t, d // 2), MM), jax.ShapeDtypeStruct((t, d // 2), MM),
                   jax.ShapeDtypeStruct((t, d), MM),
                   jax.ShapeDtypeStruct((bl * HEADS, 8, 128), F32), jax.ShapeDtypeStruct((bl * HEADS, 8, 128), F32)],
        scratch_shapes=[pltpu.VMEM((seq, dk), F32), pltpu.VMEM((seq, dk), F32), pltpu.VMEM((seq, dv), F32)],
        compiler_params=_params(2))(lg, qr, kr, vb, do, sf_in, sb_in, cos2, sin2)


def _window_count(row, w, seq):
    return (jnp.minimum(row + w // 2, seq) - jnp.maximum(row - w // 2, 0)).astype(F32)


def _window_sum(pv, row, w, seq, sign):
    acc = None
    for j in range(-(w // 2), w // 2):
        if j == 0:
            term = pv
        else:
            src = row + sign * j
            term = jnp.where((src >= 0) & (src < seq), pltpu.roll(pv, (-sign * j) % seq, 0), 0.0)
        acc = term if acc is None else acc + term
    return acc


def _pool_fwd(proj, w_grp, scale, d, bl, seq):
    t = proj.shape[0]
    dg = d // 8

    def body(p_ref, w_ref, s_ref, y_ref):
        row = lax.broadcasted_iota(jnp.int32, (seq, dg), 0)
        for gi, w in enumerate(POOL_WINDOWS):
            sl = slice(gi * dg, (gi + 1) * dg)
            pg = p_ref[:, sl].astype(F32)
            mixed = _window_sum(pg, row, w, seq, 1) / _window_count(row, w, seq) - pg
            yp = _dot(mixed.astype(MM), w_ref[gi].astype(MM))
            y_ref[:, sl] = (yp * s_ref[:, sl]).astype(MM)

    return pl.pallas_call(
        body, name="pool_fwd", grid=(bl,),
        in_specs=[pl.BlockSpec((seq, d // 2), lambda b: (b, 6)),
                  pl.BlockSpec(w_grp.shape, lambda b: (0, 0, 0)),
                  pl.BlockSpec((1, d // 2), lambda b: (0, 0))],
        out_specs=pl.BlockSpec((seq, d // 2), lambda b: (b, 0)),
        out_shape=jax.ShapeDtypeStruct((t, d // 2), MM),
        compiler_params=_params(1))(proj, w_grp, scale)


def _pool_bwd(proj, dy, w_grp, scale, d, bl, seq):
    t = proj.shape[0]
    dg = d // 8

    def body(p_ref, dy_ref, w_ref, s_ref, dp_ref, dw_ref, ds_ref):
        @pl.when(pl.program_id(0) == 0)
        def _():
            dw_ref[...] = jnp.zeros_like(dw_ref)
            ds_ref[...] = jnp.zeros_like(ds_ref)

        row = lax.broadcasted_iota(jnp.int32, (seq, dg), 0)
        for gi, w in enumerate(POOL_WINDOWS):
            sl = slice(gi * dg, (gi + 1) * dg)
            pg = p_ref[:, sl].astype(F32)
            cnt = _window_count(row, w, seq)
            mixb = (_window_sum(pg, row, w, seq, 1) / cnt - pg).astype(MM)
            wgb = w_ref[gi].astype(MM)
            yp = _dot(mixb, wgb)
            dyg = dy_ref[:, sl]
            ds_ref[:, sl] += jnp.sum(dyg * yp, axis=0, keepdims=True)
            dyp = (dyg * s_ref[:, sl]).astype(MM)
            dmixed = _dot_nt(dyp, wgb)
            dw_ref[gi] += _dot_tn(mixb, dyp)
            dp_ref[:, sl] = (_window_sum(dmixed / cnt, row, w, seq, -1) - dmixed).astype(MM)

    half = pl.BlockSpec((seq, d // 2), lambda b: (b, 0))
    wspec = pl.BlockSpec(w_grp.shape, lambda b: (0, 0, 0))
    sspec = pl.BlockSpec((1, d // 2), lambda b: (0, 0))
    return pl.pallas_call(
        body, name="pool_bwd", grid=(bl,),
        in_specs=[pl.BlockSpec((seq, d // 2), lambda b: (b, 6)), half, wspec, sspec],
        out_specs=[half, wspec, sspec],
        out_shape=[jax.ShapeDtypeStruct((t, d // 2), MM), jax.ShapeDtypeStruct(w_grp.shape, F32),
                   jax.ShapeDtypeStruct((1, d // 2), F32)],
        compiler_params=_params(1))(proj, dy, w_grp, scale)


def _attn_probs(q, kk, dh):
    s = _dot_nt(q, kk) * (float(dh) ** -0.5)
    e = jnp.exp(s - jnp.max(s, axis=-1, keepdims=True))
    return e / jnp.sum(e, axis=-1, keepdims=True)


def _attn_fwd(proj, kv, d, bl, seq, mlen, *, tq=2048):
    t = proj.shape[0]
    dh = d // 8
    tq = min(tq, seq)
    nq = seq // tq

    def body(q_ref, k_ref, v_ref, o_ref):
        a = _attn_probs(q_ref[...].astype(MM), k_ref[...].astype(MM), dh)
        o_ref[...] = _dot(a.astype(MM), v_ref[...].astype(MM)).astype(MM)

    return pl.pallas_call(
        body, name="attn_fwd", grid=(bl, HEADS, nq),
        in_specs=[pl.BlockSpec((tq, dh), lambda b, h, i: (b * nq + i, 7 * HEADS + h)),
                  pl.BlockSpec((mlen, dh), lambda b, h, i: (b, h)),
                  pl.BlockSpec((mlen, dh), lambda b, h, i: (b, HEADS + h))],
        out_specs=pl.BlockSpec((tq, dh), lambda b, h, i: (b * nq + i, h)),
        out_shape=jax.ShapeDtypeStruct((t, d // 2), MM),
        compiler_params=_params(3))(proj, kv, kv)


def _attn_bwd(proj, kv, do, d, bl, seq, mlen, *, tq=2048):
    t = proj.shape[0]
    dh = d // 8
    tq = min(tq, seq)
    nq = seq // tq

    def body(q_ref, k_ref, v_ref, do_ref, dq_ref, dk_ref, dv_ref):
        @pl.when(pl.program_id(2) == 0)
        def _():
            dk_ref[...] = jnp.zeros_like(dk_ref)
            dv_ref[...] = jnp.zeros_like(dv_ref)

        q, kk, vv = q_ref[...].astype(MM), k_ref[...].astype(MM), v_ref[...].astype(MM)
        dov = do_ref[...].astype(MM)
        a = _attn_probs(q, kk, dh)
        dp = _dot_nt(dov, vv)
        ds = (a * (dp - jnp.sum(dp * a, axis=-1, keepdims=True)) * (float(dh) ** -0.5)).astype(MM)
        dq_ref[...] = _dot(ds, kk).astype(MM)
        dk_ref[...] += _dot_tn(ds, q)
        dv_ref[...] += _dot_tn(a.astype(MM), dov)

    qs = pl.BlockSpec((tq, dh), lambda b, h, i: (b * nq + i, h))
    ms = pl.BlockSpec((mlen, dh), lambda b, h, i: (b, h))
    return pl.pallas_call(
        body, name="attn_bwd", grid=(bl, HEADS, nq),
        in_specs=[pl.BlockSpec((tq, dh), lambda b, h, i: (b * nq + i, 7 * HEADS + h)), ms,
                  pl.BlockSpec((mlen, dh), lambda b, h, i: (b, HEADS + h)), qs],
        out_specs=[qs, ms, ms],
        out_shape=[jax.ShapeDtypeStruct((t, d // 2), MM), jax.ShapeDtypeStruct((bl * mlen, d // 2), F32),
                   jax.ShapeDtypeStruct((bl * mlen, d // 2), F32)],
        compiler_params=_params(3))(proj, kv, kv, do)


def _comm_call(name, body, arrays, out_shapes):
    n = len(arrays)
    hbm = pl.BlockSpec(memory_space=pl.ANY)
    return pl.pallas_call(
        body, name=name, out_shape=out_shapes, in_specs=[hbm] * n, out_specs=[hbm] * n,
        scratch_shapes=[pltpu.SemaphoreType.DMA((7 * n,)), pltpu.SemaphoreType.DMA((7 * n,)),
                        pltpu.SemaphoreType.DMA((n,))],
    )(*arrays)


def _all_gather(name, shards):
    n = len(shards)

    def body(*refs):
        x_refs, out_refs = refs[:n], refs[n:2 * n]
        send_sems, recv_sems, local_sems = refs[2 * n:]
        x, y, c = lax.axis_index("x"), lax.axis_index("y"), lax.axis_index("c")
        me, sibling = (x, y, c), (x, y, 1 - c)
        chips = [(1 - x, y), (x, 1 - y), (1 - x, 1 - y)]

        def copy(o, k, block, to, src=None):
            slot = out_refs[o].at[4 * block[0] + 2 * block[1] + block[2]]
            return pltpu.make_async_remote_copy(
                src_ref=slot if src is None else src, dst_ref=slot, send_sem=send_sems.at[7 * o + k],
                recv_sem=recv_sems.at[7 * o + k], device_id=to, device_id_type=MESH)

        locals_, remotes = [], []
        for o in range(n):
            mine = pltpu.make_async_copy(x_refs[o], out_refs[o].at[4 * x + 2 * y + c], local_sems.at[o])
            mine.start()
            locals_.append(mine)
            first = [copy(o, 0, me, sibling, src=x_refs[o])]
            first += [copy(o, 1 + j, me, (*chip, c), src=x_refs[o]) for j, chip in enumerate(chips)]
            for cp in first:
                cp.start()
            remotes += first
        for o in range(n):
            for j, chip in enumerate(chips):
                copy(o, 1 + j, (*chip, c), me).wait_recv()
                passed = copy(o, 4 + j, (*chip, c), sibling)
                passed.start()
                remotes.append(passed)
        for o in range(n):
            copy(o, 0, sibling, me).wait_recv()
            for j, chip in enumerate(chips):
                copy(o, 4 + j, (*chip, 1 - c), me).wait_recv()
        for cp in remotes:
            cp.wait_send()
        for mine in locals_:
            mine.wait()

    outs = [jax.ShapeDtypeStruct((N_DEV,) + s.shape, s.dtype) for s in shards]
    return _comm_call(name, body, shards, outs)


def _columns_side_by_side(name, g):
    _, k, wb = g.shape

    def body(x_ref, o_ref):
        o_ref[...] = x_ref[...]

    return pl.pallas_call(
        body, name=name, grid=(N_DEV,), in_specs=[pl.BlockSpec((None, k, wb), lambda j: (j, 0, 0))],
        out_specs=pl.BlockSpec((k, wb), lambda j: (0, j)),
        out_shape=jax.ShapeDtypeStruct((k, N_DEV * wb), g.dtype), compiler_params=_params(1))(g)


def _peer_of(k, x, y, c):
    peer = (1 - x if k & 4 else x, 1 - y if k & 2 else y, 1 - c if k & 1 else c)
    return peer, 4 * peer[0] + 2 * peer[1] + peer[2]


def _split_copies(scatter, srcs, lands, send_sems, recv_sems, arriving):
    x, y, c = lax.axis_index("x"), lax.axis_index("y"), lax.axis_index("c")
    me_idx = 4 * x + 2 * y + c
    copies = []
    for o, (src, land) in enumerate(zip(srcs, lands)):
        for k in range(1, N_DEV):
            peer, p_idx = _peer_of(k, x, y, c)
            mine = src.at[p_idx] if scatter else src
            sems = dict(send_sem=send_sems.at[7 * o + k - 1], recv_sem=recv_sems.at[7 * o + k - 1],
                        device_id=peer, device_id_type=MESH)
            slot = land.at[p_idx] if arriving else land.at[me_idx]
            copies.append(pltpu.make_async_remote_copy(src_ref=mine, dst_ref=slot, **sems))
    return copies


_HBM = pl.BlockSpec(memory_space=pltpu.HBM)
_SEM = pl.BlockSpec(memory_space=pltpu.SEMAPHORE)
_EFFECT = pltpu.SideEffectType.DATAFLOW_SIDE_EFFECTING


def _exchange_start(name, scatter, arrays, after=()):
    n = len(arrays)
    lands = [lax.empty(a.shape if scatter else (N_DEV,) + a.shape, a.dtype) for a in arrays]

    def body(*refs):
        srcs, lnds = refs[:n], refs[n:2 * n]
        send_sems, recv_sems = refs[2 * n + len(after)], refs[2 * n + len(after) + 1]
        token = refs[-1]
        for cp in _split_copies(scatter, srcs, lnds, send_sems, recv_sems, False):
            cp.start()
        token[...] = jnp.zeros_like(token)

    hbm_in = [pltpu.with_memory_space_constraint(a, pltpu.HBM) for a in list(arrays) + lands]
    res = pl.pallas_call(
        body, name=name,
        out_shape=(pltpu.SemaphoreType.DMA((7 * n,)), pltpu.SemaphoreType.DMA((7 * n,)),
                   *[pltpu.HBM(a.shape, a.dtype) for a in hbm_in], jax.ShapeDtypeStruct((8, 128), F32)),
        in_specs=[_HBM] * (2 * n) + [pl.BlockSpec(memory_space=pl.ANY)] * len(after),
        out_specs=(_SEM, _SEM, *[_HBM] * (2 * n), pl.BlockSpec(memory_space=pltpu.VMEM)),
        input_output_aliases={i: 2 + i for i in range(2 * n)},
        compiler_params=pltpu.CompilerParams(has_side_effects=_EFFECT),
    )(*hbm_in, *after)
    return res[0], res[1], list(res[2:2 + n]), list(res[2 + n:2 + 2 * n]), res[-1]


def _exchange_wait(name, scatter, started, after):
    send_sems, recv_sems, srcs, lands, _ = started
    n = len(srcs)

    def body(*refs):
        src_refs, lnd_refs = refs[:n], refs[n:2 * n]
        for cp in _split_copies(scatter, src_refs, lnd_refs, refs[2 * n], refs[2 * n + 1], False):
            cp.wait_send()
        for cp in _split_copies(scatter, src_refs, lnd_refs, refs[2 * n], refs[2 * n + 1], True):
            cp.wait_recv()

    res = pl.pallas_call(
        body, name=name, out_shape=tuple(pltpu.HBM(a.shape, a.dtype) for a in srcs + lands),
        in_specs=[_HBM] * (2 * n) + [_SEM, _SEM, pl.BlockSpec(memory_space=pl.ANY)],
        out_specs=tuple([_HBM] * (2 * n)), input_output_aliases={i: i for i in range(2 * n)},
        compiler_params=pltpu.CompilerParams(has_side_effects=_EFFECT),
    )(*srcs, *lands, send_sems, recv_sems, after)
    return list(res[n:]), list(res[:n])


def _adamw(name, parts, w, m, v, prev, layer, *, tr=256):
    _, a, b = w.shape
    tr = _tile(a, tr, 8)
    c1 = 1.0 - ADAM_B1 ** ADAM_STEP
    c2 = 1.0 - ADAM_B2 ** ADAM_STEP

    def body(p_ref, w_ref, m_ref, v_ref, _g, _d, _m, _v, g_out, d_out, m_out, v_out):
        g = p_ref[0].astype(F32)
        for s in range(1, N_DEV):
            g = g + p_ref[s].astype(F32)
        mn = ADAM_B1 * m_ref[...] + (1.0 - ADAM_B1) * g
        vn = ADAM_B2 * v_ref[...] + (1.0 - ADAM_B2) * (g * g)
        g_out[...] = g
        m_out[...] = mn
        v_out[...] = vn
        d_out[...] = -ADAM_LR * ((mn / c1) / (jnp.sqrt(vn / c2) + ADAM_EPS) + ADAM_WD * w_ref[...])

    slab = pl.BlockSpec((None, tr, b), lambda i: (layer, i, 0))
    whole = pl.BlockSpec(memory_space=pl.ANY)
    return pl.pallas_call(
        body, name=name, grid=(a // tr,),
        in_specs=[pl.BlockSpec((N_DEV, tr, b), lambda i: (0, i, 0)), slab, slab, slab] + [whole] * 4,
        out_specs=[slab] * 4, out_shape=[jax.ShapeDtypeStruct(w.shape, F32)] * 4,
        input_output_aliases={4: 0, 5: 1, 6: 2, 7: 3},
        compiler_params=_params(1))(parts, w, m, v, *prev)


_COL = ("w_in", "w_pool_o", "w_mem_o", "w_ff1")
_DW_SHARDED = ("w_in", "w_ff1")
_BIG =("w_in", "w_ret_o", "w_pool_o", "w_mem_kv", "w_mem_o", "w_out", "w_ff1", "w_ff2")
_SMALL = ("ret_decay_logit", "w_pool_grp", "pool_scale", "norm1_g", "norm2_g", "mem_norm_g", "final_norm_g")
_SMALL_MM = ("w_pool_grp",)
_SMALL_F32 = tuple(n for n in _SMALL if n not in _SMALL_MM)
_WEIGHTS = ("w_in", "ret_decay_logit", "w_ret_o", "w_pool_grp", "pool_scale", "w_pool_o", "w_mem_kv", "w_mem_o",
            "w_out", "w_ff1", "w_ff2", "norm1_g", "norm2_g", "mem_norm_g", "final_norm_g")


def _small_rows(size, d):
    return -(-size // (8 * d)) * 8


def _pack_small(ws, d, names):
    parts = []
    for n in names:
        flat = ws[n].reshape(-1)
        rows = _small_rows(flat.shape[0], d)
        parts.append(jnp.pad(flat, (0, rows * d - flat.shape[0])).reshape(rows, d))
    return jnp.concatenate(parts, axis=0)[None]


def _unpack_small(packed, like, d, names):
    out, off = {}, 0
    for n in names:
        rows = _small_rows(like[n].size, d)
        out[n] = packed[0, off:off + rows].reshape(-1)[:like[n].size].reshape(like[n].shape)
        off += rows
    return out


def kernel(x, mem, w_in, ret_decay_logit, w_ret_o, w_pool_grp, pool_scale, w_pool_o, w_mem_kv, w_mem_o, w_out, w_ff1, w_ff2, norm1_g, norm2_g, mem_norm_g, final_norm_g, loss_target, m_w_in, m_ret_decay_logit, m_w_ret_o, m_w_pool_grp, m_pool_scale, m_w_pool_o, m_w_mem_kv, m_w_mem_o, m_w_out, m_w_ff1, m_w_ff2, m_norm1_g, m_norm2_g, m_mem_norm_g, m_final_norm_g, v_w_in, v_ret_decay_logit, v_w_ret_o, v_w_pool_grp, v_pool_scale, v_w_pool_o, v_w_mem_kv, v_w_mem_o, v_w_out, v_w_ff1, v_w_ff2, v_norm1_g, v_norm2_g, v_mem_norm_g, v_final_norm_g):
    w = dict(w_in=w_in, ret_decay_logit=ret_decay_logit, w_ret_o=w_ret_o, w_pool_grp=w_pool_grp,
             pool_scale=pool_scale, w_pool_o=w_pool_o, w_mem_kv=w_mem_kv, w_mem_o=w_mem_o, w_out=w_out,
             w_ff1=w_ff1, w_ff2=w_ff2, norm1_g=norm1_g, norm2_g=norm2_g, mem_norm_g=mem_norm_g,
             final_norm_g=final_norm_g)
    mom = dict(w_in=m_w_in, ret_decay_logit=m_ret_decay_logit, w_ret_o=m_w_ret_o, w_pool_grp=m_w_pool_grp,
               pool_scale=m_pool_scale, w_pool_o=m_w_pool_o, w_mem_kv=m_w_mem_kv, w_mem_o=m_w_mem_o,
               w_out=m_w_out, w_ff1=m_w_ff1, w_ff2=m_w_ff2, norm1_g=m_norm1_g, norm2_g=m_norm2_g,
               mem_norm_g=m_mem_norm_g, final_norm_g=m_final_norm_g)
    vel = dict(w_in=v_w_in, ret_decay_logit=v_ret_decay_logit, w_ret_o=v_w_ret_o, w_pool_grp=v_w_pool_grp,
               pool_scale=v_pool_scale, w_pool_o=v_w_pool_o, w_mem_kv=v_w_mem_kv, w_mem_o=v_w_mem_o,
               w_out=v_w_out, w_ff1=v_w_ff1, w_ff2=v_w_ff2, norm1_g=v_norm1_g, norm2_g=v_norm2_g,
               mem_norm_g=v_mem_norm_g, final_norm_g=v_final_norm_g)

    bl, seq, d = x.shape
    mlen = mem.shape[1]
    depth = w_in.shape[0]
    t = bl * seq
    dk = d // 8

    me_idx = 4 * lax.axis_index("x") + 2 * lax.axis_index("y") + lax.axis_index("c")

    def natural(n, g):
        if n in _DW_SHARDED:
            return _columns_side_by_side("relayout_" + n, g)
        if n in _COL:
            return jnp.transpose(g, (1, 0, 2)).reshape(g.shape[1], -1)
        return g.reshape(-1, g.shape[-1])

    def finish_gather(name, names, started, after):
        got, mine = _exchange_wait(name, False, started, after)
        return {n: natural(n, lax.dynamic_update_slice(g, sh[None], (me_idx, 0, 0)))
                for n, g, sh in zip(names, got, mine)}

    shards = [{n: w[n][l].astype(MM) for n in _BIG} for l in range(depth)]
    rest = _BIG[1:]
    (w_in0,) = _all_gather("gather_w_in", [shards[0][_BIG[0]]])
    full = [{_BIG[0]: natural(_BIG[0], w_in0)}]
    pending = _exchange_start("gather_start_0", False, [shards[0][n] for n in rest], after=[w_in0])
    first_tokens = (pending[4],)
    pending_next = None
    if depth > 1:
        pending_next = _exchange_start("gather_start_1", False, [shards[1][n] for n in _BIG], after=[pending[4]])
        first_tokens += (pending_next[4],)

    inv = ROPE_BASE ** (-jnp.arange(0, dk, 2, dtype=F32) / dk)
    ang = jnp.arange(seq, dtype=F32)[:, None] * inv[None, :]
    cos2 = jnp.concatenate([jnp.cos(ang), jnp.cos(ang)], axis=-1)
    sin2 = jnp.concatenate([-jnp.sin(ang), jnp.sin(ang)], axis=-1)
    log_g = jax.nn.log_sigmoid(ret_decay_logit)
    x2 = x.reshape(t, d)
    mem2 = mem.reshape(bl * mlen, d)
    gmem = mem_norm_g.reshape(1, d)

    def merge(a_r, y_p, o_a, g_r, g_p, g_m, w_r, w_p, w_m):
        f = lambda z: z.astype(F32)
        o_r, o_p, o_m = _dot(a_r, w_r), _dot(y_p, w_p), _dot(o_a, w_m)
        return _sigmoid(f(g_r)) * o_r + _sigmoid(f(g_p)) * o_p + _sigmoid(f(g_m)) * o_m, o_r, o_p, o_m

    def relu2(u):
        r = jnp.maximum(u.astype(MM), 0.0)
        return r * r

    def ident(a):
        return a

    saved = []
    xc = x2
    for l in range(depth):
        s = dict(x_in=xc)
        started_now = ()
        if l > 0:
            full.append(finish_gather(f"gather_wait_{l}", _BIG, pending, xc))
            if l + 1 < depth:
                pending = _exchange_start(f"gather_start_{l + 1}", False, [shards[l + 1][n] for n in _BIG],
                                          after=[full[l]["w_in"]])
                started_now = (pending[4],)
        fw = full[l]
        g1 = norm1_g[l].reshape(1, d)
        g2 = norm2_g[l].reshape(1, d)
        s["proj"], s["h1"] = _pmm("proj", _rms_prologue, [(xc, d, 0)], [g1], fw["w_in"],
                                  tm=2048, tn=1024, save_a=True, out_dtypes=(MM,),
                                  after=started_now if l > 0 else first_tokens)
        proj = s["proj"]
        s["qr"], s["kr"], s["vb"] = _ret_pre(proj, cos2, sin2, d, seq)
        s["o_raw"], s["a_ret"], s["sf"], s["sb"] = _ret_core_fwd(s["qr"], s["kr"], s["vb"], proj, log_g[l],
                                                                 d, bl, seq)
        s["y"] = _pool_fwd(proj, w_pool_grp[l], pool_scale[l].reshape(1, -1), d, bl, seq)
        started_now = ()
        if l == 0:
            fw.update(finish_gather("gather_wait_0", rest, pending, s["a_ret"]))
            pending = pending_next
        s["kv"], s["memn"] = _pmm("mem_kv", _rms_prologue, [(mem2, d, 0)], [gmem], fw["w_mem_kv"],
                                  tm=512, tn=512, save_a=True, after=started_now)
        s["o_att"] = _attn_fwd(proj, s["kv"], d, bl, seq, mlen)
        s["x_mid"], s["merged"], s["o_ret"], s["o_pool"], s["o_mem"] = _pmm(
            "merge_out", merge,
            [(s["a_ret"], d, 0), (s["y"], d // 2, 0), (s["o_att"], d // 2, 0), (proj, d, 4), (proj, d, 5), (proj, d, 6)],
            [fw["w_ret_o"], fw["w_pool_o"], fw["w_mem_o"]], fw["w_out"], tm=512, tn=1024, residual=xc, save_a=True,
            extra_outs=[(d, MM)] * 3)
        s["u"], s["h2"] = _pmm("ff1", _rms_prologue, [(s["x_mid"], d, 0)], [g2], fw["w_ff1"],
                               tm=512, tn=4096, save_a=True, out_dtypes=(MM,))
        (xc,) = _pmm("ff2", relu2, [(s["u"], s["u"].shape[1], 0)], [], fw["w_ff2"],
                     tm=512, tn=1024, residual=s["x_mid"])
        saved.append(s)

    dxc, g_final, loss_part = _loss_head(xc, loss_target.reshape(t, d), final_norm_g.reshape(1, d))
    loss = lax.psum(loss_part[0, 0], ("x", "y", "c"))

    small_names = ("w_pool_grp", "pool_scale", "norm1_g", "norm2_g", "ret_decay_logit")
    grads = {n: [None] * depth for n in small_names}
    group_a = ("w_ff1", "w_ff2")
    group_b = tuple(n for n in _BIG if n not in group_a)
    scatters = {}
    dmemn = jnp.zeros((bl * mlen, d), F32)

    def relu2_bwd(acc, u):
        return (acc * (2.0 * jnp.maximum(u.astype(F32), 0.0)),)

    def gates_bwd(acc, g_r, g_p, g_m, o_r, o_p, o_m, w_r, w_p, w_m):
        d_os, d_gs, backs = [], [], []
        for gz, oz, wz in ((g_r, o_r, w_r), (g_p, o_p, w_p), (g_m, o_m, w_m)):
            sg = _sigmoid(gz.astype(F32))
            d_o = (acc * sg).astype(MM)
            d_os.append(d_o)
            d_gs.append(acc * oz.astype(F32) * (sg * (1.0 - sg)))
            backs.append(_dot_nt(d_o, wz))
        return tuple(d_os + d_gs + backs)

    def to_send(n, g):
        a, b = w[n].shape[1:]
        if n in _DW_SHARDED:
            return g
        if n in _COL:
            return jnp.transpose(g.reshape(a, N_DEV, b), (1, 0, 2))
        return g.reshape(N_DEV, a, b)

    for l in reversed(range(depth)):
        s = saved[l]
        fw = full[l]
        proj = s["proj"]
        g1 = norm1_g[l].reshape(1, d)
        g2 = norm2_g[l].reshape(1, d)
        dw = {}
        (du,) = _pmm("ff2_bwd", ident, [(dxc, d, 0)], [], fw["w_ff2"], w_mode="nt", tm=512, tn=4096,
                     epilogue=relu2_bwd, epi_ins=[(s["u"], 0)], out_dtypes=(MM,))
        dw["w_ff2"] = _tnmm("dw_ff2", s["u"], dxc, a_fn=relu2)
        dw["w_ff1"] = _tnmm("dw_ff1", s["h2"], du, col_shards=True)
        scatters[l, "a"] = _exchange_start(f"scatter_start_a{l}", True, [to_send(n, dw[n]) for n in group_a])
        dmid, grads["norm2_g"][l] = _mm_rms_bwd("ff1_norm2_bwd", du, fw["w_ff1"], s["x_mid"], g2, dxc, tm=512)
        d_oret, d_opool, d_omem, dgr, dgp, dgm, da_ret, dy, do_att = _pmm(
            "out_bwd", ident, [(dmid, d, 0)], [], fw["w_out"], w_mode="nt", tm=256, tn=d, epilogue=gates_bwd,
            epi_ins=[(proj, 4 * d), (proj, 5 * d), (proj, 6 * d), (s["o_ret"], 0), (s["o_pool"], 0), (s["o_mem"], 0)],
            epi_full=[fw["w_ret_o"], fw["w_pool_o"], fw["w_mem_o"]], out_dtypes=(MM,) * 6 + (F32,) * 3,
            out_widths=[d] * 7 + [d // 2] * 2, after=(scatters[l, "a"][4],))
        dw["w_out"] = _tnmm("dw_out", s["merged"], dmid)
        dw["w_ret_o"] = _tnmm("dw_ret_o", s["a_ret"], d_oret)
        dw["w_pool_o"] = _tnmm("dw_pool_o", s["y"], d_opool)
        dw["w_mem_o"] = _tnmm("dw_mem_o", s["o_att"], d_omem)
        dg_ret, do_ret = _ret_post_bwd(da_ret, proj, s["o_raw"], d)
        dq, dkk, dvv, dlf, dlb = _ret_core_bwd(s["qr"], s["kr"], s["vb"], do_ret, s["sf"], s["sb"], cos2, sin2,
                                               log_g[l], d, bl, seq)
        dl = jnp.stack([dlf[:, 0, 0].reshape(bl, HEADS).sum(0), dlb[:, 0, 0].reshape(bl, HEADS).sum(0)])
        grads["ret_decay_logit"][l] = dl * jax.nn.sigmoid(-ret_decay_logit[l])
        dp, grads["w_pool_grp"][l], dscale = _pool_bwd(proj, dy, w_pool_grp[l], pool_scale[l].reshape(1, -1),
                                                       d, bl, seq)
        grads["pool_scale"][l] = dscale.reshape(-1)
        dqm, dmk, dmv = _attn_bwd(proj, s["kv"], do_att, d, bl, seq, mlen)
        dkv = jnp.concatenate([dmk, dmv], axis=-1).astype(MM)
        dw["w_mem_kv"] = _tnmm("dw_mem_kv", s["memn"], dkv)
        (dmemn,) = _pmm("mem_kv_bwd", None, [(dkv, d, 0)], [], fw["w_mem_kv"], w_mode="nt", tm=512, tn=512,
                        residual=dmemn)
        dproj = [dq, dkk, dvv, dg_ret, dp, dqm, dgr, dgp, dgm]
        dw["w_in"] = _tnmm("dw_in", s["h1"], dproj, col_shards=True, tm=512, tk=512)
        scatters[l, "b"] = _exchange_start(f"scatter_start_b{l}", True, [to_send(n, dw[n]) for n in group_b])
        dxc, grads["norm1_g"][l] = _mm_rms_bwd("proj_norm1_bwd", dproj, fw["w_in"], s["x_in"], g1, dmid, tm=256,
                                               after=(scatters[l, "b"][4],))

    _, g_memn = _rms_bwd("mem_norm_bwd", dmemn, mem2, gmem, None)
    grad_x = dxc.reshape(bl, seq, d)

    small_g = dict(ret_decay_logit=jnp.stack(grads["ret_decay_logit"]), w_pool_grp=jnp.stack(grads["w_pool_grp"]),
                   pool_scale=jnp.stack(grads["pool_scale"]),
                   norm1_g=jnp.concatenate(grads["norm1_g"], axis=0), norm2_g=jnp.concatenate(grads["norm2_g"], axis=0),
                   mem_norm_g=g_memn.reshape(-1), final_norm_g=g_final.reshape(-1))
    small_started = _exchange_start("gather_small_start", False,
                                    [_pack_small(small_g, d, _SMALL_MM)[0].astype(MM),
                                     _pack_small(small_g, d, _SMALL_F32)[0]])

    big = {n: [lax.empty(w[n].shape, F32) for _ in range(4)] for n in _BIG}

    def update(l, grp, names, after):
        recv, sent = _exchange_wait(f"scatter_wait_{grp}{l}", True, scatters[l, grp], after)
        for n, r, snt in zip(names, recv, sent):
            own = lax.dynamic_slice_in_dim(snt, me_idx, 1, axis=0)
            parts = lax.dynamic_update_slice(r, own, (me_idx, 0, 0))
            big[n] = _adamw("adamw_" + n, parts, w[n], mom[n], vel[n], big[n], l)
        return big[names[-1]][0]

    after = dxc
    for l in reversed(range(1, depth)):
        for grp, names in (("a", group_a), ("b", group_b)):
            after = update(l, grp, names, after)
    after = update(0, "a", group_a, after)
    small_lands, small_mine = _exchange_wait("gather_small_wait", False, small_started, after)
    small = [{} for _ in range(4)]
    for names, land, mine in zip((_SMALL_MM, _SMALL_F32), small_lands, small_mine):
        parts = lax.dynamic_update_slice(land, mine[None], (me_idx, 0, 0))
        w_small = _pack_small(w, d, names)
        res = _adamw("adamw_small", parts, w_small, _pack_small(mom, d, names), _pack_small(vel, d, names),
                     [lax.empty(w_small.shape, F32) for _ in range(4)], 0)
        after = res[0]
        for k in range(4):
            small[k].update(_unpack_small(res[k], w, d, names))
    update(0, "b", group_b, after)

    outs = [loss, grad_x]
    for k in range(4):
        outs += [big[n][k] if n in _BIG else small[k][n] for n in _WEIGHTS]
    return tuple(outs)
```

```python
import jax
import jax.numpy as jnp
from jax import lax
from jax.experimental import pallas as pl
from jax.experimental.pallas import tpu as pltpu

F32 = jnp.float32
MM = jnp.bfloat16
N_DEV = 8
HEADS = 4
POOL_WINDOWS = (2, 4, 8, 16)
EPS = 1e-6
ROPE_BASE = 10000.0
ADAM_LR, ADAM_B1, ADAM_B2, ADAM_EPS, ADAM_WD, ADAM_STEP = 0.001, 0.9, 0.999, 1e-08, 0.01, 10
V7X_VMEM_LIMIT = 56 * 1024 * 1024
MESH = pl.DeviceIdType.MESH


def _params(n_axes):
    return pltpu.CompilerParams(dimension_semantics=("arbitrary",) * n_axes,
                                vmem_limit_bytes=V7X_VMEM_LIMIT)


def _tile(n, pref, align=128):
    cands = [c for c in range(align, min(pref, n) + 1, align) if n % c == 0]
    return max(cands) if cands else n


def _sigmoid(z):
    return 0.5 * jnp.tanh(0.5 * z) + 0.5


def _dot(a, b):
    return jnp.dot(a, b, preferred_element_type=F32)


def _dot_nt(a, b):
    return lax.dot_general(a, b, (((1,), (1,)), ((), ())), preferred_element_type=F32)


def _dot_tn(a, b):
    return lax.dot_general(a, b, (((0,), (0,)), ((), ())), preferred_element_type=F32)


def _pmm(name, prologue, row_ins, vec_ins, w, *, tm, tn, w_mode="nn", residual=None, save_a=False,
         epilogue=None, epi_ins=(), out_dtypes=(F32,), after=(), extra_outs=(), epi_full=(), out_widths=None):
    m = row_ins[0][0].shape[0]
    if w_mode == "nn":
        k, n = w.shape
        tn = _tile(n, tn)
        w_spec = pl.BlockSpec((k, tn), lambda i, j: (0, j))
    else:
        n, k = w.shape
        tn = _tile(n, tn)
        w_spec = pl.BlockSpec((tn, k), lambda i, j: (j, 0))
    tm = _tile(m, tm, 8)
    n_row, n_vec, n_epi, n_out = len(row_ins), len(vec_ins), len(epi_ins), len(out_dtypes)
    has_res = residual is not None
    use_scr = prologue is not None
    out_widths = [n] * n_out if out_widths is None else list(out_widths)
    assert all(wd == n for wd in out_widths) or tn == n

    def body(*refs):
        row_refs = refs[:n_row]
        p = n_row
        vec_refs = refs[p:p + n_vec]
        p += n_vec
        w_ref = refs[p]
        p += 1
        res_ref = refs[p] if has_res else None
        p += int(has_res)
        epi_refs = refs[p:p + n_epi + len(epi_full)]
        p += n_epi + len(epi_full) + len(after)
        out_refs = refs[p:p + n_out]
        p += n_out
        a_out = refs[p] if save_a else None
        p += int(save_a)
        extra_refs = refs[p:p + len(extra_outs)]
        p += len(extra_outs)
        if use_scr:
            a_src = refs[p]

            @pl.when(pl.program_id(1) == 0)
            def _():
                made = prologue(*[r[...] for r in row_refs], *[v[...] for v in vec_refs])
                made = made if isinstance(made, tuple) else (made,)
                a = made[0].astype(MM)
                a_src[...] = a
                if save_a:
                    a_out[...] = a
                for e_ref, e in zip(extra_refs, made[1:]):
                    e_ref[...] = e.astype(e_ref.dtype)
        else:
            a_src = row_refs[0]
        if w_mode == "nt":
            acc = _dot_nt(a_src[...], w_ref[...])
        else:
            acc = _dot(a_src[...], w_ref[...])
        if has_res:
            acc = acc + res_ref[...]
        outs = epilogue(acc, *[e[...] for e in epi_refs]) if epilogue is not None else (acc,)
        for o_ref, o in zip(out_refs, outs):
            o_ref[...] = o.astype(o_ref.dtype)

    in_specs = [pl.BlockSpec((tm, wd), lambda i, j, cb=cb: (i, cb)) for (_, wd, cb) in row_ins]
    in_specs += [pl.BlockSpec(v.shape, lambda i, j: (0, 0)) for v in vec_ins]
    in_specs += [w_spec]
    args = [r[0] for r in row_ins] + list(vec_ins) + [w]
    if has_res:
        in_specs.append(pl.BlockSpec((tm, tn), lambda i, j: (i, j)))
        args.append(residual)
    for (arr, off) in epi_ins:
        assert off % tn == 0
        in_specs.append(pl.BlockSpec((tm, tn), lambda i, j, ob=off // tn: (i, ob + j)))
        args.append(arr)
    in_specs += [pl.BlockSpec(v.shape, lambda i, j: (0, 0)) for v in epi_full]
    args += list(epi_full)
    n_after = len(after)
    in_specs += [pl.BlockSpec(memory_space=pl.ANY)] * n_after
    args += list(after)
    out_specs = [pl.BlockSpec((tm, tn if wd == n else wd), lambda i, j: (i, j)) for wd in out_widths]
    out_shape = [jax.ShapeDtypeStruct((m, wd), dt) for wd, dt in zip(out_widths, out_dtypes)]
    if save_a:
        out_specs.append(pl.BlockSpec((tm, k), lambda i, j: (i, 0)))
        out_shape.append(jax.ShapeDtypeStruct((m, k), MM))
    for wd, dt in extra_outs:
        out_specs.append(pl.BlockSpec((tm, wd), lambda i, j: (i, 0)))
        out_shape.append(jax.ShapeDtypeStruct((m, wd), dt))
    scratch = [pltpu.VMEM((tm, k), MM)] if use_scr else []
    return pl.pallas_call(body, name=name, grid=(m // tm, n // tn), in_specs=in_specs,
                          out_specs=out_specs, out_shape=out_shape, scratch_shapes=scratch,
                          compiler_params=_params(2))(*args)


def _tnmm(name, a, b, *, tm=1024, tn=1024, tk=1024, col_shards=False, a_fn=None):
    t, m = a.shape
    pieces = list(b) if isinstance(b, (list, tuple)) else [b]
    widths = [p.shape[1] for p in pieces]
    offs = [sum(widths[:p]) for p in range(len(pieces))]
    n = sum(widths)
    tm, tk = _tile(m, tm), _tile(t, tk, 8)
    per_tile = 1
    if col_shards:
        wb = n // N_DEV
        if len(pieces) > 1:
            tn = n
        while 2 * per_tile * wb <= tn and 2 * per_tile <= N_DEV:
            per_tile *= 2
        tn = per_tile * wb
        out_spec = pl.BlockSpec((per_tile, tm, wb), lambda i, j, kk: (j, i, 0))
        out_shape = jax.ShapeDtypeStruct((N_DEV, m, wb), MM)
    else:
        tn = _tile(n, tn)
        out_spec = pl.BlockSpec((tm, tn), lambda i, j, kk: (i, j))
        out_shape = jax.ShapeDtypeStruct((m, n), MM)
    nk = t // tk

    assert len(pieces) == 1 or tn == n

    def body(a_ref, *rest):
        b_refs, (o_ref, acc) = rest[:len(pieces)], rest[len(pieces):]
        kk = pl.program_id(2)

        @pl.when(kk == 0)
        def _():
            acc[...] = jnp.zeros_like(acc)

        av = (a_ref[...] if a_fn is None else a_fn(a_ref[...])).astype(MM)
        if len(pieces) == 1:
            acc[...] += _dot_tn(av, b_refs[0][...].astype(MM))
        else:
            for b_ref, off, wd in zip(b_refs, offs, widths):
                acc[:, off:off + wd] += _dot_tn(av, b_ref[...].astype(MM))

        @pl.when(kk == nk - 1)
        def _():
            if col_shards:
                for sh in range(per_tile):
                    o_ref[sh] = acc[:, sh * wb:(sh + 1) * wb].astype(o_ref.dtype)
            else:
                o_ref[...] = acc[...].astype(o_ref.dtype)

    return pl.pallas_call(
        body, name=name, grid=(m // tm, n // tn, nk),
        in_specs=[pl.BlockSpec((tk, tm), lambda i, j, kk: (kk, i))]
        + [pl.BlockSpec((tk, tn if len(pieces) == 1 else wd), lambda i, j, kk: (kk, j)) for wd in widths],
        out_specs=out_spec, out_shape=out_shape,
        scratch_shapes=[pltpu.VMEM((tm, tn), F32)],
        compiler_params=_params(3))(a, *pieces)


def _rms_prologue(x, g):
    r = lax.rsqrt(jnp.mean(x * x, axis=-1, keepdims=True) + EPS)
    return x * r * g


def _rms_bwd_rows(dh, x, g):
    d = x.shape[-1]
    r = lax.rsqrt(jnp.mean(x * x, axis=-1, keepdims=True) + EPS)
    xh = x * r
    dxh = dh * g
    dx = r * (dxh - xh * (jnp.sum(dxh * xh, axis=-1, keepdims=True) / d))
    dg = jnp.sum(dh * xh, axis=0, keepdims=True)
    return dx, dg


def _rms_bwd(name, dh, x, g, dres, *, tm=512):
    m, d = x.shape
    tm = min(tm, m)
    has_res = dres is not None

    def body(*refs):
        if has_res:
            dh_ref, x_ref, g_ref, r_ref, dx_ref, dg_ref = refs
        else:
            dh_ref, x_ref, g_ref, dx_ref, dg_ref = refs
        dx, dg = _rms_bwd_rows(dh_ref[...], x_ref[...], g_ref[...])
        if has_res:
            dx = dx + r_ref[...]
        dx_ref[...] = dx

        @pl.when(pl.program_id(0) == 0)
        def _():
            dg_ref[...] = jnp.zeros_like(dg_ref)

        dg_ref[...] += dg

    row = pl.BlockSpec((tm, d), lambda i: (i, 0))
    vec = pl.BlockSpec((1, d), lambda i: (0, 0))
    in_specs = [row, row, vec] + ([row] if has_res else [])
    args = [dh, x, g] + ([dres] if has_res else [])
    return pl.pallas_call(body, name=name, grid=(m // tm,), in_specs=in_specs, out_specs=[row, vec],
                          out_shape=[jax.ShapeDtypeStruct((m, d), F32), jax.ShapeDtypeStruct((1, d), F32)],
                          compiler_params=_params(1))(*args)


def _mm_rms_bwd(name, a, w, x, g, dres, *, tm, after=()):
    pieces = list(a) if isinstance(a, (list, tuple)) else [a]
    widths = [p.shape[1] for p in pieces]
    m = pieces[0].shape[0]
    d = w.shape[0]
    tm = _tile(m, tm, 8)
    n_a = len(pieces)

    def body(*refs):
        a_refs = refs[:n_a]
        w_ref, x_ref, g_ref, r_ref = refs[n_a:n_a + 4]
        dx_ref, dg_ref = refs[n_a + 4 + len(after):]

        av = a_refs[0][...] if n_a == 1 else jnp.concatenate([a_ref[...] for a_ref in a_refs], axis=1)
        dh = _dot_nt(av, w_ref[...])
        dx, dg = _rms_bwd_rows(dh, x_ref[...], g_ref[...])
        dx_ref[...] = dx + r_ref[...]

        @pl.when(pl.program_id(0) == 0)
        def _():
            dg_ref[...] = jnp.zeros_like(dg_ref)

        dg_ref[...] += dg

    row = pl.BlockSpec((tm, d), lambda i: (i, 0))
    vec = pl.BlockSpec((1, d), lambda i: (0, 0))
    return pl.pallas_call(
        body, name=name, grid=(m // tm,),
        in_specs=[pl.BlockSpec((tm, wd), lambda i: (i, 0)) for wd in widths]
        + [pl.BlockSpec(w.shape, lambda i: (0, 0)), row, vec, row]
        + [pl.BlockSpec(memory_space=pl.ANY)] * len(after),
        out_specs=[row, vec],
        out_shape=[jax.ShapeDtypeStruct((m, d), F32), jax.ShapeDtypeStruct((1, d), F32)],
        compiler_params=_params(1))(*pieces, w, x, g, dres, *after)


def _loss_head(x, target, g, *, tm=256):
    m, d = x.shape
    tm = min(tm, m)

    def body(x_ref, t_ref, g_ref, dx_ref, dg_ref, loss_ref):
        xv, gv = x_ref[...], g_ref[...]
        y = _rms_prologue(xv, gv)
        err = y - t_ref[...]
        part = 0.5 * jnp.sum(jnp.sum(err * err, axis=-1, keepdims=True) / d)
        dx, dg = _rms_bwd_rows(err / d, xv, gv)
        dx_ref[...] = dx

        @pl.when(pl.program_id(0) == 0)
        def _():
            dg_ref[...] = jnp.zeros_like(dg_ref)
            loss_ref[...] = jnp.zeros_like(loss_ref)

        dg_ref[...] += dg
        loss_ref[...] += jnp.full(loss_ref.shape, part, F32)

    row = pl.BlockSpec((tm, d), lambda i: (i, 0))
    vec = pl.BlockSpec((1, d), lambda i: (0, 0))
    lspec = pl.BlockSpec((1, 128), lambda i: (0, 0))
    return pl.pallas_call(body, name="loss_head", grid=(m // tm,), in_specs=[row, row, vec],
                          out_specs=[row, vec, lspec],
                          out_shape=[jax.ShapeDtypeStruct((m, d), F32), jax.ShapeDtypeStruct((1, d), F32),
                                     jax.ShapeDtypeStruct((1, 128), F32)],
                          compiler_params=_params(1))(x, target, g)


def _rot(xv, cos2, sin2, half):
    return xv * cos2 + pltpu.roll(xv, half, 1) * sin2


def _rot_t(dv, cos2, sin2, half):
    return dv * cos2 + pltpu.roll(dv * sin2, half, 1)


def _ret_pre(proj, cos2, sin2, d, seq, *, ts=512):
    t = proj.shape[0]
    ts = min(ts, seq)
    dk = d // 8
    ns = seq // ts
    scale = float(dk) ** -0.5

    def body(q_ref, k_ref, v_ref, c_ref, s_ref, qo, ko, vo):
        c, s = c_ref[...], s_ref[...]
        for h in range(HEADS):
            sl = slice(h * dk, (h + 1) * dk)
            qo[:, sl] = _rot(q_ref[:, sl].astype(F32), c, s, dk // 2).astype(MM)
            ko[:, sl] = (_rot(k_ref[:, sl].astype(F32), c, s, dk // 2) * scale).astype(MM)
        vo[...] = v_ref[...].astype(MM)

    half = pl.BlockSpec((ts, d // 2), lambda i: (i, 0))
    tab = pl.BlockSpec((ts, dk), lambda i: (i % ns, 0))
    return pl.pallas_call(
        body, name="ret_pre", grid=(t // ts,),
        in_specs=[half, pl.BlockSpec((ts, d // 2), lambda i: (i, 1)), pl.BlockSpec((ts, d), lambda i: (i, 1)),
                  tab, tab],
        out_specs=[half, half, pl.BlockSpec((ts, d), lambda i: (i, 0))],
        out_shape=[jax.ShapeDtypeStruct((t, d // 2), MM), jax.ShapeDtypeStruct((t, d // 2), MM),
                   jax.ShapeDtypeStruct((t, d), MM)],
        compiler_params=_params(1))(proj, proj, proj, cos2, sin2)


def _ret_consts(lg_ref, h, t, dk):
    lf, lb = lg_ref[0, h], lg_ref[1, h]
    ab = (lax.broadcasted_iota(jnp.int32, (t, t), 0) - lax.broadcasted_iota(jnp.int32, (t, t), 1)).astype(F32)
    dmat = jnp.exp(jnp.where(ab >= 0, lf * ab, -lb * ab))
    up = lax.broadcasted_iota(jnp.int32, (t, dk), 0).astype(F32) + 1.0
    down = float(t) - up
    one = jnp.ones((1, 1), F32)
    return dict(ab=ab, dmat=dmat, xi_f=jnp.exp(lf * up), zeta_f=jnp.exp(lf * down), xi_b=jnp.exp(lb * up),
                zeta_b=jnp.exp(lb * down), up=up[:, 0:1], down=down[:, 0:1],
                cf=jnp.exp(one * (lf * t)), cb=jnp.exp(one * (lb * t)))


def _scaled(xv, rows):
    return (xv.astype(F32) * rows).astype(MM)


def _ret_core_fwd(qr, kr, vb, proj, lg, d, bl, seq, *, tc=256):
    t = qr.shape[0]
    dk, dv = d // 8, d // 4
    tc = min(tc, seq)
    nc = seq // tc

    def body(lg_ref, q_ref, k_ref, v_ref, g_ref, o_ref, a_ref, sf_ref, sb_ref):
        c = _ret_consts(lg_ref, pl.program_id(1), tc, dk)

        def rows_of(i):
            return pl.ds(pl.multiple_of(i * tc, tc), tc)

        def fwd_step(i, sf):
            rows = rows_of(i)
            sf_ref[i] = sf
            q, kk, v = q_ref[rows, :], k_ref[rows, :], v_ref[rows, :]
            p = (_dot_nt(q, kk) * c["dmat"]).astype(MM)
            o_ref[rows, :] = _dot(p, v) + _dot(_scaled(q, c["xi_f"]), sf.astype(MM))
            return sf * c["cf"] + _dot_tn(_scaled(kk, c["zeta_f"]), v)

        lax.fori_loop(0, nc, fwd_step, jnp.zeros((dk, dv), F32))

        def bwd_step(ii, sb):
            rows = rows_of(nc - 1 - ii)
            sb_ref[nc - 1 - ii] = sb
            q, kk, v = q_ref[rows, :], k_ref[rows, :], v_ref[rows, :]
            o_ref[rows, :] += _dot(_scaled(q, c["zeta_b"]), sb.astype(MM))
            return sb * c["cb"] + _dot_tn(_scaled(kk, c["xi_b"]), v)

        lax.fori_loop(0, nc, bwd_step, jnp.zeros((dk, dv), F32))

        def post(i, carry):
            rows = rows_of(i)
            o = o_ref[rows, :]
            oc = o - jnp.mean(o, axis=-1, keepdims=True)
            on = oc * lax.rsqrt(jnp.mean(oc * oc, axis=-1, keepdims=True) + EPS)
            g = g_ref[rows, :].astype(F32)
            a_ref[rows, :] = (on * (g * _sigmoid(g))).astype(MM)
            return carry

        lax.fori_loop(0, nc, post, 0)

    qk = pl.BlockSpec((seq, dk), lambda b, h: (b, h))
    vv = pl.BlockSpec((seq, dv), lambda b, h: (b, h))
    states = pl.BlockSpec((None, nc, dk, dv), lambda b, h: (b * HEADS + h, 0, 0, 0))
    return pl.pallas_call(
        body, name="ret_core_fwd", grid=(bl, HEADS),
        in_specs=[pl.BlockSpec(memory_space=pltpu.SMEM), qk, qk, vv,
                  pl.BlockSpec((seq, dv), lambda b, h: (b, 2 * HEADS + h))],
        out_specs=[vv, vv, states, states],
        out_shape=[jax.ShapeDtypeStruct((t, d), F32), jax.ShapeDtypeStruct((t, d), MM),
                   jax.ShapeDtypeStruct((bl * HEADS, nc, dk, dv), F32),
                   jax.ShapeDtypeStruct((bl * HEADS, nc, dk, dv), F32)],
        compiler_params=_params(2))(lg, qr, kr, vb, proj)


def _ret_post_bwd(da, proj, o_raw, d, *, ts=2048):
    t = da.shape[0]
    dv = d // 4
    ts = min(ts, t)

    def body(da_ref, g_ref, o_ref, dg_ref, do_ref):
        o, g, dav = o_ref[...], g_ref[...].astype(F32), da_ref[...]
        mu = jnp.mean(o, axis=-1, keepdims=True)
        oc = o - mu
        r = lax.rsqrt(jnp.mean(oc * oc, axis=-1, keepdims=True) + EPS)
        on = oc * r
        sg = _sigmoid(g)
        don = dav * (g * sg)
        dg_ref[...] = (dav * on * (sg * (1.0 + g * (1.0 - sg)))).astype(MM)
        do = r * (don - jnp.mean(don, axis=-1, keepdims=True) - on * jnp.mean(don * on, axis=-1, keepdims=True))
        do_ref[...] = do.astype(MM)

    blk = pl.BlockSpec((ts, dv), lambda i, h: (i, h))
    return pl.pallas_call(
        body, name="ret_post_bwd", grid=(t // ts, HEADS),
        in_specs=[blk, pl.BlockSpec((ts, dv), lambda i, h: (i, 2 * HEADS + h)), blk],
        out_specs=[blk, blk],
        out_shape=[jax.ShapeDtypeStruct((t, d), MM), jax.ShapeDtypeStruct((t, d), MM)],
        compiler_params=_params(2))(da, proj, o_raw)


def _ret_core_bwd(qr, kr, vb, do, sf_in, sb_in, cos2, sin2, lg, d, bl, seq, *, tc=256):
    t = qr.shape[0]
    dk, dv = d // 8, d // 4
    tc = min(tc, seq)
    nc = seq // tc
    scale = float(dk) ** -0.5

    def body(lg_ref, q_ref, k_ref, v_ref, do_ref, sf_all, sb_all, c_ref, s_ref, dq_ref, dk_ref, dv_ref,
             dlf_ref, dlb_ref, dq_acc, dk_acc, dv_acc):
        c = _ret_consts(lg_ref, pl.program_id(1), tc, dk)
        fwd = c["ab"] >= 0
        zero_state = jnp.zeros((dk, dv), F32)
        zero = jnp.zeros((1, 1), F32)

        def rows_of(i):
            return pl.ds(pl.multiple_of(i * tc, tc), tc)

        def total(xv):
            return jnp.sum(xv, keepdims=True)

        def fwd_sweep(i, carry):
            hh, dlf, dlb = carry
            rows = rows_of(i)
            q, kk, v, dov = q_ref[rows, :], k_ref[rows, :], v_ref[rows, :], do_ref[rows, :]
            dof, vf = dov.astype(F32), v.astype(F32)
            p = _dot_nt(q, kk) * c["dmat"]
            da = _dot_nt(dov, v)
            x = p * da * c["ab"]
            dlf = dlf + total(jnp.where(fwd, x, 0.0))
            dlb = dlb - total(jnp.where(fwd, 0.0, x))
            pb, dpb = p.astype(MM), (da * c["dmat"]).astype(MM)
            dq = _dot(dpb, kk)
            dkc = _dot_tn(dpb, q)
            dvc = _dot_tn(pb, dov)
            sf, sb = sf_all[i], sb_all[i]
            sfb, sbb = sf.astype(MM), sb.astype(MM)
            q_xf, q_zb = _scaled(q, c["xi_f"]), _scaled(q, c["zeta_b"])
            dq = dq + _dot_nt(dov, sfb) * c["xi_f"] + _dot_nt(dov, sbb) * c["zeta_b"]
            dlf = dlf + total(jnp.sum(_dot(q_xf, sfb) * dof, axis=-1, keepdims=True) * c["up"])
            dlb = dlb + total(jnp.sum(_dot(q_zb, sbb) * dof, axis=-1, keepdims=True) * c["down"])
            hb = hh.astype(MM)
            dkc = dkc + _dot_nt(v, hb) * c["xi_b"]
            dv_bx = _dot(_scaled(kk, c["xi_b"]), hb)
            dlb = dlb + total(jnp.sum(vf * dv_bx, axis=-1, keepdims=True) * c["up"])
            dlb = dlb + float(tc) * total(hh * (sb * c["cb"]))
            dq_acc[rows, :] = dq
            dk_acc[rows, :] = dkc
            dv_acc[rows, :] = dvc + dv_bx
            return hh * c["cb"] + _dot_tn(q_zb, dov), dlf, dlb

        _, dlf, dlb = lax.fori_loop(0, nc, fwd_sweep, (zero_state, zero, zero))

        def rev_sweep(ii, carry):
            gg, dlf = carry
            i = nc - 1 - ii
            rows = rows_of(i)
            q, kk, v, dov = q_ref[rows, :], k_ref[rows, :], v_ref[rows, :], do_ref[rows, :]
            gb = gg.astype(MM)
            dk_acc[rows, :] += _dot_nt(v, gb) * c["zeta_f"]
            dv_fx = _dot(_scaled(kk, c["zeta_f"]), gb)
            dv_acc[rows, :] += dv_fx
            dlf = dlf + total(jnp.sum(v.astype(F32) * dv_fx, axis=-1, keepdims=True) * c["down"])
            dlf = dlf + float(tc) * total(gg * (sf_all[i] * c["cf"]))
            return gg * c["cf"] + _dot_tn(_scaled(q, c["xi_f"]), dov), dlf

        _, dlf = lax.fori_loop(0, nc, rev_sweep, (zero_state, dlf))

        cs, sn = c_ref[...], s_ref[...]
        dq_ref[...] = _rot_t(dq_acc[...], cs, sn, dk // 2).astype(MM)
        dk_ref[...] = (_rot_t(dk_acc[...], cs, sn, dk // 2) * scale).astype(MM)
        dv_ref[...] = dv_acc[...].astype(MM)
        dlf_ref[...] = jnp.broadcast_to(dlf, dlf_ref.shape)
        dlb_ref[...] = jnp.broadcast_to(dlb, dlb_ref.shape)

    qk = pl.BlockSpec((seq, dk), lambda b, h: (b, h))
    vv = pl.BlockSpec((seq, dv), lambda b, h: (b, h))
    tab = pl.BlockSpec((seq, dk), lambda b, h: (0, 0))
    dl = pl.BlockSpec((None, 8, 128), lambda b, h: (b * HEADS + h, 0, 0))
    states = pl.BlockSpec((None, nc, dk, dv), lambda b, h: (b * HEADS + h, 0, 0, 0))
    return pl.pallas_call(
        body, name="ret_core_bwd", grid=(bl, HEADS),
        in_specs=[pl.BlockSpec(memory_space=pltpu.SMEM), qk, qk, vv, vv, states, states, tab, tab],
        out_specs=[qk, qk, vv, dl, dl],
        out_shape=[jax.ShapeDtypeStruct((t, d // 2), MM), jax.ShapeDtypeStruct((t, d // 2), MM),
                   jax.ShapeDtypeStruct((t, d), MM),
                   jax.ShapeDtypeStruct((bl * HEADS, 8, 128), F32), jax.ShapeDtypeStruct((bl * HEADS, 8, 128), F32)],
        scratch_shapes=[pltpu.VMEM((seq, dk), F32), pltpu.VMEM((seq, dk), F32), pltpu.VMEM((seq, dv), F32)],
        compiler_params=_params(2))(lg, qr, kr, vb, do, sf_in, sb_in, cos2, sin2)


def _window_count(row, w, seq):
    return (jnp.minimum(row + w // 2, seq) - jnp.maximum(row - w // 2, 0)).astype(F32)


def _window_sum(pv, row, w, seq, sign):
    acc = None
    for j in range(-(w // 2), w // 2):
        if j == 0:
            term = pv
        else:
            src = row + sign * j
            term = jnp.where((src >= 0) & (src < seq), pltpu.roll(pv, (-sign * j) % seq, 0), 0.0)
        acc = term if acc is None else acc + term
    return acc


def _pool_fwd(proj, w_grp, scale, d, bl, seq):
    t = proj.shape[0]
    dg = d // 8

    def body(p_ref, w_ref, s_ref, y_ref):
        row = lax.broadcasted_iota(jnp.int32, (seq, dg), 0)
        for gi, w in enumerate(POOL_WINDOWS):
            sl = slice(gi * dg, (gi + 1) * dg)
            pg = p_ref[:, sl].astype(F32)
            mixed = _window_sum(pg, row, w, seq, 1) / _window_count(row, w, seq) - pg
            yp = _dot(mixed.astype(MM), w_ref[gi].astype(MM))
            y_ref[:, sl] = (yp * s_ref[:, sl]).astype(MM)

    return pl.pallas_call(
        body, name="pool_fwd", grid=(bl,),
        in_specs=[pl.BlockSpec((seq, d // 2), lambda b: (b, 6)),
                  pl.BlockSpec(w_grp.shape, lambda b: (0, 0, 0)),
                  pl.BlockSpec((1, d // 2), lambda b: (0, 0))],
        out_specs=pl.BlockSpec((seq, d // 2), lambda b: (b, 0)),
        out_shape=jax.ShapeDtypeStruct((t, d // 2), MM),
        compiler_params=_params(1))(proj, w_grp, scale)


def _pool_bwd(proj, dy, w_grp, scale, d, bl, seq):
    t = proj.shape[0]
    dg = d // 8

    def body(p_ref, dy_ref, w_ref, s_ref, dp_ref, dw_ref, ds_ref):
        @pl.when(pl.program_id(0) == 0)
        def _():
            dw_ref[...] = jnp.zeros_like(dw_ref)
            ds_ref[...] = jnp.zeros_like(ds_ref)

        row = lax.broadcasted_iota(jnp.int32, (seq, dg), 0)
        for gi, w in enumerate(POOL_WINDOWS):
            sl = slice(gi * dg, (gi + 1) * dg)
            pg = p_ref[:, sl].astype(F32)
            cnt = _window_count(row, w, seq)
            mixb = (_window_sum(pg, row, w, seq, 1) / cnt - pg).astype(MM)
            wgb = w_ref[gi].astype(MM)
            yp = _dot(mixb, wgb)
            dyg = dy_ref[:, sl]
            ds_ref[:, sl] += jnp.sum(dyg * yp, axis=0, keepdims=True)
            dyp = (dyg * s_ref[:, sl]).astype(MM)
            dmixed = _dot_nt(dyp, wgb)
            dw_ref[gi] += _dot_tn(mixb, dyp)
            dp_ref[:, sl] = (_window_sum(dmixed / cnt, row, w, seq, -1) - dmixed).astype(MM)

    half = pl.BlockSpec((seq, d // 2), lambda b: (b, 0))
    wspec = pl.BlockSpec(w_grp.shape, lambda b: (0, 0, 0))
    sspec = pl.BlockSpec((1, d // 2), lambda b: (0, 0))
    return pl.pallas_call(
        body, name="pool_bwd", grid=(bl,),
        in_specs=[pl.BlockSpec((seq, d // 2), lambda b: (b, 6)), half, wspec, sspec],
        out_specs=[half, wspec, sspec],
        out_shape=[jax.ShapeDtypeStruct((t, d // 2), MM), jax.ShapeDtypeStruct(w_grp.shape, F32),
                   jax.ShapeDtypeStruct((1, d // 2), F32)],
        compiler_params=_params(1))(proj, dy, w_grp, scale)


def _attn_probs(q, kk, dh):
    s = _dot_nt(q, kk) * (float(dh) ** -0.5)
    e = jnp.exp(s - jnp.max(s, axis=-1, keepdims=True))
    return e / jnp.sum(e, axis=-1, keepdims=True)


def _attn_fwd(proj, kv, d, bl, seq, mlen, *, tq=2048):
    t = proj.shape[0]
    dh = d // 8
    tq = min(tq, seq)
    nq = seq // tq

    def body(q_ref, k_ref, v_ref, o_ref):
        a = _attn_probs(q_ref[...].astype(MM), k_ref[...].astype(MM), dh)
        o_ref[...] = _dot(a.astype(MM), v_ref[...].astype(MM)).astype(MM)

    return pl.pallas_call(
        body, name="attn_fwd", grid=(bl, HEADS, nq),
        in_specs=[pl.BlockSpec((tq, dh), lambda b, h, i: (b * nq + i, 7 * HEADS + h)),
                  pl.BlockSpec((mlen, dh), lambda b, h, i: (b, h)),
                  pl.BlockSpec((mlen, dh), lambda b, h, i: (b, HEADS + h))],
        out_specs=pl.BlockSpec((tq, dh), lambda b, h, i: (b * nq + i, h)),
        out_shape=jax.ShapeDtypeStruct((t, d // 2), MM),
        compiler_params=_params(3))(proj, kv, kv)


def _attn_bwd(proj, kv, do, d, bl, seq, mlen, *, tq=2048):
    t = proj.shape[0]
    dh = d // 8
    tq = min(tq, seq)
    nq = seq // tq

    def body(q_ref, k_ref, v_ref, do_ref, dq_ref, dk_ref, dv_ref):
        @pl.when(pl.program_id(2) == 0)
        def _():
            dk_ref[...] = jnp.zeros_like(dk_ref)
            dv_ref[...] = jnp.zeros_like(dv_ref)

        q, kk, vv = q_ref[...].astype(MM), k_ref[...].astype(MM), v_ref[...].astype(MM)
        dov = do_ref[...].astype(MM)
        a = _attn_probs(q, kk, dh)
        dp = _dot_nt(dov, vv)
        ds = (a * (dp - jnp.sum(dp * a, axis=-1, keepdims=True)) * (float(dh) ** -0.5)).astype(MM)
        dq_ref[...] = _dot(ds, kk).astype(MM)
        dk_ref[...] += _dot_tn(ds, q)
        dv_ref[...] += _dot_tn(a.astype(MM), dov)

    qs = pl.BlockSpec((tq, dh), lambda b, h, i: (b * nq + i, h))
    ms = pl.BlockSpec((mlen, dh), lambda b, h, i: (b, h))
    return pl.pallas_call(
        body, name="attn_bwd", grid=(bl, HEADS, nq),
        in_specs=[pl.BlockSpec((tq, dh), lambda b, h, i: (b * nq + i, 7 * HEADS + h)), ms,
                  pl.BlockSpec((mlen, dh), lambda b, h, i: (b, HEADS + h)), qs],
        out_specs=[qs, ms, ms],
        out_shape=[jax.ShapeDtypeStruct((t, d // 2), MM), jax.ShapeDtypeStruct((bl * mlen, d // 2), F32),
                   jax.ShapeDtypeStruct((bl * mlen, d // 2), F32)],
        compiler_params=_params(3))(proj, kv, kv, do)


def _comm_call(name, body, arrays, out_shapes):
    n = len(arrays)
    hbm = pl.BlockSpec(memory_space=pl.ANY)
    return pl.pallas_call(
        body, name=name, out_shape=out_shapes, in_specs=[hbm] * n, out_specs=[hbm] * n,
        scratch_shapes=[pltpu.SemaphoreType.DMA((7 * n,)), pltpu.SemaphoreType.DMA((7 * n,)),
                        pltpu.SemaphoreType.DMA((n,))],
    )(*arrays)


def _all_gather(name, shards):
    n = len(shards)

    def body(*refs):
        x_refs, out_refs = refs[:n], refs[n:2 * n]
        send_sems, recv_sems, local_sems = refs[2 * n:]
        x, y, c = lax.axis_index("x"), lax.axis_index("y"), lax.axis_index("c")
        me, sibling = (x, y, c), (x, y, 1 - c)
        chips = [(1 - x, y), (x, 1 - y), (1 - x, 1 - y)]

        def copy(o, k, block, to, src=None):
            slot = out_refs[o].at[4 * block[0] + 2 * block[1] + block[2]]
            return pltpu.make_async_remote_copy(
                src_ref=slot if src is None else src, dst_ref=slot, send_sem=send_sems.at[7 * o + k],
                recv_sem=recv_sems.at[7 * o + k], device_id=to, device_id_type=MESH)

        locals_, remotes = [], []
        for o in range(n):
            mine = pltpu.make_async_copy(x_refs[o], out_refs[o].at[4 * x + 2 * y + c], local_sems.at[o])
            mine.start()
            locals_.append(mine)
            first = [copy(o, 0, me, sibling, src=x_refs[o])]
            first += [copy(o, 1 + j, me, (*chip, c), src=x_refs[o]) for j, chip in enumerate(chips)]
            for cp in first:
                cp.start()
            remotes += first
        for o in range(n):
            for j, chip in enumerate(chips):
                copy(o, 1 + j, (*chip, c), me).wait_recv()
                passed = copy(o, 4 + j, (*chip, c), sibling)
                passed.start()
                remotes.append(passed)
        for o in range(n):
            copy(o, 0, sibling, me).wait_recv()
            for j, chip in enumerate(chips):
                copy(o, 4 + j, (*chip, 1 - c), me).wait_recv()
        for cp in remotes:
            cp.wait_send()
        for mine in locals_:
            mine.wait()

    outs = [jax.ShapeDtypeStruct((N_DEV,) + s.shape, s.dtype) for s in shards]
    return _comm_call(name, body, shards, outs)


def _columns_side_by_side(name, g):
    _, k, wb = g.shape

    def body(x_ref, o_ref):
        o_ref[...] = x_ref[...]

    return pl.pallas_call(
        body, name=name, grid=(N_DEV,), in_specs=[pl.BlockSpec((None, k, wb), lambda j: (j, 0, 0))],
        out_specs=pl.BlockSpec((k, wb), lambda j: (0, j)),
        out_shape=jax.ShapeDtypeStruct((k, N_DEV * wb), g.dtype), compiler_params=_params(1))(g)


def _peer_of(k, x, y, c):
    peer = (1 - x if k & 4 else x, 1 - y if k & 2 else y, 1 - c if k & 1 else c)
    return peer, 4 * peer[0] + 2 * peer[1] + peer[2]


def _split_copies(scatter, srcs, lands, send_sems, recv_sems, arriving):
    x, y, c = lax.axis_index("x"), lax.axis_index("y"), lax.axis_index("c")
    me_idx = 4 * x + 2 * y + c
    copies = []
    for o, (src, land) in enumerate(zip(srcs, lands)):
        for k in range(1, N_DEV):
            peer, p_idx = _peer_of(k, x, y, c)
            mine = src.at[p_idx] if scatter else src
            sems = dict(send_sem=send_sems.at[7 * o + k - 1], recv_sem=recv_sems.at[7 * o + k - 1],
                        device_id=peer, device_id_type=MESH)
            slot = land.at[p_idx] if arriving else land.at[me_idx]
            copies.append(pltpu.make_async_remote_copy(src_ref=mine, dst_ref=slot, **sems))
    return copies


_HBM = pl.BlockSpec(memory_space=pltpu.HBM)
_SEM = pl.BlockSpec(memory_space=pltpu.SEMAPHORE)
_EFFECT = pltpu.SideEffectType.DATAFLOW_SIDE_EFFECTING


def _own_slot_copies(scatter, srcs, lands, local_sems):
    me_idx = 4 * lax.axis_index("x") + 2 * lax.axis_index("y") + lax.axis_index("c")
    return [pltpu.make_async_copy(src.at[me_idx] if scatter else src, land.at[me_idx], local_sems.at[o])
            for o, (src, land) in enumerate(zip(srcs, lands))]


def _exchange_start(name, scatter, arrays, after=()):
    n = len(arrays)
    lands = [lax.empty(a.shape if scatter else (N_DEV,) + a.shape, a.dtype) for a in arrays]

    def body(*refs):
        srcs, lnds = refs[:n], refs[n:2 * n]
        send_sems, recv_sems, local_sems = refs[2 * n + len(after):2 * n + len(after) + 3]
        token = refs[-1]
        for cp in _split_copies(scatter, srcs, lnds, send_sems, recv_sems, False):
            cp.start()
        for cp in _own_slot_copies(scatter, srcs, lnds, local_sems):
            cp.start()
        token[...] = jnp.zeros_like(token)

    hbm_in = [pltpu.with_memory_space_constraint(a, pltpu.HBM) for a in list(arrays) + lands]
    res = pl.pallas_call(
        body, name=name,
        out_shape=(pltpu.SemaphoreType.DMA((7 * n,)), pltpu.SemaphoreType.DMA((7 * n,)), pltpu.SemaphoreType.DMA((n,)),
                   *[pltpu.HBM(a.shape, a.dtype) for a in hbm_in], jax.ShapeDtypeStruct((8, 128), F32)),
        in_specs=[_HBM] * (2 * n) + [pl.BlockSpec(memory_space=pl.ANY)] * len(after),
        out_specs=(_SEM, _SEM, _SEM, *[_HBM] * (2 * n), pl.BlockSpec(memory_space=pltpu.VMEM)),
        input_output_aliases={i: 3 + i for i in range(2 * n)},
        compiler_params=pltpu.CompilerParams(has_side_effects=_EFFECT),
    )(*hbm_in, *after)
    return res[:3], None, list(res[3:3 + n]), list(res[3 + n:3 + 2 * n]), res[-1]


def _exchange_wait(name, scatter, started, after):
    sems, _, srcs, lands, _ = started
    n = len(srcs)

    def body(*refs):
        src_refs, lnd_refs = refs[:n], refs[n:2 * n]
        send_sems, recv_sems, local_sems = refs[2 * n:2 * n + 3]
        for cp in _split_copies(scatter, src_refs, lnd_refs, send_sems, recv_sems, False):
            cp.wait_send()
        for cp in _split_copies(scatter, src_refs, lnd_refs, send_sems, recv_sems, True):
            cp.wait_recv()
        for cp in _own_slot_copies(scatter, src_refs, lnd_refs, local_sems):
            cp.wait()

    res = pl.pallas_call(
        body, name=name, out_shape=tuple(pltpu.HBM(a.shape, a.dtype) for a in srcs + lands),
        in_specs=[_HBM] * (2 * n) + [_SEM, _SEM, _SEM, pl.BlockSpec(memory_space=pl.ANY)],
        out_specs=tuple([_HBM] * (2 * n)), input_output_aliases={i: i for i in range(2 * n)},
        compiler_params=pltpu.CompilerParams(has_side_effects=_EFFECT),
    )(*srcs, *lands, *sems, after)
    return list(res[n:])


def _adamw(name, parts, w, m, v, prev, layer, *, tr=256):
    _, a, b = w.shape
    tr = _tile(a, tr, 8)
    c1 = 1.0 - ADAM_B1 ** ADAM_STEP
    c2 = 1.0 - ADAM_B2 ** ADAM_STEP

    def body(p_ref, w_ref, m_ref, v_ref, _g, _d, _m, _v, g_out, d_out, m_out, v_out):
        g = p_ref[0].astype(F32)
        for s in range(1, N_DEV):
            g = g + p_ref[s].astype(F32)
        mn = ADAM_B1 * m_ref[...] + (1.0 - ADAM_B1) * g
        vn = ADAM_B2 * v_ref[...] + (1.0 - ADAM_B2) * (g * g)
        g_out[...] = g
        m_out[...] = mn
        v_out[...] = vn
        d_out[...] = -ADAM_LR * ((mn / c1) / (jnp.sqrt(vn / c2) + ADAM_EPS) + ADAM_WD * w_ref[...])

    slab = pl.BlockSpec((None, tr, b), lambda i: (layer, i, 0))
    whole = pl.BlockSpec(memory_space=pl.ANY)
    return pl.pallas_call(
        body, name=name, grid=(a // tr,),
        in_specs=[pl.BlockSpec((N_DEV, tr, b), lambda i: (0, i, 0)), slab, slab, slab] + [whole] * 4,
        out_specs=[slab] * 4, out_shape=[jax.ShapeDtypeStruct(w.shape, F32)] * 4,
        input_output_aliases={4: 0, 5: 1, 6: 2, 7: 3},
        compiler_params=_params(1))(parts, w, m, v, *prev)


_COL = ("w_in", "w_pool_o", "w_mem_o", "w_ff1")
_DW_SHARDED = ("w_in", "w_ff1")
_BIG =("w_in", "w_ret_o", "w_pool_o", "w_mem_kv", "w_mem_o", "w_out", "w_ff1", "w_ff2")
_SMALL = ("ret_decay_logit", "w_pool_grp", "pool_scale", "norm1_g", "norm2_g", "mem_norm_g", "final_norm_g")
_SMALL_MM = ("w_pool_grp",)
_SMALL_F32 = tuple(n for n in _SMALL if n not in _SMALL_MM)
_WEIGHTS = ("w_in", "ret_decay_logit", "w_ret_o", "w_pool_grp", "pool_scale", "w_pool_o", "w_mem_kv", "w_mem_o",
            "w_out", "w_ff1", "w_ff2", "norm1_g", "norm2_g", "mem_norm_g", "final_norm_g")


def _small_rows(size, d):
    return -(-size // (8 * d)) * 8


def _pack_small(ws, d, names):
    parts = []
    for n in names:
        flat = ws[n].reshape(-1)
        rows = _small_rows(flat.shape[0], d)
        parts.append(jnp.pad(flat, (0, rows * d - flat.shape[0])).reshape(rows, d))
    return jnp.concatenate(parts, axis=0)[None]


def _unpack_small(packed, like, d, names):
    out, off = {}, 0
    for n in names:
        rows = _small_rows(like[n].size, d)
        out[n] = packed[0, off:off + rows].reshape(-1)[:like[n].size].reshape(like[n].shape)
        off += rows
    return out


def kernel(x, mem, w_in, ret_decay_logit, w_ret_o, w_pool_grp, pool_scale, w_pool_o, w_mem_kv, w_mem_o, w_out, w_ff1, w_ff2, norm1_g, norm2_g, mem_norm_g, final_norm_g, loss_target, m_w_in, m_ret_decay_logit, m_w_ret_o, m_w_pool_grp, m_pool_scale, m_w_pool_o, m_w_mem_kv, m_w_mem_o, m_w_out, m_w_ff1, m_w_ff2, m_norm1_g, m_norm2_g, m_mem_norm_g, m_final_norm_g, v_w_in, v_ret_decay_logit, v_w_ret_o, v_w_pool_grp, v_pool_scale, v_w_pool_o, v_w_mem_kv, v_w_mem_o, v_w_out, v_w_ff1, v_w_ff2, v_norm1_g, v_norm2_g, v_mem_norm_g, v_final_norm_g):
    w = dict(w_in=w_in, ret_decay_logit=ret_decay_logit, w_ret_o=w_ret_o, w_pool_grp=w_pool_grp,
             pool_scale=pool_scale, w_pool_o=w_pool_o, w_mem_kv=w_mem_kv, w_mem_o=w_mem_o, w_out=w_out,
             w_ff1=w_ff1, w_ff2=w_ff2, norm1_g=norm1_g, norm2_g=norm2_g, mem_norm_g=mem_norm_g,
             final_norm_g=final_norm_g)
    mom = dict(w_in=m_w_in, ret_decay_logit=m_ret_decay_logit, w_ret_o=m_w_ret_o, w_pool_grp=m_w_pool_grp,
               pool_scale=m_pool_scale, w_pool_o=m_w_pool_o, w_mem_kv=m_w_mem_kv, w_mem_o=m_w_mem_o,
               w_out=m_w_out, w_ff1=m_w_ff1, w_ff2=m_w_ff2, norm1_g=m_norm1_g, norm2_g=m_norm2_g,
               mem_norm_g=m_mem_norm_g, final_norm_g=m_final_norm_g)
    vel = dict(w_in=v_w_in, ret_decay_logit=v_ret_decay_logit, w_ret_o=v_w_ret_o, w_pool_grp=v_w_pool_grp,
               pool_scale=v_pool_scale, w_pool_o=v_w_pool_o, w_mem_kv=v_w_mem_kv, w_mem_o=v_w_mem_o,
               w_out=v_w_out, w_ff1=v_w_ff1, w_ff2=v_w_ff2, norm1_g=v_norm1_g, norm2_g=v_norm2_g,
               mem_norm_g=v_mem_norm_g, final_norm_g=v_final_norm_g)

    bl, seq, d = x.shape
    mlen = mem.shape[1]
    depth = w_in.shape[0]
    t = bl * seq
    dk = d // 8

    def natural(n, g):
        if n in _DW_SHARDED:
            return _columns_side_by_side("relayout_" + n, g)
        if n in _COL:
            return jnp.transpose(g, (1, 0, 2)).reshape(g.shape[1], -1)
        return g.reshape(-1, g.shape[-1])

    def finish_gather(name, names, started, after):
        return {n: natural(n, g) for n, g in zip(names, _exchange_wait(name, False, started, after))}

    shards = [{n: w[n][l].astype(MM) for n in _BIG} for l in range(depth)]
    rest = _BIG[1:]
    (w_in0,) = _all_gather("gather_w_in", [shards[0][_BIG[0]]])
    full = [{_BIG[0]: natural(_BIG[0], w_in0)}]
    pending = _exchange_start("gather_start_0", False, [shards[0][n] for n in rest], after=[w_in0])
    first_tokens = (pending[4],)
    pending_next = None
    if depth > 1:
        pending_next = _exchange_start("gather_start_1", False, [shards[1][n] for n in _BIG], after=[pending[4]])
        first_tokens += (pending_next[4],)

    inv = ROPE_BASE ** (-jnp.arange(0, dk, 2, dtype=F32) / dk)
    ang = jnp.arange(seq, dtype=F32)[:, None] * inv[None, :]
    cos2 = jnp.concatenate([jnp.cos(ang), jnp.cos(ang)], axis=-1)
    sin2 = jnp.concatenate([-jnp.sin(ang), jnp.sin(ang)], axis=-1)
    log_g = jax.nn.log_sigmoid(ret_decay_logit)
    x2 = x.reshape(t, d)
    mem2 = mem.reshape(bl * mlen, d)
    gmem = mem_norm_g.reshape(1, d)

    def merge(a_r, y_p, o_a, g_r, g_p, g_m, w_r, w_p, w_m):
        f = lambda z: z.astype(F32)
        o_r, o_p, o_m = _dot(a_r, w_r), _dot(y_p, w_p), _dot(o_a, w_m)
        return _sigmoid(f(g_r)) * o_r + _sigmoid(f(g_p)) * o_p + _sigmoid(f(g_m)) * o_m, o_r, o_p, o_m

    def relu2(u):
        r = jnp.maximum(u.astype(MM), 0.0)
        return r * r

    def ident(a):
        return a

    saved = []
    xc = x2
    for l in range(depth):
        s = dict(x_in=xc)
        started_now = ()
        if l > 0:
            full.append(finish_gather(f"gather_wait_{l}", _BIG, pending, xc))
            if l + 1 < depth:
                pending = _exchange_start(f"gather_start_{l + 1}", False, [shards[l + 1][n] for n in _BIG],
                                          after=[full[l]["w_in"]])
                started_now = (pending[4],)
        fw = full[l]
        g1 = norm1_g[l].reshape(1, d)
        g2 = norm2_g[l].reshape(1, d)
        s["proj"], s["h1"] = _pmm("proj", _rms_prologue, [(xc, d, 0)], [g1], fw["w_in"],
                                  tm=2048, tn=1024, save_a=True, out_dtypes=(MM,),
                                  after=started_now if l > 0 else first_tokens)
        proj = s["proj"]
        s["qr"], s["kr"], s["vb"] = _ret_pre(proj, cos2, sin2, d, seq)
        s["o_raw"], s["a_ret"], s["sf"], s["sb"] = _ret_core_fwd(s["qr"], s["kr"], s["vb"], proj, log_g[l],
                                                                 d, bl, seq)
        s["y"] = _pool_fwd(proj, w_pool_grp[l], pool_scale[l].reshape(1, -1), d, bl, seq)
        started_now = ()
        if l == 0:
            fw.update(finish_gather("gather_wait_0", rest, pending, s["a_ret"]))
            pending = pending_next
        s["kv"], s["memn"] = _pmm("mem_kv", _rms_prologue, [(mem2, d, 0)], [gmem], fw["w_mem_kv"],
                                  tm=512, tn=512, save_a=True, after=started_now)
        s["o_att"] = _attn_fwd(proj, s["kv"], d, bl, seq, mlen)
        s["x_mid"], s["merged"], s["o_ret"], s["o_pool"], s["o_mem"] = _pmm(
            "merge_out", merge,
            [(s["a_ret"], d, 0), (s["y"], d // 2, 0), (s["o_att"], d // 2, 0), (proj, d, 4), (proj, d, 5), (proj, d, 6)],
            [fw["w_ret_o"], fw["w_pool_o"], fw["w_mem_o"]], fw["w_out"], tm=512, tn=1024, residual=xc, save_a=True,
            extra_outs=[(d, MM)] * 3)
        s["u"], s["h2"] = _pmm("ff1", _rms_prologue, [(s["x_mid"], d, 0)], [g2], fw["w_ff1"],
                               tm=512, tn=4096, save_a=True, out_dtypes=(MM,))
        (xc,) = _pmm("ff2", relu2, [(s["u"], s["u"].shape[1], 0)], [], fw["w_ff2"],
                     tm=512, tn=1024, residual=s["x_mid"])
        saved.append(s)

    dxc, g_final, loss_part = _loss_head(xc, loss_target.reshape(t, d), final_norm_g.reshape(1, d))
    loss = lax.psum(loss_part[0, 0], ("x", "y", "c"))

    small_names = ("w_pool_grp", "pool_scale", "norm1_g", "norm2_g", "ret_decay_logit")
    grads = {n: [None] * depth for n in small_names}
    group_a = ("w_ff1", "w_ff2")
    group_b = tuple(n for n in _BIG if n not in group_a)
    scatters = {}
    dmemn = jnp.zeros((bl * mlen, d), F32)

    def relu2_bwd(acc, u):
        return (acc * (2.0 * jnp.maximum(u.astype(F32), 0.0)),)

    def gates_bwd(acc, g_r, g_p, g_m, o_r, o_p, o_m, w_r, w_p, w_m):
        d_os, d_gs, backs = [], [], []
        for gz, oz, wz in ((g_r, o_r, w_r), (g_p, o_p, w_p), (g_m, o_m, w_m)):
            sg = _sigmoid(gz.astype(F32))
            d_o = (acc * sg).astype(MM)
            d_os.append(d_o)
            d_gs.append(acc * oz.astype(F32) * (sg * (1.0 - sg)))
            backs.append(_dot_nt(d_o, wz))
        return tuple(d_os + d_gs + backs)

    def to_send(n, g):
        a, b = w[n].shape[1:]
        if n in _DW_SHARDED:
            return g
        if n in _COL:
            return jnp.transpose(g.reshape(a, N_DEV, b), (1, 0, 2))
        return g.reshape(N_DEV, a, b)

    for l in reversed(range(depth)):
        s = saved[l]
        fw = full[l]
        proj = s["proj"]
        g1 = norm1_g[l].reshape(1, d)
        g2 = norm2_g[l].reshape(1, d)
        dw = {}
        (du,) = _pmm("ff2_bwd", ident, [(dxc, d, 0)], [], fw["w_ff2"], w_mode="nt", tm=512, tn=4096,
                     epilogue=relu2_bwd, epi_ins=[(s["u"], 0)], out_dtypes=(MM,))
        dw["w_ff2"] = _tnmm("dw_ff2", s["u"], dxc, a_fn=relu2)
        dw["w_ff1"] = _tnmm("dw_ff1", s["h2"], du, col_shards=True)
        scatters[l, "a"] = _exchange_start(f"scatter_start_a{l}", True, [to_send(n, dw[n]) for n in group_a])
        dmid, grads["norm2_g"][l] = _mm_rms_bwd("ff1_norm2_bwd", du, fw["w_ff1"], s["x_mid"], g2, dxc, tm=512)
        d_oret, d_opool, d_omem, dgr, dgp, dgm, da_ret, dy, do_att = _pmm(
            "out_bwd", ident, [(dmid, d, 0)], [], fw["w_out"], w_mode="nt", tm=256, tn=d, epilogue=gates_bwd,
            epi_ins=[(proj, 4 * d), (proj, 5 * d), (proj, 6 * d), (s["o_ret"], 0), (s["o_pool"], 0), (s["o_mem"], 0)],
            epi_full=[fw["w_ret_o"], fw["w_pool_o"], fw["w_mem_o"]], out_dtypes=(MM,) * 6 + (F32,) * 3,
            out_widths=[d] * 7 + [d // 2] * 2, after=(scatters[l, "a"][4],))
        dw["w_out"] = _tnmm("dw_out", s["merged"], dmid)
        dw["w_ret_o"] = _tnmm("dw_ret_o", s["a_ret"], d_oret)
        dw["w_pool_o"] = _tnmm("dw_pool_o", s["y"], d_opool)
        dw["w_mem_o"] = _tnmm("dw_mem_o", s["o_att"], d_omem)
        dg_ret, do_ret = _ret_post_bwd(da_ret, proj, s["o_raw"], d)
        dq, dkk, dvv, dlf, dlb = _ret_core_bwd(s["qr"], s["kr"], s["vb"], do_ret, s["sf"], s["sb"], cos2, sin2,
                                               log_g[l], d, bl, seq)
        dl = jnp.stack([dlf[:, 0, 0].reshape(bl, HEADS).sum(0), dlb[:, 0, 0].reshape(bl, HEADS).sum(0)])
        grads["ret_decay_logit"][l] = dl * jax.nn.sigmoid(-ret_decay_logit[l])
        dp, grads["w_pool_grp"][l], dscale = _pool_bwd(proj, dy, w_pool_grp[l], pool_scale[l].reshape(1, -1),
                                                       d, bl, seq)
        grads["pool_scale"][l] = dscale.reshape(-1)
        dqm, dmk, dmv = _attn_bwd(proj, s["kv"], do_att, d, bl, seq, mlen)
        dkv = jnp.concatenate([dmk, dmv], axis=-1).astype(MM)
        dw["w_mem_kv"] = _tnmm("dw_mem_kv", s["memn"], dkv)
        (dmemn,) = _pmm("mem_kv_bwd", None, [(dkv, d, 0)], [], fw["w_mem_kv"], w_mode="nt", tm=512, tn=512,
                        residual=dmemn)
        dproj = [dq, dkk, dvv, dg_ret, dp, dqm, dgr, dgp, dgm]
        dw["w_in"] = _tnmm("dw_in", s["h1"], dproj, col_shards=True, tm=512, tk=512)
        scatters[l, "b"] = _exchange_start(f"scatter_start_b{l}", True, [to_send(n, dw[n]) for n in group_b])
        dxc, grads["norm1_g"][l] = _mm_rms_bwd("proj_norm1_bwd", dproj, fw["w_in"], s["x_in"], g1, dmid, tm=256,
                                               after=(scatters[l, "b"][4],))

    _, g_memn = _rms_bwd("mem_norm_bwd", dmemn, mem2, gmem, None)
    grad_x = dxc.reshape(bl, seq, d)

    small_g = dict(ret_decay_logit=jnp.stack(grads["ret_decay_logit"]), w_pool_grp=jnp.stack(grads["w_pool_grp"]),
                   pool_scale=jnp.stack(grads["pool_scale"]),
                   norm1_g=jnp.concatenate(grads["norm1_g"], axis=0), norm2_g=jnp.concatenate(grads["norm2_g"], axis=0),
                   mem_norm_g=g_memn.reshape(-1), final_norm_g=g_final.reshape(-1))
    small_started = _exchange_start("gather_small_start", False,
                                    [_pack_small(small_g, d, _SMALL_MM)[0].astype(MM),
                                     _pack_small(small_g, d, _SMALL_F32)[0]])

    big = {n: [lax.empty(w[n].shape, F32) for _ in range(4)] for n in _BIG}

    def update(l, grp, names, after):
        recv = _exchange_wait(f"scatter_wait_{grp}{l}", True, scatters[l, grp], after)
        for n, parts in zip(names, recv):
            big[n] = _adamw("adamw_" + n, parts, w[n], mom[n], vel[n], big[n], l)
        return big[names[-1]][0]

    after = dxc
    for l in reversed(range(1, depth)):
        for grp, names in (("a", group_a), ("b", group_b)):
            after = update(l, grp, names, after)
    after = update(0, "a", group_a, after)
    small_lands = _exchange_wait("gather_small_wait", False, small_started, after)
    small = [{} for _ in range(4)]
    for names, parts in zip((_SMALL_MM, _SMALL_F32), small_lands):
        w_small = _pack_small(w, d, names)
        res = _adamw("adamw_small", parts, w_small, _pack_small(mom, d, names), _pack_small(vel, d, names),
                     [lax.empty(w_small.shape, F32) for _ in range(4)], 0)
        after = res[0]
        for k in range(4):
            small[k].update(_unpack_small(res[k], w, d, names))
    update(0, "b", group_b, after)

    outs = [loss, grad_x]
    for k in range(4):
        outs += [big[n][k] if n in _BIG else small[k][n] for n in _WEIGHTS]
    return tuple(outs)
```

```python
import jax
import jax.numpy as jnp
from jax import lax
from jax.experimental import pallas as pl
from jax.experimental.pallas import tpu as pltpu

F32 = jnp.float32
MM = jnp.bfloat16
N_DEV = 8
HEADS = 4
POOL_WINDOWS = (2, 4, 8, 16)
EPS = 1e-6
ROPE_BASE = 10000.0
ADAM_LR, ADAM_B1, ADAM_B2, ADAM_EPS, ADAM_WD, ADAM_STEP = 0.001, 0.9, 0.999, 1e-08, 0.01, 10
V7X_VMEM_LIMIT = 56 * 1024 * 1024
MESH = pl.DeviceIdType.MESH


def _params(n_axes):
    return pltpu.CompilerParams(dimension_semantics=("arbitrary",) * n_axes,
                                vmem_limit_bytes=V7X_VMEM_LIMIT)


def _tile(n, pref, align=128):
    cands = [c for c in range(align, min(pref, n) + 1, align) if n % c == 0]
    return max(cands) if cands else n


def _sigmoid(z):
    return 0.5 * jnp.tanh(0.5 * z) + 0.5


def _dot(a, b):
    return jnp.dot(a, b, preferred_element_type=F32)


def _dot_nt(a, b):
    return lax.dot_general(a, b, (((1,), (1,)), ((), ())), preferred_element_type=F32)


def _dot_tn(a, b):
    return lax.dot_general(a, b, (((0,), (0,)), ((), ())), preferred_element_type=F32)


def _pmm(name, prologue, row_ins, vec_ins, w, *, tm, tn, w_mode="nn", residual=None, save_a=False,
         epilogue=None, epi_ins=(), out_dtypes=(F32,), after=(), extra_outs=(), epi_full=(), out_widths=None):
    m = row_ins[0][0].shape[0]
    if w_mode == "nn":
        k, n = w.shape
        tn = _tile(n, tn)
        w_spec = pl.BlockSpec((k, tn), lambda i, j: (0, j))
    else:
        n, k = w.shape
        tn = _tile(n, tn)
        w_spec = pl.BlockSpec((tn, k), lambda i, j: (j, 0))
    tm = _tile(m, tm, 8)
    n_row, n_vec, n_epi, n_out = len(row_ins), len(vec_ins), len(epi_ins), len(out_dtypes)
    has_res = residual is not None
    use_scr = prologue is not None
    out_widths = [n] * n_out if out_widths is None else list(out_widths)
    assert all(wd == n for wd in out_widths) or tn == n

    def body(*refs):
        row_refs = refs[:n_row]
        p = n_row
        vec_refs = refs[p:p + n_vec]
        p += n_vec
        w_ref = refs[p]
        p += 1
        res_ref = refs[p] if has_res else None
        p += int(has_res)
        epi_refs = refs[p:p + n_epi + len(epi_full)]
        p += n_epi + len(epi_full) + len(after)
        out_refs = refs[p:p + n_out]
        p += n_out
        a_out = refs[p] if save_a else None
        p += int(save_a)
        extra_refs = refs[p:p + len(extra_outs)]
        p += len(extra_outs)
        if use_scr:
            a_src = refs[p]

            @pl.when(pl.program_id(1) == 0)
            def _():
                made = prologue(*[r[...] for r in row_refs], *[v[...] for v in vec_refs])
                made = made if isinstance(made, tuple) else (made,)
                a = made[0].astype(MM)
                a_src[...] = a
                if save_a:
                    a_out[...] = a
                for e_ref, e in zip(extra_refs, made[1:]):
                    e_ref[...] = e.astype(e_ref.dtype)
        else:
            a_src = row_refs[0]
        if w_mode == "nt":
            acc = _dot_nt(a_src[...], w_ref[...])
        else:
            acc = _dot(a_src[...], w_ref[...])
        if has_res:
            acc = acc + res_ref[...]
        outs = epilogue(acc, *[e[...] for e in epi_refs]) if epilogue is not None else (acc,)
        for o_ref, o in zip(out_refs, outs):
            o_ref[...] = o.astype(o_ref.dtype)

    in_specs = [pl.BlockSpec((tm, wd), lambda i, j, cb=cb: (i, cb)) for (_, wd, cb) in row_ins]
    in_specs += [pl.BlockSpec(v.shape, lambda i, j: (0, 0)) for v in vec_ins]
    in_specs += [w_spec]
    args = [r[0] for r in row_ins] + list(vec_ins) + [w]
    if has_res:
        in_specs.append(pl.BlockSpec((tm, tn), lambda i, j: (i, j)))
        args.append(residual)
    for (arr, off) in epi_ins:
        assert off % tn == 0
        in_specs.append(pl.BlockSpec((tm, tn), lambda i, j, ob=off // tn: (i, ob + j)))
        args.append(arr)
    in_specs += [pl.BlockSpec(v.shape, lambda i, j: (0, 0)) for v in epi_full]
    args += list(epi_full)
    n_after = len(after)
    in_specs += [pl.BlockSpec(memory_space=pl.ANY)] * n_after
    args += list(after)
    out_specs = [pl.BlockSpec((tm, tn if wd == n else wd), lambda i, j: (i, j)) for wd in out_widths]
    out_shape = [jax.ShapeDtypeStruct((m, wd), dt) for wd, dt in zip(out_widths, out_dtypes)]
    if save_a:
        out_specs.append(pl.BlockSpec((tm, k), lambda i, j: (i, 0)))
        out_shape.append(jax.ShapeDtypeStruct((m, k), MM))
    for wd, dt in extra_outs:
        out_specs.append(pl.BlockSpec((tm, wd), lambda i, j: (i, 0)))
        out_shape.append(jax.ShapeDtypeStruct((m, wd), dt))
    scratch = [pltpu.VMEM((tm, k), MM)] if use_scr else []
    return pl.pallas_call(body, name=name, grid=(m // tm, n // tn), in_specs=in_specs,
                          out_specs=out_specs, out_shape=out_shape, scratch_shapes=scratch,
                          compiler_params=_params(2))(*args)


def _tnmm(name, a, b, *, tm=1024, tn=1024, tk=1024, col_shards=False, a_fn=None):
    t, m = a.shape
    pieces = list(b) if isinstance(b, (list, tuple)) else [b]
    widths = [p.shape[1] for p in pieces]
    offs = [sum(widths[:p]) for p in range(len(pieces))]
    n = sum(widths)
    tm, tk = _tile(m, tm), _tile(t, tk, 8)
    per_tile = 1
    if col_shards:
        wb = n // N_DEV
        if len(pieces) > 1:
            tn = n
        while 2 * per_tile * wb <= tn and 2 * per_tile <= N_DEV:
            per_tile *= 2
        tn = per_tile * wb
        out_spec = pl.BlockSpec((per_tile, tm, wb), lambda i, j, kk: (j, i, 0))
        out_shape = jax.ShapeDtypeStruct((N_DEV, m, wb), MM)
    else:
        tn = _tile(n, tn)
        out_spec = pl.BlockSpec((tm, tn), lambda i, j, kk: (i, j))
        out_shape = jax.ShapeDtypeStruct((m, n), MM)
    nk = t // tk

    assert len(pieces) == 1 or tn == n

    def body(a_ref, *rest):
        b_refs, (o_ref, acc) = rest[:len(pieces)], rest[len(pieces):]
        kk = pl.program_id(2)

        @pl.when(kk == 0)
        def _():
            acc[...] = jnp.zeros_like(acc)

        av = (a_ref[...] if a_fn is None else a_fn(a_ref[...])).astype(MM)
        if len(pieces) == 1:
            acc[...] += _dot_tn(av, b_refs[0][...].astype(MM))
        else:
            for b_ref, off, wd in zip(b_refs, offs, widths):
                acc[:, off:off + wd] += _dot_tn(av, b_ref[...].astype(MM))

        @pl.when(kk == nk - 1)
        def _():
            if col_shards:
                for sh in range(per_tile):
                    o_ref[sh] = acc[:, sh * wb:(sh + 1) * wb].astype(o_ref.dtype)
            else:
                o_ref[...] = acc[...].astype(o_ref.dtype)

    return pl.pallas_call(
        body, name=name, grid=(m // tm, n // tn, nk),
        in_specs=[pl.BlockSpec((tk, tm), lambda i, j, kk: (kk, i))]
        + [pl.BlockSpec((tk, tn if len(pieces) == 1 else wd), lambda i, j, kk: (kk, j)) for wd in widths],
        out_specs=out_spec, out_shape=out_shape,
        scratch_shapes=[pltpu.VMEM((tm, tn), F32)],
        compiler_params=_params(3))(a, *pieces)


def _rms_prologue(x, g):
    r = lax.rsqrt(jnp.mean(x * x, axis=-1, keepdims=True) + EPS)
    return x * r * g


def _rms_bwd_rows(dh, x, g):
    d = x.shape[-1]
    r = lax.rsqrt(jnp.mean(x * x, axis=-1, keepdims=True) + EPS)
    xh = x * r
    dxh = dh * g
    dx = r * (dxh - xh * (jnp.sum(dxh * xh, axis=-1, keepdims=True) / d))
    dg = jnp.sum(dh * xh, axis=0, keepdims=True)
    return dx, dg


def _rms_bwd(name, dh, x, g, dres, *, tm=512):
    m, d = x.shape
    tm = min(tm, m)
    has_res = dres is not None

    def body(*refs):
        if has_res:
            dh_ref, x_ref, g_ref, r_ref, dx_ref, dg_ref = refs
        else:
            dh_ref, x_ref, g_ref, dx_ref, dg_ref = refs
        dx, dg = _rms_bwd_rows(dh_ref[...], x_ref[...], g_ref[...])
        if has_res:
            dx = dx + r_ref[...]
        dx_ref[...] = dx

        @pl.when(pl.program_id(0) == 0)
        def _():
            dg_ref[...] = jnp.zeros_like(dg_ref)

        dg_ref[...] += dg

    row = pl.BlockSpec((tm, d), lambda i: (i, 0))
    vec = pl.BlockSpec((1, d), lambda i: (0, 0))
    in_specs = [row, row, vec] + ([row] if has_res else [])
    args = [dh, x, g] + ([dres] if has_res else [])
    return pl.pallas_call(body, name=name, grid=(m // tm,), in_specs=in_specs, out_specs=[row, vec],
                          out_shape=[jax.ShapeDtypeStruct((m, d), F32), jax.ShapeDtypeStruct((1, d), F32)],
                          compiler_params=_params(1))(*args)


def _mm_rms_bwd(name, a, w, x, g, dres, *, tm, after=()):
    pieces = list(a) if isinstance(a, (list, tuple)) else [a]
    widths = [p.shape[1] for p in pieces]
    m = pieces[0].shape[0]
    d = w.shape[0]
    tm = _tile(m, tm, 8)
    n_a = len(pieces)

    def body(*refs):
        a_refs = refs[:n_a]
        w_ref, x_ref, g_ref, r_ref = refs[n_a:n_a + 4]
        dx_ref, dg_ref = refs[n_a + 4 + len(after):]

        av = a_refs[0][...] if n_a == 1 else jnp.concatenate([a_ref[...] for a_ref in a_refs], axis=1)
        dh = _dot_nt(av, w_ref[...])
        dx, dg = _rms_bwd_rows(dh, x_ref[...], g_ref[...])
        dx_ref[...] = dx + r_ref[...]

        @pl.when(pl.program_id(0) == 0)
        def _():
            dg_ref[...] = jnp.zeros_like(dg_ref)

        dg_ref[...] += dg

    row = pl.BlockSpec((tm, d), lambda i: (i, 0))
    vec = pl.BlockSpec((1, d), lambda i: (0, 0))
    return pl.pallas_call(
        body, name=name, grid=(m // tm,),
        in_specs=[pl.BlockSpec((tm, wd), lambda i: (i, 0)) for wd in widths]
        + [pl.BlockSpec(w.shape, lambda i: (0, 0)), row, vec, row]
        + [pl.BlockSpec(memory_space=pl.ANY)] * len(after),
        out_specs=[row, vec],
        out_shape=[jax.ShapeDtypeStruct((m, d), F32), jax.ShapeDtypeStruct((1, d), F32)],
        compiler_params=_params(1))(*pieces, w, x, g, dres, *after)


def _loss_head(x, target, g, *, tm=256):
    m, d = x.shape
    tm = min(tm, m)

    def body(x_ref, t_ref, g_ref, dx_ref, dg_ref, loss_ref):
        xv, gv = x_ref[...], g_ref[...]
        y = _rms_prologue(xv, gv)
        err = y - t_ref[...]
        part = 0.5 * jnp.sum(jnp.sum(err * err, axis=-1, keepdims=True) / d)
        dx, dg = _rms_bwd_rows(err / d, xv, gv)
        dx_ref[...] = dx

        @pl.when(pl.program_id(0) == 0)
        def _():
            dg_ref[...] = jnp.zeros_like(dg_ref)
            loss_ref[...] = jnp.zeros_like(loss_ref)

        dg_ref[...] += dg
        loss_ref[...] += jnp.full(loss_ref.shape, part, F32)

    row = pl.BlockSpec((tm, d), lambda i: (i, 0))
    vec = pl.BlockSpec((1, d), lambda i: (0, 0))
    lspec = pl.BlockSpec((1, 128), lambda i: (0, 0))
    return pl.pallas_call(body, name="loss_head", grid=(m // tm,), in_specs=[row, row, vec],
                          out_specs=[row, vec, lspec],
                          out_shape=[jax.ShapeDtypeStruct((m, d), F32), jax.ShapeDtypeStruct((1, d), F32),
                                     jax.ShapeDtypeStruct((1, 128), F32)],
                          compiler_params=_params(1))(x, target, g)


def _rot(xv, cos2, sin2, half):
    return xv * cos2 + pltpu.roll(xv, half, 1) * sin2


def _rot_t(dv, cos2, sin2, half):
    return dv * cos2 + pltpu.roll(dv * sin2, half, 1)


def _ret_consts(lg_ref, h, t, dk):
    lf, lb = lg_ref[0, h], lg_ref[1, h]
    ab = (lax.broadcasted_iota(jnp.int32, (t, t), 0) - lax.broadcasted_iota(jnp.int32, (t, t), 1)).astype(F32)
    dmat = jnp.exp(jnp.where(ab >= 0, lf * ab, -lb * ab))
    up = lax.broadcasted_iota(jnp.int32, (t, dk), 0).astype(F32) + 1.0
    down = float(t) - up
    one = jnp.ones((1, 1), F32)
    return dict(ab=ab, dmat=dmat, xi_f=jnp.exp(lf * up), zeta_f=jnp.exp(lf * down), xi_b=jnp.exp(lb * up),
                zeta_b=jnp.exp(lb * down), up=up[:, 0:1], down=down[:, 0:1],
                cf=jnp.exp(one * (lf * t)), cb=jnp.exp(one * (lb * t)))


def _scaled(xv, rows):
    return (xv.astype(F32) * rows).astype(MM)


def _ret_core_fwd(proj, cos2, sin2, lg, d, bl, seq, *, tc=256):
    t = proj.shape[0]
    dk, dv = d // 8, d // 4
    tc = min(tc, seq)
    nc = seq // tc
    scale = float(dk) ** -0.5

    def body(lg_ref, qp_ref, kp_ref, v_ref, g_ref, c_ref, s_ref, o_ref, a_ref, sf_ref, sb_ref, q_ref, k_ref):
        c = _ret_consts(lg_ref, pl.program_id(1), tc, dk)

        def rows_of(i):
            return pl.ds(pl.multiple_of(i * tc, tc), tc)

        def rotate(i, carry):
            rows = rows_of(i)
            cs, sn = c_ref[rows, :], s_ref[rows, :]
            q_ref[rows, :] = _rot(qp_ref[rows, :].astype(F32), cs, sn, dk // 2).astype(MM)
            k_ref[rows, :] = (_rot(kp_ref[rows, :].astype(F32), cs, sn, dk // 2) * scale).astype(MM)
            return carry

        lax.fori_loop(0, nc, rotate, 0)

        def fwd_step(i, sf):
            rows = rows_of(i)
            sf_ref[i] = sf
            q, kk, v = q_ref[rows, :], k_ref[rows, :], v_ref[rows, :]
            p = (_dot_nt(q, kk) * c["dmat"]).astype(MM)
            o_ref[rows, :] = _dot(p, v) + _dot(_scaled(q, c["xi_f"]), sf.astype(MM))
            return sf * c["cf"] + _dot_tn(_scaled(kk, c["zeta_f"]), v)

        lax.fori_loop(0, nc, fwd_step, jnp.zeros((dk, dv), F32))

        def bwd_step(ii, sb):
            rows = rows_of(nc - 1 - ii)
            sb_ref[nc - 1 - ii] = sb
            q, kk, v = q_ref[rows, :], k_ref[rows, :], v_ref[rows, :]
            o_ref[rows, :] += _dot(_scaled(q, c["zeta_b"]), sb.astype(MM))
            return sb * c["cb"] + _dot_tn(_scaled(kk, c["xi_b"]), v)

        lax.fori_loop(0, nc, bwd_step, jnp.zeros((dk, dv), F32))

        def post(i, carry):
            rows = rows_of(i)
            o = o_ref[rows, :]
            oc = o - jnp.mean(o, axis=-1, keepdims=True)
            on = oc * lax.rsqrt(jnp.mean(oc * oc, axis=-1, keepdims=True) + EPS)
            g = g_ref[rows, :].astype(F32)
            a_ref[rows, :] = (on * (g * _sigmoid(g))).astype(MM)
            return carry

        lax.fori_loop(0, nc, post, 0)

    qk = pl.BlockSpec((seq, dk), lambda b, h: (b, h))
    vv = pl.BlockSpec((seq, dv), lambda b, h: (b, h))
    tab = pl.BlockSpec((seq, dk), lambda b, h: (0, 0))
    states = pl.BlockSpec((None, nc, dk, dv), lambda b, h: (b * HEADS + h, 0, 0, 0))
    return pl.pallas_call(
        body, name="ret_core_fwd", grid=(bl, HEADS),
        in_specs=[pl.BlockSpec(memory_space=pltpu.SMEM), qk, pl.BlockSpec((seq, dk), lambda b, h: (b, HEADS + h)),
                  pl.BlockSpec((seq, dv), lambda b, h: (b, HEADS + h)),
                  pl.BlockSpec((seq, dv), lambda b, h: (b, 2 * HEADS + h)), tab, tab],
        out_specs=[vv, vv, states, states, qk, qk],
        out_shape=[jax.ShapeDtypeStruct((t, d), F32), jax.ShapeDtypeStruct((t, d), MM),
                   jax.ShapeDtypeStruct((bl * HEADS, nc, dk, dv), F32),
                   jax.ShapeDtypeStruct((bl * HEADS, nc, dk, dv), F32),
                   jax.ShapeDtypeStruct((t, d // 2), MM), jax.ShapeDtypeStruct((t, d // 2), MM)],
        compiler_params=_params(2))(lg, proj, proj, proj, proj, cos2, sin2)


def _ret_post_bwd(da, proj, o_raw, d, *, ts=2048):
    t = da.shape[0]
    dv = d // 4
    ts = min(ts, t)

    def body(da_ref, g_ref, o_ref, dg_ref, do_ref):
        o, g, dav = o_ref[...], g_ref[...].astype(F32), da_ref[...]
        mu = jnp.mean(o, axis=-1, keepdims=True)
        oc = o - mu
        r = lax.rsqrt(jnp.mean(oc * oc, axis=-1, keepdims=True) + EPS)
        on = oc * r
        sg = _sigmoid(g)
        don = dav * (g * sg)
        dg_ref[...] = (dav * on * (sg * (1.0 + g * (1.0 - sg)))).astype(MM)
        do = r * (don - jnp.mean(don, axis=-1, keepdims=True) - on * jnp.mean(don * on, axis=-1, keepdims=True))
        do_ref[...] = do.astype(MM)

    blk = pl.BlockSpec((ts, dv), lambda i, h: (i, h))
    return pl.pallas_call(
        body, name="ret_post_bwd", grid=(t // ts, HEADS),
        in_specs=[blk, pl.BlockSpec((ts, dv), lambda i, h: (i, 2 * HEADS + h)), blk],
        out_specs=[blk, blk],
        out_shape=[jax.ShapeDtypeStruct((t, d), MM), jax.ShapeDtypeStruct((t, d), MM)],
        compiler_params=_params(2))(da, proj, o_raw)


def _ret_core_bwd(qr, kr, proj, do, sf_in, sb_in, cos2, sin2, lg, d, bl, seq, *, tc=256):
    t = qr.shape[0]
    dk, dv = d // 8, d // 4
    tc = min(tc, seq)
    nc = seq // tc
    scale = float(dk) ** -0.5

    def body(lg_ref, q_ref, k_ref, v_ref, do_ref, sf_all, sb_all, c_ref, s_ref, dq_ref, dk_ref, dv_ref,
             dlf_ref, dlb_ref, dq_acc, dk_acc, dv_acc):
        c = _ret_consts(lg_ref, pl.program_id(1), tc, dk)
        fwd = c["ab"] >= 0
        zero_state = jnp.zeros((dk, dv), F32)
        zero = jnp.zeros((1, 1), F32)

        def rows_of(i):
            return pl.ds(pl.multiple_of(i * tc, tc), tc)

        def total(xv):
            return jnp.sum(xv, keepdims=True)

        def fwd_sweep(i, carry):
            hh, dlf, dlb = carry
            rows = rows_of(i)
            q, kk, v, dov = q_ref[rows, :], k_ref[rows, :], v_ref[rows, :], do_ref[rows, :]
            dof, vf = dov.astype(F32), v.astype(F32)
            p = _dot_nt(q, kk) * c["dmat"]
            da = _dot_nt(dov, v)
            x = p * da * c["ab"]
            dlf = dlf + total(jnp.where(fwd, x, 0.0))
            dlb = dlb - total(jnp.where(fwd, 0.0, x))
            pb, dpb = p.astype(MM), (da * c["dmat"]).astype(MM)
            dq = _dot(dpb, kk)
            dkc = _dot_tn(dpb, q)
            dvc = _dot_tn(pb, dov)
            sf, sb = sf_all[i], sb_all[i]
            sfb, sbb = sf.astype(MM), sb.astype(MM)
            q_xf, q_zb = _scaled(q, c["xi_f"]), _scaled(q, c["zeta_b"])
            dq = dq + _dot_nt(dov, sfb) * c["xi_f"] + _dot_nt(dov, sbb) * c["zeta_b"]
            dlf = dlf + total(jnp.sum(_dot(q_xf, sfb) * dof, axis=-1, keepdims=True) * c["up"])
            dlb = dlb + total(jnp.sum(_dot(q_zb, sbb) * dof, axis=-1, keepdims=True) * c["down"])
            hb = hh.astype(MM)
            dkc = dkc + _dot_nt(v, hb) * c["xi_b"]
            dv_bx = _dot(_scaled(kk, c["xi_b"]), hb)
            dlb = dlb + total(jnp.sum(vf * dv_bx, axis=-1, keepdims=True) * c["up"])
            dlb = dlb + float(tc) * total(hh * (sb * c["cb"]))
            dq_acc[rows, :] = dq
            dk_acc[rows, :] = dkc
            dv_acc[rows, :] = dvc + dv_bx
            return hh * c["cb"] + _dot_tn(q_zb, dov), dlf, dlb

        _, dlf, dlb = lax.fori_loop(0, nc, fwd_sweep, (zero_state, zero, zero))

        def rev_sweep(ii, carry):
            gg, dlf = carry
            i = nc - 1 - ii
            rows = rows_of(i)
            q, kk, v, dov = q_ref[rows, :], k_ref[rows, :], v_ref[rows, :], do_ref[rows, :]
            gb = gg.astype(MM)
            dk_acc[rows, :] += _dot_nt(v, gb) * c["zeta_f"]
            dv_fx = _dot(_scaled(kk, c["zeta_f"]), gb)
            dv_acc[rows, :] += dv_fx
            dlf = dlf + total(jnp.sum(v.astype(F32) * dv_fx, axis=-1, keepdims=True) * c["down"])
            dlf = dlf + float(tc) * total(gg * (sf_all[i] * c["cf"]))
            return gg * c["cf"] + _dot_tn(_scaled(q, c["xi_f"]), dov), dlf

        _, dlf = lax.fori_loop(0, nc, rev_sweep, (zero_state, dlf))

        cs, sn = c_ref[...], s_ref[...]
        dq_ref[...] = _rot_t(dq_acc[...], cs, sn, dk // 2).astype(MM)
        dk_ref[...] = (_rot_t(dk_acc[...], cs, sn, dk // 2) * scale).astype(MM)
        dv_ref[...] = dv_acc[...].astype(MM)
        dlf_ref[...] = jnp.broadcast_to(dlf, dlf_ref.shape)
        dlb_ref[...] = jnp.broadcast_to(dlb, dlb_ref.shape)

    qk = pl.BlockSpec((seq, dk), lambda b, h: (b, h))
    vv = pl.BlockSpec((seq, dv), lambda b, h: (b, h))
    tab = pl.BlockSpec((seq, dk), lambda b, h: (0, 0))
    dl = pl.BlockSpec((None, 8, 128), lambda b, h: (b * HEADS + h, 0, 0))
    states = pl.BlockSpec((None, nc, dk, dv), lambda b, h: (b * HEADS + h, 0, 0, 0))
    return pl.pallas_call(
        body, name="ret_core_bwd", grid=(bl, HEADS),
        in_specs=[pl.BlockSpec(memory_space=pltpu.SMEM), qk, qk, pl.BlockSpec((seq, dv), lambda b, h: (b, HEADS + h)),
                  vv, states, states, tab, tab],
        out_specs=[qk, qk, vv, dl, dl],
        out_shape=[jax.ShapeDtypeStruct((t, d // 2), MM), jax.ShapeDtypeStruct((t, d // 2), MM),
                   jax.ShapeDtypeStruct((t, d), MM),
                   jax.ShapeDtypeStruct((bl * HEADS, 8, 128), F32), jax.ShapeDtypeStruct((bl * HEADS, 8, 128), F32)],
        scratch_shapes=[pltpu.VMEM((seq, dk), F32), pltpu.VMEM((seq, dk), F32), pltpu.VMEM((seq, dv), F32)],
        compiler_params=_params(2))(lg, qr, kr, proj, do, sf_in, sb_in, cos2, sin2)


def _window_count(row, w, seq):
    return (jnp.minimum(row + w // 2, seq) - jnp.maximum(row - w // 2, 0)).astype(F32)


def _window_sum(pv, row, w, seq, sign):
    acc = None
    for j in range(-(w // 2), w // 2):
        if j == 0:
            term = pv
        else:
            src = row + sign * j
            term = jnp.where((src >= 0) & (src < seq), pltpu.roll(pv, (-sign * j) % seq, 0), 0.0)
        acc = term if acc is None else acc + term
    return acc


def _pool_fwd(proj, w_grp, scale, d, bl, seq):
    t = proj.shape[0]
    dg = d // 8

    def body(p_ref, w_ref, s_ref, y_ref):
        row = lax.broadcasted_iota(jnp.int32, (seq, dg), 0)
        for gi, w in enumerate(POOL_WINDOWS):
            sl = slice(gi * dg, (gi + 1) * dg)
            pg = p_ref[:, sl].astype(F32)
            mixed = _window_sum(pg, row, w, seq, 1) / _window_count(row, w, seq) - pg
            yp = _dot(mixed.astype(MM), w_ref[gi].astype(MM))
            y_ref[:, sl] = (yp * s_ref[:, sl]).astype(MM)

    return pl.pallas_call(
        body, name="pool_fwd", grid=(bl,),
        in_specs=[pl.BlockSpec((seq, d // 2), lambda b: (b, 6)),
                  pl.BlockSpec(w_grp.shape, lambda b: (0, 0, 0)),
                  pl.BlockSpec((1, d // 2), lambda b: (0, 0))],
        out_specs=pl.BlockSpec((seq, d // 2), lambda b: (b, 0)),
        out_shape=jax.ShapeDtypeStruct((t, d // 2), MM),
        compiler_params=_params(1))(proj, w_grp, scale)


def _pool_bwd(proj, dy, w_grp, scale, d, bl, seq):
    t = proj.shape[0]
    dg = d // 8

    def body(p_ref, dy_ref, w_ref, s_ref, dp_ref, dw_ref, ds_ref):
        @pl.when(pl.program_id(0) == 0)
        def _():
            dw_ref[...] = jnp.zeros_like(dw_ref)
            ds_ref[...] = jnp.zeros_like(ds_ref)

        row = lax.broadcasted_iota(jnp.int32, (seq, dg), 0)
        for gi, w in enumerate(POOL_WINDOWS):
            sl = slice(gi * dg, (gi + 1) * dg)
            pg = p_ref[:, sl].astype(F32)
            cnt = _window_count(row, w, seq)
            mixb = (_window_sum(pg, row, w, seq, 1) / cnt - pg).astype(MM)
            wgb = w_ref[gi].astype(MM)
            yp = _dot(mixb, wgb)
            dyg = dy_ref[:, sl]
            ds_ref[:, sl] += jnp.sum(dyg * yp, axis=0, keepdims=True)
            dyp = (dyg * s_ref[:, sl]).astype(MM)
            dmixed = _dot_nt(dyp, wgb)
            dw_ref[gi] += _dot_tn(mixb, dyp)
            dp_ref[:, sl] = (_window_sum(dmixed / cnt, row, w, seq, -1) - dmixed).astype(MM)

    half = pl.BlockSpec((seq, d // 2), lambda b: (b, 0))
    wspec = pl.BlockSpec(w_grp.shape, lambda b: (0, 0, 0))
    sspec = pl.BlockSpec((1, d // 2), lambda b: (0, 0))
    return pl.pallas_call(
        body, name="pool_bwd", grid=(bl,),
        in_specs=[pl.BlockSpec((seq, d // 2), lambda b: (b, 6)), half, wspec, sspec],
        out_specs=[half, wspec, sspec],
        out_shape=[jax.ShapeDtypeStruct((t, d // 2), MM), jax.ShapeDtypeStruct(w_grp.shape, F32),
                   jax.ShapeDtypeStruct((1, d // 2), F32)],
        compiler_params=_params(1))(proj, dy, w_grp, scale)


def _attn_probs(q, kk, dh):
    s = _dot_nt(q, kk) * (float(dh) ** -0.5)
    e = jnp.exp(s - jnp.max(s, axis=-1, keepdims=True))
    return e / jnp.sum(e, axis=-1, keepdims=True)


def _attn_fwd(proj, kv, d, bl, seq, mlen, *, tq=2048):
    t = proj.shape[0]
    dh = d // 8
    tq = min(tq, seq)
    nq = seq // tq

    def body(q_ref, k_ref, v_ref, o_ref):
        a = _attn_probs(q_ref[...].astype(MM), k_ref[...].astype(MM), dh)
        o_ref[...] = _dot(a.astype(MM), v_ref[...].astype(MM)).astype(MM)

    return pl.pallas_call(
        body, name="attn_fwd", grid=(bl, HEADS, nq),
        in_specs=[pl.BlockSpec((tq, dh), lambda b, h, i: (b * nq + i, 7 * HEADS + h)),
                  pl.BlockSpec((mlen, dh), lambda b, h, i: (b, h)),
                  pl.BlockSpec((mlen, dh), lambda b, h, i: (b, HEADS + h))],
        out_specs=pl.BlockSpec((tq, dh), lambda b, h, i: (b * nq + i, h)),
        out_shape=jax.ShapeDtypeStruct((t, d // 2), MM),
        compiler_params=_params(3))(proj, kv, kv)


def _attn_bwd(proj, kv, do, d, bl, seq, mlen, *, tq=2048):
    t = proj.shape[0]
    dh = d // 8
    tq = min(tq, seq)
    nq = seq // tq

    def body(q_ref, k_ref, v_ref, do_ref, dq_ref, dk_ref, dv_ref):
        @pl.when(pl.program_id(2) == 0)
        def _():
            dk_ref[...] = jnp.zeros_like(dk_ref)
            dv_ref[...] = jnp.zeros_like(dv_ref)

        q, kk, vv = q_ref[...].astype(MM), k_ref[...].astype(MM), v_ref[...].astype(MM)
        dov = do_ref[...].astype(MM)
        a = _attn_probs(q, kk, dh)
        dp = _dot_nt(dov, vv)
        ds = (a * (dp - jnp.sum(dp * a, axis=-1, keepdims=True)) * (float(dh) ** -0.5)).astype(MM)
        dq_ref[...] = _dot(ds, kk).astype(MM)
        dk_ref[...] += _dot_tn(ds, q)
        dv_ref[...] += _dot_tn(a.astype(MM), dov)

    qs = pl.BlockSpec((tq, dh), lambda b, h, i: (b * nq + i, h))
    ms = pl.BlockSpec((mlen, dh), lambda b, h, i: (b, h))
    return pl.pallas_call(
        body, name="attn_bwd", grid=(bl, HEADS, nq),
        in_specs=[pl.BlockSpec((tq, dh), lambda b, h, i: (b * nq + i, 7 * HEADS + h)), ms,
                  pl.BlockSpec((mlen, dh), lambda b, h, i: (b, HEADS + h)), qs],
        out_specs=[qs, ms, ms],
        out_shape=[jax.ShapeDtypeStruct((t, d // 2), MM), jax.ShapeDtypeStruct((bl * mlen, d // 2), F32),
                   jax.ShapeDtypeStruct((bl * mlen, d // 2), F32)],
        compiler_params=_params(3))(proj, kv, kv, do)


def _comm_call(name, body, arrays, out_shapes):
    n = len(arrays)
    hbm = pl.BlockSpec(memory_space=pl.ANY)
    return pl.pallas_call(
        body, name=name, out_shape=out_shapes, in_specs=[hbm] * n, out_specs=[hbm] * n,
        scratch_shapes=[pltpu.SemaphoreType.DMA((7 * n,)), pltpu.SemaphoreType.DMA((7 * n,)),
                        pltpu.SemaphoreType.DMA((n,))],
    )(*arrays)


def _all_gather(name, shards):
    n = len(shards)

    def body(*refs):
        x_refs, out_refs = refs[:n], refs[n:2 * n]
        send_sems, recv_sems, local_sems = refs[2 * n:]
        x, y, c = lax.axis_index("x"), lax.axis_index("y"), lax.axis_index("c")
        me, sibling = (x, y, c), (x, y, 1 - c)
        chips = [(1 - x, y), (x, 1 - y), (1 - x, 1 - y)]

        def copy(o, k, block, to, src=None):
            slot = out_refs[o].at[4 * block[0] + 2 * block[1] + block[2]]
            return pltpu.make_async_remote_copy(
                src_ref=slot if src is None else src, dst_ref=slot, send_sem=send_sems.at[7 * o + k],
                recv_sem=recv_sems.at[7 * o + k], device_id=to, device_id_type=MESH)

        locals_, remotes = [], []
        for o in range(n):
            mine = pltpu.make_async_copy(x_refs[o], out_refs[o].at[4 * x + 2 * y + c], local_sems.at[o])
            mine.start()
            locals_.append(mine)
            first = [copy(o, 0, me, sibling, src=x_refs[o])]
            first += [copy(o, 1 + j, me, (*chip, c), src=x_refs[o]) for j, chip in enumerate(chips)]
            for cp in first:
                cp.start()
            remotes += first
        for o in range(n):
            for j, chip in enumerate(chips):
                copy(o, 1 + j, (*chip, c), me).wait_recv()
                passed = copy(o, 4 + j, (*chip, c), sibling)
                passed.start()
                remotes.append(passed)
        for o in range(n):
            copy(o, 0, sibling, me).wait_recv()
            for j, chip in enumerate(chips):
                copy(o, 4 + j, (*chip, 1 - c), me).wait_recv()
        for cp in remotes:
            cp.wait_send()
        for mine in locals_:
            mine.wait()

    outs = [jax.ShapeDtypeStruct((N_DEV,) + s.shape, s.dtype) for s in shards]
    return _comm_call(name, body, shards, outs)


def _columns_side_by_side(name, g):
    _, k, wb = g.shape

    def body(x_ref, o_ref):
        o_ref[...] = x_ref[...]

    return pl.pallas_call(
        body, name=name, grid=(N_DEV,), in_specs=[pl.BlockSpec((None, k, wb), lambda j: (j, 0, 0))],
        out_specs=pl.BlockSpec((k, wb), lambda j: (0, j)),
        out_shape=jax.ShapeDtypeStruct((k, N_DEV * wb), g.dtype), compiler_params=_params(1))(g)


def _peer_of(k, x, y, c):
    peer = (1 - x if k & 4 else x, 1 - y if k & 2 else y, 1 - c if k & 1 else c)
    return peer, 4 * peer[0] + 2 * peer[1] + peer[2]


def _split_copies(scatter, srcs, lands, send_sems, recv_sems, arriving):
    x, y, c = lax.axis_index("x"), lax.axis_index("y"), lax.axis_index("c")
    me_idx = 4 * x + 2 * y + c
    copies = []
    for o, (src, land) in enumerate(zip(srcs, lands)):
        for k in range(1, N_DEV):
            peer, p_idx = _peer_of(k, x, y, c)
            mine = src.at[p_idx] if scatter else src
            sems = dict(send_sem=send_sems.at[7 * o + k - 1], recv_sem=recv_sems.at[7 * o + k - 1],
                        device_id=peer, device_id_type=MESH)
            slot = land.at[p_idx] if arriving else land.at[me_idx]
            copies.append(pltpu.make_async_remote_copy(src_ref=mine, dst_ref=slot, **sems))
    return copies


_HBM = pl.BlockSpec(memory_space=pltpu.HBM)
_SEM = pl.BlockSpec(memory_space=pltpu.SEMAPHORE)
_EFFECT = pltpu.SideEffectType.DATAFLOW_SIDE_EFFECTING


def _own_slot_copies(scatter, srcs, lands, local_sems):
    me_idx = 4 * lax.axis_index("x") + 2 * lax.axis_index("y") + lax.axis_index("c")
    return [pltpu.make_async_copy(src.at[me_idx] if scatter else src, land.at[me_idx], local_sems.at[o])
            for o, (src, land) in enumerate(zip(srcs, lands))]


def _exchange_start(name, scatter, arrays, after=()):
    n = len(arrays)
    lands = [lax.empty(a.shape if scatter else (N_DEV,) + a.shape, a.dtype) for a in arrays]

    def body(*refs):
        srcs, lnds = refs[:n], refs[n:2 * n]
        send_sems, recv_sems, local_sems = refs[2 * n + len(after):2 * n + len(after) + 3]
        token = refs[-1]
        for cp in _split_copies(scatter, srcs, lnds, send_sems, recv_sems, False):
            cp.start()
        for cp in _own_slot_copies(scatter, srcs, lnds, local_sems):
            cp.start()
        token[...] = jnp.zeros_like(token)

    hbm_in = [pltpu.with_memory_space_constraint(a, pltpu.HBM) for a in list(arrays) + lands]
    res = pl.pallas_call(
        body, name=name,
        out_shape=(pltpu.SemaphoreType.DMA((7 * n,)), pltpu.SemaphoreType.DMA((7 * n,)), pltpu.SemaphoreType.DMA((n,)),
                   *[pltpu.HBM(a.shape, a.dtype) for a in hbm_in], jax.ShapeDtypeStruct((8, 128), F32)),
        in_specs=[_HBM] * (2 * n) + [pl.BlockSpec(memory_space=pl.ANY)] * len(after),
        out_specs=(_SEM, _SEM, _SEM, *[_HBM] * (2 * n), pl.BlockSpec(memory_space=pltpu.VMEM)),
        input_output_aliases={i: 3 + i for i in range(2 * n)},
        compiler_params=pltpu.CompilerParams(has_side_effects=_EFFECT),
    )(*hbm_in, *after)
    return res[:3], None, list(res[3:3 + n]), list(res[3 + n:3 + 2 * n]), res[-1]


def _exchange_wait(name, scatter, started, after):
    sems, _, srcs, lands, _ = started
    n = len(srcs)

    def body(*refs):
        src_refs, lnd_refs = refs[:n], refs[n:2 * n]
        send_sems, recv_sems, local_sems = refs[2 * n:2 * n + 3]
        for cp in _split_copies(scatter, src_refs, lnd_refs, send_sems, recv_sems, False):
            cp.wait_send()
        for cp in _split_copies(scatter, src_refs, lnd_refs, send_sems, recv_sems, True):
            cp.wait_recv()
        for cp in _own_slot_copies(scatter, src_refs, lnd_refs, local_sems):
            cp.wait()

    res = pl.pallas_call(
        body, name=name, out_shape=tuple(pltpu.HBM(a.shape, a.dtype) for a in srcs + lands),
        in_specs=[_HBM] * (2 * n) + [_SEM, _SEM, _SEM, pl.BlockSpec(memory_space=pl.ANY)],
        out_specs=tuple([_HBM] * (2 * n)), input_output_aliases={i: i for i in range(2 * n)},
        compiler_params=pltpu.CompilerParams(has_side_effects=_EFFECT),
    )(*srcs, *lands, *sems, after)
    return list(res[n:])


def _adamw(name, parts, w, m, v, prev, layer, *, tr=256):
    _, a, b = w.shape
    tr = _tile(a, tr, 8)
    c1 = 1.0 - ADAM_B1 ** ADAM_STEP
    c2 = 1.0 - ADAM_B2 ** ADAM_STEP

    def body(p_ref, w_ref, m_ref, v_ref, _g, _d, _m, _v, g_out, d_out, m_out, v_out):
        g = p_ref[0].astype(F32)
        for s in range(1, N_DEV):
            g = g + p_ref[s].astype(F32)
        mn = ADAM_B1 * m_ref[...] + (1.0 - ADAM_B1) * g
        vn = ADAM_B2 * v_ref[...] + (1.0 - ADAM_B2) * (g * g)
        g_out[...] = g
        m_out[...] = mn
        v_out[...] = vn
        d_out[...] = -ADAM_LR * ((mn / c1) / (jnp.sqrt(vn / c2) + ADAM_EPS) + ADAM_WD * w_ref[...])

    slab = pl.BlockSpec((None, tr, b), lambda i: (layer, i, 0))
    whole = pl.BlockSpec(memory_space=pl.ANY)
    return pl.pallas_call(
        body, name=name, grid=(a // tr,),
        in_specs=[pl.BlockSpec((N_DEV, tr, b), lambda i: (0, i, 0)), slab, slab, slab] + [whole] * 4,
        out_specs=[slab] * 4, out_shape=[jax.ShapeDtypeStruct(w.shape, F32)] * 4,
        input_output_aliases={4: 0, 5: 1, 6: 2, 7: 3},
        compiler_params=_params(1))(parts, w, m, v, *prev)


_COL = ("w_in", "w_pool_o", "w_mem_o", "w_ff1")
_DW_SHARDED = ("w_in", "w_ff1")
_BIG =("w_in", "w_ret_o", "w_pool_o", "w_mem_kv", "w_mem_o", "w_out", "w_ff1", "w_ff2")
_SMALL = ("ret_decay_logit", "w_pool_grp", "pool_scale", "norm1_g", "norm2_g", "mem_norm_g", "final_norm_g")
_SMALL_MM = ("w_pool_grp",)
_SMALL_F32 = tuple(n for n in _SMALL if n not in _SMALL_MM)
_WEIGHTS = ("w_in", "ret_decay_logit", "w_ret_o", "w_pool_grp", "pool_scale", "w_pool_o", "w_mem_kv", "w_mem_o",
            "w_out", "w_ff1", "w_ff2", "norm1_g", "norm2_g", "mem_norm_g", "final_norm_g")


def _small_rows(size, d):
    return -(-size // (8 * d)) * 8


def _pack_small(ws, d, names):
    parts = []
    for n in names:
        flat = ws[n].reshape(-1)
        rows = _small_rows(flat.shape[0], d)
        parts.append(jnp.pad(flat, (0, rows * d - flat.shape[0])).reshape(rows, d))
    return jnp.concatenate(parts, axis=0)[None]


def _unpack_small(packed, like, d, names):
    out, off = {}, 0
    for n in names:
        rows = _small_rows(like[n].size, d)
        out[n] = packed[0, off:off + rows].reshape(-1)[:like[n].size].reshape(like[n].shape)
        off += rows
    return out


def kernel(x, mem, w_in, ret_decay_logit, w_ret_o, w_pool_grp, pool_scale, w_pool_o, w_mem_kv, w_mem_o, w_out, w_ff1, w_ff2, norm1_g, norm2_g, mem_norm_g, final_norm_g, loss_target, m_w_in, m_ret_decay_logit, m_w_ret_o, m_w_pool_grp, m_pool_scale, m_w_pool_o, m_w_mem_kv, m_w_mem_o, m_w_out, m_w_ff1, m_w_ff2, m_norm1_g, m_norm2_g, m_mem_norm_g, m_final_norm_g, v_w_in, v_ret_decay_logit, v_w_ret_o, v_w_pool_grp, v_pool_scale, v_w_pool_o, v_w_mem_kv, v_w_mem_o, v_w_out, v_w_ff1, v_w_ff2, v_norm1_g, v_norm2_g, v_mem_norm_g, v_final_norm_g):
    w = dict(w_in=w_in, ret_decay_logit=ret_decay_logit, w_ret_o=w_ret_o, w_pool_grp=w_pool_grp,
             pool_scale=pool_scale, w_pool_o=w_pool_o, w_mem_kv=w_mem_kv, w_mem_o=w_mem_o, w_out=w_out,
             w_ff1=w_ff1, w_ff2=w_ff2, norm1_g=norm1_g, norm2_g=norm2_g, mem_norm_g=mem_norm_g,
             final_norm_g=final_norm_g)
    mom = dict(w_in=m_w_in, ret_decay_logit=m_ret_decay_logit, w_ret_o=m_w_ret_o, w_pool_grp=m_w_pool_grp,
               pool_scale=m_pool_scale, w_pool_o=m_w_pool_o, w_mem_kv=m_w_mem_kv, w_mem_o=m_w_mem_o,
               w_out=m_w_out, w_ff1=m_w_ff1, w_ff2=m_w_ff2, norm1_g=m_norm1_g, norm2_g=m_norm2_g,
               mem_norm_g=m_mem_norm_g, final_norm_g=m_final_norm_g)
    vel = dict(w_in=v_w_in, ret_decay_logit=v_ret_decay_logit, w_ret_o=v_w_ret_o, w_pool_grp=v_w_pool_grp,
               pool_scale=v_pool_scale, w_pool_o=v_w_pool_o, w_mem_kv=v_w_mem_kv, w_mem_o=v_w_mem_o,
               w_out=v_w_out, w_ff1=v_w_ff1, w_ff2=v_w_ff2, norm1_g=v_norm1_g, norm2_g=v_norm2_g,
               mem_norm_g=v_mem_norm_g, final_norm_g=v_final_norm_g)

    bl, seq, d = x.shape
    mlen = mem.shape[1]
    depth = w_in.shape[0]
    t = bl * seq
    dk = d // 8

    def natural(n, g):
        if n in _DW_SHARDED:
            return _columns_side_by_side("relayout_" + n, g)
        if n in _COL:
            return jnp.transpose(g, (1, 0, 2)).reshape(g.shape[1], -1)
        return g.reshape(-1, g.shape[-1])

    def finish_gather(name, names, started, after):
        return {n: natural(n, g) for n, g in zip(names, _exchange_wait(name, False, started, after))}

    shards = [{n: w[n][l].astype(MM) for n in _BIG} for l in range(depth)]
    rest = _BIG[1:]
    (w_in0,) = _all_gather("gather_w_in", [shards[0][_BIG[0]]])
    full = [{_BIG[0]: natural(_BIG[0], w_in0)}]
    def start_layer(l, after):
        s_in = _exchange_start(f"gather_start_in{l}", False, [shards[l][_BIG[0]]], after=after)
        s_rest = _exchange_start(f"gather_start_{l}", False, [shards[l][n] for n in rest], after=[s_in[4]])
        return s_in, s_rest, (s_in[4], s_rest[4])

    pending = (None, _exchange_start("gather_start_0", False, [shards[0][n] for n in rest], after=[w_in0]))
    first_tokens = (pending[1][4],)
    pending_next = None
    if depth > 1:
        pending_next = start_layer(1, [pending[1][4]])
        first_tokens += pending_next[2]

    inv = ROPE_BASE ** (-jnp.arange(0, dk, 2, dtype=F32) / dk)
    ang = jnp.arange(seq, dtype=F32)[:, None] * inv[None, :]
    cos2 = jnp.concatenate([jnp.cos(ang), jnp.cos(ang)], axis=-1)
    sin2 = jnp.concatenate([-jnp.sin(ang), jnp.sin(ang)], axis=-1)
    log_g = jax.nn.log_sigmoid(ret_decay_logit)
    x2 = x.reshape(t, d)
    mem2 = mem.reshape(bl * mlen, d)
    gmem = mem_norm_g.reshape(1, d)

    def merge(a_r, y_p, o_a, g_r, g_p, g_m, w_r, w_p, w_m):
        f = lambda z: z.astype(F32)
        o_r, o_p, o_m = _dot(a_r, w_r), _dot(y_p, w_p), _dot(o_a, w_m)
        return _sigmoid(f(g_r)) * o_r + _sigmoid(f(g_p)) * o_p + _sigmoid(f(g_m)) * o_m, o_r, o_p, o_m

    def relu2(u):
        r = jnp.maximum(u.astype(MM), 0.0)
        return r * r

    def ident(a):
        return a

    saved = []
    xc = x2
    for l in range(depth):
        s = dict(x_in=xc)
        started_now = ()
        if l > 0:
            pending, pending_next = pending_next, None
            full.append(finish_gather(f"gather_wait_in{l}", _BIG[:1], pending[0], xc))
            if l + 1 < depth:
                pending_next = start_layer(l + 1, [full[l]["w_in"]])
                started_now = pending_next[2]
        fw = full[l]
        g1 = norm1_g[l].reshape(1, d)
        g2 = norm2_g[l].reshape(1, d)
        s["proj"], s["h1"] = _pmm("proj", _rms_prologue, [(xc, d, 0)], [g1], fw["w_in"],
                                  tm=2048, tn=1024, save_a=True, out_dtypes=(MM,),
                                  after=started_now if l > 0 else first_tokens)
        proj = s["proj"]
        s["o_raw"], s["a_ret"], s["sf"], s["sb"], s["qr"], s["kr"] = _ret_core_fwd(proj, cos2, sin2, log_g[l],
                                                                                   d, bl, seq)
        s["y"] = _pool_fwd(proj, w_pool_grp[l], pool_scale[l].reshape(1, -1), d, bl, seq)
        fw.update(finish_gather(f"gather_wait_{l}", rest, pending[1], s["a_ret"]))
        s["kv"], s["memn"] = _pmm("mem_kv", _rms_prologue, [(mem2, d, 0)], [gmem], fw["w_mem_kv"],
                                  tm=512, tn=512, save_a=True)
        s["o_att"] = _attn_fwd(proj, s["kv"], d, bl, seq, mlen)
        s["x_mid"], s["merged"], s["o_ret"], s["o_pool"], s["o_mem"] = _pmm(
            "merge_out", merge,
            [(s["a_ret"], d, 0), (s["y"], d // 2, 0), (s["o_att"], d // 2, 0), (proj, d, 4), (proj, d, 5), (proj, d, 6)],
            [fw["w_ret_o"], fw["w_pool_o"], fw["w_mem_o"]], fw["w_out"], tm=512, tn=1024, residual=xc, save_a=True,
            extra_outs=[(d, MM)] * 3)
        s["u"], s["h2"] = _pmm("ff1", _rms_prologue, [(s["x_mid"], d, 0)], [g2], fw["w_ff1"],
                               tm=512, tn=4096, save_a=True, out_dtypes=(MM,))
        (xc,) = _pmm("ff2", relu2, [(s["u"], s["u"].shape[1], 0)], [], fw["w_ff2"],
                     tm=512, tn=1024, residual=s["x_mid"])
        saved.append(s)

    dxc, g_final, loss_part = _loss_head(xc, loss_target.reshape(t, d), final_norm_g.reshape(1, d))
    loss = lax.psum(loss_part[0, 0], ("x", "y", "c"))

    small_names = ("w_pool_grp", "pool_scale", "norm1_g", "norm2_g", "ret_decay_logit")
    grads = {n: [None] * depth for n in small_names}
    group_a = ("w_ff1", "w_ff2")
    group_b = tuple(n for n in _BIG if n not in group_a)
    scatters = {}
    dmemn = jnp.zeros((bl * mlen, d), F32)

    def relu2_bwd(acc, u):
        return (acc * (2.0 * jnp.maximum(u.astype(F32), 0.0)),)

    def gates_bwd(acc, g_r, g_p, g_m, o_r, o_p, o_m, w_r, w_p, w_m):
        d_os, d_gs, backs = [], [], []
        for gz, oz, wz in ((g_r, o_r, w_r), (g_p, o_p, w_p), (g_m, o_m, w_m)):
            sg = _sigmoid(gz.astype(F32))
            d_o = (acc * sg).astype(MM)
            d_os.append(d_o)
            d_gs.append(acc * oz.astype(F32) * (sg * (1.0 - sg)))
            backs.append(_dot_nt(d_o, wz))
        return tuple(d_os + d_gs + backs)

    def to_send(n, g):
        a, b = w[n].shape[1:]
        if n in _DW_SHARDED:
            return g
        if n in _COL:
            return jnp.transpose(g.reshape(a, N_DEV, b), (1, 0, 2))
        return g.reshape(N_DEV, a, b)

    for l in reversed(range(depth)):
        s = saved[l]
        fw = full[l]
        proj = s["proj"]
        g1 = norm1_g[l].reshape(1, d)
        g2 = norm2_g[l].reshape(1, d)
        dw = {}
        (du,) = _pmm("ff2_bwd", ident, [(dxc, d, 0)], [], fw["w_ff2"], w_mode="nt", tm=512, tn=4096,
                     epilogue=relu2_bwd, epi_ins=[(s["u"], 0)], out_dtypes=(MM,))
        dw["w_ff2"] = _tnmm("dw_ff2", s["u"], dxc, a_fn=relu2)
        dw["w_ff1"] = _tnmm("dw_ff1", s["h2"], du, col_shards=True)
        scatters[l, "a"] = _exchange_start(f"scatter_start_a{l}", True, [to_send(n, dw[n]) for n in group_a])
        dmid, grads["norm2_g"][l] = _mm_rms_bwd("ff1_norm2_bwd", du, fw["w_ff1"], s["x_mid"], g2, dxc, tm=512)
        d_oret, d_opool, d_omem, dgr, dgp, dgm, da_ret, dy, do_att = _pmm(
            "out_bwd", ident, [(dmid, d, 0)], [], fw["w_out"], w_mode="nt", tm=256, tn=d, epilogue=gates_bwd,
            epi_ins=[(proj, 4 * d), (proj, 5 * d), (proj, 6 * d), (s["o_ret"], 0), (s["o_pool"], 0), (s["o_mem"], 0)],
            epi_full=[fw["w_ret_o"], fw["w_pool_o"], fw["w_mem_o"]], out_dtypes=(MM,) * 6 + (F32,) * 3,
            out_widths=[d] * 7 + [d // 2] * 2, after=(scatters[l, "a"][4],))
        dw["w_out"] = _tnmm("dw_out", s["merged"], dmid)
        dw["w_ret_o"] = _tnmm("dw_ret_o", s["a_ret"], d_oret)
        dw["w_pool_o"] = _tnmm("dw_pool_o", s["y"], d_opool)
        dw["w_mem_o"] = _tnmm("dw_mem_o", s["o_att"], d_omem)
        dg_ret, do_ret = _ret_post_bwd(da_ret, proj, s["o_raw"], d)
        dq, dkk, dvv, dlf, dlb = _ret_core_bwd(s["qr"], s["kr"], proj, do_ret, s["sf"], s["sb"], cos2, sin2,
                                               log_g[l], d, bl, seq)
        dl = jnp.stack([dlf[:, 0, 0].reshape(bl, HEADS).sum(0), dlb[:, 0, 0].reshape(bl, HEADS).sum(0)])
        grads["ret_decay_logit"][l] = dl * jax.nn.sigmoid(-ret_decay_logit[l])
        dp, grads["w_pool_grp"][l], dscale = _pool_bwd(proj, dy, w_pool_grp[l], pool_scale[l].reshape(1, -1),
                                                       d, bl, seq)
        grads["pool_scale"][l] = dscale.reshape(-1)
        dqm, dmk, dmv = _attn_bwd(proj, s["kv"], do_att, d, bl, seq, mlen)
        dkv = jnp.concatenate([dmk, dmv], axis=-1).astype(MM)
        dw["w_mem_kv"] = _tnmm("dw_mem_kv", s["memn"], dkv)
        (dmemn,) = _pmm("mem_kv_bwd", None, [(dkv, d, 0)], [], fw["w_mem_kv"], w_mode="nt", tm=512, tn=512,
                        residual=dmemn)
        dproj = [dq, dkk, dvv, dg_ret, dp, dqm, dgr, dgp, dgm]
        dw["w_in"] = _tnmm("dw_in", s["h1"], dproj, col_shards=True, tm=512, tk=512)
        scatters[l, "b"] = _exchange_start(f"scatter_start_b{l}", True, [to_send(n, dw[n]) for n in group_b])
        dxc, grads["norm1_g"][l] = _mm_rms_bwd("proj_norm1_bwd", dproj, fw["w_in"], s["x_in"], g1, dmid, tm=256,
                                               after=(scatters[l, "b"][4],))

    _, g_memn = _rms_bwd("mem_norm_bwd", dmemn, mem2, gmem, None)
    grad_x = dxc.reshape(bl, seq, d)

    small_g = dict(ret_decay_logit=jnp.stack(grads["ret_decay_logit"]), w_pool_grp=jnp.stack(grads["w_pool_grp"]),
                   pool_scale=jnp.stack(grads["pool_scale"]),
                   norm1_g=jnp.concatenate(grads["norm1_g"], axis=0), norm2_g=jnp.concatenate(grads["norm2_g"], axis=0),
                   mem_norm_g=g_memn.reshape(-1), final_norm_g=g_final.reshape(-1))
    small_started = _exchange_start("gather_small_start", False,
                                    [_pack_small(small_g, d, _SMALL_MM)[0].astype(MM),
                                     _pack_small(small_g, d, _SMALL_F32)[0]])

    big = {n: [lax.empty(w[n].shape, F32) for _ in range(4)] for n in _BIG}

    def update(l, grp, names, after):
        recv = _exchange_wait(f"scatter_wait_{grp}{l}", True, scatters[l, grp], after)
        for n, parts in zip(names, recv):
            big[n] = _adamw("adamw_" + n, parts, w[n], mom[n], vel[n], big[n], l)
        return big[names[-1]][0]

    after = dxc
    for l in reversed(range(1, depth)):
        for grp, names in (("a", group_a), ("b", group_b)):
            after = update(l, grp, names, after)
    after = update(0, "a", group_a, after)
    small_lands = _exchange_wait("gather_small_wait", False, small_started, after)
    small = [{} for _ in range(4)]
    for names, parts in zip((_SMALL_MM, _SMALL_F32), small_lands):
        w_small = _pack_small(w, d, names)
        res = _adamw("adamw_small", parts, w_small, _pack_small(mom, d, names), _pack_small(vel, d, names),
                     [lax.empty(w_small.shape, F32) for _ in range(4)], 0)
        after = res[0]
        for k in range(4):
            small[k].update(_unpack_small(res[k], w, d, names))
    update(0, "b", group_b, after)

    outs = [loss, grad_x]
    for k in range(4):
        outs += [big[n][k] if n in _BIG else small[k][n] for n in _WEIGHTS]
    return tuple(outs)
```

```python
import jax
import jax.numpy as jnp
from jax import lax
from jax.experimental import pallas as pl
from jax.experimental.pallas import tpu as pltpu

F32 = jnp.float32
MM = jnp.bfloat16
N_DEV = 8
HEADS = 4
POOL_WINDOWS = (2, 4, 8, 16)
RET_CHUNK = 128
EPS = 1e-6
ROPE_BASE = 10000.0
ADAM_LR, ADAM_B1, ADAM_B2, ADAM_EPS, ADAM_WD, ADAM_STEP = 0.001, 0.9, 0.999, 1e-08, 0.01, 10
V7X_VMEM_LIMIT = 56 * 1024 * 1024
MESH = pl.DeviceIdType.MESH


def _params(n_axes):
    return pltpu.CompilerParams(dimension_semantics=("arbitrary",) * n_axes,
                                vmem_limit_bytes=V7X_VMEM_LIMIT)


def _tile(n, pref, align=128):
    cands = [c for c in range(align, min(pref, n) + 1, align) if n % c == 0]
    return max(cands) if cands else n


def _sigmoid(z):
    return 0.5 * jnp.tanh(0.5 * z) + 0.5


def _dot(a, b):
    return jnp.dot(a, b, preferred_element_type=F32)


def _dot_nt(a, b):
    return lax.dot_general(a, b, (((1,), (1,)), ((), ())), preferred_element_type=F32)


def _dot_tn(a, b):
    return lax.dot_general(a, b, (((0,), (0,)), ((), ())), preferred_element_type=F32)


def _pmm(name, prologue, row_ins, vec_ins, w, *, tm, tn, w_mode="nn", residual=None, save_a=False,
         epilogue=None, epi_ins=(), out_dtypes=(F32,), after=(), extra_outs=(), epi_full=(), out_widths=None):
    m = row_ins[0][0].shape[0]
    if w_mode == "nn":
        k, n = w.shape
        tn = _tile(n, tn)
        w_spec = pl.BlockSpec((k, tn), lambda i, j: (0, j))
    else:
        n, k = w.shape
        tn = _tile(n, tn)
        w_spec = pl.BlockSpec((tn, k), lambda i, j: (j, 0))
    tm = _tile(m, tm, 8)
    n_row, n_vec, n_epi, n_out = len(row_ins), len(vec_ins), len(epi_ins), len(out_dtypes)
    has_res = residual is not None
    use_scr = prologue is not None
    out_widths = [n] * n_out if out_widths is None else list(out_widths)
    assert all(wd == n for wd in out_widths) or tn == n

    def body(*refs):
        row_refs = refs[:n_row]
        p = n_row
        vec_refs = refs[p:p + n_vec]
        p += n_vec
        w_ref = refs[p]
        p += 1
        res_ref = refs[p] if has_res else None
        p += int(has_res)
        epi_refs = refs[p:p + n_epi + len(epi_full)]
        p += n_epi + len(epi_full) + len(after)
        out_refs = refs[p:p + n_out]
        p += n_out
        a_out = refs[p] if save_a else None
        p += int(save_a)
        extra_refs = refs[p:p + len(extra_outs)]
        p += len(extra_outs)
        if use_scr:
            a_src = refs[p]

            @pl.when(pl.program_id(1) == 0)
            def _():
                made = prologue(*[r[...] for r in row_refs], *[v[...] for v in vec_refs])
                made = made if isinstance(made, tuple) else (made,)
                a = made[0].astype(MM)
                a_src[...] = a
                if save_a:
                    a_out[...] = a
                for e_ref, e in zip(extra_refs, made[1:]):
                    e_ref[...] = e.astype(e_ref.dtype)
        else:
            a_src = row_refs[0]
        if w_mode == "nt":
            acc = _dot_nt(a_src[...], w_ref[...])
        else:
            acc = _dot(a_src[...], w_ref[...])
        if has_res:
            acc = acc + res_ref[...]
        outs = epilogue(acc, *[e[...] for e in epi_refs]) if epilogue is not None else (acc,)
        for o_ref, o in zip(out_refs, outs):
            o_ref[...] = o.astype(o_ref.dtype)

    in_specs = [pl.BlockSpec((tm, wd), lambda i, j, cb=cb: (i, cb)) for (_, wd, cb) in row_ins]
    in_specs += [pl.BlockSpec(v.shape, lambda i, j: (0, 0)) for v in vec_ins]
    in_specs += [w_spec]
    args = [r[0] for r in row_ins] + list(vec_ins) + [w]
    if has_res:
        in_specs.append(pl.BlockSpec((tm, tn), lambda i, j: (i, j)))
        args.append(residual)
    for (arr, off) in epi_ins:
        assert off % tn == 0
        in_specs.append(pl.BlockSpec((tm, tn), lambda i, j, ob=off // tn: (i, ob + j)))
        args.append(arr)
    in_specs += [pl.BlockSpec(v.shape, lambda i, j: (0, 0)) for v in epi_full]
    args += list(epi_full)
    n_after = len(after)
    in_specs += [pl.BlockSpec(memory_space=pl.ANY)] * n_after
    args += list(after)
    out_specs = [pl.BlockSpec((tm, tn if wd == n else wd), lambda i, j: (i, j)) for wd in out_widths]
    out_shape = [jax.ShapeDtypeStruct((m, wd), dt) for wd, dt in zip(out_widths, out_dtypes)]
    if save_a:
        out_specs.append(pl.BlockSpec((tm, k), lambda i, j: (i, 0)))
        out_shape.append(jax.ShapeDtypeStruct((m, k), MM))
    for wd, dt in extra_outs:
        out_specs.append(pl.BlockSpec((tm, wd), lambda i, j: (i, 0)))
        out_shape.append(jax.ShapeDtypeStruct((m, wd), dt))
    scratch = [pltpu.VMEM((tm, k), MM)] if use_scr else []
    return pl.pallas_call(body, name=name, grid=(m // tm, n // tn), in_specs=in_specs,
                          out_specs=out_specs, out_shape=out_shape, scratch_shapes=scratch,
                          compiler_params=_params(2))(*args)


def _tnmm(name, a, b, *, tm=1024, tn=1024, tk=1024, col_shards=False, a_fn=None):
    t, m = a.shape
    pieces = list(b) if isinstance(b, (list, tuple)) else [b]
    widths = [p.shape[1] for p in pieces]
    offs = [sum(widths[:p]) for p in range(len(pieces))]
    n = sum(widths)
    tm, tk = _tile(m, tm), _tile(t, tk, 8)
    per_tile = 1
    if col_shards:
        wb = n // N_DEV
        if len(pieces) > 1:
            tn = n
        while 2 * per_tile * wb <= tn and 2 * per_tile <= N_DEV:
            per_tile *= 2
        tn = per_tile * wb
        out_spec = pl.BlockSpec((per_tile, tm, wb), lambda i, j, kk: (j, i, 0))
        out_shape = jax.ShapeDtypeStruct((N_DEV, m, wb), MM)
    else:
        tn = _tile(n, tn)
        out_spec = pl.BlockSpec((tm, tn), lambda i, j, kk: (i, j))
        out_shape = jax.ShapeDtypeStruct((m, n), MM)
    nk = t // tk

    assert len(pieces) == 1 or tn == n

    def body(a_ref, *rest):
        b_refs, (o_ref, acc) = rest[:len(pieces)], rest[len(pieces):]
        kk = pl.program_id(2)

        @pl.when(kk == 0)
        def _():
            acc[...] = jnp.zeros_like(acc)

        av = (a_ref[...] if a_fn is None else a_fn(a_ref[...])).astype(MM)
        if len(pieces) == 1:
            acc[...] += _dot_tn(av, b_refs[0][...].astype(MM))
        else:
            for b_ref, off, wd in zip(b_refs, offs, widths):
                acc[:, off:off + wd] += _dot_tn(av, b_ref[...].astype(MM))

        @pl.when(kk == nk - 1)
        def _():
            if col_shards:
                for sh in range(per_tile):
                    o_ref[sh] = acc[:, sh * wb:(sh + 1) * wb].astype(o_ref.dtype)
            else:
                o_ref[...] = acc[...].astype(o_ref.dtype)

    return pl.pallas_call(
        body, name=name, grid=(m // tm, n // tn, nk),
        in_specs=[pl.BlockSpec((tk, tm), lambda i, j, kk: (kk, i))]
        + [pl.BlockSpec((tk, tn if len(pieces) == 1 else wd), lambda i, j, kk: (kk, j)) for wd in widths],
        out_specs=out_spec, out_shape=out_shape,
        scratch_shapes=[pltpu.VMEM((tm, tn), F32)],
        compiler_params=_params(3))(a, *pieces)


def _rms_prologue(x, g):
    r = lax.rsqrt(jnp.mean(x * x, axis=-1, keepdims=True) + EPS)
    return x * r * g


def _rms_bwd_rows(dh, x, g):
    d = x.shape[-1]
    r = lax.rsqrt(jnp.mean(x * x, axis=-1, keepdims=True) + EPS)
    xh = x * r
    dxh = dh * g
    dx = r * (dxh - xh * (jnp.sum(dxh * xh, axis=-1, keepdims=True) / d))
    dg = jnp.sum(dh * xh, axis=0, keepdims=True)
    return dx, dg


def _rms_bwd(name, dh, x, g, dres, *, tm=512):
    m, d = x.shape
    tm = min(tm, m)
    has_res = dres is not None

    def body(*refs):
        if has_res:
            dh_ref, x_ref, g_ref, r_ref, dx_ref, dg_ref = refs
        else:
            dh_ref, x_ref, g_ref, dx_ref, dg_ref = refs
        dx, dg = _rms_bwd_rows(dh_ref[...], x_ref[...], g_ref[...])
        if has_res:
            dx = dx + r_ref[...]
        dx_ref[...] = dx

        @pl.when(pl.program_id(0) == 0)
        def _():
            dg_ref[...] = jnp.zeros_like(dg_ref)

        dg_ref[...] += dg

    row = pl.BlockSpec((tm, d), lambda i: (i, 0))
    vec = pl.BlockSpec((1, d), lambda i: (0, 0))
    in_specs = [row, row, vec] + ([row] if has_res else [])
    args = [dh, x, g] + ([dres] if has_res else [])
    return pl.pallas_call(body, name=name, grid=(m // tm,), in_specs=in_specs, out_specs=[row, vec],
                          out_shape=[jax.ShapeDtypeStruct((m, d), F32), jax.ShapeDtypeStruct((1, d), F32)],
                          compiler_params=_params(1))(*args)


def _mm_rms_bwd(name, a, w, x, g, dres, *, tm, after=()):
    pieces = list(a) if isinstance(a, (list, tuple)) else [a]
    widths = [p.shape[1] for p in pieces]
    m = pieces[0].shape[0]
    d = w.shape[0]
    tm = _tile(m, tm, 8)
    n_a = len(pieces)

    def body(*refs):
        a_refs = refs[:n_a]
        w_ref, x_ref, g_ref, r_ref = refs[n_a:n_a + 4]
        dx_ref, dg_ref = refs[n_a + 4 + len(after):]

        av = a_refs[0][...] if n_a == 1 else jnp.concatenate([a_ref[...] for a_ref in a_refs], axis=1)
        dh = _dot_nt(av, w_ref[...])
        dx, dg = _rms_bwd_rows(dh, x_ref[...], g_ref[...])
        dx_ref[...] = dx + r_ref[...]

        @pl.when(pl.program_id(0) == 0)
        def _():
            dg_ref[...] = jnp.zeros_like(dg_ref)

        dg_ref[...] += dg

    row = pl.BlockSpec((tm, d), lambda i: (i, 0))
    vec = pl.BlockSpec((1, d), lambda i: (0, 0))
    return pl.pallas_call(
        body, name=name, grid=(m // tm,),
        in_specs=[pl.BlockSpec((tm, wd), lambda i: (i, 0)) for wd in widths]
        + [pl.BlockSpec(w.shape, lambda i: (0, 0)), row, vec, row]
        + [pl.BlockSpec(memory_space=pl.ANY)] * len(after),
        out_specs=[row, vec],
        out_shape=[jax.ShapeDtypeStruct((m, d), F32), jax.ShapeDtypeStruct((1, d), F32)],
        compiler_params=_params(1))(*pieces, w, x, g, dres, *after)


def _loss_head(x, target, g, *, tm=256):
    m, d = x.shape
    tm = min(tm, m)

    def body(x_ref, t_ref, g_ref, dx_ref, dg_ref, loss_ref):
        xv, gv = x_ref[...], g_ref[...]
        y = _rms_prologue(xv, gv)
        err = y - t_ref[...]
        part = 0.5 * jnp.sum(jnp.sum(err * err, axis=-1, keepdims=True) / d)
        dx, dg = _rms_bwd_rows(err / d, xv, gv)
        dx_ref[...] = dx

        @pl.when(pl.program_id(0) == 0)
        def _():
            dg_ref[...] = jnp.zeros_like(dg_ref)
            loss_ref[...] = jnp.zeros_like(loss_ref)

        dg_ref[...] += dg
        loss_ref[...] += jnp.full(loss_ref.shape, part, F32)

    row = pl.BlockSpec((tm, d), lambda i: (i, 0))
    vec = pl.BlockSpec((1, d), lambda i: (0, 0))
    lspec = pl.BlockSpec((1, 128), lambda i: (0, 0))
    return pl.pallas_call(body, name="loss_head", grid=(m // tm,), in_specs=[row, row, vec],
                          out_specs=[row, vec, lspec],
                          out_shape=[jax.ShapeDtypeStruct((m, d), F32), jax.ShapeDtypeStruct((1, d), F32),
                                     jax.ShapeDtypeStruct((1, 128), F32)],
                          compiler_params=_params(1))(x, target, g)


def _rot(xv, cos2, sin2, half):
    return xv * cos2 + pltpu.roll(xv, half, 1) * sin2


def _rot_t(dv, cos2, sin2, half):
    return dv * cos2 + pltpu.roll(dv * sin2, half, 1)


def _ret_consts(lg_ref, h, t, dk):
    lf, lb = lg_ref[0, h], lg_ref[1, h]
    ab = (lax.broadcasted_iota(jnp.int32, (t, t), 0) - lax.broadcasted_iota(jnp.int32, (t, t), 1)).astype(F32)
    dmat = jnp.exp(jnp.where(ab >= 0, lf * ab, -lb * ab))
    up = lax.broadcasted_iota(jnp.int32, (t, dk), 0).astype(F32) + 1.0
    down = float(t) - up
    one = jnp.ones((1, 1), F32)
    return dict(ab=ab, dmat=dmat, xi_f=jnp.exp(lf * up), zeta_f=jnp.exp(lf * down), xi_b=jnp.exp(lb * up),
                zeta_b=jnp.exp(lb * down), up=up[:, 0:1], down=down[:, 0:1],
                cf=jnp.exp(one * (lf * t)), cb=jnp.exp(one * (lb * t)))


def _scaled(xv, rows):
    return (xv.astype(F32) * rows).astype(MM)


def _ret_core_fwd(proj, cos2, sin2, lg, d, bl, seq, *, tc=RET_CHUNK):
    t = proj.shape[0]
    dk, dv = d // 8, d // 4
    tc = min(tc, seq)
    nc = seq // tc
    scale = float(dk) ** -0.5

    def body(lg_ref, qp_ref, kp_ref, v_ref, g_ref, c_ref, s_ref, o_ref, a_ref, sf_ref, sb_ref, q_ref, k_ref):
        c = _ret_consts(lg_ref, pl.program_id(1), tc, dk)

        def rows_of(i):
            return pl.ds(pl.multiple_of(i * tc, tc), tc)

        def rotate(i, carry):
            rows = rows_of(i)
            cs, sn = c_ref[rows, :], s_ref[rows, :]
            q_ref[rows, :] = _rot(qp_ref[rows, :].astype(F32), cs, sn, dk // 2).astype(MM)
            k_ref[rows, :] = (_rot(kp_ref[rows, :].astype(F32), cs, sn, dk // 2) * scale).astype(MM)
            return carry

        lax.fori_loop(0, nc, rotate, 0)

        def fwd_step(i, sf):
            rows = rows_of(i)
            sf_ref[i] = sf
            q, kk, v = q_ref[rows, :], k_ref[rows, :], v_ref[rows, :]
            p = (_dot_nt(q, kk) * c["dmat"]).astype(MM)
            o_ref[rows, :] = _dot(p, v) + _dot(_scaled(q, c["xi_f"]), sf.astype(MM))
            return sf * c["cf"] + _dot_tn(_scaled(kk, c["zeta_f"]), v)

        lax.fori_loop(0, nc, fwd_step, jnp.zeros((dk, dv), F32))

        def bwd_step(ii, sb):
            rows = rows_of(nc - 1 - ii)
            sb_ref[nc - 1 - ii] = sb
            q, kk, v = q_ref[rows, :], k_ref[rows, :], v_ref[rows, :]
            o_ref[rows, :] += _dot(_scaled(q, c["zeta_b"]), sb.astype(MM))
            return sb * c["cb"] + _dot_tn(_scaled(kk, c["xi_b"]), v)

        lax.fori_loop(0, nc, bwd_step, jnp.zeros((dk, dv), F32))

        def post(i, carry):
            rows = rows_of(i)
            o = o_ref[rows, :]
            oc = o - jnp.mean(o, axis=-1, keepdims=True)
            on = oc * lax.rsqrt(jnp.mean(oc * oc, axis=-1, keepdims=True) + EPS)
            g = g_ref[rows, :].astype(F32)
            a_ref[rows, :] = (on * (g * _sigmoid(g))).astype(MM)
            return carry

        lax.fori_loop(0, nc, post, 0)

    qk = pl.BlockSpec((seq, dk), lambda b, h: (b, h))
    vv = pl.BlockSpec((seq, dv), lambda b, h: (b, h))
    tab = pl.BlockSpec((seq, dk), lambda b, h: (0, 0))
    states = pl.BlockSpec((None, nc, dk, dv), lambda b, h: (b * HEADS + h, 0, 0, 0))
    return pl.pallas_call(
        body, name="ret_core_fwd", grid=(bl, HEADS),
        in_specs=[pl.BlockSpec(memory_space=pltpu.SMEM), qk, pl.BlockSpec((seq, dk), lambda b, h: (b, HEADS + h)),
                  pl.BlockSpec((seq, dv), lambda b, h: (b, HEADS + h)),
                  pl.BlockSpec((seq, dv), lambda b, h: (b, 2 * HEADS + h)), tab, tab],
        out_specs=[vv, vv, states, states, qk, qk],
        out_shape=[jax.ShapeDtypeStruct((t, d), F32), jax.ShapeDtypeStruct((t, d), MM),
                   jax.ShapeDtypeStruct((bl * HEADS, nc, dk, dv), F32),
                   jax.ShapeDtypeStruct((bl * HEADS, nc, dk, dv), F32),
                   jax.ShapeDtypeStruct((t, d // 2), MM), jax.ShapeDtypeStruct((t, d // 2), MM)],
        compiler_params=_params(2))(lg, proj, proj, proj, proj, cos2, sin2)


def _ret_post_bwd(da, proj, o_raw, d, *, ts=2048):
    t = da.shape[0]
    dv = d // 4
    ts = min(ts, t)

    def body(da_ref, g_ref, o_ref, dg_ref, do_ref):
        o, g, dav = o_ref[...], g_ref[...].astype(F32), da_ref[...]
        mu = jnp.mean(o, axis=-1, keepdims=True)
        oc = o - mu
        r = lax.rsqrt(jnp.mean(oc * oc, axis=-1, keepdims=True) + EPS)
        on = oc * r
        sg = _sigmoid(g)
        don = dav * (g * sg)
        dg_ref[...] = (dav * on * (sg * (1.0 + g * (1.0 - sg)))).astype(MM)
        do = r * (don - jnp.mean(don, axis=-1, keepdims=True) - on * jnp.mean(don * on, axis=-1, keepdims=True))
        do_ref[...] = do.astype(MM)

    blk = pl.BlockSpec((ts, dv), lambda i, h: (i, h))
    return pl.pallas_call(
        body, name="ret_post_bwd", grid=(t // ts, HEADS),
        in_specs=[blk, pl.BlockSpec((ts, dv), lambda i, h: (i, 2 * HEADS + h)), blk],
        out_specs=[blk, blk],
        out_shape=[jax.ShapeDtypeStruct((t, d), MM), jax.ShapeDtypeStruct((t, d), MM)],
        compiler_params=_params(2))(da, proj, o_raw)


def _ret_core_bwd(qr, kr, proj, do, sf_in, sb_in, cos2, sin2, lg, d, bl, seq, *, tc=RET_CHUNK):
    t = qr.shape[0]
    dk, dv = d // 8, d // 4
    tc = min(tc, seq)
    nc = seq // tc
    scale = float(dk) ** -0.5

    def body(lg_ref, q_ref, k_ref, v_ref, do_ref, sf_all, sb_all, c_ref, s_ref, dq_ref, dk_ref, dv_ref,
             dlf_ref, dlb_ref, dq_acc, dk_acc, dv_acc):
        c = _ret_consts(lg_ref, pl.program_id(1), tc, dk)
        fwd = c["ab"] >= 0
        zero_state = jnp.zeros((dk, dv), F32)
        zero = jnp.zeros((1, 1), F32)

        def rows_of(i):
            return pl.ds(pl.multiple_of(i * tc, tc), tc)

        def total(xv):
            return jnp.sum(xv, keepdims=True)

        def fwd_sweep(i, carry):
            hh, dlf, dlb = carry
            rows = rows_of(i)
            q, kk, v, dov = q_ref[rows, :], k_ref[rows, :], v_ref[rows, :], do_ref[rows, :]
            dof, vf = dov.astype(F32), v.astype(F32)
            p = _dot_nt(q, kk) * c["dmat"]
            da = _dot_nt(dov, v)
            x = p * da * c["ab"]
            dlf = dlf + total(jnp.where(fwd, x, 0.0))
            dlb = dlb - total(jnp.where(fwd, 0.0, x))
            pb, dpb = p.astype(MM), (da * c["dmat"]).astype(MM)
            dq = _dot(dpb, kk)
            dkc = _dot_tn(dpb, q)
            dvc = _dot_tn(pb, dov)
            sf, sb = sf_all[i], sb_all[i]
            sfb, sbb = sf.astype(MM), sb.astype(MM)
            q_xf, q_zb = _scaled(q, c["xi_f"]), _scaled(q, c["zeta_b"])
            dq = dq + _dot_nt(dov, sfb) * c["xi_f"] + _dot_nt(dov, sbb) * c["zeta_b"]
            dlf = dlf + total(jnp.sum(_dot(q_xf, sfb) * dof, axis=-1, keepdims=True) * c["up"])
            dlb = dlb + total(jnp.sum(_dot(q_zb, sbb) * dof, axis=-1, keepdims=True) * c["down"])
            hb = hh.astype(MM)
            dkc = dkc + _dot_nt(v, hb) * c["xi_b"]
            dv_bx = _dot(_scaled(kk, c["xi_b"]), hb)
            dlb = dlb + total(jnp.sum(vf * dv_bx, axis=-1, keepdims=True) * c["up"])
            dlb = dlb + float(tc) * total(hh * (sb * c["cb"]))
            dq_acc[rows, :] = dq
            dk_acc[rows, :] = dkc
            dv_acc[rows, :] = dvc + dv_bx
            return hh * c["cb"] + _dot_tn(q_zb, dov), dlf, dlb

        _, dlf, dlb = lax.fori_loop(0, nc, fwd_sweep, (zero_state, zero, zero))

        def rev_sweep(ii, carry):
            gg, dlf = carry
            i = nc - 1 - ii
            rows = rows_of(i)
            q, kk, v, dov = q_ref[rows, :], k_ref[rows, :], v_ref[rows, :], do_ref[rows, :]
            gb = gg.astype(MM)
            dk_acc[rows, :] += _dot_nt(v, gb) * c["zeta_f"]
            dv_fx = _dot(_scaled(kk, c["zeta_f"]), gb)
            dv_acc[rows, :] += dv_fx
            dlf = dlf + total(jnp.sum(v.astype(F32) * dv_fx, axis=-1, keepdims=True) * c["down"])
            dlf = dlf + float(tc) * total(gg * (sf_all[i] * c["cf"]))
            return gg * c["cf"] + _dot_tn(_scaled(q, c["xi_f"]), dov), dlf

        _, dlf = lax.fori_loop(0, nc, rev_sweep, (zero_state, dlf))

        cs, sn = c_ref[...], s_ref[...]
        dq_ref[...] = _rot_t(dq_acc[...], cs, sn, dk // 2).astype(MM)
        dk_ref[...] = (_rot_t(dk_acc[...], cs, sn, dk // 2) * scale).astype(MM)
        dv_ref[...] = dv_acc[...].astype(MM)
        dlf_ref[...] = jnp.broadcast_to(dlf, dlf_ref.shape)
        dlb_ref[...] = jnp.broadcast_to(dlb, dlb_ref.shape)

    qk = pl.BlockSpec((seq, dk), lambda b, h: (b, h))
    vv = pl.BlockSpec((seq, dv), lambda b, h: (b, h))
    tab = pl.BlockSpec((seq, dk), lambda b, h: (0, 0))
    dl = pl.BlockSpec((None, 8, 128), lambda b, h: (b * HEADS + h, 0, 0))
    states = pl.BlockSpec((None, nc, dk, dv), lambda b, h: (b * HEADS + h, 0, 0, 0))
    return pl.pallas_call(
        body, name="ret_core_bwd", grid=(bl, HEADS),
        in_specs=[pl.BlockSpec(memory_space=pltpu.SMEM), qk, qk, pl.BlockSpec((seq, dv), lambda b, h: (b, HEADS + h)),
                  vv, states, states, tab, tab],
        out_specs=[qk, qk, vv, dl, dl],
        out_shape=[jax.ShapeDtypeStruct((t, d // 2), MM), jax.ShapeDtypeStruct((t, d // 2), MM),
                   jax.ShapeDtypeStruct((t, d), MM),
                   jax.ShapeDtypeStruct((bl * HEADS, 8, 128), F32), jax.ShapeDtypeStruct((bl * HEADS, 8, 128), F32)],
        scratch_shapes=[pltpu.VMEM((seq, dk), F32), pltpu.VMEM((seq, dk), F32), pltpu.VMEM((seq, dv), F32)],
        compiler_params=_params(2))(lg, qr, kr, proj, do, sf_in, sb_in, cos2, sin2)


def _window_count(row, w, seq):
    return (jnp.minimum(row + w // 2, seq) - jnp.maximum(row - w // 2, 0)).astype(F32)


def _window_sum(pv, row, w, seq, sign):
    acc = None
    for j in range(-(w // 2), w // 2):
        if j == 0:
            term = pv
        else:
            src = row + sign * j
            term = jnp.where((src >= 0) & (src < seq), pltpu.roll(pv, (-sign * j) % seq, 0), 0.0)
        acc = term if acc is None else acc + term
    return acc


def _pool_fwd(proj, w_grp, scale, d, bl, seq):
    t = proj.shape[0]
    dg = d // 8

    def body(p_ref, w_ref, s_ref, y_ref):
        row = lax.broadcasted_iota(jnp.int32, (seq, dg), 0)
        for gi, w in enumerate(POOL_WINDOWS):
            sl = slice(gi * dg, (gi + 1) * dg)
            pg = p_ref[:, sl].astype(F32)
            mixed = _window_sum(pg, row, w, seq, 1) / _window_count(row, w, seq) - pg
            yp = _dot(mixed.astype(MM), w_ref[gi].astype(MM))
            y_ref[:, sl] = (yp * s_ref[:, sl]).astype(MM)

    return pl.pallas_call(
        body, name="pool_fwd", grid=(bl,),
        in_specs=[pl.BlockSpec((seq, d // 2), lambda b: (b, 6)),
                  pl.BlockSpec(w_grp.shape, lambda b: (0, 0, 0)),
                  pl.BlockSpec((1, d // 2), lambda b: (0, 0))],
        out_specs=pl.BlockSpec((seq, d // 2), lambda b: (b, 0)),
        out_shape=jax.ShapeDtypeStruct((t, d // 2), MM),
        compiler_params=_params(1))(proj, w_grp, scale)


def _pool_bwd(proj, dy, w_grp, scale, d, bl, seq):
    t = proj.shape[0]
    dg = d // 8

    def body(p_ref, dy_ref, w_ref, s_ref, dp_ref, dw_ref, ds_ref):
        @pl.when(pl.program_id(0) == 0)
        def _():
            dw_ref[...] = jnp.zeros_like(dw_ref)
            ds_ref[...] = jnp.zeros_like(ds_ref)

        row = lax.broadcasted_iota(jnp.int32, (seq, dg), 0)
        for gi, w in enumerate(POOL_WINDOWS):
            sl = slice(gi * dg, (gi + 1) * dg)
            pg = p_ref[:, sl].astype(F32)
            cnt = _window_count(row, w, seq)
            mixb = (_window_sum(pg, row, w, seq, 1) / cnt - pg).astype(MM)
            wgb = w_ref[gi].astype(MM)
            yp = _dot(mixb, wgb)
            dyg = dy_ref[:, sl]
            ds_ref[:, sl] += jnp.sum(dyg * yp, axis=0, keepdims=True)
            dyp = (dyg * s_ref[:, sl]).astype(MM)
            dmixed = _dot_nt(dyp, wgb)
            dw_ref[gi] += _dot_tn(mixb, dyp)
            dp_ref[:, sl] = (_window_sum(dmixed / cnt, row, w, seq, -1) - dmixed).astype(MM)

    half = pl.BlockSpec((seq, d // 2), lambda b: (b, 0))
    wspec = pl.BlockSpec(w_grp.shape, lambda b: (0, 0, 0))
    sspec = pl.BlockSpec((1, d // 2), lambda b: (0, 0))
    return pl.pallas_call(
        body, name="pool_bwd", grid=(bl,),
        in_specs=[pl.BlockSpec((seq, d // 2), lambda b: (b, 6)), half, wspec, sspec],
        out_specs=[half, wspec, sspec],
        out_shape=[jax.ShapeDtypeStruct((t, d // 2), MM), jax.ShapeDtypeStruct(w_grp.shape, F32),
                   jax.ShapeDtypeStruct((1, d // 2), F32)],
        compiler_params=_params(1))(proj, dy, w_grp, scale)


def _attn_probs(q, kk, dh):
    s = _dot_nt(q, kk) * (float(dh) ** -0.5)
    e = jnp.exp(s - jnp.max(s, axis=-1, keepdims=True))
    return e / jnp.sum(e, axis=-1, keepdims=True)


def _attn_fwd(proj, kv, d, bl, seq, mlen, *, tq=2048):
    t = proj.shape[0]
    dh = d // 8
    tq = min(tq, seq)
    nq = seq // tq

    def body(q_ref, k_ref, v_ref, o_ref):
        a = _attn_probs(q_ref[...].astype(MM), k_ref[...].astype(MM), dh)
        o_ref[...] = _dot(a.astype(MM), v_ref[...].astype(MM)).astype(MM)

    return pl.pallas_call(
        body, name="attn_fwd", grid=(bl, HEADS, nq),
        in_specs=[pl.BlockSpec((tq, dh), lambda b, h, i: (b * nq + i, 7 * HEADS + h)),
                  pl.BlockSpec((mlen, dh), lambda b, h, i: (b, h)),
                  pl.BlockSpec((mlen, dh), lambda b, h, i: (b, HEADS + h))],
        out_specs=pl.BlockSpec((tq, dh), lambda b, h, i: (b * nq + i, h)),
        out_shape=jax.ShapeDtypeStruct((t, d // 2), MM),
        compiler_params=_params(3))(proj, kv, kv)


def _attn_bwd(proj, kv, do, d, bl, seq, mlen, *, tq=2048):
    t = proj.shape[0]
    dh = d // 8
    tq = min(tq, seq)
    nq = seq // tq

    def body(q_ref, k_ref, v_ref, do_ref, dq_ref, dk_ref, dv_ref):
        @pl.when(pl.program_id(2) == 0)
        def _():
            dk_ref[...] = jnp.zeros_like(dk_ref)
            dv_ref[...] = jnp.zeros_like(dv_ref)

        q, kk, vv = q_ref[...].astype(MM), k_ref[...].astype(MM), v_ref[...].astype(MM)
        dov = do_ref[...].astype(MM)
        a = _attn_probs(q, kk, dh)
        dp = _dot_nt(dov, vv)
        ds = (a * (dp - jnp.sum(dp * a, axis=-1, keepdims=True)) * (float(dh) ** -0.5)).astype(MM)
        dq_ref[...] = _dot(ds, kk).astype(MM)
        dk_ref[...] += _dot_tn(ds, q)
        dv_ref[...] += _dot_tn(a.astype(MM), dov)

    qs = pl.BlockSpec((tq, dh), lambda b, h, i: (b * nq + i, h))
    ms = pl.BlockSpec((mlen, dh), lambda b, h, i: (b, h))
    return pl.pallas_call(
        body, name="attn_bwd", grid=(bl, HEADS, nq),
        in_specs=[pl.BlockSpec((tq, dh), lambda b, h, i: (b * nq + i, 7 * HEADS + h)), ms,
                  pl.BlockSpec((mlen, dh), lambda b, h, i: (b, HEADS + h)), qs],
        out_specs=[qs, ms, ms],
        out_shape=[jax.ShapeDtypeStruct((t, d // 2), MM), jax.ShapeDtypeStruct((bl * mlen, d // 2), F32),
                   jax.ShapeDtypeStruct((bl * mlen, d // 2), F32)],
        compiler_params=_params(3))(proj, kv, kv, do)


def _comm_call(name, body, arrays, out_shapes):
    n = len(arrays)
    hbm = pl.BlockSpec(memory_space=pl.ANY)
    return pl.pallas_call(
        body, name=name, out_shape=out_shapes, in_specs=[hbm] * n, out_specs=[hbm] * n,
        scratch_shapes=[pltpu.SemaphoreType.DMA((7 * n,)), pltpu.SemaphoreType.DMA((7 * n,)),
                        pltpu.SemaphoreType.DMA((n,))],
    )(*arrays)


def _all_gather(name, shards):
    n = len(shards)

    def body(*refs):
        x_refs, out_refs = refs[:n], refs[n:2 * n]
        send_sems, recv_sems, local_sems = refs[2 * n:]
        x, y, c = lax.axis_index("x"), lax.axis_index("y"), lax.axis_index("c")
        me, sibling = (x, y, c), (x, y, 1 - c)
        chips = [(1 - x, y), (x, 1 - y), (1 - x, 1 - y)]

        def copy(o, k, block, to, src=None):
            slot = out_refs[o].at[4 * block[0] + 2 * block[1] + block[2]]
            return pltpu.make_async_remote_copy(
                src_ref=slot if src is None else src, dst_ref=slot, send_sem=send_sems.at[7 * o + k],
                recv_sem=recv_sems.at[7 * o + k], device_id=to, device_id_type=MESH)

        locals_, remotes = [], []
        for o in range(n):
            mine = pltpu.make_async_copy(x_refs[o], out_refs[o].at[4 * x + 2 * y + c], local_sems.at[o])
            mine.start()
            locals_.append(mine)
            first = [copy(o, 0, me, sibling, src=x_refs[o])]
            first += [copy(o, 1 + j, me, (*chip, c), src=x_refs[o]) for j, chip in enumerate(chips)]
            for cp in first:
                cp.start()
            remotes += first
        for o in range(n):
            for j, chip in enumerate(chips):
                copy(o, 1 + j, (*chip, c), me).wait_recv()
                passed = copy(o, 4 + j, (*chip, c), sibling)
                passed.start()
                remotes.append(passed)
        for o in range(n):
            copy(o, 0, sibling, me).wait_recv()
            for j, chip in enumerate(chips):
                copy(o, 4 + j, (*chip, 1 - c), me).wait_recv()
        for cp in remotes:
            cp.wait_send()
        for mine in locals_:
            mine.wait()

    outs = [jax.ShapeDtypeStruct((N_DEV,) + s.shape, s.dtype) for s in shards]
    return _comm_call(name, body, shards, outs)


def _columns_side_by_side(name, g):
    _, k, wb = g.shape

    def body(x_ref, o_ref):
        o_ref[...] = x_ref[...]

    return pl.pallas_call(
        body, name=name, grid=(N_DEV,), in_specs=[pl.BlockSpec((None, k, wb), lambda j: (j, 0, 0))],
        out_specs=pl.BlockSpec((k, wb), lambda j: (0, j)),
        out_shape=jax.ShapeDtypeStruct((k, N_DEV * wb), g.dtype), compiler_params=_params(1))(g)


def _peer_of(k, x, y, c):
    peer = (1 - x if k & 4 else x, 1 - y if k & 2 else y, 1 - c if k & 1 else c)
    return peer, 4 * peer[0] + 2 * peer[1] + peer[2]


def _split_copies(scatter, srcs, lands, send_sems, recv_sems, arriving):
    x, y, c = lax.axis_index("x"), lax.axis_index("y"), lax.axis_index("c")
    me_idx = 4 * x + 2 * y + c
    copies = []
    for o, (src, land) in enumerate(zip(srcs, lands)):
        for k in range(1, N_DEV):
            peer, p_idx = _peer_of(k, x, y, c)
            mine = src.at[p_idx] if scatter else src
            sems = dict(send_sem=send_sems.at[7 * o + k - 1], recv_sem=recv_sems.at[7 * o + k - 1],
                        device_id=peer, device_id_type=MESH)
            slot = land.at[p_idx] if arriving else land.at[me_idx]
            copies.append(pltpu.make_async_remote_copy(src_ref=mine, dst_ref=slot, **sems))
    return copies


_HBM = pl.BlockSpec(memory_space=pltpu.HBM)
_SEM = pl.BlockSpec(memory_space=pltpu.SEMAPHORE)
_EFFECT = pltpu.SideEffectType.DATAFLOW_SIDE_EFFECTING


def _own_slot_copies(scatter, srcs, lands, local_sems):
    me_idx = 4 * lax.axis_index("x") + 2 * lax.axis_index("y") + lax.axis_index("c")
    return [pltpu.make_async_copy(src.at[me_idx] if scatter else src, land.at[me_idx], local_sems.at[o])
            for o, (src, land) in enumerate(zip(srcs, lands))]


def _exchange_start(name, scatter, arrays, after=()):
    n = len(arrays)
    lands = [lax.empty(a.shape if scatter else (N_DEV,) + a.shape, a.dtype) for a in arrays]

    def body(*refs):
        srcs, lnds = refs[:n], refs[n:2 * n]
        send_sems, recv_sems, local_sems = refs[2 * n + len(after):2 * n + len(after) + 3]
        token = refs[-1]
        for cp in _split_copies(scatter, srcs, lnds, send_sems, recv_sems, False):
            cp.start()
        for cp in _own_slot_copies(scatter, srcs, lnds, local_sems):
            cp.start()
        token[...] = jnp.zeros_like(token)

    hbm_in = [pltpu.with_memory_space_constraint(a, pltpu.HBM) for a in list(arrays) + lands]
    res = pl.pallas_call(
        body, name=name,
        out_shape=(pltpu.SemaphoreType.DMA((7 * n,)), pltpu.SemaphoreType.DMA((7 * n,)), pltpu.SemaphoreType.DMA((n,)),
                   *[pltpu.HBM(a.shape, a.dtype) for a in hbm_in], jax.ShapeDtypeStruct((8, 128), F32)),
        in_specs=[_HBM] * (2 * n) + [pl.BlockSpec(memory_space=pl.ANY)] * len(after),
        out_specs=(_SEM, _SEM, _SEM, *[_HBM] * (2 * n), pl.BlockSpec(memory_space=pltpu.VMEM)),
        input_output_aliases={i: 3 + i for i in range(2 * n)},
        compiler_params=pltpu.CompilerParams(has_side_effects=_EFFECT),
    )(*hbm_in, *after)
    return res[:3], None, list(res[3:3 + n]), list(res[3 + n:3 + 2 * n]), res[-1]


def _exchange_wait(name, scatter, started, after):
    sems, _, srcs, lands, _ = started
    n = len(srcs)

    def body(*refs):
        src_refs, lnd_refs = refs[:n], refs[n:2 * n]
        send_sems, recv_sems, local_sems = refs[2 * n:2 * n + 3]
        for cp in _split_copies(scatter, src_refs, lnd_refs, send_sems, recv_sems, False):
            cp.wait_send()
        for cp in _split_copies(scatter, src_refs, lnd_refs, send_sems, recv_sems, True):
            cp.wait_recv()
        for cp in _own_slot_copies(scatter, src_refs, lnd_refs, local_sems):
            cp.wait()

    res = pl.pallas_call(
        body, name=name, out_shape=tuple(pltpu.HBM(a.shape, a.dtype) for a in srcs + lands),
        in_specs=[_HBM] * (2 * n) + [_SEM, _SEM, _SEM, pl.BlockSpec(memory_space=pl.ANY)],
        out_specs=tuple([_HBM] * (2 * n)), input_output_aliases={i: i for i in range(2 * n)},
        compiler_params=pltpu.CompilerParams(has_side_effects=_EFFECT),
    )(*srcs, *lands, *sems, after)
    return list(res[n:])


def _adamw(name, parts, w, m, v, prev, layer, *, tr=256):
    _, a, b = w.shape
    tr = _tile(a, tr, 8)
    c1 = 1.0 - ADAM_B1 ** ADAM_STEP
    c2 = 1.0 - ADAM_B2 ** ADAM_STEP

    def body(p_ref, w_ref, m_ref, v_ref, _g, _d, _m, _v, g_out, d_out, m_out, v_out):
        g = p_ref[0].astype(F32)
        for s in range(1, N_DEV):
            g = g + p_ref[s].astype(F32)
        mn = ADAM_B1 * m_ref[...] + (1.0 - ADAM_B1) * g
        vn = ADAM_B2 * v_ref[...] + (1.0 - ADAM_B2) * (g * g)
        g_out[...] = g
        m_out[...] = mn
        v_out[...] = vn
        d_out[...] = -ADAM_LR * ((mn / c1) / (jnp.sqrt(vn / c2) + ADAM_EPS) + ADAM_WD * w_ref[...])

    slab = pl.BlockSpec((None, tr, b), lambda i: (layer, i, 0))
    whole = pl.BlockSpec(memory_space=pl.ANY)
    return pl.pallas_call(
        body, name=name, grid=(a // tr,),
        in_specs=[pl.BlockSpec((N_DEV, tr, b), lambda i: (0, i, 0)), slab, slab, slab] + [whole] * 4,
        out_specs=[slab] * 4, out_shape=[jax.ShapeDtypeStruct(w.shape, F32)] * 4,
        input_output_aliases={4: 0, 5: 1, 6: 2, 7: 3},
        compiler_params=_params(1))(parts, w, m, v, *prev)


_COL = ("w_in", "w_pool_o", "w_mem_o", "w_ff1")
_DW_SHARDED = ("w_in", "w_ff1")
_BIG =("w_in", "w_ret_o", "w_pool_o", "w_mem_kv", "w_mem_o", "w_out", "w_ff1", "w_ff2")
_SMALL = ("ret_decay_logit", "w_pool_grp", "pool_scale", "norm1_g", "norm2_g", "mem_norm_g", "final_norm_g")
_SMALL_MM = ("w_pool_grp",)
_SMALL_F32 = tuple(n for n in _SMALL if n not in _SMALL_MM)
_WEIGHTS = ("w_in", "ret_decay_logit", "w_ret_o", "w_pool_grp", "pool_scale", "w_pool_o", "w_mem_kv", "w_mem_o",
            "w_out", "w_ff1", "w_ff2", "norm1_g", "norm2_g", "mem_norm_g", "final_norm_g")


def _small_rows(size, d):
    return -(-size // (8 * d)) * 8


def _pack_small(ws, d, names):
    parts = []
    for n in names:
        flat = ws[n].reshape(-1)
        rows = _small_rows(flat.shape[0], d)
        parts.append(jnp.pad(flat, (0, rows * d - flat.shape[0])).reshape(rows, d))
    return jnp.concatenate(parts, axis=0)[None]


def _unpack_small(packed, like, d, names):
    out, off = {}, 0
    for n in names:
        rows = _small_rows(like[n].size, d)
        out[n] = packed[0, off:off + rows].reshape(-1)[:like[n].size].reshape(like[n].shape)
        off += rows
    return out


def kernel(x, mem, w_in, ret_decay_logit, w_ret_o, w_pool_grp, pool_scale, w_pool_o, w_mem_kv, w_mem_o, w_out, w_ff1, w_ff2, norm1_g, norm2_g, mem_norm_g, final_norm_g, loss_target, m_w_in, m_ret_decay_logit, m_w_ret_o, m_w_pool_grp, m_pool_scale, m_w_pool_o, m_w_mem_kv, m_w_mem_o, m_w_out, m_w_ff1, m_w_ff2, m_norm1_g, m_norm2_g, m_mem_norm_g, m_final_norm_g, v_w_in, v_ret_decay_logit, v_w_ret_o, v_w_pool_grp, v_pool_scale, v_w_pool_o, v_w_mem_kv, v_w_mem_o, v_w_out, v_w_ff1, v_w_ff2, v_norm1_g, v_norm2_g, v_mem_norm_g, v_final_norm_g):
    w = dict(w_in=w_in, ret_decay_logit=ret_decay_logit, w_ret_o=w_ret_o, w_pool_grp=w_pool_grp,
             pool_scale=pool_scale, w_pool_o=w_pool_o, w_mem_kv=w_mem_kv, w_mem_o=w_mem_o, w_out=w_out,
             w_ff1=w_ff1, w_ff2=w_ff2, norm1_g=norm1_g, norm2_g=norm2_g, mem_norm_g=mem_norm_g,
             final_norm_g=final_norm_g)
    mom = dict(w_in=m_w_in, ret_decay_logit=m_ret_decay_logit, w_ret_o=m_w_ret_o, w_pool_grp=m_w_pool_grp,
               pool_scale=m_pool_scale, w_pool_o=m_w_pool_o, w_mem_kv=m_w_mem_kv, w_mem_o=m_w_mem_o,
               w_out=m_w_out, w_ff1=m_w_ff1, w_ff2=m_w_ff2, norm1_g=m_norm1_g, norm2_g=m_norm2_g,
               mem_norm_g=m_mem_norm_g, final_norm_g=m_final_norm_g)
    vel = dict(w_in=v_w_in, ret_decay_logit=v_ret_decay_logit, w_ret_o=v_w_ret_o, w_pool_grp=v_w_pool_grp,
               pool_scale=v_pool_scale, w_pool_o=v_w_pool_o, w_mem_kv=v_w_mem_kv, w_mem_o=v_w_mem_o,
               w_out=v_w_out, w_ff1=v_w_ff1, w_ff2=v_w_ff2, norm1_g=v_norm1_g, norm2_g=v_norm2_g,
               mem_norm_g=v_mem_norm_g, final_norm_g=v_final_norm_g)

    bl, seq, d = x.shape
    mlen = mem.shape[1]
    depth = w_in.shape[0]
    t = bl * seq
    dk = d // 8

    def natural(n, g):
        if n in _DW_SHARDED:
            return _columns_side_by_side("relayout_" + n, g)
        if n in _COL:
            return jnp.transpose(g, (1, 0, 2)).reshape(g.shape[1], -1)
        return g.reshape(-1, g.shape[-1])

    def finish_gather(name, names, started, after):
        return {n: natural(n, g) for n, g in zip(names, _exchange_wait(name, False, started, after))}

    shards = [{n: w[n][l].astype(MM) for n in _BIG} for l in range(depth)]
    rest = _BIG[1:]
    (w_in0,) = _all_gather("gather_w_in", [shards[0][_BIG[0]]])
    full = [{_BIG[0]: natural(_BIG[0], w_in0)}]
    def start_layer(l, after):
        s_in = _exchange_start(f"gather_start_in{l}", False, [shards[l][_BIG[0]]], after=after)
        s_rest = _exchange_start(f"gather_start_{l}", False, [shards[l][n] for n in rest], after=[s_in[4]])
        return s_in, s_rest, (s_in[4], s_rest[4])

    pending = (None, _exchange_start("gather_start_0", False, [shards[0][n] for n in rest], after=[w_in0]))
    first_tokens = (pending[1][4],)
    pending_next = None
    if depth > 1:
        pending_next = start_layer(1, [pending[1][4]])
        first_tokens += pending_next[2]

    inv = ROPE_BASE ** (-jnp.arange(0, dk, 2, dtype=F32) / dk)
    ang = jnp.arange(seq, dtype=F32)[:, None] * inv[None, :]
    cos2 = jnp.concatenate([jnp.cos(ang), jnp.cos(ang)], axis=-1)
    sin2 = jnp.concatenate([-jnp.sin(ang), jnp.sin(ang)], axis=-1)
    log_g = jax.nn.log_sigmoid(ret_decay_logit)
    x2 = x.reshape(t, d)
    mem2 = mem.reshape(bl * mlen, d)
    gmem = mem_norm_g.reshape(1, d)

    def merge(a_r, y_p, o_a, g_r, g_p, g_m, w_r, w_p, w_m):
        f = lambda z: z.astype(F32)
        o_r, o_p, o_m = _dot(a_r, w_r), _dot(y_p, w_p), _dot(o_a, w_m)
        return _sigmoid(f(g_r)) * o_r + _sigmoid(f(g_p)) * o_p + _sigmoid(f(g_m)) * o_m, o_r, o_p, o_m

    def relu2(u):
        r = jnp.maximum(u.astype(MM), 0.0)
        return r * r

    def ident(a):
        return a

    saved = []
    xc = x2
    for l in range(depth):
        s = dict(x_in=xc)
        started_now = ()
        if l > 0:
            pending, pending_next = pending_next, None
            full.append(finish_gather(f"gather_wait_in{l}", _BIG[:1], pending[0], xc))
            if l + 1 < depth:
                pending_next = start_layer(l + 1, [full[l]["w_in"]])
                started_now = pending_next[2]
        fw = full[l]
        g1 = norm1_g[l].reshape(1, d)
        g2 = norm2_g[l].reshape(1, d)
        s["proj"], s["h1"] = _pmm("proj", _rms_prologue, [(xc, d, 0)], [g1], fw["w_in"],
                                  tm=2048, tn=1024, save_a=True, out_dtypes=(MM,),
                                  after=started_now if l > 0 else first_tokens)
        proj = s["proj"]
        s["o_raw"], s["a_ret"], s["sf"], s["sb"], s["qr"], s["kr"] = _ret_core_fwd(proj, cos2, sin2, log_g[l],
                                                                                   d, bl, seq)
        s["y"] = _pool_fwd(proj, w_pool_grp[l], pool_scale[l].reshape(1, -1), d, bl, seq)
        fw.update(finish_gather(f"gather_wait_{l}", rest, pending[1], s["a_ret"]))
        s["kv"], s["memn"] = _pmm("mem_kv", _rms_prologue, [(mem2, d, 0)], [gmem], fw["w_mem_kv"],
                                  tm=512, tn=512, save_a=True)
        s["o_att"] = _attn_fwd(proj, s["kv"], d, bl, seq, mlen)
        s["x_mid"], s["merged"], s["o_ret"], s["o_pool"], s["o_mem"] = _pmm(
            "merge_out", merge,
            [(s["a_ret"], d, 0), (s["y"], d // 2, 0), (s["o_att"], d // 2, 0), (proj, d, 4), (proj, d, 5), (proj, d, 6)],
            [fw["w_ret_o"], fw["w_pool_o"], fw["w_mem_o"]], fw["w_out"], tm=512, tn=1024, residual=xc, save_a=True,
            extra_outs=[(d, MM)] * 3)
        s["u"], s["h2"] = _pmm("ff1", _rms_prologue, [(s["x_mid"], d, 0)], [g2], fw["w_ff1"],
                               tm=512, tn=4096, save_a=True, out_dtypes=(MM,))
        (xc,) = _pmm("ff2", relu2, [(s["u"], s["u"].shape[1], 0)], [], fw["w_ff2"],
                     tm=512, tn=1024, residual=s["x_mid"])
        saved.append(s)

    dxc, g_final, loss_part = _loss_head(xc, loss_target.reshape(t, d), final_norm_g.reshape(1, d))
    loss = lax.psum(loss_part[0, 0], ("x", "y", "c"))

    small_names = ("w_pool_grp", "pool_scale", "norm1_g", "norm2_g", "ret_decay_logit")
    grads = {n: [None] * depth for n in small_names}
    group_a = ("w_ff1", "w_ff2")
    group_b = tuple(n for n in _BIG if n not in group_a)
    scatters = {}
    dmemn = jnp.zeros((bl * mlen, d), F32)

    def relu2_bwd(acc, u):
        return (acc * (2.0 * jnp.maximum(u.astype(F32), 0.0)),)

    def gates_bwd(acc, g_r, g_p, g_m, o_r, o_p, o_m, w_r, w_p, w_m):
        d_os, d_gs, backs = [], [], []
        for gz, oz, wz in ((g_r, o_r, w_r), (g_p, o_p, w_p), (g_m, o_m, w_m)):
            sg = _sigmoid(gz.astype(F32))
            d_o = (acc * sg).astype(MM)
            d_os.append(d_o)
            d_gs.append(acc * oz.astype(F32) * (sg * (1.0 - sg)))
            backs.append(_dot_nt(d_o, wz))
        return tuple(d_os + d_gs + backs)

    def to_send(n, g):
        a, b = w[n].shape[1:]
        if n in _DW_SHARDED:
            return g
        if n in _COL:
            return jnp.transpose(g.reshape(a, N_DEV, b), (1, 0, 2))
        return g.reshape(N_DEV, a, b)

    for l in reversed(range(depth)):
        s = saved[l]
        fw = full[l]
        proj = s["proj"]
        g1 = norm1_g[l].reshape(1, d)
        g2 = norm2_g[l].reshape(1, d)
        dw = {}
        (du,) = _pmm("ff2_bwd", ident, [(dxc, d, 0)], [], fw["w_ff2"], w_mode="nt", tm=512, tn=4096,
                     epilogue=relu2_bwd, epi_ins=[(s["u"], 0)], out_dtypes=(MM,))
        dw["w_ff2"] = _tnmm("dw_ff2", s["u"], dxc, a_fn=relu2)
        dw["w_ff1"] = _tnmm("dw_ff1", s["h2"], du, col_shards=True)
        scatters[l, "a"] = _exchange_start(f"scatter_start_a{l}", True, [to_send(n, dw[n]) for n in group_a])
        dmid, grads["norm2_g"][l] = _mm_rms_bwd("ff1_norm2_bwd", du, fw["w_ff1"], s["x_mid"], g2, dxc, tm=512)
        d_oret, d_opool, d_omem, dgr, dgp, dgm, da_ret, dy, do_att = _pmm(
            "out_bwd", ident, [(dmid, d, 0)], [], fw["w_out"], w_mode="nt", tm=256, tn=d, epilogue=gates_bwd,
            epi_ins=[(proj, 4 * d), (proj, 5 * d), (proj, 6 * d), (s["o_ret"], 0), (s["o_pool"], 0), (s["o_mem"], 0)],
            epi_full=[fw["w_ret_o"], fw["w_pool_o"], fw["w_mem_o"]], out_dtypes=(MM,) * 6 + (F32,) * 3,
            out_widths=[d] * 7 + [d // 2] * 2, after=(scatters[l, "a"][4],))
        dw["w_out"] = _tnmm("dw_out", s["merged"], dmid)
        dw["w_ret_o"] = _tnmm("dw_ret_o", s["a_ret"], d_oret)
        dw["w_pool_o"] = _tnmm("dw_pool_o", s["y"], d_opool)
        dw["w_mem_o"] = _tnmm("dw_mem_o", s["o_att"], d_omem)
        dg_ret, do_ret = _ret_post_bwd(da_ret, proj, s["o_raw"], d)
        dq, dkk, dvv, dlf, dlb = _ret_core_bwd(s["qr"], s["kr"], proj, do_ret, s["sf"], s["sb"], cos2, sin2,
                                               log_g[l], d, bl, seq)
        dl = jnp.stack([dlf[:, 0, 0].reshape(bl, HEADS).sum(0), dlb[:, 0, 0].reshape(bl, HEADS).sum(0)])
        grads["ret_decay_logit"][l] = dl * jax.nn.sigmoid(-ret_decay_logit[l])
        dp, grads["w_pool_grp"][l], dscale = _pool_bwd(proj, dy, w_pool_grp[l], pool_scale[l].reshape(1, -1),
                                                       d, bl, seq)
        grads["pool_scale"][l] = dscale.reshape(-1)
        dqm, dmk, dmv = _attn_bwd(proj, s["kv"], do_att, d, bl, seq, mlen)
        dkv = jnp.concatenate([dmk, dmv], axis=-1).astype(MM)
        dw["w_mem_kv"] = _tnmm("dw_mem_kv", s["memn"], dkv)
        (dmemn,) = _pmm("mem_kv_bwd", None, [(dkv, d, 0)], [], fw["w_mem_kv"], w_mode="nt", tm=512, tn=512,
                        residual=dmemn)
        dproj = [dq, dkk, dvv, dg_ret, dp, dqm, dgr, dgp, dgm]
        dw["w_in"] = _tnmm("dw_in", s["h1"], dproj, col_shards=True, tm=512, tk=512)
        scatters[l, "b"] = _exchange_start(f"scatter_start_b{l}", True, [to_send(n, dw[n]) for n in group_b])
        dxc, grads["norm1_g"][l] = _mm_rms_bwd("proj_norm1_bwd", dproj, fw["w_in"], s["x_in"], g1, dmid, tm=256,
                                               after=(scatters[l, "b"][4],))

    _, g_memn = _rms_bwd("mem_norm_bwd", dmemn, mem2, gmem, None)
    grad_x = dxc.reshape(bl, seq, d)

    small_g = dict(ret_decay_logit=jnp.stack(grads["ret_decay_logit"]), w_pool_grp=jnp.stack(grads["w_pool_grp"]),
                   pool_scale=jnp.stack(grads["pool_scale"]),
                   norm1_g=jnp.concatenate(grads["norm1_g"], axis=0), norm2_g=jnp.concatenate(grads["norm2_g"], axis=0),
                   mem_norm_g=g_memn.reshape(-1), final_norm_g=g_final.reshape(-1))
    small_started = _exchange_start("gather_small_start", False,
                                    [_pack_small(small_g, d, _SMALL_MM)[0].astype(MM),
                                     _pack_small(small_g, d, _SMALL_F32)[0]])

    big = {n: [lax.empty(w[n].shape, F32) for _ in range(4)] for n in _BIG}

    def update(l, grp, names, after):
        recv = _exchange_wait(f"scatter_wait_{grp}{l}", True, scatters[l, grp], after)
        for n, parts in zip(names, recv):
            big[n] = _adamw("adamw_" + n, parts, w[n], mom[n], vel[n], big[n], l)
        return big[names[-1]][0]

    after = dxc
    for l in reversed(range(1, depth)):
        for grp, names in (("a", group_a), ("b", group_b)):
            after = update(l, grp, names, after)
    after = update(0, "a", group_a, after)
    small_lands = _exchange_wait("gather_small_wait", False, small_started, after)
    small = [{} for _ in range(4)]
    for names, parts in zip((_SMALL_MM, _SMALL_F32), small_lands):
        w_small = _pack_small(w, d, names)
        res = _adamw("adamw_small", parts, w_small, _pack_small(mom, d, names), _pack_small(vel, d, names),
                     [lax.empty(w_small.shape, F32) for _ in range(4)], 0)
        after = res[0]
        for k in range(4):
            small[k].update(_unpack_small(res[k], w, d, names))
    update(0, "b", group_b, after)

    outs = [loss, grad_x]
    for k in range(4):
        outs += [big[n][k] if n in _BIG else small[k][n] for n in _WEIGHTS]
    return tuple(outs)
```

```python
import jax
import jax.numpy as jnp
from jax import lax
from jax.experimental import pallas as pl
from jax.experimental.pallas import tpu as pltpu

F32 = jnp.float32
MM = jnp.bfloat16
N_DEV = 8
HEADS = 4
POOL_WINDOWS = (2, 4, 8, 16)
RET_CHUNK = 256
EPS = 1e-6
ROPE_BASE = 10000.0
ADAM_LR, ADAM_B1, ADAM_B2, ADAM_EPS, ADAM_WD, ADAM_STEP = 0.001, 0.9, 0.999, 1e-08, 0.01, 10
V7X_VMEM_LIMIT = 56 * 1024 * 1024
MESH = pl.DeviceIdType.MESH


def _params(n_axes):
    return pltpu.CompilerParams(dimension_semantics=("arbitrary",) * n_axes,
                                vmem_limit_bytes=V7X_VMEM_LIMIT)


def _tile(n, pref, align=128):
    cands = [c for c in range(align, min(pref, n) + 1, align) if n % c == 0]
    return max(cands) if cands else n


def _sigmoid(z):
    return 0.5 * jnp.tanh(0.5 * z) + 0.5


def _dot(a, b):
    return jnp.dot(a, b, preferred_element_type=F32)


def _dot_nt(a, b):
    return lax.dot_general(a, b, (((1,), (1,)), ((), ())), preferred_element_type=F32)


def _dot_tn(a, b):
    return lax.dot_general(a, b, (((0,), (0,)), ((), ())), preferred_element_type=F32)


def _pmm(name, prologue, row_ins, vec_ins, w, *, tm, tn, w_mode="nn", residual=None, save_a=False,
         epilogue=None, epi_ins=(), out_dtypes=(F32,), after=(), extra_outs=(), epi_full=(), out_widths=None):
    m = row_ins[0][0].shape[0]
    if w_mode == "nn":
        k, n = w.shape
        tn = _tile(n, tn)
        w_spec = pl.BlockSpec((k, tn), lambda i, j: (0, j))
    else:
        n, k = w.shape
        tn = _tile(n, tn)
        w_spec = pl.BlockSpec((tn, k), lambda i, j: (j, 0))
    tm = _tile(m, tm, 8)
    n_row, n_vec, n_epi, n_out = len(row_ins), len(vec_ins), len(epi_ins), len(out_dtypes)
    has_res = residual is not None
    use_scr = prologue is not None
    out_widths = [n] * n_out if out_widths is None else list(out_widths)
    assert all(wd == n for wd in out_widths) or tn == n

    def body(*refs):
        row_refs = refs[:n_row]
        p = n_row
        vec_refs = refs[p:p + n_vec]
        p += n_vec
        w_ref = refs[p]
        p += 1
        res_ref = refs[p] if has_res else None
        p += int(has_res)
        epi_refs = refs[p:p + n_epi + len(epi_full)]
        p += n_epi + len(epi_full) + len(after)
        out_refs = refs[p:p + n_out]
        p += n_out
        a_out = refs[p] if save_a else None
        p += int(save_a)
        extra_refs = refs[p:p + len(extra_outs)]
        p += len(extra_outs)
        if use_scr:
            a_src = refs[p]

            @pl.when(pl.program_id(1) == 0)
            def _():
                made = prologue(*[r[...] for r in row_refs], *[v[...] for v in vec_refs])
                made = made if isinstance(made, tuple) else (made,)
                a = made[0].astype(MM)
                a_src[...] = a
                if save_a:
                    a_out[...] = a
                for e_ref, e in zip(extra_refs, made[1:]):
                    e_ref[...] = e.astype(e_ref.dtype)
        else:
            a_src = row_refs[0]
        if w_mode == "nt":
            acc = _dot_nt(a_src[...], w_ref[...])
        else:
            acc = _dot(a_src[...], w_ref[...])
        if has_res:
            acc = acc + res_ref[...]
        outs = epilogue(acc, *[e[...] for e in epi_refs]) if epilogue is not None else (acc,)
        for o_ref, o in zip(out_refs, outs):
            o_ref[...] = o.astype(o_ref.dtype)

    in_specs = [pl.BlockSpec((tm, wd), lambda i, j, cb=cb: (i, cb)) for (_, wd, cb) in row_ins]
    in_specs += [pl.BlockSpec(v.shape, lambda i, j: (0, 0)) for v in vec_ins]
    in_specs += [w_spec]
    args = [r[0] for r in row_ins] + list(vec_ins) + [w]
    if has_res:
        in_specs.append(pl.BlockSpec((tm, tn), lambda i, j: (i, j)))
        args.append(residual)
    for (arr, off) in epi_ins:
        assert off % tn == 0
        in_specs.append(pl.BlockSpec((tm, tn), lambda i, j, ob=off // tn: (i, ob + j)))
        args.append(arr)
    in_specs += [pl.BlockSpec(v.shape, lambda i, j: (0, 0)) for v in epi_full]
    args += list(epi_full)
    n_after = len(after)
    in_specs += [pl.BlockSpec(memory_space=pl.ANY)] * n_after
    args += list(after)
    out_specs = [pl.BlockSpec((tm, tn if wd == n else wd), lambda i, j: (i, j)) for wd in out_widths]
    out_shape = [jax.ShapeDtypeStruct((m, wd), dt) for wd, dt in zip(out_widths, out_dtypes)]
    if save_a:
        out_specs.append(pl.BlockSpec((tm, k), lambda i, j: (i, 0)))
        out_shape.append(jax.ShapeDtypeStruct((m, k), MM))
    for wd, dt in extra_outs:
        out_specs.append(pl.BlockSpec((tm, wd), lambda i, j: (i, 0)))
        out_shape.append(jax.ShapeDtypeStruct((m, wd), dt))
    scratch = [pltpu.VMEM((tm, k), MM)] if use_scr else []
    return pl.pallas_call(body, name=name, grid=(m // tm, n // tn), in_specs=in_specs,
                          out_specs=out_specs, out_shape=out_shape, scratch_shapes=scratch,
                          compiler_params=_params(2))(*args)


def _tnmm(name, a, b, *, tm=1024, tn=1024, tk=1024, col_shards=False, a_fn=None):
    t, m = a.shape
    pieces = list(b) if isinstance(b, (list, tuple)) else [b]
    widths = [p.shape[1] for p in pieces]
    offs = [sum(widths[:p]) for p in range(len(pieces))]
    n = sum(widths)
    tm, tk = _tile(m, tm), _tile(t, tk, 8)
    per_tile = 1
    if col_shards:
        wb = n // N_DEV
        if len(pieces) > 1:
            tn = n
        while 2 * per_tile * wb <= tn and 2 * per_tile <= N_DEV:
            per_tile *= 2
        tn = per_tile * wb
        out_spec = pl.BlockSpec((per_tile, tm, wb), lambda i, j, kk: (j, i, 0))
        out_shape = jax.ShapeDtypeStruct((N_DEV, m, wb), MM)
    else:
        tn = _tile(n, tn)
        out_spec = pl.BlockSpec((tm, tn), lambda i, j, kk: (i, j))
        out_shape = jax.ShapeDtypeStruct((m, n), MM)
    nk = t // tk

    assert len(pieces) == 1 or tn == n

    def body(a_ref, *rest):
        b_refs, (o_ref, acc) = rest[:len(pieces)], rest[len(pieces):]
        kk = pl.program_id(2)

        @pl.when(kk == 0)
        def _():
            acc[...] = jnp.zeros_like(acc)

        av = (a_ref[...] if a_fn is None else a_fn(a_ref[...])).astype(MM)
        if len(pieces) == 1:
            acc[...] += _dot_tn(av, b_refs[0][...].astype(MM))
        else:
            for b_ref, off, wd in zip(b_refs, offs, widths):
                acc[:, off:off + wd] += _dot_tn(av, b_ref[...].astype(MM))

        @pl.when(kk == nk - 1)
        def _():
            if col_shards:
                for sh in range(per_tile):
                    o_ref[sh] = acc[:, sh * wb:(sh + 1) * wb].astype(o_ref.dtype)
            else:
                o_ref[...] = acc[...].astype(o_ref.dtype)

    return pl.pallas_call(
        body, name=name, grid=(m // tm, n // tn, nk),
        in_specs=[pl.BlockSpec((tk, tm), lambda i, j, kk: (kk, i))]
        + [pl.BlockSpec((tk, tn if len(pieces) == 1 else wd), lambda i, j, kk: (kk, j)) for wd in widths],
        out_specs=out_spec, out_shape=out_shape,
        scratch_shapes=[pltpu.VMEM((tm, tn), F32)],
        compiler_params=_params(3))(a, *pieces)


def _rms_prologue(x, g):
    r = lax.rsqrt(jnp.mean(x * x, axis=-1, keepdims=True) + EPS)
    return x * r * g


def _rms_bwd_rows(dh, x, g):
    d = x.shape[-1]
    r = lax.rsqrt(jnp.mean(x * x, axis=-1, keepdims=True) + EPS)
    xh = x * r
    dxh = dh * g
    dx = r * (dxh - xh * (jnp.sum(dxh * xh, axis=-1, keepdims=True) / d))
    dg = jnp.sum(dh * xh, axis=0, keepdims=True)
    return dx, dg


def _rms_bwd(name, dh, x, g, dres, *, tm=512):
    m, d = x.shape
    tm = min(tm, m)
    has_res = dres is not None

    def body(*refs):
        if has_res:
            dh_ref, x_ref, g_ref, r_ref, dx_ref, dg_ref = refs
        else:
            dh_ref, x_ref, g_ref, dx_ref, dg_ref = refs
        dx, dg = _rms_bwd_rows(dh_ref[...], x_ref[...], g_ref[...])
        if has_res:
            dx = dx + r_ref[...]
        dx_ref[...] = dx

        @pl.when(pl.program_id(0) == 0)
        def _():
            dg_ref[...] = jnp.zeros_like(dg_ref)

        dg_ref[...] += dg

    row = pl.BlockSpec((tm, d), lambda i: (i, 0))
    vec = pl.BlockSpec((1, d), lambda i: (0, 0))
    in_specs = [row, row, vec] + ([row] if has_res else [])
    args = [dh, x, g] + ([dres] if has_res else [])
    return pl.pallas_call(body, name=name, grid=(m // tm,), in_specs=in_specs, out_specs=[row, vec],
                          out_shape=[jax.ShapeDtypeStruct((m, d), F32), jax.ShapeDtypeStruct((1, d), F32)],
                          compiler_params=_params(1))(*args)


def _mm_rms_bwd(name, a, w, x, g, dres, *, tm, after=()):
    pieces = list(a) if isinstance(a, (list, tuple)) else [a]
    widths = [p.shape[1] for p in pieces]
    m = pieces[0].shape[0]
    d = w.shape[0]
    tm = _tile(m, tm, 8)
    n_a = len(pieces)

    def body(*refs):
        a_refs = refs[:n_a]
        w_ref, x_ref, g_ref, r_ref = refs[n_a:n_a + 4]
        dx_ref, dg_ref = refs[n_a + 4 + len(after):]

        av = a_refs[0][...] if n_a == 1 else jnp.concatenate([a_ref[...] for a_ref in a_refs], axis=1)
        dh = _dot_nt(av, w_ref[...])
        dx, dg = _rms_bwd_rows(dh, x_ref[...], g_ref[...])
        dx_ref[...] = dx + r_ref[...]

        @pl.when(pl.program_id(0) == 0)
        def _():
            dg_ref[...] = jnp.zeros_like(dg_ref)

        dg_ref[...] += dg

    row = pl.BlockSpec((tm, d), lambda i: (i, 0))
    vec = pl.BlockSpec((1, d), lambda i: (0, 0))
    return pl.pallas_call(
        body, name=name, grid=(m // tm,),
        in_specs=[pl.BlockSpec((tm, wd), lambda i: (i, 0)) for wd in widths]
        + [pl.BlockSpec(w.shape, lambda i: (0, 0)), row, vec, row]
        + [pl.BlockSpec(memory_space=pl.ANY)] * len(after),
        out_specs=[row, vec],
        out_shape=[jax.ShapeDtypeStruct((m, d), F32), jax.ShapeDtypeStruct((1, d), F32)],
        compiler_params=_params(1))(*pieces, w, x, g, dres, *after)


def _loss_head(x, target, g, *, tm=256):
    m, d = x.shape
    tm = min(tm, m)

    def body(x_ref, t_ref, g_ref, dx_ref, dg_ref, loss_ref):
        xv, gv = x_ref[...], g_ref[...]
        y = _rms_prologue(xv, gv)
        err = y - t_ref[...]
        part = 0.5 * jnp.sum(jnp.sum(err * err, axis=-1, keepdims=True) / d)
        dx, dg = _rms_bwd_rows(err / d, xv, gv)
        dx_ref[...] = dx

        @pl.when(pl.program_id(0) == 0)
        def _():
            dg_ref[...] = jnp.zeros_like(dg_ref)
            loss_ref[...] = jnp.zeros_like(loss_ref)

        dg_ref[...] += dg
        loss_ref[...] += jnp.full(loss_ref.shape, part, F32)

    row = pl.BlockSpec((tm, d), lambda i: (i, 0))
    vec = pl.BlockSpec((1, d), lambda i: (0, 0))
    lspec = pl.BlockSpec((1, 128), lambda i: (0, 0))
    return pl.pallas_call(body, name="loss_head", grid=(m // tm,), in_specs=[row, row, vec],
                          out_specs=[row, vec, lspec],
                          out_shape=[jax.ShapeDtypeStruct((m, d), F32), jax.ShapeDtypeStruct((1, d), F32),
                                     jax.ShapeDtypeStruct((1, 128), F32)],
                          compiler_params=_params(1))(x, target, g)


def _rot(xv, cos2, sin2, half):
    return xv * cos2 + pltpu.roll(xv, half, 1) * sin2


def _rot_t(dv, cos2, sin2, half):
    return dv * cos2 + pltpu.roll(dv * sin2, half, 1)


def _ret_consts(lg_ref, h, t, dk):
    lf, lb = lg_ref[0, h], lg_ref[1, h]
    ab = (lax.broadcasted_iota(jnp.int32, (t, t), 0) - lax.broadcasted_iota(jnp.int32, (t, t), 1)).astype(F32)
    dmat = jnp.exp(jnp.where(ab >= 0, lf * ab, -lb * ab))
    up = lax.broadcasted_iota(jnp.int32, (t, dk), 0).astype(F32) + 1.0
    down = float(t) - up
    one = jnp.ones((1, 1), F32)
    return dict(ab=ab, dmat=dmat, xi_f=jnp.exp(lf * up), zeta_f=jnp.exp(lf * down), xi_b=jnp.exp(lb * up),
                zeta_b=jnp.exp(lb * down), up=up[:, 0:1], down=down[:, 0:1],
                cf=jnp.exp(one * (lf * t)), cb=jnp.exp(one * (lb * t)))


def _scaled(xv, rows):
    return (xv.astype(F32) * rows).astype(MM)


def _ret_core_fwd(proj, cos2, sin2, lg, d, bl, seq, *, tc=RET_CHUNK):
    t = proj.shape[0]
    dk, dv = d // 8, d // 4
    tc = min(tc, seq)
    nc = seq // tc
    scale = float(dk) ** -0.5

    def body(lg_ref, qp_ref, kp_ref, v_ref, g_ref, c_ref, s_ref, o_ref, a_ref, sf_ref, sb_ref, q_ref, k_ref, o_acc):
        c = _ret_consts(lg_ref, pl.program_id(1), tc, dk)

        def rows_of(i):
            return pl.ds(pl.multiple_of(i * tc, tc), tc)

        def rotate(i, carry):
            rows = rows_of(i)
            cs, sn = c_ref[rows, :], s_ref[rows, :]
            q_ref[rows, :] = _rot(qp_ref[rows, :].astype(F32), cs, sn, dk // 2).astype(MM)
            k_ref[rows, :] = (_rot(kp_ref[rows, :].astype(F32), cs, sn, dk // 2) * scale).astype(MM)
            return carry

        lax.fori_loop(0, nc, rotate, 0)

        def fwd_step(i, sf):
            rows = rows_of(i)
            sf_ref[i] = sf
            q, kk, v = q_ref[rows, :], k_ref[rows, :], v_ref[rows, :]
            p = (_dot_nt(q, kk) * c["dmat"]).astype(MM)
            o_acc[rows, :] = _dot(p, v) + _dot(_scaled(q, c["xi_f"]), sf.astype(MM))
            return sf * c["cf"] + _dot_tn(_scaled(kk, c["zeta_f"]), v)

        lax.fori_loop(0, nc, fwd_step, jnp.zeros((dk, dv), F32))

        def bwd_step(ii, sb):
            rows = rows_of(nc - 1 - ii)
            sb_ref[nc - 1 - ii] = sb
            q, kk, v = q_ref[rows, :], k_ref[rows, :], v_ref[rows, :]
            o_acc[rows, :] += _dot(_scaled(q, c["zeta_b"]), sb.astype(MM))
            return sb * c["cb"] + _dot_tn(_scaled(kk, c["xi_b"]), v)

        lax.fori_loop(0, nc, bwd_step, jnp.zeros((dk, dv), F32))

        def post(i, carry):
            rows = rows_of(i)
            o = o_acc[rows, :]
            o_ref[rows, :] = o.astype(o_ref.dtype)
            oc = o - jnp.mean(o, axis=-1, keepdims=True)
            on = oc * lax.rsqrt(jnp.mean(oc * oc, axis=-1, keepdims=True) + EPS)
            g = g_ref[rows, :].astype(F32)
            a_ref[rows, :] = (on * (g * _sigmoid(g))).astype(MM)
            return carry

        lax.fori_loop(0, nc, post, 0)

    qk = pl.BlockSpec((seq, dk), lambda b, h: (b, h))
    vv = pl.BlockSpec((seq, dv), lambda b, h: (b, h))
    tab = pl.BlockSpec((seq, dk), lambda b, h: (0, 0))
    states = pl.BlockSpec((None, nc, dk, dv), lambda b, h: (b * HEADS + h, 0, 0, 0))
    return pl.pallas_call(
        body, name="ret_core_fwd", grid=(bl, HEADS),
        in_specs=[pl.BlockSpec(memory_space=pltpu.SMEM), qk, pl.BlockSpec((seq, dk), lambda b, h: (b, HEADS + h)),
                  pl.BlockSpec((seq, dv), lambda b, h: (b, HEADS + h)),
                  pl.BlockSpec((seq, dv), lambda b, h: (b, 2 * HEADS + h)), tab, tab],
        out_specs=[vv, vv, states, states, qk, qk],
        out_shape=[jax.ShapeDtypeStruct((t, d), MM), jax.ShapeDtypeStruct((t, d), MM),
                   jax.ShapeDtypeStruct((bl * HEADS, nc, dk, dv), F32),
                   jax.ShapeDtypeStruct((bl * HEADS, nc, dk, dv), F32),
                   jax.ShapeDtypeStruct((t, d // 2), MM), jax.ShapeDtypeStruct((t, d // 2), MM)],
        scratch_shapes=[pltpu.VMEM((seq, dv), F32)],
        compiler_params=_params(2))(lg, proj, proj, proj, proj, cos2, sin2)


def _ret_post_bwd(da, proj, o_raw, d, *, ts=2048):
    t = da.shape[0]
    dv = d // 4
    ts = min(ts, t)

    def body(da_ref, g_ref, o_ref, dg_ref, do_ref):
        o, g, dav = o_ref[...].astype(F32), g_ref[...].astype(F32), da_ref[...].astype(F32)
        mu = jnp.mean(o, axis=-1, keepdims=True)
        oc = o - mu
        r = lax.rsqrt(jnp.mean(oc * oc, axis=-1, keepdims=True) + EPS)
        on = oc * r
        sg = _sigmoid(g)
        don = dav * (g * sg)
        dg_ref[...] = (dav * on * (sg * (1.0 + g * (1.0 - sg)))).astype(MM)
        do = r * (don - jnp.mean(don, axis=-1, keepdims=True) - on * jnp.mean(don * on, axis=-1, keepdims=True))
        do_ref[...] = do.astype(MM)

    blk = pl.BlockSpec((ts, dv), lambda i, h: (i, h))
    return pl.pallas_call(
        body, name="ret_post_bwd", grid=(t // ts, HEADS),
        in_specs=[blk, pl.BlockSpec((ts, dv), lambda i, h: (i, 2 * HEADS + h)), blk],
        out_specs=[blk, blk],
        out_shape=[jax.ShapeDtypeStruct((t, d), MM), jax.ShapeDtypeStruct((t, d), MM)],
        compiler_params=_params(2))(da, proj, o_raw)


def _ret_core_bwd(qr, kr, proj, do, sf_in, sb_in, cos2, sin2, lg, d, bl, seq, *, tc=RET_CHUNK):
    t = qr.shape[0]
    dk, dv = d // 8, d // 4
    tc = min(tc, seq)
    nc = seq // tc
    scale = float(dk) ** -0.5

    def body(lg_ref, q_ref, k_ref, v_ref, do_ref, sf_all, sb_all, c_ref, s_ref, dq_ref, dk_ref, dv_ref,
             dlf_ref, dlb_ref, dq_acc, dk_acc, dv_acc):
        c = _ret_consts(lg_ref, pl.program_id(1), tc, dk)
        fwd = c["ab"] >= 0
        zero_state = jnp.zeros((dk, dv), F32)
        zero = jnp.zeros((1, 1), F32)

        def rows_of(i):
            return pl.ds(pl.multiple_of(i * tc, tc), tc)

        def total(xv):
            return jnp.sum(xv, keepdims=True)

        def fwd_sweep(i, carry):
            hh, dlf, dlb = carry
            rows = rows_of(i)
            q, kk, v, dov = q_ref[rows, :], k_ref[rows, :], v_ref[rows, :], do_ref[rows, :]
            dof, vf = dov.astype(F32), v.astype(F32)
            p = _dot_nt(q, kk) * c["dmat"]
            da = _dot_nt(dov, v)
            x = p * da * c["ab"]
            dlf = dlf + total(jnp.where(fwd, x, 0.0))
            dlb = dlb - total(jnp.where(fwd, 0.0, x))
            pb, dpb = p.astype(MM), (da * c["dmat"]).astype(MM)
            dq = _dot(dpb, kk)
            dkc = _dot_tn(dpb, q)
            dvc = _dot_tn(pb, dov)
            sf, sb = sf_all[i], sb_all[i]
            sfb, sbb = sf.astype(MM), sb.astype(MM)
            q_xf, q_zb = _scaled(q, c["xi_f"]), _scaled(q, c["zeta_b"])
            dq = dq + _dot_nt(dov, sfb) * c["xi_f"] + _dot_nt(dov, sbb) * c["zeta_b"]
            dlf = dlf + total(jnp.sum(_dot(q_xf, sfb) * dof, axis=-1, keepdims=True) * c["up"])
            dlb = dlb + total(jnp.sum(_dot(q_zb, sbb) * dof, axis=-1, keepdims=True) * c["down"])
            hb = hh.astype(MM)
            dkc = dkc + _dot_nt(v, hb) * c["xi_b"]
            dv_bx = _dot(_scaled(kk, c["xi_b"]), hb)
            dlb = dlb + total(jnp.sum(vf * dv_bx, axis=-1, keepdims=True) * c["up"])
            dlb = dlb + float(tc) * total(hh * (sb * c["cb"]))
            dq_acc[rows, :] = dq
            dk_acc[rows, :] = dkc
            dv_acc[rows, :] = dvc + dv_bx
            return hh * c["cb"] + _dot_tn(q_zb, dov), dlf, dlb

        _, dlf, dlb = lax.fori_loop(0, nc, fwd_sweep, (zero_state, zero, zero))

        def rev_sweep(ii, carry):
            gg, dlf = carry
            i = nc - 1 - ii
            rows = rows_of(i)
            q, kk, v, dov = q_ref[rows, :], k_ref[rows, :], v_ref[rows, :], do_ref[rows, :]
            gb = gg.astype(MM)
            dk_acc[rows, :] += _dot_nt(v, gb) * c["zeta_f"]
            dv_fx = _dot(_scaled(kk, c["zeta_f"]), gb)
            dv_acc[rows, :] += dv_fx
            dlf = dlf + total(jnp.sum(v.astype(F32) * dv_fx, axis=-1, keepdims=True) * c["down"])
            dlf = dlf + float(tc) * total(gg * (sf_all[i] * c["cf"]))
            return gg * c["cf"] + _dot_tn(_scaled(q, c["xi_f"]), dov), dlf

        _, dlf = lax.fori_loop(0, nc, rev_sweep, (zero_state, dlf))

        cs, sn = c_ref[...], s_ref[...]
        dq_ref[...] = _rot_t(dq_acc[...], cs, sn, dk // 2).astype(MM)
        dk_ref[...] = (_rot_t(dk_acc[...], cs, sn, dk // 2) * scale).astype(MM)
        dv_ref[...] = dv_acc[...].astype(MM)
        dlf_ref[...] = jnp.broadcast_to(dlf, dlf_ref.shape)
        dlb_ref[...] = jnp.broadcast_to(dlb, dlb_ref.shape)

    qk = pl.BlockSpec((seq, dk), lambda b, h: (b, h))
    vv = pl.BlockSpec((seq, dv), lambda b, h: (b, h))
    tab = pl.BlockSpec((seq, dk), lambda b, h: (0, 0))
    dl = pl.BlockSpec((None, 8, 128), lambda b, h: (b * HEADS + h, 0, 0))
    states = pl.BlockSpec((None, nc, dk, dv), lambda b, h: (b * HEADS + h, 0, 0, 0))
    return pl.pallas_call(
        body, name="ret_core_bwd", grid=(bl, HEADS),
        in_specs=[pl.BlockSpec(memory_space=pltpu.SMEM), qk, qk, pl.BlockSpec((seq, dv), lambda b, h: (b, HEADS + h)),
                  vv, states, states, tab, tab],
        out_specs=[qk, qk, vv, dl, dl],
        out_shape=[jax.ShapeDtypeStruct((t, d // 2), MM), jax.ShapeDtypeStruct((t, d // 2), MM),
                   jax.ShapeDtypeStruct((t, d), MM),
                   jax.ShapeDtypeStruct((bl * HEADS, 8, 128), F32), jax.ShapeDtypeStruct((bl * HEADS, 8, 128), F32)],
        scratch_shapes=[pltpu.VMEM((seq, dk), F32), pltpu.VMEM((seq, dk), F32), pltpu.VMEM((seq, dv), F32)],
        compiler_params=_params(2))(lg, qr, kr, proj, do, sf_in, sb_in, cos2, sin2)


def _window_count(row, w, seq):
    return (jnp.minimum(row + w // 2, seq) - jnp.maximum(row - w // 2, 0)).astype(F32)


def _window_sum(pv, row, w, seq, sign):
    acc = None
    for j in range(-(w // 2), w // 2):
        if j == 0:
            term = pv
        else:
            src = row + sign * j
            term = jnp.where((src >= 0) & (src < seq), pltpu.roll(pv, (-sign * j) % seq, 0), 0.0)
        acc = term if acc is None else acc + term
    return acc


def _pool_fwd(proj, w_grp, scale, d, bl, seq):
    t = proj.shape[0]
    dg = d // 8

    def body(p_ref, w_ref, s_ref, y_ref):
        row = lax.broadcasted_iota(jnp.int32, (seq, dg), 0)
        for gi, w in enumerate(POOL_WINDOWS):
            sl = slice(gi * dg, (gi + 1) * dg)
            pg = p_ref[:, sl].astype(F32)
            mixed = _window_sum(pg, row, w, seq, 1) / _window_count(row, w, seq) - pg
            yp = _dot(mixed.astype(MM), w_ref[gi].astype(MM))
            y_ref[:, sl] = (yp * s_ref[:, sl]).astype(MM)

    return pl.pallas_call(
        body, name="pool_fwd", grid=(bl,),
        in_specs=[pl.BlockSpec((seq, d // 2), lambda b: (b, 6)),
                  pl.BlockSpec(w_grp.shape, lambda b: (0, 0, 0)),
                  pl.BlockSpec((1, d // 2), lambda b: (0, 0))],
        out_specs=pl.BlockSpec((seq, d // 2), lambda b: (b, 0)),
        out_shape=jax.ShapeDtypeStruct((t, d // 2), MM),
        compiler_params=_params(1))(proj, w_grp, scale)


def _pool_bwd(proj, dy, w_grp, scale, d, bl, seq):
    t = proj.shape[0]
    dg = d // 8

    def body(p_ref, dy_ref, w_ref, s_ref, dp_ref, dw_ref, ds_ref):
        @pl.when(pl.program_id(0) == 0)
        def _():
            dw_ref[...] = jnp.zeros_like(dw_ref)
            ds_ref[...] = jnp.zeros_like(ds_ref)

        row = lax.broadcasted_iota(jnp.int32, (seq, dg), 0)
        for gi, w in enumerate(POOL_WINDOWS):
            sl = slice(gi * dg, (gi + 1) * dg)
            pg = p_ref[:, sl].astype(F32)
            cnt = _window_count(row, w, seq)
            mixb = (_window_sum(pg, row, w, seq, 1) / cnt - pg).astype(MM)
            wgb = w_ref[gi].astype(MM)
            yp = _dot(mixb, wgb)
            dyg = dy_ref[:, sl]
            ds_ref[:, sl] += jnp.sum(dyg * yp, axis=0, keepdims=True)
            dyp = (dyg * s_ref[:, sl]).astype(MM)
            dmixed = _dot_nt(dyp, wgb)
            dw_ref[gi] += _dot_tn(mixb, dyp)
            dp_ref[:, sl] = (_window_sum(dmixed / cnt, row, w, seq, -1) - dmixed).astype(MM)

    half = pl.BlockSpec((seq, d // 2), lambda b: (b, 0))
    wspec = pl.BlockSpec(w_grp.shape, lambda b: (0, 0, 0))
    sspec = pl.BlockSpec((1, d // 2), lambda b: (0, 0))
    return pl.pallas_call(
        body, name="pool_bwd", grid=(bl,),
        in_specs=[pl.BlockSpec((seq, d // 2), lambda b: (b, 6)), half, wspec, sspec],
        out_specs=[half, wspec, sspec],
        out_shape=[jax.ShapeDtypeStruct((t, d // 2), MM), jax.ShapeDtypeStruct(w_grp.shape, F32),
                   jax.ShapeDtypeStruct((1, d // 2), F32)],
        compiler_params=_params(1))(proj, dy, w_grp, scale)


def _attn_probs(q, kk, dh):
    s = _dot_nt(q, kk) * (float(dh) ** -0.5)
    e = jnp.exp(s - jnp.max(s, axis=-1, keepdims=True))
    return e / jnp.sum(e, axis=-1, keepdims=True)


def _attn_fwd(proj, kv, d, bl, seq, mlen, *, tq=2048):
    t = proj.shape[0]
    dh = d // 8
    tq = min(tq, seq)
    nq = seq // tq

    def body(q_ref, k_ref, v_ref, o_ref):
        a = _attn_probs(q_ref[...].astype(MM), k_ref[...].astype(MM), dh)
        o_ref[...] = _dot(a.astype(MM), v_ref[...].astype(MM)).astype(MM)

    return pl.pallas_call(
        body, name="attn_fwd", grid=(bl, HEADS, nq),
        in_specs=[pl.BlockSpec((tq, dh), lambda b, h, i: (b * nq + i, 7 * HEADS + h)),
                  pl.BlockSpec((mlen, dh), lambda b, h, i: (b, h)),
                  pl.BlockSpec((mlen, dh), lambda b, h, i: (b, HEADS + h))],
        out_specs=pl.BlockSpec((tq, dh), lambda b, h, i: (b * nq + i, h)),
        out_shape=jax.ShapeDtypeStruct((t, d // 2), MM),
        compiler_params=_params(3))(proj, kv, kv)


def _attn_bwd(proj, kv, do, d, bl, seq, mlen, *, tq=2048):
    t = proj.shape[0]
    dh = d // 8
    tq = min(tq, seq)
    nq = seq // tq

    def body(q_ref, k_ref, v_ref, do_ref, dq_ref, dk_ref, dv_ref):
        @pl.when(pl.program_id(2) == 0)
        def _():
            dk_ref[...] = jnp.zeros_like(dk_ref)
            dv_ref[...] = jnp.zeros_like(dv_ref)

        q, kk, vv = q_ref[...].astype(MM), k_ref[...].astype(MM), v_ref[...].astype(MM)
        dov = do_ref[...].astype(MM)
        a = _attn_probs(q, kk, dh)
        dp = _dot_nt(dov, vv)
        ds = (a * (dp - jnp.sum(dp * a, axis=-1, keepdims=True)) * (float(dh) ** -0.5)).astype(MM)
        dq_ref[...] = _dot(ds, kk).astype(MM)
        dk_ref[...] += _dot_tn(ds, q)
        dv_ref[...] += _dot_tn(a.astype(MM), dov)

    qs = pl.BlockSpec((tq, dh), lambda b, h, i: (b * nq + i, h))
    ms = pl.BlockSpec((mlen, dh), lambda b, h, i: (b, h))
    return pl.pallas_call(
        body, name="attn_bwd", grid=(bl, HEADS, nq),
        in_specs=[pl.BlockSpec((tq, dh), lambda b, h, i: (b * nq + i, 7 * HEADS + h)), ms,
                  pl.BlockSpec((mlen, dh), lambda b, h, i: (b, HEADS + h)), qs],
        out_specs=[qs, ms, ms],
        out_shape=[jax.ShapeDtypeStruct((t, d // 2), MM), jax.ShapeDtypeStruct((bl * mlen, d // 2), F32),
                   jax.ShapeDtypeStruct((bl * mlen, d // 2), F32)],
        compiler_params=_params(3))(proj, kv, kv, do)


def _comm_call(name, body, arrays, out_shapes):
    n = len(arrays)
    hbm = pl.BlockSpec(memory_space=pl.ANY)
    return pl.pallas_call(
        body, name=name, out_shape=out_shapes, in_specs=[hbm] * n, out_specs=[hbm] * n,
        scratch_shapes=[pltpu.SemaphoreType.DMA((7 * n,)), pltpu.SemaphoreType.DMA((7 * n,)),
                        pltpu.SemaphoreType.DMA((n,))],
    )(*arrays)


def _all_gather(name, shards):
    n = len(shards)

    def body(*refs):
        x_refs, out_refs = refs[:n], refs[n:2 * n]
        send_sems, recv_sems, local_sems = refs[2 * n:]
        x, y, c = lax.axis_index("x"), lax.axis_index("y"), lax.axis_index("c")
        me, sibling = (x, y, c), (x, y, 1 - c)
        chips = [(1 - x, y), (x, 1 - y), (1 - x, 1 - y)]

        def copy(o, k, block, to, src=None):
            slot = out_refs[o].at[4 * block[0] + 2 * block[1] + block[2]]
            return pltpu.make_async_remote_copy(
                src_ref=slot if src is None else src, dst_ref=slot, send_sem=send_sems.at[7 * o + k],
                recv_sem=recv_sems.at[7 * o + k], device_id=to, device_id_type=MESH)

        locals_, remotes = [], []
        for o in range(n):
            mine = pltpu.make_async_copy(x_refs[o], out_refs[o].at[4 * x + 2 * y + c], local_sems.at[o])
            mine.start()
            locals_.append(mine)
            first = [copy(o, 0, me, sibling, src=x_refs[o])]
            first += [copy(o, 1 + j, me, (*chip, c), src=x_refs[o]) for j, chip in enumerate(chips)]
            for cp in first:
                cp.start()
            remotes += first
        for o in range(n):
            for j, chip in enumerate(chips):
                copy(o, 1 + j, (*chip, c), me).wait_recv()
                passed = copy(o, 4 + j, (*chip, c), sibling)
                passed.start()
                remotes.append(passed)
        for o in range(n):
            copy(o, 0, sibling, me).wait_recv()
            for j, chip in enumerate(chips):
                copy(o, 4 + j, (*chip, 1 - c), me).wait_recv()
        for cp in remotes:
            cp.wait_send()
        for mine in locals_:
            mine.wait()

    outs = [jax.ShapeDtypeStruct((N_DEV,) + s.shape, s.dtype) for s in shards]
    return _comm_call(name, body, shards, outs)


def _columns_side_by_side(name, g):
    _, k, wb = g.shape

    def body(x_ref, o_ref):
        o_ref[...] = x_ref[...]

    return pl.pallas_call(
        body, name=name, grid=(N_DEV,), in_specs=[pl.BlockSpec((None, k, wb), lambda j: (j, 0, 0))],
        out_specs=pl.BlockSpec((k, wb), lambda j: (0, j)),
        out_shape=jax.ShapeDtypeStruct((k, N_DEV * wb), g.dtype), compiler_params=_params(1))(g)


def _peer_of(k, x, y, c):
    peer = (1 - x if k & 4 else x, 1 - y if k & 2 else y, 1 - c if k & 1 else c)
    return peer, 4 * peer[0] + 2 * peer[1] + peer[2]


def _split_copies(scatter, srcs, lands, send_sems, recv_sems, arriving):
    x, y, c = lax.axis_index("x"), lax.axis_index("y"), lax.axis_index("c")
    me_idx = 4 * x + 2 * y + c
    copies = []
    for o, (src, land) in enumerate(zip(srcs, lands)):
        for k in range(1, N_DEV):
            peer, p_idx = _peer_of(k, x, y, c)
            mine = src.at[p_idx] if scatter else src
            sems = dict(send_sem=send_sems.at[7 * o + k - 1], recv_sem=recv_sems.at[7 * o + k - 1],
                        device_id=peer, device_id_type=MESH)
            slot = land.at[p_idx] if arriving else land.at[me_idx]
            copies.append(pltpu.make_async_remote_copy(src_ref=mine, dst_ref=slot, **sems))
    return copies


_HBM = pl.BlockSpec(memory_space=pltpu.HBM)
_SEM = pl.BlockSpec(memory_space=pltpu.SEMAPHORE)
_EFFECT = pltpu.SideEffectType.DATAFLOW_SIDE_EFFECTING


def _own_slot_copies(scatter, srcs, lands, local_sems):
    me_idx = 4 * lax.axis_index("x") + 2 * lax.axis_index("y") + lax.axis_index("c")
    return [pltpu.make_async_copy(src.at[me_idx] if scatter else src, land.at[me_idx], local_sems.at[o])
            for o, (src, land) in enumerate(zip(srcs, lands))]


def _exchange_start(name, scatter, arrays, after=()):
    n = len(arrays)
    lands = [lax.empty(a.shape if scatter else (N_DEV,) + a.shape, a.dtype) for a in arrays]

    def body(*refs):
        srcs, lnds = refs[:n], refs[n:2 * n]
        send_sems, recv_sems, local_sems = refs[2 * n + len(after):2 * n + len(after) + 3]
        token = refs[-1]
        for cp in _split_copies(scatter, srcs, lnds, send_sems, recv_sems, False):
            cp.start()
        for cp in _own_slot_copies(scatter, srcs, lnds, local_sems):
            cp.start()
        token[...] = jnp.zeros_like(token)

    hbm_in = [pltpu.with_memory_space_constraint(a, pltpu.HBM) for a in list(arrays) + lands]
    res = pl.pallas_call(
        body, name=name,
        out_shape=(pltpu.SemaphoreType.DMA((7 * n,)), pltpu.SemaphoreType.DMA((7 * n,)), pltpu.SemaphoreType.DMA((n,)),
                   *[pltpu.HBM(a.shape, a.dtype) for a in hbm_in], jax.ShapeDtypeStruct((8, 128), F32)),
        in_specs=[_HBM] * (2 * n) + [pl.BlockSpec(memory_space=pl.ANY)] * len(after),
        out_specs=(_SEM, _SEM, _SEM, *[_HBM] * (2 * n), pl.BlockSpec(memory_space=pltpu.VMEM)),
        input_output_aliases={i: 3 + i for i in range(2 * n)},
        compiler_params=pltpu.CompilerParams(has_side_effects=_EFFECT),
    )(*hbm_in, *after)
    return res[:3], None, list(res[3:3 + n]), list(res[3 + n:3 + 2 * n]), res[-1]


def _exchange_wait(name, scatter, started, after):
    sems, _, srcs, lands, _ = started
    n = len(srcs)

    def body(*refs):
        src_refs, lnd_refs = refs[:n], refs[n:2 * n]
        send_sems, recv_sems, local_sems = refs[2 * n:2 * n + 3]
        for cp in _split_copies(scatter, src_refs, lnd_refs, send_sems, recv_sems, False):
            cp.wait_send()
        for cp in _split_copies(scatter, src_refs, lnd_refs, send_sems, recv_sems, True):
            cp.wait_recv()
        for cp in _own_slot_copies(scatter, src_refs, lnd_refs, local_sems):
            cp.wait()

    res = pl.pallas_call(
        body, name=name, out_shape=tuple(pltpu.HBM(a.shape, a.dtype) for a in srcs + lands),
        in_specs=[_HBM] * (2 * n) + [_SEM, _SEM, _SEM, pl.BlockSpec(memory_space=pl.ANY)],
        out_specs=tuple([_HBM] * (2 * n)), input_output_aliases={i: i for i in range(2 * n)},
        compiler_params=pltpu.CompilerParams(has_side_effects=_EFFECT),
    )(*srcs, *lands, *sems, after)
    return list(res[n:])


def _adamw(name, parts, w, m, v, prev, layer, *, tr=256):
    _, a, b = w.shape
    tr = _tile(a, tr, 8)
    c1 = 1.0 - ADAM_B1 ** ADAM_STEP
    c2 = 1.0 - ADAM_B2 ** ADAM_STEP

    def body(p_ref, w_ref, m_ref, v_ref, _g, _d, _m, _v, g_out, d_out, m_out, v_out):
        g = p_ref[0].astype(F32)
        for s in range(1, N_DEV):
            g = g + p_ref[s].astype(F32)
        mn = ADAM_B1 * m_ref[...] + (1.0 - ADAM_B1) * g
        vn = ADAM_B2 * v_ref[...] + (1.0 - ADAM_B2) * (g * g)
        g_out[...] = g
        m_out[...] = mn
        v_out[...] = vn
        d_out[...] = -ADAM_LR * ((mn / c1) / (jnp.sqrt(vn / c2) + ADAM_EPS) + ADAM_WD * w_ref[...])

    slab = pl.BlockSpec((None, tr, b), lambda i: (layer, i, 0))
    whole = pl.BlockSpec(memory_space=pl.ANY)
    return pl.pallas_call(
        body, name=name, grid=(a // tr,),
        in_specs=[pl.BlockSpec((N_DEV, tr, b), lambda i: (0, i, 0)), slab, slab, slab] + [whole] * 4,
        out_specs=[slab] * 4, out_shape=[jax.ShapeDtypeStruct(w.shape, F32)] * 4,
        input_output_aliases={4: 0, 5: 1, 6: 2, 7: 3},
        compiler_params=_params(1))(parts, w, m, v, *prev)


_COL = ("w_in", "w_pool_o", "w_mem_o", "w_ff1")
_DW_SHARDED = ("w_in", "w_ff1")
_BIG =("w_in", "w_ret_o", "w_pool_o", "w_mem_kv", "w_mem_o", "w_out", "w_ff1", "w_ff2")
_SMALL = ("ret_decay_logit", "w_pool_grp", "pool_scale", "norm1_g", "norm2_g", "mem_norm_g", "final_norm_g")
_SMALL_MM = ("w_pool_grp",)
_SMALL_F32 = tuple(n for n in _SMALL if n not in _SMALL_MM)
_WEIGHTS = ("w_in", "ret_decay_logit", "w_ret_o", "w_pool_grp", "pool_scale", "w_pool_o", "w_mem_kv", "w_mem_o",
            "w_out", "w_ff1", "w_ff2", "norm1_g", "norm2_g", "mem_norm_g", "final_norm_g")


def _small_rows(size, d):
    return -(-size // (8 * d)) * 8


def _pack_small(ws, d, names):
    parts = []
    for n in names:
        flat = ws[n].reshape(-1)
        rows = _small_rows(flat.shape[0], d)
        parts.append(jnp.pad(flat, (0, rows * d - flat.shape[0])).reshape(rows, d))
    return jnp.concatenate(parts, axis=0)[None]


def _unpack_small(packed, like, d, names):
    out, off = {}, 0
    for n in names:
        rows = _small_rows(like[n].size, d)
        out[n] = packed[0, off:off + rows].reshape(-1)[:like[n].size].reshape(like[n].shape)
        off += rows
    return out


def kernel(x, mem, w_in, ret_decay_logit, w_ret_o, w_pool_grp, pool_scale, w_pool_o, w_mem_kv, w_mem_o, w_out, w_ff1, w_ff2, norm1_g, norm2_g, mem_norm_g, final_norm_g, loss_target, m_w_in, m_ret_decay_logit, m_w_ret_o, m_w_pool_grp, m_pool_scale, m_w_pool_o, m_w_mem_kv, m_w_mem_o, m_w_out, m_w_ff1, m_w_ff2, m_norm1_g, m_norm2_g, m_mem_norm_g, m_final_norm_g, v_w_in, v_ret_decay_logit, v_w_ret_o, v_w_pool_grp, v_pool_scale, v_w_pool_o, v_w_mem_kv, v_w_mem_o, v_w_out, v_w_ff1, v_w_ff2, v_norm1_g, v_norm2_g, v_mem_norm_g, v_final_norm_g):
    w = dict(w_in=w_in, ret_decay_logit=ret_decay_logit, w_ret_o=w_ret_o, w_pool_grp=w_pool_grp,
             pool_scale=pool_scale, w_pool_o=w_pool_o, w_mem_kv=w_mem_kv, w_mem_o=w_mem_o, w_out=w_out,
             w_ff1=w_ff1, w_ff2=w_ff2, norm1_g=norm1_g, norm2_g=norm2_g, mem_norm_g=mem_norm_g,
             final_norm_g=final_norm_g)
    mom = dict(w_in=m_w_in, ret_decay_logit=m_ret_decay_logit, w_ret_o=m_w_ret_o, w_pool_grp=m_w_pool_grp,
               pool_scale=m_pool_scale, w_pool_o=m_w_pool_o, w_mem_kv=m_w_mem_kv, w_mem_o=m_w_mem_o,
               w_out=m_w_out, w_ff1=m_w_ff1, w_ff2=m_w_ff2, norm1_g=m_norm1_g, norm2_g=m_norm2_g,
               mem_norm_g=m_mem_norm_g, final_norm_g=m_final_norm_g)
    vel = dict(w_in=v_w_in, ret_decay_logit=v_ret_decay_logit, w_ret_o=v_w_ret_o, w_pool_grp=v_w_pool_grp,
               pool_scale=v_pool_scale, w_pool_o=v_w_pool_o, w_mem_kv=v_w_mem_kv, w_mem_o=v_w_mem_o,
               w_out=v_w_out, w_ff1=v_w_ff1, w_ff2=v_w_ff2, norm1_g=v_norm1_g, norm2_g=v_norm2_g,
               mem_norm_g=v_mem_norm_g, final_norm_g=v_final_norm_g)

    bl, seq, d = x.shape
    mlen = mem.shape[1]
    depth = w_in.shape[0]
    t = bl * seq
    dk = d // 8

    def natural(n, g):
        if n in _DW_SHARDED:
            return _columns_side_by_side("relayout_" + n, g)
        if n in _COL:
            return jnp.transpose(g, (1, 0, 2)).reshape(g.shape[1], -1)
        return g.reshape(-1, g.shape[-1])

    def finish_gather(name, names, started, after):
        return {n: natural(n, g) for n, g in zip(names, _exchange_wait(name, False, started, after))}

    shards = [{n: w[n][l].astype(MM) for n in _BIG} for l in range(depth)]
    rest = _BIG[1:]
    (w_in0,) = _all_gather("gather_w_in", [shards[0][_BIG[0]]])
    full = [{_BIG[0]: natural(_BIG[0], w_in0)}]
    def start_layer(l, after):
        s_in = _exchange_start(f"gather_start_in{l}", False, [shards[l][_BIG[0]]], after=after)
        s_rest = _exchange_start(f"gather_start_{l}", False, [shards[l][n] for n in rest], after=[s_in[4]])
        return s_in, s_rest, (s_in[4], s_rest[4])

    pending = (None, _exchange_start("gather_start_0", False, [shards[0][n] for n in rest], after=[w_in0]))
    first_tokens = (pending[1][4],)
    pending_next = None
    if depth > 1:
        pending_next = start_layer(1, [pending[1][4]])
        first_tokens += pending_next[2]

    inv = ROPE_BASE ** (-jnp.arange(0, dk, 2, dtype=F32) / dk)
    ang = jnp.arange(seq, dtype=F32)[:, None] * inv[None, :]
    cos2 = jnp.concatenate([jnp.cos(ang), jnp.cos(ang)], axis=-1)
    sin2 = jnp.concatenate([-jnp.sin(ang), jnp.sin(ang)], axis=-1)
    log_g = jax.nn.log_sigmoid(ret_decay_logit)
    x2 = x.reshape(t, d)
    mem2 = mem.reshape(bl * mlen, d)
    gmem = mem_norm_g.reshape(1, d)

    def merge(a_r, y_p, o_a, g_r, g_p, g_m, w_r, w_p, w_m):
        f = lambda z: z.astype(F32)
        o_r, o_p, o_m = _dot(a_r, w_r), _dot(y_p, w_p), _dot(o_a, w_m)
        return _sigmoid(f(g_r)) * o_r + _sigmoid(f(g_p)) * o_p + _sigmoid(f(g_m)) * o_m, o_r, o_p, o_m

    def relu2(u):
        r = jnp.maximum(u.astype(MM), 0.0)
        return r * r

    def ident(a):
        return a

    saved = []
    xc = x2
    for l in range(depth):
        s = dict(x_in=xc)
        started_now = ()
        if l > 0:
            pending, pending_next = pending_next, None
            full.append(finish_gather(f"gather_wait_in{l}", _BIG[:1], pending[0], xc))
            if l + 1 < depth:
                pending_next = start_layer(l + 1, [full[l]["w_in"]])
                started_now = pending_next[2]
        fw = full[l]
        g1 = norm1_g[l].reshape(1, d)
        g2 = norm2_g[l].reshape(1, d)
        s["proj"], s["h1"] = _pmm("proj", _rms_prologue, [(xc, d, 0)], [g1], fw["w_in"],
                                  tm=2048, tn=1024, save_a=True, out_dtypes=(MM,),
                                  after=started_now if l > 0 else first_tokens)
        proj = s["proj"]
        s["o_raw"], s["a_ret"], s["sf"], s["sb"], s["qr"], s["kr"] = _ret_core_fwd(proj, cos2, sin2, log_g[l],
                                                                                   d, bl, seq)
        s["y"] = _pool_fwd(proj, w_pool_grp[l], pool_scale[l].reshape(1, -1), d, bl, seq)
        fw.update(finish_gather(f"gather_wait_{l}", rest, pending[1], s["a_ret"]))
        s["kv"], s["memn"] = _pmm("mem_kv", _rms_prologue, [(mem2, d, 0)], [gmem], fw["w_mem_kv"],
                                  tm=512, tn=512, save_a=True)
        s["o_att"] = _attn_fwd(proj, s["kv"], d, bl, seq, mlen)
        s["x_mid"], s["merged"], s["o_ret"], s["o_pool"], s["o_mem"] = _pmm(
            "merge_out", merge,
            [(s["a_ret"], d, 0), (s["y"], d // 2, 0), (s["o_att"], d // 2, 0), (proj, d, 4), (proj, d, 5), (proj, d, 6)],
            [fw["w_ret_o"], fw["w_pool_o"], fw["w_mem_o"]], fw["w_out"], tm=512, tn=1024, residual=xc, save_a=True,
            extra_outs=[(d, MM)] * 3)
        s["u"], s["h2"] = _pmm("ff1", _rms_prologue, [(s["x_mid"], d, 0)], [g2], fw["w_ff1"],
                               tm=512, tn=4096, save_a=True, out_dtypes=(MM,))
        (xc,) = _pmm("ff2", relu2, [(s["u"], s["u"].shape[1], 0)], [], fw["w_ff2"],
                     tm=512, tn=1024, residual=s["x_mid"])
        saved.append(s)

    dxc, g_final, loss_part = _loss_head(xc, loss_target.reshape(t, d), final_norm_g.reshape(1, d))
    loss = lax.psum(loss_part[0, 0], ("x", "y", "c"))

    small_names = ("w_pool_grp", "pool_scale", "norm1_g", "norm2_g", "ret_decay_logit")
    grads = {n: [None] * depth for n in small_names}
    group_a = ("w_ff1", "w_ff2")
    group_b = tuple(n for n in _BIG if n not in group_a)
    scatters = {}
    dmemn = jnp.zeros((bl * mlen, d), F32)

    def relu2_bwd(acc, u):
        return (acc * (2.0 * jnp.maximum(u.astype(F32), 0.0)),)

    def gates_bwd(acc, g_r, g_p, g_m, o_r, o_p, o_m, w_r, w_p, w_m):
        d_os, d_gs, backs = [], [], []
        for gz, oz, wz in ((g_r, o_r, w_r), (g_p, o_p, w_p), (g_m, o_m, w_m)):
            sg = _sigmoid(gz.astype(F32))
            d_o = (acc * sg).astype(MM)
            d_os.append(d_o)
            d_gs.append(acc * oz.astype(F32) * (sg * (1.0 - sg)))
            backs.append(_dot_nt(d_o, wz))
        return tuple(d_os + d_gs + backs)

    def to_send(n, g):
        a, b = w[n].shape[1:]
        if n in _DW_SHARDED:
            return g
        if n in _COL:
            return jnp.transpose(g.reshape(a, N_DEV, b), (1, 0, 2))
        return g.reshape(N_DEV, a, b)

    for l in reversed(range(depth)):
        s = saved[l]
        fw = full[l]
        proj = s["proj"]
        g1 = norm1_g[l].reshape(1, d)
        g2 = norm2_g[l].reshape(1, d)
        dw = {}
        (du,) = _pmm("ff2_bwd", ident, [(dxc, d, 0)], [], fw["w_ff2"], w_mode="nt", tm=512, tn=4096,
                     epilogue=relu2_bwd, epi_ins=[(s["u"], 0)], out_dtypes=(MM,))
        dw["w_ff2"] = _tnmm("dw_ff2", s["u"], dxc, a_fn=relu2)
        dw["w_ff1"] = _tnmm("dw_ff1", s["h2"], du, col_shards=True)
        scatters[l, "a"] = _exchange_start(f"scatter_start_a{l}", True, [to_send(n, dw[n]) for n in group_a])
        dmid, grads["norm2_g"][l] = _mm_rms_bwd("ff1_norm2_bwd", du, fw["w_ff1"], s["x_mid"], g2, dxc, tm=512)
        d_oret, d_opool, d_omem, dgr, dgp, dgm, da_ret, dy, do_att = _pmm(
            "out_bwd", ident, [(dmid, d, 0)], [], fw["w_out"], w_mode="nt", tm=256, tn=d, epilogue=gates_bwd,
            epi_ins=[(proj, 4 * d), (proj, 5 * d), (proj, 6 * d), (s["o_ret"], 0), (s["o_pool"], 0), (s["o_mem"], 0)],
            epi_full=[fw["w_ret_o"], fw["w_pool_o"], fw["w_mem_o"]], out_dtypes=(MM,) * 7 + (F32,) * 2,
            out_widths=[d] * 7 + [d // 2] * 2, after=(scatters[l, "a"][4],))
        dw["w_out"] = _tnmm("dw_out", s["merged"], dmid)
        dw["w_ret_o"] = _tnmm("dw_ret_o", s["a_ret"], d_oret)
        dw["w_pool_o"] = _tnmm("dw_pool_o", s["y"], d_opool)
        dw["w_mem_o"] = _tnmm("dw_mem_o", s["o_att"], d_omem)
        dg_ret, do_ret = _ret_post_bwd(da_ret, proj, s["o_raw"], d)
        dq, dkk, dvv, dlf, dlb = _ret_core_bwd(s["qr"], s["kr"], proj, do_ret, s["sf"], s["sb"], cos2, sin2,
                                               log_g[l], d, bl, seq)
        dl = jnp.stack([dlf[:, 0, 0].reshape(bl, HEADS).sum(0), dlb[:, 0, 0].reshape(bl, HEADS).sum(0)])
        grads["ret_decay_logit"][l] = dl * jax.nn.sigmoid(-ret_decay_logit[l])
        dp, grads["w_pool_grp"][l], dscale = _pool_bwd(proj, dy, w_pool_grp[l], pool_scale[l].reshape(1, -1),
                                                       d, bl, seq)
        grads["pool_scale"][l] = dscale.reshape(-1)
        dqm, dmk, dmv = _attn_bwd(proj, s["kv"], do_att, d, bl, seq, mlen)
        dkv = jnp.concatenate([dmk, dmv], axis=-1).astype(MM)
        dw["w_mem_kv"] = _tnmm("dw_mem_kv", s["memn"], dkv)
        (dmemn,) = _pmm("mem_kv_bwd", None, [(dkv, d, 0)], [], fw["w_mem_kv"], w_mode="nt", tm=512, tn=512,
                        residual=dmemn)
        dproj = [dq, dkk, dvv, dg_ret, dp, dqm, dgr, dgp, dgm]
        dw["w_in"] = _tnmm("dw_in", s["h1"], dproj, col_shards=True, tm=512, tk=512)
        scatters[l, "b"] = _exchange_start(f"scatter_start_b{l}", True, [to_send(n, dw[n]) for n in group_b])
        dxc, grads["norm1_g"][l] = _mm_rms_bwd("proj_norm1_bwd", dproj, fw["w_in"], s["x_in"], g1, dmid, tm=256,
                                               after=(scatters[l, "b"][4],))

    _, g_memn = _rms_bwd("mem_norm_bwd", dmemn, mem2, gmem, None)
    grad_x = dxc.reshape(bl, seq, d)

    small_g = dict(ret_decay_logit=jnp.stack(grads["ret_decay_logit"]), w_pool_grp=jnp.stack(grads["w_pool_grp"]),
                   pool_scale=jnp.stack(grads["pool_scale"]),
                   norm1_g=jnp.concatenate(grads["norm1_g"], axis=0), norm2_g=jnp.concatenate(grads["norm2_g"], axis=0),
                   mem_norm_g=g_memn.reshape(-1), final_norm_g=g_final.reshape(-1))
    small_started = _exchange_start("gather_small_start", False,
                                    [_pack_small(small_g, d, _SMALL_MM)[0].astype(MM),
                                     _pack_small(small_g, d, _SMALL_F32)[0]])

    big = {n: [lax.empty(w[n].shape, F32) for _ in range(4)] for n in _BIG}

    def update(l, grp, names, after):
        recv = _exchange_wait(f"scatter_wait_{grp}{l}", True, scatters[l, grp], after)
        for n, parts in zip(names, recv):
            big[n] = _adamw("adamw_" + n, parts, w[n], mom[n], vel[n], big[n], l)
        return big[names[-1]][0]

    after = dxc
    for l in reversed(range(1, depth)):
        for grp, names in (("a", group_a), ("b", group_b)):
            after = update(l, grp, names, after)
    after = update(0, "a", group_a, after)
    small_lands = _exchange_wait("gather_small_wait", False, small_started, after)
    small = [{} for _ in range(4)]
    for names, parts in zip((_SMALL_MM, _SMALL_F32), small_lands):
        w_small = _pack_small(w, d, names)
        res = _adamw("adamw_small", parts, w_small, _pack_small(mom, d, names), _pack_small(vel, d, names),
                     [lax.empty(w_small.shape, F32) for _ in range(4)], 0)
        after = res[0]
        for k in range(4):
            small[k].update(_unpack_small(res[k], w, d, names))
    update(0, "b", group_b, after)

    outs = [loss, grad_x]
    for k in range(4):
        outs += [big[n][k] if n in _BIG else small[k][n] for n in _WEIGHTS]
    return tuple(outs)
```

```python
import jax
import jax.numpy as jnp
from jax import lax
from jax.experimental import pallas as pl
from jax.experimental.pallas import tpu as pltpu

F32 = jnp.float32
MM = jnp.bfloat16
N_DEV = 8
HEADS = 4
POOL_WINDOWS = (2, 4, 8, 16)
RET_CHUNK = 256
EPS = 1e-6
ROPE_BASE = 10000.0
ADAM_LR, ADAM_B1, ADAM_B2, ADAM_EPS, ADAM_WD, ADAM_STEP = 0.001, 0.9, 0.999, 1e-08, 0.01, 10
V7X_VMEM_LIMIT = 56 * 1024 * 1024
MESH = pl.DeviceIdType.MESH


def _params(n_axes):
    return pltpu.CompilerParams(dimension_semantics=("arbitrary",) * n_axes,
                                vmem_limit_bytes=V7X_VMEM_LIMIT)


def _tile(n, pref, align=128):
    cands = [c for c in range(align, min(pref, n) + 1, align) if n % c == 0]
    return max(cands) if cands else n


def _sigmoid(z):
    return 0.5 * jnp.tanh(0.5 * z) + 0.5


def _dot(a, b):
    return jnp.dot(a, b, preferred_element_type=F32)


def _dot_nt(a, b):
    return lax.dot_general(a, b, (((1,), (1,)), ((), ())), preferred_element_type=F32)


def _dot_tn(a, b):
    return lax.dot_general(a, b, (((0,), (0,)), ((), ())), preferred_element_type=F32)


def _pmm(name, prologue, row_ins, vec_ins, w, *, tm, tn, w_mode="nn", residual=None, save_a=False,
         epilogue=None, epi_ins=(), out_dtypes=(F32,), after=(), extra_outs=(), epi_full=(), out_widths=None):
    m = row_ins[0][0].shape[0]
    if w_mode == "nn":
        k, n = w.shape
        tn = _tile(n, tn)
        w_spec = pl.BlockSpec((k, tn), lambda i, j: (0, j))
    else:
        n, k = w.shape
        tn = _tile(n, tn)
        w_spec = pl.BlockSpec((tn, k), lambda i, j: (j, 0))
    tm = _tile(m, tm, 8)
    n_row, n_vec, n_epi, n_out = len(row_ins), len(vec_ins), len(epi_ins), len(out_dtypes)
    has_res = residual is not None
    use_scr = prologue is not None
    out_widths = [n] * n_out if out_widths is None else list(out_widths)
    assert all(wd == n for wd in out_widths) or tn == n

    def body(*refs):
        row_refs = refs[:n_row]
        p = n_row
        vec_refs = refs[p:p + n_vec]
        p += n_vec
        w_ref = refs[p]
        p += 1
        res_ref = refs[p] if has_res else None
        p += int(has_res)
        epi_refs = refs[p:p + n_epi + len(epi_full)]
        p += n_epi + len(epi_full) + len(after)
        out_refs = refs[p:p + n_out]
        p += n_out
        a_out = refs[p] if save_a else None
        p += int(save_a)
        extra_refs = refs[p:p + len(extra_outs)]
        p += len(extra_outs)
        if use_scr:
            a_src = refs[p]

            @pl.when(pl.program_id(1) == 0)
            def _():
                made = prologue(*[r[...] for r in row_refs], *[v[...] for v in vec_refs])
                made = made if isinstance(made, tuple) else (made,)
                a = made[0].astype(MM)
                a_src[...] = a
                if save_a:
                    a_out[...] = a
                for e_ref, e in zip(extra_refs, made[1:]):
                    e_ref[...] = e.astype(e_ref.dtype)
        else:
            a_src = row_refs[0]
        if w_mode == "nt":
            acc = _dot_nt(a_src[...], w_ref[...])
        else:
            acc = _dot(a_src[...], w_ref[...])
        if has_res:
            acc = acc + res_ref[...]
        outs = epilogue(acc, *[e[...] for e in epi_refs]) if epilogue is not None else (acc,)
        for o_ref, o in zip(out_refs, outs):
            o_ref[...] = o.astype(o_ref.dtype)

    in_specs = [pl.BlockSpec((tm, wd), lambda i, j, cb=cb: (i, cb)) for (_, wd, cb) in row_ins]
    in_specs += [pl.BlockSpec(v.shape, lambda i, j: (0, 0)) for v in vec_ins]
    in_specs += [w_spec]
    args = [r[0] for r in row_ins] + list(vec_ins) + [w]
    if has_res:
        in_specs.append(pl.BlockSpec((tm, tn), lambda i, j: (i, j)))
        args.append(residual)
    for (arr, off) in epi_ins:
        assert off % tn == 0
        in_specs.append(pl.BlockSpec((tm, tn), lambda i, j, ob=off // tn: (i, ob + j)))
        args.append(arr)
    in_specs += [pl.BlockSpec(v.shape, lambda i, j: (0, 0)) for v in epi_full]
    args += list(epi_full)
    n_after = len(after)
    in_specs += [pl.BlockSpec(memory_space=pl.ANY)] * n_after
    args += list(after)
    out_specs = [pl.BlockSpec((tm, tn if wd == n else wd), lambda i, j: (i, j)) for wd in out_widths]
    out_shape = [jax.ShapeDtypeStruct((m, wd), dt) for wd, dt in zip(out_widths, out_dtypes)]
    if save_a:
        out_specs.append(pl.BlockSpec((tm, k), lambda i, j: (i, 0)))
        out_shape.append(jax.ShapeDtypeStruct((m, k), MM))
    for wd, dt in extra_outs:
        out_specs.append(pl.BlockSpec((tm, wd), lambda i, j: (i, 0)))
        out_shape.append(jax.ShapeDtypeStruct((m, wd), dt))
    scratch = [pltpu.VMEM((tm, k), MM)] if use_scr else []
    return pl.pallas_call(body, name=name, grid=(m // tm, n // tn), in_specs=in_specs,
                          out_specs=out_specs, out_shape=out_shape, scratch_shapes=scratch,
                          compiler_params=_params(2))(*args)


def _tnmm(name, a, b, *, tm=1024, tn=1024, tk=1024, col_shards=False, a_fn=None):
    t, m = a.shape
    pieces = list(b) if isinstance(b, (list, tuple)) else [b]
    widths = [p.shape[1] for p in pieces]
    offs = [sum(widths[:p]) for p in range(len(pieces))]
    n = sum(widths)
    tm, tk = _tile(m, tm), _tile(t, tk, 8)
    per_tile = 1
    if col_shards:
        wb = n // N_DEV
        if len(pieces) > 1:
            tn = n
        while 2 * per_tile * wb <= tn and 2 * per_tile <= N_DEV:
            per_tile *= 2
        tn = per_tile * wb
        out_spec = pl.BlockSpec((per_tile, tm, wb), lambda i, j, kk: (j, i, 0))
        out_shape = jax.ShapeDtypeStruct((N_DEV, m, wb), MM)
    else:
        tn = _tile(n, tn)
        out_spec = pl.BlockSpec((tm, tn), lambda i, j, kk: (i, j))
        out_shape = jax.ShapeDtypeStruct((m, n), MM)
    nk = t // tk

    assert len(pieces) == 1 or tn == n

    def body(a_ref, *rest):
        b_refs, (o_ref, acc) = rest[:len(pieces)], rest[len(pieces):]
        kk = pl.program_id(2)

        @pl.when(kk == 0)
        def _():
            acc[...] = jnp.zeros_like(acc)

        av = (a_ref[...] if a_fn is None else a_fn(a_ref[...])).astype(MM)
        if len(pieces) == 1:
            acc[...] += _dot_tn(av, b_refs[0][...].astype(MM))
        else:
            for b_ref, off, wd in zip(b_refs, offs, widths):
                acc[:, off:off + wd] += _dot_tn(av, b_ref[...].astype(MM))

        @pl.when(kk == nk - 1)
        def _():
            if col_shards:
                for sh in range(per_tile):
                    o_ref[sh] = acc[:, sh * wb:(sh + 1) * wb].astype(o_ref.dtype)
            else:
                o_ref[...] = acc[...].astype(o_ref.dtype)

    return pl.pallas_call(
        body, name=name, grid=(m // tm, n // tn, nk),
        in_specs=[pl.BlockSpec((tk, tm), lambda i, j, kk: (kk, i))]
        + [pl.BlockSpec((tk, tn if len(pieces) == 1 else wd), lambda i, j, kk: (kk, j)) for wd in widths],
        out_specs=out_spec, out_shape=out_shape,
        scratch_shapes=[pltpu.VMEM((tm, tn), F32)],
        compiler_params=_params(3))(a, *pieces)


def _rms_prologue(x, g):
    r = lax.rsqrt(jnp.mean(x * x, axis=-1, keepdims=True) + EPS)
    return x * r * g


def _rms_bwd_rows(dh, x, g):
    d = x.shape[-1]
    r = lax.rsqrt(jnp.mean(x * x, axis=-1, keepdims=True) + EPS)
    xh = x * r
    dxh = dh * g
    dx = r * (dxh - xh * (jnp.sum(dxh * xh, axis=-1, keepdims=True) / d))
    dg = jnp.sum(dh * xh, axis=0, keepdims=True)
    return dx, dg


def _rms_bwd(name, dh, x, g, dres, *, tm=512):
    m, d = x.shape
    tm = min(tm, m)
    has_res = dres is not None

    def body(*refs):
        if has_res:
            dh_ref, x_ref, g_ref, r_ref, dx_ref, dg_ref = refs
        else:
            dh_ref, x_ref, g_ref, dx_ref, dg_ref = refs
        dx, dg = _rms_bwd_rows(dh_ref[...], x_ref[...], g_ref[...])
        if has_res:
            dx = dx + r_ref[...]
        dx_ref[...] = dx

        @pl.when(pl.program_id(0) == 0)
        def _():
            dg_ref[...] = jnp.zeros_like(dg_ref)

        dg_ref[...] += dg

    row = pl.BlockSpec((tm, d), lambda i: (i, 0))
    vec = pl.BlockSpec((1, d), lambda i: (0, 0))
    in_specs = [row, row, vec] + ([row] if has_res else [])
    args = [dh, x, g] + ([dres] if has_res else [])
    return pl.pallas_call(body, name=name, grid=(m // tm,), in_specs=in_specs, out_specs=[row, vec],
                          out_shape=[jax.ShapeDtypeStruct((m, d), F32), jax.ShapeDtypeStruct((1, d), F32)],
                          compiler_params=_params(1))(*args)


def _mm_rms_bwd(name, a, w, x, g, dres, *, tm, after=()):
    pieces = list(a) if isinstance(a, (list, tuple)) else [a]
    widths = [p.shape[1] for p in pieces]
    m = pieces[0].shape[0]
    d = w.shape[0]
    tm = _tile(m, tm, 8)
    n_a = len(pieces)

    def body(*refs):
        a_refs = refs[:n_a]
        w_ref, x_ref, g_ref, r_ref = refs[n_a:n_a + 4]
        dx_ref, dg_ref = refs[n_a + 4 + len(after):]

        av = a_refs[0][...] if n_a == 1 else jnp.concatenate([a_ref[...] for a_ref in a_refs], axis=1)
        dh = _dot_nt(av, w_ref[...])
        dx, dg = _rms_bwd_rows(dh, x_ref[...], g_ref[...])
        dx_ref[...] = dx + r_ref[...]

        @pl.when(pl.program_id(0) == 0)
        def _():
            dg_ref[...] = jnp.zeros_like(dg_ref)

        dg_ref[...] += dg

    row = pl.BlockSpec((tm, d), lambda i: (i, 0))
    vec = pl.BlockSpec((1, d), lambda i: (0, 0))
    return pl.pallas_call(
        body, name=name, grid=(m // tm,),
        in_specs=[pl.BlockSpec((tm, wd), lambda i: (i, 0)) for wd in widths]
        + [pl.BlockSpec(w.shape, lambda i: (0, 0)), row, vec, row]
        + [pl.BlockSpec(memory_space=pl.ANY)] * len(after),
        out_specs=[row, vec],
        out_shape=[jax.ShapeDtypeStruct((m, d), F32), jax.ShapeDtypeStruct((1, d), F32)],
        compiler_params=_params(1))(*pieces, w, x, g, dres, *after)


def _loss_head(x, target, g, *, tm=256):
    m, d = x.shape
    tm = min(tm, m)

    def body(x_ref, t_ref, g_ref, dx_ref, dg_ref, loss_ref):
        xv, gv = x_ref[...], g_ref[...]
        y = _rms_prologue(xv, gv)
        err = y - t_ref[...]
        part = 0.5 * jnp.sum(jnp.sum(err * err, axis=-1, keepdims=True) / d)
        dx, dg = _rms_bwd_rows(err / d, xv, gv)
        dx_ref[...] = dx

        @pl.when(pl.program_id(0) == 0)
        def _():
            dg_ref[...] = jnp.zeros_like(dg_ref)
            loss_ref[...] = jnp.zeros_like(loss_ref)

        dg_ref[...] += dg
        loss_ref[...] += jnp.full(loss_ref.shape, part, F32)

    row = pl.BlockSpec((tm, d), lambda i: (i, 0))
    vec = pl.BlockSpec((1, d), lambda i: (0, 0))
    lspec = pl.BlockSpec((1, 128), lambda i: (0, 0))
    return pl.pallas_call(body, name="loss_head", grid=(m // tm,), in_specs=[row, row, vec],
                          out_specs=[row, vec, lspec],
                          out_shape=[jax.ShapeDtypeStruct((m, d), F32), jax.ShapeDtypeStruct((1, d), F32),
                                     jax.ShapeDtypeStruct((1, 128), F32)],
                          compiler_params=_params(1))(x, target, g)


def _rot(xv, cos2, sin2, half):
    return xv * cos2 + pltpu.roll(xv, half, 1) * sin2


def _rot_t(dv, cos2, sin2, half):
    return dv * cos2 + pltpu.roll(dv * sin2, half, 1)


def _ret_consts(lg_ref, h, t, dk):
    lf, lb = lg_ref[0, h], lg_ref[1, h]
    ab = (lax.broadcasted_iota(jnp.int32, (t, t), 0) - lax.broadcasted_iota(jnp.int32, (t, t), 1)).astype(F32)
    dmat = jnp.exp(jnp.where(ab >= 0, lf * ab, -lb * ab))
    up = lax.broadcasted_iota(jnp.int32, (t, dk), 0).astype(F32) + 1.0
    down = float(t) - up
    one = jnp.ones((1, 1), F32)
    return dict(ab=ab, dmat=dmat, xi_f=jnp.exp(lf * up), zeta_f=jnp.exp(lf * down), xi_b=jnp.exp(lb * up),
                zeta_b=jnp.exp(lb * down), up=up[:, 0:1], down=down[:, 0:1],
                cf=jnp.exp(one * (lf * t)), cb=jnp.exp(one * (lb * t)))


def _scaled(xv, rows):
    return (xv.astype(F32) * rows).astype(MM)


def _ret_core_fwd(proj, cos2, sin2, lg, d, bl, seq, *, tc=RET_CHUNK):
    t = proj.shape[0]
    dk, dv = d // 8, d // 4
    tc = min(tc, seq)
    nc = seq // tc
    scale = float(dk) ** -0.5

    def body(lg_ref, qp_ref, kp_ref, v_ref, g_ref, c_ref, s_ref, o_ref, a_ref, sf_ref, sb_ref, q_ref, k_ref, o_acc):
        c = _ret_consts(lg_ref, pl.program_id(1), tc, dk)

        def rows_of(i):
            return pl.ds(pl.multiple_of(i * tc, tc), tc)

        def rotate(i, carry):
            rows = rows_of(i)
            cs, sn = c_ref[rows, :], s_ref[rows, :]
            q_ref[rows, :] = _rot(qp_ref[rows, :].astype(F32), cs, sn, dk // 2).astype(MM)
            k_ref[rows, :] = (_rot(kp_ref[rows, :].astype(F32), cs, sn, dk // 2) * scale).astype(MM)
            return carry

        lax.fori_loop(0, nc, rotate, 0)

        def fwd_step(i, sf):
            rows = rows_of(i)
            sf_ref[i] = sf
            q, kk, v = q_ref[rows, :], k_ref[rows, :], v_ref[rows, :]
            p = (_dot_nt(q, kk) * c["dmat"]).astype(MM)
            o_acc[rows, :] = _dot(p, v) + _dot(_scaled(q, c["xi_f"]), sf.astype(MM))
            return sf * c["cf"] + _dot_tn(_scaled(kk, c["zeta_f"]), v)

        lax.fori_loop(0, nc, fwd_step, jnp.zeros((dk, dv), F32))

        def bwd_step(ii, sb):
            rows = rows_of(nc - 1 - ii)
            sb_ref[nc - 1 - ii] = sb
            q, kk, v = q_ref[rows, :], k_ref[rows, :], v_ref[rows, :]
            o_acc[rows, :] += _dot(_scaled(q, c["zeta_b"]), sb.astype(MM))
            return sb * c["cb"] + _dot_tn(_scaled(kk, c["xi_b"]), v)

        lax.fori_loop(0, nc, bwd_step, jnp.zeros((dk, dv), F32))

        def post(i, carry):
            rows = rows_of(i)
            o = o_acc[rows, :]
            o_ref[rows, :] = o.astype(o_ref.dtype)
            oc = o - jnp.mean(o, axis=-1, keepdims=True)
            on = oc * lax.rsqrt(jnp.mean(oc * oc, axis=-1, keepdims=True) + EPS)
            g = g_ref[rows, :].astype(F32)
            a_ref[rows, :] = (on * (g * _sigmoid(g))).astype(MM)
            return carry

        lax.fori_loop(0, nc, post, 0)

    qk = pl.BlockSpec((seq, dk), lambda b, h: (b, h))
    vv = pl.BlockSpec((seq, dv), lambda b, h: (b, h))
    tab = pl.BlockSpec((seq, dk), lambda b, h: (0, 0))
    states = pl.BlockSpec((None, nc, dk, dv), lambda b, h: (b * HEADS + h, 0, 0, 0))
    return pl.pallas_call(
        body, name="ret_core_fwd", grid=(bl, HEADS),
        in_specs=[pl.BlockSpec(memory_space=pltpu.SMEM), qk, pl.BlockSpec((seq, dk), lambda b, h: (b, HEADS + h)),
                  pl.BlockSpec((seq, dv), lambda b, h: (b, HEADS + h)),
                  pl.BlockSpec((seq, dv), lambda b, h: (b, 2 * HEADS + h)), tab, tab],
        out_specs=[vv, vv, states, states, qk, qk],
        out_shape=[jax.ShapeDtypeStruct((t, d), MM), jax.ShapeDtypeStruct((t, d), MM),
                   jax.ShapeDtypeStruct((bl * HEADS, nc, dk, dv), F32),
                   jax.ShapeDtypeStruct((bl * HEADS, nc, dk, dv), F32),
                   jax.ShapeDtypeStruct((t, d // 2), MM), jax.ShapeDtypeStruct((t, d // 2), MM)],
        scratch_shapes=[pltpu.VMEM((seq, dv), F32)],
        compiler_params=_params(2))(lg, proj, proj, proj, proj, cos2, sin2)


def _ret_post_bwd(da, proj, o_raw, d, *, ts=2048):
    t = da.shape[0]
    dv = d // 4
    ts = min(ts, t)

    def body(da_ref, g_ref, o_ref, dg_ref, do_ref):
        o, g, dav = o_ref[...].astype(F32), g_ref[...].astype(F32), da_ref[...].astype(F32)
        mu = jnp.mean(o, axis=-1, keepdims=True)
        oc = o - mu
        r = lax.rsqrt(jnp.mean(oc * oc, axis=-1, keepdims=True) + EPS)
        on = oc * r
        sg = _sigmoid(g)
        don = dav * (g * sg)
        dg_ref[...] = (dav * on * (sg * (1.0 + g * (1.0 - sg)))).astype(MM)
        do = r * (don - jnp.mean(don, axis=-1, keepdims=True) - on * jnp.mean(don * on, axis=-1, keepdims=True))
        do_ref[...] = do.astype(MM)

    blk = pl.BlockSpec((ts, dv), lambda i, h: (i, h))
    return pl.pallas_call(
        body, name="ret_post_bwd", grid=(t // ts, HEADS),
        in_specs=[blk, pl.BlockSpec((ts, dv), lambda i, h: (i, 2 * HEADS + h)), blk],
        out_specs=[blk, blk],
        out_shape=[jax.ShapeDtypeStruct((t, d), MM), jax.ShapeDtypeStruct((t, d), MM)],
        compiler_params=_params(2))(da, proj, o_raw)


def _ret_core_bwd(qr, kr, proj, do, sf_in, sb_in, cos2, sin2, lg, d, bl, seq, *, tc=RET_CHUNK):
    t = qr.shape[0]
    dk, dv = d // 8, d // 4
    tc = min(tc, seq)
    nc = seq // tc
    scale = float(dk) ** -0.5

    def body(lg_ref, q_ref, k_ref, v_ref, do_ref, sf_all, sb_all, c_ref, s_ref, dq_ref, dk_ref, dv_ref,
             dlf_ref, dlb_ref, dq_acc, dk_acc, dv_acc):
        c = _ret_consts(lg_ref, pl.program_id(1), tc, dk)
        fwd = c["ab"] >= 0
        zero_state = jnp.zeros((dk, dv), F32)
        zero = jnp.zeros((1, 1), F32)

        def rows_of(i):
            return pl.ds(pl.multiple_of(i * tc, tc), tc)

        def total(xv):
            return jnp.sum(xv, keepdims=True)

        def fwd_sweep(i, carry):
            hh, dlf, dlb = carry
            rows = rows_of(i)
            q, kk, v, dov = q_ref[rows, :], k_ref[rows, :], v_ref[rows, :], do_ref[rows, :]
            dof, vf = dov.astype(F32), v.astype(F32)
            p = _dot_nt(q, kk) * c["dmat"]
            da = _dot_nt(dov, v)
            x = p * da * c["ab"]
            dlf = dlf + total(jnp.where(fwd, x, 0.0))
            dlb = dlb - total(jnp.where(fwd, 0.0, x))
            pb, dpb = p.astype(MM), (da * c["dmat"]).astype(MM)
            dq = _dot(dpb, kk)
            dkc = _dot_tn(dpb, q)
            dvc = _dot_tn(pb, dov)
            sf, sb = sf_all[i], sb_all[i]
            sfb, sbb = sf.astype(MM), sb.astype(MM)
            q_xf, q_zb = _scaled(q, c["xi_f"]), _scaled(q, c["zeta_b"])
            dq = dq + _dot_nt(dov, sfb) * c["xi_f"] + _dot_nt(dov, sbb) * c["zeta_b"]
            dlf = dlf + total(jnp.sum(_dot(q_xf, sfb) * dof, axis=-1, keepdims=True) * c["up"])
            dlb = dlb + total(jnp.sum(_dot(q_zb, sbb) * dof, axis=-1, keepdims=True) * c["down"])
            hb = hh.astype(MM)
            dkc = dkc + _dot_nt(v, hb) * c["xi_b"]
            dv_bx = _dot(_scaled(kk, c["xi_b"]), hb)
            dlb = dlb + total(jnp.sum(vf * dv_bx, axis=-1, keepdims=True) * c["up"])
            dlb = dlb + float(tc) * total(hh * (sb * c["cb"]))
            dq_acc[rows, :] = dq
            dk_acc[rows, :] = dkc
            dv_acc[rows, :] = dvc + dv_bx
            return hh * c["cb"] + _dot_tn(q_zb, dov), dlf, dlb

        _, dlf, dlb = lax.fori_loop(0, nc, fwd_sweep, (zero_state, zero, zero))

        def rev_sweep(ii, carry):
            gg, dlf = carry
            i = nc - 1 - ii
            rows = rows_of(i)
            q, kk, v, dov = q_ref[rows, :], k_ref[rows, :], v_ref[rows, :], do_ref[rows, :]
            gb = gg.astype(MM)
            dk_acc[rows, :] += _dot_nt(v, gb) * c["zeta_f"]
            dv_fx = _dot(_scaled(kk, c["zeta_f"]), gb)
            dv_acc[rows, :] += dv_fx
            dlf = dlf + total(jnp.sum(v.astype(F32) * dv_fx, axis=-1, keepdims=True) * c["down"])
            dlf = dlf + float(tc) * total(gg * (sf_all[i] * c["cf"]))
            return gg * c["cf"] + _dot_tn(_scaled(q, c["xi_f"]), dov), dlf

        _, dlf = lax.fori_loop(0, nc, rev_sweep, (zero_state, dlf))

        cs, sn = c_ref[...], s_ref[...]
        dq_ref[...] = _rot_t(dq_acc[...], cs, sn, dk // 2).astype(MM)
        dk_ref[...] = (_rot_t(dk_acc[...], cs, sn, dk // 2) * scale).astype(MM)
        dv_ref[...] = dv_acc[...].astype(MM)
        dlf_ref[...] = jnp.broadcast_to(dlf, dlf_ref.shape)
        dlb_ref[...] = jnp.broadcast_to(dlb, dlb_ref.shape)

    qk = pl.BlockSpec((seq, dk), lambda b, h: (b, h))
    vv = pl.BlockSpec((seq, dv), lambda b, h: (b, h))
    tab = pl.BlockSpec((seq, dk), lambda b, h: (0, 0))
    dl = pl.BlockSpec((None, 8, 128), lambda b, h: (b * HEADS + h, 0, 0))
    states = pl.BlockSpec((None, nc, dk, dv), lambda b, h: (b * HEADS + h, 0, 0, 0))
    return pl.pallas_call(
        body, name="ret_core_bwd", grid=(bl, HEADS),
        in_specs=[pl.BlockSpec(memory_space=pltpu.SMEM), qk, qk, pl.BlockSpec((seq, dv), lambda b, h: (b, HEADS + h)),
                  vv, states, states, tab, tab],
        out_specs=[qk, qk, vv, dl, dl],
        out_shape=[jax.ShapeDtypeStruct((t, d // 2), MM), jax.ShapeDtypeStruct((t, d // 2), MM),
                   jax.ShapeDtypeStruct((t, d), MM),
                   jax.ShapeDtypeStruct((bl * HEADS, 8, 128), F32), jax.ShapeDtypeStruct((bl * HEADS, 8, 128), F32)],
        scratch_shapes=[pltpu.VMEM((seq, dk), F32), pltpu.VMEM((seq, dk), F32), pltpu.VMEM((seq, dv), F32)],
        compiler_params=_params(2))(lg, qr, kr, proj, do, sf_in, sb_in, cos2, sin2)


def _window_count(row, w, seq):
    return (jnp.minimum(row + w // 2, seq) - jnp.maximum(row - w // 2, 0)).astype(F32)


def _window_sum(pv, row, w, seq, sign):
    acc = None
    for j in range(-(w // 2), w // 2):
        if j == 0:
            term = pv
        else:
            src = row + sign * j
            term = jnp.where((src >= 0) & (src < seq), pltpu.roll(pv, (-sign * j) % seq, 0), 0.0)
        acc = term if acc is None else acc + term
    return acc


def _pool_fwd(proj, w_grp, scale, d, bl, seq):
    t = proj.shape[0]
    dg = d // 8

    def body(p_ref, w_ref, s_ref, y_ref):
        row = lax.broadcasted_iota(jnp.int32, (seq, dg), 0)
        for gi, w in enumerate(POOL_WINDOWS):
            sl = slice(gi * dg, (gi + 1) * dg)
            pg = p_ref[:, sl].astype(F32)
            mixed = _window_sum(pg, row, w, seq, 1) / _window_count(row, w, seq) - pg
            yp = _dot(mixed.astype(MM), w_ref[gi].astype(MM))
            y_ref[:, sl] = (yp * s_ref[:, sl]).astype(MM)

    return pl.pallas_call(
        body, name="pool_fwd", grid=(bl,),
        in_specs=[pl.BlockSpec((seq, d // 2), lambda b: (b, 6)),
                  pl.BlockSpec(w_grp.shape, lambda b: (0, 0, 0)),
                  pl.BlockSpec((1, d // 2), lambda b: (0, 0))],
        out_specs=pl.BlockSpec((seq, d // 2), lambda b: (b, 0)),
        out_shape=jax.ShapeDtypeStruct((t, d // 2), MM),
        compiler_params=_params(1))(proj, w_grp, scale)


def _pool_bwd(proj, dy, w_grp, scale, d, bl, seq):
    t = proj.shape[0]
    dg = d // 8

    def body(p_ref, dy_ref, w_ref, s_ref, dp_ref, dw_ref, ds_ref):
        @pl.when(pl.program_id(0) == 0)
        def _():
            dw_ref[...] = jnp.zeros_like(dw_ref)
            ds_ref[...] = jnp.zeros_like(ds_ref)

        row = lax.broadcasted_iota(jnp.int32, (seq, dg), 0)
        for gi, w in enumerate(POOL_WINDOWS):
            sl = slice(gi * dg, (gi + 1) * dg)
            pg = p_ref[:, sl].astype(F32)
            cnt = _window_count(row, w, seq)
            mixb = (_window_sum(pg, row, w, seq, 1) / cnt - pg).astype(MM)
            wgb = w_ref[gi].astype(MM)
            yp = _dot(mixb, wgb)
            dyg = dy_ref[:, sl]
            ds_ref[:, sl] += jnp.sum(dyg * yp, axis=0, keepdims=True)
            dyp = (dyg * s_ref[:, sl]).astype(MM)
            dmixed = _dot_nt(dyp, wgb)
            dw_ref[gi] += _dot_tn(mixb, dyp)
            dp_ref[:, sl] = (_window_sum(dmixed / cnt, row, w, seq, -1) - dmixed).astype(MM)

    half = pl.BlockSpec((seq, d // 2), lambda b: (b, 0))
    wspec = pl.BlockSpec(w_grp.shape, lambda b: (0, 0, 0))
    sspec = pl.BlockSpec((1, d // 2), lambda b: (0, 0))
    return pl.pallas_call(
        body, name="pool_bwd", grid=(bl,),
        in_specs=[pl.BlockSpec((seq, d // 2), lambda b: (b, 6)), half, wspec, sspec],
        out_specs=[half, wspec, sspec],
        out_shape=[jax.ShapeDtypeStruct((t, d // 2), MM), jax.ShapeDtypeStruct(w_grp.shape, F32),
                   jax.ShapeDtypeStruct((1, d // 2), F32)],
        compiler_params=_params(1))(proj, dy, w_grp, scale)


def _attn_probs(q, kk, dh):
    s = _dot_nt(q, kk) * (float(dh) ** -0.5)
    e = jnp.exp(s - jnp.max(s, axis=-1, keepdims=True))
    return e / jnp.sum(e, axis=-1, keepdims=True)


def _attn_fwd(proj, kv, d, bl, seq, mlen, *, tq=2048):
    t = proj.shape[0]
    dh = d // 8
    tq = min(tq, seq)
    nq = seq // tq

    def body(q_ref, k_ref, v_ref, o_ref):
        a = _attn_probs(q_ref[...].astype(MM), k_ref[...].astype(MM), dh)
        o_ref[...] = _dot(a.astype(MM), v_ref[...].astype(MM)).astype(MM)

    return pl.pallas_call(
        body, name="attn_fwd", grid=(bl, HEADS, nq),
        in_specs=[pl.BlockSpec((tq, dh), lambda b, h, i: (b * nq + i, 7 * HEADS + h)),
                  pl.BlockSpec((mlen, dh), lambda b, h, i: (b, h)),
                  pl.BlockSpec((mlen, dh), lambda b, h, i: (b, HEADS + h))],
        out_specs=pl.BlockSpec((tq, dh), lambda b, h, i: (b * nq + i, h)),
        out_shape=jax.ShapeDtypeStruct((t, d // 2), MM),
        compiler_params=_params(3))(proj, kv, kv)


def _attn_bwd(proj, kv, do, d, bl, seq, mlen, *, tq=2048):
    t = proj.shape[0]
    dh = d // 8
    tq = min(tq, seq)
    nq = seq // tq

    def body(q_ref, k_ref, v_ref, do_ref, dq_ref, dk_ref, dv_ref):
        @pl.when(pl.program_id(2) == 0)
        def _():
            dk_ref[...] = jnp.zeros_like(dk_ref)
            dv_ref[...] = jnp.zeros_like(dv_ref)

        q, kk, vv = q_ref[...].astype(MM), k_ref[...].astype(MM), v_ref[...].astype(MM)
        dov = do_ref[...].astype(MM)
        a = _attn_probs(q, kk, dh)
        dp = _dot_nt(dov, vv)
        ds = (a * (dp - jnp.sum(dp * a, axis=-1, keepdims=True)) * (float(dh) ** -0.5)).astype(MM)
        dq_ref[...] = _dot(ds, kk).astype(MM)
        dk_ref[...] += _dot_tn(ds, q)
        dv_ref[...] += _dot_tn(a.astype(MM), dov)

    qs = pl.BlockSpec((tq, dh), lambda b, h, i: (b * nq + i, h))
    ms = pl.BlockSpec((mlen, dh), lambda b, h, i: (b, h))
    return pl.pallas_call(
        body, name="attn_bwd", grid=(bl, HEADS, nq),
        in_specs=[pl.BlockSpec((tq, dh), lambda b, h, i: (b * nq + i, 7 * HEADS + h)), ms,
                  pl.BlockSpec((mlen, dh), lambda b, h, i: (b, HEADS + h)), qs],
        out_specs=[qs, ms, ms],
        out_shape=[jax.ShapeDtypeStruct((t, d // 2), MM), jax.ShapeDtypeStruct((bl * mlen, d // 2), F32),
                   jax.ShapeDtypeStruct((bl * mlen, d // 2), F32)],
        compiler_params=_params(3))(proj, kv, kv, do)


def _comm_call(name, body, arrays, out_shapes):
    n = len(arrays)
    hbm = pl.BlockSpec(memory_space=pl.ANY)
    return pl.pallas_call(
        body, name=name, out_shape=out_shapes, in_specs=[hbm] * n, out_specs=[hbm] * n,
        scratch_shapes=[pltpu.SemaphoreType.DMA((7 * n,)), pltpu.SemaphoreType.DMA((7 * n,)),
                        pltpu.SemaphoreType.DMA((n,))],
    )(*arrays)


def _all_gather(name, shards):
    n = len(shards)

    def body(*refs):
        x_refs, out_refs = refs[:n], refs[n:2 * n]
        send_sems, recv_sems, local_sems = refs[2 * n:]
        x, y, c = lax.axis_index("x"), lax.axis_index("y"), lax.axis_index("c")
        me, sibling = (x, y, c), (x, y, 1 - c)
        chips = [(1 - x, y), (x, 1 - y), (1 - x, 1 - y)]

        def copy(o, k, block, to, src=None):
            slot = out_refs[o].at[4 * block[0] + 2 * block[1] + block[2]]
            return pltpu.make_async_remote_copy(
                src_ref=slot if src is None else src, dst_ref=slot, send_sem=send_sems.at[7 * o + k],
                recv_sem=recv_sems.at[7 * o + k], device_id=to, device_id_type=MESH)

        locals_, remotes = [], []
        for o in range(n):
            mine = pltpu.make_async_copy(x_refs[o], out_refs[o].at[4 * x + 2 * y + c], local_sems.at[o])
            mine.start()
            locals_.append(mine)
            first = [copy(o, 0, me, sibling, src=x_refs[o])]
            first += [copy(o, 1 + j, me, (*chip, c), src=x_refs[o]) for j, chip in enumerate(chips)]
            for cp in first:
                cp.start()
            remotes += first
        for o in range(n):
            for j, chip in enumerate(chips):
                copy(o, 1 + j, (*chip, c), me).wait_recv()
                passed = copy(o, 4 + j, (*chip, c), sibling)
                passed.start()
                remotes.append(passed)
        for o in range(n):
            copy(o, 0, sibling, me).wait_recv()
            for j, chip in enumerate(chips):
                copy(o, 4 + j, (*chip, 1 - c), me).wait_recv()
        for cp in remotes:
            cp.wait_send()
        for mine in locals_:
            mine.wait()

    outs = [jax.ShapeDtypeStruct((N_DEV,) + s.shape, s.dtype) for s in shards]
    return _comm_call(name, body, shards, outs)


def _columns_side_by_side(name, g):
    _, k, wb = g.shape

    def body(x_ref, o_ref):
        o_ref[...] = x_ref[...]

    return pl.pallas_call(
        body, name=name, grid=(N_DEV,), in_specs=[pl.BlockSpec((None, k, wb), lambda j: (j, 0, 0))],
        out_specs=pl.BlockSpec((k, wb), lambda j: (0, j)),
        out_shape=jax.ShapeDtypeStruct((k, N_DEV * wb), g.dtype), compiler_params=_params(1))(g)


def _peer_of(k, x, y, c):
    peer = (1 - x if k & 4 else x, 1 - y if k & 2 else y, 1 - c if k & 1 else c)
    return peer, 4 * peer[0] + 2 * peer[1] + peer[2]


def _slot(land, idx, side_by_side):
    if not side_by_side:
        return land.at[idx]
    b = land.shape[1] // N_DEV
    return land.at[:, pl.ds(pl.multiple_of(idx * b, b), b)]


def _split_copies(scatter, srcs, lands, send_sems, recv_sems, arriving, mid=()):
    x, y, c = lax.axis_index("x"), lax.axis_index("y"), lax.axis_index("c")
    me_idx = 4 * x + 2 * y + c
    copies = []
    for o, (src, land) in enumerate(zip(srcs, lands)):
        for k in range(1, N_DEV):
            peer, p_idx = _peer_of(k, x, y, c)
            mine = src.at[p_idx] if scatter else src
            sems = dict(send_sem=send_sems.at[7 * o + k - 1], recv_sem=recv_sems.at[7 * o + k - 1],
                        device_id=peer, device_id_type=MESH)
            slot = _slot(land, p_idx if arriving else me_idx, o in mid)
            copies.append(pltpu.make_async_remote_copy(src_ref=mine, dst_ref=slot, **sems))
    return copies


_HBM = pl.BlockSpec(memory_space=pltpu.HBM)
_SEM = pl.BlockSpec(memory_space=pltpu.SEMAPHORE)
_EFFECT = pltpu.SideEffectType.DATAFLOW_SIDE_EFFECTING


def _own_slot_copies(scatter, srcs, lands, local_sems, mid=()):
    me_idx = 4 * lax.axis_index("x") + 2 * lax.axis_index("y") + lax.axis_index("c")
    return [pltpu.make_async_copy(src.at[me_idx] if scatter else src, _slot(land, me_idx, o in mid), local_sems.at[o])
            for o, (src, land) in enumerate(zip(srcs, lands))]


def _exchange_start(name, scatter, arrays, after=(), mid=()):
    n = len(arrays)
    lands = [lax.empty(a.shape if scatter else
                       ((a.shape[0], N_DEV * a.shape[1]) if o in mid else (N_DEV,) + a.shape), a.dtype)
             for o, a in enumerate(arrays)]

    def body(*refs):
        srcs, lnds = refs[:n], refs[n:2 * n]
        send_sems, recv_sems, local_sems = refs[2 * n + len(after):2 * n + len(after) + 3]
        token = refs[-1]
        for cp in _split_copies(scatter, srcs, lnds, send_sems, recv_sems, False, mid):
            cp.start()
        for cp in _own_slot_copies(scatter, srcs, lnds, local_sems, mid):
            cp.start()
        token[...] = jnp.zeros_like(token)

    hbm_in = [pltpu.with_memory_space_constraint(a, pltpu.HBM) for a in list(arrays) + lands]
    res = pl.pallas_call(
        body, name=name,
        out_shape=(pltpu.SemaphoreType.DMA((7 * n,)), pltpu.SemaphoreType.DMA((7 * n,)), pltpu.SemaphoreType.DMA((n,)),
                   *[pltpu.HBM(a.shape, a.dtype) for a in hbm_in], jax.ShapeDtypeStruct((8, 128), F32)),
        in_specs=[_HBM] * (2 * n) + [pl.BlockSpec(memory_space=pl.ANY)] * len(after),
        out_specs=(_SEM, _SEM, _SEM, *[_HBM] * (2 * n), pl.BlockSpec(memory_space=pltpu.VMEM)),
        input_output_aliases={i: 3 + i for i in range(2 * n)},
        compiler_params=pltpu.CompilerParams(has_side_effects=_EFFECT),
    )(*hbm_in, *after)
    return res[:3], tuple(mid), list(res[3:3 + n]), list(res[3 + n:3 + 2 * n]), res[-1]


def _exchange_wait(name, scatter, started, after):
    sems, mid, srcs, lands, _ = started
    n = len(srcs)

    def body(*refs):
        src_refs, lnd_refs = refs[:n], refs[n:2 * n]
        send_sems, recv_sems, local_sems = refs[2 * n:2 * n + 3]
        for cp in _split_copies(scatter, src_refs, lnd_refs, send_sems, recv_sems, False, mid):
            cp.wait_send()
        for cp in _split_copies(scatter, src_refs, lnd_refs, send_sems, recv_sems, True, mid):
            cp.wait_recv()
        for cp in _own_slot_copies(scatter, src_refs, lnd_refs, local_sems, mid):
            cp.wait()

    res = pl.pallas_call(
        body, name=name, out_shape=tuple(pltpu.HBM(a.shape, a.dtype) for a in srcs + lands),
        in_specs=[_HBM] * (2 * n) + [_SEM, _SEM, _SEM, pl.BlockSpec(memory_space=pl.ANY)],
        out_specs=tuple([_HBM] * (2 * n)), input_output_aliases={i: i for i in range(2 * n)},
        compiler_params=pltpu.CompilerParams(has_side_effects=_EFFECT),
    )(*srcs, *lands, *sems, after)
    return list(res[n:])


def _adamw(name, parts, w, m, v, prev, layer, *, tr=256):
    _, a, b = w.shape
    tr = _tile(a, tr, 8)
    c1 = 1.0 - ADAM_B1 ** ADAM_STEP
    c2 = 1.0 - ADAM_B2 ** ADAM_STEP

    def body(p_ref, w_ref, m_ref, v_ref, _g, _d, _m, _v, g_out, d_out, m_out, v_out):
        g = p_ref[0].astype(F32)
        for s in range(1, N_DEV):
            g = g + p_ref[s].astype(F32)
        mn = ADAM_B1 * m_ref[...] + (1.0 - ADAM_B1) * g
        vn = ADAM_B2 * v_ref[...] + (1.0 - ADAM_B2) * (g * g)
        g_out[...] = g
        m_out[...] = mn
        v_out[...] = vn
        d_out[...] = -ADAM_LR * ((mn / c1) / (jnp.sqrt(vn / c2) + ADAM_EPS) + ADAM_WD * w_ref[...])

    slab = pl.BlockSpec((None, tr, b), lambda i: (layer, i, 0))
    whole = pl.BlockSpec(memory_space=pl.ANY)
    return pl.pallas_call(
        body, name=name, grid=(a // tr,),
        in_specs=[pl.BlockSpec((N_DEV, tr, b), lambda i: (0, i, 0)), slab, slab, slab] + [whole] * 4,
        out_specs=[slab] * 4, out_shape=[jax.ShapeDtypeStruct(w.shape, F32)] * 4,
        input_output_aliases={4: 0, 5: 1, 6: 2, 7: 3},
        compiler_params=_params(1))(parts, w, m, v, *prev)


_COL = ("w_in", "w_pool_o", "w_mem_o", "w_ff1")
_DW_SHARDED = ("w_in", "w_ff1")
_BIG =("w_in", "w_ret_o", "w_pool_o", "w_mem_kv", "w_mem_o", "w_out", "w_ff1", "w_ff2")
_SMALL = ("ret_decay_logit", "w_pool_grp", "pool_scale", "norm1_g", "norm2_g", "mem_norm_g", "final_norm_g")
_SMALL_MM = ("w_pool_grp",)
_SMALL_F32 = tuple(n for n in _SMALL if n not in _SMALL_MM)
_WEIGHTS = ("w_in", "ret_decay_logit", "w_ret_o", "w_pool_grp", "pool_scale", "w_pool_o", "w_mem_kv", "w_mem_o",
            "w_out", "w_ff1", "w_ff2", "norm1_g", "norm2_g", "mem_norm_g", "final_norm_g")


def _small_rows(size, d):
    return -(-size // (8 * d)) * 8


def _pack_small(ws, d, names):
    parts = []
    for n in names:
        flat = ws[n].reshape(-1)
        rows = _small_rows(flat.shape[0], d)
        parts.append(jnp.pad(flat, (0, rows * d - flat.shape[0])).reshape(rows, d))
    return jnp.concatenate(parts, axis=0)[None]


def _unpack_small(packed, like, d, names):
    out, off = {}, 0
    for n in names:
        rows = _small_rows(like[n].size, d)
        out[n] = packed[0, off:off + rows].reshape(-1)[:like[n].size].reshape(like[n].shape)
        off += rows
    return out


def kernel(x, mem, w_in, ret_decay_logit, w_ret_o, w_pool_grp, pool_scale, w_pool_o, w_mem_kv, w_mem_o, w_out, w_ff1, w_ff2, norm1_g, norm2_g, mem_norm_g, final_norm_g, loss_target, m_w_in, m_ret_decay_logit, m_w_ret_o, m_w_pool_grp, m_pool_scale, m_w_pool_o, m_w_mem_kv, m_w_mem_o, m_w_out, m_w_ff1, m_w_ff2, m_norm1_g, m_norm2_g, m_mem_norm_g, m_final_norm_g, v_w_in, v_ret_decay_logit, v_w_ret_o, v_w_pool_grp, v_pool_scale, v_w_pool_o, v_w_mem_kv, v_w_mem_o, v_w_out, v_w_ff1, v_w_ff2, v_norm1_g, v_norm2_g, v_mem_norm_g, v_final_norm_g):
    w = dict(w_in=w_in, ret_decay_logit=ret_decay_logit, w_ret_o=w_ret_o, w_pool_grp=w_pool_grp,
             pool_scale=pool_scale, w_pool_o=w_pool_o, w_mem_kv=w_mem_kv, w_mem_o=w_mem_o, w_out=w_out,
             w_ff1=w_ff1, w_ff2=w_ff2, norm1_g=norm1_g, norm2_g=norm2_g, mem_norm_g=mem_norm_g,
             final_norm_g=final_norm_g)
    mom = dict(w_in=m_w_in, ret_decay_logit=m_ret_decay_logit, w_ret_o=m_w_ret_o, w_pool_grp=m_w_pool_grp,
               pool_scale=m_pool_scale, w_pool_o=m_w_pool_o, w_mem_kv=m_w_mem_kv, w_mem_o=m_w_mem_o,
               w_out=m_w_out, w_ff1=m_w_ff1, w_ff2=m_w_ff2, norm1_g=m_norm1_g, norm2_g=m_norm2_g,
               mem_norm_g=m_mem_norm_g, final_norm_g=m_final_norm_g)
    vel = dict(w_in=v_w_in, ret_decay_logit=v_ret_decay_logit, w_ret_o=v_w_ret_o, w_pool_grp=v_w_pool_grp,
               pool_scale=v_pool_scale, w_pool_o=v_w_pool_o, w_mem_kv=v_w_mem_kv, w_mem_o=v_w_mem_o,
               w_out=v_w_out, w_ff1=v_w_ff1, w_ff2=v_w_ff2, norm1_g=v_norm1_g, norm2_g=v_norm2_g,
               mem_norm_g=v_mem_norm_g, final_norm_g=v_final_norm_g)

    bl, seq, d = x.shape
    mlen = mem.shape[1]
    depth = w_in.shape[0]
    t = bl * seq
    dk = d // 8

    def natural(n, g, side_by_side=False):
        if side_by_side:
            return g
        if n in _DW_SHARDED:
            return _columns_side_by_side("relayout_" + n, g)
        if n in _COL:
            return jnp.transpose(g, (1, 0, 2)).reshape(g.shape[1], -1)
        return g.reshape(-1, g.shape[-1])

    def finish_gather(name, names, started, after):
        got = _exchange_wait(name, False, started, after)
        return {n: natural(n, g, o in started[1]) for o, (n, g) in enumerate(zip(names, got))}

    shards = [{n: w[n][l].astype(MM) for n in _BIG} for l in range(depth)]
    rest = _BIG[1:]
    (w_in0,) = _all_gather("gather_w_in", [shards[0][_BIG[0]]])
    full = [{_BIG[0]: natural(_BIG[0], w_in0)}]
    def start_layer(l, after):
        s_in = _exchange_start(f"gather_start_in{l}", False, [shards[l][_BIG[0]]], after=after, mid=(0,))
        s_rest = _exchange_start(f"gather_start_{l}", False, [shards[l][n] for n in rest], after=[s_in[4]],
                                 mid=rest_mid)
        return s_in, s_rest, (s_in[4], s_rest[4])

    rest_mid = tuple(o for o, n in enumerate(rest) if n in _DW_SHARDED)
    pending = (None, _exchange_start("gather_start_0", False, [shards[0][n] for n in rest], after=[w_in0],
                                     mid=rest_mid))
    first_tokens = (pending[1][4],)
    pending_next = None
    if depth > 1:
        pending_next = start_layer(1, [pending[1][4]])
        first_tokens += pending_next[2]

    inv = ROPE_BASE ** (-jnp.arange(0, dk, 2, dtype=F32) / dk)
    ang = jnp.arange(seq, dtype=F32)[:, None] * inv[None, :]
    cos2 = jnp.concatenate([jnp.cos(ang), jnp.cos(ang)], axis=-1)
    sin2 = jnp.concatenate([-jnp.sin(ang), jnp.sin(ang)], axis=-1)
    log_g = jax.nn.log_sigmoid(ret_decay_logit)
    x2 = x.reshape(t, d)
    mem2 = mem.reshape(bl * mlen, d)
    gmem = mem_norm_g.reshape(1, d)

    def merge(a_r, y_p, o_a, g_r, g_p, g_m, w_r, w_p, w_m):
        f = lambda z: z.astype(F32)
        o_r, o_p, o_m = _dot(a_r, w_r), _dot(y_p, w_p), _dot(o_a, w_m)
        return _sigmoid(f(g_r)) * o_r + _sigmoid(f(g_p)) * o_p + _sigmoid(f(g_m)) * o_m, o_r, o_p, o_m

    def relu2(u):
        r = jnp.maximum(u.astype(MM), 0.0)
        return r * r

    def ident(a):
        return a

    saved = []
    xc = x2
    for l in range(depth):
        s = dict(x_in=xc)
        started_now = ()
        if l > 0:
            pending, pending_next = pending_next, None
            full.append(finish_gather(f"gather_wait_in{l}", _BIG[:1], pending[0], xc))
            if l + 1 < depth:
                pending_next = start_layer(l + 1, [full[l]["w_in"]])
                started_now = pending_next[2]
        fw = full[l]
        g1 = norm1_g[l].reshape(1, d)
        g2 = norm2_g[l].reshape(1, d)
        s["proj"], s["h1"] = _pmm("proj", _rms_prologue, [(xc, d, 0)], [g1], fw["w_in"],
                                  tm=2048, tn=1024, save_a=True, out_dtypes=(MM,),
                                  after=started_now if l > 0 else first_tokens)
        proj = s["proj"]
        s["o_raw"], s["a_ret"], s["sf"], s["sb"], s["qr"], s["kr"] = _ret_core_fwd(proj, cos2, sin2, log_g[l],
                                                                                   d, bl, seq)
        s["y"] = _pool_fwd(proj, w_pool_grp[l], pool_scale[l].reshape(1, -1), d, bl, seq)
        fw.update(finish_gather(f"gather_wait_{l}", rest, pending[1], s["a_ret"]))
        s["kv"], s["memn"] = _pmm("mem_kv", _rms_prologue, [(mem2, d, 0)], [gmem], fw["w_mem_kv"],
                                  tm=512, tn=512, save_a=True)
        s["o_att"] = _attn_fwd(proj, s["kv"], d, bl, seq, mlen)
        s["x_mid"], s["merged"], s["o_ret"], s["o_pool"], s["o_mem"] = _pmm(
            "merge_out", merge,
            [(s["a_ret"], d, 0), (s["y"], d // 2, 0), (s["o_att"], d // 2, 0), (proj, d, 4), (proj, d, 5), (proj, d, 6)],
            [fw["w_ret_o"], fw["w_pool_o"], fw["w_mem_o"]], fw["w_out"], tm=512, tn=1024, residual=xc, save_a=True,
            extra_outs=[(d, MM)] * 3)
        s["u"], s["h2"] = _pmm("ff1", _rms_prologue, [(s["x_mid"], d, 0)], [g2], fw["w_ff1"],
                               tm=512, tn=4096, save_a=True, out_dtypes=(MM,))
        (xc,) = _pmm("ff2", relu2, [(s["u"], s["u"].shape[1], 0)], [], fw["w_ff2"],
                     tm=512, tn=1024, residual=s["x_mid"])
        saved.append(s)

    dxc, g_final, loss_part = _loss_head(xc, loss_target.reshape(t, d), final_norm_g.reshape(1, d))
    loss = lax.psum(loss_part[0, 0], ("x", "y", "c"))

    small_names = ("w_pool_grp", "pool_scale", "norm1_g", "norm2_g", "ret_decay_logit")
    grads = {n: [None] * depth for n in small_names}
    group_a = ("w_ff1", "w_ff2")
    group_b = tuple(n for n in _BIG if n not in group_a)
    scatters = {}
    dmemn = jnp.zeros((bl * mlen, d), F32)

    def relu2_bwd(acc, u):
        return (acc * (2.0 * jnp.maximum(u.astype(F32), 0.0)),)

    def gates_bwd(acc, g_r, g_p, g_m, o_r, o_p, o_m, w_r, w_p, w_m):
        d_os, d_gs, backs = [], [], []
        for gz, oz, wz in ((g_r, o_r, w_r), (g_p, o_p, w_p), (g_m, o_m, w_m)):
            sg = _sigmoid(gz.astype(F32))
            d_o = (acc * sg).astype(MM)
            d_os.append(d_o)
            d_gs.append(acc * oz.astype(F32) * (sg * (1.0 - sg)))
            backs.append(_dot_nt(d_o, wz))
        return tuple(d_os + d_gs + backs)

    def to_send(n, g):
        a, b = w[n].shape[1:]
        if n in _DW_SHARDED:
            return g
        if n in _COL:
            return jnp.transpose(g.reshape(a, N_DEV, b), (1, 0, 2))
        return g.reshape(N_DEV, a, b)

    for l in reversed(range(depth)):
        s = saved[l]
        fw = full[l]
        proj = s["proj"]
        g1 = norm1_g[l].reshape(1, d)
        g2 = norm2_g[l].reshape(1, d)
        dw = {}
        (du,) = _pmm("ff2_bwd", ident, [(dxc, d, 0)], [], fw["w_ff2"], w_mode="nt", tm=512, tn=4096,
                     epilogue=relu2_bwd, epi_ins=[(s["u"], 0)], out_dtypes=(MM,))
        dw["w_ff2"] = _tnmm("dw_ff2", s["u"], dxc, a_fn=relu2)
        dw["w_ff1"] = _tnmm("dw_ff1", s["h2"], du, col_shards=True)
        scatters[l, "a"] = _exchange_start(f"scatter_start_a{l}", True, [to_send(n, dw[n]) for n in group_a])
        dmid, grads["norm2_g"][l] = _mm_rms_bwd("ff1_norm2_bwd", du, fw["w_ff1"], s["x_mid"], g2, dxc, tm=512)
        d_oret, d_opool, d_omem, dgr, dgp, dgm, da_ret, dy, do_att = _pmm(
            "out_bwd", ident, [(dmid, d, 0)], [], fw["w_out"], w_mode="nt", tm=256, tn=d, epilogue=gates_bwd,
            epi_ins=[(proj, 4 * d), (proj, 5 * d), (proj, 6 * d), (s["o_ret"], 0), (s["o_pool"], 0), (s["o_mem"], 0)],
            epi_full=[fw["w_ret_o"], fw["w_pool_o"], fw["w_mem_o"]], out_dtypes=(MM,) * 7 + (F32,) * 2,
            out_widths=[d] * 7 + [d // 2] * 2, after=(scatters[l, "a"][4],))
        dw["w_out"] = _tnmm("dw_out", s["merged"], dmid)
        dw["w_ret_o"] = _tnmm("dw_ret_o", s["a_ret"], d_oret)
        dw["w_pool_o"] = _tnmm("dw_pool_o", s["y"], d_opool)
        dw["w_mem_o"] = _tnmm("dw_mem_o", s["o_att"], d_omem)
        dg_ret, do_ret = _ret_post_bwd(da_ret, proj, s["o_raw"], d)
        dq, dkk, dvv, dlf, dlb = _ret_core_bwd(s["qr"], s["kr"], proj, do_ret, s["sf"], s["sb"], cos2, sin2,
                                               log_g[l], d, bl, seq)
        dl = jnp.stack([dlf[:, 0, 0].reshape(bl, HEADS).sum(0), dlb[:, 0, 0].reshape(bl, HEADS).sum(0)])
        grads["ret_decay_logit"][l] = dl * jax.nn.sigmoid(-ret_decay_logit[l])
        dp, grads["w_pool_grp"][l], dscale = _pool_bwd(proj, dy, w_pool_grp[l], pool_scale[l].reshape(1, -1),
                                                       d, bl, seq)
        grads["pool_scale"][l] = dscale.reshape(-1)
        dqm, dmk, dmv = _attn_bwd(proj, s["kv"], do_att, d, bl, seq, mlen)
        dkv = jnp.concatenate([dmk, dmv], axis=-1).astype(MM)
        dw["w_mem_kv"] = _tnmm("dw_mem_kv", s["memn"], dkv)
        (dmemn,) = _pmm("mem_kv_bwd", None, [(dkv, d, 0)], [], fw["w_mem_kv"], w_mode="nt", tm=512, tn=512,
                        residual=dmemn)
        dproj = [dq, dkk, dvv, dg_ret, dp, dqm, dgr, dgp, dgm]
        dw["w_in"] = _tnmm("dw_in", s["h1"], dproj, col_shards=True, tm=512, tk=512)
        scatters[l, "b"] = _exchange_start(f"scatter_start_b{l}", True, [to_send(n, dw[n]) for n in group_b])
        dxc, grads["norm1_g"][l] = _mm_rms_bwd("proj_norm1_bwd", dproj, fw["w_in"], s["x_in"], g1, dmid, tm=256,
                                               after=(scatters[l, "b"][4],))

    _, g_memn = _rms_bwd("mem_norm_bwd", dmemn, mem2, gmem, None)
    grad_x = dxc.reshape(bl, seq, d)

    small_g = dict(ret_decay_logit=jnp.stack(grads["ret_decay_logit"]), w_pool_grp=jnp.stack(grads["w_pool_grp"]),
                   pool_scale=jnp.stack(grads["pool_scale"]),
                   norm1_g=jnp.concatenate(grads["norm1_g"], axis=0), norm2_g=jnp.concatenate(grads["norm2_g"], axis=0),
                   mem_norm_g=g_memn.reshape(-1), final_norm_g=g_final.reshape(-1))
    small_started = _exchange_start("gather_small_start", False,
                                    [_pack_small(small_g, d, _SMALL_MM)[0].astype(MM),
                                     _pack_small(small_g, d, _SMALL_F32)[0]])

    big = {n: [lax.empty(w[n].shape, F32) for _ in range(4)] for n in _BIG}

    def update(l, grp, names, after):
        recv = _exchange_wait(f"scatter_wait_{grp}{l}", True, scatters[l, grp], after)
        for n, parts in zip(names, recv):
            big[n] = _adamw("adamw_" + n, parts, w[n], mom[n], vel[n], big[n], l)
        return big[names[-1]][0]

    after = dxc
    for l in reversed(range(1, depth)):
        for grp, names in (("a", group_a), ("b", group_b)):
            after = update(l, grp, names, after)
    after = update(0, "a", group_a, after)
    small_lands = _exchange_wait("gather_small_wait", False, small_started, after)
    small = [{} for _ in range(4)]
    for names, parts in zip((_SMALL_MM, _SMALL_F32), small_lands):
        w_small = _pack_small(w, d, names)
        res = _adamw("adamw_small", parts, w_small, _pack_small(mom, d, names), _pack_small(vel, d, names),
                     [lax.empty(w_small.shape, F32) for _ in range(4)], 0)
        after = res[0]
        for k in range(4):
            small[k].update(_unpack_small(res[k], w, d, names))
    update(0, "b", group_b, after)

    outs = [loss, grad_x]
    for k in range(4):
        outs += [big[n][k] if n in _BIG else small[k][n] for n in _WEIGHTS]
    return tuple(outs)
```

```python
import jax
import jax.numpy as jnp
from jax import lax
from jax.experimental import pallas as pl
from jax.experimental.pallas import tpu as pltpu

F32 = jnp.float32
MM = jnp.bfloat16
N_DEV = 8
HEADS = 4
POOL_WINDOWS = (2, 4, 8, 16)
RET_CHUNK = 256
EPS = 1e-6
ROPE_BASE = 10000.0
ADAM_LR, ADAM_B1, ADAM_B2, ADAM_EPS, ADAM_WD, ADAM_STEP = 0.001, 0.9, 0.999, 1e-08, 0.01, 10
V7X_VMEM_LIMIT = 56 * 1024 * 1024
MESH = pl.DeviceIdType.MESH


def _params(n_axes):
    return pltpu.CompilerParams(dimension_semantics=("arbitrary",) * n_axes,
                                vmem_limit_bytes=V7X_VMEM_LIMIT)


def _tile(n, pref, align=128):
    cands = [c for c in range(align, min(pref, n) + 1, align) if n % c == 0]
    return max(cands) if cands else n


def _sigmoid(z):
    return 0.5 * jnp.tanh(0.5 * z) + 0.5


def _dot(a, b):
    return jnp.dot(a, b, preferred_element_type=F32)


def _dot_nt(a, b):
    return lax.dot_general(a, b, (((1,), (1,)), ((), ())), preferred_element_type=F32)


def _dot_tn(a, b):
    return lax.dot_general(a, b, (((0,), (0,)), ((), ())), preferred_element_type=F32)


def _pmm(name, prologue, row_ins, vec_ins, w, *, tm, tn, w_mode="nn", residual=None, save_a=False,
         epilogue=None, epi_ins=(), out_dtypes=(F32,), after=(), extra_outs=(), epi_full=(), out_widths=None):
    m = row_ins[0][0].shape[0]
    if w_mode == "nn":
        k, n = w.shape
        tn = _tile(n, tn)
        w_spec = pl.BlockSpec((k, tn), lambda i, j: (0, j))
    else:
        n, k = w.shape
        tn = _tile(n, tn)
        w_spec = pl.BlockSpec((tn, k), lambda i, j: (j, 0))
    tm = _tile(m, tm, 8)
    n_row, n_vec, n_epi, n_out = len(row_ins), len(vec_ins), len(epi_ins), len(out_dtypes)
    has_res = residual is not None
    use_scr = prologue is not None
    out_widths = [n] * n_out if out_widths is None else list(out_widths)
    assert all(wd == n for wd in out_widths) or tn == n

    def body(*refs):
        row_refs = refs[:n_row]
        p = n_row
        vec_refs = refs[p:p + n_vec]
        p += n_vec
        w_ref = refs[p]
        p += 1
        res_ref = refs[p] if has_res else None
        p += int(has_res)
        epi_refs = refs[p:p + n_epi + len(epi_full)]
        p += n_epi + len(epi_full) + len(after)
        out_refs = refs[p:p + n_out]
        p += n_out
        a_out = refs[p] if save_a else None
        p += int(save_a)
        extra_refs = refs[p:p + len(extra_outs)]
        p += len(extra_outs)
        if use_scr:
            a_src = refs[p]

            @pl.when(pl.program_id(1) == 0)
            def _():
                made = prologue(*[r[...] for r in row_refs], *[v[...] for v in vec_refs])
                made = made if isinstance(made, tuple) else (made,)
                a = made[0].astype(MM)
                a_src[...] = a
                if save_a:
                    a_out[...] = a
                for e_ref, e in zip(extra_refs, made[1:]):
                    e_ref[...] = e.astype(e_ref.dtype)
        else:
            a_src = row_refs[0]
        if w_mode == "nt":
            acc = _dot_nt(a_src[...], w_ref[...])
        else:
            acc = _dot(a_src[...], w_ref[...])
        if has_res:
            acc = acc + res_ref[...]
        outs = epilogue(acc, *[e[...] for e in epi_refs]) if epilogue is not None else (acc,)
        for o_ref, o in zip(out_refs, outs):
            o_ref[...] = o.astype(o_ref.dtype)

    in_specs = [pl.BlockSpec((tm, wd), lambda i, j, cb=cb: (i, cb)) for (_, wd, cb) in row_ins]
    in_specs += [pl.BlockSpec(v.shape, lambda i, j: (0, 0)) for v in vec_ins]
    in_specs += [w_spec]
    args = [r[0] for r in row_ins] + list(vec_ins) + [w]
    if has_res:
        in_specs.append(pl.BlockSpec((tm, tn), lambda i, j: (i, j)))
        args.append(residual)
    for (arr, off) in epi_ins:
        assert off % tn == 0
        in_specs.append(pl.BlockSpec((tm, tn), lambda i, j, ob=off // tn: (i, ob + j)))
        args.append(arr)
    in_specs += [pl.BlockSpec(v.shape, lambda i, j: (0, 0)) for v in epi_full]
    args += list(epi_full)
    n_after = len(after)
    in_specs += [pl.BlockSpec(memory_space=pl.ANY)] * n_after
    args += list(after)
    out_specs = [pl.BlockSpec((tm, tn if wd == n else wd), lambda i, j: (i, j)) for wd in out_widths]
    out_shape = [jax.ShapeDtypeStruct((m, wd), dt) for wd, dt in zip(out_widths, out_dtypes)]
    if save_a:
        out_specs.append(pl.BlockSpec((tm, k), lambda i, j: (i, 0)))
        out_shape.append(jax.ShapeDtypeStruct((m, k), MM))
    for wd, dt in extra_outs:
        out_specs.append(pl.BlockSpec((tm, wd), lambda i, j: (i, 0)))
        out_shape.append(jax.ShapeDtypeStruct((m, wd), dt))
    scratch = [pltpu.VMEM((tm, k), MM)] if use_scr else []
    return pl.pallas_call(body, name=name, grid=(m // tm, n // tn), in_specs=in_specs,
                          out_specs=out_specs, out_shape=out_shape, scratch_shapes=scratch,
                          compiler_params=_params(2))(*args)


def _tnmm(name, a, b, *, tm=1024, tn=1024, tk=1024, col_shards=False, a_fn=None):
    t, m = a.shape
    pieces = list(b) if isinstance(b, (list, tuple)) else [b]
    widths = [p.shape[1] for p in pieces]
    offs = [sum(widths[:p]) for p in range(len(pieces))]
    n = sum(widths)
    tm, tk = _tile(m, tm), _tile(t, tk, 8)
    per_tile = 1
    if col_shards:
        wb = n // N_DEV
        if len(pieces) > 1:
            tn = n
        while 2 * per_tile * wb <= tn and 2 * per_tile <= N_DEV:
            per_tile *= 2
        tn = per_tile * wb
        out_spec = pl.BlockSpec((per_tile, tm, wb), lambda i, j, kk: (j, i, 0))
        out_shape = jax.ShapeDtypeStruct((N_DEV, m, wb), MM)
    else:
        tn = _tile(n, tn)
        out_spec = pl.BlockSpec((tm, tn), lambda i, j, kk: (i, j))
        out_shape = jax.ShapeDtypeStruct((m, n), MM)
    nk = t // tk

    assert len(pieces) == 1 or tn == n

    def body(a_ref, *rest):
        b_refs, (o_ref, acc) = rest[:len(pieces)], rest[len(pieces):]
        kk = pl.program_id(2)

        @pl.when(kk == 0)
        def _():
            acc[...] = jnp.zeros_like(acc)

        av = (a_ref[...] if a_fn is None else a_fn(a_ref[...])).astype(MM)
        if len(pieces) == 1:
            acc[...] += _dot_tn(av, b_refs[0][...].astype(MM))
        else:
            for b_ref, off, wd in zip(b_refs, offs, widths):
                acc[:, off:off + wd] += _dot_tn(av, b_ref[...].astype(MM))

        @pl.when(kk == nk - 1)
        def _():
            if col_shards:
                for sh in range(per_tile):
                    o_ref[sh] = acc[:, sh * wb:(sh + 1) * wb].astype(o_ref.dtype)
            else:
                o_ref[...] = acc[...].astype(o_ref.dtype)

    return pl.pallas_call(
        body, name=name, grid=(m // tm, n // tn, nk),
        in_specs=[pl.BlockSpec((tk, tm), lambda i, j, kk: (kk, i))]
        + [pl.BlockSpec((tk, tn if len(pieces) == 1 else wd), lambda i, j, kk: (kk, j)) for wd in widths],
        out_specs=out_spec, out_shape=out_shape,
        scratch_shapes=[pltpu.VMEM((tm, tn), F32)],
        compiler_params=_params(3))(a, *pieces)


def _rms_prologue(x, g):
    r = lax.rsqrt(jnp.mean(x * x, axis=-1, keepdims=True) + EPS)
    return x * r * g


def _rms_bwd_rows(dh, x, g):
    d = x.shape[-1]
    r = lax.rsqrt(jnp.mean(x * x, axis=-1, keepdims=True) + EPS)
    xh = x * r
    dxh = dh * g
    dx = r * (dxh - xh * (jnp.sum(dxh * xh, axis=-1, keepdims=True) / d))
    dg = jnp.sum(dh * xh, axis=0, keepdims=True)
    return dx, dg


def _rms_bwd(name, dh, x, g, dres, *, tm=512):
    m, d = x.shape
    tm = min(tm, m)
    has_res = dres is not None

    def body(*refs):
        if has_res:
            dh_ref, x_ref, g_ref, r_ref, dx_ref, dg_ref = refs
        else:
            dh_ref, x_ref, g_ref, dx_ref, dg_ref = refs
        dx, dg = _rms_bwd_rows(dh_ref[...], x_ref[...], g_ref[...])
        if has_res:
            dx = dx + r_ref[...]
        dx_ref[...] = dx

        @pl.when(pl.program_id(0) == 0)
        def _():
            dg_ref[...] = jnp.zeros_like(dg_ref)

        dg_ref[...] += dg

    row = pl.BlockSpec((tm, d), lambda i: (i, 0))
    vec = pl.BlockSpec((1, d), lambda i: (0, 0))
    in_specs = [row, row, vec] + ([row] if has_res else [])
    args = [dh, x, g] + ([dres] if has_res else [])
    return pl.pallas_call(body, name=name, grid=(m // tm,), in_specs=in_specs, out_specs=[row, vec],
                          out_shape=[jax.ShapeDtypeStruct((m, d), F32), jax.ShapeDtypeStruct((1, d), F32)],
                          compiler_params=_params(1))(*args)


def _mm_rms_bwd(name, a, w, x, g, dres, *, tm, after=()):
    pieces = list(a) if isinstance(a, (list, tuple)) else [a]
    widths = [p.shape[1] for p in pieces]
    m = pieces[0].shape[0]
    d = w.shape[0]
    tm = _tile(m, tm, 8)
    n_a = len(pieces)

    def body(*refs):
        a_refs = refs[:n_a]
        w_ref, x_ref, g_ref, r_ref = refs[n_a:n_a + 4]
        dx_ref, dg_ref = refs[n_a + 4 + len(after):]

        av = a_refs[0][...] if n_a == 1 else jnp.concatenate([a_ref[...] for a_ref in a_refs], axis=1)
        dh = _dot_nt(av, w_ref[...])
        dx, dg = _rms_bwd_rows(dh, x_ref[...], g_ref[...])
        dx_ref[...] = dx + r_ref[...]

        @pl.when(pl.program_id(0) == 0)
        def _():
            dg_ref[...] = jnp.zeros_like(dg_ref)

        dg_ref[...] += dg

    row = pl.BlockSpec((tm, d), lambda i: (i, 0))
    vec = pl.BlockSpec((1, d), lambda i: (0, 0))
    return pl.pallas_call(
        body, name=name, grid=(m // tm,),
        in_specs=[pl.BlockSpec((tm, wd), lambda i: (i, 0)) for wd in widths]
        + [pl.BlockSpec(w.shape, lambda i: (0, 0)), row, vec, row]
        + [pl.BlockSpec(memory_space=pl.ANY)] * len(after),
        out_specs=[row, vec],
        out_shape=[jax.ShapeDtypeStruct((m, d), F32), jax.ShapeDtypeStruct((1, d), F32)],
        compiler_params=_params(1))(*pieces, w, x, g, dres, *after)


def _loss_head(x, target, g, *, tm=256):
    m, d = x.shape
    tm = min(tm, m)

    def body(x_ref, t_ref, g_ref, dx_ref, dg_ref, loss_ref):
        xv, gv = x_ref[...], g_ref[...]
        y = _rms_prologue(xv, gv)
        err = y - t_ref[...]
        part = 0.5 * jnp.sum(jnp.sum(err * err, axis=-1, keepdims=True) / d)
        dx, dg = _rms_bwd_rows(err / d, xv, gv)
        dx_ref[...] = dx

        @pl.when(pl.program_id(0) == 0)
        def _():
            dg_ref[...] = jnp.zeros_like(dg_ref)
            loss_ref[...] = jnp.zeros_like(loss_ref)

        dg_ref[...] += dg
        loss_ref[...] += jnp.full(loss_ref.shape, part, F32)

    row = pl.BlockSpec((tm, d), lambda i: (i, 0))
    vec = pl.BlockSpec((1, d), lambda i: (0, 0))
    lspec = pl.BlockSpec((1, 128), lambda i: (0, 0))
    return pl.pallas_call(body, name="loss_head", grid=(m // tm,), in_specs=[row, row, vec],
                          out_specs=[row, vec, lspec],
                          out_shape=[jax.ShapeDtypeStruct((m, d), F32), jax.ShapeDtypeStruct((1, d), F32),
                                     jax.ShapeDtypeStruct((1, 128), F32)],
                          compiler_params=_params(1))(x, target, g)


def _rot(xv, cos2, sin2, half):
    return xv * cos2 + pltpu.roll(xv, half, 1) * sin2


def _rot_t(dv, cos2, sin2, half):
    return dv * cos2 + pltpu.roll(dv * sin2, half, 1)


def _ret_consts(lg_ref, h, t, dk):
    lf, lb = lg_ref[0, h], lg_ref[1, h]
    ab = (lax.broadcasted_iota(jnp.int32, (t, t), 0) - lax.broadcasted_iota(jnp.int32, (t, t), 1)).astype(F32)
    dmat = jnp.exp(jnp.where(ab >= 0, lf * ab, -lb * ab))
    up = lax.broadcasted_iota(jnp.int32, (t, dk), 0).astype(F32) + 1.0
    down = float(t) - up
    one = jnp.ones((1, 1), F32)
    return dict(ab=ab, dmat=dmat, xi_f=jnp.exp(lf * up), zeta_f=jnp.exp(lf * down), xi_b=jnp.exp(lb * up),
                zeta_b=jnp.exp(lb * down), up=up[:, 0:1], down=down[:, 0:1],
                cf=jnp.exp(one * (lf * t)), cb=jnp.exp(one * (lb * t)))


def _scaled(xv, rows):
    return (xv.astype(F32) * rows).astype(MM)


def _ret_core_fwd(proj, cos2, sin2, lg, d, bl, seq, *, tc=RET_CHUNK):
    t = proj.shape[0]
    dk, dv = d // 8, d // 4
    tc = min(tc, seq)
    nc = seq // tc
    scale = float(dk) ** -0.5

    def body(lg_ref, qp_ref, kp_ref, v_ref, g_ref, c_ref, s_ref, o_ref, a_ref, sf_ref, sb_ref, q_ref, k_ref, o_acc):
        c = _ret_consts(lg_ref, pl.program_id(1), tc, dk)

        def rows_of(i):
            return pl.ds(pl.multiple_of(i * tc, tc), tc)

        def rotate(i, carry):
            rows = rows_of(i)
            cs, sn = c_ref[rows, :], s_ref[rows, :]
            q_ref[rows, :] = _rot(qp_ref[rows, :].astype(F32), cs, sn, dk // 2).astype(MM)
            k_ref[rows, :] = (_rot(kp_ref[rows, :].astype(F32), cs, sn, dk // 2) * scale).astype(MM)
            return carry

        lax.fori_loop(0, nc, rotate, 0)

        def fwd_step(i, sf):
            rows = rows_of(i)
            sf_ref[i] = sf
            q, kk, v = q_ref[rows, :], k_ref[rows, :], v_ref[rows, :]
            p = (_dot_nt(q, kk) * c["dmat"]).astype(MM)
            o_acc[rows, :] = _dot(p, v) + _dot(_scaled(q, c["xi_f"]), sf.astype(MM))
            return sf * c["cf"] + _dot_tn(_scaled(kk, c["zeta_f"]), v)

        lax.fori_loop(0, nc, fwd_step, jnp.zeros((dk, dv), F32))

        def bwd_step(ii, sb):
            rows = rows_of(nc - 1 - ii)
            sb_ref[nc - 1 - ii] = sb
            q, kk, v = q_ref[rows, :], k_ref[rows, :], v_ref[rows, :]
            o_acc[rows, :] += _dot(_scaled(q, c["zeta_b"]), sb.astype(MM))
            return sb * c["cb"] + _dot_tn(_scaled(kk, c["xi_b"]), v)

        lax.fori_loop(0, nc, bwd_step, jnp.zeros((dk, dv), F32))

        def post(i, carry):
            rows = rows_of(i)
            o = o_acc[rows, :]
            o_ref[rows, :] = o.astype(o_ref.dtype)
            oc = o - jnp.mean(o, axis=-1, keepdims=True)
            on = oc * lax.rsqrt(jnp.mean(oc * oc, axis=-1, keepdims=True) + EPS)
            g = g_ref[rows, :].astype(F32)
            a_ref[rows, :] = (on * (g * _sigmoid(g))).astype(MM)
            return carry

        lax.fori_loop(0, nc, post, 0)

    qk = pl.BlockSpec((seq, dk), lambda b, h: (b, h))
    vv = pl.BlockSpec((seq, dv), lambda b, h: (b, h))
    tab = pl.BlockSpec((seq, dk), lambda b, h: (0, 0))
    states = pl.BlockSpec((None, nc, dk, dv), lambda b, h: (b * HEADS + h, 0, 0, 0))
    return pl.pallas_call(
        body, name="ret_core_fwd", grid=(bl, HEADS),
        in_specs=[pl.BlockSpec(memory_space=pltpu.SMEM), qk, pl.BlockSpec((seq, dk), lambda b, h: (b, HEADS + h)),
                  pl.BlockSpec((seq, dv), lambda b, h: (b, HEADS + h)),
                  pl.BlockSpec((seq, dv), lambda b, h: (b, 2 * HEADS + h)), tab, tab],
        out_specs=[vv, vv, states, states, qk, qk],
        out_shape=[jax.ShapeDtypeStruct((t, d), MM), jax.ShapeDtypeStruct((t, d), MM),
                   jax.ShapeDtypeStruct((bl * HEADS, nc, dk, dv), F32),
                   jax.ShapeDtypeStruct((bl * HEADS, nc, dk, dv), F32),
                   jax.ShapeDtypeStruct((t, d // 2), MM), jax.ShapeDtypeStruct((t, d // 2), MM)],
        scratch_shapes=[pltpu.VMEM((seq, dv), F32)],
        compiler_params=_params(2))(lg, proj, proj, proj, proj, cos2, sin2)


def _ret_post_bwd(da, proj, o_raw, d, *, ts=2048):
    t = da.shape[0]
    dv = d // 4
    ts = min(ts, t)

    def body(da_ref, g_ref, o_ref, dg_ref, do_ref):
        o, g, dav = o_ref[...].astype(F32), g_ref[...].astype(F32), da_ref[...].astype(F32)
        mu = jnp.mean(o, axis=-1, keepdims=True)
        oc = o - mu
        r = lax.rsqrt(jnp.mean(oc * oc, axis=-1, keepdims=True) + EPS)
        on = oc * r
        sg = _sigmoid(g)
        don = dav * (g * sg)
        dg_ref[...] = (dav * on * (sg * (1.0 + g * (1.0 - sg)))).astype(MM)
        do = r * (don - jnp.mean(don, axis=-1, keepdims=True) - on * jnp.mean(don * on, axis=-1, keepdims=True))
        do_ref[...] = do.astype(MM)

    blk = pl.BlockSpec((ts, dv), lambda i, h: (i, h))
    return pl.pallas_call(
        body, name="ret_post_bwd", grid=(t // ts, HEADS),
        in_specs=[blk, pl.BlockSpec((ts, dv), lambda i, h: (i, 2 * HEADS + h)), blk],
        out_specs=[blk, blk],
        out_shape=[jax.ShapeDtypeStruct((t, d), MM), jax.ShapeDtypeStruct((t, d), MM)],
        compiler_params=_params(2))(da, proj, o_raw)


def _ret_core_bwd(qr, kr, proj, do, sf_in, sb_in, cos2, sin2, lg, d, bl, seq, *, tc=RET_CHUNK):
    t = qr.shape[0]
    dk, dv = d // 8, d // 4
    tc = min(tc, seq)
    nc = seq // tc
    scale = float(dk) ** -0.5

    def body(lg_ref, q_ref, k_ref, v_ref, do_ref, sf_all, sb_all, c_ref, s_ref, dq_ref, dk_ref, dv_ref,
             dlf_ref, dlb_ref, dq_acc, dk_acc, dv_acc):
        c = _ret_consts(lg_ref, pl.program_id(1), tc, dk)
        fwd = c["ab"] >= 0
        zero_state = jnp.zeros((dk, dv), F32)
        zero = jnp.zeros((1, 1), F32)

        def rows_of(i):
            return pl.ds(pl.multiple_of(i * tc, tc), tc)

        def total(xv):
            return jnp.sum(xv, keepdims=True)

        def fwd_sweep(i, carry):
            hh, dlf, dlb = carry
            rows = rows_of(i)
            q, kk, v, dov = q_ref[rows, :], k_ref[rows, :], v_ref[rows, :], do_ref[rows, :]
            dof, vf = dov.astype(F32), v.astype(F32)
            p = _dot_nt(q, kk) * c["dmat"]
            da = _dot_nt(dov, v)
            x = p * da * c["ab"]
            dlf = dlf + total(jnp.where(fwd, x, 0.0))
            dlb = dlb - total(jnp.where(fwd, 0.0, x))
            pb, dpb = p.astype(MM), (da * c["dmat"]).astype(MM)
            dq = _dot(dpb, kk)
            dkc = _dot_tn(dpb, q)
            dvc = _dot_tn(pb, dov)
            sf, sb = sf_all[i], sb_all[i]
            sfb, sbb = sf.astype(MM), sb.astype(MM)
            q_xf, q_zb = _scaled(q, c["xi_f"]), _scaled(q, c["zeta_b"])
            dq = dq + _dot_nt(dov, sfb) * c["xi_f"] + _dot_nt(dov, sbb) * c["zeta_b"]
            dlf = dlf + total(jnp.sum(_dot(q_xf, sfb) * dof, axis=-1, keepdims=True) * c["up"])
            dlb = dlb + total(jnp.sum(_dot(q_zb, sbb) * dof, axis=-1, keepdims=True) * c["down"])
            hb = hh.astype(MM)
            dkc = dkc + _dot_nt(v, hb) * c["xi_b"]
            dv_bx = _dot(_scaled(kk, c["xi_b"]), hb)
            dlb = dlb + total(jnp.sum(vf * dv_bx, axis=-1, keepdims=True) * c["up"])
            dlb = dlb + float(tc) * total(hh * (sb * c["cb"]))
            dq_acc[rows, :] = dq
            dk_acc[rows, :] = dkc
            dv_acc[rows, :] = dvc + dv_bx
            return hh * c["cb"] + _dot_tn(q_zb, dov), dlf, dlb

        _, dlf, dlb = lax.fori_loop(0, nc, fwd_sweep, (zero_state, zero, zero))

        def rev_sweep(ii, carry):
            gg, dlf = carry
            i = nc - 1 - ii
            rows = rows_of(i)
            q, kk, v, dov = q_ref[rows, :], k_ref[rows, :], v_ref[rows, :], do_ref[rows, :]
            gb = gg.astype(MM)
            dk_acc[rows, :] += _dot_nt(v, gb) * c["zeta_f"]
            dv_fx = _dot(_scaled(kk, c["zeta_f"]), gb)
            dv_acc[rows, :] += dv_fx
            dlf = dlf + total(jnp.sum(v.astype(F32) * dv_fx, axis=-1, keepdims=True) * c["down"])
            dlf = dlf + float(tc) * total(gg * (sf_all[i] * c["cf"]))
            return gg * c["cf"] + _dot_tn(_scaled(q, c["xi_f"]), dov), dlf

        _, dlf = lax.fori_loop(0, nc, rev_sweep, (zero_state, dlf))

        cs, sn = c_ref[...], s_ref[...]
        dq_ref[...] = _rot_t(dq_acc[...], cs, sn, dk // 2).astype(MM)
        dk_ref[...] = (_rot_t(dk_acc[...], cs, sn, dk // 2) * scale).astype(MM)
        dv_ref[...] = dv_acc[...].astype(MM)
        dlf_ref[...] = jnp.broadcast_to(dlf, dlf_ref.shape)
        dlb_ref[...] = jnp.broadcast_to(dlb, dlb_ref.shape)

    qk = pl.BlockSpec((seq, dk), lambda b, h: (b, h))
    vv = pl.BlockSpec((seq, dv), lambda b, h: (b, h))
    tab = pl.BlockSpec((seq, dk), lambda b, h: (0, 0))
    dl = pl.BlockSpec((None, 8, 128), lambda b, h: (b * HEADS + h, 0, 0))
    states = pl.BlockSpec((None, nc, dk, dv), lambda b, h: (b * HEADS + h, 0, 0, 0))
    return pl.pallas_call(
        body, name="ret_core_bwd", grid=(bl, HEADS),
        in_specs=[pl.BlockSpec(memory_space=pltpu.SMEM), qk, qk, pl.BlockSpec((seq, dv), lambda b, h: (b, HEADS + h)),
                  vv, states, states, tab, tab],
        out_specs=[qk, qk, vv, dl, dl],
        out_shape=[jax.ShapeDtypeStruct((t, d // 2), MM), jax.ShapeDtypeStruct((t, d // 2), MM),
                   jax.ShapeDtypeStruct((t, d), MM),
                   jax.ShapeDtypeStruct((bl * HEADS, 8, 128), F32), jax.ShapeDtypeStruct((bl * HEADS, 8, 128), F32)],
        scratch_shapes=[pltpu.VMEM((seq, dk), F32), pltpu.VMEM((seq, dk), F32), pltpu.VMEM((seq, dv), F32)],
        compiler_params=_params(2))(lg, qr, kr, proj, do, sf_in, sb_in, cos2, sin2)


def _window_count(row, w, seq):
    return (jnp.minimum(row + w // 2, seq) - jnp.maximum(row - w // 2, 0)).astype(F32)


def _window_sum(pv, row, w, seq, sign):
    acc = None
    for j in range(-(w // 2), w // 2):
        if j == 0:
            term = pv
        else:
            src = row + sign * j
            term = jnp.where((src >= 0) & (src < seq), pltpu.roll(pv, (-sign * j) % seq, 0), 0.0)
        acc = term if acc is None else acc + term
    return acc


def _pool_fwd(proj, w_grp, scale, d, bl, seq):
    t = proj.shape[0]
    dg = d // 8

    def body(p_ref, w_ref, s_ref, y_ref):
        row = lax.broadcasted_iota(jnp.int32, (seq, dg), 0)
        for gi, w in enumerate(POOL_WINDOWS):
            sl = slice(gi * dg, (gi + 1) * dg)
            pg = p_ref[:, sl].astype(F32)
            mixed = _window_sum(pg, row, w, seq, 1) / _window_count(row, w, seq) - pg
            yp = _dot(mixed.astype(MM), w_ref[gi].astype(MM))
            y_ref[:, sl] = (yp * s_ref[:, sl]).astype(MM)

    return pl.pallas_call(
        body, name="pool_fwd", grid=(bl,),
        in_specs=[pl.BlockSpec((seq, d // 2), lambda b: (b, 6)),
                  pl.BlockSpec(w_grp.shape, lambda b: (0, 0, 0)),
                  pl.BlockSpec((1, d // 2), lambda b: (0, 0))],
        out_specs=pl.BlockSpec((seq, d // 2), lambda b: (b, 0)),
        out_shape=jax.ShapeDtypeStruct((t, d // 2), MM),
        compiler_params=_params(1))(proj, w_grp, scale)


def _pool_bwd(proj, dy, w_grp, scale, d, bl, seq):
    t = proj.shape[0]
    dg = d // 8

    def body(p_ref, dy_ref, w_ref, s_ref, dp_ref, dw_ref, ds_ref):
        @pl.when(pl.program_id(0) == 0)
        def _():
            dw_ref[...] = jnp.zeros_like(dw_ref)
            ds_ref[...] = jnp.zeros_like(ds_ref)

        row = lax.broadcasted_iota(jnp.int32, (seq, dg), 0)
        for gi, w in enumerate(POOL_WINDOWS):
            sl = slice(gi * dg, (gi + 1) * dg)
            pg = p_ref[:, sl].astype(F32)
            cnt = _window_count(row, w, seq)
            mixb = (_window_sum(pg, row, w, seq, 1) / cnt - pg).astype(MM)
            wgb = w_ref[gi].astype(MM)
            yp = _dot(mixb, wgb)
            dyg = dy_ref[:, sl]
            ds_ref[:, sl] += jnp.sum(dyg * yp, axis=0, keepdims=True)
            dyp = (dyg * s_ref[:, sl]).astype(MM)
            dmixed = _dot_nt(dyp, wgb)
            dw_ref[gi] += _dot_tn(mixb, dyp)
            dp_ref[:, sl] = (_window_sum(dmixed / cnt, row, w, seq, -1) - dmixed).astype(MM)

    half = pl.BlockSpec((seq, d // 2), lambda b: (b, 0))
    wspec = pl.BlockSpec(w_grp.shape, lambda b: (0, 0, 0))
    sspec = pl.BlockSpec((1, d // 2), lambda b: (0, 0))
    return pl.pallas_call(
        body, name="pool_bwd", grid=(bl,),
        in_specs=[pl.BlockSpec((seq, d // 2), lambda b: (b, 6)), half, wspec, sspec],
        out_specs=[half, wspec, sspec],
        out_shape=[jax.ShapeDtypeStruct((t, d // 2), MM), jax.ShapeDtypeStruct(w_grp.shape, F32),
                   jax.ShapeDtypeStruct((1, d // 2), F32)],
        compiler_params=_params(1))(proj, dy, w_grp, scale)


def _attn_probs(q, kk, dh):
    s = _dot_nt(q, kk) * (float(dh) ** -0.5)
    e = jnp.exp(s - jnp.max(s, axis=-1, keepdims=True))
    return e / jnp.sum(e, axis=-1, keepdims=True)


def _attn_fwd(proj, kv, d, bl, seq, mlen, *, tq=2048):
    t = proj.shape[0]
    dh = d // 8
    tq = min(tq, seq)
    nq = seq // tq

    def body(q_ref, k_ref, v_ref, o_ref):
        a = _attn_probs(q_ref[...].astype(MM), k_ref[...].astype(MM), dh)
        o_ref[...] = _dot(a.astype(MM), v_ref[...].astype(MM)).astype(MM)

    return pl.pallas_call(
        body, name="attn_fwd", grid=(bl, HEADS, nq),
        in_specs=[pl.BlockSpec((tq, dh), lambda b, h, i: (b * nq + i, 7 * HEADS + h)),
                  pl.BlockSpec((mlen, dh), lambda b, h, i: (b, h)),
                  pl.BlockSpec((mlen, dh), lambda b, h, i: (b, HEADS + h))],
        out_specs=pl.BlockSpec((tq, dh), lambda b, h, i: (b * nq + i, h)),
        out_shape=jax.ShapeDtypeStruct((t, d // 2), MM),
        compiler_params=_params(3))(proj, kv, kv)


def _attn_bwd(proj, kv, do, d, bl, seq, mlen, *, tq=2048):
    t = proj.shape[0]
    dh = d // 8
    tq = min(tq, seq)
    nq = seq // tq

    def body(q_ref, k_ref, v_ref, do_ref, dq_ref, dk_ref, dv_ref):
        @pl.when(pl.program_id(2) == 0)
        def _():
            dk_ref[...] = jnp.zeros_like(dk_ref)
            dv_ref[...] = jnp.zeros_like(dv_ref)

        q, kk, vv = q_ref[...].astype(MM), k_ref[...].astype(MM), v_ref[...].astype(MM)
        dov = do_ref[...].astype(MM)
        a = _attn_probs(q, kk, dh)
        dp = _dot_nt(dov, vv)
        ds = (a * (dp - jnp.sum(dp * a, axis=-1, keepdims=True)) * (float(dh) ** -0.5)).astype(MM)
        dq_ref[...] = _dot(ds, kk).astype(MM)
        dk_ref[...] += _dot_tn(ds, q)
        dv_ref[...] += _dot_tn(a.astype(MM), dov)

    qs = pl.BlockSpec((tq, dh), lambda b, h, i: (b * nq + i, h))
    ms = pl.BlockSpec((mlen, dh), lambda b, h, i: (b, h))
    return pl.pallas_call(
        body, name="attn_bwd", grid=(bl, HEADS, nq),
        in_specs=[pl.BlockSpec((tq, dh), lambda b, h, i: (b * nq + i, 7 * HEADS + h)), ms,
                  pl.BlockSpec((mlen, dh), lambda b, h, i: (b, HEADS + h)), qs],
        out_specs=[qs, ms, ms],
        out_shape=[jax.ShapeDtypeStruct((t, d // 2), MM), jax.ShapeDtypeStruct((bl * mlen, d // 2), F32),
                   jax.ShapeDtypeStruct((bl * mlen, d // 2), F32)],
        compiler_params=_params(3))(proj, kv, kv, do)


def _comm_call(name, body, arrays, out_shapes):
    n = len(arrays)
    hbm = pl.BlockSpec(memory_space=pl.ANY)
    return pl.pallas_call(
        body, name=name, out_shape=out_shapes, in_specs=[hbm] * n, out_specs=[hbm] * n,
        scratch_shapes=[pltpu.SemaphoreType.DMA((7 * n,)), pltpu.SemaphoreType.DMA((7 * n,)),
                        pltpu.SemaphoreType.DMA((n,))],
    )(*arrays)


def _all_gather(name, shards):
    n = len(shards)

    def body(*refs):
        x_refs, out_refs = refs[:n], refs[n:2 * n]
        send_sems, recv_sems, local_sems = refs[2 * n:]
        x, y, c = lax.axis_index("x"), lax.axis_index("y"), lax.axis_index("c")
        me, sibling = (x, y, c), (x, y, 1 - c)
        chips = [(1 - x, y), (x, 1 - y), (1 - x, 1 - y)]

        def copy(o, k, block, to, src=None):
            slot = out_refs[o].at[4 * block[0] + 2 * block[1] + block[2]]
            return pltpu.make_async_remote_copy(
                src_ref=slot if src is None else src, dst_ref=slot, send_sem=send_sems.at[7 * o + k],
                recv_sem=recv_sems.at[7 * o + k], device_id=to, device_id_type=MESH)

        locals_, remotes = [], []
        for o in range(n):
            mine = pltpu.make_async_copy(x_refs[o], out_refs[o].at[4 * x + 2 * y + c], local_sems.at[o])
            mine.start()
            locals_.append(mine)
            first = [copy(o, 0, me, sibling, src=x_refs[o])]
            first += [copy(o, 1 + j, me, (*chip, c), src=x_refs[o]) for j, chip in enumerate(chips)]
            for cp in first:
                cp.start()
            remotes += first
        for o in range(n):
            for j, chip in enumerate(chips):
                copy(o, 1 + j, (*chip, c), me).wait_recv()
                passed = copy(o, 4 + j, (*chip, c), sibling)
                passed.start()
                remotes.append(passed)
        for o in range(n):
            copy(o, 0, sibling, me).wait_recv()
            for j, chip in enumerate(chips):
                copy(o, 4 + j, (*chip, 1 - c), me).wait_recv()
        for cp in remotes:
            cp.wait_send()
        for mine in locals_:
            mine.wait()

    outs = [jax.ShapeDtypeStruct((N_DEV,) + s.shape, s.dtype) for s in shards]
    return _comm_call(name, body, shards, outs)


def _columns_side_by_side(name, g):
    _, k, wb = g.shape

    def body(x_ref, o_ref):
        o_ref[...] = x_ref[...]

    return pl.pallas_call(
        body, name=name, grid=(N_DEV,), in_specs=[pl.BlockSpec((None, k, wb), lambda j: (j, 0, 0))],
        out_specs=pl.BlockSpec((k, wb), lambda j: (0, j)),
        out_shape=jax.ShapeDtypeStruct((k, N_DEV * wb), g.dtype), compiler_params=_params(1))(g)


def _peer_of(k, x, y, c):
    peer = (1 - x if k & 4 else x, 1 - y if k & 2 else y, 1 - c if k & 1 else c)
    return peer, 4 * peer[0] + 2 * peer[1] + peer[2]


def _slot(land, idx, side_by_side):
    if not side_by_side:
        return land.at[idx]
    b = land.shape[1] // N_DEV
    return land.at[:, pl.ds(pl.multiple_of(idx * b, b), b)]


def _split_copies(scatter, srcs, lands, send_sems, recv_sems, arriving, mid=()):
    x, y, c = lax.axis_index("x"), lax.axis_index("y"), lax.axis_index("c")
    me_idx = 4 * x + 2 * y + c
    copies = []
    for o, (src, land) in enumerate(zip(srcs, lands)):
        for k in range(1, N_DEV):
            peer, p_idx = _peer_of(k, x, y, c)
            mine = src.at[p_idx] if scatter else src
            sems = dict(send_sem=send_sems.at[7 * o + k - 1], recv_sem=recv_sems.at[7 * o + k - 1],
                        device_id=peer, device_id_type=MESH)
            slot = _slot(land, p_idx if arriving else me_idx, o in mid)
            copies.append(pltpu.make_async_remote_copy(src_ref=mine, dst_ref=slot, **sems))
    return copies


_HBM = pl.BlockSpec(memory_space=pltpu.HBM)
_SEM = pl.BlockSpec(memory_space=pltpu.SEMAPHORE)
_EFFECT = pltpu.SideEffectType.DATAFLOW_SIDE_EFFECTING


def _own_slot_copies(scatter, srcs, lands, local_sems, mid=()):
    me_idx = 4 * lax.axis_index("x") + 2 * lax.axis_index("y") + lax.axis_index("c")
    return [pltpu.make_async_copy(src.at[me_idx] if scatter else src, _slot(land, me_idx, o in mid), local_sems.at[o])
            for o, (src, land) in enumerate(zip(srcs, lands))]


def _exchange_start(name, scatter, arrays, after=(), mid=()):
    n = len(arrays)
    lands = [lax.empty(a.shape if scatter else
                       ((a.shape[0], N_DEV * a.shape[1]) if o in mid else (N_DEV,) + a.shape), a.dtype)
             for o, a in enumerate(arrays)]

    def body(*refs):
        srcs, lnds = refs[:n], refs[n:2 * n]
        send_sems, recv_sems, local_sems = refs[2 * n + len(after):2 * n + len(after) + 3]
        token = refs[-1]
        for cp in _split_copies(scatter, srcs, lnds, send_sems, recv_sems, False, mid):
            cp.start()
        for cp in _own_slot_copies(scatter, srcs, lnds, local_sems, mid):
            cp.start()
        token[...] = jnp.zeros_like(token)

    hbm_in = [pltpu.with_memory_space_constraint(a, pltpu.HBM) for a in list(arrays) + lands]
    res = pl.pallas_call(
        body, name=name,
        out_shape=(pltpu.SemaphoreType.DMA((7 * n,)), pltpu.SemaphoreType.DMA((7 * n,)), pltpu.SemaphoreType.DMA((n,)),
                   *[pltpu.HBM(a.shape, a.dtype) for a in hbm_in], jax.ShapeDtypeStruct((8, 128), F32)),
        in_specs=[_HBM] * (2 * n) + [pl.BlockSpec(memory_space=pl.ANY)] * len(after),
        out_specs=(_SEM, _SEM, _SEM, *[_HBM] * (2 * n), pl.BlockSpec(memory_space=pltpu.VMEM)),
        input_output_aliases={i: 3 + i for i in range(2 * n)},
        compiler_params=pltpu.CompilerParams(has_side_effects=_EFFECT),
    )(*hbm_in, *after)
    return res[:3], tuple(mid), list(res[3:3 + n]), list(res[3 + n:3 + 2 * n]), res[-1]


def _exchange_wait(name, scatter, started, after):
    sems, mid, srcs, lands, _ = started
    n = len(srcs)

    def body(*refs):
        src_refs, lnd_refs = refs[:n], refs[n:2 * n]
        send_sems, recv_sems, local_sems = refs[2 * n:2 * n + 3]
        for cp in _split_copies(scatter, src_refs, lnd_refs, send_sems, recv_sems, False, mid):
            cp.wait_send()
        for cp in _split_copies(scatter, src_refs, lnd_refs, send_sems, recv_sems, True, mid):
            cp.wait_recv()
        for cp in _own_slot_copies(scatter, src_refs, lnd_refs, local_sems, mid):
            cp.wait()

    res = pl.pallas_call(
        body, name=name, out_shape=tuple(pltpu.HBM(a.shape, a.dtype) for a in srcs + lands),
        in_specs=[_HBM] * (2 * n) + [_SEM, _SEM, _SEM, pl.BlockSpec(memory_space=pl.ANY)],
        out_specs=tuple([_HBM] * (2 * n)), input_output_aliases={i: i for i in range(2 * n)},
        compiler_params=pltpu.CompilerParams(has_side_effects=_EFFECT),
    )(*srcs, *lands, *sems, after)
    return list(res[n:])


def _adamw(name, parts, w, m, v, prev, layer, *, tr=256):
    _, a, b = w.shape
    tr = _tile(a, tr, 8)
    c1 = 1.0 - ADAM_B1 ** ADAM_STEP
    c2 = 1.0 - ADAM_B2 ** ADAM_STEP

    def body(p_ref, w_ref, m_ref, v_ref, _g, _d, _m, _v, g_out, d_out, m_out, v_out):
        g = p_ref[0].astype(F32)
        for s in range(1, N_DEV):
            g = g + p_ref[s].astype(F32)
        mn = ADAM_B1 * m_ref[...] + (1.0 - ADAM_B1) * g
        vn = ADAM_B2 * v_ref[...] + (1.0 - ADAM_B2) * (g * g)
        g_out[...] = g
        m_out[...] = mn
        v_out[...] = vn
        d_out[...] = -ADAM_LR * ((mn / c1) / (jnp.sqrt(vn / c2) + ADAM_EPS) + ADAM_WD * w_ref[...])

    slab = pl.BlockSpec((None, tr, b), lambda i: (layer, i, 0))
    whole = pl.BlockSpec(memory_space=pl.ANY)
    return pl.pallas_call(
        body, name=name, grid=(a // tr,),
        in_specs=[pl.BlockSpec((N_DEV, tr, b), lambda i: (0, i, 0)), slab, slab, slab] + [whole] * 4,
        out_specs=[slab] * 4, out_shape=[jax.ShapeDtypeStruct(w.shape, F32)] * 4,
        input_output_aliases={4: 0, 5: 1, 6: 2, 7: 3},
        compiler_params=_params(1))(parts, w, m, v, *prev)


_COL = ("w_in", "w_pool_o", "w_mem_o", "w_ff1")
_DW_SHARDED = ("w_in", "w_ff1")
_BIG =("w_in", "w_ret_o", "w_pool_o", "w_mem_kv", "w_mem_o", "w_out", "w_ff1", "w_ff2")
_SMALL = ("ret_decay_logit", "w_pool_grp", "pool_scale", "norm1_g", "norm2_g", "mem_norm_g", "final_norm_g")
_SMALL_MM = ("w_pool_grp",)
_SMALL_F32 = tuple(n for n in _SMALL if n not in _SMALL_MM)
_WEIGHTS = ("w_in", "ret_decay_logit", "w_ret_o", "w_pool_grp", "pool_scale", "w_pool_o", "w_mem_kv", "w_mem_o",
            "w_out", "w_ff1", "w_ff2", "norm1_g", "norm2_g", "mem_norm_g", "final_norm_g")


def _small_rows(size, d):
    return -(-size // (8 * d)) * 8


def _pack_small(ws, d, names):
    parts = []
    for n in names:
        flat = ws[n].reshape(-1)
        rows = _small_rows(flat.shape[0], d)
        parts.append(jnp.pad(flat, (0, rows * d - flat.shape[0])).reshape(rows, d))
    return jnp.concatenate(parts, axis=0)[None]


def _unpack_small(packed, like, d, names):
    out, off = {}, 0
    for n in names:
        rows = _small_rows(like[n].size, d)
        out[n] = packed[0, off:off + rows].reshape(-1)[:like[n].size].reshape(like[n].shape)
        off += rows
    return out


def kernel(x, mem, w_in, ret_decay_logit, w_ret_o, w_pool_grp, pool_scale, w_pool_o, w_mem_kv, w_mem_o, w_out, w_ff1, w_ff2, norm1_g, norm2_g, mem_norm_g, final_norm_g, loss_target, m_w_in, m_ret_decay_logit, m_w_ret_o, m_w_pool_grp, m_pool_scale, m_w_pool_o, m_w_mem_kv, m_w_mem_o, m_w_out, m_w_ff1, m_w_ff2, m_norm1_g, m_norm2_g, m_mem_norm_g, m_final_norm_g, v_w_in, v_ret_decay_logit, v_w_ret_o, v_w_pool_grp, v_pool_scale, v_w_pool_o, v_w_mem_kv, v_w_mem_o, v_w_out, v_w_ff1, v_w_ff2, v_norm1_g, v_norm2_g, v_mem_norm_g, v_final_norm_g):
    w = dict(w_in=w_in, ret_decay_logit=ret_decay_logit, w_ret_o=w_ret_o, w_pool_grp=w_pool_grp,
             pool_scale=pool_scale, w_pool_o=w_pool_o, w_mem_kv=w_mem_kv, w_mem_o=w_mem_o, w_out=w_out,
             w_ff1=w_ff1, w_ff2=w_ff2, norm1_g=norm1_g, norm2_g=norm2_g, mem_norm_g=mem_norm_g,
             final_norm_g=final_norm_g)
    mom = dict(w_in=m_w_in, ret_decay_logit=m_ret_decay_logit, w_ret_o=m_w_ret_o, w_pool_grp=m_w_pool_grp,
               pool_scale=m_pool_scale, w_pool_o=m_w_pool_o, w_mem_kv=m_w_mem_kv, w_mem_o=m_w_mem_o,
               w_out=m_w_out, w_ff1=m_w_ff1, w_ff2=m_w_ff2, norm1_g=m_norm1_g, norm2_g=m_norm2_g,
               mem_norm_g=m_mem_norm_g, final_norm_g=m_final_norm_g)
    vel = dict(w_in=v_w_in, ret_decay_logit=v_ret_decay_logit, w_ret_o=v_w_ret_o, w_pool_grp=v_w_pool_grp,
               pool_scale=v_pool_scale, w_pool_o=v_w_pool_o, w_mem_kv=v_w_mem_kv, w_mem_o=v_w_mem_o,
               w_out=v_w_out, w_ff1=v_w_ff1, w_ff2=v_w_ff2, norm1_g=v_norm1_g, norm2_g=v_norm2_g,
               mem_norm_g=v_mem_norm_g, final_norm_g=v_final_norm_g)

    bl, seq, d = x.shape
    mlen = mem.shape[1]
    depth = w_in.shape[0]
    t = bl * seq
    dk = d // 8

    def natural(n, g, side_by_side=False):
        if side_by_side:
            return g
        if n in _DW_SHARDED:
            return _columns_side_by_side("relayout_" + n, g)
        if n in _COL:
            return jnp.transpose(g, (1, 0, 2)).reshape(g.shape[1], -1)
        return g.reshape(-1, g.shape[-1])

    def finish_gather(name, names, started, after):
        got = _exchange_wait(name, False, started, after)
        return {n: natural(n, g, o in started[1]) for o, (n, g) in enumerate(zip(names, got))}

    shards = [{n: w[n][l].astype(MM) for n in _BIG} for l in range(depth)]
    rest = _BIG[1:]
    (w_in0,) = _all_gather("gather_w_in", [shards[0][_BIG[0]]])
    full = [{_BIG[0]: natural(_BIG[0], w_in0)}]
    def start_layer(l, after):
        s_in = _exchange_start(f"gather_start_in{l}", False, [shards[l][_BIG[0]]], after=after, mid=(0,))
        s_rest = _exchange_start(f"gather_start_{l}", False, [shards[l][n] for n in rest], after=[s_in[4]],
                                 mid=rest_mid)
        return s_in, s_rest, (s_in[4], s_rest[4])

    rest_mid = tuple(o for o, n in enumerate(rest) if n in _COL)
    pending = (None, _exchange_start("gather_start_0", False, [shards[0][n] for n in rest], after=[w_in0],
                                     mid=rest_mid))
    first_tokens = (pending[1][4],)
    pending_next = None
    if depth > 1:
        pending_next = start_layer(1, [pending[1][4]])
        first_tokens += pending_next[2]

    inv = ROPE_BASE ** (-jnp.arange(0, dk, 2, dtype=F32) / dk)
    ang = jnp.arange(seq, dtype=F32)[:, None] * inv[None, :]
    cos2 = jnp.concatenate([jnp.cos(ang), jnp.cos(ang)], axis=-1)
    sin2 = jnp.concatenate([-jnp.sin(ang), jnp.sin(ang)], axis=-1)
    log_g = jax.nn.log_sigmoid(ret_decay_logit)
    x2 = x.reshape(t, d)
    mem2 = mem.reshape(bl * mlen, d)
    gmem = mem_norm_g.reshape(1, d)

    def merge(a_r, y_p, o_a, g_r, g_p, g_m, w_r, w_p, w_m):
        f = lambda z: z.astype(F32)
        o_r, o_p, o_m = _dot(a_r, w_r), _dot(y_p, w_p), _dot(o_a, w_m)
        return _sigmoid(f(g_r)) * o_r + _sigmoid(f(g_p)) * o_p + _sigmoid(f(g_m)) * o_m, o_r, o_p, o_m

    def relu2(u):
        r = jnp.maximum(u.astype(MM), 0.0)
        return r * r

    def ident(a):
        return a

    saved = []
    xc = x2
    for l in range(depth):
        s = dict(x_in=xc)
        started_now = ()
        if l > 0:
            pending, pending_next = pending_next, None
            full.append(finish_gather(f"gather_wait_in{l}", _BIG[:1], pending[0], xc))
            if l + 1 < depth:
                pending_next = start_layer(l + 1, [full[l]["w_in"]])
                started_now = pending_next[2]
        fw = full[l]
        g1 = norm1_g[l].reshape(1, d)
        g2 = norm2_g[l].reshape(1, d)
        s["proj"], s["h1"] = _pmm("proj", _rms_prologue, [(xc, d, 0)], [g1], fw["w_in"],
                                  tm=2048, tn=1024, save_a=True, out_dtypes=(MM,),
                                  after=started_now if l > 0 else first_tokens)
        proj = s["proj"]
        s["o_raw"], s["a_ret"], s["sf"], s["sb"], s["qr"], s["kr"] = _ret_core_fwd(proj, cos2, sin2, log_g[l],
                                                                                   d, bl, seq)
        s["y"] = _pool_fwd(proj, w_pool_grp[l], pool_scale[l].reshape(1, -1), d, bl, seq)
        fw.update(finish_gather(f"gather_wait_{l}", rest, pending[1], s["a_ret"]))
        s["kv"], s["memn"] = _pmm("mem_kv", _rms_prologue, [(mem2, d, 0)], [gmem], fw["w_mem_kv"],
                                  tm=512, tn=512, save_a=True)
        s["o_att"] = _attn_fwd(proj, s["kv"], d, bl, seq, mlen)
        s["x_mid"], s["merged"], s["o_ret"], s["o_pool"], s["o_mem"] = _pmm(
            "merge_out", merge,
            [(s["a_ret"], d, 0), (s["y"], d // 2, 0), (s["o_att"], d // 2, 0), (proj, d, 4), (proj, d, 5), (proj, d, 6)],
            [fw["w_ret_o"], fw["w_pool_o"], fw["w_mem_o"]], fw["w_out"], tm=512, tn=1024, residual=xc, save_a=True,
            extra_outs=[(d, MM)] * 3)
        s["u"], s["h2"] = _pmm("ff1", _rms_prologue, [(s["x_mid"], d, 0)], [g2], fw["w_ff1"],
                               tm=512, tn=4096, save_a=True, out_dtypes=(MM,))
        (xc,) = _pmm("ff2", relu2, [(s["u"], s["u"].shape[1], 0)], [], fw["w_ff2"],
                     tm=512, tn=1024, residual=s["x_mid"])
        saved.append(s)

    dxc, g_final, loss_part = _loss_head(xc, loss_target.reshape(t, d), final_norm_g.reshape(1, d))
    loss = lax.psum(loss_part[0, 0], ("x", "y", "c"))

    small_names = ("w_pool_grp", "pool_scale", "norm1_g", "norm2_g", "ret_decay_logit")
    grads = {n: [None] * depth for n in small_names}
    group_a = ("w_ff1", "w_ff2")
    group_b = tuple(n for n in _BIG if n not in group_a)
    scatters = {}
    dmemn = jnp.zeros((bl * mlen, d), F32)

    def relu2_bwd(acc, u):
        return (acc * (2.0 * jnp.maximum(u.astype(F32), 0.0)),)

    def gates_bwd(acc, g_r, g_p, g_m, o_r, o_p, o_m, w_r, w_p, w_m):
        d_os, d_gs, backs = [], [], []
        for gz, oz, wz in ((g_r, o_r, w_r), (g_p, o_p, w_p), (g_m, o_m, w_m)):
            sg = _sigmoid(gz.astype(F32))
            d_o = (acc * sg).astype(MM)
            d_os.append(d_o)
            d_gs.append(acc * oz.astype(F32) * (sg * (1.0 - sg)))
            backs.append(_dot_nt(d_o, wz))
        return tuple(d_os + d_gs + backs)

    def to_send(n, g):
        a, b = w[n].shape[1:]
        if n in _DW_SHARDED:
            return g
        if n in _COL:
            return jnp.transpose(g.reshape(a, N_DEV, b), (1, 0, 2))
        return g.reshape(N_DEV, a, b)

    for l in reversed(range(depth)):
        s = saved[l]
        fw = full[l]
        proj = s["proj"]
        g1 = norm1_g[l].reshape(1, d)
        g2 = norm2_g[l].reshape(1, d)
        dw = {}
        (du,) = _pmm("ff2_bwd", ident, [(dxc, d, 0)], [], fw["w_ff2"], w_mode="nt", tm=512, tn=4096,
                     epilogue=relu2_bwd, epi_ins=[(s["u"], 0)], out_dtypes=(MM,))
        dw["w_ff2"] = _tnmm("dw_ff2", s["u"], dxc, a_fn=relu2)
        dw["w_ff1"] = _tnmm("dw_ff1", s["h2"], du, col_shards=True)
        scatters[l, "a"] = _exchange_start(f"scatter_start_a{l}", True, [to_send(n, dw[n]) for n in group_a])
        dmid, grads["norm2_g"][l] = _mm_rms_bwd("ff1_norm2_bwd", du, fw["w_ff1"], s["x_mid"], g2, dxc, tm=512)
        d_oret, d_opool, d_omem, dgr, dgp, dgm, da_ret, dy, do_att = _pmm(
            "out_bwd", ident, [(dmid, d, 0)], [], fw["w_out"], w_mode="nt", tm=256, tn=d, epilogue=gates_bwd,
            epi_ins=[(proj, 4 * d), (proj, 5 * d), (proj, 6 * d), (s["o_ret"], 0), (s["o_pool"], 0), (s["o_mem"], 0)],
            epi_full=[fw["w_ret_o"], fw["w_pool_o"], fw["w_mem_o"]], out_dtypes=(MM,) * 7 + (F32,) * 2,
            out_widths=[d] * 7 + [d // 2] * 2, after=(scatters[l, "a"][4],))
        dw["w_out"] = _tnmm("dw_out", s["merged"], dmid)
        dw["w_ret_o"] = _tnmm("dw_ret_o", s["a_ret"], d_oret)
        dw["w_pool_o"] = _tnmm("dw_pool_o", s["y"], d_opool)
        dw["w_mem_o"] = _tnmm("dw_mem_o", s["o_att"], d_omem)
        dg_ret, do_ret = _ret_post_bwd(da_ret, proj, s["o_raw"], d)
        dq, dkk, dvv, dlf, dlb = _ret_core_bwd(s["qr"], s["kr"], proj, do_ret, s["sf"], s["sb"], cos2, sin2,
                                               log_g[l], d, bl, seq)
        dl = jnp.stack([dlf[:, 0, 0].reshape(bl, HEADS).sum(0), dlb[:, 0, 0].reshape(bl, HEADS).sum(0)])
        grads["ret_decay_logit"][l] = dl * jax.nn.sigmoid(-ret_decay_logit[l])
        dp, grads["w_pool_grp"][l], dscale = _pool_bwd(proj, dy, w_pool_grp[l], pool_scale[l].reshape(1, -1),
                                                       d, bl, seq)
        grads["pool_scale"][l] = dscale.reshape(-1)
        dqm, dmk, dmv = _attn_bwd(proj, s["kv"], do_att, d, bl, seq, mlen)
        dkv = jnp.concatenate([dmk, dmv], axis=-1).astype(MM)
        dw["w_mem_kv"] = _tnmm("dw_mem_kv", s["memn"], dkv)
        (dmemn,) = _pmm("mem_kv_bwd", None, [(dkv, d, 0)], [], fw["w_mem_kv"], w_mode="nt", tm=512, tn=512,
                        residual=dmemn)
        dproj = [dq, dkk, dvv, dg_ret, dp, dqm, dgr, dgp, dgm]
        dw["w_in"] = _tnmm("dw_in", s["h1"], dproj, col_shards=True, tm=512, tk=512)
        scatters[l, "b"] = _exchange_start(f"scatter_start_b{l}", True, [to_send(n, dw[n]) for n in group_b])
        dxc, grads["norm1_g"][l] = _mm_rms_bwd("proj_norm1_bwd", dproj, fw["w_in"], s["x_in"], g1, dmid, tm=256,
                                               after=(scatters[l, "b"][4],))

    _, g_memn = _rms_bwd("mem_norm_bwd", dmemn, mem2, gmem, None)
    grad_x = dxc.reshape(bl, seq, d)

    small_g = dict(ret_decay_logit=jnp.stack(grads["ret_decay_logit"]), w_pool_grp=jnp.stack(grads["w_pool_grp"]),
                   pool_scale=jnp.stack(grads["pool_scale"]),
                   norm1_g=jnp.concatenate(grads["norm1_g"], axis=0), norm2_g=jnp.concatenate(grads["norm2_g"], axis=0),
                   mem_norm_g=g_memn.reshape(-1), final_norm_g=g_final.reshape(-1))
    small_started = _exchange_start("gather_small_start", False,
                                    [_pack_small(small_g, d, _SMALL_MM)[0].astype(MM),
                                     _pack_small(small_g, d, _SMALL_F32)[0]])

    big = {n: [lax.empty(w[n].shape, F32) for _ in range(4)] for n in _BIG}

    def update(l, grp, names, after):
        recv = _exchange_wait(f"scatter_wait_{grp}{l}", True, scatters[l, grp], after)
        for n, parts in zip(names, recv):
            big[n] = _adamw("adamw_" + n, parts, w[n], mom[n], vel[n], big[n], l)
        return big[names[-1]][0]

    after = dxc
    for l in reversed(range(1, depth)):
        for grp, names in (("a", group_a), ("b", group_b)):
            after = update(l, grp, names, after)
    after = update(0, "a", group_a, after)
    small_lands = _exchange_wait("gather_small_wait", False, small_started, after)
    small = [{} for _ in range(4)]
    for names, parts in zip((_SMALL_MM, _SMALL_F32), small_lands):
        w_small = _pack_small(w, d, names)
        res = _adamw("adamw_small", parts, w_small, _pack_small(mom, d, names), _pack_small(vel, d, names),
                     [lax.empty(w_small.shape, F32) for _ in range(4)], 0)
        after = res[0]
        for k in range(4):
            small[k].update(_unpack_small(res[k], w, d, names))
    update(0, "b", group_b, after)

    outs = [loss, grad_x]
    for k in range(4):
        outs += [big[n][k] if n in _BIG else small[k][n] for n in _WEIGHTS]
    return tuple(outs)
```

```python
import jax
import jax.numpy as jnp
from jax import lax
from jax.experimental import pallas as pl
from jax.experimental.pallas import tpu as pltpu

F32 = jnp.float32
MM = jnp.bfloat16
N_DEV = 8
HEADS = 4
POOL_WINDOWS = (2, 4, 8, 16)
RET_CHUNK = 256
EPS = 1e-6
ROPE_BASE = 10000.0
ADAM_LR, ADAM_B1, ADAM_B2, ADAM_EPS, ADAM_WD, ADAM_STEP = 0.001, 0.9, 0.999, 1e-08, 0.01, 10
V7X_VMEM_LIMIT = 56 * 1024 * 1024
MESH = pl.DeviceIdType.MESH


def _params(n_axes):
    return pltpu.CompilerParams(dimension_semantics=("arbitrary",) * n_axes,
                                vmem_limit_bytes=V7X_VMEM_LIMIT)


def _tile(n, pref, align=128):
    cands = [c for c in range(align, min(pref, n) + 1, align) if n % c == 0]
    return max(cands) if cands else n


def _sigmoid(z):
    return 0.5 * jnp.tanh(0.5 * z) + 0.5


def _dot(a, b):
    return jnp.dot(a, b, preferred_element_type=F32)


def _dot_nt(a, b):
    return lax.dot_general(a, b, (((1,), (1,)), ((), ())), preferred_element_type=F32)


def _dot_tn(a, b):
    return lax.dot_general(a, b, (((0,), (0,)), ((), ())), preferred_element_type=F32)


def _pmm(name, prologue, row_ins, vec_ins, w, *, tm, tn, w_mode="nn", residual=None, save_a=False,
         epilogue=None, epi_ins=(), out_dtypes=(F32,), after=(), extra_outs=(), epi_full=(), out_widths=None):
    m = row_ins[0][0].shape[0]
    if w_mode == "nn":
        k, n = w.shape
        tn = _tile(n, tn)
        w_spec = pl.BlockSpec((k, tn), lambda i, j: (0, j))
    else:
        n, k = w.shape
        tn = _tile(n, tn)
        w_spec = pl.BlockSpec((tn, k), lambda i, j: (j, 0))
    tm = _tile(m, tm, 8)
    n_row, n_vec, n_epi, n_out = len(row_ins), len(vec_ins), len(epi_ins), len(out_dtypes)
    has_res = residual is not None
    use_scr = prologue is not None
    out_widths = [n] * n_out if out_widths is None else list(out_widths)
    assert all(wd == n for wd in out_widths) or tn == n

    def body(*refs):
        row_refs = refs[:n_row]
        p = n_row
        vec_refs = refs[p:p + n_vec]
        p += n_vec
        w_ref = refs[p]
        p += 1
        res_ref = refs[p] if has_res else None
        p += int(has_res)
        epi_refs = refs[p:p + n_epi + len(epi_full)]
        p += n_epi + len(epi_full) + len(after)
        out_refs = refs[p:p + n_out]
        p += n_out
        a_out = refs[p] if save_a else None
        p += int(save_a)
        extra_refs = refs[p:p + len(extra_outs)]
        p += len(extra_outs)
        if use_scr:
            a_src = refs[p]

            @pl.when(pl.program_id(1) == 0)
            def _():
                made = prologue(*[r[...] for r in row_refs], *[v[...] for v in vec_refs])
                made = made if isinstance(made, tuple) else (made,)
                a = made[0].astype(MM)
                a_src[...] = a
                if save_a:
                    a_out[...] = a
                for e_ref, e in zip(extra_refs, made[1:]):
                    e_ref[...] = e.astype(e_ref.dtype)
        else:
            a_src = row_refs[0]
        if w_mode == "nt":
            acc = _dot_nt(a_src[...], w_ref[...])
        else:
            acc = _dot(a_src[...], w_ref[...])
        if has_res:
            acc = acc + res_ref[...]
        outs = epilogue(acc, *[e[...] for e in epi_refs]) if epilogue is not None else (acc,)
        for o_ref, o in zip(out_refs, outs):
            o_ref[...] = o.astype(o_ref.dtype)

    in_specs = [pl.BlockSpec((tm, wd), lambda i, j, cb=cb: (i, cb)) for (_, wd, cb) in row_ins]
    in_specs += [pl.BlockSpec(v.shape, lambda i, j: (0, 0)) for v in vec_ins]
    in_specs += [w_spec]
    args = [r[0] for r in row_ins] + list(vec_ins) + [w]
    if has_res:
        in_specs.append(pl.BlockSpec((tm, tn), lambda i, j: (i, j)))
        args.append(residual)
    for (arr, off) in epi_ins:
        assert off % tn == 0
        in_specs.append(pl.BlockSpec((tm, tn), lambda i, j, ob=off // tn: (i, ob + j)))
        args.append(arr)
    in_specs += [pl.BlockSpec(v.shape, lambda i, j: (0, 0)) for v in epi_full]
    args += list(epi_full)
    n_after = len(after)
    in_specs += [pl.BlockSpec(memory_space=pl.ANY)] * n_after
    args += list(after)
    out_specs = [pl.BlockSpec((tm, tn if wd == n else wd), lambda i, j: (i, j)) for wd in out_widths]
    out_shape = [jax.ShapeDtypeStruct((m, wd), dt) for wd, dt in zip(out_widths, out_dtypes)]
    if save_a:
        out_specs.append(pl.BlockSpec((tm, k), lambda i, j: (i, 0)))
        out_shape.append(jax.ShapeDtypeStruct((m, k), MM))
    for wd, dt in extra_outs:
        out_specs.append(pl.BlockSpec((tm, wd), lambda i, j: (i, 0)))
        out_shape.append(jax.ShapeDtypeStruct((m, wd), dt))
    scratch = [pltpu.VMEM((tm, k), MM)] if use_scr else []
    return pl.pallas_call(body, name=name, grid=(m // tm, n // tn), in_specs=in_specs,
                          out_specs=out_specs, out_shape=out_shape, scratch_shapes=scratch,
                          compiler_params=_params(2))(*args)


def _tnmm(name, a, b, *, tm=1024, tn=1024, tk=1024, col_shards=False, a_fn=None):
    t, m = a.shape
    pieces = list(b) if isinstance(b, (list, tuple)) else [b]
    widths = [p.shape[1] for p in pieces]
    offs = [sum(widths[:p]) for p in range(len(pieces))]
    n = sum(widths)
    tm, tk = _tile(m, tm), _tile(t, tk, 8)
    per_tile = 1
    if col_shards:
        wb = n // N_DEV
        if len(pieces) > 1:
            tn = n
        while 2 * per_tile * wb <= tn and 2 * per_tile <= N_DEV:
            per_tile *= 2
        tn = per_tile * wb
        out_spec = pl.BlockSpec((per_tile, tm, wb), lambda i, j, kk: (j, i, 0))
        out_shape = jax.ShapeDtypeStruct((N_DEV, m, wb), MM)
    else:
        tn = _tile(n, tn)
        out_spec = pl.BlockSpec((tm, tn), lambda i, j, kk: (i, j))
        out_shape = jax.ShapeDtypeStruct((m, n), MM)
    nk = t // tk

    assert len(pieces) == 1 or tn == n

    def body(a_ref, *rest):
        b_refs, (o_ref, acc) = rest[:len(pieces)], rest[len(pieces):]
        kk = pl.program_id(2)

        @pl.when(kk == 0)
        def _():
            acc[...] = jnp.zeros_like(acc)

        av = (a_ref[...] if a_fn is None else a_fn(a_ref[...])).astype(MM)
        if len(pieces) == 1:
            acc[...] += _dot_tn(av, b_refs[0][...].astype(MM))
        else:
            for b_ref, off, wd in zip(b_refs, offs, widths):
                acc[:, off:off + wd] += _dot_tn(av, b_ref[...].astype(MM))

        @pl.when(kk == nk - 1)
        def _():
            if col_shards:
                for sh in range(per_tile):
                    o_ref[sh] = acc[:, sh * wb:(sh + 1) * wb].astype(o_ref.dtype)
            else:
                o_ref[...] = acc[...].astype(o_ref.dtype)

    return pl.pallas_call(
        body, name=name, grid=(m // tm, n // tn, nk),
        in_specs=[pl.BlockSpec((tk, tm), lambda i, j, kk: (kk, i))]
        + [pl.BlockSpec((tk, tn if len(pieces) == 1 else wd), lambda i, j, kk: (kk, j)) for wd in widths],
        out_specs=out_spec, out_shape=out_shape,
        scratch_shapes=[pltpu.VMEM((tm, tn), F32)],
        compiler_params=_params(3))(a, *pieces)


def _rms_prologue(x, g):
    r = lax.rsqrt(jnp.mean(x * x, axis=-1, keepdims=True) + EPS)
    return x * r * g


def _rms_bwd_rows(dh, x, g):
    d = x.shape[-1]
    r = lax.rsqrt(jnp.mean(x * x, axis=-1, keepdims=True) + EPS)
    xh = x * r
    dxh = dh * g
    dx = r * (dxh - xh * (jnp.sum(dxh * xh, axis=-1, keepdims=True) / d))
    dg = jnp.sum(dh * xh, axis=0, keepdims=True)
    return dx, dg


def _rms_bwd(name, dh, x, g, dres, *, tm=512):
    m, d = x.shape
    tm = min(tm, m)
    has_res = dres is not None

    def body(*refs):
        if has_res:
            dh_ref, x_ref, g_ref, r_ref, dx_ref, dg_ref = refs
        else:
            dh_ref, x_ref, g_ref, dx_ref, dg_ref = refs
        dx, dg = _rms_bwd_rows(dh_ref[...], x_ref[...], g_ref[...])
        if has_res:
            dx = dx + r_ref[...]
        dx_ref[...] = dx

        @pl.when(pl.program_id(0) == 0)
        def _():
            dg_ref[...] = jnp.zeros_like(dg_ref)

        dg_ref[...] += dg

    row = pl.BlockSpec((tm, d), lambda i: (i, 0))
    vec = pl.BlockSpec((1, d), lambda i: (0, 0))
    in_specs = [row, row, vec] + ([row] if has_res else [])
    args = [dh, x, g] + ([dres] if has_res else [])
    return pl.pallas_call(body, name=name, grid=(m // tm,), in_specs=in_specs, out_specs=[row, vec],
                          out_shape=[jax.ShapeDtypeStruct((m, d), F32), jax.ShapeDtypeStruct((1, d), F32)],
                          compiler_params=_params(1))(*args)


def _mm_rms_bwd(name, a, w, x, g, dres, *, tm, after=()):
    pieces = list(a) if isinstance(a, (list, tuple)) else [a]
    widths = [p.shape[1] for p in pieces]
    m = pieces[0].shape[0]
    d = w.shape[0]
    tm = _tile(m, tm, 8)
    n_a = len(pieces)

    def body(*refs):
        a_refs = refs[:n_a]
        w_ref, x_ref, g_ref, r_ref = refs[n_a:n_a + 4]
        dx_ref, dg_ref = refs[n_a + 4 + len(after):]

        av = a_refs[0][...] if n_a == 1 else jnp.concatenate([a_ref[...] for a_ref in a_refs], axis=1)
        dh = _dot_nt(av, w_ref[...])
        dx, dg = _rms_bwd_rows(dh, x_ref[...], g_ref[...])
        dx_ref[...] = dx + r_ref[...]

        @pl.when(pl.program_id(0) == 0)
        def _():
            dg_ref[...] = jnp.zeros_like(dg_ref)

        dg_ref[...] += dg

    row = pl.BlockSpec((tm, d), lambda i: (i, 0))
    vec = pl.BlockSpec((1, d), lambda i: (0, 0))
    return pl.pallas_call(
        body, name=name, grid=(m // tm,),
        in_specs=[pl.BlockSpec((tm, wd), lambda i: (i, 0)) for wd in widths]
        + [pl.BlockSpec(w.shape, lambda i: (0, 0)), row, vec, row]
        + [pl.BlockSpec(memory_space=pl.ANY)] * len(after),
        out_specs=[row, vec],
        out_shape=[jax.ShapeDtypeStruct((m, d), F32), jax.ShapeDtypeStruct((1, d), F32)],
        compiler_params=_params(1))(*pieces, w, x, g, dres, *after)


def _loss_head(x, target, g, *, tm=256):
    m, d = x.shape
    tm = min(tm, m)

    def body(x_ref, t_ref, g_ref, dx_ref, dg_ref, loss_ref):
        xv, gv = x_ref[...], g_ref[...]
        y = _rms_prologue(xv, gv)
        err = y - t_ref[...]
        part = 0.5 * jnp.sum(jnp.sum(err * err, axis=-1, keepdims=True) / d)
        dx, dg = _rms_bwd_rows(err / d, xv, gv)
        dx_ref[...] = dx

        @pl.when(pl.program_id(0) == 0)
        def _():
            dg_ref[...] = jnp.zeros_like(dg_ref)
            loss_ref[...] = jnp.zeros_like(loss_ref)

        dg_ref[...] += dg
        loss_ref[...] += jnp.full(loss_ref.shape, part, F32)

    row = pl.BlockSpec((tm, d), lambda i: (i, 0))
    vec = pl.BlockSpec((1, d), lambda i: (0, 0))
    lspec = pl.BlockSpec((1, 128), lambda i: (0, 0))
    return pl.pallas_call(body, name="loss_head", grid=(m // tm,), in_specs=[row, row, vec],
                          out_specs=[row, vec, lspec],
                          out_shape=[jax.ShapeDtypeStruct((m, d), F32), jax.ShapeDtypeStruct((1, d), F32),
                                     jax.ShapeDtypeStruct((1, 128), F32)],
                          compiler_params=_params(1))(x, target, g)


def _rot(xv, cos2, sin2, half):
    return xv * cos2 + pltpu.roll(xv, half, 1) * sin2


def _rot_t(dv, cos2, sin2, half):
    return dv * cos2 + pltpu.roll(dv * sin2, half, 1)


def _ret_consts(lg_ref, h, t, dk):
    lf, lb = lg_ref[0, h], lg_ref[1, h]
    ab = (lax.broadcasted_iota(jnp.int32, (t, t), 0) - lax.broadcasted_iota(jnp.int32, (t, t), 1)).astype(F32)
    dmat = jnp.exp(jnp.where(ab >= 0, lf * ab, -lb * ab))
    up = lax.broadcasted_iota(jnp.int32, (t, dk), 0).astype(F32) + 1.0
    down = float(t) - up
    one = jnp.ones((1, 1), F32)
    return dict(ab=ab, dmat=dmat, xi_f=jnp.exp(lf * up), zeta_f=jnp.exp(lf * down), xi_b=jnp.exp(lb * up),
                zeta_b=jnp.exp(lb * down), up=up[:, 0:1], down=down[:, 0:1],
                cf=jnp.exp(one * (lf * t)), cb=jnp.exp(one * (lb * t)))


def _scaled(xv, rows):
    return (xv.astype(F32) * rows).astype(MM)


def _ret_core_fwd(proj, cos2, sin2, lg, d, bl, seq, *, tc=RET_CHUNK):
    t = proj.shape[0]
    dk, dv = d // 8, d // 4
    tc = min(tc, seq)
    nc = seq // tc
    scale = float(dk) ** -0.5

    def body(lg_ref, qp_ref, kp_ref, v_ref, g_ref, c_ref, s_ref, o_ref, a_ref, sf_ref, sb_ref, q_ref, k_ref, o_acc):
        c = _ret_consts(lg_ref, pl.program_id(1), tc, dk)

        def rows_of(i):
            return pl.ds(pl.multiple_of(i * tc, tc), tc)

        def rotate(i, carry):
            rows = rows_of(i)
            cs, sn = c_ref[rows, :], s_ref[rows, :]
            q_ref[rows, :] = _rot(qp_ref[rows, :].astype(F32), cs, sn, dk // 2).astype(MM)
            k_ref[rows, :] = (_rot(kp_ref[rows, :].astype(F32), cs, sn, dk // 2) * scale).astype(MM)
            return carry

        lax.fori_loop(0, nc, rotate, 0)

        def fwd_step(i, sf):
            rows = rows_of(i)
            sf_ref[i] = sf
            q, kk, v = q_ref[rows, :], k_ref[rows, :], v_ref[rows, :]
            p = (_dot_nt(q, kk) * c["dmat"]).astype(MM)
            o_acc[rows, :] = _dot(p, v) + _dot(_scaled(q, c["xi_f"]), sf.astype(MM))
            return sf * c["cf"] + _dot_tn(_scaled(kk, c["zeta_f"]), v)

        lax.fori_loop(0, nc, fwd_step, jnp.zeros((dk, dv), F32))

        def bwd_step(ii, sb):
            rows = rows_of(nc - 1 - ii)
            sb_ref[nc - 1 - ii] = sb
            q, kk, v = q_ref[rows, :], k_ref[rows, :], v_ref[rows, :]
            o_acc[rows, :] += _dot(_scaled(q, c["zeta_b"]), sb.astype(MM))
            return sb * c["cb"] + _dot_tn(_scaled(kk, c["xi_b"]), v)

        lax.fori_loop(0, nc, bwd_step, jnp.zeros((dk, dv), F32))

        def post(i, carry):
            rows = rows_of(i)
            o = o_acc[rows, :]
            o_ref[rows, :] = o.astype(o_ref.dtype)
            oc = o - jnp.mean(o, axis=-1, keepdims=True)
            on = oc * lax.rsqrt(jnp.mean(oc * oc, axis=-1, keepdims=True) + EPS)
            g = g_ref[rows, :].astype(F32)
            a_ref[rows, :] = (on * (g * _sigmoid(g))).astype(MM)
            return carry

        lax.fori_loop(0, nc, post, 0)

    qk = pl.BlockSpec((seq, dk), lambda b, h: (b, h))
    vv = pl.BlockSpec((seq, dv), lambda b, h: (b, h))
    tab = pl.BlockSpec((seq, dk), lambda b, h: (0, 0))
    states = pl.BlockSpec((None, nc, dk, dv), lambda b, h: (b * HEADS + h, 0, 0, 0))
    return pl.pallas_call(
        body, name="ret_core_fwd", grid=(bl, HEADS),
        in_specs=[pl.BlockSpec(memory_space=pltpu.SMEM), qk, pl.BlockSpec((seq, dk), lambda b, h: (b, HEADS + h)),
                  pl.BlockSpec((seq, dv), lambda b, h: (b, HEADS + h)),
                  pl.BlockSpec((seq, dv), lambda b, h: (b, 2 * HEADS + h)), tab, tab],
        out_specs=[vv, vv, states, states, qk, qk],
        out_shape=[jax.ShapeDtypeStruct((t, d), MM), jax.ShapeDtypeStruct((t, d), MM),
                   jax.ShapeDtypeStruct((bl * HEADS, nc, dk, dv), F32),
                   jax.ShapeDtypeStruct((bl * HEADS, nc, dk, dv), F32),
                   jax.ShapeDtypeStruct((t, d // 2), MM), jax.ShapeDtypeStruct((t, d // 2), MM)],
        scratch_shapes=[pltpu.VMEM((seq, dv), F32)],
        compiler_params=_params(2))(lg, proj, proj, proj, proj, cos2, sin2)


def _ret_post_bwd(da, proj, o_raw, d, *, ts=2048):
    t = da.shape[0]
    dv = d // 4
    ts = min(ts, t)

    def body(da_ref, g_ref, o_ref, dg_ref, do_ref):
        o, g, dav = o_ref[...].astype(F32), g_ref[...].astype(F32), da_ref[...].astype(F32)
        mu = jnp.mean(o, axis=-1, keepdims=True)
        oc = o - mu
        r = lax.rsqrt(jnp.mean(oc * oc, axis=-1, keepdims=True) + EPS)
        on = oc * r
        sg = _sigmoid(g)
        don = dav * (g * sg)
        dg_ref[...] = (dav * on * (sg * (1.0 + g * (1.0 - sg)))).astype(MM)
        do = r * (don - jnp.mean(don, axis=-1, keepdims=True) - on * jnp.mean(don * on, axis=-1, keepdims=True))
        do_ref[...] = do.astype(MM)

    blk = pl.BlockSpec((ts, dv), lambda i, h: (i, h))
    return pl.pallas_call(
        body, name="ret_post_bwd", grid=(t // ts, HEADS),
        in_specs=[blk, pl.BlockSpec((ts, dv), lambda i, h: (i, 2 * HEADS + h)), blk],
        out_specs=[blk, blk],
        out_shape=[jax.ShapeDtypeStruct((t, d), MM), jax.ShapeDtypeStruct((t, d), MM)],
        compiler_params=_params(2))(da, proj, o_raw)


def _ret_core_bwd(qr, kr, proj, do, sf_in, sb_in, cos2, sin2, lg, d, bl, seq, *, tc=RET_CHUNK):
    t = qr.shape[0]
    dk, dv = d // 8, d // 4
    tc = min(tc, seq)
    nc = seq // tc
    scale = float(dk) ** -0.5

    def body(lg_ref, q_ref, k_ref, v_ref, do_ref, sf_all, sb_all, c_ref, s_ref, dq_ref, dk_ref, dv_ref,
             dlf_ref, dlb_ref, dq_acc, dk_acc, dv_acc):
        c = _ret_consts(lg_ref, pl.program_id(1), tc, dk)
        fwd = c["ab"] >= 0
        zero_state = jnp.zeros((dk, dv), F32)
        zero = jnp.zeros((1, 1), F32)

        def rows_of(i):
            return pl.ds(pl.multiple_of(i * tc, tc), tc)

        def total(xv):
            return jnp.sum(xv, keepdims=True)

        def fwd_sweep(i, carry):
            hh, dlf, dlb = carry
            rows = rows_of(i)
            q, kk, v, dov = q_ref[rows, :], k_ref[rows, :], v_ref[rows, :], do_ref[rows, :]
            dof, vf = dov.astype(F32), v.astype(F32)
            p = _dot_nt(q, kk) * c["dmat"]
            da = _dot_nt(dov, v)
            x = p * da * c["ab"]
            dlf = dlf + total(jnp.where(fwd, x, 0.0))
            dlb = dlb - total(jnp.where(fwd, 0.0, x))
            pb, dpb = p.astype(MM), (da * c["dmat"]).astype(MM)
            dq = _dot(dpb, kk)
            dkc = _dot_tn(dpb, q)
            dvc = _dot_tn(pb, dov)
            sf, sb = sf_all[i], sb_all[i]
            sfb, sbb = sf.astype(MM), sb.astype(MM)
            q_xf, q_zb = _scaled(q, c["xi_f"]), _scaled(q, c["zeta_b"])
            dq = dq + _dot_nt(dov, sfb) * c["xi_f"] + _dot_nt(dov, sbb) * c["zeta_b"]
            dlf = dlf + total(jnp.sum(_dot(q_xf, sfb) * dof, axis=-1, keepdims=True) * c["up"])
            dlb = dlb + total(jnp.sum(_dot(q_zb, sbb) * dof, axis=-1, keepdims=True) * c["down"])
            hb = hh.astype(MM)
            dkc = dkc + _dot_nt(v, hb) * c["xi_b"]
            dv_bx = _dot(_scaled(kk, c["xi_b"]), hb)
            dlb = dlb + total(jnp.sum(vf * dv_bx, axis=-1, keepdims=True) * c["up"])
            dlb = dlb + float(tc) * total(hh * (sb * c["cb"]))
            dq_acc[rows, :] = dq
            dk_acc[rows, :] = dkc
            dv_acc[rows, :] = dvc + dv_bx
            return hh * c["cb"] + _dot_tn(q_zb, dov), dlf, dlb

        _, dlf, dlb = lax.fori_loop(0, nc, fwd_sweep, (zero_state, zero, zero))

        def rev_sweep(ii, carry):
            gg, dlf = carry
            i = nc - 1 - ii
            rows = rows_of(i)
            q, kk, v, dov = q_ref[rows, :], k_ref[rows, :], v_ref[rows, :], do_ref[rows, :]
            gb = gg.astype(MM)
            dk_acc[rows, :] += _dot_nt(v, gb) * c["zeta_f"]
            dv_fx = _dot(_scaled(kk, c["zeta_f"]), gb)
            dv_acc[rows, :] += dv_fx
            dlf = dlf + total(jnp.sum(v.astype(F32) * dv_fx, axis=-1, keepdims=True) * c["down"])
            dlf = dlf + float(tc) * total(gg * (sf_all[i] * c["cf"]))
            return gg * c["cf"] + _dot_tn(_scaled(q, c["xi_f"]), dov), dlf

        _, dlf = lax.fori_loop(0, nc, rev_sweep, (zero_state, dlf))

        cs, sn = c_ref[...], s_ref[...]
        dq_ref[...] = _rot_t(dq_acc[...], cs, sn, dk // 2).astype(MM)
        dk_ref[...] = (_rot_t(dk_acc[...], cs, sn, dk // 2) * scale).astype(MM)
        dv_ref[...] = dv_acc[...].astype(MM)
        dlf_ref[...] = jnp.broadcast_to(dlf, dlf_ref.shape)
        dlb_ref[...] = jnp.broadcast_to(dlb, dlb_ref.shape)

    qk = pl.BlockSpec((seq, dk), lambda b, h: (b, h))
    vv = pl.BlockSpec((seq, dv), lambda b, h: (b, h))
    tab = pl.BlockSpec((seq, dk), lambda b, h: (0, 0))
    dl = pl.BlockSpec((None, 8, 128), lambda b, h: (b * HEADS + h, 0, 0))
    states = pl.BlockSpec((None, nc, dk, dv), lambda b, h: (b * HEADS + h, 0, 0, 0))
    return pl.pallas_call(
        body, name="ret_core_bwd", grid=(bl, HEADS),
        in_specs=[pl.BlockSpec(memory_space=pltpu.SMEM), qk, qk, pl.BlockSpec((seq, dv), lambda b, h: (b, HEADS + h)),
                  vv, states, states, tab, tab],
        out_specs=[qk, qk, vv, dl, dl],
        out_shape=[jax.ShapeDtypeStruct((t, d // 2), MM), jax.ShapeDtypeStruct((t, d // 2), MM),
                   jax.ShapeDtypeStruct((t, d), MM),
                   jax.ShapeDtypeStruct((bl * HEADS, 8, 128), F32), jax.ShapeDtypeStruct((bl * HEADS, 8, 128), F32)],
        scratch_shapes=[pltpu.VMEM((seq, dk), F32), pltpu.VMEM((seq, dk), F32), pltpu.VMEM((seq, dv), F32)],
        compiler_params=_params(2))(lg, qr, kr, proj, do, sf_in, sb_in, cos2, sin2)


def _window_count(row, w, seq):
    return (jnp.minimum(row + w // 2, seq) - jnp.maximum(row - w // 2, 0)).astype(F32)


def _window_sum(pv, row, w, seq, sign):
    acc = None
    for j in range(-(w // 2), w // 2):
        if j == 0:
            term = pv
        else:
            src = row + sign * j
            term = jnp.where((src >= 0) & (src < seq), pltpu.roll(pv, (-sign * j) % seq, 0), 0.0)
        acc = term if acc is None else acc + term
    return acc


def _pool_fwd(proj, w_grp, scale, d, bl, seq):
    t = proj.shape[0]
    dg = d // 8

    def body(p_ref, w_ref, s_ref, y_ref):
        row = lax.broadcasted_iota(jnp.int32, (seq, dg), 0)
        for gi, w in enumerate(POOL_WINDOWS):
            sl = slice(gi * dg, (gi + 1) * dg)
            pg = p_ref[:, sl].astype(F32)
            mixed = _window_sum(pg, row, w, seq, 1) / _window_count(row, w, seq) - pg
            yp = _dot(mixed.astype(MM), w_ref[gi].astype(MM))
            y_ref[:, sl] = (yp * s_ref[:, sl]).astype(MM)

    return pl.pallas_call(
        body, name="pool_fwd", grid=(bl,),
        in_specs=[pl.BlockSpec((seq, d // 2), lambda b: (b, 6)),
                  pl.BlockSpec(w_grp.shape, lambda b: (0, 0, 0)),
                  pl.BlockSpec((1, d // 2), lambda b: (0, 0))],
        out_specs=pl.BlockSpec((seq, d // 2), lambda b: (b, 0)),
        out_shape=jax.ShapeDtypeStruct((t, d // 2), MM),
        compiler_params=_params(1))(proj, w_grp, scale)


def _pool_bwd(proj, dy, w_grp, scale, d, bl, seq):
    t = proj.shape[0]
    dg = d // 8

    def body(p_ref, dy_ref, w_ref, s_ref, dp_ref, dw_ref, ds_ref):
        @pl.when(pl.program_id(0) == 0)
        def _():
            dw_ref[...] = jnp.zeros_like(dw_ref)
            ds_ref[...] = jnp.zeros_like(ds_ref)

        row = lax.broadcasted_iota(jnp.int32, (seq, dg), 0)
        for gi, w in enumerate(POOL_WINDOWS):
            sl = slice(gi * dg, (gi + 1) * dg)
            pg = p_ref[:, sl].astype(F32)
            cnt = _window_count(row, w, seq)
            mixb = (_window_sum(pg, row, w, seq, 1) / cnt - pg).astype(MM)
            wgb = w_ref[gi].astype(MM)
            yp = _dot(mixb, wgb)
            dyg = dy_ref[:, sl]
            ds_ref[:, sl] += jnp.sum(dyg * yp, axis=0, keepdims=True)
            dyp = (dyg * s_ref[:, sl]).astype(MM)
            dmixed = _dot_nt(dyp, wgb)
            dw_ref[gi] += _dot_tn(mixb, dyp)
            dp_ref[:, sl] = (_window_sum(dmixed / cnt, row, w, seq, -1) - dmixed).astype(MM)

    half = pl.BlockSpec((seq, d // 2), lambda b: (b, 0))
    wspec = pl.BlockSpec(w_grp.shape, lambda b: (0, 0, 0))
    sspec = pl.BlockSpec((1, d // 2), lambda b: (0, 0))
    return pl.pallas_call(
        body, name="pool_bwd", grid=(bl,),
        in_specs=[pl.BlockSpec((seq, d // 2), lambda b: (b, 6)), half, wspec, sspec],
        out_specs=[half, wspec, sspec],
        out_shape=[jax.ShapeDtypeStruct((t, d // 2), MM), jax.ShapeDtypeStruct(w_grp.shape, F32),
                   jax.ShapeDtypeStruct((1, d // 2), F32)],
        compiler_params=_params(1))(proj, dy, w_grp, scale)


def _attn_probs(q, kk, dh):
    s = _dot_nt(q, kk) * (float(dh) ** -0.5)
    e = jnp.exp(s - jnp.max(s, axis=-1, keepdims=True))
    return e / jnp.sum(e, axis=-1, keepdims=True)


def _attn_fwd(proj, kv, d, bl, seq, mlen, *, tq=2048):
    t = proj.shape[0]
    dh = d // 8
    tq = min(tq, seq)
    nq = seq // tq

    def body(q_ref, k_ref, v_ref, o_ref):
        a = _attn_probs(q_ref[...].astype(MM), k_ref[...].astype(MM), dh)
        o_ref[...] = _dot(a.astype(MM), v_ref[...].astype(MM)).astype(MM)

    return pl.pallas_call(
        body, name="attn_fwd", grid=(bl, HEADS, nq),
        in_specs=[pl.BlockSpec((tq, dh), lambda b, h, i: (b * nq + i, 7 * HEADS + h)),
                  pl.BlockSpec((mlen, dh), lambda b, h, i: (b, h)),
                  pl.BlockSpec((mlen, dh), lambda b, h, i: (b, HEADS + h))],
        out_specs=pl.BlockSpec((tq, dh), lambda b, h, i: (b * nq + i, h)),
        out_shape=jax.ShapeDtypeStruct((t, d // 2), MM),
        compiler_params=_params(3))(proj, kv, kv)


def _attn_bwd(proj, kv, do, d, bl, seq, mlen, *, tq=2048):
    t = proj.shape[0]
    dh = d // 8
    tq = min(tq, seq)
    nq = seq // tq

    def body(q_ref, k_ref, v_ref, do_ref, dq_ref, dk_ref, dv_ref):
        @pl.when(pl.program_id(2) == 0)
        def _():
            dk_ref[...] = jnp.zeros_like(dk_ref)
            dv_ref[...] = jnp.zeros_like(dv_ref)

        q, kk, vv = q_ref[...].astype(MM), k_ref[...].astype(MM), v_ref[...].astype(MM)
        dov = do_ref[...].astype(MM)
        a = _attn_probs(q, kk, dh)
        dp = _dot_nt(dov, vv)
        ds = (a * (dp - jnp.sum(dp * a, axis=-1, keepdims=True)) * (float(dh) ** -0.5)).astype(MM)
        dq_ref[...] = _dot(ds, kk).astype(MM)
        dk_ref[...] += _dot_tn(ds, q)
        dv_ref[...] += _dot_tn(a.astype(MM), dov)

    qs = pl.BlockSpec((tq, dh), lambda b, h, i: (b * nq + i, h))
    ms = pl.BlockSpec((mlen, dh), lambda b, h, i: (b, h))
    return pl.pallas_call(
        body, name="attn_bwd", grid=(bl, HEADS, nq),
        in_specs=[pl.BlockSpec((tq, dh), lambda b, h, i: (b * nq + i, 7 * HEADS + h)), ms,
                  pl.BlockSpec((mlen, dh), lambda b, h, i: (b, HEADS + h)), qs],
        out_specs=[qs, ms, ms],
        out_shape=[jax.ShapeDtypeStruct((t, d // 2), MM), jax.ShapeDtypeStruct((bl * mlen, d // 2), F32),
                   jax.ShapeDtypeStruct((bl * mlen, d // 2), F32)],
        compiler_params=_params(3))(proj, kv, kv, do)


def _comm_call(name, body, arrays, out_shapes):
    n = len(arrays)
    hbm = pl.BlockSpec(memory_space=pl.ANY)
    return pl.pallas_call(
        body, name=name, out_shape=out_shapes, in_specs=[hbm] * n, out_specs=[hbm] * n,
        scratch_shapes=[pltpu.SemaphoreType.DMA((7 * n,)), pltpu.SemaphoreType.DMA((7 * n,)),
                        pltpu.SemaphoreType.DMA((n,))],
    )(*arrays)


def _all_gather(name, shards):
    n = len(shards)

    def body(*refs):
        x_refs, out_refs = refs[:n], refs[n:2 * n]
        send_sems, recv_sems, local_sems = refs[2 * n:]
        x, y, c = lax.axis_index("x"), lax.axis_index("y"), lax.axis_index("c")
        me, sibling = (x, y, c), (x, y, 1 - c)
        chips = [(1 - x, y), (x, 1 - y), (1 - x, 1 - y)]

        def copy(o, k, block, to, src=None):
            slot = out_refs[o].at[4 * block[0] + 2 * block[1] + block[2]]
            return pltpu.make_async_remote_copy(
                src_ref=slot if src is None else src, dst_ref=slot, send_sem=send_sems.at[7 * o + k],
                recv_sem=recv_sems.at[7 * o + k], device_id=to, device_id_type=MESH)

        locals_, remotes = [], []
        for o in range(n):
            mine = pltpu.make_async_copy(x_refs[o], out_refs[o].at[4 * x + 2 * y + c], local_sems.at[o])
            mine.start()
            locals_.append(mine)
            first = [copy(o, 0, me, sibling, src=x_refs[o])]
            first += [copy(o, 1 + j, me, (*chip, c), src=x_refs[o]) for j, chip in enumerate(chips)]
            for cp in first:
                cp.start()
            remotes += first
        for o in range(n):
            for j, chip in enumerate(chips):
                copy(o, 1 + j, (*chip, c), me).wait_recv()
                passed = copy(o, 4 + j, (*chip, c), sibling)
                passed.start()
                remotes.append(passed)
        for o in range(n):
            copy(o, 0, sibling, me).wait_recv()
            for j, chip in enumerate(chips):
                copy(o, 4 + j, (*chip, 1 - c), me).wait_recv()
        for cp in remotes:
            cp.wait_send()
        for mine in locals_:
            mine.wait()

    outs = [jax.ShapeDtypeStruct((N_DEV,) + s.shape, s.dtype) for s in shards]
    return _comm_call(name, body, shards, outs)


def _columns_side_by_side(name, g):
    _, k, wb = g.shape

    def body(x_ref, o_ref):
        o_ref[...] = x_ref[...]

    return pl.pallas_call(
        body, name=name, grid=(N_DEV,), in_specs=[pl.BlockSpec((None, k, wb), lambda j: (j, 0, 0))],
        out_specs=pl.BlockSpec((k, wb), lambda j: (0, j)),
        out_shape=jax.ShapeDtypeStruct((k, N_DEV * wb), g.dtype), compiler_params=_params(1))(g)


def _peer_of(k, x, y, c):
    peer = (1 - x if k & 4 else x, 1 - y if k & 2 else y, 1 - c if k & 1 else c)
    return peer, 4 * peer[0] + 2 * peer[1] + peer[2]


def _slot(land, idx, side_by_side):
    if not side_by_side:
        return land.at[idx]
    b = land.shape[1] // N_DEV
    return land.at[:, pl.ds(pl.multiple_of(idx * b, b), b)]


def _split_copies(scatter, srcs, lands, send_sems, recv_sems, arriving, mid=()):
    x, y, c = lax.axis_index("x"), lax.axis_index("y"), lax.axis_index("c")
    me_idx = 4 * x + 2 * y + c
    copies = []
    for o, (src, land) in enumerate(zip(srcs, lands)):
        for k in range(1, N_DEV):
            peer, p_idx = _peer_of(k, x, y, c)
            mine = src.at[p_idx] if scatter else src
            sems = dict(send_sem=send_sems.at[7 * o + k - 1], recv_sem=recv_sems.at[7 * o + k - 1],
                        device_id=peer, device_id_type=MESH)
            slot = _slot(land, p_idx if arriving else me_idx, o in mid)
            copies.append(pltpu.make_async_remote_copy(src_ref=mine, dst_ref=slot, **sems))
    return copies


_HBM = pl.BlockSpec(memory_space=pltpu.HBM)
_SEM = pl.BlockSpec(memory_space=pltpu.SEMAPHORE)
_EFFECT = pltpu.SideEffectType.DATAFLOW_SIDE_EFFECTING


def _own_slot_copies(scatter, srcs, lands, local_sems, mid=()):
    me_idx = 4 * lax.axis_index("x") + 2 * lax.axis_index("y") + lax.axis_index("c")
    return [pltpu.make_async_copy(src.at[me_idx] if scatter else src, _slot(land, me_idx, o in mid), local_sems.at[o])
            for o, (src, land) in enumerate(zip(srcs, lands))]


def _exchange_start(name, scatter, arrays, after=(), mid=()):
    n = len(arrays)
    lands = [lax.empty(a.shape if scatter else
                       ((a.shape[0], N_DEV * a.shape[1]) if o in mid else (N_DEV,) + a.shape), a.dtype)
             for o, a in enumerate(arrays)]

    def body(*refs):
        srcs, lnds = refs[:n], refs[n:2 * n]
        send_sems, recv_sems, local_sems = refs[2 * n + len(after):2 * n + len(after) + 3]
        token = refs[-1]
        for cp in _split_copies(scatter, srcs, lnds, send_sems, recv_sems, False, mid):
            cp.start()
        for cp in _own_slot_copies(scatter, srcs, lnds, local_sems, mid):
            cp.start()
        token[...] = jnp.zeros_like(token)

    hbm_in = [pltpu.with_memory_space_constraint(a, pltpu.HBM) for a in list(arrays) + lands]
    res = pl.pallas_call(
        body, name=name,
        out_shape=(pltpu.SemaphoreType.DMA((7 * n,)), pltpu.SemaphoreType.DMA((7 * n,)), pltpu.SemaphoreType.DMA((n,)),
                   *[pltpu.HBM(a.shape, a.dtype) for a in hbm_in], jax.ShapeDtypeStruct((8, 128), F32)),
        in_specs=[_HBM] * (2 * n) + [pl.BlockSpec(memory_space=pl.ANY)] * len(after),
        out_specs=(_SEM, _SEM, _SEM, *[_HBM] * (2 * n), pl.BlockSpec(memory_space=pltpu.VMEM)),
        input_output_aliases={i: 3 + i for i in range(2 * n)},
        compiler_params=pltpu.CompilerParams(has_side_effects=_EFFECT),
    )(*hbm_in, *after)
    return res[:3], tuple(mid), list(res[3:3 + n]), list(res[3 + n:3 + 2 * n]), res[-1]


def _exchange_wait(name, scatter, started, after):
    sems, mid, srcs, lands, _ = started
    n = len(srcs)

    def body(*refs):
        src_refs, lnd_refs = refs[:n], refs[n:2 * n]
        send_sems, recv_sems, local_sems = refs[2 * n:2 * n + 3]
        for cp in _split_copies(scatter, src_refs, lnd_refs, send_sems, recv_sems, False, mid):
            cp.wait_send()
        for cp in _split_copies(scatter, src_refs, lnd_refs, send_sems, recv_sems, True, mid):
            cp.wait_recv()
        for cp in _own_slot_copies(scatter, src_refs, lnd_refs, local_sems, mid):
            cp.wait()

    res = pl.pallas_call(
        body, name=name, out_shape=tuple(pltpu.HBM(a.shape, a.dtype) for a in srcs + lands),
        in_specs=[_HBM] * (2 * n) + [_SEM, _SEM, _SEM, pl.BlockSpec(memory_space=pl.ANY)],
        out_specs=tuple([_HBM] * (2 * n)), input_output_aliases={i: i for i in range(2 * n)},
        compiler_params=pltpu.CompilerParams(has_side_effects=_EFFECT),
    )(*srcs, *lands, *sems, after)
    return list(res[n:])


def _adamw(name, parts, w, m, v, prev, layer, *, tr=256):
    _, a, b = w.shape
    tr = _tile(a, tr, 8)
    c1 = 1.0 - ADAM_B1 ** ADAM_STEP
    c2 = 1.0 - ADAM_B2 ** ADAM_STEP

    def body(p_ref, w_ref, m_ref, v_ref, _g, _d, _m, _v, g_out, d_out, m_out, v_out):
        g = p_ref[0].astype(F32)
        for s in range(1, N_DEV):
            g = g + p_ref[s].astype(F32)
        mn = ADAM_B1 * m_ref[...] + (1.0 - ADAM_B1) * g
        vn = ADAM_B2 * v_ref[...] + (1.0 - ADAM_B2) * (g * g)
        g_out[...] = g
        m_out[...] = mn
        v_out[...] = vn
        d_out[...] = -ADAM_LR * ((mn / c1) / (jnp.sqrt(vn / c2) + ADAM_EPS) + ADAM_WD * w_ref[...])

    slab = pl.BlockSpec((None, tr, b), lambda i: (layer, i, 0))
    whole = pl.BlockSpec(memory_space=pl.ANY)
    return pl.pallas_call(
        body, name=name, grid=(a // tr,),
        in_specs=[pl.BlockSpec((N_DEV, tr, b), lambda i: (0, i, 0)), slab, slab, slab] + [whole] * 4,
        out_specs=[slab] * 4, out_shape=[jax.ShapeDtypeStruct(w.shape, F32)] * 4,
        input_output_aliases={4: 0, 5: 1, 6: 2, 7: 3},
        compiler_params=_params(1))(parts, w, m, v, *prev)


_COL = ("w_in", "w_pool_o", "w_mem_o", "w_ff1")
_DW_SHARDED = ("w_in", "w_ff1")
_BIG =("w_in", "w_ret_o", "w_pool_o", "w_mem_kv", "w_mem_o", "w_out", "w_ff1", "w_ff2")
_SMALL = ("ret_decay_logit", "w_pool_grp", "pool_scale", "norm1_g", "norm2_g", "mem_norm_g", "final_norm_g")
_SMALL_MM = ("w_pool_grp",)
_SMALL_F32 = tuple(n for n in _SMALL if n not in _SMALL_MM)
_WEIGHTS = ("w_in", "ret_decay_logit", "w_ret_o", "w_pool_grp", "pool_scale", "w_pool_o", "w_mem_kv", "w_mem_o",
            "w_out", "w_ff1", "w_ff2", "norm1_g", "norm2_g", "mem_norm_g", "final_norm_g")


def _small_rows(size, d):
    return -(-size // (8 * d)) * 8


def _pack_small(ws, d, names):
    parts = []
    for n in names:
        flat = ws[n].reshape(-1)
        rows = _small_rows(flat.shape[0], d)
        parts.append(jnp.pad(flat, (0, rows * d - flat.shape[0])).reshape(rows, d))
    return jnp.concatenate(parts, axis=0)[None]


def _unpack_small(packed, like, d, names):
    out, off = {}, 0
    for n in names:
        rows = _small_rows(like[n].size, d)
        out[n] = packed[0, off:off + rows].reshape(-1)[:like[n].size].reshape(like[n].shape)
        off += rows
    return out


def kernel(x, mem, w_in, ret_decay_logit, w_ret_o, w_pool_grp, pool_scale, w_pool_o, w_mem_kv, w_mem_o, w_out, w_ff1, w_ff2, norm1_g, norm2_g, mem_norm_g, final_norm_g, loss_target, m_w_in, m_ret_decay_logit, m_w_ret_o, m_w_pool_grp, m_pool_scale, m_w_pool_o, m_w_mem_kv, m_w_mem_o, m_w_out, m_w_ff1, m_w_ff2, m_norm1_g, m_norm2_g, m_mem_norm_g, m_final_norm_g, v_w_in, v_ret_decay_logit, v_w_ret_o, v_w_pool_grp, v_pool_scale, v_w_pool_o, v_w_mem_kv, v_w_mem_o, v_w_out, v_w_ff1, v_w_ff2, v_norm1_g, v_norm2_g, v_mem_norm_g, v_final_norm_g):
    w = dict(w_in=w_in, ret_decay_logit=ret_decay_logit, w_ret_o=w_ret_o, w_pool_grp=w_pool_grp,
             pool_scale=pool_scale, w_pool_o=w_pool_o, w_mem_kv=w_mem_kv, w_mem_o=w_mem_o, w_out=w_out,
             w_ff1=w_ff1, w_ff2=w_ff2, norm1_g=norm1_g, norm2_g=norm2_g, mem_norm_g=mem_norm_g,
             final_norm_g=final_norm_g)
    mom = dict(w_in=m_w_in, ret_decay_logit=m_ret_decay_logit, w_ret_o=m_w_ret_o, w_pool_grp=m_w_pool_grp,
               pool_scale=m_pool_scale, w_pool_o=m_w_pool_o, w_mem_kv=m_w_mem_kv, w_mem_o=m_w_mem_o,
               w_out=m_w_out, w_ff1=m_w_ff1, w_ff2=m_w_ff2, norm1_g=m_norm1_g, norm2_g=m_norm2_g,
               mem_norm_g=m_mem_norm_g, final_norm_g=m_final_norm_g)
    vel = dict(w_in=v_w_in, ret_decay_logit=v_ret_decay_logit, w_ret_o=v_w_ret_o, w_pool_grp=v_w_pool_grp,
               pool_scale=v_pool_scale, w_pool_o=v_w_pool_o, w_mem_kv=v_w_mem_kv, w_mem_o=v_w_mem_o,
               w_out=v_w_out, w_ff1=v_w_ff1, w_ff2=v_w_ff2, norm1_g=v_norm1_g, norm2_g=v_norm2_g,
               mem_norm_g=v_mem_norm_g, final_norm_g=v_final_norm_g)

    bl, seq, d = x.shape
    mlen = mem.shape[1]
    depth = w_in.shape[0]
    t = bl * seq
    dk = d // 8

    def natural(n, g, side_by_side=False):
        if side_by_side:
            return g
        if n in _DW_SHARDED:
            return _columns_side_by_side("relayout_" + n, g)
        if n in _COL:
            return jnp.transpose(g, (1, 0, 2)).reshape(g.shape[1], -1)
        return g.reshape(-1, g.shape[-1])

    def finish_gather(name, names, started, after):
        got = _exchange_wait(name, False, started, after)
        return {n: natural(n, g, o in started[1]) for o, (n, g) in enumerate(zip(names, got))}

    shards = [{n: w[n][l].astype(MM) for n in _BIG} for l in range(depth)]
    rest = _BIG[1:]
    (w_in0,) = _all_gather("gather_w_in", [shards[0][_BIG[0]]])
    full = [{_BIG[0]: natural(_BIG[0], w_in0)}]
    def start_layer(l, after):
        s_in = _exchange_start(f"gather_start_in{l}", False, [shards[l][_BIG[0]]], after=after, mid=(0,))
        s_rest = _exchange_start(f"gather_start_{l}", False, [shards[l][n] for n in rest], after=[s_in[4]],
                                 mid=rest_mid)
        return s_in, s_rest, (s_in[4], s_rest[4])

    rest_mid = tuple(o for o, n in enumerate(rest) if n in _COL)
    pending = (None, _exchange_start("gather_start_0", False, [shards[0][n] for n in rest], after=[w_in0],
                                     mid=rest_mid))
    first_tokens = (pending[1][4],)
    pending_next = None
    if depth > 1:
        pending_next = start_layer(1, [pending[1][4]])
        first_tokens += pending_next[2]

    inv = ROPE_BASE ** (-jnp.arange(0, dk, 2, dtype=F32) / dk)
    ang = jnp.arange(seq, dtype=F32)[:, None] * inv[None, :]
    cos2 = jnp.concatenate([jnp.cos(ang), jnp.cos(ang)], axis=-1)
    sin2 = jnp.concatenate([-jnp.sin(ang), jnp.sin(ang)], axis=-1)
    log_g = jax.nn.log_sigmoid(ret_decay_logit)
    x2 = x.reshape(t, d)
    mem2 = mem.reshape(bl * mlen, d)
    gmem = mem_norm_g.reshape(1, d)

    def merge(a_r, y_p, o_a, g_r, g_p, g_m, w_r, w_p, w_m):
        f = lambda z: z.astype(F32)
        o_r, o_p, o_m = _dot(a_r, w_r), _dot(y_p, w_p), _dot(o_a, w_m)
        return _sigmoid(f(g_r)) * o_r + _sigmoid(f(g_p)) * o_p + _sigmoid(f(g_m)) * o_m, o_r, o_p, o_m

    def relu2(u):
        r = jnp.maximum(u.astype(MM), 0.0)
        return r * r

    def ident(a):
        return a

    saved = []
    xc = x2
    for l in range(depth):
        s = dict(x_in=xc)
        started_now = ()
        if l > 0:
            pending, pending_next = pending_next, None
            full.append(finish_gather(f"gather_wait_in{l}", _BIG[:1], pending[0], xc))
            if l + 1 < depth:
                pending_next = start_layer(l + 1, [full[l]["w_in"]])
                started_now = pending_next[2]
        fw = full[l]
        g1 = norm1_g[l].reshape(1, d)
        g2 = norm2_g[l].reshape(1, d)
        s["proj"], s["h1"] = _pmm("proj", _rms_prologue, [(xc, d, 0)], [g1], fw["w_in"],
                                  tm=2048, tn=1024, save_a=True, out_dtypes=(MM,),
                                  after=started_now if l > 0 else first_tokens)
        proj = s["proj"]
        s["o_raw"], s["a_ret"], s["sf"], s["sb"], s["qr"], s["kr"] = _ret_core_fwd(proj, cos2, sin2, log_g[l],
                                                                                   d, bl, seq)
        s["y"] = _pool_fwd(proj, w_pool_grp[l], pool_scale[l].reshape(1, -1), d, bl, seq)
        fw.update(finish_gather(f"gather_wait_{l}", rest, pending[1], s["a_ret"]))
        s["kv"], s["memn"] = _pmm("mem_kv", _rms_prologue, [(mem2, d, 0)], [gmem], fw["w_mem_kv"],
                                  tm=512, tn=512, save_a=True)
        s["o_att"] = _attn_fwd(proj, s["kv"], d, bl, seq, mlen)
        s["x_mid"], s["merged"], s["o_ret"], s["o_pool"], s["o_mem"] = _pmm(
            "merge_out", merge,
            [(s["a_ret"], d, 0), (s["y"], d // 2, 0), (s["o_att"], d // 2, 0), (proj, d, 4), (proj, d, 5), (proj, d, 6)],
            [fw["w_ret_o"], fw["w_pool_o"], fw["w_mem_o"]], fw["w_out"], tm=512, tn=1024, residual=xc, save_a=True,
            extra_outs=[(d, MM)] * 3)
        s["u"], s["h2"] = _pmm("ff1", _rms_prologue, [(s["x_mid"], d, 0)], [g2], fw["w_ff1"],
                               tm=512, tn=4096, save_a=True, out_dtypes=(MM,))
        (xc,) = _pmm("ff2", relu2, [(s["u"], s["u"].shape[1], 0)], [], fw["w_ff2"],
                     tm=512, tn=1024, residual=s["x_mid"])
        saved.append(s)

    dxc, g_final, loss_part = _loss_head(xc, loss_target.reshape(t, d), final_norm_g.reshape(1, d))
    loss = lax.psum(loss_part[0, 0], ("x", "y", "c"))

    small_names = ("w_pool_grp", "pool_scale", "norm1_g", "norm2_g", "ret_decay_logit")
    grads = {n: [None] * depth for n in small_names}
    group_a = ("w_ff1", "w_ff2")
    group_b = tuple(n for n in _BIG if n not in group_a)
    scatters = {}
    dmemn = jnp.zeros((bl * mlen, d), F32)

    def relu2_bwd(acc, u):
        return (acc * (2.0 * jnp.maximum(u.astype(F32), 0.0)),)

    def gates_bwd(acc, g_r, g_p, g_m, o_r, o_p, o_m, w_r, w_p, w_m):
        d_os, d_gs, backs = [], [], []
        for gz, oz, wz in ((g_r, o_r, w_r), (g_p, o_p, w_p), (g_m, o_m, w_m)):
            sg = _sigmoid(gz.astype(F32))
            d_o = (acc * sg).astype(MM)
            d_os.append(d_o)
            d_gs.append(acc * oz.astype(F32) * (sg * (1.0 - sg)))
            backs.append(_dot_nt(d_o, wz))
        return tuple(d_os + d_gs + backs)

    def to_send(n, g):
        a, b = w[n].shape[1:]
        if n in _DW_SHARDED:
            return g
        if n in _COL:
            return jnp.transpose(g.reshape(a, N_DEV, b), (1, 0, 2))
        return g.reshape(N_DEV, a, b)

    for l in reversed(range(depth)):
        s = saved[l]
        fw = full[l]
        proj = s["proj"]
        g1 = norm1_g[l].reshape(1, d)
        g2 = norm2_g[l].reshape(1, d)
        dw = {}
        (du,) = _pmm("ff2_bwd", ident, [(dxc, d, 0)], [], fw["w_ff2"], w_mode="nt", tm=512, tn=4096,
                     epilogue=relu2_bwd, epi_ins=[(s["u"], 0)], out_dtypes=(MM,))
        dw["w_ff2"] = _tnmm("dw_ff2", s["u"], dxc, a_fn=relu2)
        dw["w_ff1"] = _tnmm("dw_ff1", s["h2"], du, col_shards=True)
        scatters[l, "a"] = _exchange_start(f"scatter_start_a{l}", True, [to_send(n, dw[n]) for n in group_a])
        dmid, grads["norm2_g"][l] = _mm_rms_bwd("ff1_norm2_bwd", du, fw["w_ff1"], s["x_mid"], g2, dxc, tm=512)
        d_oret, d_opool, d_omem, dgr, dgp, dgm, da_ret, dy, do_att = _pmm(
            "out_bwd", ident, [(dmid, d, 0)], [], fw["w_out"], w_mode="nt", tm=256, tn=d, epilogue=gates_bwd,
            epi_ins=[(proj, 4 * d), (proj, 5 * d), (proj, 6 * d), (s["o_ret"], 0), (s["o_pool"], 0), (s["o_mem"], 0)],
            epi_full=[fw["w_ret_o"], fw["w_pool_o"], fw["w_mem_o"]], out_dtypes=(MM,) * 7 + (F32,) * 2,
            out_widths=[d] * 7 + [d // 2] * 2, after=(scatters[l, "a"][4],))
        dw["w_out"] = _tnmm("dw_out", s["merged"], dmid)
        dw["w_ret_o"] = _tnmm("dw_ret_o", s["a_ret"], d_oret)
        dw["w_pool_o"] = _tnmm("dw_pool_o", s["y"], d_opool)
        dw["w_mem_o"] = _tnmm("dw_mem_o", s["o_att"], d_omem)
        dg_ret, do_ret = _ret_post_bwd(da_ret, proj, s["o_raw"], d)
        dq, dkk, dvv, dlf, dlb = _ret_core_bwd(s["qr"], s["kr"], proj, do_ret, s["sf"], s["sb"], cos2, sin2,
                                               log_g[l], d, bl, seq)
        dl = jnp.stack([dlf[:, 0, 0].reshape(bl, HEADS).sum(0), dlb[:, 0, 0].reshape(bl, HEADS).sum(0)])
        grads["ret_decay_logit"][l] = dl * jax.nn.sigmoid(-ret_decay_logit[l])
        dp, grads["w_pool_grp"][l], dscale = _pool_bwd(proj, dy, w_pool_grp[l], pool_scale[l].reshape(1, -1),
                                                       d, bl, seq)
        grads["pool_scale"][l] = dscale.reshape(-1)
        early = ()
        if l == 0:
            pool_g = dict(w_pool_grp=jnp.stack(grads["w_pool_grp"]))
            small_mm_started = _exchange_start("gather_small_mm_start", False,
                                               [_pack_small(pool_g, d, _SMALL_MM)[0].astype(MM)])
            early = (small_mm_started[4],)
        dqm, dmk, dmv = _attn_bwd(proj, s["kv"], do_att, d, bl, seq, mlen)
        dkv = jnp.concatenate([dmk, dmv], axis=-1).astype(MM)
        dw["w_mem_kv"] = _tnmm("dw_mem_kv", s["memn"], dkv)
        (dmemn,) = _pmm("mem_kv_bwd", None, [(dkv, d, 0)], [], fw["w_mem_kv"], w_mode="nt", tm=512, tn=512,
                        residual=dmemn, after=early)
        dproj = [dq, dkk, dvv, dg_ret, dp, dqm, dgr, dgp, dgm]
        dw["w_in"] = _tnmm("dw_in", s["h1"], dproj, col_shards=True, tm=512, tk=512)
        scatters[l, "b"] = _exchange_start(f"scatter_start_b{l}", True, [to_send(n, dw[n]) for n in group_b])
        dxc, grads["norm1_g"][l] = _mm_rms_bwd("proj_norm1_bwd", dproj, fw["w_in"], s["x_in"], g1, dmid, tm=256,
                                               after=(scatters[l, "b"][4],))

    _, g_memn = _rms_bwd("mem_norm_bwd", dmemn, mem2, gmem, None)
    grad_x = dxc.reshape(bl, seq, d)

    small_g = dict(ret_decay_logit=jnp.stack(grads["ret_decay_logit"]), pool_scale=jnp.stack(grads["pool_scale"]),
                   norm1_g=jnp.concatenate(grads["norm1_g"], axis=0), norm2_g=jnp.concatenate(grads["norm2_g"], axis=0),
                   mem_norm_g=g_memn.reshape(-1), final_norm_g=g_final.reshape(-1))
    small_started = _exchange_start("gather_small_start", False, [_pack_small(small_g, d, _SMALL_F32)[0]])

    big = {n: [lax.empty(w[n].shape, F32) for _ in range(4)] for n in _BIG}

    def update(l, grp, names, after):
        recv = _exchange_wait(f"scatter_wait_{grp}{l}", True, scatters[l, grp], after)
        for n, parts in zip(names, recv):
            big[n] = _adamw("adamw_" + n, parts, w[n], mom[n], vel[n], big[n], l)
        return big[names[-1]][0]

    after = dxc
    for l in reversed(range(1, depth)):
        for grp, names in (("a", group_a), ("b", group_b)):
            after = update(l, grp, names, after)
    after = update(0, "a", group_a, after)
    small_lands = (_exchange_wait("gather_small_mm_wait", False, small_mm_started, after)
                   + _exchange_wait("gather_small_wait", False, small_started, after))
    small = [{} for _ in range(4)]
    for names, parts in zip((_SMALL_MM, _SMALL_F32), small_lands):
        w_small = _pack_small(w, d, names)
        res = _adamw("adamw_small", parts, w_small, _pack_small(mom, d, names), _pack_small(vel, d, names),
                     [lax.empty(w_small.shape, F32) for _ in range(4)], 0)
        after = res[0]
        for k in range(4):
            small[k].update(_unpack_small(res[k], w, d, names))
    update(0, "b", group_b, after)

    outs = [loss, grad_x]
    for k in range(4):
        outs += [big[n][k] if n in _BIG else small[k][n] for n in _WEIGHTS]
    return tuple(outs)
```
